```python
import math
import jax, jax.numpy as jnp
from jax import lax
import numpy as np

D_MODEL = 1024
BATCH = 4
SEQ = 4096
DEPTH = 4
DEC_BATCH = 128
DEC_SEQ = 4
PAST_LEN = 8192
PAGE_SIZE = 128

N_MIXERS = 3
N_A_LAYERS = (DEPTH + 2) // 3
N_B_LAYERS = (DEPTH + 1) // 3
N_C_LAYERS = DEPTH // 3
D_FF = 4 * D_MODEL
EPS = 1e-6
NEG_INF = -1e30

A_CHUNK = 128
A_D_FFN = 6 * D_MODEL
A_HALF = A_D_FFN // 2
A_GROUPS = 8
A_GROUP_W = A_HALF // A_GROUPS

B_HEADS = 16
B_KV_HEADS = 4
B_HEAD_DIM = 64
B_WINDOW = 128
B_BLOCK = 128
N_BUCKETS = 32
MAX_DISTANCE = 128

C_D_INNER = 2 * D_MODEL
C_HEAD_DIM = 64
C_HEADS = C_D_INNER // C_HEAD_DIM
C_GROUPS = 4
C_D_STATE = 128
C_D_CONV = 4
C_CONV_DIM = C_D_INNER + 2 * C_GROUPS * C_D_STATE
C_IN_DIM = C_D_INNER + C_CONV_DIM + C_HEADS
C_CHUNK = 128

kernel_name = "hybrid_gmlp_swa_ssd_step"


def rmsnorm(x, g):
    xf = x.astype(jnp.float32)
    y = xf * lax.rsqrt(jnp.mean(xf * xf, axis=-1, keepdims=True) + EPS)
    return (y * g.astype(jnp.float32)).astype(x.dtype)


def sq_relu_mlp(x, w_up, w_down):
    h = jax.nn.relu(x @ w_up)
    return (h * h) @ w_down


def chunk_mlp_mixer(x, w_in, norm_v, w_sp, b_sp, w_out, chunk):
    b, L, _ = x.shape
    nc = L // chunk
    uv = jax.nn.gelu(x @ w_in, approximate=False)
    u, v = jnp.split(uv, 2, axis=-1)
    v = rmsnorm(v, norm_v)
    vg = v.reshape(b, nc, chunk, A_GROUPS, A_GROUP_W)
    causal = jnp.tril(jnp.ones((chunk, chunk), dtype=bool))
    ws = jnp.where(causal[None], w_sp[:, :chunk, :chunk], 0).astype(v.dtype)
    bias = jnp.swapaxes(b_sp[:, :chunk], 0, 1)[None, None, :, :, None]
    s = jnp.einsum('gij,bcjgw->bcigw', ws, vg) + bias.astype(v.dtype)
    out = (u * s.reshape(b, L, A_HALF)) @ w_out
    return out, v


def t5_bucket(rel):
    max_exact = N_BUCKETS // 2
    n = jnp.maximum(rel, 0)
    nf = jnp.maximum(n, 1).astype(jnp.float32)
    large = max_exact + (jnp.log(nf / max_exact) / math.log(MAX_DISTANCE / max_exact)
                         * (N_BUCKETS - max_exact)).astype(jnp.int32)
    large = jnp.minimum(large, N_BUCKETS - 1)
    return jnp.where(n < max_exact, n, large)


def banded_attention(q, k, v, q_pos, k_pos, sinks, rel_bias):
    b, nb, Q, H, Dh = q.shape
    K = k.shape[2]
    R = H // B_KV_HEADS
    qg = q.reshape(b, nb, Q, B_KV_HEADS, R, Dh)
    logits = jnp.einsum('bnqgrd,bnkgd->bngrqk', qg, k).astype(jnp.float32) * (Dh ** -0.5)
    rel = q_pos[:, :, None] - k_pos[:, None, :]
    valid = (rel >= 0) & (rel < B_WINDOW) & (k_pos[:, None, :] >= 0)
    bias = rel_bias.astype(jnp.float32)[t5_bucket(rel)]
    bias = bias.reshape(nb, Q, K, B_KV_HEADS, R).transpose(0, 3, 4, 1, 2)
    logits = jnp.where(valid[:, None, None], logits + bias[None], NEG_INF)
    sink = jnp.broadcast_to(sinks.astype(jnp.float32).reshape(B_KV_HEADS, R)[None, None, :, :, None, None],
                            (b, nb, B_KV_HEADS, R, Q, 1))
    probs = jax.nn.softmax(jnp.concatenate([logits, sink], axis=-1), axis=-1)[..., :-1]
    out = jnp.einsum('bngrqk,bnkgd->bnqgrd', probs.astype(v.dtype), v)
    return out.reshape(b, nb * Q, H * Dh)


def swa_project(x, w_qkv, q_norm, k_norm):
    b, L, _ = x.shape
    qkv = x @ w_qkv
    q, k, v = jnp.split(qkv, [B_HEADS * B_HEAD_DIM, (B_HEADS + B_KV_HEADS) * B_HEAD_DIM], axis=-1)
    q = rmsnorm(q.reshape(b, L, B_HEADS, B_HEAD_DIM), q_norm)
    k = rmsnorm(k.reshape(b, L, B_KV_HEADS, B_HEAD_DIM), k_norm)
    v = v.reshape(b, L, B_KV_HEADS, B_HEAD_DIM)
    return q, k, v


def swa_prompt(x, w_qkv, q_norm, k_norm, sinks, rel_bias, w_out):
    b, L, _ = x.shape
    q, k, v = swa_project(x, w_qkv, q_norm, k_norm)
    nb = L // B_BLOCK
    qb = q.reshape(b, nb, B_BLOCK, B_HEADS, B_HEAD_DIM)
    kb = k.reshape(b, nb, B_BLOCK, B_KV_HEADS, B_HEAD_DIM)
    vb = v.reshape(b, nb, B_BLOCK, B_KV_HEADS, B_HEAD_DIM)
    kband = jnp.concatenate([jnp.concatenate([jnp.zeros_like(kb[:, :1]), kb[:, :-1]], axis=1), kb], axis=2)
    vband = jnp.concatenate([jnp.concatenate([jnp.zeros_like(vb[:, :1]), vb[:, :-1]], axis=1), vb], axis=2)
    q_pos = jnp.arange(L, dtype=jnp.int32).reshape(nb, B_BLOCK)
    k_pos = (jnp.arange(nb, dtype=jnp.int32)[:, None] - 1) * B_BLOCK + jnp.arange(2 * B_BLOCK, dtype=jnp.int32)[None, :]
    o = banded_attention(qb, kband, vband, q_pos, k_pos, sinks, rel_bias)
    keep = min(B_WINDOW, L)
    return o @ w_out, k[:, L - keep:], v[:, L - keep:]


def swa_sample(x, k_cache, v_cache, w_qkv, q_norm, k_norm, sinks, rel_bias, w_out):
    b, L, _ = x.shape
    q, k, v = swa_project(x, w_qkv, q_norm, k_norm)
    W = k_cache.shape[1]
    kk = jnp.concatenate([k_cache.astype(k.dtype), k], axis=1)
    vv = jnp.concatenate([v_cache.astype(v.dtype), v], axis=1)
    q_pos = (PAST_LEN + jnp.arange(L, dtype=jnp.int32))[None]
    k_pos = (PAST_LEN - W + jnp.arange(W + L, dtype=jnp.int32))[None]
    o = banded_attention(q[:, None], kk[:, None], vv[:, None], q_pos, k_pos, sinks, rel_bias)
    return o @ w_out, kk[:, L:], vv[:, L:]


def causal_conv(xbc, buf, w, bias):
    L = xbc.shape[1]
    xp = jnp.concatenate([buf.astype(xbc.dtype), xbc], axis=1)
    y = bias
    for tap in range(C_D_CONV):
        y = y + xp[:, tap:tap + L] * w[tap]
    return jax.nn.silu(y), xp[:, L:]


def ssd_scan(x, dt, A, Bm, Cm, h0, chunk):
    b, L, H, P = x.shape
    G, N = Bm.shape[2], Bm.shape[3]
    R = H // G
    nc = L // chunk
    f32 = jnp.float32
    xf = x.astype(f32).reshape(b, nc, chunk, G, R, P)
    dtc = dt.reshape(b, nc, chunk, G, R)
    Bc = Bm.astype(f32).reshape(b, nc, chunk, G, N)
    Cc = Cm.astype(f32).reshape(b, nc, chunk, G, N)
    acs = jnp.cumsum(dtc * A.reshape(G, R), axis=2)
    diff = acs[:, :, :, None] - acs[:, :, None]
    causal = jnp.tril(jnp.ones((chunk, chunk), dtype=bool))[:, :, None, None]
    decay = jnp.where(causal, jnp.exp(jnp.where(causal, diff, 0.0)), 0.0)
    cb = jnp.einsum('bcign,bcjgn->bcijg', Cc, Bc)
    w = cb[..., None] * decay * dtc[:, :, None]
    y_intra = jnp.einsum('bcijgr,bcjgrp->bcigrp', w, xf)
    decay_end = jnp.exp(acs[:, :, -1:] - acs)
    states = jnp.einsum('bcjgn,bcjgr,bcjgrp->bcgrpn', Bc, decay_end * dtc, xf)
    chunk_decay = jnp.exp(acs[:, :, -1])

    def step(h, inp):
        st, dec = inp
        return h * dec[..., None, None] + st, h

    h0g = h0.astype(f32).reshape(b, G, R, P, N)
    h_final, h_prev = lax.scan(step, h0g, (jnp.moveaxis(states, 1, 0), jnp.moveaxis(chunk_decay, 1, 0)))
    h_prev = jnp.moveaxis(h_prev, 0, 1)
    y_inter = jnp.einsum('bcign,bcgrpn,bcigr->bcigrp', Cc, h_prev, jnp.exp(acs))
    y = (y_intra + y_inter).reshape(b, L, H, P)
    return y, h_final.reshape(b, H, P, N)


def ssd_mixer(x, h0, conv_buf, w_in, conv_w, conv_b, dt_bias, a_log, d_skip, norm_w, w_out, chunk):
    b, L, _ = x.shape
    zxbcdt = x @ w_in
    z, xbc, dt = jnp.split(zxbcdt, [C_D_INNER, C_D_INNER + C_CONV_DIM], axis=-1)
    xbc_act, new_buf = causal_conv(xbc, conv_buf, conv_w, conv_b)
    xs, Bm, Cm = jnp.split(xbc_act, [C_D_INNER, C_D_INNER + C_GROUPS * C_D_STATE], axis=-1)
    xs = xs.reshape(b, L, C_HEADS, C_HEAD_DIM)
    Bm = Bm.reshape(b, L, C_GROUPS, C_D_STATE)
    Cm = Cm.reshape(b, L, C_GROUPS, C_D_STATE)
    dt = jax.nn.softplus(dt.astype(jnp.float32) + dt_bias.astype(jnp.float32))
    A = -jnp.exp(a_log.astype(jnp.float32))
    y, h = ssd_scan(xs, dt, A, Bm, Cm, h0, chunk)
    y = y + d_skip.astype(jnp.float32)[:, None] * xs.astype(jnp.float32)
    g = y.reshape(b, L, C_D_INNER) * jax.nn.silu(z.astype(jnp.float32))
    gg = g.reshape(b, L, C_GROUPS, C_D_INNER // C_GROUPS)
    gg = gg * lax.rsqrt(jnp.mean(gg * gg, axis=-1, keepdims=True) + EPS)
    y = (gg.reshape(b, L, C_D_INNER) * norm_w.astype(jnp.float32)).astype(x.dtype)
    return y @ w_out, h.astype(x.dtype), new_buf


def setup_inputs(seed: int = 0) -> dict:
    key = jax.random.key(seed)
    ks = iter(jax.random.split(key, 40))
    nrm = lambda shape, scale: jax.random.normal(next(ks), shape, jnp.float32) * scale
    win = min(B_WINDOW, PAST_LEN)
    dt0 = jnp.exp(jax.random.uniform(next(ks), (N_C_LAYERS, C_HEADS), jnp.float32, math.log(1e-3), math.log(1e-1)))
    return {
        "x_prompt": nrm((BATCH, SEQ, D_MODEL), 1.0),
        "x_sample": nrm((DEC_BATCH, DEC_SEQ, D_MODEL), 1.0),
        "cache_swa_k": nrm((N_B_LAYERS, DEC_BATCH, win, B_KV_HEADS, B_HEAD_DIM), 1.0),
        "cache_swa_v": nrm((N_B_LAYERS, DEC_BATCH, win, B_KV_HEADS, B_HEAD_DIM), 1.0),
        "state_ssm": nrm((N_C_LAYERS, DEC_BATCH, C_HEADS, C_HEAD_DIM, C_D_STATE), 0.1),
        "state_conv": nrm((N_C_LAYERS, DEC_BATCH, C_D_CONV - 1, C_CONV_DIM), 1.0),
        "norm_mixer": 1.0 + nrm((DEPTH, D_MODEL), 0.02),
        "norm_mlp": 1.0 + nrm((DEPTH, D_MODEL), 0.02),
        "mlp_w_up": nrm((DEPTH, D_MODEL, D_FF), D_MODEL ** -0.5),
        "mlp_w_down": nrm((DEPTH, D_FF, D_MODEL), D_FF ** -0.5),
        "a_w_in": nrm((N_A_LAYERS, D_MODEL, A_D_FFN), D_MODEL ** -0.5),
        "a_norm_v": 1.0 + nrm((N_A_LAYERS, A_HALF), 0.02),
        "a_w_spatial": nrm((N_A_LAYERS, A_GROUPS, A_CHUNK, A_CHUNK), 0.5 * A_CHUNK ** -0.5),
        "a_b_spatial": 1.0 + nrm((N_A_LAYERS, A_GROUPS, A_CHUNK), 0.1),
        "a_w_out": nrm((N_A_LAYERS, A_HALF, D_MODEL), A_HALF ** -0.5),
        "b_w_qkv": nrm((N_B_LAYERS, D_MODEL, (B_HEADS + 2 * B_KV_HEADS) * B_HEAD_DIM), D_MODEL ** -0.5),
        "b_q_norm": 1.0 + nrm((N_B_LAYERS, B_HEAD_DIM), 0.02),
        "b_k_norm": 1.0 + nrm((N_B_LAYERS, B_HEAD_DIM), 0.02),
        "b_sinks": nrm((N_B_LAYERS, B_HEADS), 0.5),
        "rel_bias": nrm((N_BUCKETS, B_HEADS), 0.5),
        "b_w_out": nrm((N_B_LAYERS, B_HEADS * B_HEAD_DIM, D_MODEL), (B_HEADS * B_HEAD_DIM) ** -0.5),
        "c_w_in": nrm((N_C_LAYERS, D_MODEL, C_IN_DIM), D_MODEL ** -0.5),
        "c_conv_w": nrm((N_C_LAYERS, C_D_CONV, C_CONV_DIM), C_D_CONV ** -0.5),
        "c_conv_b": nrm((N_C_LAYERS, C_CONV_DIM), 0.02),
        "c_dt_bias": dt0 + jnp.log(-jnp.expm1(-dt0)),
        "c_a_log": jnp.log(jax.random.uniform(next(ks), (N_C_LAYERS, C_HEADS), jnp.float32, 1.0, 16.0)),
        "c_d": 1.0 + nrm((N_C_LAYERS, C_HEADS), 0.02),
        "c_norm": 1.0 + nrm((N_C_LAYERS, C_D_INNER), 0.02),
        "c_w_out": nrm((N_C_LAYERS, C_D_INNER, D_MODEL), C_D_INNER ** -0.5),
    }


def reference(x_prompt, x_sample, cache_swa_k, cache_swa_v, state_ssm, state_conv,
              norm_mixer, norm_mlp, mlp_w_up, mlp_w_down,
              a_w_in, a_norm_v, a_w_spatial, a_b_spatial, a_w_out,
              b_w_qkv, b_q_norm, b_k_norm, b_sinks, rel_bias, b_w_out,
              c_w_in, c_conv_w, c_conv_b, c_dt_bias, c_a_log, c_d, c_norm, c_w_out):
    xp, xs = x_prompt, x_sample
    chunk_v_s = []
    swa_kp, swa_vp, swa_ks, swa_vs = [], [], [], []
    ssm_p, conv_p, ssm_s, conv_s = [], [], [], []
    for i in range(DEPTH):
        kind = i % N_MIXERS
        j = i // N_MIXERS
        hp = rmsnorm(xp, norm_mixer[i])
        hs = rmsnorm(xs, norm_mixer[i])
        if kind == 0:
            op, _ = chunk_mlp_mixer(hp, a_w_in[j], a_norm_v[j], a_w_spatial[j], a_b_spatial[j], a_w_out[j], A_CHUNK)
            os_, v_new = chunk_mlp_mixer(hs, a_w_in[j], a_norm_v[j], a_w_spatial[j], a_b_spatial[j], a_w_out[j], DEC_SEQ)
            chunk_v_s.append(v_new)
        elif kind == 1:
            op, kp, vp = swa_prompt(hp, b_w_qkv[j], b_q_norm[j], b_k_norm[j], b_sinks[j], rel_bias, b_w_out[j])
            os_, ks_, vs_ = swa_sample(hs, cache_swa_k[j], cache_swa_v[j], b_w_qkv[j], b_q_norm[j], b_k_norm[j],
                                       b_sinks[j], rel_bias, b_w_out[j])
            swa_kp.append(kp); swa_vp.append(vp); swa_ks.append(ks_); swa_vs.append(vs_)
        else:
            h0p = jnp.zeros((xp.shape[0], C_HEADS, C_HEAD_DIM, C_D_STATE), xp.dtype)
            bufp = jnp.zeros((xp.shape[0], C_D_CONV - 1, C_CONV_DIM), xp.dtype)
            op, hpn, bpn = ssd_mixer(hp, h0p, bufp, c_w_in[j], c_conv_w[j], c_conv_b[j], c_dt_bias[j],
                                     c_a_log[j], c_d[j], c_norm[j], c_w_out[j], C_CHUNK)
            os_, hsn, bsn = ssd_mixer(hs, state_ssm[j], state_conv[j], c_w_in[j], c_conv_w[j], c_conv_b[j],
                                      c_dt_bias[j], c_a_log[j], c_d[j], c_norm[j], c_w_out[j], DEC_SEQ)
            ssm_p.append(hpn); conv_p.append(bpn); ssm_s.append(hsn); conv_s.append(bsn)
        xp = xp + op
        xs = xs + os_
        xp = xp + sq_relu_mlp(rmsnorm(xp, norm_mlp[i]), mlp_w_up[i], mlp_w_down[i])
        xs = xs + sq_relu_mlp(rmsnorm(xs, norm_mlp[i]), mlp_w_up[i], mlp_w_down[i])
    return (xp, xs, jnp.stack(chunk_v_s),
            jnp.stack(swa_kp), jnp.stack(swa_vp), jnp.stack(swa_ks), jnp.stack(swa_vs),
            jnp.stack(ssm_p), jnp.stack(conv_p), jnp.stack(ssm_s), jnp.stack(conv_s))
```

```python
import functools
import math

import jax
import jax.numpy as jnp
import numpy as np
from jax import lax
from jax.experimental import pallas as pl
from jax.experimental.pallas import tpu as pltpu

f32 = jnp.float32
bf16 = jnp.bfloat16

D_MODEL = 1024
BATCH = 4
SEQ = 4096
DEPTH = 4
DEC_BATCH = 128
DEC_SEQ = 4
PAST_LEN = 8192
N_MIXERS = 3
D_FF = 4 * D_MODEL
EPS = 1e-6
NEG_INF = -1e30

A_CHUNK = 128
A_D_FFN = 6 * D_MODEL
A_HALF = A_D_FFN // 2
A_GROUPS = 8
A_GROUP_W = A_HALF // A_GROUPS

B_HEADS = 16
B_KV_HEADS = 4
B_HEAD_DIM = 64
B_REP = B_HEADS // B_KV_HEADS
B_WINDOW = 128
B_BLOCK = 128
B_Q_DIM = B_HEADS * B_HEAD_DIM
B_KV_DIM = B_KV_HEADS * B_HEAD_DIM
N_BUCKETS = 32
MAX_DISTANCE = 128

C_D_INNER = 2 * D_MODEL
C_HEAD_DIM = 64
C_HEADS = C_D_INNER // C_HEAD_DIM
C_GROUPS = 4
C_REP = C_HEADS // C_GROUPS
C_D_STATE = 128
C_D_CONV = 4
C_BC_DIM = C_GROUPS * C_D_STATE
C_CONV_DIM = C_D_INNER + 2 * C_BC_DIM
C_GROUP_W = C_D_INNER // C_GROUPS
C_CHUNK = 128

LANES = 128
SUBLANES = 8
VMEM_LIMIT_BYTES = 56 * 1024 * 1024

PROMPT_ROWS = BATCH * SEQ
SAMPLE_ROWS = DEC_BATCH * DEC_SEQ
TM_PROMPT = 1024
TM_SAMPLE = SAMPLE_ROWS


def _params(*sem):
    return pltpu.CompilerParams(dimension_semantics=sem, vmem_limit_bytes=VMEM_LIMIT_BYTES)


def _rms(x, g):
    ms = jnp.mean(x * x, axis=-1, keepdims=True)
    return x * lax.rsqrt(ms + EPS) * g


def _gelu(x):
    return 0.5 * x * (1.0 + lax.erf(x * math.sqrt(0.5)))


def _silu(x):
    return x * jax.nn.sigmoid(x)


def _softplus(x):
    return jnp.maximum(x, 0.0) + jnp.log1p(jnp.exp(-jnp.abs(x)))


def _dot(a, b):
    return jnp.dot(a, b, preferred_element_type=f32)


def _dot_nt(a, b):
    return lax.dot_general(a, b, (((1,), (1,)), ((), ())), preferred_element_type=f32)


def _dot_tn(a, b):
    return lax.dot_general(a, b, (((0,), (0,)), ((), ())), preferred_element_type=f32)


def _dot_exact_lhs01(a01, x):
    a = a01.astype(bf16)
    hi = x.astype(bf16)
    r1 = x - hi.astype(f32)
    mid = r1.astype(bf16)
    lo = (r1 - mid.astype(f32)).astype(bf16)
    return _dot(a, hi) + _dot(a, mid) + _dot(a, lo)


def _dot_exact_rhs01(x, b01):
    b = b01.astype(bf16)
    hi = x.astype(bf16)
    r1 = x - hi.astype(f32)
    mid = r1.astype(bf16)
    lo = (r1 - mid.astype(f32)).astype(bf16)
    return _dot(hi, b) + _dot(mid, b) + _dot(lo, b)


def _norm_matmul_kernel(x_ref, g_ref, w_ref, o_ref, xn_ref, *, act):
    @pl.when(pl.program_id(1) == 0)
    def _():
        xn_ref[...] = _rms(x_ref[...], g_ref[...]).astype(bf16)

    y = _dot(xn_ref[...], w_ref[...])
    if act == "gelu":
        y = _gelu(y)
    o_ref[...] = y


def norm_matmul(x, g, w, *, tm, tn, act=None):
    m, k = x.shape
    n = w.shape[1]
    return pl.pallas_call(
        functools.partial(_norm_matmul_kernel, act=act),
        grid=(m // tm, n // tn),
        in_specs=[
            pl.BlockSpec((tm, k), lambda i, j: (i, 0)),
            pl.BlockSpec((1, k), lambda i, j: (0, 0)),
            pl.BlockSpec((k, tn), lambda i, j: (0, j)),
        ],
        out_specs=pl.BlockSpec((tm, tn), lambda i, j: (i, j)),
        out_shape=jax.ShapeDtypeStruct((m, n), f32),
        scratch_shapes=[pltpu.VMEM((tm, k), bf16)],
        compiler_params=_params("parallel", "arbitrary"),
        name="norm_matmul",
    )(x, g, w)


def _mlp_kernel(x_ref, g_ref, wu_ref, wd_ref, o_ref, xn_ref):
    @pl.when(pl.program_id(1) == 0)
    def _():
        x = x_ref[...]
        xn_ref[...] = _rms(x, g_ref[...]).astype(bf16)
        o_ref[...] = x

    h = jnp.maximum(_dot(xn_ref[...], wu_ref[...]), 0.0)
    o_ref[...] += _dot((h * h).astype(bf16), wd_ref[...])


def mlp(x, g, w_up, w_down, *, tm, tf):
    m, d = x.shape
    ff = w_up.shape[1]
    return pl.pallas_call(
        _mlp_kernel,
        grid=(m // tm, ff // tf),
        in_specs=[
            pl.BlockSpec((tm, d), lambda i, j: (i, 0)),
            pl.BlockSpec((1, d), lambda i, j: (0, 0)),
            pl.BlockSpec((d, tf), lambda i, j: (0, j)),
            pl.BlockSpec((tf, d), lambda i, j: (j, 0)),
        ],
        out_specs=pl.BlockSpec((tm, d), lambda i, j: (i, 0)),
        out_shape=jax.ShapeDtypeStruct((m, d), f32),
        scratch_shapes=[pltpu.VMEM((tm, d), bf16)],
        compiler_params=_params("parallel", "arbitrary"),
        name="mlp",
    )(x, g, w_up, w_down)


def _matmul_res_kernel(a_ref, w_ref, r_ref, o_ref):
    o_ref[...] = r_ref[...] + _dot(a_ref[...], w_ref[...])


def matmul_res(a, w, res, *, tm):
    m, k = a.shape
    n = w.shape[1]
    return pl.pallas_call(
        _matmul_res_kernel,
        grid=(m // tm,),
        in_specs=[
            pl.BlockSpec((tm, k), lambda i: (i, 0)),
            pl.BlockSpec((k, n), lambda i: (0, 0)),
            pl.BlockSpec((tm, n), lambda i: (i, 0)),
        ],
        out_specs=pl.BlockSpec((tm, n), lambda i: (i, 0)),
        out_shape=jax.ShapeDtypeStruct((m, n), f32),
        compiler_params=_params("parallel"),
        name="matmul_res",
    )(a, w, res)


GATE_TM = 2 * A_CHUNK


def _gate_prompt_kernel(uv_ref, nv_ref, ws_ref, bs_ref, us_ref):
    vn = _rms(uv_ref[:, A_HALF:], nv_ref[...])
    row = lax.broadcasted_iota(jnp.int32, (A_CHUNK, A_CHUNK), 0)
    col = lax.broadcasted_iota(jnp.int32, (A_CHUNK, A_CHUNK), 1)
    causal = row >= col
    bs = bs_ref[...]
    for g in range(A_GROUPS):
        w = jnp.where(causal, ws_ref[g], 0.0).astype(bf16)
        bias = bs[:, g:g + 1]
        for c in range(GATE_TM // A_CHUNK):
            rows = slice(c * A_CHUNK, (c + 1) * A_CHUNK)
            cols = slice(g * A_GROUP_W, (g + 1) * A_GROUP_W)
            s = _dot(w, vn[rows, cols].astype(bf16)) + bias
            us_ref[rows, cols] = (uv_ref[rows, cols] * s).astype(bf16)


def gate_prompt(uv, norm_v, w_sp, b_sp_t):
    m = uv.shape[0]
    return pl.pallas_call(
        _gate_prompt_kernel,
        grid=(m // GATE_TM,),
        in_specs=[
            pl.BlockSpec((GATE_TM, A_D_FFN), lambda i: (i, 0)),
            pl.BlockSpec((1, A_HALF), lambda i: (0, 0)),
            pl.BlockSpec((A_GROUPS, A_CHUNK, A_CHUNK), lambda i: (0, 0, 0)),
            pl.BlockSpec((A_CHUNK, A_GROUPS), lambda i: (0, 0)),
        ],
        out_specs=pl.BlockSpec((GATE_TM, A_HALF), lambda i: (i, 0)),
        out_shape=jax.ShapeDtypeStruct((m, A_HALF), bf16),
        compiler_params=_params("parallel"),
        name="gate_prompt",
    )(uv, norm_v, w_sp, b_sp_t)


GATE_BT = 16


def _gate_sample_kernel(ws_ref, bs_ref, uv_ref, nv_ref, us_ref, v_ref):
    nv = nv_ref[...]
    vn = [_rms(uv_ref[t, :, A_HALF:], nv) for t in range(DEC_SEQ)]
    for t in range(DEC_SEQ):
        v_ref[t] = vn[t]
    for g in range(A_GROUPS):
        cols = slice(g * A_GROUP_W, (g + 1) * A_GROUP_W)
        for t in range(DEC_SEQ):
            s = ws_ref[(g * DEC_SEQ + t) * DEC_SEQ] * vn[0][:, cols]
            for t2 in range(1, t + 1):
                s = s + ws_ref[(g * DEC_SEQ + t) * DEC_SEQ + t2] * vn[t2][:, cols]
            s = s + bs_ref[g * DEC_SEQ + t]
            us_ref[t, :, cols] = (uv_ref[t, :, cols] * s).astype(bf16)


def gate_sample(uv_t, norm_v, ws_flat, bs_flat):
    smem = pl.BlockSpec(memory_space=pltpu.SMEM)
    return pl.pallas_call(
        _gate_sample_kernel,
        grid=(DEC_BATCH // GATE_BT,),
        in_specs=[
            smem,
            smem,
            pl.BlockSpec((DEC_SEQ, GATE_BT, A_D_FFN), lambda i: (0, i, 0)),
            pl.BlockSpec((1, A_HALF), lambda i: (0, 0)),
        ],
        out_specs=[
            pl.BlockSpec((DEC_SEQ, GATE_BT, A_HALF), lambda i: (0, i, 0)),
            pl.BlockSpec((DEC_SEQ, GATE_BT, A_HALF), lambda i: (0, i, 0)),
        ],
        out_shape=[
            jax.ShapeDtypeStruct((DEC_SEQ, DEC_BATCH, A_HALF), bf16),
            jax.ShapeDtypeStruct((DEC_SEQ, DEC_BATCH, A_HALF), f32),
        ],
        compiler_params=_params("parallel"),
        name="gate_sample",
    )(ws_flat, bs_flat, uv_t, norm_v)


def _bucket_table():
    i = np.arange(B_BLOCK)[:, None]
    j = np.arange(2 * B_BLOCK)[None, :]
    n = np.maximum(B_BLOCK + i - j, 0)
    max_exact = N_BUCKETS // 2
    nf = np.maximum(n, 1).astype(np.float64)
    val = np.log(nf / max_exact) / math.log(MAX_DISTANCE / max_exact) * (N_BUCKETS - max_exact)
    in_window = (n >= max_exact) & (n < B_WINDOW)
    assert np.all(np.abs(val - np.round(val))[in_window & (n != max_exact)] > 1e-3)
    large = np.minimum(max_exact + np.floor(val + 1e-9).astype(np.int64), N_BUCKETS - 1)
    return np.where(n < max_exact, n, large).astype(np.int32)


def _bias_table_kernel(rb_ref, bk_ref, o_ref):
    bk = bk_ref[...]
    for h in range(B_HEADS):
        acc = jnp.zeros(bk.shape, f32)
        for b in range(N_BUCKETS):
            acc = jnp.where(bk == b, rb_ref[b * B_HEADS + h], acc)
        o_ref[h] = acc


def bias_table(rel_bias):
    return pl.pallas_call(
        _bias_table_kernel,
        in_specs=[pl.BlockSpec(memory_space=pltpu.SMEM), pl.BlockSpec(memory_space=pltpu.VMEM)],
        out_specs=pl.BlockSpec(memory_space=pltpu.VMEM),
        out_shape=jax.ShapeDtypeStruct((B_HEADS, B_BLOCK, 2 * B_BLOCK), f32),
        name="bias_table",
    )(rel_bias.reshape(-1), jnp.asarray(_bucket_table()))


def _softmax_with_sink(logits, sink):
    m = jnp.maximum(jnp.max(logits, axis=-1, keepdims=True), sink)
    p = jnp.exp(logits - m)
    return p, jnp.sum(p, axis=-1, keepdims=True) + jnp.exp(sink - m)


def _attn_prompt_kernel(sink_ref, qkv_ref, tab_ref, qn_ref, kn_ref, o_ref, ko_ref, vo_ref, kband_ref, vband_ref):
    n = pl.program_id(1)

    @pl.when(n == 0)
    def _():
        kband_ref[0:B_BLOCK, :] = jnp.zeros((B_BLOCK, B_KV_DIM), kband_ref.dtype)
        vband_ref[0:B_BLOCK, :] = jnp.zeros((B_BLOCK, B_KV_DIM), vband_ref.dtype)

    qn = qn_ref[...]
    kn = kn_ref[...]
    for g in range(B_KV_HEADS):
        gd = slice(g * B_HEAD_DIM, (g + 1) * B_HEAD_DIM)
        kg = _rms(qkv_ref[:, B_Q_DIM + g * B_HEAD_DIM:B_Q_DIM + (g + 1) * B_HEAD_DIM], kn)
        ko_ref[0, :, gd] = kg
        kband_ref[B_BLOCK:, gd] = kg
    v = qkv_ref[:, B_Q_DIM + B_KV_DIM:]
    vo_ref[0] = v
    vband_ref[B_BLOCK:, :] = v

    i = lax.broadcasted_iota(jnp.int32, (B_BLOCK, 2 * B_BLOCK), 0)
    j = lax.broadcasted_iota(jnp.int32, (B_BLOCK, 2 * B_BLOCK), 1)
    first_key = jnp.where(n == 0, B_BLOCK, 0)
    valid = (j > i) & (j <= i + B_WINDOW) & (j >= first_key)
    for g in range(B_KV_HEADS):
        gd = slice(g * B_HEAD_DIM, (g + 1) * B_HEAD_DIM)
        kg = kband_ref[:, gd].astype(bf16)
        vg = vband_ref[:, gd].astype(bf16)
        for r in range(B_REP):
            h = g * B_REP + r
            hd = slice(h * B_HEAD_DIM, (h + 1) * B_HEAD_DIM)
            qh = _rms(qkv_ref[:, hd], qn).astype(bf16)
            logits = _dot_nt(qh, kg) * (B_HEAD_DIM ** -0.5)
            logits = jnp.where(valid, logits + tab_ref[h], NEG_INF)
            p, denom = _softmax_with_sink(logits, sink_ref[h])
            o_ref[:, hd] = (_dot(p.astype(bf16), vg) / denom).astype(bf16)
    kband_ref[0:B_BLOCK, :] = kband_ref[B_BLOCK:, :]
    vband_ref[0:B_BLOCK, :] = vband_ref[B_BLOCK:, :]


def attn_prompt(qkv, table, q_norm, k_norm, sinks):
    nb = SEQ // B_BLOCK
    return pl.pallas_call(
        _attn_prompt_kernel,
        grid=(BATCH, nb),
        in_specs=[
            pl.BlockSpec(memory_space=pltpu.SMEM),
            pl.BlockSpec((B_BLOCK, B_Q_DIM + 2 * B_KV_DIM), lambda b, n: (b * nb + n, 0)),
            pl.BlockSpec((B_HEADS, B_BLOCK, 2 * B_BLOCK), lambda b, n: (0, 0, 0)),
            pl.BlockSpec((1, B_HEAD_DIM), lambda b, n: (0, 0)),
            pl.BlockSpec((1, B_HEAD_DIM), lambda b, n: (0, 0)),
        ],
        out_specs=[
            pl.BlockSpec((B_BLOCK, B_Q_DIM), lambda b, n: (b * nb + n, 0)),
            pl.BlockSpec((1, B_BLOCK, B_KV_DIM), lambda b, n: (b, 0, 0)),
            pl.BlockSpec((1, B_BLOCK, B_KV_DIM), lambda b, n: (b, 0, 0)),
        ],
        out_shape=[
            jax.ShapeDtypeStruct((PROMPT_ROWS, B_Q_DIM), bf16),
            jax.ShapeDtypeStruct((BATCH, B_BLOCK, B_KV_DIM), f32),
            jax.ShapeDtypeStruct((BATCH, B_BLOCK, B_KV_DIM), f32),
        ],
        scratch_shapes=[pltpu.VMEM((2 * B_BLOCK, B_KV_DIM), f32), pltpu.VMEM((2 * B_BLOCK, B_KV_DIM), f32)],
        compiler_params=_params("parallel", "arbitrary"),
        name="attn_prompt",
    )(sinks, qkv, table, q_norm, k_norm)


ATT_BB = 8
ATT_QR = B_REP * DEC_SEQ
ATT_KEYS = B_WINDOW + 2 * DEC_SEQ


def _attn_sample_kernel(sink_ref, q_ref, kn_ref, vn_ref, kc_ref, vc_ref, tab_ref, qn_ref, knm_ref,
                        o_ref, ko_ref, vo_ref, kall_ref, vall_ref):
    qn = qn_ref[...]
    knm = knm_ref[...]
    row = lax.broadcasted_iota(jnp.int32, (ATT_QR, ATT_KEYS), 0)
    j = lax.broadcasted_iota(jnp.int32, (ATT_QR, ATT_KEYS), 1)
    t = row % DEC_SEQ
    valid = (j > t) & (j <= t + B_WINDOW)
    r_col = lax.broadcasted_iota(jnp.int32, (ATT_QR, 1), 0) // DEC_SEQ
    pad = jnp.zeros((ATT_KEYS - B_WINDOW - DEC_SEQ, B_KV_DIM), f32)
    kall_ref[B_WINDOW + DEC_SEQ:, :] = pad
    vall_ref[B_WINDOW + DEC_SEQ:, :] = pad
    for s in range(ATT_BB):
        kc = kc_ref[s]
        vc = vc_ref[s]
        k_new = jnp.concatenate(
            [_rms(kn_ref[s, :, g * B_HEAD_DIM:(g + 1) * B_HEAD_DIM], knm) for g in range(B_KV_HEADS)], axis=1)
        v_new = vn_ref[s]
        ko_ref[s, 0:B_WINDOW - DEC_SEQ, :] = kc[DEC_SEQ:, :]
        ko_ref[s, B_WINDOW - DEC_SEQ:, :] = k_new
        vo_ref[s, 0:B_WINDOW - DEC_SEQ, :] = vc[DEC_SEQ:, :]
        vo_ref[s, B_WINDOW - DEC_SEQ:, :] = v_new
        kall_ref[0:B_WINDOW, :] = kc
        kall_ref[B_WINDOW:B_WINDOW + DEC_SEQ, :] = k_new
        vall_ref[0:B_WINDOW, :] = vc
        vall_ref[B_WINDOW:B_WINDOW + DEC_SEQ, :] = v_new
        q = q_ref[s]
        for g in range(B_KV_HEADS):
            gd = slice(g * B_HEAD_DIM, (g + 1) * B_HEAD_DIM)
            qg = _rms(q[:, gd], qn).astype(bf16)
            keys = kall_ref[:, gd].astype(bf16)
            vals = vall_ref[:, gd].astype(bf16)
            logits = _dot_nt(qg, keys) * (B_HEAD_DIM ** -0.5)
            logits = jnp.where(valid, logits + tab_ref[g], NEG_INF)
            sink = jnp.zeros((ATT_QR, 1), f32)
            for r in range(B_REP):
                sink = jnp.where(r_col == r, sink_ref[g * B_REP + r], sink)
            p, denom = _softmax_with_sink(logits, sink)
            o_ref[s, :, gd] = _dot(p.astype(bf16), vals) / denom


def attn_sample(q_s, k_new, v_new, k_cache, v_cache, table_s, q_norm, k_norm, sinks):
    blk = lambda *shape: pl.BlockSpec((ATT_BB,) + shape, lambda i: (i,) + (0,) * len(shape))
    full = lambda *shape: pl.BlockSpec(shape, lambda i: (0,) * len(shape))
    return pl.pallas_call(
        _attn_sample_kernel,
        grid=(DEC_BATCH // ATT_BB,),
        in_specs=[
            pl.BlockSpec(memory_space=pltpu.SMEM),
            blk(ATT_QR, B_KV_DIM), blk(DEC_SEQ, B_KV_DIM), blk(DEC_SEQ, B_KV_DIM),
            blk(B_WINDOW, B_KV_DIM), blk(B_WINDOW, B_KV_DIM),
            full(B_KV_HEADS, ATT_QR, ATT_KEYS), full(1, B_HEAD_DIM), full(1, B_HEAD_DIM),
        ],
        out_specs=[blk(ATT_QR, B_KV_DIM), blk(B_WINDOW, B_KV_DIM), blk(B_WINDOW, B_KV_DIM)],
        out_shape=[
            jax.ShapeDtypeStruct((DEC_BATCH, ATT_QR, B_KV_DIM), f32),
            jax.ShapeDtypeStruct((DEC_BATCH, B_WINDOW, B_KV_DIM), f32),
            jax.ShapeDtypeStruct((DEC_BATCH, B_WINDOW, B_KV_DIM), f32),
        ],
        scratch_shapes=[pltpu.VMEM((ATT_KEYS, B_KV_DIM), f32), pltpu.VMEM((ATT_KEYS, B_KV_DIM), f32)],
        compiler_params=_params("parallel"),
        name="attn_sample",
    )(sinks, q_s, k_new, v_new, k_cache, v_cache, table_s, q_norm, k_norm)


CONV_PAD = SUBLANES


def _gated_group_norm(y, z, norm_w):
    gt = y * _silu(z)
    parts = []
    for g in range(C_GROUPS):
        gg = gt[:, g * C_GROUP_W:(g + 1) * C_GROUP_W]
        parts.append(gg * lax.rsqrt(jnp.mean(gg * gg, axis=-1, keepdims=True) + EPS))
    return jnp.concatenate(parts, axis=1) * norm_w


def _ssd_prompt_kernel(zx_ref, dtr_ref, cw_ref, cb_ref, dtb_ref, alog_ref, dsk_ref, nw_ref,
                       yn_ref, hfin_ref, cout_ref, xpad_ref, ht_ref, y_ref):
    c = pl.program_id(1)

    @pl.when(c == 0)
    def _():
        xpad_ref[0:CONV_PAD, :] = jnp.zeros((CONV_PAD, C_CONV_DIM), f32)
        ht_ref[...] = jnp.zeros(ht_ref.shape, f32)

    T = C_CHUNK
    xbc = zx_ref[:, C_D_INNER:]
    xpad_ref[CONV_PAD:, :] = xbc
    cw = cw_ref[...]
    acc = cb_ref[...]
    for tap in range(C_D_CONV):
        acc = acc + xpad_ref[pl.ds(CONV_PAD - (C_D_CONV - 1) + tap, T), :] * cw[tap:tap + 1, :]
    xpad_ref[0:CONV_PAD, :] = xbc[T - CONV_PAD:, :]
    cout_ref[0] = xbc[T - (C_D_CONV - 1):, :]
    act = _silu(acc)
    xs = act[:, :C_D_INNER]
    bm = act[:, C_D_INNER:C_D_INNER + C_BC_DIM]
    cm = act[:, C_D_INNER + C_BC_DIM:]

    dt = _softplus(dtr_ref[...] + dtb_ref[...])
    a_neg = -jnp.exp(alog_ref[...])
    row = lax.broadcasted_iota(jnp.int32, (T, T), 0)
    col = lax.broadcasted_iota(jnp.int32, (T, T), 1)
    causal = row >= col
    acs = _dot_exact_lhs01(causal.astype(f32), dt * a_neg)
    acs_t = acs.T
    dsk = dsk_ref[...]

    for g in range(C_GROUPS):
        ns = slice(g * C_D_STATE, (g + 1) * C_D_STATE)
        b_g = bm[:, ns]
        c_g = cm[:, ns].astype(bf16)
        cb = _dot_nt(c_g, b_g.astype(bf16))
        b_gt = b_g.T.astype(bf16)
        for r in range(C_REP):
            h = g * C_REP + r
            ps = slice(h * C_HEAD_DIM, (h + 1) * C_HEAD_DIM)
            a_col = jnp.broadcast_to(acs[:, h:h + 1], (T, T))
            a_row = jnp.broadcast_to(acs_t[h:h + 1, :], (T, T))
            decay = jnp.where(causal, jnp.exp(a_col - a_row), 0.0)
            a_colp = a_col[:, :C_HEAD_DIM]
            a_last = a_colp[T - 1:T, :]
            xh = xs[:, ps]
            xdt = xh * dt[:, h:h + 1]
            y_intra = _dot((cb * decay).astype(bf16), xdt.astype(bf16))
            h_prev = ht_ref[h]
            y_inter = jnp.exp(a_colp) * _dot(c_g, h_prev.astype(bf16))
            xw = (xdt * jnp.exp(a_last - a_colp)).astype(bf16)
            ht_ref[h] = h_prev * jnp.exp(a_last) + _dot(b_gt, xw)
            y_ref[:, ps] = y_intra + y_inter + dsk[:, ps] * xh

    yn_ref[...] = _gated_group_norm(y_ref[...], zx_ref[:, :C_D_INNER], nw_ref[...]).astype(bf16)

    @pl.when(c == pl.num_programs(1) - 1)
    def _():
        for h in range(C_HEADS):
            hfin_ref[0, h * C_HEAD_DIM:(h + 1) * C_HEAD_DIM, :] = ht_ref[h].T


def ssd_prompt(zx, dtr, conv_w, conv_b, dt_bias, a_log, d_skip, norm_w):
    nc = SEQ // C_CHUNK
    full = lambda *shape: pl.BlockSpec(shape, lambda b, c: (0,) * len(shape))
    return pl.pallas_call(
        _ssd_prompt_kernel,
        grid=(BATCH, nc),
        in_specs=[
            pl.BlockSpec((C_CHUNK, C_D_INNER + C_CONV_DIM), lambda b, c: (b * nc + c, 0)),
            pl.BlockSpec((C_CHUNK, LANES), lambda b, c: (b * nc + c, 0)),
            full(C_D_CONV, C_CONV_DIM), full(1, C_CONV_DIM), full(1, LANES), full(1, LANES),
            full(1, C_D_INNER), full(1, C_D_INNER),
        ],
        out_specs=[
            pl.BlockSpec((C_CHUNK, C_D_INNER), lambda b, c: (b * nc + c, 0)),
            pl.BlockSpec((1, C_D_INNER, C_D_STATE), lambda b, c: (b, 0, 0)),
            pl.BlockSpec((1, C_D_CONV - 1, C_CONV_DIM), lambda b, c: (b, 0, 0)),
        ],
        out_shape=[
            jax.ShapeDtypeStruct((PROMPT_ROWS, C_D_INNER), bf16),
            jax.ShapeDtypeStruct((BATCH, C_D_INNER, C_D_STATE), f32),
            jax.ShapeDtypeStruct((BATCH, C_D_CONV - 1, C_CONV_DIM), f32),
        ],
        scratch_shapes=[
            pltpu.VMEM((CONV_PAD + C_CHUNK, C_CONV_DIM), f32),
            pltpu.VMEM((C_HEADS, C_D_STATE, C_HEAD_DIM), f32),
            pltpu.VMEM((C_CHUNK, C_D_INNER), f32),
        ],
        compiler_params=_params("parallel", "arbitrary"),
        name="ssd_prompt",
    )(zx, dtr, conv_w, conv_b, dt_bias, a_log, d_skip, norm_w)


SSD_BB = 8
SSD_TP = SUBLANES
_N_PAIRS = DEC_SEQ * (DEC_SEQ + 1) // 2
_N_COEF = _N_PAIRS + 2 * DEC_SEQ


def _ssd_sample_kernel(zx_ref, dtr_ref, cs_ref, h0_ref, cw_ref, cb_ref, dtb_ref, alog_ref, dsk_ref, nw_ref,
                       sel_ref, yn_ref, hn_ref, cout_ref, c_scr, b_scr, xw_scr, yi_scr, cd_scr):
    L = DEC_SEQ
    cw = cw_ref[...]
    xp = [cs_ref[k] for k in range(C_D_CONV - 1)] + [zx_ref[t, :, C_D_INNER:] for t in range(L)]
    for k in range(C_D_CONV - 1):
        cout_ref[k] = xp[L + k]
    act = []
    for t in range(L):
        acc = cb_ref[...]
        for tap in range(C_D_CONV):
            acc = acc + xp[t + tap] * cw[tap:tap + 1, :]
        act.append(_silu(acc))
    xs = [a[:, :C_D_INNER] for a in act]
    bm = [a[:, C_D_INNER:C_D_INNER + C_BC_DIM] for a in act]
    cm = [a[:, C_D_INNER + C_BC_DIM:] for a in act]

    a_neg = -jnp.exp(alog_ref[...])
    dt = [_softplus(dtr_ref[t] + dtb_ref[...]) for t in range(L)]
    acs = []
    for t in range(L):
        acs.append(dt[t] * a_neg if t == 0 else acs[t - 1] + dt[t] * a_neg)

    lane_group = lax.broadcasted_iota(jnp.int32, (SSD_BB, LANES), 1) // C_REP
    coefs = []
    for t in range(L):
        for t2 in range(t + 1):
            cbh = jnp.zeros((SSD_BB, LANES), f32)
            for g in range(C_GROUPS):
                ns = slice(g * C_D_STATE, (g + 1) * C_D_STATE)
                cbg = jnp.sum(cm[t][:, ns] * bm[t2][:, ns], axis=-1, keepdims=True)
                cbh = jnp.where(lane_group == g, cbg, cbh)
            coefs.append(cbh * jnp.exp(acs[t] - acs[t2]) * dt[t2])
    for t in range(L):
        coefs.append(jnp.exp(acs[t]))
    for t in range(L):
        coefs.append(jnp.exp(acs[L - 1] - acs[t]) * dt[t])
    coef = jnp.concatenate(coefs, axis=0)
    cexp = _dot_exact_rhs01(coef, sel_ref[...])
    cexp = [cexp[k * SSD_BB:(k + 1) * SSD_BB, :] for k in range(_N_COEF)]
    w_intra = cexp[:_N_PAIRS]
    w_inter = cexp[_N_PAIRS:_N_PAIRS + L]
    w_state = cexp[_N_PAIRS + L:]

    cd = jnp.concatenate([jnp.exp(acs[L - 1]), jnp.zeros((LANES - SSD_BB, LANES), f32)], axis=0)
    cd_t = cd.T
    for s in range(SSD_BB):
        cd_scr[s] = jnp.broadcast_to(cd_t[0:C_HEADS, s:s + 1], (C_HEADS, C_D_STATE))

    zeros_tail = jnp.zeros((SSD_BB, SSD_TP - L, C_D_INNER), f32)
    c_scr[:, L:, :] = zeros_tail[:, :, :C_BC_DIM]
    b_scr[:, L:, :] = zeros_tail[:, :, :C_BC_DIM]
    xw_scr[:, L:, :] = zeros_tail
    for t in range(L):
        xw_t = xs[t] * w_state[t]
        for s in range(SSD_BB):
            c_scr[s, t:t + 1, :] = cm[t][s:s + 1, :]
            b_scr[s, t:t + 1, :] = bm[t][s:s + 1, :]
            xw_scr[s, t:t + 1, :] = xw_t[s:s + 1, :]

    for s in range(SSD_BB):
        for g in range(C_GROUPS):
            ns = slice(g * C_D_STATE, (g + 1) * C_D_STATE)
            gs = slice(g * C_GROUP_W, (g + 1) * C_GROUP_W)
            h0 = h0_ref[s, gs, :]
            yi = _dot_nt(c_scr[s, :, ns].astype(bf16), h0.astype(bf16))
            for t in range(L):
                yi_scr[t, s:s + 1, gs] = yi[t:t + 1, :]
            st = _dot_tn(xw_scr[s, :, gs].astype(bf16), b_scr[s, :, ns].astype(bf16))
            for r in range(C_REP):
                h = g * C_REP + r
                rs = slice(r * C_HEAD_DIM, (r + 1) * C_HEAD_DIM)
                scale = cd_scr[s, h:h + 1, :]
                hn_ref[s, h * C_HEAD_DIM:(h + 1) * C_HEAD_DIM, :] = h0[rs, :] * scale + st[rs, :]

    dsk = dsk_ref[...]
    nw = nw_ref[...]
    pair = 0
    for t in range(L):
        y = w_inter[t] * yi_scr[t] + dsk * xs[t]
        for t2 in range(t + 1):
            y = y + w_intra[pair] * xs[t2]
            pair += 1
        yn_ref[t] = _gated_group_norm(y, zx_ref[t, :, :C_D_INNER], nw).astype(bf16)


def ssd_sample(zx_t, dtr_t, conv_state_t, h0, conv_w, conv_b, dt_bias, a_log, d_skip, norm_w, sel):
    tmaj = lambda n, w: pl.BlockSpec((n, SSD_BB, w), lambda i: (0, i, 0))
    full = lambda *shape: pl.BlockSpec(shape, lambda i: (0,) * len(shape))
    return pl.pallas_call(
        _ssd_sample_kernel,
        grid=(DEC_BATCH // SSD_BB,),
        in_specs=[
            tmaj(DEC_SEQ, C_D_INNER + C_CONV_DIM), tmaj(DEC_SEQ, LANES), tmaj(C_D_CONV - 1, C_CONV_DIM),
            pl.BlockSpec((SSD_BB, C_D_INNER, C_D_STATE), lambda i: (i, 0, 0)),
            full(C_D_CONV, C_CONV_DIM), full(1, C_CONV_DIM), full(1, LANES), full(1, LANES),
            full(1, C_D_INNER), full(1, C_D_INNER), full(LANES, C_D_INNER),
        ],
        out_specs=[
            tmaj(DEC_SEQ, C_D_INNER),
            pl.BlockSpec((SSD_BB, C_D_INNER, C_D_STATE), lambda i: (i, 0, 0)),
            tmaj(C_D_CONV - 1, C_CONV_DIM),
        ],
        out_shape=[
            jax.ShapeDtypeStruct((DEC_SEQ, DEC_BATCH, C_D_INNER), bf16),
            jax.ShapeDtypeStruct((DEC_BATCH, C_D_INNER, C_D_STATE), f32),
            jax.ShapeDtypeStruct((C_D_CONV - 1, DEC_BATCH, C_CONV_DIM), f32),
        ],
        scratch_shapes=[
            pltpu.VMEM((SSD_BB, SSD_TP, C_BC_DIM), f32),
            pltpu.VMEM((SSD_BB, SSD_TP, C_BC_DIM), f32),
            pltpu.VMEM((SSD_BB, SSD_TP, C_D_INNER), f32),
            pltpu.VMEM((DEC_SEQ, SSD_BB, C_D_INNER), f32),
            pltpu.VMEM((SSD_BB, C_HEADS, C_D_STATE), f32),
        ],
        compiler_params=_params("parallel"),
        name="ssd_sample",
    )(zx_t, dtr_t, conv_state_t, h0, conv_w, conv_b, dt_bias, a_log, d_skip, norm_w, sel)


def _pad_lanes(v):
    return jnp.pad(v.astype(f32), (0, LANES - v.shape[0])).reshape(1, LANES)


def _mixer_a(xp, xs, g, w_in, norm_v, w_sp, b_sp, w_out):
    w_in = w_in.astype(bf16)
    w_out = w_out.astype(bf16)
    nv = norm_v.reshape(1, A_HALF)
    uv_p = norm_matmul(xp, g, w_in, tm=TM_PROMPT, tn=1024, act="gelu")
    us_p = gate_prompt(uv_p, nv, w_sp, b_sp.T)
    xp = matmul_res(us_p, w_out, xp, tm=TM_PROMPT)

    uv_s = norm_matmul(xs, g, w_in, tm=TM_SAMPLE, tn=1024, act="gelu")
    us_s, v_s = gate_sample(uv_s.reshape(DEC_SEQ, DEC_BATCH, A_D_FFN), nv,
                            w_sp[:, :DEC_SEQ, :DEC_SEQ].reshape(-1), b_sp[:, :DEC_SEQ].reshape(-1))
    xs = matmul_res(us_s.reshape(SAMPLE_ROWS, A_HALF), w_out, xs, tm=TM_SAMPLE)
    return xp, xs, jnp.swapaxes(v_s, 0, 1)


def _mixer_b(xp, xs, g, k_cache, v_cache, w_qkv, q_norm, k_norm, sinks, rel_bias, w_out):
    w_qkv = w_qkv.astype(bf16)
    w_out = w_out.astype(bf16)
    qn = q_norm.reshape(1, B_HEAD_DIM)
    kn = k_norm.reshape(1, B_HEAD_DIM)
    table = bias_table(rel_bias)

    qkv_p = norm_matmul(xp, g, w_qkv, tm=TM_PROMPT, tn=512)
    o_p, k_p, v_p = attn_prompt(qkv_p, table, qn, kn, sinks)
    xp = matmul_res(o_p, w_out, xp, tm=TM_PROMPT)

    qkv_s = norm_matmul(xs, g, w_qkv, tm=TM_SAMPLE, tn=512).reshape(DEC_SEQ, DEC_BATCH, -1)
    q_s = qkv_s[:, :, :B_Q_DIM].reshape(DEC_SEQ, DEC_BATCH, B_KV_HEADS, B_REP, B_HEAD_DIM)
    q_s = q_s.transpose(1, 3, 0, 2, 4).reshape(DEC_BATCH, ATT_QR, B_KV_DIM)
    k_new = jnp.swapaxes(qkv_s[:, :, B_Q_DIM:B_Q_DIM + B_KV_DIM], 0, 1)
    v_new = jnp.swapaxes(qkv_s[:, :, B_Q_DIM + B_KV_DIM:], 0, 1)
    table_s = table[:, :DEC_SEQ, :ATT_KEYS].reshape(B_KV_HEADS, ATT_QR, ATT_KEYS)
    o_s, k_s, v_s = attn_sample(q_s, k_new, v_new,
                                k_cache.reshape(DEC_BATCH, B_WINDOW, B_KV_DIM),
                                v_cache.reshape(DEC_BATCH, B_WINDOW, B_KV_DIM),
                                table_s, qn, kn, sinks)
    o_s = o_s.reshape(DEC_BATCH, B_REP, DEC_SEQ, B_KV_HEADS, B_HEAD_DIM).transpose(2, 0, 3, 1, 4)
    xs = matmul_res(o_s.reshape(SAMPLE_ROWS, B_Q_DIM).astype(bf16), w_out, xs, tm=TM_SAMPLE)
    kv_shape_p = (BATCH, B_WINDOW, B_KV_HEADS, B_HEAD_DIM)
    kv_shape_s = (DEC_BATCH, B_WINDOW, B_KV_HEADS, B_HEAD_DIM)
    return xp, xs, k_p.reshape(kv_shape_p), v_p.reshape(kv_shape_p), k_s.reshape(kv_shape_s), v_s.reshape(kv_shape_s)


def _mixer_c(xp, xs, g, h0, conv_state, w_in, conv_w, conv_b, dt_bias, a_log, d_skip, norm_w, w_out):
    w_zx = w_in[:, :C_D_INNER + C_CONV_DIM].astype(bf16)
    w_dt = jnp.pad(w_in[:, C_D_INNER + C_CONV_DIM:], ((0, 0), (0, LANES - C_HEADS))).astype(bf16)
    w_out = w_out.astype(bf16)
    cb = conv_b.reshape(1, C_CONV_DIM)
    dtb = _pad_lanes(dt_bias)
    alog = _pad_lanes(a_log)
    dsk = jnp.repeat(d_skip.astype(f32), C_HEAD_DIM).reshape(1, C_D_INNER)
    nw = norm_w.reshape(1, C_D_INNER)

    zx_p = norm_matmul(xp, g, w_zx, tm=TM_PROMPT, tn=1024)
    dtr_p = norm_matmul(xp, g, w_dt, tm=TM_PROMPT, tn=LANES)
    yn_p, h_p, conv_p = ssd_prompt(zx_p, dtr_p, conv_w, cb, dtb, alog, dsk, nw)
    xp = matmul_res(yn_p, w_out, xp, tm=TM_PROMPT)

    zx_s = norm_matmul(xs, g, w_zx, tm=TM_SAMPLE, tn=1024)
    dtr_s = norm_matmul(xs, g, w_dt, tm=TM_SAMPLE, tn=LANES)
    sel = (jnp.arange(LANES)[:, None] == jnp.arange(C_D_INNER)[None, :] // C_HEAD_DIM).astype(f32)
    yn_s, h_s, conv_s = ssd_sample(
        zx_s.reshape(DEC_SEQ, DEC_BATCH, -1), dtr_s.reshape(DEC_SEQ, DEC_BATCH, LANES),
        jnp.swapaxes(conv_state, 0, 1), h0.reshape(DEC_BATCH, C_D_INNER, C_D_STATE),
        conv_w, cb, dtb, alog, dsk, nw, sel)
    xs = matmul_res(yn_s.reshape(SAMPLE_ROWS, C_D_INNER), w_out, xs, tm=TM_SAMPLE)
    st_shape = (C_HEADS, C_HEAD_DIM, C_D_STATE)
    return (xp, xs, h_p.reshape((BATCH,) + st_shape), conv_p,
            h_s.reshape((DEC_BATCH,) + st_shape), jnp.swapaxes(conv_s, 0, 1))


def kernel(x_prompt, x_sample, cache_swa_k, cache_swa_v, state_ssm, state_conv, norm_mixer, norm_mlp, mlp_w_up, mlp_w_down, a_w_in, a_norm_v, a_w_spatial, a_b_spatial, a_w_out, b_w_qkv, b_q_norm, b_k_norm, b_sinks, rel_bias, b_w_out, c_w_in, c_conv_w, c_conv_b, c_dt_bias, c_a_log, c_d, c_norm, c_w_out):
    xp = x_prompt.reshape(PROMPT_ROWS, D_MODEL)
    xs = jnp.swapaxes(x_sample, 0, 1).reshape(SAMPLE_ROWS, D_MODEL)
    chunk_v_s = []
    swa_kp, swa_vp, swa_ks, swa_vs = [], [], [], []
    ssm_p, conv_p, ssm_s, conv_s = [], [], [], []
    for i in range(DEPTH):
        kind = i % N_MIXERS
        j = i // N_MIXERS
        g = norm_mixer[i].reshape(1, D_MODEL)
        if kind == 0:
            xp, xs, v_new = _mixer_a(xp, xs, g, a_w_in[j], a_norm_v[j], a_w_spatial[j], a_b_spatial[j], a_w_out[j])
            chunk_v_s.append(v_new)
        elif kind == 1:
            xp, xs, kp, vp, ks_, vs_ = _mixer_b(xp, xs, g, cache_swa_k[j], cache_swa_v[j], b_w_qkv[j], b_q_norm[j],
                                                b_k_norm[j], b_sinks[j], rel_bias, b_w_out[j])
            swa_kp.append(kp); swa_vp.append(vp); swa_ks.append(ks_); swa_vs.append(vs_)
        else:
            xp, xs, hp, bp, hs, bs = _mixer_c(xp, xs, g, state_ssm[j], state_conv[j], c_w_in[j], c_conv_w[j],
                                              c_conv_b[j], c_dt_bias[j], c_a_log[j], c_d[j], c_norm[j], c_w_out[j])
            ssm_p.append(hp); conv_p.append(bp); ssm_s.append(hs); conv_s.append(bs)
        gm = norm_mlp[i].reshape(1, D_MODEL)
        w_up = mlp_w_up[i].astype(bf16)
        w_down = mlp_w_down[i].astype(bf16)
        xp = mlp(xp, gm, w_up, w_down, tm=TM_PROMPT, tf=1024)
        xs = mlp(xs, gm, w_up, w_down, tm=TM_SAMPLE, tf=1024)
    y_prompt = xp.reshape(BATCH, SEQ, D_MODEL)
    y_sample = jnp.swapaxes(xs.reshape(DEC_SEQ, DEC_BATCH, D_MODEL), 0, 1)
    return (y_prompt, y_sample, jnp.stack(chunk_v_s),
            jnp.stack(swa_kp), jnp.stack(swa_vp), jnp.stack(swa_ks), jnp.stack(swa_vs),
            jnp.stack(ssm_p), jnp.stack(conv_p), jnp.stack(ssm_s), jnp.stack(conv_s))
```

```python
import functools
import math

import jax
import jax.numpy as jnp
import numpy as np
from jax import lax
from jax.experimental import pallas as pl
from jax.experimental.pallas import tpu as pltpu

f32 = jnp.float32
bf16 = jnp.bfloat16

D_MODEL = 1024
BATCH = 4
SEQ = 4096
DEPTH = 4
DEC_BATCH = 128
DEC_SEQ = 4
PAST_LEN = 8192
N_MIXERS = 3
D_FF = 4 * D_MODEL
EPS = 1e-6
NEG_INF = -1e30

A_CHUNK = 128
A_D_FFN = 6 * D_MODEL
A_HALF = A_D_FFN // 2
A_GROUPS = 8
A_GROUP_W = A_HALF // A_GROUPS

B_HEADS = 16
B_KV_HEADS = 4
B_HEAD_DIM = 64
B_REP = B_HEADS // B_KV_HEADS
B_WINDOW = 128
B_BLOCK = 128
B_Q_DIM = B_HEADS * B_HEAD_DIM
B_KV_DIM = B_KV_HEADS * B_HEAD_DIM
N_BUCKETS = 32
MAX_DISTANCE = 128

C_D_INNER = 2 * D_MODEL
C_HEAD_DIM = 64
C_HEADS = C_D_INNER // C_HEAD_DIM
C_GROUPS = 4
C_REP = C_HEADS // C_GROUPS
C_D_STATE = 128
C_D_CONV = 4
C_BC_DIM = C_GROUPS * C_D_STATE
C_CONV_DIM = C_D_INNER + 2 * C_BC_DIM
C_GROUP_W = C_D_INNER // C_GROUPS
C_CHUNK = 128

LANES = 128
SUBLANES = 8
VMEM_LIMIT_BYTES = 56 * 1024 * 1024

PROMPT_ROWS = BATCH * SEQ
SAMPLE_ROWS = DEC_BATCH * DEC_SEQ
TM_PROMPT = 1024
TM_SAMPLE = SAMPLE_ROWS


def _params(*sem):
    return pltpu.CompilerParams(dimension_semantics=sem, vmem_limit_bytes=VMEM_LIMIT_BYTES)


def _rms(x, g):
    ms = jnp.mean(x * x, axis=-1, keepdims=True)
    return x * lax.rsqrt(ms + EPS) * g


def _gelu(x):
    return 0.5 * x * (1.0 + lax.erf(x * math.sqrt(0.5)))


def _silu(x):
    return x * jax.nn.sigmoid(x)


def _softplus(x):
    return jnp.maximum(x, 0.0) + jnp.log1p(jnp.exp(-jnp.abs(x)))


def _dot(a, b):
    return jnp.dot(a, b, preferred_element_type=f32)


def _dot_nt(a, b):
    return lax.dot_general(a, b, (((1,), (1,)), ((), ())), preferred_element_type=f32)


def _dot_tn(a, b):
    return lax.dot_general(a, b, (((0,), (0,)), ((), ())), preferred_element_type=f32)


def _dot_exact_lhs01(a01, x):
    a = a01.astype(bf16)
    hi = x.astype(bf16)
    r1 = x - hi.astype(f32)
    mid = r1.astype(bf16)
    lo = (r1 - mid.astype(f32)).astype(bf16)
    return _dot(a, hi) + _dot(a, mid) + _dot(a, lo)


def _dot_exact_rhs01(x, b01):
    b = b01.astype(bf16)
    hi = x.astype(bf16)
    r1 = x - hi.astype(f32)
    mid = r1.astype(bf16)
    lo = (r1 - mid.astype(f32)).astype(bf16)
    return _dot(hi, b) + _dot(mid, b) + _dot(lo, b)


def _norm_matmul_kernel(x_ref, g_ref, w_ref, o_ref, xn_ref, *, act):
    @pl.when(pl.program_id(1) == 0)
    def _():
        xn_ref[...] = _rms(x_ref[...], g_ref[...]).astype(bf16)

    y = _dot(xn_ref[...], w_ref[...])
    if act == "gelu":
        y = _gelu(y)
    o_ref[...] = y


def norm_matmul(x, g, w, *, tm, tn, act=None):
    m, k = x.shape
    n = w.shape[1]
    return pl.pallas_call(
        functools.partial(_norm_matmul_kernel, act=act),
        grid=(m // tm, n // tn),
        in_specs=[
            pl.BlockSpec((tm, k), lambda i, j: (i, 0)),
            pl.BlockSpec((1, k), lambda i, j: (0, 0)),
            pl.BlockSpec((k, tn), lambda i, j: (0, j)),
        ],
        out_specs=pl.BlockSpec((tm, tn), lambda i, j: (i, j)),
        out_shape=jax.ShapeDtypeStruct((m, n), f32),
        scratch_shapes=[pltpu.VMEM((tm, k), bf16)],
        compiler_params=_params("parallel", "arbitrary"),
        name="norm_matmul",
    )(x, g, w)


def _mlp_kernel(x_ref, g_ref, wu_ref, wd_ref, o_ref, xn_ref):
    @pl.when(pl.program_id(1) == 0)
    def _():
        x = x_ref[...]
        xn_ref[...] = _rms(x, g_ref[...]).astype(bf16)
        o_ref[...] = x

    h = jnp.maximum(_dot(xn_ref[...], wu_ref[...]), 0.0)
    o_ref[...] += _dot((h * h).astype(bf16), wd_ref[...])


def mlp(x, g, w_up, w_down, *, tm, tf):
    m, d = x.shape
    ff = w_up.shape[1]
    return pl.pallas_call(
        _mlp_kernel,
        grid=(m // tm, ff // tf),
        in_specs=[
            pl.BlockSpec((tm, d), lambda i, j: (i, 0)),
            pl.BlockSpec((1, d), lambda i, j: (0, 0)),
            pl.BlockSpec((d, tf), lambda i, j: (0, j)),
            pl.BlockSpec((tf, d), lambda i, j: (j, 0)),
        ],
        out_specs=pl.BlockSpec((tm, d), lambda i, j: (i, 0)),
        out_shape=jax.ShapeDtypeStruct((m, d), f32),
        scratch_shapes=[pltpu.VMEM((tm, d), bf16)],
        compiler_params=_params("parallel", "arbitrary"),
        name="mlp",
    )(x, g, w_up, w_down)


def _matmul_res_kernel(a_ref, w_ref, r_ref, o_ref):
    o_ref[...] = r_ref[...] + _dot(a_ref[...], w_ref[...])


def matmul_res(a, w, res, *, tm):
    m, k = a.shape
    n = w.shape[1]
    return pl.pallas_call(
        _matmul_res_kernel,
        grid=(m // tm,),
        in_specs=[
            pl.BlockSpec((tm, k), lambda i: (i, 0)),
            pl.BlockSpec((k, n), lambda i: (0, 0)),
            pl.BlockSpec((tm, n), lambda i: (i, 0)),
        ],
        out_specs=pl.BlockSpec((tm, n), lambda i: (i, 0)),
        out_shape=jax.ShapeDtypeStruct((m, n), f32),
        compiler_params=_params("parallel"),
        name="matmul_res",
    )(a, w, res)


A_BLK_GROUPS = 2
A_BLK = A_BLK_GROUPS * A_GROUP_W
A_NBLK = A_HALF // A_BLK


def _mixer_a_kernel(*refs, sample):
    if sample:
        ws_ref, bs_ref, x_ref, g_ref, win_ref, nv_ref, wout_ref, o_ref, vo_ref, xn_ref, v_ref, ssq_ref, us_ref = refs
    else:
        x_ref, g_ref, win_ref, nv_ref, ws_ref, bs_ref, wout_ref, o_ref, xn_ref, v_ref, ssq_ref, us_ref = refs
    j = pl.program_id(1)
    tm = x_ref.shape[0]

    @pl.when(j == 0)
    def _():
        xn_ref[...] = _rms(x_ref[...], g_ref[...]).astype(bf16)
        ssq_ref[...] = jnp.zeros(ssq_ref.shape, f32)

    for k in range(A_NBLK):
        @pl.when(j == k)
        def _(k=k):
            v = _gelu(_dot(xn_ref[...], win_ref[...]))
            v_ref[:, k * A_BLK:(k + 1) * A_BLK] = v
            ssq_ref[...] += jnp.sum(v * v, axis=-1, keepdims=True)

    if not sample:
        row = lax.broadcasted_iota(jnp.int32, (A_CHUNK, A_CHUNK), 0)
        col = lax.broadcasted_iota(jnp.int32, (A_CHUNK, A_CHUNK), 1)
        causal = row >= col

    for k in range(A_NBLK):
        @pl.when(j == A_NBLK + k)
        def _(k=k):
            u = _gelu(_dot(xn_ref[...], win_ref[...]))
            rinv = lax.rsqrt(ssq_ref[...] * (1.0 / A_HALF) + EPS)
            for gg in range(A_BLK_GROUPS):
                g = k * A_BLK_GROUPS + gg
                cols = slice(g * A_GROUP_W, (g + 1) * A_GROUP_W)
                ucols = slice(gg * A_GROUP_W, (gg + 1) * A_GROUP_W)
                vn = v_ref[:, cols] * rinv * nv_ref[:, cols]
                if sample:
                    vo_ref[:, cols] = vn
                    vt = [vn[t * DEC_BATCH:(t + 1) * DEC_BATCH] for t in range(DEC_SEQ)]
                    s_rows = []
                    for t in range(DEC_SEQ):
                        s = ws_ref[(g * DEC_SEQ + t) * DEC_SEQ] * vt[0]
                        for t2 in range(1, t + 1):
                            s = s + ws_ref[(g * DEC_SEQ + t) * DEC_SEQ + t2] * vt[t2]
                        s_rows.append(s + bs_ref[g * DEC_SEQ + t])
                    s = jnp.concatenate(s_rows, axis=0)
                else:
                    w = jnp.where(causal, ws_ref[g], 0.0).astype(bf16)
                    bias = bs_ref[:, g:g + 1]
                    vb = vn.astype(bf16)
                    s = jnp.concatenate(
                        [_dot(w, vb[c * A_CHUNK:(c + 1) * A_CHUNK]) + bias for c in range(tm // A_CHUNK)], axis=0)
                us_ref[:, ucols] = (u[:, ucols] * s).astype(bf16)
            y = _dot(us_ref[...], wout_ref[...])
            if k == 0:
                o_ref[...] = x_ref[...] + y
            else:
                o_ref[...] += y


def mixer_a(x, g, w_in, norm_v, w_sp, b_sp, w_out, *, tm, sample):
    m, d = x.shape
    nj = 2 * A_NBLK
    row = lambda w: pl.BlockSpec((tm, w), lambda i, j: (i, 0))
    full = lambda *shape: pl.BlockSpec(shape, lambda i, j: (0,) * len(shape))
    smem = pl.BlockSpec(memory_space=pltpu.SMEM)
    win_spec = pl.BlockSpec((d, A_BLK), lambda i, j: (0, (j + A_NBLK) % nj))
    wout_spec = pl.BlockSpec((A_BLK, d), lambda i, j: (jnp.maximum(j - A_NBLK, 0), 0))
    if sample:
        in_specs = [smem, smem, row(d), full(1, d), win_spec, full(1, A_HALF), wout_spec]
        args = (w_sp, b_sp, x, g, w_in, norm_v, w_out)
        out_specs = [row(d), row(A_HALF)]
        out_shape = [jax.ShapeDtypeStruct((m, d), f32), jax.ShapeDtypeStruct((m, A_HALF), f32)]
    else:
        in_specs = [row(d), full(1, d), win_spec, full(1, A_HALF), full(A_GROUPS, A_CHUNK, A_CHUNK),
                    full(A_CHUNK, A_GROUPS), wout_spec]
        args = (x, g, w_in, norm_v, w_sp, b_sp, w_out)
        out_specs = row(d)
        out_shape = jax.ShapeDtypeStruct((m, d), f32)
    return pl.pallas_call(
        functools.partial(_mixer_a_kernel, sample=sample),
        grid=(m // tm, nj),
        in_specs=in_specs,
        out_specs=out_specs,
        out_shape=out_shape,
        scratch_shapes=[pltpu.VMEM((tm, d), bf16), pltpu.VMEM((tm, A_HALF), f32), pltpu.VMEM((tm, 1), f32),
                        pltpu.VMEM((tm, A_BLK), bf16)],
        compiler_params=_params("parallel", "arbitrary"),
        name="mixer_a_sample" if sample else "mixer_a_prompt",
    )(*args)


def _bucket_table():
    i = np.arange(B_BLOCK)[:, None]
    j = np.arange(2 * B_BLOCK)[None, :]
    n = np.maximum(B_BLOCK + i - j, 0)
    max_exact = N_BUCKETS // 2
    nf = np.maximum(n, 1).astype(np.float64)
    val = np.log(nf / max_exact) / math.log(MAX_DISTANCE / max_exact) * (N_BUCKETS - max_exact)
    in_window = (n >= max_exact) & (n < B_WINDOW)
    assert np.all(np.abs(val - np.round(val))[in_window & (n != max_exact)] > 1e-3)
    large = np.minimum(max_exact + np.floor(val + 1e-9).astype(np.int64), N_BUCKETS - 1)
    return np.where(n < max_exact, n, large).astype(np.int32)


def _bias_table_kernel(rb_ref, bk_ref, o_ref):
    bk = bk_ref[...]
    for h in range(B_HEADS):
        acc = jnp.zeros(bk.shape, f32)
        for b in range(N_BUCKETS):
            acc = jnp.where(bk == b, rb_ref[b * B_HEADS + h], acc)
        o_ref[h] = acc


def bias_table(rel_bias):
    return pl.pallas_call(
        _bias_table_kernel,
        in_specs=[pl.BlockSpec(memory_space=pltpu.SMEM), pl.BlockSpec(memory_space=pltpu.VMEM)],
        out_specs=pl.BlockSpec(memory_space=pltpu.VMEM),
        out_shape=jax.ShapeDtypeStruct((B_HEADS, B_BLOCK, 2 * B_BLOCK), f32),
        name="bias_table",
    )(rel_bias.reshape(-1), jnp.asarray(_bucket_table()))


def _softmax_with_sink(logits, sink):
    m = jnp.maximum(jnp.max(logits, axis=-1, keepdims=True), sink)
    p = jnp.exp(logits - m)
    return p, jnp.sum(p, axis=-1, keepdims=True) + jnp.exp(sink - m)


def _attn_prompt_kernel(sink_ref, qkv_ref, tab_ref, qn_ref, kn_ref, o_ref, ko_ref, vo_ref, kband_ref, vband_ref):
    n = pl.program_id(1)

    @pl.when(n == 0)
    def _():
        kband_ref[0:B_BLOCK, :] = jnp.zeros((B_BLOCK, B_KV_DIM), kband_ref.dtype)
        vband_ref[0:B_BLOCK, :] = jnp.zeros((B_BLOCK, B_KV_DIM), vband_ref.dtype)

    qn = qn_ref[...]
    kn = kn_ref[...]
    for g in range(B_KV_HEADS):
        gd = slice(g * B_HEAD_DIM, (g + 1) * B_HEAD_DIM)
        kg = _rms(qkv_ref[:, B_Q_DIM + g * B_HEAD_DIM:B_Q_DIM + (g + 1) * B_HEAD_DIM], kn)
        ko_ref[0, :, gd] = kg
        kband_ref[B_BLOCK:, gd] = kg
    v = qkv_ref[:, B_Q_DIM + B_KV_DIM:]
    vo_ref[0] = v
    vband_ref[B_BLOCK:, :] = v

    i = lax.broadcasted_iota(jnp.int32, (B_BLOCK, 2 * B_BLOCK), 0)
    j = lax.broadcasted_iota(jnp.int32, (B_BLOCK, 2 * B_BLOCK), 1)
    first_key = jnp.where(n == 0, B_BLOCK, 0)
    valid = (j > i) & (j <= i + B_WINDOW) & (j >= first_key)
    for g in range(B_KV_HEADS):
        gd = slice(g * B_HEAD_DIM, (g + 1) * B_HEAD_DIM)
        kg = kband_ref[:, gd].astype(bf16)
        vg = vband_ref[:, gd].astype(bf16)
        for r in range(B_REP):
            h = g * B_REP + r
            hd = slice(h * B_HEAD_DIM, (h + 1) * B_HEAD_DIM)
            qh = _rms(qkv_ref[:, hd], qn).astype(bf16)
            logits = _dot_nt(qh, kg) * (B_HEAD_DIM ** -0.5)
            logits = jnp.where(valid, logits + tab_ref[h], NEG_INF)
            p, denom = _softmax_with_sink(logits, sink_ref[h])
            o_ref[:, hd] = (_dot(p.astype(bf16), vg) / denom).astype(bf16)
    kband_ref[0:B_BLOCK, :] = kband_ref[B_BLOCK:, :]
    vband_ref[0:B_BLOCK, :] = vband_ref[B_BLOCK:, :]


def attn_prompt(qkv, table, q_norm, k_norm, sinks):
    nb = SEQ // B_BLOCK
    return pl.pallas_call(
        _attn_prompt_kernel,
        grid=(BATCH, nb),
        in_specs=[
            pl.BlockSpec(memory_space=pltpu.SMEM),
            pl.BlockSpec((B_BLOCK, B_Q_DIM + 2 * B_KV_DIM), lambda b, n: (b * nb + n, 0)),
            pl.BlockSpec((B_HEADS, B_BLOCK, 2 * B_BLOCK), lambda b, n: (0, 0, 0)),
            pl.BlockSpec((1, B_HEAD_DIM), lambda b, n: (0, 0)),
            pl.BlockSpec((1, B_HEAD_DIM), lambda b, n: (0, 0)),
        ],
        out_specs=[
            pl.BlockSpec((B_BLOCK, B_Q_DIM), lambda b, n: (b * nb + n, 0)),
            pl.BlockSpec((1, B_BLOCK, B_KV_DIM), lambda b, n: (b, 0, 0)),
            pl.BlockSpec((1, B_BLOCK, B_KV_DIM), lambda b, n: (b, 0, 0)),
        ],
        out_shape=[
            jax.ShapeDtypeStruct((PROMPT_ROWS, B_Q_DIM), bf16),
            jax.ShapeDtypeStruct((BATCH, B_BLOCK, B_KV_DIM), f32),
            jax.ShapeDtypeStruct((BATCH, B_BLOCK, B_KV_DIM), f32),
        ],
        scratch_shapes=[pltpu.VMEM((2 * B_BLOCK, B_KV_DIM), f32), pltpu.VMEM((2 * B_BLOCK, B_KV_DIM), f32)],
        compiler_params=_params("parallel", "arbitrary"),
        name="attn_prompt",
    )(sinks, qkv, table, q_norm, k_norm)


ATT_BB = 8
ATT_QR = B_REP * DEC_SEQ
ATT_KEYS = B_WINDOW + 2 * DEC_SEQ


def _attn_sample_kernel(sink_ref, q_ref, kn_ref, vn_ref, kc_ref, vc_ref, tab_ref, qn_ref, knm_ref,
                        o_ref, ko_ref, vo_ref, kall_ref, vall_ref):
    qn = qn_ref[...]
    knm = knm_ref[...]
    row = lax.broadcasted_iota(jnp.int32, (ATT_QR, ATT_KEYS), 0)
    j = lax.broadcasted_iota(jnp.int32, (ATT_QR, ATT_KEYS), 1)
    t = row % DEC_SEQ
    valid = (j > t) & (j <= t + B_WINDOW)
    r_col = lax.broadcasted_iota(jnp.int32, (ATT_QR, 1), 0) // DEC_SEQ
    pad = jnp.zeros((ATT_KEYS - B_WINDOW - DEC_SEQ, B_KV_DIM), f32)
    kall_ref[B_WINDOW + DEC_SEQ:, :] = pad
    vall_ref[B_WINDOW + DEC_SEQ:, :] = pad
    for s in range(ATT_BB):
        kc = kc_ref[s]
        vc = vc_ref[s]
        k_new = jnp.concatenate(
            [_rms(kn_ref[s, :, g * B_HEAD_DIM:(g + 1) * B_HEAD_DIM], knm) for g in range(B_KV_HEADS)], axis=1)
        v_new = vn_ref[s]
        ko_ref[s, 0:B_WINDOW - DEC_SEQ, :] = kc[DEC_SEQ:, :]
        ko_ref[s, B_WINDOW - DEC_SEQ:, :] = k_new
        vo_ref[s, 0:B_WINDOW - DEC_SEQ, :] = vc[DEC_SEQ:, :]
        vo_ref[s, B_WINDOW - DEC_SEQ:, :] = v_new
        kall_ref[0:B_WINDOW, :] = kc
        kall_ref[B_WINDOW:B_WINDOW + DEC_SEQ, :] = k_new
        vall_ref[0:B_WINDOW, :] = vc
        vall_ref[B_WINDOW:B_WINDOW + DEC_SEQ, :] = v_new
        q = q_ref[s]
        for g in range(B_KV_HEADS):
            gd = slice(g * B_HEAD_DIM, (g + 1) * B_HEAD_DIM)
            qg = _rms(q[:, gd], qn).astype(bf16)
            keys = kall_ref[:, gd].astype(bf16)
            vals = vall_ref[:, gd].astype(bf16)
            logits = _dot_nt(qg, keys) * (B_HEAD_DIM ** -0.5)
            logits = jnp.where(valid, logits + tab_ref[g], NEG_INF)
            sink = jnp.zeros((ATT_QR, 1), f32)
            for r in range(B_REP):
                sink = jnp.where(r_col == r, sink_ref[g * B_REP + r], sink)
            p, denom = _softmax_with_sink(logits, sink)
            o_ref[s, :, gd] = _dot(p.astype(bf16), vals) / denom


def attn_sample(q_s, k_new, v_new, k_cache, v_cache, table_s, q_norm, k_norm, sinks):
    blk = lambda *shape: pl.BlockSpec((ATT_BB,) + shape, lambda i: (i,) + (0,) * len(shape))
    full = lambda *shape: pl.BlockSpec(shape, lambda i: (0,) * len(shape))
    return pl.pallas_call(
        _attn_sample_kernel,
        grid=(DEC_BATCH // ATT_BB,),
        in_specs=[
            pl.BlockSpec(memory_space=pltpu.SMEM),
            blk(ATT_QR, B_KV_DIM), blk(DEC_SEQ, B_KV_DIM), blk(DEC_SEQ, B_KV_DIM),
            blk(B_WINDOW, B_KV_DIM), blk(B_WINDOW, B_KV_DIM),
            full(B_KV_HEADS, ATT_QR, ATT_KEYS), full(1, B_HEAD_DIM), full(1, B_HEAD_DIM),
        ],
        out_specs=[blk(ATT_QR, B_KV_DIM), blk(B_WINDOW, B_KV_DIM), blk(B_WINDOW, B_KV_DIM)],
        out_shape=[
            jax.ShapeDtypeStruct((DEC_BATCH, ATT_QR, B_KV_DIM), f32),
            jax.ShapeDtypeStruct((DEC_BATCH, B_WINDOW, B_KV_DIM), f32),
            jax.ShapeDtypeStruct((DEC_BATCH, B_WINDOW, B_KV_DIM), f32),
        ],
        scratch_shapes=[pltpu.VMEM((ATT_KEYS, B_KV_DIM), f32), pltpu.VMEM((ATT_KEYS, B_KV_DIM), f32)],
        compiler_params=_params("parallel"),
        name="attn_sample",
    )(sinks, q_s, k_new, v_new, k_cache, v_cache, table_s, q_norm, k_norm)


CONV_PAD = SUBLANES


def _gated_group_norm(y, z, norm_w):
    gt = y * _silu(z)
    parts = []
    for g in range(C_GROUPS):
        gg = gt[:, g * C_GROUP_W:(g + 1) * C_GROUP_W]
        parts.append(gg * lax.rsqrt(jnp.mean(gg * gg, axis=-1, keepdims=True) + EPS))
    return jnp.concatenate(parts, axis=1) * norm_w


def _ssd_prompt_kernel(zx_ref, dtr_ref, cw_ref, cb_ref, dtb_ref, alog_ref, dsk_ref, nw_ref,
                       yn_ref, hfin_ref, cout_ref, xpad_ref, ht_ref, y_ref):
    c = pl.program_id(1)

    @pl.when(c == 0)
    def _():
        xpad_ref[0:CONV_PAD, :] = jnp.zeros((CONV_PAD, C_CONV_DIM), f32)
        ht_ref[...] = jnp.zeros(ht_ref.shape, f32)

    T = C_CHUNK
    xbc = zx_ref[:, C_D_INNER:]
    xpad_ref[CONV_PAD:, :] = xbc
    cw = cw_ref[...]
    acc = cb_ref[...]
    for tap in range(C_D_CONV):
        acc = acc + xpad_ref[pl.ds(CONV_PAD - (C_D_CONV - 1) + tap, T), :] * cw[tap:tap + 1, :]
    xpad_ref[0:CONV_PAD, :] = xbc[T - CONV_PAD:, :]
    cout_ref[0] = xbc[T - (C_D_CONV - 1):, :]
    act = _silu(acc)
    xs = act[:, :C_D_INNER]
    bm = act[:, C_D_INNER:C_D_INNER + C_BC_DIM]
    cm = act[:, C_D_INNER + C_BC_DIM:]

    dt = _softplus(dtr_ref[...] + dtb_ref[...])
    a_neg = -jnp.exp(alog_ref[...])
    row = lax.broadcasted_iota(jnp.int32, (T, T), 0)
    col = lax.broadcasted_iota(jnp.int32, (T, T), 1)
    causal = row >= col
    acs = _dot_exact_lhs01(causal.astype(f32), dt * a_neg)
    acs_t = acs.T
    dsk = dsk_ref[...]

    for g in range(C_GROUPS):
        ns = slice(g * C_D_STATE, (g + 1) * C_D_STATE)
        b_g = bm[:, ns]
        c_g = cm[:, ns].astype(bf16)
        cb = _dot_nt(c_g, b_g.astype(bf16))
        b_gt = b_g.T.astype(bf16)
        for r in range(C_REP):
            h = g * C_REP + r
            ps = slice(h * C_HEAD_DIM, (h + 1) * C_HEAD_DIM)
            a_col = jnp.broadcast_to(acs[:, h:h + 1], (T, T))
            a_row = jnp.broadcast_to(acs_t[h:h + 1, :], (T, T))
            decay = jnp.where(causal, jnp.exp(a_col - a_row), 0.0)
            a_colp = a_col[:, :C_HEAD_DIM]
            a_last = a_colp[T - 1:T, :]
            xh = xs[:, ps]
            xdt = xh * dt[:, h:h + 1]
            y_intra = _dot((cb * decay).astype(bf16), xdt.astype(bf16))
            h_prev = ht_ref[h]
            y_inter = jnp.exp(a_colp) * _dot(c_g, h_prev.astype(bf16))
            xw = (xdt * jnp.exp(a_last - a_colp)).astype(bf16)
            ht_ref[h] = h_prev * jnp.exp(a_last) + _dot(b_gt, xw)
            y_ref[:, ps] = y_intra + y_inter + dsk[:, ps] * xh

    yn_ref[...] = _gated_group_norm(y_ref[...], zx_ref[:, :C_D_INNER], nw_ref[...]).astype(bf16)

    @pl.when(c == pl.num_programs(1) - 1)
    def _():
        for h in range(C_HEADS):
            hfin_ref[0, h * C_HEAD_DIM:(h + 1) * C_HEAD_DIM, :] = ht_ref[h].T


def ssd_prompt(zx, dtr, conv_w, conv_b, dt_bias, a_log, d_skip, norm_w):
    nc = SEQ // C_CHUNK
    full = lambda *shape: pl.BlockSpec(shape, lambda b, c: (0,) * len(shape))
    return pl.pallas_call(
        _ssd_prompt_kernel,
        grid=(BATCH, nc),
        in_specs=[
            pl.BlockSpec((C_CHUNK, C_D_INNER + C_CONV_DIM), lambda b, c: (b * nc + c, 0)),
            pl.BlockSpec((C_CHUNK, LANES), lambda b, c: (b * nc + c, 0)),
            full(C_D_CONV, C_CONV_DIM), full(1, C_CONV_DIM), full(1, LANES), full(1, LANES),
            full(1, C_D_INNER), full(1, C_D_INNER),
        ],
        out_specs=[
            pl.BlockSpec((C_CHUNK, C_D_INNER), lambda b, c: (b * nc + c, 0)),
            pl.BlockSpec((1, C_D_INNER, C_D_STATE), lambda b, c: (b, 0, 0)),
            pl.BlockSpec((1, C_D_CONV - 1, C_CONV_DIM), lambda b, c: (b, 0, 0)),
        ],
        out_shape=[
            jax.ShapeDtypeStruct((PROMPT_ROWS, C_D_INNER), bf16),
            jax.ShapeDtypeStruct((BATCH, C_D_INNER, C_D_STATE), f32),
            jax.ShapeDtypeStruct((BATCH, C_D_CONV - 1, C_CONV_DIM), f32),
        ],
        scratch_shapes=[
            pltpu.VMEM((CONV_PAD + C_CHUNK, C_CONV_DIM), f32),
            pltpu.VMEM((C_HEADS, C_D_STATE, C_HEAD_DIM), f32),
            pltpu.VMEM((C_CHUNK, C_D_INNER), f32),
        ],
        compiler_params=_params("parallel", "arbitrary"),
        name="ssd_prompt",
    )(zx, dtr, conv_w, conv_b, dt_bias, a_log, d_skip, norm_w)


SSD_BB = 8
SSD_TP = SUBLANES
_N_PAIRS = DEC_SEQ * (DEC_SEQ + 1) // 2
_N_COEF = _N_PAIRS + 2 * DEC_SEQ


def _ssd_sample_kernel(zx_ref, dtr_ref, cs_ref, h0_ref, cw_ref, cb_ref, dtb_ref, alog_ref, dsk_ref, nw_ref,
                       sel_ref, yn_ref, hn_ref, cout_ref, c_scr, b_scr, xw_scr, yi_scr, cd_scr):
    L = DEC_SEQ
    cw = cw_ref[...]
    xp = [cs_ref[k] for k in range(C_D_CONV - 1)] + [zx_ref[t, :, C_D_INNER:] for t in range(L)]
    for k in range(C_D_CONV - 1):
        cout_ref[k] = xp[L + k]
    act = []
    for t in range(L):
        acc = cb_ref[...]
        for tap in range(C_D_CONV):
            acc = acc + xp[t + tap] * cw[tap:tap + 1, :]
        act.append(_silu(acc))
    xs = [a[:, :C_D_INNER] for a in act]
    bm = [a[:, C_D_INNER:C_D_INNER + C_BC_DIM] for a in act]
    cm = [a[:, C_D_INNER + C_BC_DIM:] for a in act]

    a_neg = -jnp.exp(alog_ref[...])
    dt = [_softplus(dtr_ref[t] + dtb_ref[...]) for t in range(L)]
    acs = []
    for t in range(L):
        acs.append(dt[t] * a_neg if t == 0 else acs[t - 1] + dt[t] * a_neg)

    lane_group = lax.broadcasted_iota(jnp.int32, (SSD_BB, LANES), 1) // C_REP
    coefs = []
    for t in range(L):
        for t2 in range(t + 1):
            cbh = jnp.zeros((SSD_BB, LANES), f32)
            for g in range(C_GROUPS):
                ns = slice(g * C_D_STATE, (g + 1) * C_D_STATE)
                cbg = jnp.sum(cm[t][:, ns] * bm[t2][:, ns], axis=-1, keepdims=True)
                cbh = jnp.where(lane_group == g, cbg, cbh)
            coefs.append(cbh * jnp.exp(acs[t] - acs[t2]) * dt[t2])
    for t in range(L):
        coefs.append(jnp.exp(acs[t]))
    for t in range(L):
        coefs.append(jnp.exp(acs[L - 1] - acs[t]) * dt[t])
    coef = jnp.concatenate(coefs, axis=0)
    cexp = _dot_exact_rhs01(coef, sel_ref[...])
    cexp = [cexp[k * SSD_BB:(k + 1) * SSD_BB, :] for k in range(_N_COEF)]
    w_intra = cexp[:_N_PAIRS]
    w_inter = cexp[_N_PAIRS:_N_PAIRS + L]
    w_state = cexp[_N_PAIRS + L:]

    cd = jnp.concatenate([jnp.exp(acs[L - 1]), jnp.zeros((LANES - SSD_BB, LANES), f32)], axis=0)
    cd_t = cd.T
    for s in range(SSD_BB):
        cd_scr[s] = jnp.broadcast_to(cd_t[0:C_HEADS, s:s + 1], (C_HEADS, C_D_STATE))

    zeros_tail = jnp.zeros((SSD_BB, SSD_TP - L, C_D_INNER), f32)
    c_scr[:, L:, :] = zeros_tail[:, :, :C_BC_DIM]
    b_scr[:, L:, :] = zeros_tail[:, :, :C_BC_DIM]
    xw_scr[:, L:, :] = zeros_tail
    for t in range(L):
        xw_t = xs[t] * w_state[t]
        for s in range(SSD_BB):
            c_scr[s, t:t + 1, :] = cm[t][s:s + 1, :]
            b_scr[s, t:t + 1, :] = bm[t][s:s + 1, :]
            xw_scr[s, t:t + 1, :] = xw_t[s:s + 1, :]

    for s in range(SSD_BB):
        for g in range(C_GROUPS):
            ns = slice(g * C_D_STATE, (g + 1) * C_D_STATE)
            gs = slice(g * C_GROUP_W, (g + 1) * C_GROUP_W)
            h0 = h0_ref[s, gs, :]
            yi = _dot_nt(c_scr[s, :, ns].astype(bf16), h0.astype(bf16))
            for t in range(L):
                yi_scr[t, s:s + 1, gs] = yi[t:t + 1, :]
            st = _dot_tn(xw_scr[s, :, gs].astype(bf16), b_scr[s, :, ns].astype(bf16))
            for r in range(C_REP):
                h = g * C_REP + r
                rs = slice(r * C_HEAD_DIM, (r + 1) * C_HEAD_DIM)
                scale = cd_scr[s, h:h + 1, :]
                hn_ref[s, h * C_HEAD_DIM:(h + 1) * C_HEAD_DIM, :] = h0[rs, :] * scale + st[rs, :]

    dsk = dsk_ref[...]
    nw = nw_ref[...]
    pair = 0
    for t in range(L):
        y = w_inter[t] * yi_scr[t] + dsk * xs[t]
        for t2 in range(t + 1):
            y = y + w_intra[pair] * xs[t2]
            pair += 1
        yn_ref[t] = _gated_group_norm(y, zx_ref[t, :, :C_D_INNER], nw).astype(bf16)


def ssd_sample(zx_t, dtr_t, conv_state_t, h0, conv_w, conv_b, dt_bias, a_log, d_skip, norm_w, sel):
    tmaj = lambda n, w: pl.BlockSpec((n, SSD_BB, w), lambda i: (0, i, 0))
    full = lambda *shape: pl.BlockSpec(shape, lambda i: (0,) * len(shape))
    return pl.pallas_call(
        _ssd_sample_kernel,
        grid=(DEC_BATCH // SSD_BB,),
        in_specs=[
            tmaj(DEC_SEQ, C_D_INNER + C_CONV_DIM), tmaj(DEC_SEQ, LANES), tmaj(C_D_CONV - 1, C_CONV_DIM),
            pl.BlockSpec((SSD_BB, C_D_INNER, C_D_STATE), lambda i: (i, 0, 0)),
            full(C_D_CONV, C_CONV_DIM), full(1, C_CONV_DIM), full(1, LANES), full(1, LANES),
            full(1, C_D_INNER), full(1, C_D_INNER), full(LANES, C_D_INNER),
        ],
        out_specs=[
            tmaj(DEC_SEQ, C_D_INNER),
            pl.BlockSpec((SSD_BB, C_D_INNER, C_D_STATE), lambda i: (i, 0, 0)),
            tmaj(C_D_CONV - 1, C_CONV_DIM),
        ],
        out_shape=[
            jax.ShapeDtypeStruct((DEC_SEQ, DEC_BATCH, C_D_INNER), bf16),
            jax.ShapeDtypeStruct((DEC_BATCH, C_D_INNER, C_D_STATE), f32),
            jax.ShapeDtypeStruct((C_D_CONV - 1, DEC_BATCH, C_CONV_DIM), f32),
        ],
        scratch_shapes=[
            pltpu.VMEM((SSD_BB, SSD_TP, C_BC_DIM), f32),
            pltpu.VMEM((SSD_BB, SSD_TP, C_BC_DIM), f32),
            pltpu.VMEM((SSD_BB, SSD_TP, C_D_INNER), f32),
            pltpu.VMEM((DEC_SEQ, SSD_BB, C_D_INNER), f32),
            pltpu.VMEM((SSD_BB, C_HEADS, C_D_STATE), f32),
        ],
        compiler_params=_params("parallel"),
        name="ssd_sample",
    )(zx_t, dtr_t, conv_state_t, h0, conv_w, conv_b, dt_bias, a_log, d_skip, norm_w, sel)


def _pad_lanes(v):
    return jnp.pad(v.astype(f32), (0, LANES - v.shape[0])).reshape(1, LANES)


def _mixer_a(xp, xs, g, w_in, norm_v, w_sp, b_sp, w_out):
    w_in = w_in.astype(bf16)
    w_out = w_out.astype(bf16)
    nv = norm_v.reshape(1, A_HALF)
    xp = mixer_a(xp, g, w_in, nv, w_sp, b_sp.T, w_out, tm=TM_PROMPT, sample=False)
    xs, v_s = mixer_a(xs, g, w_in, nv, w_sp[:, :DEC_SEQ, :DEC_SEQ].reshape(-1), b_sp[:, :DEC_SEQ].reshape(-1),
                      w_out, tm=TM_SAMPLE, sample=True)
    return xp, xs, jnp.swapaxes(v_s.reshape(DEC_SEQ, DEC_BATCH, A_HALF), 0, 1)


def _mixer_b(xp, xs, g, k_cache, v_cache, w_qkv, q_norm, k_norm, sinks, rel_bias, w_out):
    w_qkv = w_qkv.astype(bf16)
    w_out = w_out.astype(bf16)
    qn = q_norm.reshape(1, B_HEAD_DIM)
    kn = k_norm.reshape(1, B_HEAD_DIM)
    table = bias_table(rel_bias)

    qkv_p = norm_matmul(xp, g, w_qkv, tm=TM_PROMPT, tn=512)
    o_p, k_p, v_p = attn_prompt(qkv_p, table, qn, kn, sinks)
    xp = matmul_res(o_p, w_out, xp, tm=TM_PROMPT)

    qkv_s = norm_matmul(xs, g, w_qkv, tm=TM_SAMPLE, tn=512).reshape(DEC_SEQ, DEC_BATCH, -1)
    q_s = qkv_s[:, :, :B_Q_DIM].reshape(DEC_SEQ, DEC_BATCH, B_KV_HEADS, B_REP, B_HEAD_DIM)
    q_s = q_s.transpose(1, 3, 0, 2, 4).reshape(DEC_BATCH, ATT_QR, B_KV_DIM)
    k_new = jnp.swapaxes(qkv_s[:, :, B_Q_DIM:B_Q_DIM + B_KV_DIM], 0, 1)
    v_new = jnp.swapaxes(qkv_s[:, :, B_Q_DIM + B_KV_DIM:], 0, 1)
    table_s = table[:, :DEC_SEQ, :ATT_KEYS].reshape(B_KV_HEADS, ATT_QR, ATT_KEYS)
    o_s, k_s, v_s = attn_sample(q_s, k_new, v_new,
                                k_cache.reshape(DEC_BATCH, B_WINDOW, B_KV_DIM),
                                v_cache.reshape(DEC_BATCH, B_WINDOW, B_KV_DIM),
                                table_s, qn, kn, sinks)
    o_s = o_s.reshape(DEC_BATCH, B_REP, DEC_SEQ, B_KV_HEADS, B_HEAD_DIM).transpose(2, 0, 3, 1, 4)
    xs = matmul_res(o_s.reshape(SAMPLE_ROWS, B_Q_DIM).astype(bf16), w_out, xs, tm=TM_SAMPLE)
    kv_shape_p = (BATCH, B_WINDOW, B_KV_HEADS, B_HEAD_DIM)
    kv_shape_s = (DEC_BATCH, B_WINDOW, B_KV_HEADS, B_HEAD_DIM)
    return xp, xs, k_p.reshape(kv_shape_p), v_p.reshape(kv_shape_p), k_s.reshape(kv_shape_s), v_s.reshape(kv_shape_s)


def _mixer_c(xp, xs, g, h0, conv_state, w_in, conv_w, conv_b, dt_bias, a_log, d_skip, norm_w, w_out):
    w_zx = w_in[:, :C_D_INNER + C_CONV_DIM].astype(bf16)
    w_dt = jnp.pad(w_in[:, C_D_INNER + C_CONV_DIM:], ((0, 0), (0, LANES - C_HEADS))).astype(bf16)
    w_out = w_out.astype(bf16)
    cb = conv_b.reshape(1, C_CONV_DIM)
    dtb = _pad_lanes(dt_bias)
    alog = _pad_lanes(a_log)
    dsk = jnp.repeat(d_skip.astype(f32), C_HEAD_DIM).reshape(1, C_D_INNER)
    nw = norm_w.reshape(1, C_D_INNER)

    zx_p = norm_matmul(xp, g, w_zx, tm=TM_PROMPT, tn=1024)
    dtr_p = norm_matmul(xp, g, w_dt, tm=TM_PROMPT, tn=LANES)
    yn_p, h_p, conv_p = ssd_prompt(zx_p, dtr_p, conv_w, cb, dtb, alog, dsk, nw)
    xp = matmul_res(yn_p, w_out, xp, tm=TM_PROMPT)

    zx_s = norm_matmul(xs, g, w_zx, tm=TM_SAMPLE, tn=1024)
    dtr_s = norm_matmul(xs, g, w_dt, tm=TM_SAMPLE, tn=LANES)
    sel = (jnp.arange(LANES)[:, None] == jnp.arange(C_D_INNER)[None, :] // C_HEAD_DIM).astype(f32)
    yn_s, h_s, conv_s = ssd_sample(
        zx_s.reshape(DEC_SEQ, DEC_BATCH, -1), dtr_s.reshape(DEC_SEQ, DEC_BATCH, LANES),
        jnp.swapaxes(conv_state, 0, 1), h0.reshape(DEC_BATCH, C_D_INNER, C_D_STATE),
        conv_w, cb, dtb, alog, dsk, nw, sel)
    xs = matmul_res(yn_s.reshape(SAMPLE_ROWS, C_D_INNER), w_out, xs, tm=TM_SAMPLE)
    st_shape = (C_HEADS, C_HEAD_DIM, C_D_STATE)
    return (xp, xs, h_p.reshape((BATCH,) + st_shape), conv_p,
            h_s.reshape((DEC_BATCH,) + st_shape), jnp.swapaxes(conv_s, 0, 1))


def kernel(x_prompt, x_sample, cache_swa_k, cache_swa_v, state_ssm, state_conv, norm_mixer, norm_mlp, mlp_w_up, mlp_w_down, a_w_in, a_norm_v, a_w_spatial, a_b_spatial, a_w_out, b_w_qkv, b_q_norm, b_k_norm, b_sinks, rel_bias, b_w_out, c_w_in, c_conv_w, c_conv_b, c_dt_bias, c_a_log, c_d, c_norm, c_w_out):
    xp = x_prompt.reshape(PROMPT_ROWS, D_MODEL)
    xs = jnp.swapaxes(x_sample, 0, 1).reshape(SAMPLE_ROWS, D_MODEL)
    chunk_v_s = []
    swa_kp, swa_vp, swa_ks, swa_vs = [], [], [], []
    ssm_p, conv_p, ssm_s, conv_s = [], [], [], []
    for i in range(DEPTH):
        kind = i % N_MIXERS
        j = i // N_MIXERS
        g = norm_mixer[i].reshape(1, D_MODEL)
        if kind == 0:
            xp, xs, v_new = _mixer_a(xp, xs, g, a_w_in[j], a_norm_v[j], a_w_spatial[j], a_b_spatial[j], a_w_out[j])
            chunk_v_s.append(v_new)
        elif kind == 1:
            xp, xs, kp, vp, ks_, vs_ = _mixer_b(xp, xs, g, cache_swa_k[j], cache_swa_v[j], b_w_qkv[j], b_q_norm[j],
                                                b_k_norm[j], b_sinks[j], rel_bias, b_w_out[j])
            swa_kp.append(kp); swa_vp.append(vp); swa_ks.append(ks_); swa_vs.append(vs_)
        else:
            xp, xs, hp, bp, hs, bs = _mixer_c(xp, xs, g, state_ssm[j], state_conv[j], c_w_in[j], c_conv_w[j],
                                              c_conv_b[j], c_dt_bias[j], c_a_log[j], c_d[j], c_norm[j], c_w_out[j])
            ssm_p.append(hp); conv_p.append(bp); ssm_s.append(hs); conv_s.append(bs)
        gm = norm_mlp[i].reshape(1, D_MODEL)
        w_up = mlp_w_up[i].astype(bf16)
        w_down = mlp_w_down[i].astype(bf16)
        xp = mlp(xp, gm, w_up, w_down, tm=TM_PROMPT, tf=1024)
        xs = mlp(xs, gm, w_up, w_down, tm=TM_SAMPLE, tf=1024)
    y_prompt = xp.reshape(BATCH, SEQ, D_MODEL)
    y_sample = jnp.swapaxes(xs.reshape(DEC_SEQ, DEC_BATCH, D_MODEL), 0, 1)
    return (y_prompt, y_sample, jnp.stack(chunk_v_s),
            jnp.stack(swa_kp), jnp.stack(swa_vp), jnp.stack(swa_ks), jnp.stack(swa_vs),
            jnp.stack(ssm_p), jnp.stack(conv_p), jnp.stack(ssm_s), jnp.stack(conv_s))
```

```python
import functools
import math

import jax
import jax.numpy as jnp
import numpy as np
from jax import lax
from jax.experimental import pallas as pl
from jax.experimental.pallas import tpu as pltpu

f32 = jnp.float32
bf16 = jnp.bfloat16

D_MODEL = 1024
BATCH = 4
SEQ = 4096
DEPTH = 4
DEC_BATCH = 128
DEC_SEQ = 4
PAST_LEN = 8192
N_MIXERS = 3
D_FF = 4 * D_MODEL
EPS = 1e-6
NEG_INF = -1e30

A_CHUNK = 128
A_D_FFN = 6 * D_MODEL
A_HALF = A_D_FFN // 2
A_GROUPS = 8
A_GROUP_W = A_HALF // A_GROUPS

B_HEADS = 16
B_KV_HEADS = 4
B_HEAD_DIM = 64
B_REP = B_HEADS // B_KV_HEADS
B_WINDOW = 128
B_BLOCK = 128
B_Q_DIM = B_HEADS * B_HEAD_DIM
B_KV_DIM = B_KV_HEADS * B_HEAD_DIM
N_BUCKETS = 32
MAX_DISTANCE = 128

C_D_INNER = 2 * D_MODEL
C_HEAD_DIM = 64
C_HEADS = C_D_INNER // C_HEAD_DIM
C_GROUPS = 4
C_REP = C_HEADS // C_GROUPS
C_D_STATE = 128
C_D_CONV = 4
C_BC_DIM = C_GROUPS * C_D_STATE
C_CONV_DIM = C_D_INNER + 2 * C_BC_DIM
C_GROUP_W = C_D_INNER // C_GROUPS
C_CHUNK = 128

LANES = 128
SUBLANES = 8
VMEM_LIMIT_BYTES = 56 * 1024 * 1024

PROMPT_ROWS = BATCH * SEQ
SAMPLE_ROWS = DEC_BATCH * DEC_SEQ
TM_PROMPT = 1024
TM_SAMPLE = SAMPLE_ROWS


def _params(*sem):
    return pltpu.CompilerParams(dimension_semantics=sem, vmem_limit_bytes=VMEM_LIMIT_BYTES)


def _rms(x, g):
    ms = jnp.mean(x * x, axis=-1, keepdims=True)
    return x * lax.rsqrt(ms + EPS) * g


def _gelu(x):
    return 0.5 * x * (1.0 + lax.erf(x * math.sqrt(0.5)))


def _silu(x):
    return x * jax.nn.sigmoid(x)


def _softplus(x):
    return jnp.maximum(x, 0.0) + jnp.log1p(jnp.exp(-jnp.abs(x)))


def _dot(a, b):
    return jnp.dot(a, b, preferred_element_type=f32)


def _dot_nt(a, b):
    return lax.dot_general(a, b, (((1,), (1,)), ((), ())), preferred_element_type=f32)


def _dot_tn(a, b):
    return lax.dot_general(a, b, (((0,), (0,)), ((), ())), preferred_element_type=f32)


def _dot_exact_lhs01(a01, x):
    a = a01.astype(bf16)
    hi = x.astype(bf16)
    r1 = x - hi.astype(f32)
    mid = r1.astype(bf16)
    lo = (r1 - mid.astype(f32)).astype(bf16)
    return _dot(a, hi) + _dot(a, mid) + _dot(a, lo)


def _dot_exact_rhs01(x, b01):
    b = b01.astype(bf16)
    hi = x.astype(bf16)
    r1 = x - hi.astype(f32)
    mid = r1.astype(bf16)
    lo = (r1 - mid.astype(f32)).astype(bf16)
    return _dot(hi, b) + _dot(mid, b) + _dot(lo, b)


def _norm_matmul_kernel(x_ref, g_ref, w_ref, o_ref, xn_ref, *, act):
    @pl.when(pl.program_id(1) == 0)
    def _():
        xn_ref[...] = _rms(x_ref[...], g_ref[...]).astype(bf16)

    y = _dot(xn_ref[...], w_ref[...])
    if act == "gelu":
        y = _gelu(y)
    o_ref[...] = y


def norm_matmul(x, g, w, *, tm, tn, act=None):
    m, k = x.shape
    n = w.shape[1]
    return pl.pallas_call(
        functools.partial(_norm_matmul_kernel, act=act),
        grid=(m // tm, n // tn),
        in_specs=[
            pl.BlockSpec((tm, k), lambda i, j: (i, 0)),
            pl.BlockSpec((1, k), lambda i, j: (0, 0)),
            pl.BlockSpec((k, tn), lambda i, j: (0, j)),
        ],
        out_specs=pl.BlockSpec((tm, tn), lambda i, j: (i, j)),
        out_shape=jax.ShapeDtypeStruct((m, n), f32),
        scratch_shapes=[pltpu.VMEM((tm, k), bf16)],
        compiler_params=_params("parallel", "arbitrary"),
        name="norm_matmul",
    )(x, g, w)


def _mlp_kernel(x_ref, g_ref, wu_ref, wd_ref, o_ref, xn_ref):
    @pl.when(pl.program_id(1) == 0)
    def _():
        x = x_ref[...]
        xn_ref[...] = _rms(x, g_ref[...]).astype(bf16)
        o_ref[...] = x

    h = jnp.maximum(_dot(xn_ref[...], wu_ref[...]), 0.0)
    o_ref[...] += _dot((h * h).astype(bf16), wd_ref[...])


def mlp(x, g, w_up, w_down, *, tm, tf):
    m, d = x.shape
    ff = w_up.shape[1]
    return pl.pallas_call(
        _mlp_kernel,
        grid=(m // tm, ff // tf),
        in_specs=[
            pl.BlockSpec((tm, d), lambda i, j: (i, 0)),
            pl.BlockSpec((1, d), lambda i, j: (0, 0)),
            pl.BlockSpec((d, tf), lambda i, j: (0, j)),
            pl.BlockSpec((tf, d), lambda i, j: (j, 0)),
        ],
        out_specs=pl.BlockSpec((tm, d), lambda i, j: (i, 0)),
        out_shape=jax.ShapeDtypeStruct((m, d), f32),
        scratch_shapes=[pltpu.VMEM((tm, d), bf16)],
        compiler_params=_params("parallel", "arbitrary"),
        name="mlp",
    )(x, g, w_up, w_down)


def _matmul_res_kernel(a_ref, w_ref, r_ref, o_ref):
    o_ref[...] = r_ref[...] + _dot(a_ref[...], w_ref[...])


def matmul_res(a, w, res, *, tm):
    m, k = a.shape
    n = w.shape[1]
    return pl.pallas_call(
        _matmul_res_kernel,
        grid=(m // tm,),
        in_specs=[
            pl.BlockSpec((tm, k), lambda i: (i, 0)),
            pl.BlockSpec((k, n), lambda i: (0, 0)),
            pl.BlockSpec((tm, n), lambda i: (i, 0)),
        ],
        out_specs=pl.BlockSpec((tm, n), lambda i: (i, 0)),
        out_shape=jax.ShapeDtypeStruct((m, n), f32),
        compiler_params=_params("parallel"),
        name="matmul_res",
    )(a, w, res)


A_BLK_GROUPS = 2
A_BLK = A_BLK_GROUPS * A_GROUP_W
A_NBLK = A_HALF // A_BLK


def _mixer_a_kernel(*refs, sample):
    if sample:
        ws_ref, bs_ref, x_ref, g_ref, win_ref, nv_ref, wout_ref, o_ref, vo_ref, xn_ref, v_ref, ssq_ref, us_ref = refs
    else:
        x_ref, g_ref, win_ref, nv_ref, ws_ref, bs_ref, wout_ref, o_ref, xn_ref, v_ref, ssq_ref, us_ref = refs
    j = pl.program_id(1)
    tm = x_ref.shape[0]

    @pl.when(j == 0)
    def _():
        xn_ref[...] = _rms(x_ref[...], g_ref[...]).astype(bf16)
        ssq_ref[...] = jnp.zeros(ssq_ref.shape, f32)

    for k in range(A_NBLK):
        @pl.when(j == k)
        def _(k=k):
            v = _gelu(_dot(xn_ref[...], win_ref[...]))
            v_ref[:, k * A_BLK:(k + 1) * A_BLK] = v
            ssq_ref[...] += jnp.sum(v * v, axis=-1, keepdims=True)

    if not sample:
        row = lax.broadcasted_iota(jnp.int32, (A_CHUNK, A_CHUNK), 0)
        col = lax.broadcasted_iota(jnp.int32, (A_CHUNK, A_CHUNK), 1)
        causal = row >= col

    for k in range(A_NBLK):
        @pl.when(j == A_NBLK + k)
        def _(k=k):
            u = _gelu(_dot(xn_ref[...], win_ref[...]))
            rinv = lax.rsqrt(ssq_ref[...] * (1.0 / A_HALF) + EPS)
            for gg in range(A_BLK_GROUPS):
                g = k * A_BLK_GROUPS + gg
                cols = slice(g * A_GROUP_W, (g + 1) * A_GROUP_W)
                ucols = slice(gg * A_GROUP_W, (gg + 1) * A_GROUP_W)
                vn = v_ref[:, cols] * rinv * nv_ref[:, cols]
                if sample:
                    vo_ref[:, cols] = vn
                    vt = [vn[t * DEC_BATCH:(t + 1) * DEC_BATCH] for t in range(DEC_SEQ)]
                    s_rows = []
                    for t in range(DEC_SEQ):
                        s = ws_ref[(g * DEC_SEQ + t) * DEC_SEQ] * vt[0]
                        for t2 in range(1, t + 1):
                            s = s + ws_ref[(g * DEC_SEQ + t) * DEC_SEQ + t2] * vt[t2]
                        s_rows.append(s + bs_ref[g * DEC_SEQ + t])
                    s = jnp.concatenate(s_rows, axis=0)
                else:
                    w = jnp.where(causal, ws_ref[g], 0.0).astype(bf16)
                    bias = bs_ref[:, g:g + 1]
                    vb = vn.astype(bf16)
                    s = jnp.concatenate(
                        [_dot(w, vb[c * A_CHUNK:(c + 1) * A_CHUNK]) + bias for c in range(tm // A_CHUNK)], axis=0)
                us_ref[:, ucols] = (u[:, ucols] * s).astype(bf16)
            y = _dot(us_ref[...], wout_ref[...])
            if k == 0:
                o_ref[...] = x_ref[...] + y
            else:
                o_ref[...] += y


def mixer_a(x, g, w_in, norm_v, w_sp, b_sp, w_out, *, tm, sample):
    m, d = x.shape
    nj = 2 * A_NBLK
    row = lambda w: pl.BlockSpec((tm, w), lambda i, j: (i, 0))
    full = lambda *shape: pl.BlockSpec(shape, lambda i, j: (0,) * len(shape))
    smem = pl.BlockSpec(memory_space=pltpu.SMEM)
    win_spec = pl.BlockSpec((d, A_BLK), lambda i, j: (0, (j + A_NBLK) % nj))
    wout_spec = pl.BlockSpec((A_BLK, d), lambda i, j: (jnp.maximum(j - A_NBLK, 0), 0))
    if sample:
        in_specs = [smem, smem, row(d), full(1, d), win_spec, full(1, A_HALF), wout_spec]
        args = (w_sp, b_sp, x, g, w_in, norm_v, w_out)
        out_specs = [row(d), row(A_HALF)]
        out_shape = [jax.ShapeDtypeStruct((m, d), f32), jax.ShapeDtypeStruct((m, A_HALF), f32)]
    else:
        in_specs = [row(d), full(1, d), win_spec, full(1, A_HALF), full(A_GROUPS, A_CHUNK, A_CHUNK),
                    full(A_CHUNK, A_GROUPS), wout_spec]
        args = (x, g, w_in, norm_v, w_sp, b_sp, w_out)
        out_specs = row(d)
        out_shape = jax.ShapeDtypeStruct((m, d), f32)
    return pl.pallas_call(
        functools.partial(_mixer_a_kernel, sample=sample),
        grid=(m // tm, nj),
        in_specs=in_specs,
        out_specs=out_specs,
        out_shape=out_shape,
        scratch_shapes=[pltpu.VMEM((tm, d), bf16), pltpu.VMEM((tm, A_HALF), f32), pltpu.VMEM((tm, 1), f32),
                        pltpu.VMEM((tm, A_BLK), bf16)],
        compiler_params=_params("parallel", "arbitrary"),
        name="mixer_a_sample" if sample else "mixer_a_prompt",
    )(*args)


def _bucket_table():
    i = np.arange(B_BLOCK)[:, None]
    j = np.arange(2 * B_BLOCK)[None, :]
    n = np.maximum(B_BLOCK + i - j, 0)
    max_exact = N_BUCKETS // 2
    nf = np.maximum(n, 1).astype(np.float64)
    val = np.log(nf / max_exact) / math.log(MAX_DISTANCE / max_exact) * (N_BUCKETS - max_exact)
    in_window = (n >= max_exact) & (n < B_WINDOW)
    assert np.all(np.abs(val - np.round(val))[in_window & (n != max_exact)] > 1e-3)
    large = np.minimum(max_exact + np.floor(val + 1e-9).astype(np.int64), N_BUCKETS - 1)
    return np.where(n < max_exact, n, large).astype(np.int32)


def _bias_table_kernel(rb_ref, bk_ref, o_ref):
    bk = bk_ref[...]
    for h in range(B_HEADS):
        acc = jnp.zeros(bk.shape, f32)
        for b in range(N_BUCKETS):
            acc = jnp.where(bk == b, rb_ref[b * B_HEADS + h], acc)
        o_ref[h] = acc


def bias_table(rel_bias):
    return pl.pallas_call(
        _bias_table_kernel,
        in_specs=[pl.BlockSpec(memory_space=pltpu.SMEM), pl.BlockSpec(memory_space=pltpu.VMEM)],
        out_specs=pl.BlockSpec(memory_space=pltpu.VMEM),
        out_shape=jax.ShapeDtypeStruct((B_HEADS, B_BLOCK, 2 * B_BLOCK), f32),
        name="bias_table",
    )(rel_bias.reshape(-1), jnp.asarray(_bucket_table()))


def _softmax_with_sink(logits, sink):
    m = jnp.maximum(jnp.max(logits, axis=-1, keepdims=True), sink)
    p = jnp.exp(logits - m)
    return p, jnp.sum(p, axis=-1, keepdims=True) + jnp.exp(sink - m)


def _rms_head_pairs(x, g2, lo):
    sq = x * x
    s_lo = jnp.sum(jnp.where(lo, sq, 0.0), axis=-1, keepdims=True)
    s_hi = jnp.sum(jnp.where(lo, 0.0, sq), axis=-1, keepdims=True)
    r = lax.rsqrt(jnp.where(lo, s_lo, s_hi) * (1.0 / B_HEAD_DIM) + EPS)
    return x * r * g2


def _attn_prompt_kernel(sink_ref, qkv_ref, tab_ref, qn_ref, kn_ref, o_ref, ko_ref, vo_ref,
                        kband_ref, vband_ref, q_ref, p_ref, rhs_ref):
    n = pl.program_id(1)
    T = B_BLOCK

    @pl.when(n == 0)
    def _():
        kband_ref[0:T, :] = jnp.zeros((T, B_KV_DIM), f32)
        vband_ref[0:T, :] = jnp.zeros((T, B_KV_DIM), f32)
        rhs_ref[...] = jnp.ones(rhs_ref.shape, bf16)

    lo = lax.broadcasted_iota(jnp.int32, (1, LANES), 1) < B_HEAD_DIM
    qn2 = qn_ref[...]
    kn2 = kn_ref[...]
    for t in range(B_KV_DIM // LANES):
        lanes = slice(t * LANES, (t + 1) * LANES)
        k2 = _rms_head_pairs(qkv_ref[:, B_Q_DIM + t * LANES:B_Q_DIM + (t + 1) * LANES], kn2, lo)
        ko_ref[0, :, lanes] = k2
        kband_ref[T:, lanes] = k2
    v = qkv_ref[:, B_Q_DIM + B_KV_DIM:]
    vo_ref[0] = v
    vband_ref[T:, :] = v
    for t in range(B_Q_DIM // LANES):
        q_ref[t * T:(t + 1) * T, :] = _rms_head_pairs(qkv_ref[:, t * LANES:(t + 1) * LANES], qn2, lo).astype(bf16)

    key_ops = {}
    band_row = lax.broadcasted_iota(jnp.int32, (2 * T, LANES), 0)
    for t in range(B_KV_DIM // LANES):
        lanes = slice(t * LANES, (t + 1) * LANES)
        kt = kband_ref[:, lanes]
        kr = pltpu.roll(kt, B_HEAD_DIM, axis=1)
        vt = jnp.where(band_row == 0, 0.0, vband_ref[:, lanes])
        vr = pltpu.roll(vt, B_HEAD_DIM, axis=1)
        hi = jnp.logical_not(lo)
        for half, (ksrc, vsrc) in enumerate(((kt, vt), (kr, vr))):
            g_lo, g_hi = (2 * t, 2 * t + 1) if half == 0 else (2 * t + 1, 2 * t)
            key_ops[(g_lo, 0)] = jnp.where(lo, ksrc, 0.0).astype(bf16)
            key_ops[(g_hi, 1)] = jnp.where(hi, ksrc, 0.0).astype(bf16)
            rhs_ref[g_lo * 2 + 0, :, 0:LANES] = jnp.where(lo, vsrc, 1.0).astype(bf16)
            rhs_ref[g_hi * 2 + 1, :, 0:LANES] = jnp.where(hi, vsrc, 1.0).astype(bf16)

    i = lax.broadcasted_iota(jnp.int32, (T, 2 * T), 0)
    j = lax.broadcasted_iota(jnp.int32, (T, 2 * T), 1)
    first_key = jnp.where(n == 0, T, 0)
    valid = (j > i) & (j <= i + B_WINDOW) & (j >= first_key)
    sink_col = j == 0
    lo_t = lax.broadcasted_iota(jnp.int32, (T, LANES), 1) < B_HEAD_DIM
    for g in range(B_KV_HEADS):
        q2 = q_ref[2 * g * T:(2 * g + 2) * T, :]
        res = []
        for half in range(2):
            logits = _dot_nt(q2, key_ops[(g, half)]) * (B_HEAD_DIM ** -0.5)
            for pair in range(2):
                h = g * B_REP + 2 * pair + half
                l = jnp.where(valid, logits[pair * T:(pair + 1) * T] + tab_ref[h], NEG_INF)
                l = jnp.where(sink_col, sink_ref[h], l)
                p = jnp.exp(l - jnp.max(l, axis=-1, keepdims=True))
                p_ref[half, pair * T:(pair + 1) * T, :] = p.astype(bf16)
            res.append(_dot(p_ref[half], rhs_ref[g * 2 + half]))
        for pair in range(2):
            rows = slice(pair * T, (pair + 1) * T)
            even = res[0][rows, 0:LANES] / res[0][rows, LANES:]
            odd = res[1][rows, 0:LANES] / res[1][rows, LANES:]
            t = 2 * g + pair
            o_ref[:, t * LANES:(t + 1) * LANES] = jnp.where(lo_t, even, odd).astype(bf16)
    kband_ref[0:T, :] = kband_ref[T:, :]
    vband_ref[0:T, :] = vband_ref[T:, :]


def attn_prompt(qkv, table, q_norm, k_norm, sinks):
    nb = SEQ // B_BLOCK
    return pl.pallas_call(
        _attn_prompt_kernel,
        grid=(BATCH, nb),
        in_specs=[
            pl.BlockSpec(memory_space=pltpu.SMEM),
            pl.BlockSpec((B_BLOCK, B_Q_DIM + 2 * B_KV_DIM), lambda b, n: (b * nb + n, 0)),
            pl.BlockSpec((B_HEADS, B_BLOCK, 2 * B_BLOCK), lambda b, n: (0, 0, 0)),
            pl.BlockSpec((1, LANES), lambda b, n: (0, 0)),
            pl.BlockSpec((1, LANES), lambda b, n: (0, 0)),
        ],
        out_specs=[
            pl.BlockSpec((B_BLOCK, B_Q_DIM), lambda b, n: (b * nb + n, 0)),
            pl.BlockSpec((1, B_BLOCK, B_KV_DIM), lambda b, n: (b, 0, 0)),
            pl.BlockSpec((1, B_BLOCK, B_KV_DIM), lambda b, n: (b, 0, 0)),
        ],
        out_shape=[
            jax.ShapeDtypeStruct((PROMPT_ROWS, B_Q_DIM), bf16),
            jax.ShapeDtypeStruct((BATCH, B_BLOCK, B_KV_DIM), f32),
            jax.ShapeDtypeStruct((BATCH, B_BLOCK, B_KV_DIM), f32),
        ],
        scratch_shapes=[
            pltpu.VMEM((2 * B_BLOCK, B_KV_DIM), f32),
            pltpu.VMEM((2 * B_BLOCK, B_KV_DIM), f32),
            pltpu.VMEM((B_Q_DIM // LANES * B_BLOCK, LANES), bf16),
            pltpu.VMEM((2, 2 * B_BLOCK, 2 * B_BLOCK), bf16),
            pltpu.VMEM((2 * B_KV_HEADS, 2 * B_BLOCK, 2 * LANES), bf16),
        ],
        compiler_params=_params("parallel", "arbitrary"),
        name="attn_prompt",
    )(sinks, qkv, table, jnp.tile(q_norm, (1, LANES // B_HEAD_DIM)), jnp.tile(k_norm, (1, LANES // B_HEAD_DIM)))


ATT_BB = 8
ATT_QR = B_REP * DEC_SEQ
ATT_KEYS = B_WINDOW + 2 * DEC_SEQ


def _attn_sample_kernel(sink_ref, q_ref, kn_ref, vn_ref, kc_ref, vc_ref, tab_ref, qn_ref, knm_ref,
                        o_ref, ko_ref, vo_ref, kall_ref, vall_ref):
    qn = qn_ref[...]
    knm = knm_ref[...]
    row = lax.broadcasted_iota(jnp.int32, (ATT_QR, ATT_KEYS), 0)
    j = lax.broadcasted_iota(jnp.int32, (ATT_QR, ATT_KEYS), 1)
    t = row % DEC_SEQ
    valid = (j > t) & (j <= t + B_WINDOW)
    r_col = lax.broadcasted_iota(jnp.int32, (ATT_QR, 1), 0) // DEC_SEQ
    pad = jnp.zeros((ATT_KEYS - B_WINDOW - DEC_SEQ, B_KV_DIM), f32)
    kall_ref[B_WINDOW + DEC_SEQ:, :] = pad
    vall_ref[B_WINDOW + DEC_SEQ:, :] = pad
    for s in range(ATT_BB):
        kc = kc_ref[s]
        vc = vc_ref[s]
        k_new = jnp.concatenate(
            [_rms(kn_ref[s, :, g * B_HEAD_DIM:(g + 1) * B_HEAD_DIM], knm) for g in range(B_KV_HEADS)], axis=1)
        v_new = vn_ref[s]
        ko_ref[s, 0:B_WINDOW - DEC_SEQ, :] = kc[DEC_SEQ:, :]
        ko_ref[s, B_WINDOW - DEC_SEQ:, :] = k_new
        vo_ref[s, 0:B_WINDOW - DEC_SEQ, :] = vc[DEC_SEQ:, :]
        vo_ref[s, B_WINDOW - DEC_SEQ:, :] = v_new
        kall_ref[0:B_WINDOW, :] = kc
        kall_ref[B_WINDOW:B_WINDOW + DEC_SEQ, :] = k_new
        vall_ref[0:B_WINDOW, :] = vc
        vall_ref[B_WINDOW:B_WINDOW + DEC_SEQ, :] = v_new
        q = q_ref[s]
        for g in range(B_KV_HEADS):
            gd = slice(g * B_HEAD_DIM, (g + 1) * B_HEAD_DIM)
            qg = _rms(q[:, gd], qn).astype(bf16)
            keys = kall_ref[:, gd].astype(bf16)
            vals = vall_ref[:, gd].astype(bf16)
            logits = _dot_nt(qg, keys) * (B_HEAD_DIM ** -0.5)
            logits = jnp.where(valid, logits + tab_ref[g], NEG_INF)
            sink = jnp.zeros((ATT_QR, 1), f32)
            for r in range(B_REP):
                sink = jnp.where(r_col == r, sink_ref[g * B_REP + r], sink)
            p, denom = _softmax_with_sink(logits, sink)
            o_ref[s, :, gd] = _dot(p.astype(bf16), vals) / denom


def attn_sample(q_s, k_new, v_new, k_cache, v_cache, table_s, q_norm, k_norm, sinks):
    blk = lambda *shape: pl.BlockSpec((ATT_BB,) + shape, lambda i: (i,) + (0,) * len(shape))
    full = lambda *shape: pl.BlockSpec(shape, lambda i: (0,) * len(shape))
    return pl.pallas_call(
        _attn_sample_kernel,
        grid=(DEC_BATCH // ATT_BB,),
        in_specs=[
            pl.BlockSpec(memory_space=pltpu.SMEM),
            blk(ATT_QR, B_KV_DIM), blk(DEC_SEQ, B_KV_DIM), blk(DEC_SEQ, B_KV_DIM),
            blk(B_WINDOW, B_KV_DIM), blk(B_WINDOW, B_KV_DIM),
            full(B_KV_HEADS, ATT_QR, ATT_KEYS), full(1, B_HEAD_DIM), full(1, B_HEAD_DIM),
        ],
        out_specs=[blk(ATT_QR, B_KV_DIM), blk(B_WINDOW, B_KV_DIM), blk(B_WINDOW, B_KV_DIM)],
        out_shape=[
            jax.ShapeDtypeStruct((DEC_BATCH, ATT_QR, B_KV_DIM), f32),
            jax.ShapeDtypeStruct((DEC_BATCH, B_WINDOW, B_KV_DIM), f32),
            jax.ShapeDtypeStruct((DEC_BATCH, B_WINDOW, B_KV_DIM), f32),
        ],
        scratch_shapes=[pltpu.VMEM((ATT_KEYS, B_KV_DIM), f32), pltpu.VMEM((ATT_KEYS, B_KV_DIM), f32)],
        compiler_params=_params("parallel"),
        name="attn_sample",
    )(sinks, q_s, k_new, v_new, k_cache, v_cache, table_s, q_norm, k_norm)


CONV_PAD = SUBLANES


def _gated_group_norm(y, z, norm_w):
    gt = y * _silu(z)
    parts = []
    for g in range(C_GROUPS):
        gg = gt[:, g * C_GROUP_W:(g + 1) * C_GROUP_W]
        parts.append(gg * lax.rsqrt(jnp.mean(gg * gg, axis=-1, keepdims=True) + EPS))
    return jnp.concatenate(parts, axis=1) * norm_w


LOG2E = math.log2(math.e)


def _expand_heads(v, sel3):
    lane = lax.broadcasted_iota(jnp.int32, (1, LANES), 1)
    v = jnp.where(lane < C_HEADS, v, 0.0)
    hi = v.astype(bf16).astype(f32)
    r1 = v - hi
    mid = r1.astype(bf16).astype(f32)
    lo = r1 - mid
    packed = hi + pltpu.roll(mid, C_HEADS, axis=1) + pltpu.roll(lo, 2 * C_HEADS, axis=1)
    return _dot(packed.astype(bf16), sel3)


def _ssd_prompt_kernel(zx_ref, dtr_ref, cw_ref, cb_ref, dtb_ref, alog_ref, dsk_ref, nw_ref, sel_ref,
                       yn_ref, hfin_ref, cout_ref, xpad_ref, ht_ref, y_ref):
    c = pl.program_id(1)
    T = C_CHUNK

    @pl.when(c == 0)
    def _():
        xpad_ref[0:CONV_PAD, :] = jnp.zeros((CONV_PAD, C_CONV_DIM), f32)
        ht_ref[...] = jnp.zeros(ht_ref.shape, f32)

    xbc = zx_ref[:, C_D_INNER:]
    xpad_ref[CONV_PAD:, :] = xbc
    xp = xpad_ref[...]
    cw = cw_ref[...]
    acc = cb_ref[...]
    for tap in range(C_D_CONV - 1):
        shifted = pltpu.roll(xp, C_D_CONV - 1 - tap, axis=0)[CONV_PAD:, :]
        acc = acc + shifted * cw[tap:tap + 1, :]
    acc = acc + xbc * cw[C_D_CONV - 1:C_D_CONV, :]
    xpad_ref[0:CONV_PAD, :] = xbc[T - CONV_PAD:, :]
    cout_ref[0] = xbc[T - (C_D_CONV - 1):, :]
    act = _silu(acc)
    xs = act[:, :C_D_INNER]
    bm = act[:, C_D_INNER:C_D_INNER + C_BC_DIM]
    cm = act[:, C_D_INNER + C_BC_DIM:]
    xb = xs.astype(bf16)

    dt = _softplus(dtr_ref[...] + dtb_ref[...])
    a_neg = -jnp.exp(alog_ref[...])
    row = lax.broadcasted_iota(jnp.int32, (T, T), 0)
    col = lax.broadcasted_iota(jnp.int32, (T, T), 1)
    causal = row >= col
    acs = _dot_exact_lhs01(causal.astype(f32), dt * a_neg)
    a2 = acs * LOG2E
    sel3 = sel_ref[...]
    e_exp = jnp.exp2(_expand_heads(a2, sel3))
    w_exp = _expand_heads(jnp.exp(acs[T - 1:T, :] - acs) * dt, sel3)
    cdec = e_exp[T - 1:T, :]
    b2_t = a2.T - jnp.log2(dt.T)
    xw = (xs * w_exp).astype(bf16)
    hb = ht_ref[...].astype(bf16)
    dsk = dsk_ref[...]
    lo_t = lax.broadcasted_iota(jnp.int32, (T, LANES), 1) < C_HEAD_DIM

    for g in range(C_GROUPS):
        ns = slice(g * C_D_STATE, (g + 1) * C_D_STATE)
        gs = slice(g * C_GROUP_W, (g + 1) * C_GROUP_W)
        b_g = bm[:, ns]
        c_g = cm[:, ns].astype(bf16)
        cb = _dot_nt(c_g, b_g.astype(bf16))
        yi = _dot(c_g, hb[:, gs])
        ht_ref[:, gs] = ht_ref[:, gs] * cdec[:, gs] + _dot(b_g.T.astype(bf16), xw[:, gs])
        for tt in range(C_GROUP_W // LANES):
            t = g * (C_GROUP_W // LANES) + tt
            lanes = slice(t * LANES, (t + 1) * LANES)
            xt = xb[:, lanes]
            res = []
            for half in range(2):
                h = 2 * t + half
                a_col = jnp.broadcast_to(a2[:, h:h + 1], (T, T))
                b_row = jnp.broadcast_to(b2_t[h:h + 1, :], (T, T))
                w = jnp.where(causal, cb * jnp.exp2(a_col - b_row), 0.0)
                res.append(_dot(w.astype(bf16), xt))
            y_intra = jnp.where(lo_t, res[0], res[1])
            y_ref[:, lanes] = y_intra + e_exp[:, lanes] * yi[:, tt * LANES:(tt + 1) * LANES] + dsk[:, lanes] * xs[:, lanes]

    yn_ref[...] = _gated_group_norm(y_ref[...], zx_ref[:, :C_D_INNER], nw_ref[...]).astype(bf16)

    @pl.when(c == pl.num_programs(1) - 1)
    def _():
        for t in range(C_D_INNER // LANES):
            hfin_ref[0, t * LANES:(t + 1) * LANES, :] = ht_ref[:, t * LANES:(t + 1) * LANES].T


def _head_select3():
    k = np.arange(LANES)[:, None]
    ch = np.arange(C_D_INNER)[None, :] // C_HEAD_DIM
    return jnp.asarray((k % C_HEADS == ch) & (k < 3 * C_HEADS), dtype=bf16)


def ssd_prompt(zx, dtr, conv_w, conv_b, dt_bias, a_log, d_skip, norm_w):
    nc = SEQ // C_CHUNK
    full = lambda *shape: pl.BlockSpec(shape, lambda b, c: (0,) * len(shape))
    return pl.pallas_call(
        _ssd_prompt_kernel,
        grid=(BATCH, nc),
        in_specs=[
            pl.BlockSpec((C_CHUNK, C_D_INNER + C_CONV_DIM), lambda b, c: (b * nc + c, 0)),
            pl.BlockSpec((C_CHUNK, LANES), lambda b, c: (b * nc + c, 0)),
            full(C_D_CONV, C_CONV_DIM), full(1, C_CONV_DIM), full(1, LANES), full(1, LANES),
            full(1, C_D_INNER), full(1, C_D_INNER), full(LANES, C_D_INNER),
        ],
        out_specs=[
            pl.BlockSpec((C_CHUNK, C_D_INNER), lambda b, c: (b * nc + c, 0)),
            pl.BlockSpec((1, C_D_INNER, C_D_STATE), lambda b, c: (b, 0, 0)),
            pl.BlockSpec((1, C_D_CONV - 1, C_CONV_DIM), lambda b, c: (b, 0, 0)),
        ],
        out_shape=[
            jax.ShapeDtypeStruct((PROMPT_ROWS, C_D_INNER), bf16),
            jax.ShapeDtypeStruct((BATCH, C_D_INNER, C_D_STATE), f32),
            jax.ShapeDtypeStruct((BATCH, C_D_CONV - 1, C_CONV_DIM), f32),
        ],
        scratch_shapes=[
            pltpu.VMEM((CONV_PAD + C_CHUNK, C_CONV_DIM), f32),
            pltpu.VMEM((C_D_STATE, C_D_INNER), f32),
            pltpu.VMEM((C_CHUNK, C_D_INNER), f32),
        ],
        compiler_params=_params("parallel", "arbitrary"),
        name="ssd_prompt",
    )(zx, dtr, conv_w, conv_b, dt_bias, a_log, d_skip, norm_w, _head_select3())


SSD_BB = 8
SSD_TP = SUBLANES
_N_PAIRS = DEC_SEQ * (DEC_SEQ + 1) // 2
_N_COEF = _N_PAIRS + 2 * DEC_SEQ


def _ssd_sample_kernel(zx_ref, dtr_ref, cs_ref, h0_ref, cw_ref, cb_ref, dtb_ref, alog_ref, dsk_ref, nw_ref,
                       sel_ref, yn_ref, hn_ref, cout_ref, c_scr, b_scr, xw_scr, yi_scr, cd_scr):
    L = DEC_SEQ
    cw = cw_ref[...]
    xp = [cs_ref[k] for k in range(C_D_CONV - 1)] + [zx_ref[t, :, C_D_INNER:] for t in range(L)]
    for k in range(C_D_CONV - 1):
        cout_ref[k] = xp[L + k]
    act = []
    for t in range(L):
        acc = cb_ref[...]
        for tap in range(C_D_CONV):
            acc = acc + xp[t + tap] * cw[tap:tap + 1, :]
        act.append(_silu(acc))
    xs = [a[:, :C_D_INNER] for a in act]
    bm = [a[:, C_D_INNER:C_D_INNER + C_BC_DIM] for a in act]
    cm = [a[:, C_D_INNER + C_BC_DIM:] for a in act]

    a_neg = -jnp.exp(alog_ref[...])
    dt = [_softplus(dtr_ref[t] + dtb_ref[...]) for t in range(L)]
    acs = []
    for t in range(L):
        acs.append(dt[t] * a_neg if t == 0 else acs[t - 1] + dt[t] * a_neg)

    lane_group = lax.broadcasted_iota(jnp.int32, (SSD_BB, LANES), 1) // C_REP
    coefs = []
    for t in range(L):
        for t2 in range(t + 1):
            cbh = jnp.zeros((SSD_BB, LANES), f32)
            for g in range(C_GROUPS):
                ns = slice(g * C_D_STATE, (g + 1) * C_D_STATE)
                cbg = jnp.sum(cm[t][:, ns] * bm[t2][:, ns], axis=-1, keepdims=True)
                cbh = jnp.where(lane_group == g, cbg, cbh)
            coefs.append(cbh * jnp.exp(acs[t] - acs[t2]) * dt[t2])
    for t in range(L):
        coefs.append(jnp.exp(acs[t]))
    for t in range(L):
        coefs.append(jnp.exp(acs[L - 1] - acs[t]) * dt[t])
    coef = jnp.concatenate(coefs, axis=0)
    cexp = _dot_exact_rhs01(coef, sel_ref[...])
    cexp = [cexp[k * SSD_BB:(k + 1) * SSD_BB, :] for k in range(_N_COEF)]
    w_intra = cexp[:_N_PAIRS]
    w_inter = cexp[_N_PAIRS:_N_PAIRS + L]
    w_state = cexp[_N_PAIRS + L:]

    cd = jnp.concatenate([jnp.exp(acs[L - 1]), jnp.zeros((LANES - SSD_BB, LANES), f32)], axis=0)
    cd_t = cd.T
    for s in range(SSD_BB):
        cd_scr[s] = jnp.broadcast_to(cd_t[0:C_HEADS, s:s + 1], (C_HEADS, C_D_STATE))

    zeros_tail = jnp.zeros((SSD_BB, SSD_TP - L, C_D_INNER), f32)
    c_scr[:, L:, :] = zeros_tail[:, :, :C_BC_DIM]
    b_scr[:, L:, :] = zeros_tail[:, :, :C_BC_DIM]
    xw_scr[:, L:, :] = zeros_tail
    for t in range(L):
        xw_t = xs[t] * w_state[t]
        for s in range(SSD_BB):
            c_scr[s, t:t + 1, :] = cm[t][s:s + 1, :]
            b_scr[s, t:t + 1, :] = bm[t][s:s + 1, :]
            xw_scr[s, t:t + 1, :] = xw_t[s:s + 1, :]

    for s in range(SSD_BB):
        for g in range(C_GROUPS):
            ns = slice(g * C_D_STATE, (g + 1) * C_D_STATE)
            gs = slice(g * C_GROUP_W, (g + 1) * C_GROUP_W)
            h0 = h0_ref[s, gs, :]
            yi = _dot_nt(c_scr[s, :, ns].astype(bf16), h0.astype(bf16))
            for t in range(L):
                yi_scr[t, s:s + 1, gs] = yi[t:t + 1, :]
            st = _dot_tn(xw_scr[s, :, gs].astype(bf16), b_scr[s, :, ns].astype(bf16))
            for r in range(C_REP):
                h = g * C_REP + r
                rs = slice(r * C_HEAD_DIM, (r + 1) * C_HEAD_DIM)
                scale = cd_scr[s, h:h + 1, :]
                hn_ref[s, h * C_HEAD_DIM:(h + 1) * C_HEAD_DIM, :] = h0[rs, :] * scale + st[rs, :]

    dsk = dsk_ref[...]
    nw = nw_ref[...]
    pair = 0
    for t in range(L):
        y = w_inter[t] * yi_scr[t] + dsk * xs[t]
        for t2 in range(t + 1):
            y = y + w_intra[pair] * xs[t2]
            pair += 1
        yn_ref[t] = _gated_group_norm(y, zx_ref[t, :, :C_D_INNER], nw).astype(bf16)


def ssd_sample(zx_t, dtr_t, conv_state_t, h0, conv_w, conv_b, dt_bias, a_log, d_skip, norm_w, sel):
    tmaj = lambda n, w: pl.BlockSpec((n, SSD_BB, w), lambda i: (0, i, 0))
    full = lambda *shape: pl.BlockSpec(shape, lambda i: (0,) * len(shape))
    return pl.pallas_call(
        _ssd_sample_kernel,
        grid=(DEC_BATCH // SSD_BB,),
        in_specs=[
            tmaj(DEC_SEQ, C_D_INNER + C_CONV_DIM), tmaj(DEC_SEQ, LANES), tmaj(C_D_CONV - 1, C_CONV_DIM),
            pl.BlockSpec((SSD_BB, C_D_INNER, C_D_STATE), lambda i: (i, 0, 0)),
            full(C_D_CONV, C_CONV_DIM), full(1, C_CONV_DIM), full(1, LANES), full(1, LANES),
            full(1, C_D_INNER), full(1, C_D_INNER), full(LANES, C_D_INNER),
        ],
        out_specs=[
            tmaj(DEC_SEQ, C_D_INNER),
            pl.BlockSpec((SSD_BB, C_D_INNER, C_D_STATE), lambda i: (i, 0, 0)),
            tmaj(C_D_CONV - 1, C_CONV_DIM),
        ],
        out_shape=[
            jax.ShapeDtypeStruct((DEC_SEQ, DEC_BATCH, C_D_INNER), bf16),
            jax.ShapeDtypeStruct((DEC_BATCH, C_D_INNER, C_D_STATE), f32),
            jax.ShapeDtypeStruct((C_D_CONV - 1, DEC_BATCH, C_CONV_DIM), f32),
        ],
        scratch_shapes=[
            pltpu.VMEM((SSD_BB, SSD_TP, C_BC_DIM), f32),
            pltpu.VMEM((SSD_BB, SSD_TP, C_BC_DIM), f32),
            pltpu.VMEM((SSD_BB, SSD_TP, C_D_INNER), f32),
            pltpu.VMEM((DEC_SEQ, SSD_BB, C_D_INNER), f32),
            pltpu.VMEM((SSD_BB, C_HEADS, C_D_STATE), f32),
        ],
        compiler_params=_params("parallel"),
        name="ssd_sample",
    )(zx_t, dtr_t, conv_state_t, h0, conv_w, conv_b, dt_bias, a_log, d_skip, norm_w, sel)


def _pad_lanes(v):
    return jnp.pad(v.astype(f32), (0, LANES - v.shape[0])).reshape(1, LANES)


def _mixer_a(xp, xs, g, w_in, norm_v, w_sp, b_sp, w_out):
    w_in = w_in.astype(bf16)
    w_out = w_out.astype(bf16)
    nv = norm_v.reshape(1, A_HALF)
    xp = mixer_a(xp, g, w_in, nv, w_sp, b_sp.T, w_out, tm=TM_PROMPT, sample=False)
    xs, v_s = mixer_a(xs, g, w_in, nv, w_sp[:, :DEC_SEQ, :DEC_SEQ].reshape(-1), b_sp[:, :DEC_SEQ].reshape(-1),
                      w_out, tm=TM_SAMPLE, sample=True)
    return xp, xs, jnp.swapaxes(v_s.reshape(DEC_SEQ, DEC_BATCH, A_HALF), 0, 1)


def _mixer_b(xp, xs, g, k_cache, v_cache, w_qkv, q_norm, k_norm, sinks, rel_bias, w_out):
    w_qkv = w_qkv.astype(bf16)
    w_out = w_out.astype(bf16)
    qn = q_norm.reshape(1, B_HEAD_DIM)
    kn = k_norm.reshape(1, B_HEAD_DIM)
    table = bias_table(rel_bias)

    qkv_p = norm_matmul(xp, g, w_qkv, tm=TM_PROMPT, tn=512)
    o_p, k_p, v_p = attn_prompt(qkv_p, table, qn, kn, sinks)
    xp = matmul_res(o_p, w_out, xp, tm=TM_PROMPT)

    qkv_s = norm_matmul(xs, g, w_qkv, tm=TM_SAMPLE, tn=512).reshape(DEC_SEQ, DEC_BATCH, -1)
    q_s = qkv_s[:, :, :B_Q_DIM].reshape(DEC_SEQ, DEC_BATCH, B_KV_HEADS, B_REP, B_HEAD_DIM)
    q_s = q_s.transpose(1, 3, 0, 2, 4).reshape(DEC_BATCH, ATT_QR, B_KV_DIM)
    k_new = jnp.swapaxes(qkv_s[:, :, B_Q_DIM:B_Q_DIM + B_KV_DIM], 0, 1)
    v_new = jnp.swapaxes(qkv_s[:, :, B_Q_DIM + B_KV_DIM:], 0, 1)
    table_s = table[:, :DEC_SEQ, :ATT_KEYS].reshape(B_KV_HEADS, ATT_QR, ATT_KEYS)
    o_s, k_s, v_s = attn_sample(q_s, k_new, v_new,
                                k_cache.reshape(DEC_BATCH, B_WINDOW, B_KV_DIM),
                                v_cache.reshape(DEC_BATCH, B_WINDOW, B_KV_DIM),
                                table_s, qn, kn, sinks)
    o_s = o_s.reshape(DEC_BATCH, B_REP, DEC_SEQ, B_KV_HEADS, B_HEAD_DIM).transpose(2, 0, 3, 1, 4)
    xs = matmul_res(o_s.reshape(SAMPLE_ROWS, B_Q_DIM).astype(bf16), w_out, xs, tm=TM_SAMPLE)
    kv_shape_p = (BATCH, B_WINDOW, B_KV_HEADS, B_HEAD_DIM)
    kv_shape_s = (DEC_BATCH, B_WINDOW, B_KV_HEADS, B_HEAD_DIM)
    return xp, xs, k_p.reshape(kv_shape_p), v_p.reshape(kv_shape_p), k_s.reshape(kv_shape_s), v_s.reshape(kv_shape_s)


def _mixer_c(xp, xs, g, h0, conv_state, w_in, conv_w, conv_b, dt_bias, a_log, d_skip, norm_w, w_out):
    w_zx = w_in[:, :C_D_INNER + C_CONV_DIM].astype(bf16)
    w_dt = jnp.pad(w_in[:, C_D_INNER + C_CONV_DIM:], ((0, 0), (0, LANES - C_HEADS))).astype(bf16)
    w_out = w_out.astype(bf16)
    cb = conv_b.reshape(1, C_CONV_DIM)
    dtb = _pad_lanes(dt_bias)
    alog = _pad_lanes(a_log)
    dsk = jnp.repeat(d_skip.astype(f32), C_HEAD_DIM).reshape(1, C_D_INNER)
    nw = norm_w.reshape(1, C_D_INNER)

    zx_p = norm_matmul(xp, g, w_zx, tm=TM_PROMPT, tn=1024)
    dtr_p = norm_matmul(xp, g, w_dt, tm=TM_PROMPT, tn=LANES)
    yn_p, h_p, conv_p = ssd_prompt(zx_p, dtr_p, conv_w, cb, dtb, alog, dsk, nw)
    xp = matmul_res(yn_p, w_out, xp, tm=TM_PROMPT)

    zx_s = norm_matmul(xs, g, w_zx, tm=TM_SAMPLE, tn=1024)
    dtr_s = norm_matmul(xs, g, w_dt, tm=TM_SAMPLE, tn=LANES)
    sel = (jnp.arange(LANES)[:, None] == jnp.arange(C_D_INNER)[None, :] // C_HEAD_DIM).astype(f32)
    yn_s, h_s, conv_s = ssd_sample(
        zx_s.reshape(DEC_SEQ, DEC_BATCH, -1), dtr_s.reshape(DEC_SEQ, DEC_BATCH, LANES),
        jnp.swapaxes(conv_state, 0, 1), h0.reshape(DEC_BATCH, C_D_INNER, C_D_STATE),
        conv_w, cb, dtb, alog, dsk, nw, sel)
    xs = matmul_res(yn_s.reshape(SAMPLE_ROWS, C_D_INNER), w_out, xs, tm=TM_SAMPLE)
    st_shape = (C_HEADS, C_HEAD_DIM, C_D_STATE)
    return (xp, xs, h_p.reshape((BATCH,) + st_shape), conv_p,
            h_s.reshape((DEC_BATCH,) + st_shape), jnp.swapaxes(conv_s, 0, 1))


def kernel(x_prompt, x_sample, cache_swa_k, cache_swa_v, state_ssm, state_conv, norm_mixer, norm_mlp, mlp_w_up, mlp_w_down, a_w_in, a_norm_v, a_w_spatial, a_b_spatial, a_w_out, b_w_qkv, b_q_norm, b_k_norm, b_sinks, rel_bias, b_w_out, c_w_in, c_conv_w, c_conv_b, c_dt_bias, c_a_log, c_d, c_norm, c_w_out):
    xp = x_prompt.reshape(PROMPT_ROWS, D_MODEL)
    xs = jnp.swapaxes(x_sample, 0, 1).reshape(SAMPLE_ROWS, D_MODEL)
    chunk_v_s = []
    swa_kp, swa_vp, swa_ks, swa_vs = [], [], [], []
    ssm_p, conv_p, ssm_s, conv_s = [], [], [], []
    for i in range(DEPTH):
        kind = i % N_MIXERS
        j = i // N_MIXERS
        g = norm_mixer[i].reshape(1, D_MODEL)
        if kind == 0:
            xp, xs, v_new = _mixer_a(xp, xs, g, a_w_in[j], a_norm_v[j], a_w_spatial[j], a_b_spatial[j], a_w_out[j])
            chunk_v_s.append(v_new)
        elif kind == 1:
            xp, xs, kp, vp, ks_, vs_ = _mixer_b(xp, xs, g, cache_swa_k[j], cache_swa_v[j], b_w_qkv[j], b_q_norm[j],
                                                b_k_norm[j], b_sinks[j], rel_bias, b_w_out[j])
            swa_kp.append(kp); swa_vp.append(vp); swa_ks.append(ks_); swa_vs.append(vs_)
        else:
            xp, xs, hp, bp, hs, bs = _mixer_c(xp, xs, g, state_ssm[j], state_conv[j], c_w_in[j], c_conv_w[j],
                                              c_conv_b[j], c_dt_bias[j], c_a_log[j], c_d[j], c_norm[j], c_w_out[j])
            ssm_p.append(hp); conv_p.append(bp); ssm_s.append(hs); conv_s.append(bs)
        gm = norm_mlp[i].reshape(1, D_MODEL)
        w_up = mlp_w_up[i].astype(bf16)
        w_down = mlp_w_down[i].astype(bf16)
        xp = mlp(xp, gm, w_up, w_down, tm=TM_PROMPT, tf=1024)
        xs = mlp(xs, gm, w_up, w_down, tm=TM_SAMPLE, tf=1024)
    y_prompt = xp.reshape(BATCH, SEQ, D_MODEL)
    y_sample = jnp.swapaxes(xs.reshape(DEC_SEQ, DEC_BATCH, D_MODEL), 0, 1)
    return (y_prompt, y_sample, jnp.stack(chunk_v_s),
            jnp.stack(swa_kp), jnp.stack(swa_vp), jnp.stack(swa_ks), jnp.stack(swa_vs),
            jnp.stack(ssm_p), jnp.stack(conv_p), jnp.stack(ssm_s), jnp.stack(conv_s))
```

```python
import functools
import math

import jax
import jax.numpy as jnp
import numpy as np
from jax import lax
from jax.experimental import pallas as pl
from jax.experimental.pallas import tpu as pltpu

f32 = jnp.float32
bf16 = jnp.bfloat16

D_MODEL = 1024
BATCH = 4
SEQ = 4096
DEPTH = 4
DEC_BATCH = 128
DEC_SEQ = 4
PAST_LEN = 8192
N_MIXERS = 3
D_FF = 4 * D_MODEL
EPS = 1e-6
NEG_INF = -1e30

A_CHUNK = 128
A_D_FFN = 6 * D_MODEL
A_HALF = A_D_FFN // 2
A_GROUPS = 8
A_GROUP_W = A_HALF // A_GROUPS

B_HEADS = 16
B_KV_HEADS = 4
B_HEAD_DIM = 64
B_REP = B_HEADS // B_KV_HEADS
B_WINDOW = 128
B_BLOCK = 128
B_Q_DIM = B_HEADS * B_HEAD_DIM
B_KV_DIM = B_KV_HEADS * B_HEAD_DIM
N_BUCKETS = 32
MAX_DISTANCE = 128

C_D_INNER = 2 * D_MODEL
C_HEAD_DIM = 64
C_HEADS = C_D_INNER // C_HEAD_DIM
C_GROUPS = 4
C_REP = C_HEADS // C_GROUPS
C_D_STATE = 128
C_D_CONV = 4
C_BC_DIM = C_GROUPS * C_D_STATE
C_CONV_DIM = C_D_INNER + 2 * C_BC_DIM
C_GROUP_W = C_D_INNER // C_GROUPS
C_CHUNK = 128

LANES = 128
SUBLANES = 8
VMEM_LIMIT_BYTES = 56 * 1024 * 1024

PROMPT_ROWS = BATCH * SEQ
SAMPLE_ROWS = DEC_BATCH * DEC_SEQ
TM_PROMPT = 1024
TM_SAMPLE = SAMPLE_ROWS
MLP_TF = 1024


def _params(*sem):
    return pltpu.CompilerParams(dimension_semantics=sem, vmem_limit_bytes=VMEM_LIMIT_BYTES)


def _rms(x, g):
    ms = jnp.mean(x * x, axis=-1, keepdims=True)
    return x * lax.rsqrt(ms + EPS) * g


def _gelu(x):
    return 0.5 * x * (1.0 + lax.erf(x * math.sqrt(0.5)))


def _silu(x):
    return x * jax.nn.sigmoid(x)


def _softplus(x):
    return jnp.maximum(x, 0.0) + jnp.log1p(jnp.exp(-jnp.abs(x)))


def _dot(a, b):
    return jnp.dot(a, b, preferred_element_type=f32)


def _dot_nt(a, b):
    return lax.dot_general(a, b, (((1,), (1,)), ((), ())), preferred_element_type=f32)


def _dot_tn(a, b):
    return lax.dot_general(a, b, (((0,), (0,)), ((), ())), preferred_element_type=f32)


def _dot_exact_lhs01(a01, x):
    a = a01.astype(bf16)
    hi = x.astype(bf16)
    r1 = x - hi.astype(f32)
    mid = r1.astype(bf16)
    lo = (r1 - mid.astype(f32)).astype(bf16)
    return _dot(a, hi) + _dot(a, mid) + _dot(a, lo)


def _dot_exact_rhs01(x, b01):
    b = b01.astype(bf16)
    hi = x.astype(bf16)
    r1 = x - hi.astype(f32)
    mid = r1.astype(bf16)
    lo = (r1 - mid.astype(f32)).astype(bf16)
    return _dot(hi, b) + _dot(mid, b) + _dot(lo, b)


def _norm_matmul_kernel(x_ref, g_ref, w_ref, o_ref, xn_ref, *, act):
    @pl.when(pl.program_id(1) == 0)
    def _():
        xn_ref[...] = _rms(x_ref[...], g_ref[...]).astype(bf16)

    y = _dot(xn_ref[...], w_ref[0].astype(bf16))
    if act == "gelu":
        y = _gelu(y)
    o_ref[...] = y


def norm_matmul(x, g, w, layer, n, *, tm, tn, act=None):
    m, k = x.shape
    return pl.pallas_call(
        functools.partial(_norm_matmul_kernel, act=act),
        grid=(m // tm, n // tn),
        in_specs=[
            pl.BlockSpec((tm, k), lambda i, j: (i, 0)),
            pl.BlockSpec((1, k), lambda i, j: (0, 0)),
            pl.BlockSpec((1, k, tn), lambda i, j: (layer, 0, j)),
        ],
        out_specs=pl.BlockSpec((tm, tn), lambda i, j: (i, j)),
        out_shape=jax.ShapeDtypeStruct((m, n), f32),
        scratch_shapes=[pltpu.VMEM((tm, k), bf16)],
        compiler_params=_params("parallel", "arbitrary"),
        name="norm_matmul",
    )(x, g, w)


def _mlp_kernel(x_ref, g_ref, wu_ref, wd_ref, o_ref, xn_ref):
    @pl.when(pl.program_id(1) == 0)
    def _():
        x = x_ref[...]
        xn_ref[...] = _rms(x, g_ref[...]).astype(bf16)
        o_ref[...] = x

    h = jnp.maximum(_dot(xn_ref[...], wu_ref[0].astype(bf16)), 0.0)
    o_ref[...] += _dot((h * h).astype(bf16), wd_ref[0].astype(bf16))


def mlp(x, g, w_up, w_down, layer, *, tm, tf):
    m, d = x.shape
    ff = w_up.shape[2]
    return pl.pallas_call(
        _mlp_kernel,
        grid=(m // tm, ff // tf),
        in_specs=[
            pl.BlockSpec((tm, d), lambda i, j: (i, 0)),
            pl.BlockSpec((1, d), lambda i, j: (0, 0)),
            pl.BlockSpec((1, d, tf), lambda i, j: (layer, 0, j)),
            pl.BlockSpec((1, tf, d), lambda i, j: (layer, j, 0)),
        ],
        out_specs=pl.BlockSpec((tm, d), lambda i, j: (i, 0)),
        out_shape=jax.ShapeDtypeStruct((m, d), f32),
        scratch_shapes=[pltpu.VMEM((tm, d), bf16)],
        compiler_params=_params("parallel", "arbitrary"),
        name="mlp",
    )(x, g, w_up, w_down)


def _matmul_res_kernel(a_ref, w_ref, r_ref, o_ref, wb_ref):
    @pl.when(pl.program_id(0) == 0)
    def _():
        wb_ref[...] = w_ref[0].astype(bf16)

    o_ref[...] = r_ref[...] + _dot(a_ref[...], wb_ref[...])


def matmul_res(a, w, layer, res, *, tm):
    m, k = a.shape
    n = w.shape[2]
    return pl.pallas_call(
        _matmul_res_kernel,
        grid=(m // tm,),
        in_specs=[
            pl.BlockSpec((tm, k), lambda i: (i, 0)),
            pl.BlockSpec((1, k, n), lambda i: (layer, 0, 0)),
            pl.BlockSpec((tm, n), lambda i: (i, 0)),
        ],
        out_specs=pl.BlockSpec((tm, n), lambda i: (i, 0)),
        out_shape=jax.ShapeDtypeStruct((m, n), f32),
        scratch_shapes=[pltpu.VMEM((k, n), bf16)],
        compiler_params=_params("arbitrary"),
        name="matmul_res",
    )(a, w, res)


A_BLK_GROUPS = 2
A_BLK = A_BLK_GROUPS * A_GROUP_W
A_NBLK = A_HALF // A_BLK


def _mixer_a_kernel(*refs, sample):
    if sample:
        ws_ref, bs_ref, x_ref, g_ref, win_ref, nv_ref, wout_ref, o_ref, vo_ref, xn_ref, v_ref, ssq_ref, us_ref = refs
    else:
        x_ref, g_ref, win_ref, nv_ref, ws_ref, bs_ref, wout_ref, o_ref, xn_ref, v_ref, ssq_ref, us_ref = refs
    j = pl.program_id(1)
    tm = x_ref.shape[0]

    @pl.when(j == 0)
    def _():
        xn_ref[...] = _rms(x_ref[...], g_ref[...]).astype(bf16)
        ssq_ref[...] = jnp.zeros(ssq_ref.shape, f32)

    for k in range(A_NBLK):
        @pl.when(j == k)
        def _(k=k):
            v = _gelu(_dot(xn_ref[...], win_ref[0].astype(bf16)))
            v_ref[:, k * A_BLK:(k + 1) * A_BLK] = v
            ssq_ref[...] += jnp.sum(v * v, axis=-1, keepdims=True)

    if not sample:
        row = lax.broadcasted_iota(jnp.int32, (A_CHUNK, A_CHUNK), 0)
        col = lax.broadcasted_iota(jnp.int32, (A_CHUNK, A_CHUNK), 1)
        causal = row >= col

    for k in range(A_NBLK):
        @pl.when(j == A_NBLK + k)
        def _(k=k):
            u = _gelu(_dot(xn_ref[...], win_ref[0].astype(bf16)))
            rinv = lax.rsqrt(ssq_ref[...] * (1.0 / A_HALF) + EPS)
            for gg in range(A_BLK_GROUPS):
                g = k * A_BLK_GROUPS + gg
                cols = slice(g * A_GROUP_W, (g + 1) * A_GROUP_W)
                ucols = slice(gg * A_GROUP_W, (gg + 1) * A_GROUP_W)
                vn = v_ref[:, cols] * rinv * nv_ref[:, cols]
                if sample:
                    vo_ref[:, cols] = vn
                    vt = [vn[t * DEC_BATCH:(t + 1) * DEC_BATCH] for t in range(DEC_SEQ)]
                    s_rows = []
                    for t in range(DEC_SEQ):
                        s = ws_ref[(g * DEC_SEQ + t) * DEC_SEQ] * vt[0]
                        for t2 in range(1, t + 1):
                            s = s + ws_ref[(g * DEC_SEQ + t) * DEC_SEQ + t2] * vt[t2]
                        s_rows.append(s + bs_ref[g * DEC_SEQ + t])
                    s = jnp.concatenate(s_rows, axis=0)
                else:
                    w = jnp.where(causal, ws_ref[g], 0.0).astype(bf16)
                    bias = bs_ref[:, g:g + 1]
                    vb = vn.astype(bf16)
                    s = jnp.concatenate(
                        [_dot(w, vb[c * A_CHUNK:(c + 1) * A_CHUNK]) + bias for c in range(tm // A_CHUNK)], axis=0)
                us_ref[:, ucols] = (u[:, ucols] * s).astype(bf16)
            y = _dot(us_ref[...], wout_ref[0].astype(bf16))
            if k == 0:
                o_ref[...] = x_ref[...] + y
            else:
                o_ref[...] += y


def mixer_a(x, g, w_in, norm_v, w_sp, b_sp, w_out, layer, *, tm, sample):
    m, d = x.shape
    nj = 2 * A_NBLK
    row = lambda w: pl.BlockSpec((tm, w), lambda i, j: (i, 0))
    full = lambda *shape: pl.BlockSpec(shape, lambda i, j: (0,) * len(shape))
    smem = pl.BlockSpec(memory_space=pltpu.SMEM)
    win_spec = pl.BlockSpec((1, d, A_BLK), lambda i, j: (layer, 0, (j + A_NBLK) % nj))
    wout_spec = pl.BlockSpec((1, A_BLK, d), lambda i, j: (layer, jnp.maximum(j - A_NBLK, 0), 0))
    if sample:
        in_specs = [smem, smem, row(d), full(1, d), win_spec, full(1, A_HALF), wout_spec]
        args = (w_sp, b_sp, x, g, w_in, norm_v, w_out)
        out_specs = [row(d), row(A_HALF)]
        out_shape = [jax.ShapeDtypeStruct((m, d), f32), jax.ShapeDtypeStruct((m, A_HALF), f32)]
    else:
        in_specs = [row(d), full(1, d), win_spec, full(1, A_HALF), full(A_GROUPS, A_CHUNK, A_CHUNK),
                    full(A_CHUNK, A_GROUPS), wout_spec]
        args = (x, g, w_in, norm_v, w_sp, b_sp, w_out)
        out_specs = row(d)
        out_shape = jax.ShapeDtypeStruct((m, d), f32)
    return pl.pallas_call(
        functools.partial(_mixer_a_kernel, sample=sample),
        grid=(m // tm, nj),
        in_specs=in_specs,
        out_specs=out_specs,
        out_shape=out_shape,
        scratch_shapes=[pltpu.VMEM((tm, d), bf16), pltpu.VMEM((tm, A_HALF), f32), pltpu.VMEM((tm, 1), f32),
                        pltpu.VMEM((tm, A_BLK), bf16)],
        compiler_params=_params("parallel", "arbitrary"),
        name="mixer_a_sample" if sample else "mixer_a_prompt",
    )(*args)


def _bucket_table():
    i = np.arange(B_BLOCK)[:, None]
    j = np.arange(2 * B_BLOCK)[None, :]
    n = np.maximum(B_BLOCK + i - j, 0)
    max_exact = N_BUCKETS // 2
    nf = np.maximum(n, 1).astype(np.float64)
    val = np.log(nf / max_exact) / math.log(MAX_DISTANCE / max_exact) * (N_BUCKETS - max_exact)
    in_window = (n >= max_exact) & (n < B_WINDOW)
    assert np.all(np.abs(val - np.round(val))[in_window & (n != max_exact)] > 1e-3)
    large = np.minimum(max_exact + np.floor(val + 1e-9).astype(np.int64), N_BUCKETS - 1)
    return np.where(n < max_exact, n, large).astype(np.int32)


def _bias_table_kernel(rb_ref, bk_ref, o_ref):
    bk = bk_ref[...]
    for h in range(B_HEADS):
        acc = jnp.zeros(bk.shape, f32)
        for b in range(N_BUCKETS):
            acc = jnp.where(bk == b, rb_ref[b * B_HEADS + h], acc)
        o_ref[h] = acc


def bias_table(rel_bias):
    return pl.pallas_call(
        _bias_table_kernel,
        in_specs=[pl.BlockSpec(memory_space=pltpu.SMEM), pl.BlockSpec(memory_space=pltpu.VMEM)],
        out_specs=pl.BlockSpec(memory_space=pltpu.VMEM),
        out_shape=jax.ShapeDtypeStruct((B_HEADS, B_BLOCK, 2 * B_BLOCK), f32),
        name="bias_table",
    )(rel_bias.reshape(-1), jnp.asarray(_bucket_table()))


def _softmax_with_sink(logits, sink):
    m = jnp.maximum(jnp.max(logits, axis=-1, keepdims=True), sink)
    p = jnp.exp(logits - m)
    return p, jnp.sum(p, axis=-1, keepdims=True) + jnp.exp(sink - m)


def _rms_head_pairs(x, g2, lo):
    sq = x * x
    s_lo = jnp.sum(jnp.where(lo, sq, 0.0), axis=-1, keepdims=True)
    s_hi = jnp.sum(jnp.where(lo, 0.0, sq), axis=-1, keepdims=True)
    r = lax.rsqrt(jnp.where(lo, s_lo, s_hi) * (1.0 / B_HEAD_DIM) + EPS)
    return x * r * g2


def _attn_prompt_kernel(sink_ref, qkv_ref, tab_ref, qn_ref, kn_ref, o_ref, ko_ref, vo_ref,
                        kband_ref, vband_ref, q_ref, p_ref, rhs_ref):
    n = pl.program_id(1)
    T = B_BLOCK

    @pl.when(n == 0)
    def _():
        kband_ref[0:T, :] = jnp.zeros((T, B_KV_DIM), f32)
        vband_ref[0:T, :] = jnp.zeros((T, B_KV_DIM), f32)
        rhs_ref[...] = jnp.ones(rhs_ref.shape, bf16)

    lo = lax.broadcasted_iota(jnp.int32, (1, LANES), 1) < B_HEAD_DIM
    qn2 = qn_ref[...]
    kn2 = kn_ref[...]
    for t in range(B_KV_DIM // LANES):
        lanes = slice(t * LANES, (t + 1) * LANES)
        k2 = _rms_head_pairs(qkv_ref[:, B_Q_DIM + t * LANES:B_Q_DIM + (t + 1) * LANES], kn2, lo)
        ko_ref[0, :, lanes] = k2
        kband_ref[T:, lanes] = k2
    v = qkv_ref[:, B_Q_DIM + B_KV_DIM:]
    vo_ref[0] = v
    vband_ref[T:, :] = v
    for t in range(B_Q_DIM // LANES):
        q_ref[t * T:(t + 1) * T, :] = _rms_head_pairs(qkv_ref[:, t * LANES:(t + 1) * LANES], qn2, lo).astype(bf16)

    key_ops = {}
    band_row = lax.broadcasted_iota(jnp.int32, (2 * T, LANES), 0)
    for t in range(B_KV_DIM // LANES):
        lanes = slice(t * LANES, (t + 1) * LANES)
        kt = kband_ref[:, lanes]
        kr = pltpu.roll(kt, B_HEAD_DIM, axis=1)
        vt = jnp.where(band_row == 0, 0.0, vband_ref[:, lanes])
        vr = pltpu.roll(vt, B_HEAD_DIM, axis=1)
        hi = jnp.logical_not(lo)
        for half, (ksrc, vsrc) in enumerate(((kt, vt), (kr, vr))):
            g_lo, g_hi = (2 * t, 2 * t + 1) if half == 0 else (2 * t + 1, 2 * t)
            key_ops[(g_lo, 0)] = jnp.where(lo, ksrc, 0.0).astype(bf16)
            key_ops[(g_hi, 1)] = jnp.where(hi, ksrc, 0.0).astype(bf16)
            rhs_ref[g_lo * 2 + 0, :, 0:LANES] = jnp.where(lo, vsrc, 1.0).astype(bf16)
            rhs_ref[g_hi * 2 + 1, :, 0:LANES] = jnp.where(hi, vsrc, 1.0).astype(bf16)

    i = lax.broadcasted_iota(jnp.int32, (T, 2 * T), 0)
    j = lax.broadcasted_iota(jnp.int32, (T, 2 * T), 1)
    first_key = jnp.where(n == 0, T, 0)
    valid = (j > i) & (j <= i + B_WINDOW) & (j >= first_key)
    sink_col = j == 0
    lo_t = lax.broadcasted_iota(jnp.int32, (T, LANES), 1) < B_HEAD_DIM
    for g in range(B_KV_HEADS):
        q2 = q_ref[2 * g * T:(2 * g + 2) * T, :]
        res = []
        for half in range(2):
            logits = _dot_nt(q2, key_ops[(g, half)]) * (B_HEAD_DIM ** -0.5)
            for pair in range(2):
                h = g * B_REP + 2 * pair + half
                l = jnp.where(valid, logits[pair * T:(pair + 1) * T] + tab_ref[h], NEG_INF)
                l = jnp.where(sink_col, sink_ref[h], l)
                p = jnp.exp(l - jnp.max(l, axis=-1, keepdims=True))
                p_ref[half, pair * T:(pair + 1) * T, :] = p.astype(bf16)
            res.append(_dot(p_ref[half], rhs_ref[g * 2 + half]))
        for pair in range(2):
            rows = slice(pair * T, (pair + 1) * T)
            even = res[0][rows, 0:LANES] / res[0][rows, LANES:]
            odd = res[1][rows, 0:LANES] / res[1][rows, LANES:]
            t = 2 * g + pair
            o_ref[:, t * LANES:(t + 1) * LANES] = jnp.where(lo_t, even, odd).astype(bf16)
    kband_ref[0:T, :] = kband_ref[T:, :]
    vband_ref[0:T, :] = vband_ref[T:, :]


def attn_prompt(qkv, table, q_norm, k_norm, sinks):
    nb = SEQ // B_BLOCK
    return pl.pallas_call(
        _attn_prompt_kernel,
        grid=(BATCH, nb),
        in_specs=[
            pl.BlockSpec(memory_space=pltpu.SMEM),
            pl.BlockSpec((B_BLOCK, B_Q_DIM + 2 * B_KV_DIM), lambda b, n: (b * nb + n, 0)),
            pl.BlockSpec((B_HEADS, B_BLOCK, 2 * B_BLOCK), lambda b, n: (0, 0, 0)),
            pl.BlockSpec((1, LANES), lambda b, n: (0, 0)),
            pl.BlockSpec((1, LANES), lambda b, n: (0, 0)),
        ],
        out_specs=[
            pl.BlockSpec((B_BLOCK, B_Q_DIM), lambda b, n: (b * nb + n, 0)),
            pl.BlockSpec((1, B_BLOCK, B_KV_DIM), lambda b, n: (b, 0, 0)),
            pl.BlockSpec((1, B_BLOCK, B_KV_DIM), lambda b, n: (b, 0, 0)),
        ],
        out_shape=[
            jax.ShapeDtypeStruct((PROMPT_ROWS, B_Q_DIM), bf16),
            jax.ShapeDtypeStruct((BATCH, B_BLOCK, B_KV_DIM), f32),
            jax.ShapeDtypeStruct((BATCH, B_BLOCK, B_KV_DIM), f32),
        ],
        scratch_shapes=[
            pltpu.VMEM((2 * B_BLOCK, B_KV_DIM), f32),
            pltpu.VMEM((2 * B_BLOCK, B_KV_DIM), f32),
            pltpu.VMEM((B_Q_DIM // LANES * B_BLOCK, LANES), bf16),
            pltpu.VMEM((2, 2 * B_BLOCK, 2 * B_BLOCK), bf16),
            pltpu.VMEM((2 * B_KV_HEADS, 2 * B_BLOCK, 2 * LANES), bf16),
        ],
        compiler_params=_params("parallel", "arbitrary"),
        name="attn_prompt",
    )(sinks, qkv, table, jnp.tile(q_norm, (1, LANES // B_HEAD_DIM)), jnp.tile(k_norm, (1, LANES // B_HEAD_DIM)))


ATT_BB = 8
ATT_QR = B_REP * DEC_SEQ
ATT_KEYS = B_WINDOW + 2 * DEC_SEQ


def _attn_sample_kernel(sink_ref, q_ref, kn_ref, vn_ref, kc_ref, vc_ref, tab_ref, qn_ref, knm_ref,
                        o_ref, ko_ref, vo_ref, kall_ref, vall_ref):
    qn = qn_ref[...]
    knm = knm_ref[...]
    row = lax.broadcasted_iota(jnp.int32, (ATT_QR, ATT_KEYS), 0)
    j = lax.broadcasted_iota(jnp.int32, (ATT_QR, ATT_KEYS), 1)
    t = row % DEC_SEQ
    valid = (j > t) & (j <= t + B_WINDOW)
    r_col = lax.broadcasted_iota(jnp.int32, (ATT_QR, 1), 0) // DEC_SEQ
    pad = jnp.zeros((ATT_KEYS - B_WINDOW - DEC_SEQ, B_KV_DIM), f32)
    kall_ref[B_WINDOW + DEC_SEQ:, :] = pad
    vall_ref[B_WINDOW + DEC_SEQ:, :] = pad
    for s in range(ATT_BB):
        kc = kc_ref[s]
        vc = vc_ref[s]
        k_new = jnp.concatenate(
            [_rms(kn_ref[s, :, g * B_HEAD_DIM:(g + 1) * B_HEAD_DIM], knm) for g in range(B_KV_HEADS)], axis=1)
        v_new = vn_ref[s]
        ko_ref[s, 0:B_WINDOW - DEC_SEQ, :] = kc[DEC_SEQ:, :]
        ko_ref[s, B_WINDOW - DEC_SEQ:, :] = k_new
        vo_ref[s, 0:B_WINDOW - DEC_SEQ, :] = vc[DEC_SEQ:, :]
        vo_ref[s, B_WINDOW - DEC_SEQ:, :] = v_new
        kall_ref[0:B_WINDOW, :] = kc
        kall_ref[B_WINDOW:B_WINDOW + DEC_SEQ, :] = k_new
        vall_ref[0:B_WINDOW, :] = vc
        vall_ref[B_WINDOW:B_WINDOW + DEC_SEQ, :] = v_new
        q = q_ref[s]
        for g in range(B_KV_HEADS):
            gd = slice(g * B_HEAD_DIM, (g + 1) * B_HEAD_DIM)
            qg = _rms(q[:, gd], qn).astype(bf16)
            keys = kall_ref[:, gd].astype(bf16)
            vals = vall_ref[:, gd].astype(bf16)
            logits = _dot_nt(qg, keys) * (B_HEAD_DIM ** -0.5)
            logits = jnp.where(valid, logits + tab_ref[g], NEG_INF)
            sink = jnp.zeros((ATT_QR, 1), f32)
            for r in range(B_REP):
                sink = jnp.where(r_col == r, sink_ref[g * B_REP + r], sink)
            p, denom = _softmax_with_sink(logits, sink)
            o_ref[s, :, gd] = _dot(p.astype(bf16), vals) / denom


def attn_sample(q_s, k_new, v_new, k_cache, v_cache, table_s, q_norm, k_norm, sinks):
    blk = lambda *shape: pl.BlockSpec((ATT_BB,) + shape, lambda i: (i,) + (0,) * len(shape))
    full = lambda *shape: pl.BlockSpec(shape, lambda i: (0,) * len(shape))
    return pl.pallas_call(
        _attn_sample_kernel,
        grid=(DEC_BATCH // ATT_BB,),
        in_specs=[
            pl.BlockSpec(memory_space=pltpu.SMEM),
            blk(ATT_QR, B_KV_DIM), blk(DEC_SEQ, B_KV_DIM), blk(DEC_SEQ, B_KV_DIM),
            blk(B_WINDOW, B_KV_DIM), blk(B_WINDOW, B_KV_DIM),
            full(B_KV_HEADS, ATT_QR, ATT_KEYS), full(1, B_HEAD_DIM), full(1, B_HEAD_DIM),
        ],
        out_specs=[blk(ATT_QR, B_KV_DIM), blk(B_WINDOW, B_KV_DIM), blk(B_WINDOW, B_KV_DIM)],
        out_shape=[
            jax.ShapeDtypeStruct((DEC_BATCH, ATT_QR, B_KV_DIM), f32),
            jax.ShapeDtypeStruct((DEC_BATCH, B_WINDOW, B_KV_DIM), f32),
            jax.ShapeDtypeStruct((DEC_BATCH, B_WINDOW, B_KV_DIM), f32),
        ],
        scratch_shapes=[pltpu.VMEM((ATT_KEYS, B_KV_DIM), f32), pltpu.VMEM((ATT_KEYS, B_KV_DIM), f32)],
        compiler_params=_params("parallel"),
        name="attn_sample",
    )(sinks, q_s, k_new, v_new, k_cache, v_cache, table_s, q_norm, k_norm)


CONV_PAD = SUBLANES


def _gated_group_norm(y, z, norm_w):
    gt = y * _silu(z)
    parts = []
    for g in range(C_GROUPS):
        gg = gt[:, g * C_GROUP_W:(g + 1) * C_GROUP_W]
        parts.append(gg * lax.rsqrt(jnp.mean(gg * gg, axis=-1, keepdims=True) + EPS))
    return jnp.concatenate(parts, axis=1) * norm_w


LOG2E = math.log2(math.e)


def _expand_heads(v, sel3):
    lane = lax.broadcasted_iota(jnp.int32, (1, LANES), 1)
    v = jnp.where(lane < C_HEADS, v, 0.0)
    hi = v.astype(bf16).astype(f32)
    r1 = v - hi
    mid = r1.astype(bf16).astype(f32)
    lo = r1 - mid
    packed = hi + pltpu.roll(mid, C_HEADS, axis=1) + pltpu.roll(lo, 2 * C_HEADS, axis=1)
    return _dot(packed.astype(bf16), sel3)


def _ssd_prompt_kernel(zx_ref, dtr_ref, cw_ref, cb_ref, dtb_ref, alog_ref, dsk_ref, nw_ref, sel_ref,
                       yn_ref, hfin_ref, cout_ref, xpad_ref, ht_ref, y_ref):
    c = pl.program_id(1)
    T = C_CHUNK

    @pl.when(c == 0)
    def _():
        xpad_ref[0:CONV_PAD, :] = jnp.zeros((CONV_PAD, C_CONV_DIM), f32)
        ht_ref[...] = jnp.zeros(ht_ref.shape, f32)

    xbc = zx_ref[:, C_D_INNER:]
    xpad_ref[CONV_PAD:, :] = xbc
    xp = xpad_ref[...]
    cw = cw_ref[...]
    acc = cb_ref[...]
    for tap in range(C_D_CONV - 1):
        shifted = pltpu.roll(xp, C_D_CONV - 1 - tap, axis=0)[CONV_PAD:, :]
        acc = acc + shifted * cw[tap:tap + 1, :]
    acc = acc + xbc * cw[C_D_CONV - 1:C_D_CONV, :]
    xpad_ref[0:CONV_PAD, :] = xbc[T - CONV_PAD:, :]
    cout_ref[0] = xbc[T - (C_D_CONV - 1):, :]
    act = _silu(acc)
    xs = act[:, :C_D_INNER]
    bm = act[:, C_D_INNER:C_D_INNER + C_BC_DIM]
    cm = act[:, C_D_INNER + C_BC_DIM:]
    xb = xs.astype(bf16)

    dt = _softplus(dtr_ref[...] + dtb_ref[...])
    a_neg = -jnp.exp(alog_ref[...])
    row = lax.broadcasted_iota(jnp.int32, (T, T), 0)
    col = lax.broadcasted_iota(jnp.int32, (T, T), 1)
    causal = row >= col
    acs = _dot_exact_lhs01(causal.astype(f32), dt * a_neg)
    a2 = acs * LOG2E
    sel3 = sel_ref[...]
    e_exp = jnp.exp2(_expand_heads(a2, sel3))
    w_exp = _expand_heads(jnp.exp(acs[T - 1:T, :] - acs) * dt, sel3)
    cdec = e_exp[T - 1:T, :]
    b2_t = a2.T - jnp.log2(dt.T)
    xw = (xs * w_exp).astype(bf16)
    hb = ht_ref[...].astype(bf16)
    dsk = dsk_ref[...]
    lo_t = lax.broadcasted_iota(jnp.int32, (T, LANES), 1) < C_HEAD_DIM

    for g in range(C_GROUPS):
        ns = slice(g * C_D_STATE, (g + 1) * C_D_STATE)
        gs = slice(g * C_GROUP_W, (g + 1) * C_GROUP_W)
        b_g = bm[:, ns]
        c_g = cm[:, ns].astype(bf16)
        cb = _dot_nt(c_g, b_g.astype(bf16))
        yi = _dot(c_g, hb[:, gs])
        ht_ref[:, gs] = ht_ref[:, gs] * cdec[:, gs] + _dot(b_g.T.astype(bf16), xw[:, gs])
        for tt in range(C_GROUP_W // LANES):
            t = g * (C_GROUP_W // LANES) + tt
            lanes = slice(t * LANES, (t + 1) * LANES)
            xt = xb[:, lanes]
            res = []
            for half in range(2):
                h = 2 * t + half
                a_col = jnp.broadcast_to(a2[:, h:h + 1], (T, T))
                b_row = jnp.broadcast_to(b2_t[h:h + 1, :], (T, T))
                w = jnp.where(causal, cb * jnp.exp2(a_col - b_row), 0.0)
                res.append(_dot(w.astype(bf16), xt))
            y_intra = jnp.where(lo_t, res[0], res[1])
            y_ref[:, lanes] = y_intra + e_exp[:, lanes] * yi[:, tt * LANES:(tt + 1) * LANES] + dsk[:, lanes] * xs[:, lanes]

    yn_ref[...] = _gated_group_norm(y_ref[...], zx_ref[:, :C_D_INNER], nw_ref[...]).astype(bf16)

    @pl.when(c == pl.num_programs(1) - 1)
    def _():
        for t in range(C_D_INNER // LANES):
            hfin_ref[0, t * LANES:(t + 1) * LANES, :] = ht_ref[:, t * LANES:(t + 1) * LANES].T


def _head_select3():
    k = np.arange(LANES)[:, None]
    ch = np.arange(C_D_INNER)[None, :] // C_HEAD_DIM
    return jnp.asarray((k % C_HEADS == ch) & (k < 3 * C_HEADS), dtype=bf16)


def ssd_prompt(zx, dtr, conv_w, conv_b, dt_bias, a_log, d_skip, norm_w):
    nc = SEQ // C_CHUNK
    full = lambda *shape: pl.BlockSpec(shape, lambda b, c: (0,) * len(shape))
    return pl.pallas_call(
        _ssd_prompt_kernel,
        grid=(BATCH, nc),
        in_specs=[
            pl.BlockSpec((C_CHUNK, C_D_INNER + C_CONV_DIM), lambda b, c: (b * nc + c, 0)),
            pl.BlockSpec((C_CHUNK, LANES), lambda b, c: (b * nc + c, 0)),
            full(C_D_CONV, C_CONV_DIM), full(1, C_CONV_DIM), full(1, LANES), full(1, LANES),
            full(1, C_D_INNER), full(1, C_D_INNER), full(LANES, C_D_INNER),
        ],
        out_specs=[
            pl.BlockSpec((C_CHUNK, C_D_INNER), lambda b, c: (b * nc + c, 0)),
            pl.BlockSpec((1, C_D_INNER, C_D_STATE), lambda b, c: (b, 0, 0)),
            pl.BlockSpec((1, C_D_CONV - 1, C_CONV_DIM), lambda b, c: (b, 0, 0)),
        ],
        out_shape=[
            jax.ShapeDtypeStruct((PROMPT_ROWS, C_D_INNER), bf16),
            jax.ShapeDtypeStruct((BATCH, C_D_INNER, C_D_STATE), f32),
            jax.ShapeDtypeStruct((BATCH, C_D_CONV - 1, C_CONV_DIM), f32),
        ],
        scratch_shapes=[
            pltpu.VMEM((CONV_PAD + C_CHUNK, C_CONV_DIM), f32),
            pltpu.VMEM((C_D_STATE, C_D_INNER), f32),
            pltpu.VMEM((C_CHUNK, C_D_INNER), f32),
        ],
        compiler_params=_params("parallel", "arbitrary"),
        name="ssd_prompt",
    )(zx, dtr, conv_w, conv_b, dt_bias, a_log, d_skip, norm_w, _head_select3())


SSD_BB = 8
SSD_TP = SUBLANES
_N_PAIRS = DEC_SEQ * (DEC_SEQ + 1) // 2
_N_COEF = _N_PAIRS + 2 * DEC_SEQ


def _ssd_sample_kernel(zx_ref, dtr_ref, cs_ref, h0_ref, cw_ref, cb_ref, dtb_ref, alog_ref, dsk_ref, nw_ref,
                       sel_ref, yn_ref, hn_ref, cout_ref, c_scr, b_scr, xw_scr, yi_scr, cd_scr):
    L = DEC_SEQ
    cw = cw_ref[...]
    xp = [cs_ref[k] for k in range(C_D_CONV - 1)] + [zx_ref[t, :, C_D_INNER:] for t in range(L)]
    for k in range(C_D_CONV - 1):
        cout_ref[k] = xp[L + k]
    act = []
    for t in range(L):
        acc = cb_ref[...]
        for tap in range(C_D_CONV):
            acc = acc + xp[t + tap] * cw[tap:tap + 1, :]
        act.append(_silu(acc))
    xs = [a[:, :C_D_INNER] for a in act]
    bm = [a[:, C_D_INNER:C_D_INNER + C_BC_DIM] for a in act]
    cm = [a[:, C_D_INNER + C_BC_DIM:] for a in act]

    a_neg = -jnp.exp(alog_ref[...])
    dt = [_softplus(dtr_ref[t] + dtb_ref[...]) for t in range(L)]
    acs = []
    for t in range(L):
        acs.append(dt[t] * a_neg if t == 0 else acs[t - 1] + dt[t] * a_neg)

    lane_group = lax.broadcasted_iota(jnp.int32, (SSD_BB, LANES), 1) // C_REP
    coefs = []
    for t in range(L):
        for t2 in range(t + 1):
            cbh = jnp.zeros((SSD_BB, LANES), f32)
            for g in range(C_GROUPS):
                ns = slice(g * C_D_STATE, (g + 1) * C_D_STATE)
                cbg = jnp.sum(cm[t][:, ns] * bm[t2][:, ns], axis=-1, keepdims=True)
                cbh = jnp.where(lane_group == g, cbg, cbh)
            coefs.append(cbh * jnp.exp(acs[t] - acs[t2]) * dt[t2])
    for t in range(L):
        coefs.append(jnp.exp(acs[t]))
    for t in range(L):
        coefs.append(jnp.exp(acs[L - 1] - acs[t]) * dt[t])
    coef = jnp.concatenate(coefs, axis=0)
    cexp = _dot_exact_rhs01(coef, sel_ref[...])
    cexp = [cexp[k * SSD_BB:(k + 1) * SSD_BB, :] for k in range(_N_COEF)]
    w_intra = cexp[:_N_PAIRS]
    w_inter = cexp[_N_PAIRS:_N_PAIRS + L]
    w_state = cexp[_N_PAIRS + L:]

    cd = jnp.concatenate([jnp.exp(acs[L - 1]), jnp.zeros((LANES - SSD_BB, LANES), f32)], axis=0)
    cd_t = cd.T
    for s in range(SSD_BB):
        cd_scr[s] = jnp.broadcast_to(cd_t[0:C_HEADS, s:s + 1], (C_HEADS, C_D_STATE))

    zeros_tail = jnp.zeros((SSD_BB, SSD_TP - L, C_D_INNER), f32)
    c_scr[:, L:, :] = zeros_tail[:, :, :C_BC_DIM]
    b_scr[:, L:, :] = zeros_tail[:, :, :C_BC_DIM]
    xw_scr[:, L:, :] = zeros_tail
    for t in range(L):
        xw_t = xs[t] * w_state[t]
        for s in range(SSD_BB):
            c_scr[s, t:t + 1, :] = cm[t][s:s + 1, :]
            b_scr[s, t:t + 1, :] = bm[t][s:s + 1, :]
            xw_scr[s, t:t + 1, :] = xw_t[s:s + 1, :]

    for s in range(SSD_BB):
        for g in range(C_GROUPS):
            ns = slice(g * C_D_STATE, (g + 1) * C_D_STATE)
            gs = slice(g * C_GROUP_W, (g + 1) * C_GROUP_W)
            h0 = h0_ref[s, gs, :]
            yi = _dot_nt(c_scr[s, :, ns].astype(bf16), h0.astype(bf16))
            for t in range(L):
                yi_scr[t, s:s + 1, gs] = yi[t:t + 1, :]
            st = _dot_tn(xw_scr[s, :, gs].astype(bf16), b_scr[s, :, ns].astype(bf16))
            for r in range(C_REP):
                h = g * C_REP + r
                rs = slice(r * C_HEAD_DIM, (r + 1) * C_HEAD_DIM)
                scale = cd_scr[s, h:h + 1, :]
                hn_ref[s, h * C_HEAD_DIM:(h + 1) * C_HEAD_DIM, :] = h0[rs, :] * scale + st[rs, :]

    dsk = dsk_ref[...]
    nw = nw_ref[...]
    pair = 0
    for t in range(L):
        y = w_inter[t] * yi_scr[t] + dsk * xs[t]
        for t2 in range(t + 1):
            y = y + w_intra[pair] * xs[t2]
            pair += 1
        yn_ref[t] = _gated_group_norm(y, zx_ref[t, :, :C_D_INNER], nw).astype(bf16)


def ssd_sample(zx_t, dtr_t, conv_state_t, h0, conv_w, conv_b, dt_bias, a_log, d_skip, norm_w, sel):
    tmaj = lambda n, w: pl.BlockSpec((n, SSD_BB, w), lambda i: (0, i, 0))
    full = lambda *shape: pl.BlockSpec(shape, lambda i: (0,) * len(shape))
    return pl.pallas_call(
        _ssd_sample_kernel,
        grid=(DEC_BATCH // SSD_BB,),
        in_specs=[
            tmaj(DEC_SEQ, C_D_INNER + C_CONV_DIM), tmaj(DEC_SEQ, LANES), tmaj(C_D_CONV - 1, C_CONV_DIM),
            pl.BlockSpec((SSD_BB, C_D_INNER, C_D_STATE), lambda i: (i, 0, 0)),
            full(C_D_CONV, C_CONV_DIM), full(1, C_CONV_DIM), full(1, LANES), full(1, LANES),
            full(1, C_D_INNER), full(1, C_D_INNER), full(LANES, C_D_INNER),
        ],
        out_specs=[
            tmaj(DEC_SEQ, C_D_INNER),
            pl.BlockSpec((SSD_BB, C_D_INNER, C_D_STATE), lambda i: (i, 0, 0)),
            tmaj(C_D_CONV - 1, C_CONV_DIM),
        ],
        out_shape=[
            jax.ShapeDtypeStruct((DEC_SEQ, DEC_BATCH, C_D_INNER), bf16),
            jax.ShapeDtypeStruct((DEC_BATCH, C_D_INNER, C_D_STATE), f32),
            jax.ShapeDtypeStruct((C_D_CONV - 1, DEC_BATCH, C_CONV_DIM), f32),
        ],
        scratch_shapes=[
            pltpu.VMEM((SSD_BB, SSD_TP, C_BC_DIM), f32),
            pltpu.VMEM((SSD_BB, SSD_TP, C_BC_DIM), f32),
            pltpu.VMEM((SSD_BB, SSD_TP, C_D_INNER), f32),
            pltpu.VMEM((DEC_SEQ, SSD_BB, C_D_INNER), f32),
            pltpu.VMEM((SSD_BB, C_HEADS, C_D_STATE), f32),
        ],
        compiler_params=_params("parallel"),
        name="ssd_sample",
    )(zx_t, dtr_t, conv_state_t, h0, conv_w, conv_b, dt_bias, a_log, d_skip, norm_w, sel)


def _pad_lanes(v):
    return jnp.pad(v.astype(f32), (0, LANES - v.shape[0])).reshape(1, LANES)


def _mixer_a(xp, xs, g, j, w_in, norm_v, w_sp, b_sp, w_out):
    nv = norm_v.reshape(1, A_HALF)
    xp = mixer_a(xp, g, w_in, nv, w_sp, b_sp.T, w_out, j, tm=TM_PROMPT, sample=False)
    xs, v_s = mixer_a(xs, g, w_in, nv, w_sp[:, :DEC_SEQ, :DEC_SEQ].reshape(-1), b_sp[:, :DEC_SEQ].reshape(-1),
                      w_out, j, tm=TM_SAMPLE, sample=True)
    return xp, xs, jnp.swapaxes(v_s.reshape(DEC_SEQ, DEC_BATCH, A_HALF), 0, 1)


def _mixer_b(xp, xs, g, j, k_cache, v_cache, w_qkv, q_norm, k_norm, sinks, rel_bias, w_out):
    qn = q_norm.reshape(1, B_HEAD_DIM)
    kn = k_norm.reshape(1, B_HEAD_DIM)
    table = bias_table(rel_bias)
    n_qkv = B_Q_DIM + 2 * B_KV_DIM

    qkv_p = norm_matmul(xp, g, w_qkv, j, n_qkv, tm=TM_PROMPT, tn=512)
    o_p, k_p, v_p = attn_prompt(qkv_p, table, qn, kn, sinks)
    xp = matmul_res(o_p, w_out, j, xp, tm=TM_PROMPT)

    qkv_s = norm_matmul(xs, g, w_qkv, j, n_qkv, tm=TM_SAMPLE, tn=512).reshape(DEC_SEQ, DEC_BATCH, -1)
    q_s = qkv_s[:, :, :B_Q_DIM].reshape(DEC_SEQ, DEC_BATCH, B_KV_HEADS, B_REP, B_HEAD_DIM)
    q_s = q_s.transpose(1, 3, 0, 2, 4).reshape(DEC_BATCH, ATT_QR, B_KV_DIM)
    k_new = jnp.swapaxes(qkv_s[:, :, B_Q_DIM:B_Q_DIM + B_KV_DIM], 0, 1)
    v_new = jnp.swapaxes(qkv_s[:, :, B_Q_DIM + B_KV_DIM:], 0, 1)
    table_s = table[:, :DEC_SEQ, :ATT_KEYS].reshape(B_KV_HEADS, ATT_QR, ATT_KEYS)
    o_s, k_s, v_s = attn_sample(q_s, k_new, v_new,
                                k_cache.reshape(DEC_BATCH, B_WINDOW, B_KV_DIM),
                                v_cache.reshape(DEC_BATCH, B_WINDOW, B_KV_DIM),
                                table_s, qn, kn, sinks)
    o_s = o_s.reshape(DEC_BATCH, B_REP, DEC_SEQ, B_KV_HEADS, B_HEAD_DIM).transpose(2, 0, 3, 1, 4)
    xs = matmul_res(o_s.reshape(SAMPLE_ROWS, B_Q_DIM).astype(bf16), w_out, j, xs, tm=TM_SAMPLE)
    kv_shape_p = (BATCH, B_WINDOW, B_KV_HEADS, B_HEAD_DIM)
    kv_shape_s = (DEC_BATCH, B_WINDOW, B_KV_HEADS, B_HEAD_DIM)
    return xp, xs, k_p.reshape(kv_shape_p), v_p.reshape(kv_shape_p), k_s.reshape(kv_shape_s), v_s.reshape(kv_shape_s)


def _mixer_c(xp, xs, g, j, h0, conv_state, w_in, conv_w, conv_b, dt_bias, a_log, d_skip, norm_w, w_out):
    n_zx = C_D_INNER + C_CONV_DIM
    w_dt = jnp.pad(w_in[j, :, n_zx:], ((0, 0), (0, LANES - C_HEADS)))[None]
    cb = conv_b.reshape(1, C_CONV_DIM)
    dtb = _pad_lanes(dt_bias)
    alog = _pad_lanes(a_log)
    dsk = jnp.repeat(d_skip.astype(f32), C_HEAD_DIM).reshape(1, C_D_INNER)
    nw = norm_w.reshape(1, C_D_INNER)

    zx_p = norm_matmul(xp, g, w_in, j, n_zx, tm=TM_PROMPT, tn=1024)
    dtr_p = norm_matmul(xp, g, w_dt, 0, LANES, tm=TM_PROMPT, tn=LANES)
    yn_p, h_p, conv_p = ssd_prompt(zx_p, dtr_p, conv_w, cb, dtb, alog, dsk, nw)
    xp = matmul_res(yn_p, w_out, j, xp, tm=TM_PROMPT)

    zx_s = norm_matmul(xs, g, w_in, j, n_zx, tm=TM_SAMPLE, tn=1024)
    dtr_s = norm_matmul(xs, g, w_dt, 0, LANES, tm=TM_SAMPLE, tn=LANES)
    sel = (jnp.arange(LANES)[:, None] == jnp.arange(C_D_INNER)[None, :] // C_HEAD_DIM).astype(f32)
    yn_s, h_s, conv_s = ssd_sample(
        zx_s.reshape(DEC_SEQ, DEC_BATCH, -1), dtr_s.reshape(DEC_SEQ, DEC_BATCH, LANES),
        jnp.swapaxes(conv_state, 0, 1), h0.reshape(DEC_BATCH, C_D_INNER, C_D_STATE),
        conv_w, cb, dtb, alog, dsk, nw, sel)
    xs = matmul_res(yn_s.reshape(SAMPLE_ROWS, C_D_INNER), w_out, j, xs, tm=TM_SAMPLE)
    st_shape = (C_HEADS, C_HEAD_DIM, C_D_STATE)
    return (xp, xs, h_p.reshape((BATCH,) + st_shape), conv_p,
            h_s.reshape((DEC_BATCH,) + st_shape), jnp.swapaxes(conv_s, 0, 1))


def kernel(x_prompt, x_sample, cache_swa_k, cache_swa_v, state_ssm, state_conv, norm_mixer, norm_mlp, mlp_w_up, mlp_w_down, a_w_in, a_norm_v, a_w_spatial, a_b_spatial, a_w_out, b_w_qkv, b_q_norm, b_k_norm, b_sinks, rel_bias, b_w_out, c_w_in, c_conv_w, c_conv_b, c_dt_bias, c_a_log, c_d, c_norm, c_w_out):
    xp = x_prompt.reshape(PROMPT_ROWS, D_MODEL)
    xs = jnp.swapaxes(x_sample, 0, 1).reshape(SAMPLE_ROWS, D_MODEL)
    chunk_v_s = []
    swa_kp, swa_vp, swa_ks, swa_vs = [], [], [], []
    ssm_p, conv_p, ssm_s, conv_s = [], [], [], []
    for i in range(DEPTH):
        kind = i % N_MIXERS
        j = i // N_MIXERS
        g = norm_mixer[i].reshape(1, D_MODEL)
        if kind == 0:
            xp, xs, v_new = _mixer_a(xp, xs, g, j, a_w_in, a_norm_v[j], a_w_spatial[j], a_b_spatial[j], a_w_out)
            chunk_v_s.append(v_new)
        elif kind == 1:
            xp, xs, kp, vp, ks_, vs_ = _mixer_b(xp, xs, g, j, cache_swa_k[j], cache_swa_v[j], b_w_qkv, b_q_norm[j],
                                                b_k_norm[j], b_sinks[j], rel_bias, b_w_out)
            swa_kp.append(kp); swa_vp.append(vp); swa_ks.append(ks_); swa_vs.append(vs_)
        else:
            xp, xs, hp, bp, hs, bs = _mixer_c(xp, xs, g, j, state_ssm[j], state_conv[j], c_w_in, c_conv_w[j],
                                              c_conv_b[j], c_dt_bias[j], c_a_log[j], c_d[j], c_norm[j], c_w_out)
            ssm_p.append(hp); conv_p.append(bp); ssm_s.append(hs); conv_s.append(bs)
        gm = norm_mlp[i].reshape(1, D_MODEL)
        xp = mlp(xp, gm, mlp_w_up, mlp_w_down, i, tm=TM_PROMPT, tf=MLP_TF)
        xs = mlp(xs, gm, mlp_w_up, mlp_w_down, i, tm=TM_SAMPLE, tf=MLP_TF)
    y_prompt = xp.reshape(BATCH, SEQ, D_MODEL)
    y_sample = jnp.swapaxes(xs.reshape(DEC_SEQ, DEC_BATCH, D_MODEL), 0, 1)
    return (y_prompt, y_sample, jnp.stack(chunk_v_s),
            jnp.stack(swa_kp), jnp.stack(swa_vp), jnp.stack(swa_ks), jnp.stack(swa_vs),
            jnp.stack(ssm_p), jnp.stack(conv_p), jnp.stack(ssm_s), jnp.stack(conv_s))
```

```python
import functools
import math

import jax
import jax.numpy as jnp
import numpy as np
from jax import lax
from jax.experimental import pallas as pl
from jax.experimental.pallas import tpu as pltpu

f32 = jnp.float32
bf16 = jnp.bfloat16

D_MODEL = 1024
BATCH = 4
SEQ = 4096
DEPTH = 4
DEC_BATCH = 128
DEC_SEQ = 4
PAST_LEN = 8192
N_MIXERS = 3
D_FF = 4 * D_MODEL
EPS = 1e-6
NEG_INF = -1e30

A_CHUNK = 128
A_D_FFN = 6 * D_MODEL
A_HALF = A_D_FFN // 2
A_GROUPS = 8
A_GROUP_W = A_HALF // A_GROUPS

B_HEADS = 16
B_KV_HEADS = 4
B_HEAD_DIM = 64
B_REP = B_HEADS // B_KV_HEADS
B_WINDOW = 128
B_BLOCK = 128
B_Q_DIM = B_HEADS * B_HEAD_DIM
B_KV_DIM = B_KV_HEADS * B_HEAD_DIM
N_BUCKETS = 32
MAX_DISTANCE = 128

C_D_INNER = 2 * D_MODEL
C_HEAD_DIM = 64
C_HEADS = C_D_INNER // C_HEAD_DIM
C_GROUPS = 4
C_REP = C_HEADS // C_GROUPS
C_D_STATE = 128
C_D_CONV = 4
C_BC_DIM = C_GROUPS * C_D_STATE
C_CONV_DIM = C_D_INNER + 2 * C_BC_DIM
C_GROUP_W = C_D_INNER // C_GROUPS
C_CHUNK = 128

LANES = 128
SUBLANES = 8
VMEM_LIMIT_BYTES = 56 * 1024 * 1024

PROMPT_ROWS = BATCH * SEQ
SAMPLE_ROWS = DEC_BATCH * DEC_SEQ
TM_PROMPT = 1024
TM_SAMPLE = SAMPLE_ROWS
MLP_TF = 1024


def _params(*sem):
    return pltpu.CompilerParams(dimension_semantics=sem, vmem_limit_bytes=VMEM_LIMIT_BYTES)


def _rms(x, g):
    ms = jnp.mean(x * x, axis=-1, keepdims=True)
    return x * lax.rsqrt(ms + EPS) * g


def _gelu(x):
    return 0.5 * x * (1.0 + lax.erf(x * math.sqrt(0.5)))


def _silu(x):
    return x * jax.nn.sigmoid(x)


def _softplus(x):
    return jnp.maximum(x, 0.0) + jnp.log1p(jnp.exp(-jnp.abs(x)))


def _dot(a, b):
    return jnp.dot(a, b, preferred_element_type=f32)


def _dot_nt(a, b):
    return lax.dot_general(a, b, (((1,), (1,)), ((), ())), preferred_element_type=f32)


def _dot_tn(a, b):
    return lax.dot_general(a, b, (((0,), (0,)), ((), ())), preferred_element_type=f32)


def _dot_exact_lhs01(a01, x):
    a = a01.astype(bf16)
    hi = x.astype(bf16)
    r1 = x - hi.astype(f32)
    mid = r1.astype(bf16)
    lo = (r1 - mid.astype(f32)).astype(bf16)
    return _dot(a, hi) + _dot(a, mid) + _dot(a, lo)


def _dot_exact_rhs01(x, b01):
    b = b01.astype(bf16)
    hi = x.astype(bf16)
    r1 = x - hi.astype(f32)
    mid = r1.astype(bf16)
    lo = (r1 - mid.astype(f32)).astype(bf16)
    return _dot(hi, b) + _dot(mid, b) + _dot(lo, b)


def _norm_matmul_kernel(*refs, nj, tail):
    if tail:
        x_ref, g_ref, w_ref, wt_ref, o_ref, ot_ref, xn_ref = refs
    else:
        x_ref, g_ref, w_ref, o_ref, xn_ref = refs
    j = pl.program_id(1)

    @pl.when(j == 0)
    def _():
        xn_ref[...] = _rms(x_ref[...], g_ref[...]).astype(bf16)

    @pl.when(j < nj)
    def _():
        o_ref[...] = _dot(xn_ref[...], w_ref[0].astype(bf16))

    if tail:
        @pl.when(j == nj)
        def _():
            ot_ref[...] = _dot(xn_ref[...], wt_ref[...].astype(bf16))


def norm_matmul(x, g, w, layer, n, *, tm, tn, w_tail=None):
    m, k = x.shape
    nj = n // tn
    tail = w_tail is not None
    last = nj - 1
    in_specs = [
        pl.BlockSpec((tm, k), lambda i, j: (i, 0)),
        pl.BlockSpec((1, k), lambda i, j: (0, 0)),
        pl.BlockSpec((1, k, tn), lambda i, j: (layer, 0, jnp.minimum(j, last))),
    ]
    out_specs = [pl.BlockSpec((tm, tn), lambda i, j: (i, jnp.minimum(j, last)))]
    out_shape = [jax.ShapeDtypeStruct((m, n), f32)]
    args = [x, g, w]
    if tail:
        in_specs.append(pl.BlockSpec((k, LANES), lambda i, j: (0, 0)))
        out_specs.append(pl.BlockSpec((tm, LANES), lambda i, j: (i, 0)))
        out_shape.append(jax.ShapeDtypeStruct((m, LANES), f32))
        args.append(w_tail)
    out = pl.pallas_call(
        functools.partial(_norm_matmul_kernel, nj=nj, tail=tail),
        grid=(m // tm, nj + (1 if tail else 0)),
        in_specs=in_specs,
        out_specs=out_specs,
        out_shape=out_shape,
        scratch_shapes=[pltpu.VMEM((tm, k), bf16)],
        compiler_params=_params("parallel", "arbitrary"),
        name="norm_matmul",
    )(*args)
    return out if tail else out[0]


def _mlp_kernel(*refs, proj):
    if proj:
        a_ref, wo_ref, x_ref, g_ref, wu_ref, wd_ref, o_ref, xn_ref = refs
    else:
        x_ref, g_ref, wu_ref, wd_ref, o_ref, xn_ref = refs

    @pl.when(pl.program_id(1) == 0)
    def _():
        x = x_ref[...]
        if proj:
            x = x + _dot(a_ref[...], wo_ref[0])
        xn_ref[...] = _rms(x, g_ref[...]).astype(bf16)
        o_ref[...] = x

    h = jnp.maximum(_dot(xn_ref[...], wu_ref[0].astype(bf16)), 0.0)
    o_ref[...] += _dot((h * h).astype(bf16), wd_ref[0].astype(bf16))


def mlp(x, g, w_up, w_down, layer, *, tm, tf, proj=None):
    m, d = x.shape
    ff = w_up.shape[2]
    in_specs = [
        pl.BlockSpec((tm, d), lambda i, j: (i, 0)),
        pl.BlockSpec((1, d), lambda i, j: (0, 0)),
        pl.BlockSpec((1, d, tf), lambda i, j: (layer, 0, j)),
        pl.BlockSpec((1, tf, d), lambda i, j: (layer, j, 0)),
    ]
    args = [x, g, w_up, w_down]
    if proj is not None:
        a, w_o, lo = proj
        k = a.shape[1]
        in_specs = [pl.BlockSpec((tm, k), lambda i, j: (i, 0)),
                    pl.BlockSpec((1, k, d), lambda i, j: (lo, 0, 0), pipeline_mode=pl.Buffered(1))] + in_specs
        args = [a, w_o] + args
    return pl.pallas_call(
        functools.partial(_mlp_kernel, proj=proj is not None),
        grid=(m // tm, ff // tf),
        in_specs=in_specs,
        out_specs=pl.BlockSpec((tm, d), lambda i, j: (i, 0)),
        out_shape=jax.ShapeDtypeStruct((m, d), f32),
        scratch_shapes=[pltpu.VMEM((tm, d), bf16)],
        compiler_params=_params("parallel", "arbitrary"),
        name="mlp",
    )(*args)


A_BLK_GROUPS = 2
A_BLK = A_BLK_GROUPS * A_GROUP_W
A_NBLK = A_HALF // A_BLK


def _mixer_a_kernel(*refs, sample):
    if sample:
        ws_ref, bs_ref, x_ref, g_ref, win_ref, nv_ref, wout_ref, o_ref, vo_ref, xn_ref, v_ref, ssq_ref, us_ref = refs
    else:
        x_ref, g_ref, win_ref, nv_ref, ws_ref, bs_ref, wout_ref, o_ref, xn_ref, v_ref, ssq_ref, us_ref = refs
    j = pl.program_id(1)
    tm = x_ref.shape[0]

    @pl.when(j == 0)
    def _():
        xn_ref[...] = _rms(x_ref[...], g_ref[...]).astype(bf16)
        ssq_ref[...] = jnp.zeros(ssq_ref.shape, f32)

    for k in range(A_NBLK):
        @pl.when(j == k)
        def _(k=k):
            v = _gelu(_dot(xn_ref[...], win_ref[0].astype(bf16)))
            v_ref[:, k * A_BLK:(k + 1) * A_BLK] = v
            ssq_ref[...] += jnp.sum(v * v, axis=-1, keepdims=True)

    if not sample:
        row = lax.broadcasted_iota(jnp.int32, (A_CHUNK, A_CHUNK), 0)
        col = lax.broadcasted_iota(jnp.int32, (A_CHUNK, A_CHUNK), 1)
        causal = row >= col

    for k in range(A_NBLK):
        @pl.when(j == A_NBLK + k)
        def _(k=k):
            u = _gelu(_dot(xn_ref[...], win_ref[0].astype(bf16)))
            rinv = lax.rsqrt(ssq_ref[...] * (1.0 / A_HALF) + EPS)
            for gg in range(A_BLK_GROUPS):
                g = k * A_BLK_GROUPS + gg
                cols = slice(g * A_GROUP_W, (g + 1) * A_GROUP_W)
                ucols = slice(gg * A_GROUP_W, (gg + 1) * A_GROUP_W)
                vn = v_ref[:, cols] * rinv * nv_ref[:, cols]
                if sample:
                    vo_ref[:, cols] = vn
                    vt = [vn[t * DEC_BATCH:(t + 1) * DEC_BATCH] for t in range(DEC_SEQ)]
                    s_rows = []
                    for t in range(DEC_SEQ):
                        s = ws_ref[(g * DEC_SEQ + t) * DEC_SEQ] * vt[0]
                        for t2 in range(1, t + 1):
                            s = s + ws_ref[(g * DEC_SEQ + t) * DEC_SEQ + t2] * vt[t2]
                        s_rows.append(s + bs_ref[g * DEC_SEQ + t])
                    s = jnp.concatenate(s_rows, axis=0)
                else:
                    w = jnp.where(causal, ws_ref[g], 0.0).astype(bf16)
                    bias = bs_ref[:, g:g + 1]
                    vb = vn.astype(bf16)
                    s = jnp.concatenate(
                        [_dot(w, vb[c * A_CHUNK:(c + 1) * A_CHUNK]) + bias for c in range(tm // A_CHUNK)], axis=0)
                us_ref[:, ucols] = (u[:, ucols] * s).astype(bf16)
            y = _dot(us_ref[...], wout_ref[0].astype(bf16))
            if k == 0:
                o_ref[...] = x_ref[...] + y
            else:
                o_ref[...] += y


def mixer_a(x, g, w_in, norm_v, w_sp, b_sp, w_out, layer, *, tm, sample):
    m, d = x.shape
    nj = 2 * A_NBLK
    row = lambda w: pl.BlockSpec((tm, w), lambda i, j: (i, 0))
    full = lambda *shape: pl.BlockSpec(shape, lambda i, j: (0,) * len(shape))
    smem = pl.BlockSpec(memory_space=pltpu.SMEM)
    win_spec = pl.BlockSpec((1, d, A_BLK), lambda i, j: (layer, 0, (j + A_NBLK) % nj))
    wout_spec = pl.BlockSpec((1, A_BLK, d), lambda i, j: (layer, jnp.maximum(j - A_NBLK, 0), 0))
    if sample:
        in_specs = [smem, smem, row(d), full(1, d), win_spec, full(1, A_HALF), wout_spec]
        args = (w_sp, b_sp, x, g, w_in, norm_v, w_out)
        out_specs = [row(d), row(A_HALF)]
        out_shape = [jax.ShapeDtypeStruct((m, d), f32), jax.ShapeDtypeStruct((m, A_HALF), f32)]
    else:
        in_specs = [row(d), full(1, d), win_spec, full(1, A_HALF), full(A_GROUPS, A_CHUNK, A_CHUNK),
                    full(A_CHUNK, A_GROUPS), wout_spec]
        args = (x, g, w_in, norm_v, w_sp, b_sp, w_out)
        out_specs = row(d)
        out_shape = jax.ShapeDtypeStruct((m, d), f32)
    return pl.pallas_call(
        functools.partial(_mixer_a_kernel, sample=sample),
        grid=(m // tm, nj),
        in_specs=in_specs,
        out_specs=out_specs,
        out_shape=out_shape,
        scratch_shapes=[pltpu.VMEM((tm, d), bf16), pltpu.VMEM((tm, A_HALF), f32), pltpu.VMEM((tm, 1), f32),
                        pltpu.VMEM((tm, A_BLK), bf16)],
        compiler_params=_params("parallel", "arbitrary"),
        name="mixer_a_sample" if sample else "mixer_a_prompt",
    )(*args)


def _bucket_table():
    i = np.arange(B_BLOCK)[:, None]
    j = np.arange(2 * B_BLOCK)[None, :]
    n = np.maximum(B_BLOCK + i - j, 0)
    max_exact = N_BUCKETS // 2
    nf = np.maximum(n, 1).astype(np.float64)
    val = np.log(nf / max_exact) / math.log(MAX_DISTANCE / max_exact) * (N_BUCKETS - max_exact)
    in_window = (n >= max_exact) & (n < B_WINDOW)
    assert np.all(np.abs(val - np.round(val))[in_window & (n != max_exact)] > 1e-3)
    large = np.minimum(max_exact + np.floor(val + 1e-9).astype(np.int64), N_BUCKETS - 1)
    return np.where(n < max_exact, n, large).astype(np.int32)


def _bias_table_kernel(rb_ref, bk_ref, o_ref):
    bk = bk_ref[...]
    for h in range(B_HEADS):
        acc = jnp.zeros(bk.shape, f32)
        for b in range(N_BUCKETS):
            acc = jnp.where(bk == b, rb_ref[b * B_HEADS + h], acc)
        o_ref[h] = acc


def bias_table(rel_bias):
    return pl.pallas_call(
        _bias_table_kernel,
        in_specs=[pl.BlockSpec(memory_space=pltpu.SMEM), pl.BlockSpec(memory_space=pltpu.VMEM)],
        out_specs=pl.BlockSpec(memory_space=pltpu.VMEM),
        out_shape=jax.ShapeDtypeStruct((B_HEADS, B_BLOCK, 2 * B_BLOCK), f32),
        name="bias_table",
    )(rel_bias.reshape(-1), jnp.asarray(_bucket_table()))


def _softmax_with_sink(logits, sink):
    m = jnp.maximum(jnp.max(logits, axis=-1, keepdims=True), sink)
    p = jnp.exp(logits - m)
    return p, jnp.sum(p, axis=-1, keepdims=True) + jnp.exp(sink - m)


def _rms_head_pairs(x, g2, lo):
    sq = x * x
    s_lo = jnp.sum(jnp.where(lo, sq, 0.0), axis=-1, keepdims=True)
    s_hi = jnp.sum(jnp.where(lo, 0.0, sq), axis=-1, keepdims=True)
    r = lax.rsqrt(jnp.where(lo, s_lo, s_hi) * (1.0 / B_HEAD_DIM) + EPS)
    return x * r * g2


def _attn_prompt_kernel(sink_ref, qkv_ref, tab_ref, qn_ref, kn_ref, o_ref, ko_ref, vo_ref,
                        kband_ref, vband_ref, q_ref, p_ref, rhs_ref):
    n = pl.program_id(1)
    T = B_BLOCK

    @pl.when(n == 0)
    def _():
        kband_ref[0:T, :] = jnp.zeros((T, B_KV_DIM), f32)
        vband_ref[0:T, :] = jnp.zeros((T, B_KV_DIM), f32)
        rhs_ref[...] = jnp.ones(rhs_ref.shape, bf16)

    lo = lax.broadcasted_iota(jnp.int32, (1, LANES), 1) < B_HEAD_DIM
    qn2 = qn_ref[...]
    kn2 = kn_ref[...]
    for t in range(B_KV_DIM // LANES):
        lanes = slice(t * LANES, (t + 1) * LANES)
        k2 = _rms_head_pairs(qkv_ref[:, B_Q_DIM + t * LANES:B_Q_DIM + (t + 1) * LANES], kn2, lo)
        ko_ref[0, :, lanes] = k2
        kband_ref[T:, lanes] = k2
    v = qkv_ref[:, B_Q_DIM + B_KV_DIM:]
    vo_ref[0] = v
    vband_ref[T:, :] = v
    for t in range(B_Q_DIM // LANES):
        q_ref[t * T:(t + 1) * T, :] = _rms_head_pairs(qkv_ref[:, t * LANES:(t + 1) * LANES], qn2, lo).astype(bf16)

    key_ops = {}
    band_row = lax.broadcasted_iota(jnp.int32, (2 * T, LANES), 0)
    for t in range(B_KV_DIM // LANES):
        lanes = slice(t * LANES, (t + 1) * LANES)
        kt = kband_ref[:, lanes]
        kr = pltpu.roll(kt, B_HEAD_DIM, axis=1)
        vt = jnp.where(band_row == 0, 0.0, vband_ref[:, lanes])
        vr = pltpu.roll(vt, B_HEAD_DIM, axis=1)
        hi = jnp.logical_not(lo)
        for half, (ksrc, vsrc) in enumerate(((kt, vt), (kr, vr))):
            g_lo, g_hi = (2 * t, 2 * t + 1) if half == 0 else (2 * t + 1, 2 * t)
            key_ops[(g_lo, 0)] = jnp.where(lo, ksrc, 0.0).astype(bf16)
            key_ops[(g_hi, 1)] = jnp.where(hi, ksrc, 0.0).astype(bf16)
            rhs_ref[g_lo * 2 + 0, :, 0:LANES] = jnp.where(lo, vsrc, 1.0).astype(bf16)
            rhs_ref[g_hi * 2 + 1, :, 0:LANES] = jnp.where(hi, vsrc, 1.0).astype(bf16)

    i = lax.broadcasted_iota(jnp.int32, (T, 2 * T), 0)
    j = lax.broadcasted_iota(jnp.int32, (T, 2 * T), 1)
    first_key = jnp.where(n == 0, T, 0)
    valid = (j > i) & (j <= i + B_WINDOW) & (j >= first_key)
    sink_col = j == 0
    lo_t = lax.broadcasted_iota(jnp.int32, (T, LANES), 1) < B_HEAD_DIM
    for g in range(B_KV_HEADS):
        q2 = q_ref[2 * g * T:(2 * g + 2) * T, :]
        res = []
        for half in range(2):
            logits = _dot_nt(q2, key_ops[(g, half)]) * (B_HEAD_DIM ** -0.5)
            for pair in range(2):
                h = g * B_REP + 2 * pair + half
                l = jnp.where(valid, logits[pair * T:(pair + 1) * T] + tab_ref[h], NEG_INF)
                l = jnp.where(sink_col, sink_ref[h], l)
                p = jnp.exp(l - jnp.max(l, axis=-1, keepdims=True))
                p_ref[half, pair * T:(pair + 1) * T, :] = p.astype(bf16)
            res.append(_dot(p_ref[half], rhs_ref[g * 2 + half]))
        for pair in range(2):
            rows = slice(pair * T, (pair + 1) * T)
            even = res[0][rows, 0:LANES] / res[0][rows, LANES:]
            odd = res[1][rows, 0:LANES] / res[1][rows, LANES:]
            t = 2 * g + pair
            o_ref[:, t * LANES:(t + 1) * LANES] = jnp.where(lo_t, even, odd).astype(bf16)
    kband_ref[0:T, :] = kband_ref[T:, :]
    vband_ref[0:T, :] = vband_ref[T:, :]


def attn_prompt(qkv, table, q_norm, k_norm, sinks):
    nb = SEQ // B_BLOCK
    return pl.pallas_call(
        _attn_prompt_kernel,
        grid=(BATCH, nb),
        in_specs=[
            pl.BlockSpec(memory_space=pltpu.SMEM),
            pl.BlockSpec((B_BLOCK, B_Q_DIM + 2 * B_KV_DIM), lambda b, n: (b * nb + n, 0)),
            pl.BlockSpec((B_HEADS, B_BLOCK, 2 * B_BLOCK), lambda b, n: (0, 0, 0)),
            pl.BlockSpec((1, LANES), lambda b, n: (0, 0)),
            pl.BlockSpec((1, LANES), lambda b, n: (0, 0)),
        ],
        out_specs=[
            pl.BlockSpec((B_BLOCK, B_Q_DIM), lambda b, n: (b * nb + n, 0)),
            pl.BlockSpec((1, B_BLOCK, B_KV_DIM), lambda b, n: (b, 0, 0)),
            pl.BlockSpec((1, B_BLOCK, B_KV_DIM), lambda b, n: (b, 0, 0)),
        ],
        out_shape=[
            jax.ShapeDtypeStruct((PROMPT_ROWS, B_Q_DIM), bf16),
            jax.ShapeDtypeStruct((BATCH, B_BLOCK, B_KV_DIM), f32),
            jax.ShapeDtypeStruct((BATCH, B_BLOCK, B_KV_DIM), f32),
        ],
        scratch_shapes=[
            pltpu.VMEM((2 * B_BLOCK, B_KV_DIM), f32),
            pltpu.VMEM((2 * B_BLOCK, B_KV_DIM), f32),
            pltpu.VMEM((B_Q_DIM // LANES * B_BLOCK, LANES), bf16),
            pltpu.VMEM((2, 2 * B_BLOCK, 2 * B_BLOCK), bf16),
            pltpu.VMEM((2 * B_KV_HEADS, 2 * B_BLOCK, 2 * LANES), bf16),
        ],
        compiler_params=_params("parallel", "arbitrary"),
        name="attn_prompt",
    )(sinks, qkv, table, jnp.tile(q_norm, (1, LANES // B_HEAD_DIM)), jnp.tile(k_norm, (1, LANES // B_HEAD_DIM)))


ATT_BB = 8
ATT_QR = B_REP * DEC_SEQ
ATT_KEYS = B_WINDOW + 2 * DEC_SEQ


def _attn_sample_kernel(sink_ref, q_ref, kn_ref, vn_ref, kc_ref, vc_ref, tab_ref, qn_ref, knm_ref,
                        o_ref, ko_ref, vo_ref, kall_ref, vall_ref):
    qn = qn_ref[...]
    knm = knm_ref[...]
    row = lax.broadcasted_iota(jnp.int32, (ATT_QR, ATT_KEYS), 0)
    j = lax.broadcasted_iota(jnp.int32, (ATT_QR, ATT_KEYS), 1)
    t = row % DEC_SEQ
    valid = (j > t) & (j <= t + B_WINDOW)
    r_col = lax.broadcasted_iota(jnp.int32, (ATT_QR, 1), 0) // DEC_SEQ
    pad = jnp.zeros((ATT_KEYS - B_WINDOW - DEC_SEQ, B_KV_DIM), f32)
    kall_ref[B_WINDOW + DEC_SEQ:, :] = pad
    vall_ref[B_WINDOW + DEC_SEQ:, :] = pad
    for s in range(ATT_BB):
        kc = kc_ref[s]
        vc = vc_ref[s]
        k_new = jnp.concatenate(
            [_rms(kn_ref[s, :, g * B_HEAD_DIM:(g + 1) * B_HEAD_DIM], knm) for g in range(B_KV_HEADS)], axis=1)
        v_new = vn_ref[s]
        ko_ref[s, 0:B_WINDOW - DEC_SEQ, :] = kc[DEC_SEQ:, :]
        ko_ref[s, B_WINDOW - DEC_SEQ:, :] = k_new
        vo_ref[s, 0:B_WINDOW - DEC_SEQ, :] = vc[DEC_SEQ:, :]
        vo_ref[s, B_WINDOW - DEC_SEQ:, :] = v_new
        kall_ref[0:B_WINDOW, :] = kc
        kall_ref[B_WINDOW:B_WINDOW + DEC_SEQ, :] = k_new
        vall_ref[0:B_WINDOW, :] = vc
        vall_ref[B_WINDOW:B_WINDOW + DEC_SEQ, :] = v_new
        q = q_ref[s]
        for g in range(B_KV_HEADS):
            gd = slice(g * B_HEAD_DIM, (g + 1) * B_HEAD_DIM)
            qg = _rms(q[:, gd], qn).astype(bf16)
            keys = kall_ref[:, gd].astype(bf16)
            vals = vall_ref[:, gd].astype(bf16)
            logits = _dot_nt(qg, keys) * (B_HEAD_DIM ** -0.5)
            logits = jnp.where(valid, logits + tab_ref[g], NEG_INF)
            sink = jnp.zeros((ATT_QR, 1), f32)
            for r in range(B_REP):
                sink = jnp.where(r_col == r, sink_ref[g * B_REP + r], sink)
            p, denom = _softmax_with_sink(logits, sink)
            o_ref[s, :, gd] = _dot(p.astype(bf16), vals) / denom


def attn_sample(q_s, k_new, v_new, k_cache, v_cache, table_s, q_norm, k_norm, sinks):
    blk = lambda *shape: pl.BlockSpec((ATT_BB,) + shape, lambda i: (i,) + (0,) * len(shape))
    full = lambda *shape: pl.BlockSpec(shape, lambda i: (0,) * len(shape))
    return pl.pallas_call(
        _attn_sample_kernel,
        grid=(DEC_BATCH // ATT_BB,),
        in_specs=[
            pl.BlockSpec(memory_space=pltpu.SMEM),
            blk(ATT_QR, B_KV_DIM), blk(DEC_SEQ, B_KV_DIM), blk(DEC_SEQ, B_KV_DIM),
            blk(B_WINDOW, B_KV_DIM), blk(B_WINDOW, B_KV_DIM),
            full(B_KV_HEADS, ATT_QR, ATT_KEYS), full(1, B_HEAD_DIM), full(1, B_HEAD_DIM),
        ],
        out_specs=[blk(ATT_QR, B_KV_DIM), blk(B_WINDOW, B_KV_DIM), blk(B_WINDOW, B_KV_DIM)],
        out_shape=[
            jax.ShapeDtypeStruct((DEC_BATCH, ATT_QR, B_KV_DIM), f32),
            jax.ShapeDtypeStruct((DEC_BATCH, B_WINDOW, B_KV_DIM), f32),
            jax.ShapeDtypeStruct((DEC_BATCH, B_WINDOW, B_KV_DIM), f32),
        ],
        scratch_shapes=[pltpu.VMEM((ATT_KEYS, B_KV_DIM), f32), pltpu.VMEM((ATT_KEYS, B_KV_DIM), f32)],
        compiler_params=_params("parallel"),
        name="attn_sample",
    )(sinks, q_s, k_new, v_new, k_cache, v_cache, table_s, q_norm, k_norm)


CONV_PAD = SUBLANES


def _gated_group_norm(y, z, norm_w):
    gt = y * _silu(z)
    parts = []
    for g in range(C_GROUPS):
        gg = gt[:, g * C_GROUP_W:(g + 1) * C_GROUP_W]
        parts.append(gg * lax.rsqrt(jnp.mean(gg * gg, axis=-1, keepdims=True) + EPS))
    return jnp.concatenate(parts, axis=1) * norm_w


LOG2E = math.log2(math.e)


def _expand_heads(v, sel3):
    lane = lax.broadcasted_iota(jnp.int32, (1, LANES), 1)
    v = jnp.where(lane < C_HEADS, v, 0.0)
    hi = v.astype(bf16).astype(f32)
    r1 = v - hi
    mid = r1.astype(bf16).astype(f32)
    lo = r1 - mid
    packed = hi + pltpu.roll(mid, C_HEADS, axis=1) + pltpu.roll(lo, 2 * C_HEADS, axis=1)
    return _dot(packed.astype(bf16), sel3)


def _ssd_prompt_kernel(zx_ref, dtr_ref, cw_ref, cb_ref, dtb_ref, alog_ref, dsk_ref, nw_ref, sel_ref,
                       yn_ref, hfin_ref, cout_ref, xpad_ref, ht_ref, y_ref):
    c = pl.program_id(1)
    T = C_CHUNK

    @pl.when(c == 0)
    def _():
        xpad_ref[0:CONV_PAD, :] = jnp.zeros((CONV_PAD, C_CONV_DIM), f32)
        ht_ref[...] = jnp.zeros(ht_ref.shape, f32)

    xbc = zx_ref[:, C_D_INNER:]
    xpad_ref[CONV_PAD:, :] = xbc
    xp = xpad_ref[...]
    cw = cw_ref[...]
    acc = cb_ref[...]
    for tap in range(C_D_CONV - 1):
        shifted = pltpu.roll(xp, C_D_CONV - 1 - tap, axis=0)[CONV_PAD:, :]
        acc = acc + shifted * cw[tap:tap + 1, :]
    acc = acc + xbc * cw[C_D_CONV - 1:C_D_CONV, :]
    xpad_ref[0:CONV_PAD, :] = xbc[T - CONV_PAD:, :]
    cout_ref[0] = xbc[T - (C_D_CONV - 1):, :]
    act = _silu(acc)
    xs = act[:, :C_D_INNER]
    bm = act[:, C_D_INNER:C_D_INNER + C_BC_DIM]
    cm = act[:, C_D_INNER + C_BC_DIM:]
    xb = xs.astype(bf16)

    dt = _softplus(dtr_ref[...] + dtb_ref[...])
    a_neg = -jnp.exp(alog_ref[...])
    row = lax.broadcasted_iota(jnp.int32, (T, T), 0)
    col = lax.broadcasted_iota(jnp.int32, (T, T), 1)
    causal = row >= col
    acs = _dot_exact_lhs01(causal.astype(f32), dt * a_neg)
    a2 = acs * LOG2E
    sel3 = sel_ref[...]
    e_exp = jnp.exp2(_expand_heads(a2, sel3))
    w_exp = _expand_heads(jnp.exp(acs[T - 1:T, :] - acs) * dt, sel3)
    cdec = e_exp[T - 1:T, :]
    b2_t = a2.T - jnp.log2(dt.T)
    xw = (xs * w_exp).astype(bf16)
    hb = ht_ref[...].astype(bf16)
    dsk = dsk_ref[...]
    lo_t = lax.broadcasted_iota(jnp.int32, (T, LANES), 1) < C_HEAD_DIM

    for g in range(C_GROUPS):
        ns = slice(g * C_D_STATE, (g + 1) * C_D_STATE)
        gs = slice(g * C_GROUP_W, (g + 1) * C_GROUP_W)
        b_g = bm[:, ns]
        c_g = cm[:, ns].astype(bf16)
        cb = _dot_nt(c_g, b_g.astype(bf16))
        yi = _dot(c_g, hb[:, gs])
        ht_ref[:, gs] = ht_ref[:, gs] * cdec[:, gs] + _dot(b_g.T.astype(bf16), xw[:, gs])
        for tt in range(C_GROUP_W // LANES):
            t = g * (C_GROUP_W // LANES) + tt
            lanes = slice(t * LANES, (t + 1) * LANES)
            xt = xb[:, lanes]
            res = []
            for half in range(2):
                h = 2 * t + half
                a_col = jnp.broadcast_to(a2[:, h:h + 1], (T, T))
                b_row = jnp.broadcast_to(b2_t[h:h + 1, :], (T, T))
                w = jnp.where(causal, cb * jnp.exp2(a_col - b_row), 0.0)
                res.append(_dot(w.astype(bf16), xt))
            y_intra = jnp.where(lo_t, res[0], res[1])
            y_ref[:, lanes] = y_intra + e_exp[:, lanes] * yi[:, tt * LANES:(tt + 1) * LANES] + dsk[:, lanes] * xs[:, lanes]

    yn_ref[...] = _gated_group_norm(y_ref[...], zx_ref[:, :C_D_INNER], nw_ref[...]).astype(bf16)

    @pl.when(c == pl.num_programs(1) - 1)
    def _():
        for t in range(C_D_INNER // LANES):
            hfin_ref[0, t * LANES:(t + 1) * LANES, :] = ht_ref[:, t * LANES:(t + 1) * LANES].T


def _head_select3():
    k = np.arange(LANES)[:, None]
    ch = np.arange(C_D_INNER)[None, :] // C_HEAD_DIM
    return jnp.asarray((k % C_HEADS == ch) & (k < 3 * C_HEADS), dtype=bf16)


def ssd_prompt(zx, dtr, conv_w, conv_b, dt_bias, a_log, d_skip, norm_w):
    nc = SEQ // C_CHUNK
    full = lambda *shape: pl.BlockSpec(shape, lambda b, c: (0,) * len(shape))
    return pl.pallas_call(
        _ssd_prompt_kernel,
        grid=(BATCH, nc),
        in_specs=[
            pl.BlockSpec((C_CHUNK, C_D_INNER + C_CONV_DIM), lambda b, c: (b * nc + c, 0)),
            pl.BlockSpec((C_CHUNK, LANES), lambda b, c: (b * nc + c, 0)),
            full(C_D_CONV, C_CONV_DIM), full(1, C_CONV_DIM), full(1, LANES), full(1, LANES),
            full(1, C_D_INNER), full(1, C_D_INNER), full(LANES, C_D_INNER),
        ],
        out_specs=[
            pl.BlockSpec((C_CHUNK, C_D_INNER), lambda b, c: (b * nc + c, 0)),
            pl.BlockSpec((1, C_D_INNER, C_D_STATE), lambda b, c: (b, 0, 0)),
            pl.BlockSpec((1, C_D_CONV - 1, C_CONV_DIM), lambda b, c: (b, 0, 0)),
        ],
        out_shape=[
            jax.ShapeDtypeStruct((PROMPT_ROWS, C_D_INNER), bf16),
            jax.ShapeDtypeStruct((BATCH, C_D_INNER, C_D_STATE), f32),
            jax.ShapeDtypeStruct((BATCH, C_D_CONV - 1, C_CONV_DIM), f32),
        ],
        scratch_shapes=[
            pltpu.VMEM((CONV_PAD + C_CHUNK, C_CONV_DIM), f32),
            pltpu.VMEM((C_D_STATE, C_D_INNER), f32),
            pltpu.VMEM((C_CHUNK, C_D_INNER), f32),
        ],
        compiler_params=_params("parallel", "arbitrary"),
        name="ssd_prompt",
    )(zx, dtr, conv_w, conv_b, dt_bias, a_log, d_skip, norm_w, _head_select3())


SSD_BB = 8
SSD_TP = SUBLANES
_N_PAIRS = DEC_SEQ * (DEC_SEQ + 1) // 2
_N_COEF = _N_PAIRS + 2 * DEC_SEQ


def _ssd_sample_kernel(zx_ref, dtr_ref, cs_ref, h0_ref, cw_ref, cb_ref, dtb_ref, alog_ref, dsk_ref, nw_ref,
                       sel_ref, yn_ref, hn_ref, cout_ref, c_scr, b_scr, xw_scr, yi_scr, cd_scr):
    L = DEC_SEQ
    cw = cw_ref[...]
    xp = [cs_ref[k] for k in range(C_D_CONV - 1)] + [zx_ref[t, :, C_D_INNER:] for t in range(L)]
    for k in range(C_D_CONV - 1):
        cout_ref[k] = xp[L + k]
    act = []
    for t in range(L):
        acc = cb_ref[...]
        for tap in range(C_D_CONV):
            acc = acc + xp[t + tap] * cw[tap:tap + 1, :]
        act.append(_silu(acc))
    xs = [a[:, :C_D_INNER] for a in act]
    bm = [a[:, C_D_INNER:C_D_INNER + C_BC_DIM] for a in act]
    cm = [a[:, C_D_INNER + C_BC_DIM:] for a in act]

    a_neg = -jnp.exp(alog_ref[...])
    dt = [_softplus(dtr_ref[t] + dtb_ref[...]) for t in range(L)]
    acs = []
    for t in range(L):
        acs.append(dt[t] * a_neg if t == 0 else acs[t - 1] + dt[t] * a_neg)

    lane_group = lax.broadcasted_iota(jnp.int32, (SSD_BB, LANES), 1) // C_REP
    coefs = []
    for t in range(L):
        for t2 in range(t + 1):
            cbh = jnp.zeros((SSD_BB, LANES), f32)
            for g in range(C_GROUPS):
                ns = slice(g * C_D_STATE, (g + 1) * C_D_STATE)
                cbg = jnp.sum(cm[t][:, ns] * bm[t2][:, ns], axis=-1, keepdims=True)
                cbh = jnp.where(lane_group == g, cbg, cbh)
            coefs.append(cbh * jnp.exp(acs[t] - acs[t2]) * dt[t2])
    for t in range(L):
        coefs.append(jnp.exp(acs[t]))
    for t in range(L):
        coefs.append(jnp.exp(acs[L - 1] - acs[t]) * dt[t])
    coef = jnp.concatenate(coefs, axis=0)
    cexp = _dot_exact_rhs01(coef, sel_ref[...])
    cexp = [cexp[k * SSD_BB:(k + 1) * SSD_BB, :] for k in range(_N_COEF)]
    w_intra = cexp[:_N_PAIRS]
    w_inter = cexp[_N_PAIRS:_N_PAIRS + L]
    w_state = cexp[_N_PAIRS + L:]

    cd = jnp.concatenate([jnp.exp(acs[L - 1]), jnp.zeros((LANES - SSD_BB, LANES), f32)], axis=0)
    cd_t = cd.T
    for s in range(SSD_BB):
        cd_scr[s] = jnp.broadcast_to(cd_t[0:C_HEADS, s:s + 1], (C_HEADS, C_D_STATE))

    zeros_tail = jnp.zeros((SSD_BB, SSD_TP - L, C_D_INNER), f32)
    c_scr[:, L:, :] = zeros_tail[:, :, :C_BC_DIM]
    b_scr[:, L:, :] = zeros_tail[:, :, :C_BC_DIM]
    xw_scr[:, L:, :] = zeros_tail
    for t in range(L):
        xw_t = xs[t] * w_state[t]
        for s in range(SSD_BB):
            c_scr[s, t:t + 1, :] = cm[t][s:s + 1, :]
            b_scr[s, t:t + 1, :] = bm[t][s:s + 1, :]
            xw_scr[s, t:t + 1, :] = xw_t[s:s + 1, :]

    for s in range(SSD_BB):
        for g in range(C_GROUPS):
            ns = slice(g * C_D_STATE, (g + 1) * C_D_STATE)
            gs = slice(g * C_GROUP_W, (g + 1) * C_GROUP_W)
            h0 = h0_ref[s, gs, :]
            yi = _dot_nt(c_scr[s, :, ns].astype(bf16), h0.astype(bf16))
            for t in range(L):
                yi_scr[t, s:s + 1, gs] = yi[t:t + 1, :]
            st = _dot_tn(xw_scr[s, :, gs].astype(bf16), b_scr[s, :, ns].astype(bf16))
            for r in range(C_REP):
                h = g * C_REP + r
                rs = slice(r * C_HEAD_DIM, (r + 1) * C_HEAD_DIM)
                scale = cd_scr[s, h:h + 1, :]
                hn_ref[s, h * C_HEAD_DIM:(h + 1) * C_HEAD_DIM, :] = h0[rs, :] * scale + st[rs, :]

    dsk = dsk_ref[...]
    nw = nw_ref[...]
    pair = 0
    for t in range(L):
        y = w_inter[t] * yi_scr[t] + dsk * xs[t]
        for t2 in range(t + 1):
            y = y + w_intra[pair] * xs[t2]
            pair += 1
        yn_ref[t] = _gated_group_norm(y, zx_ref[t, :, :C_D_INNER], nw).astype(bf16)


def ssd_sample(zx_t, dtr_t, conv_state_t, h0, conv_w, conv_b, dt_bias, a_log, d_skip, norm_w, sel):
    tmaj = lambda n, w: pl.BlockSpec((n, SSD_BB, w), lambda i: (0, i, 0))
    full = lambda *shape: pl.BlockSpec(shape, lambda i: (0,) * len(shape))
    return pl.pallas_call(
        _ssd_sample_kernel,
        grid=(DEC_BATCH // SSD_BB,),
        in_specs=[
            tmaj(DEC_SEQ, C_D_INNER + C_CONV_DIM), tmaj(DEC_SEQ, LANES), tmaj(C_D_CONV - 1, C_CONV_DIM),
            pl.BlockSpec((SSD_BB, C_D_INNER, C_D_STATE), lambda i: (i, 0, 0)),
            full(C_D_CONV, C_CONV_DIM), full(1, C_CONV_DIM), full(1, LANES), full(1, LANES),
            full(1, C_D_INNER), full(1, C_D_INNER), full(LANES, C_D_INNER),
        ],
        out_specs=[
            tmaj(DEC_SEQ, C_D_INNER),
            pl.BlockSpec((SSD_BB, C_D_INNER, C_D_STATE), lambda i: (i, 0, 0)),
            tmaj(C_D_CONV - 1, C_CONV_DIM),
        ],
        out_shape=[
            jax.ShapeDtypeStruct((DEC_SEQ, DEC_BATCH, C_D_INNER), bf16),
            jax.ShapeDtypeStruct((DEC_BATCH, C_D_INNER, C_D_STATE), f32),
            jax.ShapeDtypeStruct((C_D_CONV - 1, DEC_BATCH, C_CONV_DIM), f32),
        ],
        scratch_shapes=[
            pltpu.VMEM((SSD_BB, SSD_TP, C_BC_DIM), f32),
            pltpu.VMEM((SSD_BB, SSD_TP, C_BC_DIM), f32),
            pltpu.VMEM((SSD_BB, SSD_TP, C_D_INNER), f32),
            pltpu.VMEM((DEC_SEQ, SSD_BB, C_D_INNER), f32),
            pltpu.VMEM((SSD_BB, C_HEADS, C_D_STATE), f32),
        ],
        compiler_params=_params("parallel"),
        name="ssd_sample",
    )(zx_t, dtr_t, conv_state_t, h0, conv_w, conv_b, dt_bias, a_log, d_skip, norm_w, sel)


def _pad_lanes(v):
    return jnp.pad(v.astype(f32), (0, LANES - v.shape[0])).reshape(1, LANES)


def _mixer_a(xp, xs, g, j, w_in, norm_v, w_sp, b_sp, w_out):
    nv = norm_v.reshape(1, A_HALF)
    xp = mixer_a(xp, g, w_in, nv, w_sp, b_sp.T, w_out, j, tm=TM_PROMPT, sample=False)
    xs, v_s = mixer_a(xs, g, w_in, nv, w_sp[:, :DEC_SEQ, :DEC_SEQ].reshape(-1), b_sp[:, :DEC_SEQ].reshape(-1),
                      w_out, j, tm=TM_SAMPLE, sample=True)
    return xp, xs, jnp.swapaxes(v_s.reshape(DEC_SEQ, DEC_BATCH, A_HALF), 0, 1)


def _mixer_b(xp, xs, g, j, k_cache, v_cache, w_qkv, q_norm, k_norm, sinks, rel_bias):
    qn = q_norm.reshape(1, B_HEAD_DIM)
    kn = k_norm.reshape(1, B_HEAD_DIM)
    table = bias_table(rel_bias)
    n_qkv = B_Q_DIM + 2 * B_KV_DIM

    qkv_p = norm_matmul(xp, g, w_qkv, j, n_qkv, tm=TM_PROMPT, tn=512)
    o_p, k_p, v_p = attn_prompt(qkv_p, table, qn, kn, sinks)

    qkv_s = norm_matmul(xs, g, w_qkv, j, n_qkv, tm=TM_SAMPLE, tn=512).reshape(DEC_SEQ, DEC_BATCH, -1)
    q_s = qkv_s[:, :, :B_Q_DIM].reshape(DEC_SEQ, DEC_BATCH, B_KV_HEADS, B_REP, B_HEAD_DIM)
    q_s = q_s.transpose(1, 3, 0, 2, 4).reshape(DEC_BATCH, ATT_QR, B_KV_DIM)
    k_new = jnp.swapaxes(qkv_s[:, :, B_Q_DIM:B_Q_DIM + B_KV_DIM], 0, 1)
    v_new = jnp.swapaxes(qkv_s[:, :, B_Q_DIM + B_KV_DIM:], 0, 1)
    table_s = table[:, :DEC_SEQ, :ATT_KEYS].reshape(B_KV_HEADS, ATT_QR, ATT_KEYS)
    o_s, k_s, v_s = attn_sample(q_s, k_new, v_new,
                                k_cache.reshape(DEC_BATCH, B_WINDOW, B_KV_DIM),
                                v_cache.reshape(DEC_BATCH, B_WINDOW, B_KV_DIM),
                                table_s, qn, kn, sinks)
    o_s = o_s.reshape(DEC_BATCH, B_REP, DEC_SEQ, B_KV_HEADS, B_HEAD_DIM).transpose(2, 0, 3, 1, 4)
    o_s = o_s.reshape(SAMPLE_ROWS, B_Q_DIM).astype(bf16)
    kv_shape_p = (BATCH, B_WINDOW, B_KV_HEADS, B_HEAD_DIM)
    kv_shape_s = (DEC_BATCH, B_WINDOW, B_KV_HEADS, B_HEAD_DIM)
    return o_p, o_s, k_p.reshape(kv_shape_p), v_p.reshape(kv_shape_p), k_s.reshape(kv_shape_s), v_s.reshape(kv_shape_s)


def _mixer_c(xp, xs, g, j, h0, conv_state, w_in, conv_w, conv_b, dt_bias, a_log, d_skip, norm_w):
    n_zx = C_D_INNER + C_CONV_DIM
    w_dt = jnp.pad(w_in[j, :, n_zx:], ((0, 0), (0, LANES - C_HEADS)))
    cb = conv_b.reshape(1, C_CONV_DIM)
    dtb = _pad_lanes(dt_bias)
    alog = _pad_lanes(a_log)
    dsk = jnp.repeat(d_skip.astype(f32), C_HEAD_DIM).reshape(1, C_D_INNER)
    nw = norm_w.reshape(1, C_D_INNER)

    zx_p, dtr_p = norm_matmul(xp, g, w_in, j, n_zx, tm=TM_PROMPT, tn=1024, w_tail=w_dt)
    yn_p, h_p, conv_p = ssd_prompt(zx_p, dtr_p, conv_w, cb, dtb, alog, dsk, nw)

    zx_s, dtr_s = norm_matmul(xs, g, w_in, j, n_zx, tm=TM_SAMPLE, tn=1024, w_tail=w_dt)
    sel = (jnp.arange(LANES)[:, None] == jnp.arange(C_D_INNER)[None, :] // C_HEAD_DIM).astype(f32)
    yn_s, h_s, conv_s = ssd_sample(
        zx_s.reshape(DEC_SEQ, DEC_BATCH, -1), dtr_s.reshape(DEC_SEQ, DEC_BATCH, LANES),
        jnp.swapaxes(conv_state, 0, 1), h0.reshape(DEC_BATCH, C_D_INNER, C_D_STATE),
        conv_w, cb, dtb, alog, dsk, nw, sel)
    st_shape = (C_HEADS, C_HEAD_DIM, C_D_STATE)
    return (yn_p, yn_s.reshape(SAMPLE_ROWS, C_D_INNER), h_p.reshape((BATCH,) + st_shape), conv_p,
            h_s.reshape((DEC_BATCH,) + st_shape), jnp.swapaxes(conv_s, 0, 1))


def kernel(x_prompt, x_sample, cache_swa_k, cache_swa_v, state_ssm, state_conv, norm_mixer, norm_mlp, mlp_w_up, mlp_w_down, a_w_in, a_norm_v, a_w_spatial, a_b_spatial, a_w_out, b_w_qkv, b_q_norm, b_k_norm, b_sinks, rel_bias, b_w_out, c_w_in, c_conv_w, c_conv_b, c_dt_bias, c_a_log, c_d, c_norm, c_w_out):
    xp = x_prompt.reshape(PROMPT_ROWS, D_MODEL)
    xs = jnp.swapaxes(x_sample, 0, 1).reshape(SAMPLE_ROWS, D_MODEL)
    chunk_v_s = []
    swa_kp, swa_vp, swa_ks, swa_vs = [], [], [], []
    ssm_p, conv_p, ssm_s, conv_s = [], [], [], []
    a_w_in, a_w_out, c_w_in = a_w_in.astype(bf16), a_w_out.astype(bf16), c_w_in.astype(bf16)
    b_w_out, c_w_out = b_w_out.astype(bf16), c_w_out.astype(bf16)
    for i in range(DEPTH):
        kind = i % N_MIXERS
        j = i // N_MIXERS
        g = norm_mixer[i].reshape(1, D_MODEL)
        proj_p = proj_s = None
        if kind == 0:
            xp, xs, v_new = _mixer_a(xp, xs, g, j, a_w_in, a_norm_v[j], a_w_spatial[j], a_b_spatial[j], a_w_out)
            chunk_v_s.append(v_new)
        elif kind == 1:
            o_p, o_s, kp, vp, ks_, vs_ = _mixer_b(xp, xs, g, j, cache_swa_k[j], cache_swa_v[j], b_w_qkv, b_q_norm[j],
                                                  b_k_norm[j], b_sinks[j], rel_bias)
            proj_p, proj_s = (o_p, b_w_out, j), (o_s, b_w_out, j)
            swa_kp.append(kp); swa_vp.append(vp); swa_ks.append(ks_); swa_vs.append(vs_)
        else:
            y_p, y_s, hp, bp, hs, bs = _mixer_c(xp, xs, g, j, state_ssm[j], state_conv[j], c_w_in, c_conv_w[j],
                                                c_conv_b[j], c_dt_bias[j], c_a_log[j], c_d[j], c_norm[j])
            proj_p, proj_s = (y_p, c_w_out, j), (y_s, c_w_out, j)
            ssm_p.append(hp); conv_p.append(bp); ssm_s.append(hs); conv_s.append(bs)
        gm = norm_mlp[i].reshape(1, D_MODEL)
        xp = mlp(xp, gm, mlp_w_up, mlp_w_down, i, tm=TM_PROMPT, tf=MLP_TF, proj=proj_p)
        xs = mlp(xs, gm, mlp_w_up, mlp_w_down, i, tm=TM_SAMPLE, tf=MLP_TF, proj=proj_s)
    y_prompt = xp.reshape(BATCH, SEQ, D_MODEL)
    y_sample = jnp.swapaxes(xs.reshape(DEC_SEQ, DEC_BATCH, D_MODEL), 0, 1)
    return (y_prompt, y_sample, jnp.stack(chunk_v_s),
            jnp.stack(swa_kp), jnp.stack(swa_vp), jnp.stack(swa_ks), jnp.stack(swa_vs),
            jnp.stack(ssm_p), jnp.stack(conv_p), jnp.stack(ssm_s), jnp.stack(conv_s))
```

```python
import functools
import math

import jax
import jax.numpy as jnp
import numpy as np
from jax import lax
from jax.experimental import pallas as pl
from jax.experimental.pallas import tpu as pltpu

f32 = jnp.float32
bf16 = jnp.bfloat16

D_MODEL = 1024
BATCH = 4
SEQ = 4096
DEPTH = 4
DEC_BATCH = 128
DEC_SEQ = 4
PAST_LEN = 8192
N_MIXERS = 3
D_FF = 4 * D_MODEL
EPS = 1e-6
NEG_INF = -1e30

A_CHUNK = 128
A_D_FFN = 6 * D_MODEL
A_HALF = A_D_FFN // 2
A_GROUPS = 8
A_GROUP_W = A_HALF // A_GROUPS

B_HEADS = 16
B_KV_HEADS = 4
B_HEAD_DIM = 64
B_REP = B_HEADS // B_KV_HEADS
B_WINDOW = 128
B_BLOCK = 128
B_Q_DIM = B_HEADS * B_HEAD_DIM
B_KV_DIM = B_KV_HEADS * B_HEAD_DIM
N_BUCKETS = 32
MAX_DISTANCE = 128

C_D_INNER = 2 * D_MODEL
C_HEAD_DIM = 64
C_HEADS = C_D_INNER // C_HEAD_DIM
C_GROUPS = 4
C_REP = C_HEADS // C_GROUPS
C_D_STATE = 128
C_D_CONV = 4
C_BC_DIM = C_GROUPS * C_D_STATE
C_CONV_DIM = C_D_INNER + 2 * C_BC_DIM
C_GROUP_W = C_D_INNER // C_GROUPS
C_CHUNK = 128

LANES = 128
SUBLANES = 8
VMEM_LIMIT_BYTES = 56 * 1024 * 1024

PROMPT_ROWS = BATCH * SEQ
SAMPLE_ROWS = DEC_BATCH * DEC_SEQ
TM_PROMPT = 1024
TM_SAMPLE = SAMPLE_ROWS
MLP_TF = 1024


def _params(*sem):
    return pltpu.CompilerParams(dimension_semantics=sem, vmem_limit_bytes=VMEM_LIMIT_BYTES)


def _rms(x, g):
    ms = jnp.mean(x * x, axis=-1, keepdims=True)
    return x * lax.rsqrt(ms + EPS) * g


def _gelu(x):
    return 0.5 * x * (1.0 + lax.erf(x * math.sqrt(0.5)))


def _silu(x):
    return x * jax.nn.sigmoid(x)


def _softplus(x):
    return jnp.maximum(x, 0.0) + jnp.log1p(jnp.exp(-jnp.abs(x)))


def _dot(a, b):
    return jnp.dot(a, b, preferred_element_type=f32)


def _dot_nt(a, b):
    return lax.dot_general(a, b, (((1,), (1,)), ((), ())), preferred_element_type=f32)


def _dot_tn(a, b):
    return lax.dot_general(a, b, (((0,), (0,)), ((), ())), preferred_element_type=f32)


def _dot_exact_lhs01(a01, x):
    a = a01.astype(bf16)
    hi = x.astype(bf16)
    r1 = x - hi.astype(f32)
    mid = r1.astype(bf16)
    lo = (r1 - mid.astype(f32)).astype(bf16)
    return _dot(a, hi) + _dot(a, mid) + _dot(a, lo)


def _dot_exact_rhs01(x, b01):
    b = b01.astype(bf16)
    hi = x.astype(bf16)
    r1 = x - hi.astype(f32)
    mid = r1.astype(bf16)
    lo = (r1 - mid.astype(f32)).astype(bf16)
    return _dot(hi, b) + _dot(mid, b) + _dot(lo, b)


def _norm_matmul_kernel(*refs, nj, tail):
    if tail:
        x_ref, g_ref, w_ref, wt_ref, o_ref, ot_ref, xn_ref = refs
    else:
        x_ref, g_ref, w_ref, o_ref, xn_ref = refs
    j = pl.program_id(1)

    @pl.when(j == 0)
    def _():
        xn_ref[...] = _rms(x_ref[...], g_ref[...]).astype(bf16)

    @pl.when(j < nj)
    def _():
        o_ref[...] = _dot(xn_ref[...], w_ref[0].astype(bf16)).astype(o_ref.dtype)

    if tail:
        @pl.when(j == nj)
        def _():
            ot_ref[...] = _dot(xn_ref[...], wt_ref[...].astype(bf16))


def norm_matmul(x, g, w, layer, n, *, tm, tn, w_tail=None, out_dtype=f32):
    m, k = x.shape
    nj = n // tn
    tail = w_tail is not None
    last = nj - 1
    in_specs = [
        pl.BlockSpec((tm, k), lambda i, j: (i, 0)),
        pl.BlockSpec((1, k), lambda i, j: (0, 0)),
        pl.BlockSpec((1, k, tn), lambda i, j: (layer, 0, jnp.minimum(j, last))),
    ]
    out_specs = [pl.BlockSpec((tm, tn), lambda i, j: (i, jnp.minimum(j, last)))]
    out_shape = [jax.ShapeDtypeStruct((m, n), out_dtype)]
    args = [x, g, w]
    if tail:
        in_specs.append(pl.BlockSpec((k, LANES), lambda i, j: (0, 0)))
        out_specs.append(pl.BlockSpec((tm, LANES), lambda i, j: (i, 0)))
        out_shape.append(jax.ShapeDtypeStruct((m, LANES), f32))
        args.append(w_tail)
    out = pl.pallas_call(
        functools.partial(_norm_matmul_kernel, nj=nj, tail=tail),
        grid=(m // tm, nj + (1 if tail else 0)),
        in_specs=in_specs,
        out_specs=out_specs,
        out_shape=out_shape,
        scratch_shapes=[pltpu.VMEM((tm, k), bf16)],
        compiler_params=_params("parallel", "arbitrary"),
        name="norm_matmul",
    )(*args)
    return out if tail else out[0]


def _mlp_kernel(*refs, proj):
    if proj:
        a_ref, wo_ref, x_ref, g_ref, wu_ref, wd_ref, o_ref, xn_ref = refs
    else:
        x_ref, g_ref, wu_ref, wd_ref, o_ref, xn_ref = refs

    @pl.when(pl.program_id(1) == 0)
    def _():
        x = x_ref[...]
        if proj:
            x = x + _dot(a_ref[...], wo_ref[0])
        xn_ref[...] = _rms(x, g_ref[...]).astype(bf16)
        o_ref[...] = x

    h = jnp.maximum(_dot(xn_ref[...], wu_ref[0].astype(bf16)), 0.0)
    o_ref[...] += _dot((h * h).astype(bf16), wd_ref[0].astype(bf16))


def mlp(x, g, w_up, w_down, layer, *, tm, tf, proj=None):
    m, d = x.shape
    ff = w_up.shape[2]
    in_specs = [
        pl.BlockSpec((tm, d), lambda i, j: (i, 0)),
        pl.BlockSpec((1, d), lambda i, j: (0, 0)),
        pl.BlockSpec((1, d, tf), lambda i, j: (layer, 0, j)),
        pl.BlockSpec((1, tf, d), lambda i, j: (layer, j, 0)),
    ]
    args = [x, g, w_up, w_down]
    if proj is not None:
        a, w_o, lo = proj
        k = a.shape[1]
        in_specs = [pl.BlockSpec((tm, k), lambda i, j: (i, 0)),
                    pl.BlockSpec((1, k, d), lambda i, j: (lo, 0, 0), pipeline_mode=pl.Buffered(1))] + in_specs
        args = [a, w_o] + args
    return pl.pallas_call(
        functools.partial(_mlp_kernel, proj=proj is not None),
        grid=(m // tm, ff // tf),
        in_specs=in_specs,
        out_specs=pl.BlockSpec((tm, d), lambda i, j: (i, 0)),
        out_shape=jax.ShapeDtypeStruct((m, d), f32),
        scratch_shapes=[pltpu.VMEM((tm, d), bf16)],
        compiler_params=_params("parallel", "arbitrary"),
        name="mlp",
    )(*args)


A_BLK_GROUPS = 2
A_BLK = A_BLK_GROUPS * A_GROUP_W
A_NBLK = A_HALF // A_BLK


def _mixer_a_kernel(*refs, sample):
    if sample:
        ws_ref, bs_ref, x_ref, g_ref, win_ref, nv_ref, wout_ref, o_ref, vo_ref, xn_ref, v_ref, ssq_ref, us_ref = refs
    else:
        x_ref, g_ref, win_ref, nv_ref, ws_ref, bs_ref, wout_ref, o_ref, xn_ref, v_ref, ssq_ref, us_ref = refs
    j = pl.program_id(1)
    tm = x_ref.shape[0]

    @pl.when(j == 0)
    def _():
        xn_ref[...] = _rms(x_ref[...], g_ref[...]).astype(bf16)
        ssq_ref[...] = jnp.zeros(ssq_ref.shape, f32)

    for k in range(A_NBLK):
        @pl.when(j == k)
        def _(k=k):
            v = _gelu(_dot(xn_ref[...], win_ref[0].astype(bf16)))
            v_ref[:, k * A_BLK:(k + 1) * A_BLK] = v
            ssq_ref[...] += jnp.sum(v * v, axis=-1, keepdims=True)

    if not sample:
        row = lax.broadcasted_iota(jnp.int32, (A_CHUNK, A_CHUNK), 0)
        col = lax.broadcasted_iota(jnp.int32, (A_CHUNK, A_CHUNK), 1)
        causal = row >= col

    for k in range(A_NBLK):
        @pl.when(j == A_NBLK + k)
        def _(k=k):
            u = _gelu(_dot(xn_ref[...], win_ref[0].astype(bf16)))
            rinv = lax.rsqrt(ssq_ref[...] * (1.0 / A_HALF) + EPS)
            for gg in range(A_BLK_GROUPS):
                g = k * A_BLK_GROUPS + gg
                cols = slice(g * A_GROUP_W, (g + 1) * A_GROUP_W)
                ucols = slice(gg * A_GROUP_W, (gg + 1) * A_GROUP_W)
                vn = v_ref[:, cols] * rinv * nv_ref[:, cols]
                if sample:
                    vo_ref[:, cols] = vn
                    vt = [vn[t * DEC_BATCH:(t + 1) * DEC_BATCH] for t in range(DEC_SEQ)]
                    s_rows = []
                    for t in range(DEC_SEQ):
                        s = ws_ref[(g * DEC_SEQ + t) * DEC_SEQ] * vt[0]
                        for t2 in range(1, t + 1):
                            s = s + ws_ref[(g * DEC_SEQ + t) * DEC_SEQ + t2] * vt[t2]
                        s_rows.append(s + bs_ref[g * DEC_SEQ + t])
                    s = jnp.concatenate(s_rows, axis=0)
                else:
                    w = jnp.where(causal, ws_ref[g], 0.0).astype(bf16)
                    bias = bs_ref[:, g:g + 1]
                    vb = vn.astype(bf16)
                    s = jnp.concatenate(
                        [_dot(w, vb[c * A_CHUNK:(c + 1) * A_CHUNK]) + bias for c in range(tm // A_CHUNK)], axis=0)
                us_ref[:, ucols] = (u[:, ucols] * s).astype(bf16)
            y = _dot(us_ref[...], wout_ref[0].astype(bf16))
            if k == 0:
                o_ref[...] = x_ref[...] + y
            else:
                o_ref[...] += y


def mixer_a(x, g, w_in, norm_v, w_sp, b_sp, w_out, layer, *, tm, sample):
    m, d = x.shape
    nj = 2 * A_NBLK
    row = lambda w: pl.BlockSpec((tm, w), lambda i, j: (i, 0))
    full = lambda *shape: pl.BlockSpec(shape, lambda i, j: (0,) * len(shape))
    smem = pl.BlockSpec(memory_space=pltpu.SMEM)
    win_spec = pl.BlockSpec((1, d, A_BLK), lambda i, j: (layer, 0, (j + A_NBLK) % nj))
    wout_spec = pl.BlockSpec((1, A_BLK, d), lambda i, j: (layer, jnp.maximum(j - A_NBLK, 0), 0))
    if sample:
        in_specs = [smem, smem, row(d), full(1, d), win_spec, full(1, A_HALF), wout_spec]
        args = (w_sp, b_sp, x, g, w_in, norm_v, w_out)
        out_specs = [row(d), row(A_HALF)]
        out_shape = [jax.ShapeDtypeStruct((m, d), f32), jax.ShapeDtypeStruct((m, A_HALF), f32)]
    else:
        in_specs = [row(d), full(1, d), win_spec, full(1, A_HALF), full(A_GROUPS, A_CHUNK, A_CHUNK),
                    full(A_CHUNK, A_GROUPS), wout_spec]
        args = (x, g, w_in, norm_v, w_sp, b_sp, w_out)
        out_specs = row(d)
        out_shape = jax.ShapeDtypeStruct((m, d), f32)
    return pl.pallas_call(
        functools.partial(_mixer_a_kernel, sample=sample),
        grid=(m // tm, nj),
        in_specs=in_specs,
        out_specs=out_specs,
        out_shape=out_shape,
        scratch_shapes=[pltpu.VMEM((tm, d), bf16), pltpu.VMEM((tm, A_HALF), f32), pltpu.VMEM((tm, 1), f32),
                        pltpu.VMEM((tm, A_BLK), bf16)],
        compiler_params=_params("parallel", "arbitrary"),
        name="mixer_a_sample" if sample else "mixer_a_prompt",
    )(*args)


def _bucket_table():
    i = np.arange(B_BLOCK)[:, None]
    j = np.arange(2 * B_BLOCK)[None, :]
    n = np.maximum(B_BLOCK + i - j, 0)
    max_exact = N_BUCKETS // 2
    nf = np.maximum(n, 1).astype(np.float64)
    val = np.log(nf / max_exact) / math.log(MAX_DISTANCE / max_exact) * (N_BUCKETS - max_exact)
    in_window = (n >= max_exact) & (n < B_WINDOW)
    assert np.all(np.abs(val - np.round(val))[in_window & (n != max_exact)] > 1e-3)
    large = np.minimum(max_exact + np.floor(val + 1e-9).astype(np.int64), N_BUCKETS - 1)
    return np.where(n < max_exact, n, large).astype(np.int32)


def _bias_table_kernel(rb_ref, bk_ref, o_ref):
    bk = bk_ref[...]
    for h in range(B_HEADS):
        acc = jnp.zeros(bk.shape, f32)
        for b in range(N_BUCKETS):
            acc = jnp.where(bk == b, rb_ref[b * B_HEADS + h], acc)
        o_ref[h] = acc


def bias_table(rel_bias):
    return pl.pallas_call(
        _bias_table_kernel,
        in_specs=[pl.BlockSpec(memory_space=pltpu.SMEM), pl.BlockSpec(memory_space=pltpu.VMEM)],
        out_specs=pl.BlockSpec(memory_space=pltpu.VMEM),
        out_shape=jax.ShapeDtypeStruct((B_HEADS, B_BLOCK, 2 * B_BLOCK), f32),
        name="bias_table",
    )(rel_bias.reshape(-1), jnp.asarray(_bucket_table()))


def _softmax_with_sink(logits, sink):
    m = jnp.maximum(jnp.max(logits, axis=-1, keepdims=True), sink)
    p = jnp.exp(logits - m)
    return p, jnp.sum(p, axis=-1, keepdims=True) + jnp.exp(sink - m)


def _rms_head_pairs(x, g2, lo):
    sq = x * x
    s_lo = jnp.sum(jnp.where(lo, sq, 0.0), axis=-1, keepdims=True)
    s_hi = jnp.sum(jnp.where(lo, 0.0, sq), axis=-1, keepdims=True)
    r = lax.rsqrt(jnp.where(lo, s_lo, s_hi) * (1.0 / B_HEAD_DIM) + EPS)
    return x * r * g2


def _attn_prompt_kernel(sink_ref, qkv_ref, tab_ref, qn_ref, kn_ref, o_ref, ko_ref, vo_ref,
                        kband_ref, vband_ref, q_ref, p_ref, rhs_ref):
    n = pl.program_id(1)
    T = B_BLOCK

    @pl.when(n == 0)
    def _():
        kband_ref[0:T, :] = jnp.zeros((T, B_KV_DIM), f32)
        vband_ref[0:T, :] = jnp.zeros((T, B_KV_DIM), f32)
        rhs_ref[...] = jnp.ones(rhs_ref.shape, bf16)

    lo = lax.broadcasted_iota(jnp.int32, (1, LANES), 1) < B_HEAD_DIM
    qn2 = qn_ref[...]
    kn2 = kn_ref[...]
    for t in range(B_KV_DIM // LANES):
        lanes = slice(t * LANES, (t + 1) * LANES)
        k2 = _rms_head_pairs(qkv_ref[:, B_Q_DIM + t * LANES:B_Q_DIM + (t + 1) * LANES].astype(f32), kn2, lo)
        ko_ref[0, :, lanes] = k2
        kband_ref[T:, lanes] = k2
    v = qkv_ref[:, B_Q_DIM + B_KV_DIM:].astype(f32)
    vo_ref[0] = v
    vband_ref[T:, :] = v
    for t in range(B_Q_DIM // LANES):
        q2 = qkv_ref[:, t * LANES:(t + 1) * LANES].astype(f32)
        q_ref[t * T:(t + 1) * T, :] = _rms_head_pairs(q2, qn2, lo).astype(bf16)

    key_ops = {}
    band_row = lax.broadcasted_iota(jnp.int32, (2 * T, LANES), 0)
    for t in range(B_KV_DIM // LANES):
        lanes = slice(t * LANES, (t + 1) * LANES)
        kt = kband_ref[:, lanes]
        kr = pltpu.roll(kt, B_HEAD_DIM, axis=1)
        vt = jnp.where(band_row == 0, 0.0, vband_ref[:, lanes])
        vr = pltpu.roll(vt, B_HEAD_DIM, axis=1)
        hi = jnp.logical_not(lo)
        for half, (ksrc, vsrc) in enumerate(((kt, vt), (kr, vr))):
            g_lo, g_hi = (2 * t, 2 * t + 1) if half == 0 else (2 * t + 1, 2 * t)
            key_ops[(g_lo, 0)] = jnp.where(lo, ksrc, 0.0).astype(bf16)
            key_ops[(g_hi, 1)] = jnp.where(hi, ksrc, 0.0).astype(bf16)
            rhs_ref[g_lo * 2 + 0, :, 0:LANES] = jnp.where(lo, vsrc, 1.0).astype(bf16)
            rhs_ref[g_hi * 2 + 1, :, 0:LANES] = jnp.where(hi, vsrc, 1.0).astype(bf16)

    i = lax.broadcasted_iota(jnp.int32, (T, 2 * T), 0)
    j = lax.broadcasted_iota(jnp.int32, (T, 2 * T), 1)
    first_key = jnp.where(n == 0, T, 0)
    valid = (j > i) & (j <= i + B_WINDOW) & (j >= first_key)
    sink_col = j == 0
    lo_t = lax.broadcasted_iota(jnp.int32, (T, LANES), 1) < B_HEAD_DIM
    for g in range(B_KV_HEADS):
        q2 = q_ref[2 * g * T:(2 * g + 2) * T, :]
        res = []
        for half in range(2):
            logits = _dot_nt(q2, key_ops[(g, half)]) * (B_HEAD_DIM ** -0.5)
            for pair in range(2):
                h = g * B_REP + 2 * pair + half
                l = jnp.where(valid, logits[pair * T:(pair + 1) * T] + tab_ref[h], NEG_INF)
                l = jnp.where(sink_col, sink_ref[h], l)
                p = jnp.exp(l - jnp.max(l, axis=-1, keepdims=True))
                p_ref[half, pair * T:(pair + 1) * T, :] = p.astype(bf16)
            res.append(_dot(p_ref[half], rhs_ref[g * 2 + half]))
        for pair in range(2):
            rows = slice(pair * T, (pair + 1) * T)
            even = res[0][rows, 0:LANES] / res[0][rows, LANES:]
            odd = res[1][rows, 0:LANES] / res[1][rows, LANES:]
            t = 2 * g + pair
            o_ref[:, t * LANES:(t + 1) * LANES] = jnp.where(lo_t, even, odd).astype(bf16)
    kband_ref[0:T, :] = kband_ref[T:, :]
    vband_ref[0:T, :] = vband_ref[T:, :]


def attn_prompt(qkv, table, q_norm, k_norm, sinks):
    nb = SEQ // B_BLOCK
    return pl.pallas_call(
        _attn_prompt_kernel,
        grid=(BATCH, nb),
        in_specs=[
            pl.BlockSpec(memory_space=pltpu.SMEM),
            pl.BlockSpec((B_BLOCK, B_Q_DIM + 2 * B_KV_DIM), lambda b, n: (b * nb + n, 0)),
            pl.BlockSpec((B_HEADS, B_BLOCK, 2 * B_BLOCK), lambda b, n: (0, 0, 0)),
            pl.BlockSpec((1, LANES), lambda b, n: (0, 0)),
            pl.BlockSpec((1, LANES), lambda b, n: (0, 0)),
        ],
        out_specs=[
            pl.BlockSpec((B_BLOCK, B_Q_DIM), lambda b, n: (b * nb + n, 0)),
            pl.BlockSpec((1, B_BLOCK, B_KV_DIM), lambda b, n: (b, 0, 0)),
            pl.BlockSpec((1, B_BLOCK, B_KV_DIM), lambda b, n: (b, 0, 0)),
        ],
        out_shape=[
            jax.ShapeDtypeStruct((PROMPT_ROWS, B_Q_DIM), bf16),
            jax.ShapeDtypeStruct((BATCH, B_BLOCK, B_KV_DIM), f32),
            jax.ShapeDtypeStruct((BATCH, B_BLOCK, B_KV_DIM), f32),
        ],
        scratch_shapes=[
            pltpu.VMEM((2 * B_BLOCK, B_KV_DIM), f32),
            pltpu.VMEM((2 * B_BLOCK, B_KV_DIM), f32),
            pltpu.VMEM((B_Q_DIM // LANES * B_BLOCK, LANES), bf16),
            pltpu.VMEM((2, 2 * B_BLOCK, 2 * B_BLOCK), bf16),
            pltpu.VMEM((2 * B_KV_HEADS, 2 * B_BLOCK, 2 * LANES), bf16),
        ],
        compiler_params=_params("parallel", "arbitrary"),
        name="attn_prompt",
    )(sinks, qkv, table, jnp.tile(q_norm, (1, LANES // B_HEAD_DIM)), jnp.tile(k_norm, (1, LANES // B_HEAD_DIM)))


ATT_BB = 8
ATT_QR = B_REP * DEC_SEQ
ATT_KEYS = B_WINDOW + 2 * DEC_SEQ


def _attn_sample_kernel(sink_ref, q_ref, kn_ref, vn_ref, kc_ref, vc_ref, tab_ref, qn_ref, knm_ref,
                        o_ref, ko_ref, vo_ref, kall_ref, vall_ref):
    qn = qn_ref[...]
    knm = knm_ref[...]
    row = lax.broadcasted_iota(jnp.int32, (ATT_QR, ATT_KEYS), 0)
    j = lax.broadcasted_iota(jnp.int32, (ATT_QR, ATT_KEYS), 1)
    t = row % DEC_SEQ
    valid = (j > t) & (j <= t + B_WINDOW)
    r_col = lax.broadcasted_iota(jnp.int32, (ATT_QR, 1), 0) // DEC_SEQ
    pad = jnp.zeros((ATT_KEYS - B_WINDOW - DEC_SEQ, B_KV_DIM), f32)
    kall_ref[B_WINDOW + DEC_SEQ:, :] = pad
    vall_ref[B_WINDOW + DEC_SEQ:, :] = pad
    for s in range(ATT_BB):
        kc = kc_ref[s]
        vc = vc_ref[s]
        k_new = jnp.concatenate(
            [_rms(kn_ref[s, :, g * B_HEAD_DIM:(g + 1) * B_HEAD_DIM], knm) for g in range(B_KV_HEADS)], axis=1)
        v_new = vn_ref[s]
        ko_ref[s, 0:B_WINDOW - DEC_SEQ, :] = kc[DEC_SEQ:, :]
        ko_ref[s, B_WINDOW - DEC_SEQ:, :] = k_new
        vo_ref[s, 0:B_WINDOW - DEC_SEQ, :] = vc[DEC_SEQ:, :]
        vo_ref[s, B_WINDOW - DEC_SEQ:, :] = v_new
        kall_ref[0:B_WINDOW, :] = kc
        kall_ref[B_WINDOW:B_WINDOW + DEC_SEQ, :] = k_new
        vall_ref[0:B_WINDOW, :] = vc
        vall_ref[B_WINDOW:B_WINDOW + DEC_SEQ, :] = v_new
        q = q_ref[s]
        for g in range(B_KV_HEADS):
            gd = slice(g * B_HEAD_DIM, (g + 1) * B_HEAD_DIM)
            qg = _rms(q[:, gd], qn).astype(bf16)
            keys = kall_ref[:, gd].astype(bf16)
            vals = vall_ref[:, gd].astype(bf16)
            logits = _dot_nt(qg, keys) * (B_HEAD_DIM ** -0.5)
            logits = jnp.where(valid, logits + tab_ref[g], NEG_INF)
            sink = jnp.zeros((ATT_QR, 1), f32)
            for r in range(B_REP):
                sink = jnp.where(r_col == r, sink_ref[g * B_REP + r], sink)
            p, denom = _softmax_with_sink(logits, sink)
            o_ref[s, :, gd] = _dot(p.astype(bf16), vals) / denom


def attn_sample(q_s, k_new, v_new, k_cache, v_cache, table_s, q_norm, k_norm, sinks):
    blk = lambda *shape: pl.BlockSpec((ATT_BB,) + shape, lambda i: (i,) + (0,) * len(shape))
    full = lambda *shape: pl.BlockSpec(shape, lambda i: (0,) * len(shape))
    return pl.pallas_call(
        _attn_sample_kernel,
        grid=(DEC_BATCH // ATT_BB,),
        in_specs=[
            pl.BlockSpec(memory_space=pltpu.SMEM),
            blk(ATT_QR, B_KV_DIM), blk(DEC_SEQ, B_KV_DIM), blk(DEC_SEQ, B_KV_DIM),
            blk(B_WINDOW, B_KV_DIM), blk(B_WINDOW, B_KV_DIM),
            full(B_KV_HEADS, ATT_QR, ATT_KEYS), full(1, B_HEAD_DIM), full(1, B_HEAD_DIM),
        ],
        out_specs=[blk(ATT_QR, B_KV_DIM), blk(B_WINDOW, B_KV_DIM), blk(B_WINDOW, B_KV_DIM)],
        out_shape=[
            jax.ShapeDtypeStruct((DEC_BATCH, ATT_QR, B_KV_DIM), f32),
            jax.ShapeDtypeStruct((DEC_BATCH, B_WINDOW, B_KV_DIM), f32),
            jax.ShapeDtypeStruct((DEC_BATCH, B_WINDOW, B_KV_DIM), f32),
        ],
        scratch_shapes=[pltpu.VMEM((ATT_KEYS, B_KV_DIM), f32), pltpu.VMEM((ATT_KEYS, B_KV_DIM), f32)],
        compiler_params=_params("parallel"),
        name="attn_sample",
    )(sinks, q_s, k_new, v_new, k_cache, v_cache, table_s, q_norm, k_norm)


CONV_PAD = SUBLANES


def _gated_group_norm(y, z, norm_w):
    gt = y * _silu(z)
    parts = []
    for g in range(C_GROUPS):
        gg = gt[:, g * C_GROUP_W:(g + 1) * C_GROUP_W]
        parts.append(gg * lax.rsqrt(jnp.mean(gg * gg, axis=-1, keepdims=True) + EPS))
    return jnp.concatenate(parts, axis=1) * norm_w


LOG2E = math.log2(math.e)


def _expand_heads(v, sel3):
    lane = lax.broadcasted_iota(jnp.int32, (1, LANES), 1)
    v = jnp.where(lane < C_HEADS, v, 0.0)
    hi = v.astype(bf16).astype(f32)
    r1 = v - hi
    mid = r1.astype(bf16).astype(f32)
    lo = r1 - mid
    packed = hi + pltpu.roll(mid, C_HEADS, axis=1) + pltpu.roll(lo, 2 * C_HEADS, axis=1)
    return _dot(packed.astype(bf16), sel3)


def _ssd_prompt_kernel(zx_ref, dtr_ref, cw_ref, cb_ref, dtb_ref, alog_ref, dsk_ref, nw_ref, sel_ref,
                       yn_ref, hfin_ref, cout_ref, xpad_ref, ht_ref, y_ref):
    c = pl.program_id(1)
    T = C_CHUNK

    @pl.when(c == 0)
    def _():
        xpad_ref[0:CONV_PAD, :] = jnp.zeros((CONV_PAD, C_CONV_DIM), f32)
        ht_ref[...] = jnp.zeros(ht_ref.shape, f32)

    xbc = zx_ref[:, C_D_INNER:].astype(f32)
    xpad_ref[CONV_PAD:, :] = xbc
    xp = xpad_ref[...]
    cw = cw_ref[...]
    acc = cb_ref[...]
    for tap in range(C_D_CONV - 1):
        shifted = pltpu.roll(xp, C_D_CONV - 1 - tap, axis=0)[CONV_PAD:, :]
        acc = acc + shifted * cw[tap:tap + 1, :]
    acc = acc + xbc * cw[C_D_CONV - 1:C_D_CONV, :]
    xpad_ref[0:CONV_PAD, :] = xbc[T - CONV_PAD:, :]
    cout_ref[0] = xbc[T - (C_D_CONV - 1):, :]
    act = _silu(acc)
    xs = act[:, :C_D_INNER]
    bm = act[:, C_D_INNER:C_D_INNER + C_BC_DIM]
    cm = act[:, C_D_INNER + C_BC_DIM:]
    xb = xs.astype(bf16)

    dt = _softplus(dtr_ref[...] + dtb_ref[...])
    a_neg = -jnp.exp(alog_ref[...])
    row = lax.broadcasted_iota(jnp.int32, (T, T), 0)
    col = lax.broadcasted_iota(jnp.int32, (T, T), 1)
    causal = row >= col
    acs = _dot_exact_lhs01(causal.astype(f32), dt * a_neg)
    a2 = acs * LOG2E
    sel3 = sel_ref[...]
    e_exp = jnp.exp2(_expand_heads(a2, sel3))
    w_exp = _expand_heads(jnp.exp(acs[T - 1:T, :] - acs) * dt, sel3)
    cdec = e_exp[T - 1:T, :]
    b2_t = a2.T - jnp.log2(dt.T)
    xw = (xs * w_exp).astype(bf16)
    hb = ht_ref[...].astype(bf16)
    dsk = dsk_ref[...]
    lo_t = lax.broadcasted_iota(jnp.int32, (T, LANES), 1) < C_HEAD_DIM

    for g in range(C_GROUPS):
        ns = slice(g * C_D_STATE, (g + 1) * C_D_STATE)
        gs = slice(g * C_GROUP_W, (g + 1) * C_GROUP_W)
        b_g = bm[:, ns]
        c_g = cm[:, ns].astype(bf16)
        cb = _dot_nt(c_g, b_g.astype(bf16))
        yi = _dot(c_g, hb[:, gs])
        ht_ref[:, gs] = ht_ref[:, gs] * cdec[:, gs] + _dot(b_g.T.astype(bf16), xw[:, gs])
        for tt in range(C_GROUP_W // LANES):
            t = g * (C_GROUP_W // LANES) + tt
            lanes = slice(t * LANES, (t + 1) * LANES)
            xt = xb[:, lanes]
            res = []
            for half in range(2):
                h = 2 * t + half
                a_col = jnp.broadcast_to(a2[:, h:h + 1], (T, T))
                b_row = jnp.broadcast_to(b2_t[h:h + 1, :], (T, T))
                w = jnp.where(causal, cb * jnp.exp2(a_col - b_row), 0.0)
                res.append(_dot(w.astype(bf16), xt))
            y_intra = jnp.where(lo_t, res[0], res[1])
            y_ref[:, lanes] = y_intra + e_exp[:, lanes] * yi[:, tt * LANES:(tt + 1) * LANES] + dsk[:, lanes] * xs[:, lanes]

    yn_ref[...] = _gated_group_norm(y_ref[...], zx_ref[:, :C_D_INNER].astype(f32), nw_ref[...]).astype(bf16)

    @pl.when(c == pl.num_programs(1) - 1)
    def _():
        for t in range(C_D_INNER // LANES):
            hfin_ref[0, t * LANES:(t + 1) * LANES, :] = ht_ref[:, t * LANES:(t + 1) * LANES].T


def _head_select3():
    k = np.arange(LANES)[:, None]
    ch = np.arange(C_D_INNER)[None, :] // C_HEAD_DIM
    return jnp.asarray((k % C_HEADS == ch) & (k < 3 * C_HEADS), dtype=bf16)


def ssd_prompt(zx, dtr, conv_w, conv_b, dt_bias, a_log, d_skip, norm_w):
    nc = SEQ // C_CHUNK
    full = lambda *shape: pl.BlockSpec(shape, lambda b, c: (0,) * len(shape))
    return pl.pallas_call(
        _ssd_prompt_kernel,
        grid=(BATCH, nc),
        in_specs=[
            pl.BlockSpec((C_CHUNK, C_D_INNER + C_CONV_DIM), lambda b, c: (b * nc + c, 0)),
            pl.BlockSpec((C_CHUNK, LANES), lambda b, c: (b * nc + c, 0)),
            full(C_D_CONV, C_CONV_DIM), full(1, C_CONV_DIM), full(1, LANES), full(1, LANES),
            full(1, C_D_INNER), full(1, C_D_INNER), full(LANES, C_D_INNER),
        ],
        out_specs=[
            pl.BlockSpec((C_CHUNK, C_D_INNER), lambda b, c: (b * nc + c, 0)),
            pl.BlockSpec((1, C_D_INNER, C_D_STATE), lambda b, c: (b, 0, 0)),
            pl.BlockSpec((1, C_D_CONV - 1, C_CONV_DIM), lambda b, c: (b, 0, 0)),
        ],
        out_shape=[
            jax.ShapeDtypeStruct((PROMPT_ROWS, C_D_INNER), bf16),
            jax.ShapeDtypeStruct((BATCH, C_D_INNER, C_D_STATE), f32),
            jax.ShapeDtypeStruct((BATCH, C_D_CONV - 1, C_CONV_DIM), f32),
        ],
        scratch_shapes=[
            pltpu.VMEM((CONV_PAD + C_CHUNK, C_CONV_DIM), f32),
            pltpu.VMEM((C_D_STATE, C_D_INNER), f32),
            pltpu.VMEM((C_CHUNK, C_D_INNER), f32),
        ],
        compiler_params=_params("parallel", "arbitrary"),
        name="ssd_prompt",
    )(zx, dtr, conv_w, conv_b, dt_bias, a_log, d_skip, norm_w, _head_select3())


SSD_BB = 8
SSD_TP = SUBLANES
_N_PAIRS = DEC_SEQ * (DEC_SEQ + 1) // 2
_N_COEF = _N_PAIRS + 2 * DEC_SEQ


def _ssd_sample_kernel(zx_ref, dtr_ref, cs_ref, h0_ref, cw_ref, cb_ref, dtb_ref, alog_ref, dsk_ref, nw_ref,
                       sel_ref, yn_ref, hn_ref, cout_ref, c_scr, b_scr, xw_scr, yi_scr, cd_scr):
    L = DEC_SEQ
    cw = cw_ref[...]
    xp = [cs_ref[k] for k in range(C_D_CONV - 1)] + [zx_ref[t, :, C_D_INNER:] for t in range(L)]
    for k in range(C_D_CONV - 1):
        cout_ref[k] = xp[L + k]
    act = []
    for t in range(L):
        acc = cb_ref[...]
        for tap in range(C_D_CONV):
            acc = acc + xp[t + tap] * cw[tap:tap + 1, :]
        act.append(_silu(acc))
    xs = [a[:, :C_D_INNER] for a in act]
    bm = [a[:, C_D_INNER:C_D_INNER + C_BC_DIM] for a in act]
    cm = [a[:, C_D_INNER + C_BC_DIM:] for a in act]

    a_neg = -jnp.exp(alog_ref[...])
    dt = [_softplus(dtr_ref[t] + dtb_ref[...]) for t in range(L)]
    acs = []
    for t in range(L):
        acs.append(dt[t] * a_neg if t == 0 else acs[t - 1] + dt[t] * a_neg)

    lane_group = lax.broadcasted_iota(jnp.int32, (SSD_BB, LANES), 1) // C_REP
    coefs = []
    for t in range(L):
        for t2 in range(t + 1):
            cbh = jnp.zeros((SSD_BB, LANES), f32)
            for g in range(C_GROUPS):
                ns = slice(g * C_D_STATE, (g + 1) * C_D_STATE)
                cbg = jnp.sum(cm[t][:, ns] * bm[t2][:, ns], axis=-1, keepdims=True)
                cbh = jnp.where(lane_group == g, cbg, cbh)
            coefs.append(cbh * jnp.exp(acs[t] - acs[t2]) * dt[t2])
    for t in range(L):
        coefs.append(jnp.exp(acs[t]))
    for t in range(L):
        coefs.append(jnp.exp(acs[L - 1] - acs[t]) * dt[t])
    coef = jnp.concatenate(coefs, axis=0)
    cexp = _dot_exact_rhs01(coef, sel_ref[...])
    cexp = [cexp[k * SSD_BB:(k + 1) * SSD_BB, :] for k in range(_N_COEF)]
    w_intra = cexp[:_N_PAIRS]
    w_inter = cexp[_N_PAIRS:_N_PAIRS + L]
    w_state = cexp[_N_PAIRS + L:]

    cd = jnp.concatenate([jnp.exp(acs[L - 1]), jnp.zeros((LANES - SSD_BB, LANES), f32)], axis=0)
    cd_t = cd.T
    for s in range(SSD_BB):
        cd_scr[s] = jnp.broadcast_to(cd_t[0:C_HEADS, s:s + 1], (C_HEADS, C_D_STATE))

    zeros_tail = jnp.zeros((SSD_BB, SSD_TP - L, C_D_INNER), f32)
    c_scr[:, L:, :] = zeros_tail[:, :, :C_BC_DIM]
    b_scr[:, L:, :] = zeros_tail[:, :, :C_BC_DIM]
    xw_scr[:, L:, :] = zeros_tail
    for t in range(L):
        xw_t = xs[t] * w_state[t]
        for s in range(SSD_BB):
            c_scr[s, t:t + 1, :] = cm[t][s:s + 1, :]
            b_scr[s, t:t + 1, :] = bm[t][s:s + 1, :]
            xw_scr[s, t:t + 1, :] = xw_t[s:s + 1, :]

    for s in range(SSD_BB):
        for g in range(C_GROUPS):
            ns = slice(g * C_D_STATE, (g + 1) * C_D_STATE)
            gs = slice(g * C_GROUP_W, (g + 1) * C_GROUP_W)
            h0 = h0_ref[s, gs, :]
            yi = _dot_nt(c_scr[s, :, ns].astype(bf16), h0.astype(bf16))
            for t in range(L):
                yi_scr[t, s:s + 1, gs] = yi[t:t + 1, :]
            st = _dot_tn(xw_scr[s, :, gs].astype(bf16), b_scr[s, :, ns].astype(bf16))
            for r in range(C_REP):
                h = g * C_REP + r
                rs = slice(r * C_HEAD_DIM, (r + 1) * C_HEAD_DIM)
                scale = cd_scr[s, h:h + 1, :]
                hn_ref[s, h * C_HEAD_DIM:(h + 1) * C_HEAD_DIM, :] = h0[rs, :] * scale + st[rs, :]

    dsk = dsk_ref[...]
    nw = nw_ref[...]
    pair = 0
    for t in range(L):
        y = w_inter[t] * yi_scr[t] + dsk * xs[t]
        for t2 in range(t + 1):
            y = y + w_intra[pair] * xs[t2]
            pair += 1
        yn_ref[t] = _gated_group_norm(y, zx_ref[t, :, :C_D_INNER], nw).astype(bf16)


def ssd_sample(zx_t, dtr_t, conv_state_t, h0, conv_w, conv_b, dt_bias, a_log, d_skip, norm_w, sel):
    tmaj = lambda n, w: pl.BlockSpec((n, SSD_BB, w), lambda i: (0, i, 0))
    full = lambda *shape: pl.BlockSpec(shape, lambda i: (0,) * len(shape))
    return pl.pallas_call(
        _ssd_sample_kernel,
        grid=(DEC_BATCH // SSD_BB,),
        in_specs=[
            tmaj(DEC_SEQ, C_D_INNER + C_CONV_DIM), tmaj(DEC_SEQ, LANES), tmaj(C_D_CONV - 1, C_CONV_DIM),
            pl.BlockSpec((SSD_BB, C_D_INNER, C_D_STATE), lambda i: (i, 0, 0)),
            full(C_D_CONV, C_CONV_DIM), full(1, C_CONV_DIM), full(1, LANES), full(1, LANES),
            full(1, C_D_INNER), full(1, C_D_INNER), full(LANES, C_D_INNER),
        ],
        out_specs=[
            tmaj(DEC_SEQ, C_D_INNER),
            pl.BlockSpec((SSD_BB, C_D_INNER, C_D_STATE), lambda i: (i, 0, 0)),
            tmaj(C_D_CONV - 1, C_CONV_DIM),
        ],
        out_shape=[
            jax.ShapeDtypeStruct((DEC_SEQ, DEC_BATCH, C_D_INNER), bf16),
            jax.ShapeDtypeStruct((DEC_BATCH, C_D_INNER, C_D_STATE), f32),
            jax.ShapeDtypeStruct((C_D_CONV - 1, DEC_BATCH, C_CONV_DIM), f32),
        ],
        scratch_shapes=[
            pltpu.VMEM((SSD_BB, SSD_TP, C_BC_DIM), f32),
            pltpu.VMEM((SSD_BB, SSD_TP, C_BC_DIM), f32),
            pltpu.VMEM((SSD_BB, SSD_TP, C_D_INNER), f32),
            pltpu.VMEM((DEC_SEQ, SSD_BB, C_D_INNER), f32),
            pltpu.VMEM((SSD_BB, C_HEADS, C_D_STATE), f32),
        ],
        compiler_params=_params("parallel"),
        name="ssd_sample",
    )(zx_t, dtr_t, conv_state_t, h0, conv_w, conv_b, dt_bias, a_log, d_skip, norm_w, sel)


def _pad_lanes(v):
    return jnp.pad(v.astype(f32), (0, LANES - v.shape[0])).reshape(1, LANES)


def _mixer_a(xp, xs, g, j, w_in, norm_v, w_sp, b_sp, w_out):
    nv = norm_v.reshape(1, A_HALF)
    xp = mixer_a(xp, g, w_in, nv, w_sp, b_sp.T, w_out, j, tm=TM_PROMPT, sample=False)
    xs, v_s = mixer_a(xs, g, w_in, nv, w_sp[:, :DEC_SEQ, :DEC_SEQ].reshape(-1), b_sp[:, :DEC_SEQ].reshape(-1),
                      w_out, j, tm=TM_SAMPLE, sample=True)
    return xp, xs, jnp.swapaxes(v_s.reshape(DEC_SEQ, DEC_BATCH, A_HALF), 0, 1)


def _mixer_b(xp, xs, g, j, k_cache, v_cache, w_qkv, q_norm, k_norm, sinks, rel_bias):
    qn = q_norm.reshape(1, B_HEAD_DIM)
    kn = k_norm.reshape(1, B_HEAD_DIM)
    table = bias_table(rel_bias)
    n_qkv = B_Q_DIM + 2 * B_KV_DIM

    qkv_p = norm_matmul(xp, g, w_qkv, j, n_qkv, tm=TM_PROMPT, tn=512, out_dtype=bf16)
    o_p, k_p, v_p = attn_prompt(qkv_p, table, qn, kn, sinks)

    qkv_s = norm_matmul(xs, g, w_qkv, j, n_qkv, tm=TM_SAMPLE, tn=512).reshape(DEC_SEQ, DEC_BATCH, -1)
    q_s = qkv_s[:, :, :B_Q_DIM].reshape(DEC_SEQ, DEC_BATCH, B_KV_HEADS, B_REP, B_HEAD_DIM)
    q_s = q_s.transpose(1, 3, 0, 2, 4).reshape(DEC_BATCH, ATT_QR, B_KV_DIM)
    k_new = jnp.swapaxes(qkv_s[:, :, B_Q_DIM:B_Q_DIM + B_KV_DIM], 0, 1)
    v_new = jnp.swapaxes(qkv_s[:, :, B_Q_DIM + B_KV_DIM:], 0, 1)
    table_s = table[:, :DEC_SEQ, :ATT_KEYS].reshape(B_KV_HEADS, ATT_QR, ATT_KEYS)
    o_s, k_s, v_s = attn_sample(q_s, k_new, v_new,
                                k_cache.reshape(DEC_BATCH, B_WINDOW, B_KV_DIM),
                                v_cache.reshape(DEC_BATCH, B_WINDOW, B_KV_DIM),
                                table_s, qn, kn, sinks)
    o_s = o_s.reshape(DEC_BATCH, B_REP, DEC_SEQ, B_KV_HEADS, B_HEAD_DIM).transpose(2, 0, 3, 1, 4)
    o_s = o_s.reshape(SAMPLE_ROWS, B_Q_DIM).astype(bf16)
    kv_shape_p = (BATCH, B_WINDOW, B_KV_HEADS, B_HEAD_DIM)
    kv_shape_s = (DEC_BATCH, B_WINDOW, B_KV_HEADS, B_HEAD_DIM)
    return o_p, o_s, k_p.reshape(kv_shape_p), v_p.reshape(kv_shape_p), k_s.reshape(kv_shape_s), v_s.reshape(kv_shape_s)


def _mixer_c(xp, xs, g, j, h0, conv_state, w_in, conv_w, conv_b, dt_bias, a_log, d_skip, norm_w):
    n_zx = C_D_INNER + C_CONV_DIM
    w_dt = jnp.pad(w_in[j, :, n_zx:], ((0, 0), (0, LANES - C_HEADS)))
    cb = conv_b.reshape(1, C_CONV_DIM)
    dtb = _pad_lanes(dt_bias)
    alog = _pad_lanes(a_log)
    dsk = jnp.repeat(d_skip.astype(f32), C_HEAD_DIM).reshape(1, C_D_INNER)
    nw = norm_w.reshape(1, C_D_INNER)

    zx_p, dtr_p = norm_matmul(xp, g, w_in, j, n_zx, tm=TM_PROMPT, tn=1024, w_tail=w_dt, out_dtype=bf16)
    yn_p, h_p, conv_p = ssd_prompt(zx_p, dtr_p, conv_w, cb, dtb, alog, dsk, nw)

    zx_s, dtr_s = norm_matmul(xs, g, w_in, j, n_zx, tm=TM_SAMPLE, tn=1024, w_tail=w_dt)
    sel = (jnp.arange(LANES)[:, None] == jnp.arange(C_D_INNER)[None, :] // C_HEAD_DIM).astype(f32)
    yn_s, h_s, conv_s = ssd_sample(
        zx_s.reshape(DEC_SEQ, DEC_BATCH, -1), dtr_s.reshape(DEC_SEQ, DEC_BATCH, LANES),
        jnp.swapaxes(conv_state, 0, 1), h0.reshape(DEC_BATCH, C_D_INNER, C_D_STATE),
        conv_w, cb, dtb, alog, dsk, nw, sel)
    st_shape = (C_HEADS, C_HEAD_DIM, C_D_STATE)
    return (yn_p, yn_s.reshape(SAMPLE_ROWS, C_D_INNER), h_p.reshape((BATCH,) + st_shape), conv_p,
            h_s.reshape((DEC_BATCH,) + st_shape), jnp.swapaxes(conv_s, 0, 1))


def kernel(x_prompt, x_sample, cache_swa_k, cache_swa_v, state_ssm, state_conv, norm_mixer, norm_mlp, mlp_w_up, mlp_w_down, a_w_in, a_norm_v, a_w_spatial, a_b_spatial, a_w_out, b_w_qkv, b_q_norm, b_k_norm, b_sinks, rel_bias, b_w_out, c_w_in, c_conv_w, c_conv_b, c_dt_bias, c_a_log, c_d, c_norm, c_w_out):
    xp = x_prompt.reshape(PROMPT_ROWS, D_MODEL)
    xs = jnp.swapaxes(x_sample, 0, 1).reshape(SAMPLE_ROWS, D_MODEL)
    chunk_v_s = []
    swa_kp, swa_vp, swa_ks, swa_vs = [], [], [], []
    ssm_p, conv_p, ssm_s, conv_s = [], [], [], []
    a_w_in, a_w_out, c_w_in = a_w_in.astype(bf16), a_w_out.astype(bf16), c_w_in.astype(bf16)
    b_w_out, c_w_out = b_w_out.astype(bf16), c_w_out.astype(bf16)
    for i in range(DEPTH):
        kind = i % N_MIXERS
        j = i // N_MIXERS
        g = norm_mixer[i].reshape(1, D_MODEL)
        proj_p = proj_s = None
        if kind == 0:
            xp, xs, v_new = _mixer_a(xp, xs, g, j, a_w_in, a_norm_v[j], a_w_spatial[j], a_b_spatial[j], a_w_out)
            chunk_v_s.append(v_new)
        elif kind == 1:
            o_p, o_s, kp, vp, ks_, vs_ = _mixer_b(xp, xs, g, j, cache_swa_k[j], cache_swa_v[j], b_w_qkv, b_q_norm[j],
                                                  b_k_norm[j], b_sinks[j], rel_bias)
            proj_p, proj_s = (o_p, b_w_out, j), (o_s, b_w_out, j)
            swa_kp.append(kp); swa_vp.append(vp); swa_ks.append(ks_); swa_vs.append(vs_)
        else:
            y_p, y_s, hp, bp, hs, bs = _mixer_c(xp, xs, g, j, state_ssm[j], state_conv[j], c_w_in, c_conv_w[j],
                                                c_conv_b[j], c_dt_bias[j], c_a_log[j], c_d[j], c_norm[j])
            proj_p, proj_s = (y_p, c_w_out, j), (y_s, c_w_out, j)
            ssm_p.append(hp); conv_p.append(bp); ssm_s.append(hs); conv_s.append(bs)
        gm = norm_mlp[i].reshape(1, D_MODEL)
        xp = mlp(xp, gm, mlp_w_up, mlp_w_down, i, tm=TM_PROMPT, tf=MLP_TF, proj=proj_p)
        xs = mlp(xs, gm, mlp_w_up, mlp_w_down, i, tm=TM_SAMPLE, tf=MLP_TF, proj=proj_s)
    y_prompt = xp.reshape(BATCH, SEQ, D_MODEL)
    y_sample = jnp.swapaxes(xs.reshape(DEC_SEQ, DEC_BATCH, D_MODEL), 0, 1)
    return (y_prompt, y_sample, jnp.stack(chunk_v_s),
            jnp.stack(swa_kp), jnp.stack(swa_vp), jnp.stack(swa_ks), jnp.stack(swa_vs),
            jnp.stack(ssm_p), jnp.stack(conv_p), jnp.stack(ssm_s), jnp.stack(conv_s))
```

```python
import functools
import math

import jax
import jax.numpy as jnp
import numpy as np
from jax import lax
from jax.experimental import pallas as pl
from jax.experimental.pallas import tpu as pltpu

f32 = jnp.float32
bf16 = jnp.bfloat16

D_MODEL = 1024
BATCH = 4
SEQ = 4096
DEPTH = 4
DEC_BATCH = 128
DEC_SEQ = 4
PAST_LEN = 8192
N_MIXERS = 3
D_FF = 4 * D_MODEL
EPS = 1e-6
NEG_INF = -1e30

A_CHUNK = 128
A_D_FFN = 6 * D_MODEL
A_HALF = A_D_FFN // 2
A_GROUPS = 8
A_GROUP_W = A_HALF // A_GROUPS

B_HEADS = 16
B_KV_HEADS = 4
B_HEAD_DIM = 64
B_REP = B_HEADS // B_KV_HEADS
B_WINDOW = 128
B_BLOCK = 128
B_Q_DIM = B_HEADS * B_HEAD_DIM
B_KV_DIM = B_KV_HEADS * B_HEAD_DIM
N_BUCKETS = 32
MAX_DISTANCE = 128

C_D_INNER = 2 * D_MODEL
C_HEAD_DIM = 64
C_HEADS = C_D_INNER // C_HEAD_DIM
C_GROUPS = 4
C_REP = C_HEADS // C_GROUPS
C_D_STATE = 128
C_D_CONV = 4
C_BC_DIM = C_GROUPS * C_D_STATE
C_CONV_DIM = C_D_INNER + 2 * C_BC_DIM
C_GROUP_W = C_D_INNER // C_GROUPS
C_CHUNK = 128

LANES = 128
SUBLANES = 8
VMEM_LIMIT_BYTES = 56 * 1024 * 1024

PROMPT_ROWS = BATCH * SEQ
SAMPLE_ROWS = DEC_BATCH * DEC_SEQ
TM_PROMPT = 1024
TM_SAMPLE = SAMPLE_ROWS
MLP_TF = 1024


def _params(*sem):
    return pltpu.CompilerParams(dimension_semantics=sem, vmem_limit_bytes=VMEM_LIMIT_BYTES)


def _rms(x, g):
    ms = jnp.mean(x * x, axis=-1, keepdims=True)
    return x * lax.rsqrt(ms + EPS) * g


def _gelu(x):
    return 0.5 * x * (1.0 + lax.erf(x * math.sqrt(0.5)))


def _silu(x):
    return x * jax.nn.sigmoid(x)


def _softplus(x):
    return jnp.maximum(x, 0.0) + jnp.log1p(jnp.exp(-jnp.abs(x)))


def _dot(a, b):
    return jnp.dot(a, b, preferred_element_type=f32)


def _dot_nt(a, b):
    return lax.dot_general(a, b, (((1,), (1,)), ((), ())), preferred_element_type=f32)


def _dot_tn(a, b):
    return lax.dot_general(a, b, (((0,), (0,)), ((), ())), preferred_element_type=f32)


def _dot_exact_lhs01(a01, x):
    a = a01.astype(bf16)
    hi = x.astype(bf16)
    r1 = x - hi.astype(f32)
    mid = r1.astype(bf16)
    lo = (r1 - mid.astype(f32)).astype(bf16)
    return _dot(a, hi) + _dot(a, mid) + _dot(a, lo)


def _dot_exact_rhs01(x, b01):
    b = b01.astype(bf16)
    hi = x.astype(bf16)
    r1 = x - hi.astype(f32)
    mid = r1.astype(bf16)
    lo = (r1 - mid.astype(f32)).astype(bf16)
    return _dot(hi, b) + _dot(mid, b) + _dot(lo, b)


def _norm_matmul_kernel(*refs, nj, tail):
    if tail:
        x_ref, g_ref, w_ref, wt_ref, o_ref, ot_ref, xn_ref = refs
    else:
        x_ref, g_ref, w_ref, o_ref, xn_ref = refs
    j = pl.program_id(1)

    @pl.when(j == 0)
    def _():
        xn_ref[...] = _rms(x_ref[...], g_ref[...]).astype(bf16)

    @pl.when(j < nj)
    def _():
        y = _dot(xn_ref[...], w_ref[0].astype(bf16)).astype(o_ref.dtype)
        if len(o_ref.shape) == 3:
            o_ref[0] = y
        else:
            o_ref[...] = y

    if tail:
        @pl.when(j == nj)
        def _():
            ot_ref[...] = _dot(xn_ref[...], wt_ref[...].astype(bf16))


def norm_matmul(x, g, w, layer, n, *, tm, tn, w_tail=None, out_dtype=f32, blocked=False):
    m, k = x.shape
    nj = n // tn
    tail = w_tail is not None
    last = nj - 1
    if blocked:
        w_spec = pl.BlockSpec((1, k, tn), lambda i, j: (jnp.minimum(j, last), 0, 0))
        out_specs = [pl.BlockSpec((1, tm, tn), lambda i, j: (jnp.minimum(j, last), i, 0))]
        out_shape = [jax.ShapeDtypeStruct((nj, m, tn), out_dtype)]
    else:
        w_spec = pl.BlockSpec((1, k, tn), lambda i, j: (layer, 0, jnp.minimum(j, last)))
        out_specs = [pl.BlockSpec((tm, tn), lambda i, j: (i, jnp.minimum(j, last)))]
        out_shape = [jax.ShapeDtypeStruct((m, n), out_dtype)]
    in_specs = [
        pl.BlockSpec((tm, k), lambda i, j: (i, 0)),
        pl.BlockSpec((1, k), lambda i, j: (0, 0)),
        w_spec,
    ]
    args = [x, g, w]
    if tail:
        in_specs.append(pl.BlockSpec((k, LANES), lambda i, j: (0, 0)))
        out_specs.append(pl.BlockSpec((tm, LANES), lambda i, j: (i, 0)))
        out_shape.append(jax.ShapeDtypeStruct((m, LANES), f32))
        args.append(w_tail)
    out = pl.pallas_call(
        functools.partial(_norm_matmul_kernel, nj=nj, tail=tail),
        grid=(m // tm, nj + (1 if tail else 0)),
        in_specs=in_specs,
        out_specs=out_specs,
        out_shape=out_shape,
        scratch_shapes=[pltpu.VMEM((tm, k), bf16)],
        compiler_params=_params("parallel", "arbitrary"),
        name="norm_matmul",
    )(*args)
    return out if tail else out[0]


def _mlp_kernel(*refs, proj):
    if proj:
        a_ref, wo_ref, x_ref, g_ref, wu_ref, wd_ref, o_ref, xn_ref = refs
    else:
        x_ref, g_ref, wu_ref, wd_ref, o_ref, xn_ref = refs

    @pl.when(pl.program_id(1) == 0)
    def _():
        x = x_ref[...]
        if proj:
            x = x + _dot(a_ref[...], wo_ref[0])
        xn_ref[...] = _rms(x, g_ref[...]).astype(bf16)
        o_ref[...] = x

    h = jnp.maximum(_dot(xn_ref[...], wu_ref[0].astype(bf16)), 0.0)
    o_ref[...] += _dot((h * h).astype(bf16), wd_ref[0].astype(bf16))


def mlp(x, g, w_up, w_down, layer, *, tm, tf, proj=None):
    m, d = x.shape
    ff = w_up.shape[2]
    in_specs = [
        pl.BlockSpec((tm, d), lambda i, j: (i, 0)),
        pl.BlockSpec((1, d), lambda i, j: (0, 0)),
        pl.BlockSpec((1, d, tf), lambda i, j: (layer, 0, j)),
        pl.BlockSpec((1, tf, d), lambda i, j: (layer, j, 0)),
    ]
    args = [x, g, w_up, w_down]
    if proj is not None:
        a, w_o, lo = proj
        k = a.shape[1]
        in_specs = [pl.BlockSpec((tm, k), lambda i, j: (i, 0)),
                    pl.BlockSpec((1, k, d), lambda i, j: (lo, 0, 0), pipeline_mode=pl.Buffered(1))] + in_specs
        args = [a, w_o] + args
    return pl.pallas_call(
        functools.partial(_mlp_kernel, proj=proj is not None),
        grid=(m // tm, ff // tf),
        in_specs=in_specs,
        out_specs=pl.BlockSpec((tm, d), lambda i, j: (i, 0)),
        out_shape=jax.ShapeDtypeStruct((m, d), f32),
        scratch_shapes=[pltpu.VMEM((tm, d), bf16)],
        compiler_params=_params("parallel", "arbitrary"),
        name="mlp",
    )(*args)


A_BLK_GROUPS = 2
A_BLK = A_BLK_GROUPS * A_GROUP_W
A_NBLK = A_HALF // A_BLK


def _mixer_a_kernel(*refs, sample):
    if sample:
        ws_ref, bs_ref, x_ref, g_ref, win_ref, nv_ref, wout_ref, o_ref, vo_ref, xn_ref, v_ref, ssq_ref, us_ref = refs
    else:
        x_ref, g_ref, win_ref, nv_ref, ws_ref, bs_ref, wout_ref, o_ref, xn_ref, v_ref, ssq_ref, us_ref = refs
    j = pl.program_id(1)
    tm = x_ref.shape[0]

    @pl.when(j == 0)
    def _():
        xn_ref[...] = _rms(x_ref[...], g_ref[...]).astype(bf16)
        ssq_ref[...] = jnp.zeros(ssq_ref.shape, f32)

    for k in range(A_NBLK):
        @pl.when(j == k)
        def _(k=k):
            v = _gelu(_dot(xn_ref[...], win_ref[0, 0].astype(bf16)))
            v_ref[:, k * A_BLK:(k + 1) * A_BLK] = v
            ssq_ref[...] += jnp.sum(v * v, axis=-1, keepdims=True)

    if not sample:
        row = lax.broadcasted_iota(jnp.int32, (A_CHUNK, A_CHUNK), 0)
        col = lax.broadcasted_iota(jnp.int32, (A_CHUNK, A_CHUNK), 1)
        causal = row >= col

    for k in range(A_NBLK):
        @pl.when(j == A_NBLK + k)
        def _(k=k):
            u = _gelu(_dot(xn_ref[...], win_ref[0, 0].astype(bf16)))
            rinv = lax.rsqrt(ssq_ref[...] * (1.0 / A_HALF) + EPS)
            for gg in range(A_BLK_GROUPS):
                g = k * A_BLK_GROUPS + gg
                cols = slice(g * A_GROUP_W, (g + 1) * A_GROUP_W)
                ucols = slice(gg * A_GROUP_W, (gg + 1) * A_GROUP_W)
                vn = v_ref[:, cols] * rinv * nv_ref[:, cols]
                if sample:
                    vo_ref[:, cols] = vn
                    vt = [vn[t * DEC_BATCH:(t + 1) * DEC_BATCH] for t in range(DEC_SEQ)]
                    s_rows = []
                    for t in range(DEC_SEQ):
                        s = ws_ref[(g * DEC_SEQ + t) * DEC_SEQ] * vt[0]
                        for t2 in range(1, t + 1):
                            s = s + ws_ref[(g * DEC_SEQ + t) * DEC_SEQ + t2] * vt[t2]
                        s_rows.append(s + bs_ref[g * DEC_SEQ + t])
                    s = jnp.concatenate(s_rows, axis=0)
                else:
                    w = jnp.where(causal, ws_ref[g], 0.0).astype(bf16)
                    bias = bs_ref[:, g:g + 1]
                    vb = vn.astype(bf16)
                    s = jnp.concatenate(
                        [_dot(w, vb[c * A_CHUNK:(c + 1) * A_CHUNK]) + bias for c in range(tm // A_CHUNK)], axis=0)
                us_ref[:, ucols] = (u[:, ucols] * s).astype(bf16)
            y = _dot(us_ref[...], wout_ref[0].astype(bf16))
            if k == 0:
                o_ref[...] = x_ref[...] + y
            else:
                o_ref[...] += y


def mixer_a(x, g, w_in, norm_v, w_sp, b_sp, w_out, layer, *, tm, sample):
    m, d = x.shape
    nj = 2 * A_NBLK
    row = lambda w: pl.BlockSpec((tm, w), lambda i, j: (i, 0))
    full = lambda *shape: pl.BlockSpec(shape, lambda i, j: (0,) * len(shape))
    smem = pl.BlockSpec(memory_space=pltpu.SMEM)
    win_spec = pl.BlockSpec((1, 1, d, A_BLK), lambda i, j: (layer, (j + A_NBLK) % nj, 0, 0))
    wout_spec = pl.BlockSpec((1, A_BLK, d), lambda i, j: (layer, jnp.maximum(j - A_NBLK, 0), 0))
    if sample:
        in_specs = [smem, smem, row(d), full(1, d), win_spec, full(1, A_HALF), wout_spec]
        args = (w_sp, b_sp, x, g, w_in, norm_v, w_out)
        out_specs = [row(d), row(A_HALF)]
        out_shape = [jax.ShapeDtypeStruct((m, d), f32), jax.ShapeDtypeStruct((m, A_HALF), f32)]
    else:
        in_specs = [row(d), full(1, d), win_spec, full(1, A_HALF), full(A_GROUPS, A_CHUNK, A_CHUNK),
                    full(A_CHUNK, A_GROUPS), wout_spec]
        args = (x, g, w_in, norm_v, w_sp, b_sp, w_out)
        out_specs = row(d)
        out_shape = jax.ShapeDtypeStruct((m, d), f32)
    return pl.pallas_call(
        functools.partial(_mixer_a_kernel, sample=sample),
        grid=(m // tm, nj),
        in_specs=in_specs,
        out_specs=out_specs,
        out_shape=out_shape,
        scratch_shapes=[pltpu.VMEM((tm, d), bf16), pltpu.VMEM((tm, A_HALF), f32), pltpu.VMEM((tm, 1), f32),
                        pltpu.VMEM((tm, A_BLK), bf16)],
        compiler_params=_params("parallel", "arbitrary"),
        name="mixer_a_sample" if sample else "mixer_a_prompt",
    )(*args)


def _bucket_table():
    i = np.arange(B_BLOCK)[:, None]
    j = np.arange(2 * B_BLOCK)[None, :]
    n = np.maximum(B_BLOCK + i - j, 0)
    max_exact = N_BUCKETS // 2
    nf = np.maximum(n, 1).astype(np.float64)
    val = np.log(nf / max_exact) / math.log(MAX_DISTANCE / max_exact) * (N_BUCKETS - max_exact)
    in_window = (n >= max_exact) & (n < B_WINDOW)
    assert np.all(np.abs(val - np.round(val))[in_window & (n != max_exact)] > 1e-3)
    large = np.minimum(max_exact + np.floor(val + 1e-9).astype(np.int64), N_BUCKETS - 1)
    return np.where(n < max_exact, n, large).astype(np.int32)


def _bias_table_kernel(rb_ref, bk_ref, o_ref):
    bk = bk_ref[...]
    for h in range(B_HEADS):
        acc = jnp.zeros(bk.shape, f32)
        for b in range(N_BUCKETS):
            acc = jnp.where(bk == b, rb_ref[b * B_HEADS + h], acc)
        o_ref[h] = acc


def bias_table(rel_bias):
    return pl.pallas_call(
        _bias_table_kernel,
        in_specs=[pl.BlockSpec(memory_space=pltpu.SMEM), pl.BlockSpec(memory_space=pltpu.VMEM)],
        out_specs=pl.BlockSpec(memory_space=pltpu.VMEM),
        out_shape=jax.ShapeDtypeStruct((B_HEADS, B_BLOCK, 2 * B_BLOCK), f32),
        name="bias_table",
    )(rel_bias.reshape(-1), jnp.asarray(_bucket_table()))


def _softmax_with_sink(logits, sink):
    m = jnp.maximum(jnp.max(logits, axis=-1, keepdims=True), sink)
    p = jnp.exp(logits - m)
    return p, jnp.sum(p, axis=-1, keepdims=True) + jnp.exp(sink - m)


def _rms_head_pairs(x, g2, lo):
    sq = x * x
    s_lo = jnp.sum(jnp.where(lo, sq, 0.0), axis=-1, keepdims=True)
    s_hi = jnp.sum(jnp.where(lo, 0.0, sq), axis=-1, keepdims=True)
    r = lax.rsqrt(jnp.where(lo, s_lo, s_hi) * (1.0 / B_HEAD_DIM) + EPS)
    return x * r * g2


def _attn_prompt_kernel(sink_ref, qkv_ref, tab_ref, qn_ref, kn_ref, o_ref, ko_ref, vo_ref,
                        kband_ref, vband_ref, q_ref, p_ref, rhs_ref):
    n = pl.program_id(1)
    T = B_BLOCK

    @pl.when(n == 0)
    def _():
        kband_ref[0:T, :] = jnp.zeros((T, B_KV_DIM), f32)
        vband_ref[0:T, :] = jnp.zeros((T, B_KV_DIM), f32)
        rhs_ref[...] = jnp.ones(rhs_ref.shape, bf16)

    lo = lax.broadcasted_iota(jnp.int32, (1, LANES), 1) < B_HEAD_DIM
    qn2 = qn_ref[...]
    kn2 = kn_ref[...]
    for t in range(B_KV_DIM // LANES):
        lanes = slice(t * LANES, (t + 1) * LANES)
        k2 = _rms_head_pairs(qkv_ref[:, B_Q_DIM + t * LANES:B_Q_DIM + (t + 1) * LANES].astype(f32), kn2, lo)
        ko_ref[0, :, lanes] = k2
        kband_ref[T:, lanes] = k2
    v = qkv_ref[:, B_Q_DIM + B_KV_DIM:].astype(f32)
    vo_ref[0] = v
    vband_ref[T:, :] = v
    for t in range(B_Q_DIM // LANES):
        q2 = qkv_ref[:, t * LANES:(t + 1) * LANES].astype(f32)
        q_ref[t * T:(t + 1) * T, :] = _rms_head_pairs(q2, qn2, lo).astype(bf16)

    key_ops = {}
    band_row = lax.broadcasted_iota(jnp.int32, (2 * T, LANES), 0)
    for t in range(B_KV_DIM // LANES):
        lanes = slice(t * LANES, (t + 1) * LANES)
        kt = kband_ref[:, lanes]
        kr = pltpu.roll(kt, B_HEAD_DIM, axis=1)
        vt = jnp.where(band_row == 0, 0.0, vband_ref[:, lanes])
        vr = pltpu.roll(vt, B_HEAD_DIM, axis=1)
        hi = jnp.logical_not(lo)
        for half, (ksrc, vsrc) in enumerate(((kt, vt), (kr, vr))):
            g_lo, g_hi = (2 * t, 2 * t + 1) if half == 0 else (2 * t + 1, 2 * t)
            key_ops[(g_lo, 0)] = jnp.where(lo, ksrc, 0.0).astype(bf16)
            key_ops[(g_hi, 1)] = jnp.where(hi, ksrc, 0.0).astype(bf16)
            rhs_ref[g_lo * 2 + 0, :, 0:LANES] = jnp.where(lo, vsrc, 1.0).astype(bf16)
            rhs_ref[g_hi * 2 + 1, :, 0:LANES] = jnp.where(hi, vsrc, 1.0).astype(bf16)

    i = lax.broadcasted_iota(jnp.int32, (T, 2 * T), 0)
    j = lax.broadcasted_iota(jnp.int32, (T, 2 * T), 1)
    first_key = jnp.where(n == 0, T, 0)
    valid = (j > i) & (j <= i + B_WINDOW) & (j >= first_key)
    sink_col = j == 0
    lo_t = lax.broadcasted_iota(jnp.int32, (T, LANES), 1) < B_HEAD_DIM
    for g in range(B_KV_HEADS):
        q2 = q_ref[2 * g * T:(2 * g + 2) * T, :]
        res = []
        for half in range(2):
            logits = _dot_nt(q2, key_ops[(g, half)]) * (B_HEAD_DIM ** -0.5)
            for pair in range(2):
                h = g * B_REP + 2 * pair + half
                l = jnp.where(valid, logits[pair * T:(pair + 1) * T] + tab_ref[h], NEG_INF)
                l = jnp.where(sink_col, sink_ref[h], l)
                p = jnp.exp(l - jnp.max(l, axis=-1, keepdims=True))
                p_ref[half, pair * T:(pair + 1) * T, :] = p.astype(bf16)
            res.append(_dot(p_ref[half], rhs_ref[g * 2 + half]))
        for pair in range(2):
            rows = slice(pair * T, (pair + 1) * T)
            even = res[0][rows, 0:LANES] / res[0][rows, LANES:]
            odd = res[1][rows, 0:LANES] / res[1][rows, LANES:]
            t = 2 * g + pair
            o_ref[:, t * LANES:(t + 1) * LANES] = jnp.where(lo_t, even, odd).astype(bf16)
    kband_ref[0:T, :] = kband_ref[T:, :]
    vband_ref[0:T, :] = vband_ref[T:, :]


def attn_prompt(qkv, table, q_norm, k_norm, sinks):
    nb = SEQ // B_BLOCK
    return pl.pallas_call(
        _attn_prompt_kernel,
        grid=(BATCH, nb),
        in_specs=[
            pl.BlockSpec(memory_space=pltpu.SMEM),
            pl.BlockSpec((B_BLOCK, B_Q_DIM + 2 * B_KV_DIM), lambda b, n: (b * nb + n, 0)),
            pl.BlockSpec((B_HEADS, B_BLOCK, 2 * B_BLOCK), lambda b, n: (0, 0, 0)),
            pl.BlockSpec((1, LANES), lambda b, n: (0, 0)),
            pl.BlockSpec((1, LANES), lambda b, n: (0, 0)),
        ],
        out_specs=[
            pl.BlockSpec((B_BLOCK, B_Q_DIM), lambda b, n: (b * nb + n, 0)),
            pl.BlockSpec((1, B_BLOCK, B_KV_DIM), lambda b, n: (b, 0, 0)),
            pl.BlockSpec((1, B_BLOCK, B_KV_DIM), lambda b, n: (b, 0, 0)),
        ],
        out_shape=[
            jax.ShapeDtypeStruct((PROMPT_ROWS, B_Q_DIM), bf16),
            jax.ShapeDtypeStruct((BATCH, B_BLOCK, B_KV_DIM), f32),
            jax.ShapeDtypeStruct((BATCH, B_BLOCK, B_KV_DIM), f32),
        ],
        scratch_shapes=[
            pltpu.VMEM((2 * B_BLOCK, B_KV_DIM), f32),
            pltpu.VMEM((2 * B_BLOCK, B_KV_DIM), f32),
            pltpu.VMEM((B_Q_DIM // LANES * B_BLOCK, LANES), bf16),
            pltpu.VMEM((2, 2 * B_BLOCK, 2 * B_BLOCK), bf16),
            pltpu.VMEM((2 * B_KV_HEADS, 2 * B_BLOCK, 2 * LANES), bf16),
        ],
        compiler_params=_params("parallel", "arbitrary"),
        name="attn_prompt",
    )(sinks, qkv, table, jnp.tile(q_norm, (1, LANES // B_HEAD_DIM)), jnp.tile(k_norm, (1, LANES // B_HEAD_DIM)))


ATT_BB = 8
ATT_QR = B_REP * DEC_SEQ
ATT_KEYS = B_WINDOW + 2 * DEC_SEQ


def _attn_sample_kernel(sink_ref, q_ref, kn_ref, vn_ref, kc_ref, vc_ref, tab_ref, qn_ref, knm_ref,
                        o_ref, ko_ref, vo_ref, kall_ref, vall_ref):
    qn = qn_ref[...]
    knm = knm_ref[...]
    row = lax.broadcasted_iota(jnp.int32, (ATT_QR, ATT_KEYS), 0)
    j = lax.broadcasted_iota(jnp.int32, (ATT_QR, ATT_KEYS), 1)
    t = row % DEC_SEQ
    valid = (j > t) & (j <= t + B_WINDOW)
    r_col = lax.broadcasted_iota(jnp.int32, (ATT_QR, 1), 0) // DEC_SEQ
    pad = jnp.zeros((ATT_KEYS - B_WINDOW - DEC_SEQ, B_KV_DIM), f32)
    kall_ref[B_WINDOW + DEC_SEQ:, :] = pad
    vall_ref[B_WINDOW + DEC_SEQ:, :] = pad
    for s in range(ATT_BB):
        kc = kc_ref[s]
        vc = vc_ref[s]
        k_new = jnp.concatenate(
            [_rms(kn_ref[s, :, g * B_HEAD_DIM:(g + 1) * B_HEAD_DIM], knm) for g in range(B_KV_HEADS)], axis=1)
        v_new = vn_ref[s]
        ko_ref[s, 0:B_WINDOW - DEC_SEQ, :] = kc[DEC_SEQ:, :]
        ko_ref[s, B_WINDOW - DEC_SEQ:, :] = k_new
        vo_ref[s, 0:B_WINDOW - DEC_SEQ, :] = vc[DEC_SEQ:, :]
        vo_ref[s, B_WINDOW - DEC_SEQ:, :] = v_new
        kall_ref[0:B_WINDOW, :] = kc
        kall_ref[B_WINDOW:B_WINDOW + DEC_SEQ, :] = k_new
        vall_ref[0:B_WINDOW, :] = vc
        vall_ref[B_WINDOW:B_WINDOW + DEC_SEQ, :] = v_new
        q = q_ref[s]
        for g in range(B_KV_HEADS):
            gd = slice(g * B_HEAD_DIM, (g + 1) * B_HEAD_DIM)
            qg = _rms(q[:, gd], qn).astype(bf16)
            keys = kall_ref[:, gd].astype(bf16)
            vals = vall_ref[:, gd].astype(bf16)
            logits = _dot_nt(qg, keys) * (B_HEAD_DIM ** -0.5)
            logits = jnp.where(valid, logits + tab_ref[g], NEG_INF)
            sink = jnp.zeros((ATT_QR, 1), f32)
            for r in range(B_REP):
                sink = jnp.where(r_col == r, sink_ref[g * B_REP + r], sink)
            p, denom = _softmax_with_sink(logits, sink)
            o_ref[s, :, gd] = _dot(p.astype(bf16), vals) / denom


def attn_sample(q_s, k_new, v_new, k_cache, v_cache, table_s, q_norm, k_norm, sinks):
    blk = lambda *shape: pl.BlockSpec((ATT_BB,) + shape, lambda i: (i,) + (0,) * len(shape))
    full = lambda *shape: pl.BlockSpec(shape, lambda i: (0,) * len(shape))
    return pl.pallas_call(
        _attn_sample_kernel,
        grid=(DEC_BATCH // ATT_BB,),
        in_specs=[
            pl.BlockSpec(memory_space=pltpu.SMEM),
            blk(ATT_QR, B_KV_DIM), blk(DEC_SEQ, B_KV_DIM), blk(DEC_SEQ, B_KV_DIM),
            blk(B_WINDOW, B_KV_DIM), blk(B_WINDOW, B_KV_DIM),
            full(B_KV_HEADS, ATT_QR, ATT_KEYS), full(1, B_HEAD_DIM), full(1, B_HEAD_DIM),
        ],
        out_specs=[blk(ATT_QR, B_KV_DIM), blk(B_WINDOW, B_KV_DIM), blk(B_WINDOW, B_KV_DIM)],
        out_shape=[
            jax.ShapeDtypeStruct((DEC_BATCH, ATT_QR, B_KV_DIM), f32),
            jax.ShapeDtypeStruct((DEC_BATCH, B_WINDOW, B_KV_DIM), f32),
            jax.ShapeDtypeStruct((DEC_BATCH, B_WINDOW, B_KV_DIM), f32),
        ],
        scratch_shapes=[pltpu.VMEM((ATT_KEYS, B_KV_DIM), f32), pltpu.VMEM((ATT_KEYS, B_KV_DIM), f32)],
        compiler_params=_params("parallel"),
        name="attn_sample",
    )(sinks, q_s, k_new, v_new, k_cache, v_cache, table_s, q_norm, k_norm)


CONV_PAD = SUBLANES
ZX_BLK = 1024


def _gated_group_norm(y, z, norm_w):
    gt = y * _silu(z)
    parts = []
    for g in range(C_GROUPS):
        gg = gt[:, g * C_GROUP_W:(g + 1) * C_GROUP_W]
        parts.append(gg * lax.rsqrt(jnp.mean(gg * gg, axis=-1, keepdims=True) + EPS))
    return jnp.concatenate(parts, axis=1) * norm_w


LOG2E = math.log2(math.e)


def _expand_heads(v, sel3):
    lane = lax.broadcasted_iota(jnp.int32, (1, LANES), 1)
    v = jnp.where(lane < C_HEADS, v, 0.0)
    hi = v.astype(bf16).astype(f32)
    r1 = v - hi
    mid = r1.astype(bf16).astype(f32)
    lo = r1 - mid
    packed = hi + pltpu.roll(mid, C_HEADS, axis=1) + pltpu.roll(lo, 2 * C_HEADS, axis=1)
    return _dot(packed.astype(bf16), sel3)


def _ssd_prompt_kernel(zx_ref, dtr_ref, cw_ref, cb_ref, dtb_ref, alog_ref, dsk_ref, nw_ref, sel_ref,
                       yn_ref, hfin_ref, cout_ref, xpad_ref, ht_ref, y_ref):
    c = pl.program_id(1)
    T = C_CHUNK

    @pl.when(c == 0)
    def _():
        xpad_ref[0:CONV_PAD, :] = jnp.zeros((CONV_PAD, C_CONV_DIM), f32)
        ht_ref[...] = jnp.zeros(ht_ref.shape, f32)

    for k in range(C_CONV_DIM // ZX_BLK):
        xpad_ref[CONV_PAD:, k * ZX_BLK:(k + 1) * ZX_BLK] = zx_ref[C_D_INNER // ZX_BLK + k].astype(f32)
    xp = xpad_ref[...]
    xbc = xp[CONV_PAD:, :]
    cw = cw_ref[...]
    acc = cb_ref[...]
    for tap in range(C_D_CONV - 1):
        shifted = pltpu.roll(xp, C_D_CONV - 1 - tap, axis=0)[CONV_PAD:, :]
        acc = acc + shifted * cw[tap:tap + 1, :]
    acc = acc + xbc * cw[C_D_CONV - 1:C_D_CONV, :]
    xpad_ref[0:CONV_PAD, :] = xbc[T - CONV_PAD:, :]
    cout_ref[0] = xbc[T - (C_D_CONV - 1):, :]
    act = _silu(acc)
    xs = act[:, :C_D_INNER]
    bm = act[:, C_D_INNER:C_D_INNER + C_BC_DIM]
    cm = act[:, C_D_INNER + C_BC_DIM:]
    xb = xs.astype(bf16)

    dt = _softplus(dtr_ref[...] + dtb_ref[...])
    a_neg = -jnp.exp(alog_ref[...])
    row = lax.broadcasted_iota(jnp.int32, (T, T), 0)
    col = lax.broadcasted_iota(jnp.int32, (T, T), 1)
    causal = row >= col
    acs = _dot_exact_lhs01(causal.astype(f32), dt * a_neg)
    a2 = acs * LOG2E
    sel3 = sel_ref[...]
    e_exp = jnp.exp2(_expand_heads(a2, sel3))
    w_exp = _expand_heads(jnp.exp(acs[T - 1:T, :] - acs) * dt, sel3)
    cdec = e_exp[T - 1:T, :]
    b2_t = a2.T - jnp.log2(dt.T)
    xw = (xs * w_exp).astype(bf16)
    hb = ht_ref[...].astype(bf16)
    dsk = dsk_ref[...]
    lo_t = lax.broadcasted_iota(jnp.int32, (T, LANES), 1) < C_HEAD_DIM

    for g in range(C_GROUPS):
        ns = slice(g * C_D_STATE, (g + 1) * C_D_STATE)
        gs = slice(g * C_GROUP_W, (g + 1) * C_GROUP_W)
        b_g = bm[:, ns]
        c_g = cm[:, ns].astype(bf16)
        cb = _dot_nt(c_g, b_g.astype(bf16))
        yi = _dot(c_g, hb[:, gs])
        ht_ref[:, gs] = ht_ref[:, gs] * cdec[:, gs] + _dot(b_g.T.astype(bf16), xw[:, gs])
        for tt in range(C_GROUP_W // LANES):
            t = g * (C_GROUP_W // LANES) + tt
            lanes = slice(t * LANES, (t + 1) * LANES)
            xt = xb[:, lanes]
            res = []
            for half in range(2):
                h = 2 * t + half
                a_col = jnp.broadcast_to(a2[:, h:h + 1], (T, T))
                b_row = jnp.broadcast_to(b2_t[h:h + 1, :], (T, T))
                w = jnp.where(causal, cb * jnp.exp2(a_col - b_row), 0.0)
                res.append(_dot(w.astype(bf16), xt))
            y_intra = jnp.where(lo_t, res[0], res[1])
            y_ref[:, lanes] = y_intra + e_exp[:, lanes] * yi[:, tt * LANES:(tt + 1) * LANES] + dsk[:, lanes] * xs[:, lanes]

    z = jnp.concatenate([zx_ref[k].astype(f32) for k in range(C_D_INNER // ZX_BLK)], axis=1)
    yn_ref[...] = _gated_group_norm(y_ref[...], z, nw_ref[...]).astype(bf16)

    @pl.when(c == pl.num_programs(1) - 1)
    def _():
        for t in range(C_D_INNER // LANES):
            hfin_ref[0, t * LANES:(t + 1) * LANES, :] = ht_ref[:, t * LANES:(t + 1) * LANES].T


def _head_select3():
    k = np.arange(LANES)[:, None]
    ch = np.arange(C_D_INNER)[None, :] // C_HEAD_DIM
    return jnp.asarray((k % C_HEADS == ch) & (k < 3 * C_HEADS), dtype=bf16)


def ssd_prompt(zx, dtr, conv_w, conv_b, dt_bias, a_log, d_skip, norm_w):
    nc = SEQ // C_CHUNK
    full = lambda *shape: pl.BlockSpec(shape, lambda b, c: (0,) * len(shape))
    return pl.pallas_call(
        _ssd_prompt_kernel,
        grid=(BATCH, nc),
        in_specs=[
            pl.BlockSpec(((C_D_INNER + C_CONV_DIM) // ZX_BLK, C_CHUNK, ZX_BLK), lambda b, c: (0, b * nc + c, 0)),
            pl.BlockSpec((C_CHUNK, LANES), lambda b, c: (b * nc + c, 0)),
            full(C_D_CONV, C_CONV_DIM), full(1, C_CONV_DIM), full(1, LANES), full(1, LANES),
            full(1, C_D_INNER), full(1, C_D_INNER), full(LANES, C_D_INNER),
        ],
        out_specs=[
            pl.BlockSpec((C_CHUNK, C_D_INNER), lambda b, c: (b * nc + c, 0)),
            pl.BlockSpec((1, C_D_INNER, C_D_STATE), lambda b, c: (b, 0, 0)),
            pl.BlockSpec((1, C_D_CONV - 1, C_CONV_DIM), lambda b, c: (b, 0, 0)),
        ],
        out_shape=[
            jax.ShapeDtypeStruct((PROMPT_ROWS, C_D_INNER), bf16),
            jax.ShapeDtypeStruct((BATCH, C_D_INNER, C_D_STATE), f32),
            jax.ShapeDtypeStruct((BATCH, C_D_CONV - 1, C_CONV_DIM), f32),
        ],
        scratch_shapes=[
            pltpu.VMEM((CONV_PAD + C_CHUNK, C_CONV_DIM), f32),
            pltpu.VMEM((C_D_STATE, C_D_INNER), f32),
            pltpu.VMEM((C_CHUNK, C_D_INNER), f32),
        ],
        compiler_params=_params("parallel", "arbitrary"),
        name="ssd_prompt",
    )(zx, dtr, conv_w, conv_b, dt_bias, a_log, d_skip, norm_w, _head_select3())


SSD_BB = 8
SSD_TP = SUBLANES
_N_PAIRS = DEC_SEQ * (DEC_SEQ + 1) // 2
_N_COEF = _N_PAIRS + 2 * DEC_SEQ


def _ssd_sample_kernel(zx_ref, dtr_ref, cs_ref, h0_ref, cw_ref, cb_ref, dtb_ref, alog_ref, dsk_ref, nw_ref,
                       sel_ref, yn_ref, hn_ref, cout_ref, c_scr, b_scr, xw_scr, yi_scr, cd_scr):
    L = DEC_SEQ
    cw = cw_ref[...]
    xp = [cs_ref[k] for k in range(C_D_CONV - 1)] + [zx_ref[t, :, C_D_INNER:] for t in range(L)]
    for k in range(C_D_CONV - 1):
        cout_ref[k] = xp[L + k]
    act = []
    for t in range(L):
        acc = cb_ref[...]
        for tap in range(C_D_CONV):
            acc = acc + xp[t + tap] * cw[tap:tap + 1, :]
        act.append(_silu(acc))
    xs = [a[:, :C_D_INNER] for a in act]
    bm = [a[:, C_D_INNER:C_D_INNER + C_BC_DIM] for a in act]
    cm = [a[:, C_D_INNER + C_BC_DIM:] for a in act]

    a_neg = -jnp.exp(alog_ref[...])
    dt = [_softplus(dtr_ref[t] + dtb_ref[...]) for t in range(L)]
    acs = []
    for t in range(L):
        acs.append(dt[t] * a_neg if t == 0 else acs[t - 1] + dt[t] * a_neg)

    lane_group = lax.broadcasted_iota(jnp.int32, (SSD_BB, LANES), 1) // C_REP
    coefs = []
    for t in range(L):
        for t2 in range(t + 1):
            cbh = jnp.zeros((SSD_BB, LANES), f32)
            for g in range(C_GROUPS):
                ns = slice(g * C_D_STATE, (g + 1) * C_D_STATE)
                cbg = jnp.sum(cm[t][:, ns] * bm[t2][:, ns], axis=-1, keepdims=True)
                cbh = jnp.where(lane_group == g, cbg, cbh)
            coefs.append(cbh * jnp.exp(acs[t] - acs[t2]) * dt[t2])
    for t in range(L):
        coefs.append(jnp.exp(acs[t]))
    for t in range(L):
        coefs.append(jnp.exp(acs[L - 1] - acs[t]) * dt[t])
    coef = jnp.concatenate(coefs, axis=0)
    cexp = _dot_exact_rhs01(coef, sel_ref[...])
    cexp = [cexp[k * SSD_BB:(k + 1) * SSD_BB, :] for k in range(_N_COEF)]
    w_intra = cexp[:_N_PAIRS]
    w_inter = cexp[_N_PAIRS:_N_PAIRS + L]
    w_state = cexp[_N_PAIRS + L:]

    cd = jnp.concatenate([jnp.exp(acs[L - 1]), jnp.zeros((LANES - SSD_BB, LANES), f32)], axis=0)
    cd_t = cd.T
    for s in range(SSD_BB):
        cd_scr[s] = jnp.broadcast_to(cd_t[0:C_HEADS, s:s + 1], (C_HEADS, C_D_STATE))

    zeros_tail = jnp.zeros((SSD_BB, SSD_TP - L, C_D_INNER), f32)
    c_scr[:, L:, :] = zeros_tail[:, :, :C_BC_DIM]
    b_scr[:, L:, :] = zeros_tail[:, :, :C_BC_DIM]
    xw_scr[:, L:, :] = zeros_tail
    for t in range(L):
        xw_t = xs[t] * w_state[t]
        for s in range(SSD_BB):
            c_scr[s, t:t + 1, :] = cm[t][s:s + 1, :]
            b_scr[s, t:t + 1, :] = bm[t][s:s + 1, :]
            xw_scr[s, t:t + 1, :] = xw_t[s:s + 1, :]

    for s in range(SSD_BB):
        for g in range(C_GROUPS):
            ns = slice(g * C_D_STATE, (g + 1) * C_D_STATE)
            gs = slice(g * C_GROUP_W, (g + 1) * C_GROUP_W)
            h0 = h0_ref[s, gs, :]
            yi = _dot_nt(c_scr[s, :, ns].astype(bf16), h0.astype(bf16))
            for t in range(L):
                yi_scr[t, s:s + 1, gs] = yi[t:t + 1, :]
            st = _dot_tn(xw_scr[s, :, gs].astype(bf16), b_scr[s, :, ns].astype(bf16))
            for r in range(C_REP):
                h = g * C_REP + r
                rs = slice(r * C_HEAD_DIM, (r + 1) * C_HEAD_DIM)
                scale = cd_scr[s, h:h + 1, :]
                hn_ref[s, h * C_HEAD_DIM:(h + 1) * C_HEAD_DIM, :] = h0[rs, :] * scale + st[rs, :]

    dsk = dsk_ref[...]
    nw = nw_ref[...]
    pair = 0
    for t in range(L):
        y = w_inter[t] * yi_scr[t] + dsk * xs[t]
        for t2 in range(t + 1):
            y = y + w_intra[pair] * xs[t2]
            pair += 1
        yn_ref[t] = _gated_group_norm(y, zx_ref[t, :, :C_D_INNER], nw).astype(bf16)


def ssd_sample(zx_t, dtr_t, conv_state_t, h0, conv_w, conv_b, dt_bias, a_log, d_skip, norm_w, sel):
    tmaj = lambda n, w: pl.BlockSpec((n, SSD_BB, w), lambda i: (0, i, 0))
    full = lambda *shape: pl.BlockSpec(shape, lambda i: (0,) * len(shape))
    return pl.pallas_call(
        _ssd_sample_kernel,
        grid=(DEC_BATCH // SSD_BB,),
        in_specs=[
            tmaj(DEC_SEQ, C_D_INNER + C_CONV_DIM), tmaj(DEC_SEQ, LANES), tmaj(C_D_CONV - 1, C_CONV_DIM),
            pl.BlockSpec((SSD_BB, C_D_INNER, C_D_STATE), lambda i: (i, 0, 0)),
            full(C_D_CONV, C_CONV_DIM), full(1, C_CONV_DIM), full(1, LANES), full(1, LANES),
            full(1, C_D_INNER), full(1, C_D_INNER), full(LANES, C_D_INNER),
        ],
        out_specs=[
            tmaj(DEC_SEQ, C_D_INNER),
            pl.BlockSpec((SSD_BB, C_D_INNER, C_D_STATE), lambda i: (i, 0, 0)),
            tmaj(C_D_CONV - 1, C_CONV_DIM),
        ],
        out_shape=[
            jax.ShapeDtypeStruct((DEC_SEQ, DEC_BATCH, C_D_INNER), bf16),
            jax.ShapeDtypeStruct((DEC_BATCH, C_D_INNER, C_D_STATE), f32),
            jax.ShapeDtypeStruct((C_D_CONV - 1, DEC_BATCH, C_CONV_DIM), f32),
        ],
        scratch_shapes=[
            pltpu.VMEM((SSD_BB, SSD_TP, C_BC_DIM), f32),
            pltpu.VMEM((SSD_BB, SSD_TP, C_BC_DIM), f32),
            pltpu.VMEM((SSD_BB, SSD_TP, C_D_INNER), f32),
            pltpu.VMEM((DEC_SEQ, SSD_BB, C_D_INNER), f32),
            pltpu.VMEM((SSD_BB, C_HEADS, C_D_STATE), f32),
        ],
        compiler_params=_params("parallel"),
        name="ssd_sample",
    )(zx_t, dtr_t, conv_state_t, h0, conv_w, conv_b, dt_bias, a_log, d_skip, norm_w, sel)


def _pad_lanes(v):
    return jnp.pad(v.astype(f32), (0, LANES - v.shape[0])).reshape(1, LANES)


def _mixer_a(xp, xs, g, j, w_in, norm_v, w_sp, b_sp, w_out):
    nv = norm_v.reshape(1, A_HALF)
    xp = mixer_a(xp, g, w_in, nv, w_sp, b_sp.T, w_out, j, tm=TM_PROMPT, sample=False)
    xs, v_s = mixer_a(xs, g, w_in, nv, w_sp[:, :DEC_SEQ, :DEC_SEQ].reshape(-1), b_sp[:, :DEC_SEQ].reshape(-1),
                      w_out, j, tm=TM_SAMPLE, sample=True)
    return xp, xs, jnp.swapaxes(v_s.reshape(DEC_SEQ, DEC_BATCH, A_HALF), 0, 1)


def _mixer_b(xp, xs, g, j, k_cache, v_cache, w_qkv, q_norm, k_norm, sinks, rel_bias):
    qn = q_norm.reshape(1, B_HEAD_DIM)
    kn = k_norm.reshape(1, B_HEAD_DIM)
    table = bias_table(rel_bias)
    n_qkv = B_Q_DIM + 2 * B_KV_DIM

    qkv_p = norm_matmul(xp, g, w_qkv, j, n_qkv, tm=TM_PROMPT, tn=512, out_dtype=bf16)
    o_p, k_p, v_p = attn_prompt(qkv_p, table, qn, kn, sinks)

    qkv_s = norm_matmul(xs, g, w_qkv, j, n_qkv, tm=TM_SAMPLE, tn=512).reshape(DEC_SEQ, DEC_BATCH, -1)
    q_s = qkv_s[:, :, :B_Q_DIM].reshape(DEC_SEQ, DEC_BATCH, B_KV_HEADS, B_REP, B_HEAD_DIM)
    q_s = q_s.transpose(1, 3, 0, 2, 4).reshape(DEC_BATCH, ATT_QR, B_KV_DIM)
    k_new = jnp.swapaxes(qkv_s[:, :, B_Q_DIM:B_Q_DIM + B_KV_DIM], 0, 1)
    v_new = jnp.swapaxes(qkv_s[:, :, B_Q_DIM + B_KV_DIM:], 0, 1)
    table_s = table[:, :DEC_SEQ, :ATT_KEYS].reshape(B_KV_HEADS, ATT_QR, ATT_KEYS)
    o_s, k_s, v_s = attn_sample(q_s, k_new, v_new,
                                k_cache.reshape(DEC_BATCH, B_WINDOW, B_KV_DIM),
                                v_cache.reshape(DEC_BATCH, B_WINDOW, B_KV_DIM),
                                table_s, qn, kn, sinks)
    o_s = o_s.reshape(DEC_BATCH, B_REP, DEC_SEQ, B_KV_HEADS, B_HEAD_DIM).transpose(2, 0, 3, 1, 4)
    o_s = o_s.reshape(SAMPLE_ROWS, B_Q_DIM).astype(bf16)
    kv_shape_p = (BATCH, B_WINDOW, B_KV_HEADS, B_HEAD_DIM)
    kv_shape_s = (DEC_BATCH, B_WINDOW, B_KV_HEADS, B_HEAD_DIM)
    return o_p, o_s, k_p.reshape(kv_shape_p), v_p.reshape(kv_shape_p), k_s.reshape(kv_shape_s), v_s.reshape(kv_shape_s)


def _mixer_c(xp, xs, g, j, h0, conv_state, w_in, conv_w, conv_b, dt_bias, a_log, d_skip, norm_w):
    n_zx = C_D_INNER + C_CONV_DIM
    w_dt = jnp.pad(w_in[j, :, n_zx:], ((0, 0), (0, LANES - C_HEADS)))
    cb = conv_b.reshape(1, C_CONV_DIM)
    dtb = _pad_lanes(dt_bias)
    alog = _pad_lanes(a_log)
    dsk = jnp.repeat(d_skip.astype(f32), C_HEAD_DIM).reshape(1, C_D_INNER)
    nw = norm_w.reshape(1, C_D_INNER)

    w_zx = w_in[j, :, :n_zx].reshape(D_MODEL, n_zx // ZX_BLK, ZX_BLK).swapaxes(0, 1)
    zx_p, dtr_p = norm_matmul(xp, g, w_zx, 0, n_zx, tm=TM_PROMPT, tn=ZX_BLK, w_tail=w_dt, out_dtype=bf16,
                              blocked=True)
    yn_p, h_p, conv_p = ssd_prompt(zx_p, dtr_p, conv_w, cb, dtb, alog, dsk, nw)

    zx_s, dtr_s = norm_matmul(xs, g, w_in, j, n_zx, tm=TM_SAMPLE, tn=1024, w_tail=w_dt)
    sel = (jnp.arange(LANES)[:, None] == jnp.arange(C_D_INNER)[None, :] // C_HEAD_DIM).astype(f32)
    yn_s, h_s, conv_s = ssd_sample(
        zx_s.reshape(DEC_SEQ, DEC_BATCH, -1), dtr_s.reshape(DEC_SEQ, DEC_BATCH, LANES),
        jnp.swapaxes(conv_state, 0, 1), h0.reshape(DEC_BATCH, C_D_INNER, C_D_STATE),
        conv_w, cb, dtb, alog, dsk, nw, sel)
    st_shape = (C_HEADS, C_HEAD_DIM, C_D_STATE)
    return (yn_p, yn_s.reshape(SAMPLE_ROWS, C_D_INNER), h_p.reshape((BATCH,) + st_shape), conv_p,
            h_s.reshape((DEC_BATCH,) + st_shape), jnp.swapaxes(conv_s, 0, 1))


def kernel(x_prompt, x_sample, cache_swa_k, cache_swa_v, state_ssm, state_conv, norm_mixer, norm_mlp, mlp_w_up, mlp_w_down, a_w_in, a_norm_v, a_w_spatial, a_b_spatial, a_w_out, b_w_qkv, b_q_norm, b_k_norm, b_sinks, rel_bias, b_w_out, c_w_in, c_conv_w, c_conv_b, c_dt_bias, c_a_log, c_d, c_norm, c_w_out):
    xp = x_prompt.reshape(PROMPT_ROWS, D_MODEL)
    xs = jnp.swapaxes(x_sample, 0, 1).reshape(SAMPLE_ROWS, D_MODEL)
    chunk_v_s = []
    swa_kp, swa_vp, swa_ks, swa_vs = [], [], [], []
    ssm_p, conv_p, ssm_s, conv_s = [], [], [], []
    a_w_in = a_w_in.reshape(-1, D_MODEL, 2 * A_NBLK, A_BLK).swapaxes(1, 2).astype(bf16)
    a_w_out, c_w_in = a_w_out.astype(bf16), c_w_in.astype(bf16)
    b_w_out, c_w_out = b_w_out.astype(bf16), c_w_out.astype(bf16)
    for i in range(DEPTH):
        kind = i % N_MIXERS
        j = i // N_MIXERS
        g = norm_mixer[i].reshape(1, D_MODEL)
        proj_p = proj_s = None
        if kind == 0:
            xp, xs, v_new = _mixer_a(xp, xs, g, j, a_w_in, a_norm_v[j], a_w_spatial[j], a_b_spatial[j], a_w_out)
            chunk_v_s.append(v_new)
        elif kind == 1:
            o_p, o_s, kp, vp, ks_, vs_ = _mixer_b(xp, xs, g, j, cache_swa_k[j], cache_swa_v[j], b_w_qkv, b_q_norm[j],
                                                  b_k_norm[j], b_sinks[j], rel_bias)
            proj_p, proj_s = (o_p, b_w_out, j), (o_s, b_w_out, j)
            swa_kp.append(kp); swa_vp.append(vp); swa_ks.append(ks_); swa_vs.append(vs_)
        else:
            y_p, y_s, hp, bp, hs, bs = _mixer_c(xp, xs, g, j, state_ssm[j], state_conv[j], c_w_in, c_conv_w[j],
                                                c_conv_b[j], c_dt_bias[j], c_a_log[j], c_d[j], c_norm[j])
            proj_p, proj_s = (y_p, c_w_out, j), (y_s, c_w_out, j)
            ssm_p.append(hp); conv_p.append(bp); ssm_s.append(hs); conv_s.append(bs)
        gm = norm_mlp[i].reshape(1, D_MODEL)
        xp = mlp(xp, gm, mlp_w_up, mlp_w_down, i, tm=TM_PROMPT, tf=MLP_TF, proj=proj_p)
        xs = mlp(xs, gm, mlp_w_up, mlp_w_down, i, tm=TM_SAMPLE, tf=MLP_TF, proj=proj_s)
    y_prompt = xp.reshape(BATCH, SEQ, D_MODEL)
    y_sample = jnp.swapaxes(xs.reshape(DEC_SEQ, DEC_BATCH, D_MODEL), 0, 1)
    return (y_prompt, y_sample, jnp.stack(chunk_v_s),
            jnp.stack(swa_kp), jnp.stack(swa_vp), jnp.stack(swa_ks), jnp.stack(swa_vs),
            jnp.stack(ssm_p), jnp.stack(conv_p), jnp.stack(ssm_s), jnp.stack(conv_s))
```

```python
import functools
import math

import jax
import jax.numpy as jnp
import numpy as np
from jax import lax
from jax.experimental import pallas as pl
from jax.experimental.pallas import tpu as pltpu

f32 = jnp.float32
bf16 = jnp.bfloat16

D_MODEL = 1024
BATCH = 4
SEQ = 4096
DEPTH = 4
DEC_BATCH = 128
DEC_SEQ = 4
PAST_LEN = 8192
N_MIXERS = 3
D_FF = 4 * D_MODEL
EPS = 1e-6
NEG_INF = -1e30

A_CHUNK = 128
A_D_FFN = 6 * D_MODEL
A_HALF = A_D_FFN // 2
A_GROUPS = 8
A_GROUP_W = A_HALF // A_GROUPS

B_HEADS = 16
B_KV_HEADS = 4
B_HEAD_DIM = 64
B_REP = B_HEADS // B_KV_HEADS
B_WINDOW = 128
B_BLOCK = 128
B_Q_DIM = B_HEADS * B_HEAD_DIM
B_KV_DIM = B_KV_HEADS * B_HEAD_DIM
N_BUCKETS = 32
MAX_DISTANCE = 128

C_D_INNER = 2 * D_MODEL
C_HEAD_DIM = 64
C_HEADS = C_D_INNER // C_HEAD_DIM
C_GROUPS = 4
C_REP = C_HEADS // C_GROUPS
C_D_STATE = 128
C_D_CONV = 4
C_BC_DIM = C_GROUPS * C_D_STATE
C_CONV_DIM = C_D_INNER + 2 * C_BC_DIM
C_GROUP_W = C_D_INNER // C_GROUPS
C_CHUNK = 128

LANES = 128
SUBLANES = 8
VMEM_LIMIT_BYTES = 56 * 1024 * 1024

PROMPT_ROWS = BATCH * SEQ
SAMPLE_ROWS = DEC_BATCH * DEC_SEQ
TM_PROMPT = 1024
TM_SAMPLE = SAMPLE_ROWS
MLP_TF = 1024


def _params(*sem):
    return pltpu.CompilerParams(dimension_semantics=sem, vmem_limit_bytes=VMEM_LIMIT_BYTES)


def _rms(x, g):
    ms = jnp.mean(x * x, axis=-1, keepdims=True)
    return x * lax.rsqrt(ms + EPS) * g


def _gelu(x):
    return 0.5 * x * (1.0 + lax.erf(x * math.sqrt(0.5)))


def _silu(x):
    return x * jax.nn.sigmoid(x)


def _softplus(x):
    return jnp.maximum(x, 0.0) + jnp.log1p(jnp.exp(-jnp.abs(x)))


def _dot(a, b):
    return jnp.dot(a, b, preferred_element_type=f32)


def _dot_nt(a, b):
    return lax.dot_general(a, b, (((1,), (1,)), ((), ())), preferred_element_type=f32)


def _dot_tn(a, b):
    return lax.dot_general(a, b, (((0,), (0,)), ((), ())), preferred_element_type=f32)


def _dot_exact_lhs01(a01, x):
    a = a01.astype(bf16)
    hi = x.astype(bf16)
    r1 = x - hi.astype(f32)
    mid = r1.astype(bf16)
    lo = (r1 - mid.astype(f32)).astype(bf16)
    return _dot(a, hi) + _dot(a, mid) + _dot(a, lo)


def _dot_exact_rhs01(x, b01):
    b = b01.astype(bf16)
    hi = x.astype(bf16)
    r1 = x - hi.astype(f32)
    mid = r1.astype(bf16)
    lo = (r1 - mid.astype(f32)).astype(bf16)
    return _dot(hi, b) + _dot(mid, b) + _dot(lo, b)


def _norm_matmul_kernel(*refs, nj, tail):
    if tail:
        x_ref, g_ref, w_ref, wt_ref, o_ref, ot_ref, xn_ref = refs
    else:
        x_ref, g_ref, w_ref, o_ref, xn_ref = refs
    j = pl.program_id(1)

    @pl.when(j == 0)
    def _():
        xn_ref[...] = _rms(x_ref[...], g_ref[...]).astype(bf16)

    @pl.when(j < nj)
    def _():
        o_ref[...] = _dot(xn_ref[...], w_ref[0].astype(bf16)).astype(o_ref.dtype)

    if tail:
        @pl.when(j == nj)
        def _():
            ot_ref[...] = _dot(xn_ref[...], wt_ref[...].astype(bf16))


def norm_matmul(x, g, w, layer, n, *, tm, tn, w_tail=None, out_dtype=f32):
    m, k = x.shape
    nj = n // tn
    tail = w_tail is not None
    last = nj - 1
    in_specs = [
        pl.BlockSpec((tm, k), lambda i, j: (i, 0)),
        pl.BlockSpec((1, k), lambda i, j: (0, 0)),
        pl.BlockSpec((1, k, tn), lambda i, j: (layer, 0, jnp.minimum(j, last))),
    ]
    out_specs = [pl.BlockSpec((tm, tn), lambda i, j: (i, jnp.minimum(j, last)))]
    out_shape = [jax.ShapeDtypeStruct((m, n), out_dtype)]
    args = [x, g, w]
    if tail:
        in_specs.append(pl.BlockSpec((k, LANES), lambda i, j: (0, 0)))
        out_specs.append(pl.BlockSpec((tm, LANES), lambda i, j: (i, 0)))
        out_shape.append(jax.ShapeDtypeStruct((m, LANES), f32))
        args.append(w_tail)
    out = pl.pallas_call(
        functools.partial(_norm_matmul_kernel, nj=nj, tail=tail),
        grid=(m // tm, nj + (1 if tail else 0)),
        in_specs=in_specs,
        out_specs=out_specs,
        out_shape=out_shape,
        scratch_shapes=[pltpu.VMEM((tm, k), bf16)],
        compiler_params=_params("parallel", "arbitrary"),
        name="norm_matmul",
    )(*args)
    return out if tail else out[0]


def _mlp_kernel(*refs, proj):
    if proj:
        a_ref, wo_ref, x_ref, g_ref, wu_ref, wd_ref, o_ref, xn_ref = refs
    else:
        x_ref, g_ref, wu_ref, wd_ref, o_ref, xn_ref = refs

    @pl.when(pl.program_id(1) == 0)
    def _():
        x = x_ref[...]
        if proj:
            x = x + _dot(a_ref[...], wo_ref[0])
        xn_ref[...] = _rms(x, g_ref[...]).astype(bf16)
        o_ref[...] = x

    h = jnp.maximum(_dot(xn_ref[...], wu_ref[0].astype(bf16)), 0.0)
    o_ref[...] += _dot((h * h).astype(bf16), wd_ref[0].astype(bf16))


def mlp(x, g, w_up, w_down, layer, *, tm, tf, proj=None):
    m, d = x.shape
    ff = w_up.shape[2]
    in_specs = [
        pl.BlockSpec((tm, d), lambda i, j: (i, 0)),
        pl.BlockSpec((1, d), lambda i, j: (0, 0)),
        pl.BlockSpec((1, d, tf), lambda i, j: (layer, 0, j)),
        pl.BlockSpec((1, tf, d), lambda i, j: (layer, j, 0)),
    ]
    args = [x, g, w_up, w_down]
    if proj is not None:
        a, w_o, lo = proj
        k = a.shape[1]
        in_specs = [pl.BlockSpec((tm, k), lambda i, j: (i, 0)),
                    pl.BlockSpec((1, k, d), lambda i, j: (lo, 0, 0), pipeline_mode=pl.Buffered(1))] + in_specs
        args = [a, w_o] + args
    return pl.pallas_call(
        functools.partial(_mlp_kernel, proj=proj is not None),
        grid=(m // tm, ff // tf),
        in_specs=in_specs,
        out_specs=pl.BlockSpec((tm, d), lambda i, j: (i, 0)),
        out_shape=jax.ShapeDtypeStruct((m, d), f32),
        scratch_shapes=[pltpu.VMEM((tm, d), bf16)],
        compiler_params=_params("parallel", "arbitrary"),
        name="mlp",
    )(*args)


A_BLK_GROUPS = 2
A_BLK = A_BLK_GROUPS * A_GROUP_W
A_NBLK = A_HALF // A_BLK


def _mixer_a_kernel(*refs, sample):
    if sample:
        ws_ref, bs_ref, x_ref, g_ref, win_ref, nv_ref, wout_ref, o_ref, vo_ref, xn_ref, v_ref, ssq_ref, us_ref = refs
    else:
        x_ref, g_ref, win_ref, nv_ref, ws_ref, bs_ref, wout_ref, o_ref, xn_ref, v_ref, ssq_ref, us_ref = refs
    j = pl.program_id(1)
    tm = x_ref.shape[0]

    @pl.when(j == 0)
    def _():
        xn_ref[...] = _rms(x_ref[...], g_ref[...]).astype(bf16)
        ssq_ref[...] = jnp.zeros(ssq_ref.shape, f32)

    for k in range(A_NBLK):
        @pl.when(j == k)
        def _(k=k):
            v = _gelu(_dot(xn_ref[...], win_ref[0].astype(bf16)))
            v_ref[:, k * A_BLK:(k + 1) * A_BLK] = v
            ssq_ref[...] += jnp.sum(v * v, axis=-1, keepdims=True)

    if not sample:
        row = lax.broadcasted_iota(jnp.int32, (A_CHUNK, A_CHUNK), 0)
        col = lax.broadcasted_iota(jnp.int32, (A_CHUNK, A_CHUNK), 1)
        causal = row >= col

    for k in range(A_NBLK):
        @pl.when(j == A_NBLK + k)
        def _(k=k):
            u = _gelu(_dot(xn_ref[...], win_ref[0].astype(bf16)))
            rinv = lax.rsqrt(ssq_ref[...] * (1.0 / A_HALF) + EPS)
            for gg in range(A_BLK_GROUPS):
                g = k * A_BLK_GROUPS + gg
                cols = slice(g * A_GROUP_W, (g + 1) * A_GROUP_W)
                ucols = slice(gg * A_GROUP_W, (gg + 1) * A_GROUP_W)
                vn = v_ref[:, cols] * rinv * nv_ref[:, cols]
                if sample:
                    vo_ref[:, cols] = vn
                    vt = [vn[t * DEC_BATCH:(t + 1) * DEC_BATCH] for t in range(DEC_SEQ)]
                    s_rows = []
                    for t in range(DEC_SEQ):
                        s = ws_ref[(g * DEC_SEQ + t) * DEC_SEQ] * vt[0]
                        for t2 in range(1, t + 1):
                            s = s + ws_ref[(g * DEC_SEQ + t) * DEC_SEQ + t2] * vt[t2]
                        s_rows.append(s + bs_ref[g * DEC_SEQ + t])
                    s = jnp.concatenate(s_rows, axis=0)
                else:
                    w = jnp.where(causal, ws_ref[g], 0.0).astype(bf16)
                    bias = bs_ref[:, g:g + 1]
                    vb = vn.astype(bf16)
                    s = jnp.concatenate(
                        [_dot(w, vb[c * A_CHUNK:(c + 1) * A_CHUNK]) + bias for c in range(tm // A_CHUNK)], axis=0)
                us_ref[:, ucols] = (u[:, ucols] * s).astype(bf16)
            y = _dot(us_ref[...], wout_ref[0].astype(bf16))
            if k == 0:
                o_ref[...] = x_ref[...] + y
            else:
                o_ref[...] += y


def mixer_a(x, g, w_in, norm_v, w_sp, b_sp, w_out, layer, *, tm, sample):
    m, d = x.shape
    nj = 2 * A_NBLK
    row = lambda w: pl.BlockSpec((tm, w), lambda i, j: (i, 0))
    full = lambda *shape: pl.BlockSpec(shape, lambda i, j: (0,) * len(shape))
    smem = pl.BlockSpec(memory_space=pltpu.SMEM)
    win_spec = pl.BlockSpec((1, d, A_BLK), lambda i, j: (layer, 0, (j + A_NBLK) % nj))
    wout_spec = pl.BlockSpec((1, A_BLK, d), lambda i, j: (layer, jnp.maximum(j - A_NBLK, 0), 0))
    if sample:
        in_specs = [smem, smem, row(d), full(1, d), win_spec, full(1, A_HALF), wout_spec]
        args = (w_sp, b_sp, x, g, w_in, norm_v, w_out)
        out_specs = [row(d), row(A_HALF)]
        out_shape = [jax.ShapeDtypeStruct((m, d), f32), jax.ShapeDtypeStruct((m, A_HALF), f32)]
    else:
        in_specs = [row(d), full(1, d), win_spec, full(1, A_HALF), full(A_GROUPS, A_CHUNK, A_CHUNK),
                    full(A_CHUNK, A_GROUPS), wout_spec]
        args = (x, g, w_in, norm_v, w_sp, b_sp, w_out)
        out_specs = row(d)
        out_shape = jax.ShapeDtypeStruct((m, d), f32)
    return pl.pallas_call(
        functools.partial(_mixer_a_kernel, sample=sample),
        grid=(m // tm, nj),
        in_specs=in_specs,
        out_specs=out_specs,
        out_shape=out_shape,
        scratch_shapes=[pltpu.VMEM((tm, d), bf16), pltpu.VMEM((tm, A_HALF), f32), pltpu.VMEM((tm, 1), f32),
                        pltpu.VMEM((tm, A_BLK), bf16)],
        compiler_params=_params("parallel", "arbitrary"),
        name="mixer_a_sample" if sample else "mixer_a_prompt",
    )(*args)


def _bucket_table():
    i = np.arange(B_BLOCK)[:, None]
    j = np.arange(2 * B_BLOCK)[None, :]
    n = np.maximum(B_BLOCK + i - j, 0)
    max_exact = N_BUCKETS // 2
    nf = np.maximum(n, 1).astype(np.float64)
    val = np.log(nf / max_exact) / math.log(MAX_DISTANCE / max_exact) * (N_BUCKETS - max_exact)
    in_window = (n >= max_exact) & (n < B_WINDOW)
    assert np.all(np.abs(val - np.round(val))[in_window & (n != max_exact)] > 1e-3)
    large = np.minimum(max_exact + np.floor(val + 1e-9).astype(np.int64), N_BUCKETS - 1)
    return np.where(n < max_exact, n, large).astype(np.int32)


def _bias_table_kernel(rb_ref, bk_ref, o_ref):
    bk = bk_ref[...]
    for h in range(B_HEADS):
        acc = jnp.zeros(bk.shape, f32)
        for b in range(N_BUCKETS):
            acc = jnp.where(bk == b, rb_ref[b * B_HEADS + h], acc)
        o_ref[h] = acc


def bias_table(rel_bias):
    return pl.pallas_call(
        _bias_table_kernel,
        in_specs=[pl.BlockSpec(memory_space=pltpu.SMEM), pl.BlockSpec(memory_space=pltpu.VMEM)],
        out_specs=pl.BlockSpec(memory_space=pltpu.VMEM),
        out_shape=jax.ShapeDtypeStruct((B_HEADS, B_BLOCK, 2 * B_BLOCK), f32),
        name="bias_table",
    )(rel_bias.reshape(-1), jnp.asarray(_bucket_table()))


def _softmax_with_sink(logits, sink):
    m = jnp.maximum(jnp.max(logits, axis=-1, keepdims=True), sink)
    p = jnp.exp(logits - m)
    return p, jnp.sum(p, axis=-1, keepdims=True) + jnp.exp(sink - m)


def _rms_head_pairs(x, g2, lo):
    sq = x * x
    s_lo = jnp.sum(jnp.where(lo, sq, 0.0), axis=-1, keepdims=True)
    s_hi = jnp.sum(jnp.where(lo, 0.0, sq), axis=-1, keepdims=True)
    r = lax.rsqrt(jnp.where(lo, s_lo, s_hi) * (1.0 / B_HEAD_DIM) + EPS)
    return x * r * g2


def _attn_prompt_kernel(sink_ref, qkv_ref, tab_ref, qn_ref, kn_ref, o_ref, ko_ref, vo_ref,
                        kband_ref, vband_ref, q_ref, p_ref, rhs_ref):
    n = pl.program_id(1)
    T = B_BLOCK

    @pl.when(n == 0)
    def _():
        kband_ref[0:T, :] = jnp.zeros((T, B_KV_DIM), f32)
        vband_ref[0:T, :] = jnp.zeros((T, B_KV_DIM), f32)
        rhs_ref[...] = jnp.ones(rhs_ref.shape, bf16)

    lo = lax.broadcasted_iota(jnp.int32, (1, LANES), 1) < B_HEAD_DIM
    qn2 = qn_ref[...]
    kn2 = kn_ref[...]
    for t in range(B_KV_DIM // LANES):
        lanes = slice(t * LANES, (t + 1) * LANES)
        k2 = _rms_head_pairs(qkv_ref[:, B_Q_DIM + t * LANES:B_Q_DIM + (t + 1) * LANES].astype(f32), kn2, lo)
        ko_ref[0, :, lanes] = k2
        kband_ref[T:, lanes] = k2
    v = qkv_ref[:, B_Q_DIM + B_KV_DIM:].astype(f32)
    vo_ref[0] = v
    vband_ref[T:, :] = v
    for t in range(B_Q_DIM // LANES):
        q2 = qkv_ref[:, t * LANES:(t + 1) * LANES].astype(f32)
        q_ref[t * T:(t + 1) * T, :] = _rms_head_pairs(q2, qn2, lo).astype(bf16)

    key_ops = {}
    band_row = lax.broadcasted_iota(jnp.int32, (2 * T, LANES), 0)
    for t in range(B_KV_DIM // LANES):
        lanes = slice(t * LANES, (t + 1) * LANES)
        kt = kband_ref[:, lanes]
        kr = pltpu.roll(kt, B_HEAD_DIM, axis=1)
        vt = jnp.where(band_row == 0, 0.0, vband_ref[:, lanes])
        vr = pltpu.roll(vt, B_HEAD_DIM, axis=1)
        hi = jnp.logical_not(lo)
        for half, (ksrc, vsrc) in enumerate(((kt, vt), (kr, vr))):
            g_lo, g_hi = (2 * t, 2 * t + 1) if half == 0 else (2 * t + 1, 2 * t)
            key_ops[(g_lo, 0)] = jnp.where(lo, ksrc, 0.0).astype(bf16)
            key_ops[(g_hi, 1)] = jnp.where(hi, ksrc, 0.0).astype(bf16)
            rhs_ref[g_lo * 2 + 0, :, 0:LANES] = jnp.where(lo, vsrc, 1.0).astype(bf16)
            rhs_ref[g_hi * 2 + 1, :, 0:LANES] = jnp.where(hi, vsrc, 1.0).astype(bf16)

    i = lax.broadcasted_iota(jnp.int32, (T, 2 * T), 0)
    j = lax.broadcasted_iota(jnp.int32, (T, 2 * T), 1)
    first_key = jnp.where(n == 0, T, 0)
    valid = (j > i) & (j <= i + B_WINDOW) & (j >= first_key)
    sink_col = j == 0
    lo_t = lax.broadcasted_iota(jnp.int32, (T, LANES), 1) < B_HEAD_DIM
    for g in range(B_KV_HEADS):
        q2 = q_ref[2 * g * T:(2 * g + 2) * T, :]
        res = []
        for half in range(2):
            logits = _dot_nt(q2, key_ops[(g, half)]) * (B_HEAD_DIM ** -0.5)
            for pair in range(2):
                h = g * B_REP + 2 * pair + half
                l = jnp.where(valid, logits[pair * T:(pair + 1) * T] + tab_ref[h], NEG_INF)
                l = jnp.where(sink_col, sink_ref[h], l)
                p = jnp.exp(l - jnp.max(l, axis=-1, keepdims=True))
                p_ref[half, pair * T:(pair + 1) * T, :] = p.astype(bf16)
            res.append(_dot(p_ref[half], rhs_ref[g * 2 + half]))
        for pair in range(2):
            rows = slice(pair * T, (pair + 1) * T)
            even = res[0][rows, 0:LANES] / res[0][rows, LANES:]
            odd = res[1][rows, 0:LANES] / res[1][rows, LANES:]
            t = 2 * g + pair
            o_ref[:, t * LANES:(t + 1) * LANES] = jnp.where(lo_t, even, odd).astype(bf16)
    kband_ref[0:T, :] = kband_ref[T:, :]
    vband_ref[0:T, :] = vband_ref[T:, :]


def attn_prompt(qkv, table, q_norm, k_norm, sinks):
    nb = SEQ // B_BLOCK
    return pl.pallas_call(
        _attn_prompt_kernel,
        grid=(BATCH, nb),
        in_specs=[
            pl.BlockSpec(memory_space=pltpu.SMEM),
            pl.BlockSpec((B_BLOCK, B_Q_DIM + 2 * B_KV_DIM), lambda b, n: (b * nb + n, 0)),
            pl.BlockSpec((B_HEADS, B_BLOCK, 2 * B_BLOCK), lambda b, n: (0, 0, 0)),
            pl.BlockSpec((1, LANES), lambda b, n: (0, 0)),
            pl.BlockSpec((1, LANES), lambda b, n: (0, 0)),
        ],
        out_specs=[
            pl.BlockSpec((B_BLOCK, B_Q_DIM), lambda b, n: (b * nb + n, 0)),
            pl.BlockSpec((1, B_BLOCK, B_KV_DIM), lambda b, n: (b, 0, 0)),
            pl.BlockSpec((1, B_BLOCK, B_KV_DIM), lambda b, n: (b, 0, 0)),
        ],
        out_shape=[
            jax.ShapeDtypeStruct((PROMPT_ROWS, B_Q_DIM), bf16),
            jax.ShapeDtypeStruct((BATCH, B_BLOCK, B_KV_DIM), f32),
            jax.ShapeDtypeStruct((BATCH, B_BLOCK, B_KV_DIM), f32),
        ],
        scratch_shapes=[
            pltpu.VMEM((2 * B_BLOCK, B_KV_DIM), f32),
            pltpu.VMEM((2 * B_BLOCK, B_KV_DIM), f32),
            pltpu.VMEM((B_Q_DIM // LANES * B_BLOCK, LANES), bf16),
            pltpu.VMEM((2, 2 * B_BLOCK, 2 * B_BLOCK), bf16),
            pltpu.VMEM((2 * B_KV_HEADS, 2 * B_BLOCK, 2 * LANES), bf16),
        ],
        compiler_params=_params("parallel", "arbitrary"),
        name="attn_prompt",
    )(sinks, qkv, table, jnp.tile(q_norm, (1, LANES // B_HEAD_DIM)), jnp.tile(k_norm, (1, LANES // B_HEAD_DIM)))


ATT_BB = 8
ATT_QR = B_REP * DEC_SEQ
ATT_KEYS = B_WINDOW + 2 * DEC_SEQ


def _attn_sample_kernel(sink_ref, q_ref, kn_ref, vn_ref, kc_ref, vc_ref, tab_ref, qn_ref, knm_ref,
                        o_ref, ko_ref, vo_ref, kall_ref, vall_ref):
    qn2 = qn_ref[...]
    kn2 = knm_ref[...]
    n_rows = B_KV_HEADS * ATT_QR
    lo = lax.broadcasted_iota(jnp.int32, (1, LANES), 1) < B_HEAD_DIM
    row = lax.broadcasted_iota(jnp.int32, (n_rows, ATT_KEYS), 0)
    j = lax.broadcasted_iota(jnp.int32, (n_rows, ATT_KEYS), 1)
    t = row % DEC_SEQ
    valid = (j > t) & (j <= t + B_WINDOW)
    q_row_group = lax.broadcasted_iota(jnp.int32, (n_rows, B_KV_DIM), 0) // ATT_QR
    q_lane_group = lax.broadcasted_iota(jnp.int32, (n_rows, B_KV_DIM), 1) // B_HEAD_DIM
    own_group = q_row_group == q_lane_group
    o_lane_group = lax.broadcasted_iota(jnp.int32, (ATT_QR, B_KV_DIM), 1) // B_HEAD_DIM
    pad = jnp.zeros((ATT_KEYS - B_WINDOW - DEC_SEQ, B_KV_DIM), f32)
    bias = tab_ref[...]
    sink = sink_ref[...]
    for s in range(ATT_BB):
        kc = kc_ref[s]
        vc = vc_ref[s]
        k_new = jnp.concatenate(
            [_rms_head_pairs(kn_ref[s, :, tt * LANES:(tt + 1) * LANES], kn2, lo) for tt in range(B_KV_DIM // LANES)],
            axis=1)
        v_new = vn_ref[s]
        ko_ref[s, 0:B_WINDOW - DEC_SEQ, :] = kc[DEC_SEQ:, :]
        ko_ref[s, B_WINDOW - DEC_SEQ:, :] = k_new
        vo_ref[s, 0:B_WINDOW - DEC_SEQ, :] = vc[DEC_SEQ:, :]
        vo_ref[s, B_WINDOW - DEC_SEQ:, :] = v_new
        kall_ref[s, 0:B_WINDOW, :] = kc
        kall_ref[s, B_WINDOW:B_WINDOW + DEC_SEQ, :] = k_new
        kall_ref[s, B_WINDOW + DEC_SEQ:, :] = pad
        vall_ref[s, 0:B_WINDOW, :] = vc
        vall_ref[s, B_WINDOW:B_WINDOW + DEC_SEQ, :] = v_new
        vall_ref[s, B_WINDOW + DEC_SEQ:, :] = pad
        q = q_ref[s]
        qn = jnp.concatenate(
            [_rms_head_pairs(q[:, tt * LANES:(tt + 1) * LANES], qn2, lo) for tt in range(B_KV_DIM // LANES)], axis=1)
        q_all = jnp.where(own_group, jnp.concatenate([qn] * B_KV_HEADS, axis=0), 0.0).astype(bf16)
        logits = _dot_nt(q_all, kall_ref[s].astype(bf16)) * (B_HEAD_DIM ** -0.5)
        logits = jnp.where(valid, logits + bias, NEG_INF)
        p, denom = _softmax_with_sink(logits, sink)
        res = _dot(p.astype(bf16), vall_ref[s].astype(bf16)) / denom
        out = jnp.zeros((ATT_QR, B_KV_DIM), f32)
        for g in range(B_KV_HEADS):
            out = jnp.where(o_lane_group == g, res[g * ATT_QR:(g + 1) * ATT_QR, :], out)
        o_ref[s] = out


def attn_sample(q_s, k_new, v_new, k_cache, v_cache, table_s, q_norm, k_norm, sinks):
    blk = lambda *shape: pl.BlockSpec((ATT_BB,) + shape, lambda i: (i,) + (0,) * len(shape))
    full = lambda *shape: pl.BlockSpec(shape, lambda i: (0,) * len(shape))
    return pl.pallas_call(
        _attn_sample_kernel,
        grid=(DEC_BATCH // ATT_BB,),
        in_specs=[
            full(B_KV_HEADS * ATT_QR, 1),
            blk(ATT_QR, B_KV_DIM), blk(DEC_SEQ, B_KV_DIM), blk(DEC_SEQ, B_KV_DIM),
            blk(B_WINDOW, B_KV_DIM), blk(B_WINDOW, B_KV_DIM),
            full(B_KV_HEADS * ATT_QR, ATT_KEYS), full(1, LANES), full(1, LANES),
        ],
        out_specs=[blk(ATT_QR, B_KV_DIM), blk(B_WINDOW, B_KV_DIM), blk(B_WINDOW, B_KV_DIM)],
        out_shape=[
            jax.ShapeDtypeStruct((DEC_BATCH, ATT_QR, B_KV_DIM), f32),
            jax.ShapeDtypeStruct((DEC_BATCH, B_WINDOW, B_KV_DIM), f32),
            jax.ShapeDtypeStruct((DEC_BATCH, B_WINDOW, B_KV_DIM), f32),
        ],
        scratch_shapes=[pltpu.VMEM((ATT_BB, ATT_KEYS, B_KV_DIM), f32), pltpu.VMEM((ATT_BB, ATT_KEYS, B_KV_DIM), f32)],
        compiler_params=_params("parallel"),
        name="attn_sample",
    )(jnp.repeat(sinks, DEC_SEQ).reshape(B_KV_HEADS * ATT_QR, 1), q_s, k_new, v_new, k_cache, v_cache,
      table_s.reshape(B_KV_HEADS * ATT_QR, ATT_KEYS),
      jnp.tile(q_norm, (1, LANES // B_HEAD_DIM)), jnp.tile(k_norm, (1, LANES // B_HEAD_DIM)))


CONV_PAD = SUBLANES

def _gated_group_norm(y, z, norm_w):
    gt = y * _silu(z)
    parts = []
    for g in range(C_GROUPS):
        gg = gt[:, g * C_GROUP_W:(g + 1) * C_GROUP_W]
        parts.append(gg * lax.rsqrt(jnp.mean(gg * gg, axis=-1, keepdims=True) + EPS))
    return jnp.concatenate(parts, axis=1) * norm_w


LOG2E = math.log2(math.e)


def _expand_heads(v, sel3):
    lane = lax.broadcasted_iota(jnp.int32, (1, LANES), 1)
    v = jnp.where(lane < C_HEADS, v, 0.0)
    hi = v.astype(bf16).astype(f32)
    r1 = v - hi
    mid = r1.astype(bf16).astype(f32)
    lo = r1 - mid
    packed = hi + pltpu.roll(mid, C_HEADS, axis=1) + pltpu.roll(lo, 2 * C_HEADS, axis=1)
    return _dot(packed.astype(bf16), sel3)


def _ssd_prompt_kernel(zx_ref, dtr_ref, cw_ref, cb_ref, dtb_ref, alog_ref, dsk_ref, nw_ref, sel_ref,
                       yn_ref, hfin_ref, cout_ref, xpad_ref, ht_ref, y_ref):
    c = pl.program_id(1)
    T = C_CHUNK

    @pl.when(c == 0)
    def _():
        xpad_ref[0:CONV_PAD, :] = jnp.zeros((CONV_PAD, C_CONV_DIM), f32)
        ht_ref[...] = jnp.zeros(ht_ref.shape, f32)

    xbc = zx_ref[:, C_D_INNER:].astype(f32)
    xpad_ref[CONV_PAD:, :] = xbc
    xp = xpad_ref[...]
    cw = cw_ref[...]
    acc = cb_ref[...]
    for tap in range(C_D_CONV - 1):
        shifted = pltpu.roll(xp, C_D_CONV - 1 - tap, axis=0)[CONV_PAD:, :]
        acc = acc + shifted * cw[tap:tap + 1, :]
    acc = acc + xbc * cw[C_D_CONV - 1:C_D_CONV, :]
    xpad_ref[0:CONV_PAD, :] = xbc[T - CONV_PAD:, :]
    cout_ref[0] = xbc[T - (C_D_CONV - 1):, :]
    act = _silu(acc)
    xs = act[:, :C_D_INNER]
    bm = act[:, C_D_INNER:C_D_INNER + C_BC_DIM]
    cm = act[:, C_D_INNER + C_BC_DIM:]
    xb = xs.astype(bf16)

    dt = _softplus(dtr_ref[...] + dtb_ref[...])
    a_neg = -jnp.exp(alog_ref[...])
    row = lax.broadcasted_iota(jnp.int32, (T, T), 0)
    col = lax.broadcasted_iota(jnp.int32, (T, T), 1)
    causal = row >= col
    acs = _dot_exact_lhs01(causal.astype(f32), dt * a_neg)
    a2 = acs * LOG2E
    sel3 = sel_ref[...]
    e_exp = jnp.exp2(_expand_heads(a2, sel3))
    w_exp = _expand_heads(jnp.exp(acs[T - 1:T, :] - acs) * dt, sel3)
    cdec = e_exp[T - 1:T, :]
    b2_t = a2.T - jnp.log2(dt.T)
    xw = (xs * w_exp).astype(bf16)
    hb = ht_ref[...].astype(bf16)
    dsk = dsk_ref[...]
    lo_t = lax.broadcasted_iota(jnp.int32, (T, LANES), 1) < C_HEAD_DIM

    for g in range(C_GROUPS):
        ns = slice(g * C_D_STATE, (g + 1) * C_D_STATE)
        gs = slice(g * C_GROUP_W, (g + 1) * C_GROUP_W)
        b_g = bm[:, ns]
        c_g = cm[:, ns].astype(bf16)
        cb = _dot_nt(c_g, b_g.astype(bf16))
        yi = _dot(c_g, hb[:, gs])
        ht_ref[:, gs] = ht_ref[:, gs] * cdec[:, gs] + _dot(b_g.T.astype(bf16), xw[:, gs])
        for tt in range(C_GROUP_W // LANES):
            t = g * (C_GROUP_W // LANES) + tt
            lanes = slice(t * LANES, (t + 1) * LANES)
            xt = xb[:, lanes]
            res = []
            for half in range(2):
                h = 2 * t + half
                a_col = jnp.broadcast_to(a2[:, h:h + 1], (T, T))
                b_row = jnp.broadcast_to(b2_t[h:h + 1, :], (T, T))
                w = jnp.where(causal, cb * jnp.exp2(a_col - b_row), 0.0)
                res.append(_dot(w.astype(bf16), xt))
            y_intra = jnp.where(lo_t, res[0], res[1])
            y_ref[:, lanes] = y_intra + e_exp[:, lanes] * yi[:, tt * LANES:(tt + 1) * LANES] + dsk[:, lanes] * xs[:, lanes]

    yn_ref[...] = _gated_group_norm(y_ref[...], zx_ref[:, :C_D_INNER].astype(f32), nw_ref[...]).astype(bf16)

    @pl.when(c == pl.num_programs(1) - 1)
    def _():
        for t in range(C_D_INNER // LANES):
            hfin_ref[0, t * LANES:(t + 1) * LANES, :] = ht_ref[:, t * LANES:(t + 1) * LANES].T


def _head_select3():
    k = np.arange(LANES)[:, None]
    ch = np.arange(C_D_INNER)[None, :] // C_HEAD_DIM
    return jnp.asarray((k % C_HEADS == ch) & (k < 3 * C_HEADS), dtype=bf16)


def ssd_prompt(zx, dtr, conv_w, conv_b, dt_bias, a_log, d_skip, norm_w):
    nc = SEQ // C_CHUNK
    full = lambda *shape: pl.BlockSpec(shape, lambda b, c: (0,) * len(shape))
    return pl.pallas_call(
        _ssd_prompt_kernel,
        grid=(BATCH, nc),
        in_specs=[
            pl.BlockSpec((C_CHUNK, C_D_INNER + C_CONV_DIM), lambda b, c: (b * nc + c, 0)),
            pl.BlockSpec((C_CHUNK, LANES), lambda b, c: (b * nc + c, 0)),
            full(C_D_CONV, C_CONV_DIM), full(1, C_CONV_DIM), full(1, LANES), full(1, LANES),
            full(1, C_D_INNER), full(1, C_D_INNER), full(LANES, C_D_INNER),
        ],
        out_specs=[
            pl.BlockSpec((C_CHUNK, C_D_INNER), lambda b, c: (b * nc + c, 0)),
            pl.BlockSpec((1, C_D_INNER, C_D_STATE), lambda b, c: (b, 0, 0)),
            pl.BlockSpec((1, C_D_CONV - 1, C_CONV_DIM), lambda b, c: (b, 0, 0)),
        ],
        out_shape=[
            jax.ShapeDtypeStruct((PROMPT_ROWS, C_D_INNER), bf16),
            jax.ShapeDtypeStruct((BATCH, C_D_INNER, C_D_STATE), f32),
            jax.ShapeDtypeStruct((BATCH, C_D_CONV - 1, C_CONV_DIM), f32),
        ],
        scratch_shapes=[
            pltpu.VMEM((CONV_PAD + C_CHUNK, C_CONV_DIM), f32),
            pltpu.VMEM((C_D_STATE, C_D_INNER), f32),
            pltpu.VMEM((C_CHUNK, C_D_INNER), f32),
        ],
        compiler_params=_params("parallel", "arbitrary"),
        name="ssd_prompt",
    )(zx, dtr, conv_w, conv_b, dt_bias, a_log, d_skip, norm_w, _head_select3())


SSD_BB = 8
SSD_TP = SUBLANES
_N_PAIRS = DEC_SEQ * (DEC_SEQ + 1) // 2
_N_COEF = _N_PAIRS + 2 * DEC_SEQ


def _ssd_sample_kernel(zx_ref, dtr_ref, cs_ref, h0_ref, cw_ref, cb_ref, dtb_ref, alog_ref, dsk_ref, nw_ref,
                       sel_ref, yn_ref, hn_ref, cout_ref, c_scr, b_scr, xw_scr, yi_scr, cd_scr):
    L = DEC_SEQ
    cw = cw_ref[...]
    xp = [cs_ref[k] for k in range(C_D_CONV - 1)] + [zx_ref[t, :, C_D_INNER:] for t in range(L)]
    for k in range(C_D_CONV - 1):
        cout_ref[k] = xp[L + k]
    act = []
    for t in range(L):
        acc = cb_ref[...]
        for tap in range(C_D_CONV):
            acc = acc + xp[t + tap] * cw[tap:tap + 1, :]
        act.append(_silu(acc))
    xs = [a[:, :C_D_INNER] for a in act]
    bm = [a[:, C_D_INNER:C_D_INNER + C_BC_DIM] for a in act]
    cm = [a[:, C_D_INNER + C_BC_DIM:] for a in act]

    a_neg = -jnp.exp(alog_ref[...])
    dt = [_softplus(dtr_ref[t] + dtb_ref[...]) for t in range(L)]
    acs = []
    for t in range(L):
        acs.append(dt[t] * a_neg if t == 0 else acs[t - 1] + dt[t] * a_neg)

    lane_group = lax.broadcasted_iota(jnp.int32, (SSD_BB, LANES), 1) // C_REP
    coefs = []
    for t in range(L):
        for t2 in range(t + 1):
            cbh = jnp.zeros((SSD_BB, LANES), f32)
            for g in range(C_GROUPS):
                ns = slice(g * C_D_STATE, (g + 1) * C_D_STATE)
                cbg = jnp.sum(cm[t][:, ns] * bm[t2][:, ns], axis=-1, keepdims=True)
                cbh = jnp.where(lane_group == g, cbg, cbh)
            coefs.append(cbh * jnp.exp(acs[t] - acs[t2]) * dt[t2])
    for t in range(L):
        coefs.append(jnp.exp(acs[t]))
    for t in range(L):
        coefs.append(jnp.exp(acs[L - 1] - acs[t]) * dt[t])
    coef = jnp.concatenate(coefs, axis=0)
    cexp = _dot_exact_rhs01(coef, sel_ref[...])
    cexp = [cexp[k * SSD_BB:(k + 1) * SSD_BB, :] for k in range(_N_COEF)]
    w_intra = cexp[:_N_PAIRS]
    w_inter = cexp[_N_PAIRS:_N_PAIRS + L]
    w_state = cexp[_N_PAIRS + L:]

    cd = jnp.concatenate([jnp.exp(acs[L - 1]), jnp.zeros((LANES - SSD_BB, LANES), f32)], axis=0)
    cd_t = cd.T
    for s in range(SSD_BB):
        cd_scr[s] = jnp.broadcast_to(cd_t[0:C_HEADS, s:s + 1], (C_HEADS, C_D_STATE))

    zeros_tail = jnp.zeros((SSD_BB, SSD_TP - L, C_D_INNER), f32)
    c_scr[:, L:, :] = zeros_tail[:, :, :C_BC_DIM]
    b_scr[:, L:, :] = zeros_tail[:, :, :C_BC_DIM]
    xw_scr[:, L:, :] = zeros_tail
    for t in range(L):
        xw_t = xs[t] * w_state[t]
        for s in range(SSD_BB):
            c_scr[s, t:t + 1, :] = cm[t][s:s + 1, :]
            b_scr[s, t:t + 1, :] = bm[t][s:s + 1, :]
            xw_scr[s, t:t + 1, :] = xw_t[s:s + 1, :]

    for s in range(SSD_BB):
        for g in range(C_GROUPS):
            ns = slice(g * C_D_STATE, (g + 1) * C_D_STATE)
            gs = slice(g * C_GROUP_W, (g + 1) * C_GROUP_W)
            h0 = h0_ref[s, gs, :]
            yi = _dot_nt(c_scr[s, :, ns].astype(bf16), h0.astype(bf16))
            for t in range(L):
                yi_scr[t, s:s + 1, gs] = yi[t:t + 1, :]
            st = _dot_tn(xw_scr[s, :, gs].astype(bf16), b_scr[s, :, ns].astype(bf16))
            for r in range(C_REP):
                h = g * C_REP + r
                rs = slice(r * C_HEAD_DIM, (r + 1) * C_HEAD_DIM)
                scale = cd_scr[s, h:h + 1, :]
                hn_ref[s, h * C_HEAD_DIM:(h + 1) * C_HEAD_DIM, :] = h0[rs, :] * scale + st[rs, :]

    dsk = dsk_ref[...]
    nw = nw_ref[...]
    pair = 0
    for t in range(L):
        y = w_inter[t] * yi_scr[t] + dsk * xs[t]
        for t2 in range(t + 1):
            y = y + w_intra[pair] * xs[t2]
            pair += 1
        yn_ref[t] = _gated_group_norm(y, zx_ref[t, :, :C_D_INNER], nw).astype(bf16)


def ssd_sample(zx_t, dtr_t, conv_state_t, h0, conv_w, conv_b, dt_bias, a_log, d_skip, norm_w, sel):
    tmaj = lambda n, w: pl.BlockSpec((n, SSD_BB, w), lambda i: (0, i, 0))
    full = lambda *shape: pl.BlockSpec(shape, lambda i: (0,) * len(shape))
    return pl.pallas_call(
        _ssd_sample_kernel,
        grid=(DEC_BATCH // SSD_BB,),
        in_specs=[
            tmaj(DEC_SEQ, C_D_INNER + C_CONV_DIM), tmaj(DEC_SEQ, LANES), tmaj(C_D_CONV - 1, C_CONV_DIM),
            pl.BlockSpec((SSD_BB, C_D_INNER, C_D_STATE), lambda i: (i, 0, 0)),
            full(C_D_CONV, C_CONV_DIM), full(1, C_CONV_DIM), full(1, LANES), full(1, LANES),
            full(1, C_D_INNER), full(1, C_D_INNER), full(LANES, C_D_INNER),
        ],
        out_specs=[
            tmaj(DEC_SEQ, C_D_INNER),
            pl.BlockSpec((SSD_BB, C_D_INNER, C_D_STATE), lambda i: (i, 0, 0)),
            tmaj(C_D_CONV - 1, C_CONV_DIM),
        ],
        out_shape=[
            jax.ShapeDtypeStruct((DEC_SEQ, DEC_BATCH, C_D_INNER), bf16),
            jax.ShapeDtypeStruct((DEC_BATCH, C_D_INNER, C_D_STATE), f32),
            jax.ShapeDtypeStruct((C_D_CONV - 1, DEC_BATCH, C_CONV_DIM), f32),
        ],
        scratch_shapes=[
            pltpu.VMEM((SSD_BB, SSD_TP, C_BC_DIM), f32),
            pltpu.VMEM((SSD_BB, SSD_TP, C_BC_DIM), f32),
            pltpu.VMEM((SSD_BB, SSD_TP, C_D_INNER), f32),
            pltpu.VMEM((DEC_SEQ, SSD_BB, C_D_INNER), f32),
            pltpu.VMEM((SSD_BB, C_HEADS, C_D_STATE), f32),
        ],
        compiler_params=_params("parallel"),
        name="ssd_sample",
    )(zx_t, dtr_t, conv_state_t, h0, conv_w, conv_b, dt_bias, a_log, d_skip, norm_w, sel)


def _pad_lanes(v):
    return jnp.pad(v.astype(f32), (0, LANES - v.shape[0])).reshape(1, LANES)


def _mixer_a(xp, xs, g, j, w_in, norm_v, w_sp, b_sp, w_out):
    nv = norm_v.reshape(1, A_HALF)
    xp = mixer_a(xp, g, w_in, nv, w_sp, b_sp.T, w_out, j, tm=TM_PROMPT, sample=False)
    xs, v_s = mixer_a(xs, g, w_in, nv, w_sp[:, :DEC_SEQ, :DEC_SEQ].reshape(-1), b_sp[:, :DEC_SEQ].reshape(-1),
                      w_out, j, tm=TM_SAMPLE, sample=True)
    return xp, xs, jnp.swapaxes(v_s.reshape(DEC_SEQ, DEC_BATCH, A_HALF), 0, 1)


def _mixer_b(xp, xs, g, j, k_cache, v_cache, w_qkv, q_norm, k_norm, sinks, rel_bias):
    qn = q_norm.reshape(1, B_HEAD_DIM)
    kn = k_norm.reshape(1, B_HEAD_DIM)
    table = bias_table(rel_bias)
    n_qkv = B_Q_DIM + 2 * B_KV_DIM

    qkv_p = norm_matmul(xp, g, w_qkv, j, n_qkv, tm=TM_PROMPT, tn=n_qkv, out_dtype=bf16)
    o_p, k_p, v_p = attn_prompt(qkv_p, table, qn, kn, sinks)

    qkv_s = norm_matmul(xs, g, w_qkv, j, n_qkv, tm=TM_SAMPLE, tn=512).reshape(DEC_SEQ, DEC_BATCH, -1)
    q_s = qkv_s[:, :, :B_Q_DIM].reshape(DEC_SEQ, DEC_BATCH, B_KV_HEADS, B_REP, B_HEAD_DIM)
    q_s = q_s.transpose(1, 3, 0, 2, 4).reshape(DEC_BATCH, ATT_QR, B_KV_DIM)
    k_new = jnp.swapaxes(qkv_s[:, :, B_Q_DIM:B_Q_DIM + B_KV_DIM], 0, 1)
    v_new = jnp.swapaxes(qkv_s[:, :, B_Q_DIM + B_KV_DIM:], 0, 1)
    table_s = table[:, :DEC_SEQ, :ATT_KEYS].reshape(B_KV_HEADS, ATT_QR, ATT_KEYS)
    o_s, k_s, v_s = attn_sample(q_s, k_new, v_new,
                                k_cache.reshape(DEC_BATCH, B_WINDOW, B_KV_DIM),
                                v_cache.reshape(DEC_BATCH, B_WINDOW, B_KV_DIM),
                                table_s, qn, kn, sinks)
    o_s = o_s.reshape(DEC_BATCH, B_REP, DEC_SEQ, B_KV_HEADS, B_HEAD_DIM).transpose(2, 0, 3, 1, 4)
    o_s = o_s.reshape(SAMPLE_ROWS, B_Q_DIM).astype(bf16)
    kv_shape_p = (BATCH, B_WINDOW, B_KV_HEADS, B_HEAD_DIM)
    kv_shape_s = (DEC_BATCH, B_WINDOW, B_KV_HEADS, B_HEAD_DIM)
    return o_p, o_s, k_p.reshape(kv_shape_p), v_p.reshape(kv_shape_p), k_s.reshape(kv_shape_s), v_s.reshape(kv_shape_s)


def _mixer_c(xp, xs, g, j, h0, conv_state, w_in, conv_w, conv_b, dt_bias, a_log, d_skip, norm_w):
    n_zx = C_D_INNER + C_CONV_DIM
    w_dt = jnp.pad(w_in[j, :, n_zx:], ((0, 0), (0, LANES - C_HEADS)))
    cb = conv_b.reshape(1, C_CONV_DIM)
    dtb = _pad_lanes(dt_bias)
    alog = _pad_lanes(a_log)
    dsk = jnp.repeat(d_skip.astype(f32), C_HEAD_DIM).reshape(1, C_D_INNER)
    nw = norm_w.reshape(1, C_D_INNER)

    zx_p, dtr_p = norm_matmul(xp, g, w_in, j, n_zx, tm=TM_PROMPT, tn=n_zx // 2, w_tail=w_dt, out_dtype=bf16)
    yn_p, h_p, conv_p = ssd_prompt(zx_p, dtr_p, conv_w, cb, dtb, alog, dsk, nw)

    zx_s, dtr_s = norm_matmul(xs, g, w_in, j, n_zx, tm=TM_SAMPLE, tn=1024, w_tail=w_dt)
    sel = (jnp.arange(LANES)[:, None] == jnp.arange(C_D_INNER)[None, :] // C_HEAD_DIM).astype(f32)
    yn_s, h_s, conv_s = ssd_sample(
        zx_s.reshape(DEC_SEQ, DEC_BATCH, -1), dtr_s.reshape(DEC_SEQ, DEC_BATCH, LANES),
        jnp.swapaxes(conv_state, 0, 1), h0.reshape(DEC_BATCH, C_D_INNER, C_D_STATE),
        conv_w, cb, dtb, alog, dsk, nw, sel)
    st_shape = (C_HEADS, C_HEAD_DIM, C_D_STATE)
    return (yn_p, yn_s.reshape(SAMPLE_ROWS, C_D_INNER), h_p.reshape((BATCH,) + st_shape), conv_p,
            h_s.reshape((DEC_BATCH,) + st_shape), jnp.swapaxes(conv_s, 0, 1))


def kernel(x_prompt, x_sample, cache_swa_k, cache_swa_v, state_ssm, state_conv, norm_mixer, norm_mlp, mlp_w_up, mlp_w_down, a_w_in, a_norm_v, a_w_spatial, a_b_spatial, a_w_out, b_w_qkv, b_q_norm, b_k_norm, b_sinks, rel_bias, b_w_out, c_w_in, c_conv_w, c_conv_b, c_dt_bias, c_a_log, c_d, c_norm, c_w_out):
    xp = x_prompt.reshape(PROMPT_ROWS, D_MODEL)
    xs = jnp.swapaxes(x_sample, 0, 1).reshape(SAMPLE_ROWS, D_MODEL)
    chunk_v_s = []
    swa_kp, swa_vp, swa_ks, swa_vs = [], [], [], []
    ssm_p, conv_p, ssm_s, conv_s = [], [], [], []
    a_w_in, a_w_out, c_w_in = a_w_in.astype(bf16), a_w_out.astype(bf16), c_w_in.astype(bf16)
    b_w_out, c_w_out = b_w_out.astype(bf16), c_w_out.astype(bf16)
    for i in range(DEPTH):
        kind = i % N_MIXERS
        j = i // N_MIXERS
        g = norm_mixer[i].reshape(1, D_MODEL)
        proj_p = proj_s = None
        if kind == 0:
            xp, xs, v_new = _mixer_a(xp, xs, g, j, a_w_in, a_norm_v[j], a_w_spatial[j], a_b_spatial[j], a_w_out)
            chunk_v_s.append(v_new)
        elif kind == 1:
            o_p, o_s, kp, vp, ks_, vs_ = _mixer_b(xp, xs, g, j, cache_swa_k[j], cache_swa_v[j], b_w_qkv, b_q_norm[j],
                                                  b_k_norm[j], b_sinks[j], rel_bias)
            proj_p, proj_s = (o_p, b_w_out, j), (o_s, b_w_out, j)
            swa_kp.append(kp); swa_vp.append(vp); swa_ks.append(ks_); swa_vs.append(vs_)
        else:
            y_p, y_s, hp, bp, hs, bs = _mixer_c(xp, xs, g, j, state_ssm[j], state_conv[j], c_w_in, c_conv_w[j],
                                                c_conv_b[j], c_dt_bias[j], c_a_log[j], c_d[j], c_norm[j])
            proj_p, proj_s = (y_p, c_w_out, j), (y_s, c_w_out, j)
            ssm_p.append(hp); conv_p.append(bp); ssm_s.append(hs); conv_s.append(bs)
        gm = norm_mlp[i].reshape(1, D_MODEL)
        xp = mlp(xp, gm, mlp_w_up, mlp_w_down, i, tm=TM_PROMPT, tf=MLP_TF, proj=proj_p)
        xs = mlp(xs, gm, mlp_w_up, mlp_w_down, i, tm=TM_SAMPLE, tf=MLP_TF, proj=proj_s)
    y_prompt = xp.reshape(BATCH, SEQ, D_MODEL)
    y_sample = jnp.swapaxes(xs.reshape(DEC_SEQ, DEC_BATCH, D_MODEL), 0, 1)
    return (y_prompt, y_sample, jnp.stack(chunk_v_s),
            jnp.stack(swa_kp), jnp.stack(swa_vp), jnp.stack(swa_ks), jnp.stack(swa_vs),
            jnp.stack(ssm_p), jnp.stack(conv_p), jnp.stack(ssm_s), jnp.stack(conv_s))
```

```python
import functools
import math

import jax
import jax.numpy as jnp
import numpy as np
from jax import lax
from jax.experimental import pallas as pl
from jax.experimental.pallas import tpu as pltpu

f32 = jnp.float32
bf16 = jnp.bfloat16

D_MODEL = 1024
BATCH = 4
SEQ = 4096
DEPTH = 4
DEC_BATCH = 128
DEC_SEQ = 4
PAST_LEN = 8192
N_MIXERS = 3
D_FF = 4 * D_MODEL
EPS = 1e-6
NEG_INF = -1e30

A_CHUNK = 128
A_D_FFN = 6 * D_MODEL
A_HALF = A_D_FFN // 2
A_GROUPS = 8
A_GROUP_W = A_HALF // A_GROUPS

B_HEADS = 16
B_KV_HEADS = 4
B_HEAD_DIM = 64
B_REP = B_HEADS // B_KV_HEADS
B_WINDOW = 128
B_BLOCK = 128
B_Q_DIM = B_HEADS * B_HEAD_DIM
B_KV_DIM = B_KV_HEADS * B_HEAD_DIM
N_BUCKETS = 32
MAX_DISTANCE = 128

C_D_INNER = 2 * D_MODEL
C_HEAD_DIM = 64
C_HEADS = C_D_INNER // C_HEAD_DIM
C_GROUPS = 4
C_REP = C_HEADS // C_GROUPS
C_D_STATE = 128
C_D_CONV = 4
C_BC_DIM = C_GROUPS * C_D_STATE
C_CONV_DIM = C_D_INNER + 2 * C_BC_DIM
C_GROUP_W = C_D_INNER // C_GROUPS
C_CHUNK = 128

LANES = 128
SUBLANES = 8
VMEM_LIMIT_BYTES = 56 * 1024 * 1024

PROMPT_ROWS = BATCH * SEQ
SAMPLE_ROWS = DEC_BATCH * DEC_SEQ
TM_PROMPT = 1024
TM_SAMPLE = SAMPLE_ROWS
MLP_TF = 1024


def _params(*sem):
    return pltpu.CompilerParams(dimension_semantics=sem, vmem_limit_bytes=VMEM_LIMIT_BYTES)


def _rms(x, g):
    ms = jnp.mean(x * x, axis=-1, keepdims=True)
    return x * lax.rsqrt(ms + EPS) * g


def _gelu(x):
    return 0.5 * x * (1.0 + lax.erf(x * math.sqrt(0.5)))


def _silu(x):
    return x * jax.nn.sigmoid(x)


def _softplus(x):
    return jnp.maximum(x, 0.0) + jnp.log1p(jnp.exp(-jnp.abs(x)))


def _dot(a, b):
    return jnp.dot(a, b, preferred_element_type=f32)


def _dot_nt(a, b):
    return lax.dot_general(a, b, (((1,), (1,)), ((), ())), preferred_element_type=f32)


def _dot_tn(a, b):
    return lax.dot_general(a, b, (((0,), (0,)), ((), ())), preferred_element_type=f32)


def _dot_exact_lhs01(a01, x):
    a = a01.astype(bf16)
    hi = x.astype(bf16)
    r1 = x - hi.astype(f32)
    mid = r1.astype(bf16)
    lo = (r1 - mid.astype(f32)).astype(bf16)
    return _dot(a, hi) + _dot(a, mid) + _dot(a, lo)


def _dot_exact_rhs01(x, b01):
    b = b01.astype(bf16)
    hi = x.astype(bf16)
    r1 = x - hi.astype(f32)
    mid = r1.astype(bf16)
    lo = (r1 - mid.astype(f32)).astype(bf16)
    return _dot(hi, b) + _dot(mid, b) + _dot(lo, b)


def _norm_matmul_kernel(*refs, nj, tail):
    if tail:
        x_ref, g_ref, w_ref, wt_ref, o_ref, ot_ref, xn_ref = refs
    else:
        x_ref, g_ref, w_ref, o_ref, xn_ref = refs
    j = pl.program_id(1)

    @pl.when(j == 0)
    def _():
        xn_ref[...] = _rms(x_ref[...], g_ref[...]).astype(bf16)

    @pl.when(j < nj)
    def _():
        o_ref[...] = _dot(xn_ref[...], w_ref[0].astype(bf16)).astype(o_ref.dtype)

    if tail:
        @pl.when(j == nj)
        def _():
            ot_ref[...] = _dot(xn_ref[...], wt_ref[...].astype(bf16))


def norm_matmul(x, g, w, layer, n, *, tm, tn, w_tail=None, out_dtype=f32):
    m, k = x.shape
    nj = n // tn
    tail = w_tail is not None
    last = nj - 1
    in_specs = [
        pl.BlockSpec((tm, k), lambda i, j: (i, 0)),
        pl.BlockSpec((1, k), lambda i, j: (0, 0)),
        pl.BlockSpec((1, k, tn), lambda i, j: (layer, 0, jnp.minimum(j, last))),
    ]
    out_specs = [pl.BlockSpec((tm, tn), lambda i, j: (i, jnp.minimum(j, last)))]
    out_shape = [jax.ShapeDtypeStruct((m, n), out_dtype)]
    args = [x, g, w]
    if tail:
        in_specs.append(pl.BlockSpec((k, LANES), lambda i, j: (0, 0)))
        out_specs.append(pl.BlockSpec((tm, LANES), lambda i, j: (i, 0)))
        out_shape.append(jax.ShapeDtypeStruct((m, LANES), f32))
        args.append(w_tail)
    out = pl.pallas_call(
        functools.partial(_norm_matmul_kernel, nj=nj, tail=tail),
        grid=(m // tm, nj + (1 if tail else 0)),
        in_specs=in_specs,
        out_specs=out_specs,
        out_shape=out_shape,
        scratch_shapes=[pltpu.VMEM((tm, k), bf16)],
        compiler_params=_params("parallel", "arbitrary"),
        name="norm_matmul",
    )(*args)
    return out if tail else out[0]


def _mlp_kernel(*refs, proj):
    if proj:
        a_ref, wo_ref, x_ref, g_ref, wu_ref, wd_ref, o_ref, xn_ref = refs
    else:
        x_ref, g_ref, wu_ref, wd_ref, o_ref, xn_ref = refs

    @pl.when(pl.program_id(1) == 0)
    def _():
        x = x_ref[...]
        if proj:
            x = x + _dot(a_ref[...], wo_ref[0])
        xn_ref[...] = _rms(x, g_ref[...]).astype(bf16)
        o_ref[...] = x

    h = jnp.maximum(_dot(xn_ref[...], wu_ref[0].astype(bf16)), 0.0)
    o_ref[...] += _dot((h * h).astype(bf16), wd_ref[0].astype(bf16))


def mlp(x, g, w_up, w_down, layer, *, tm, tf, proj=None):
    m, d = x.shape
    ff = w_up.shape[2]
    in_specs = [
        pl.BlockSpec((tm, d), lambda i, j: (i, 0)),
        pl.BlockSpec((1, d), lambda i, j: (0, 0)),
        pl.BlockSpec((1, d, tf), lambda i, j: (layer, 0, j)),
        pl.BlockSpec((1, tf, d), lambda i, j: (layer, j, 0)),
    ]
    args = [x, g, w_up, w_down]
    if proj is not None:
        a, w_o, lo = proj
        k = a.shape[1]
        in_specs = [pl.BlockSpec((tm, k), lambda i, j: (i, 0)),
                    pl.BlockSpec((1, k, d), lambda i, j: (lo, 0, 0), pipeline_mode=pl.Buffered(1))] + in_specs
        args = [a, w_o] + args
    return pl.pallas_call(
        functools.partial(_mlp_kernel, proj=proj is not None),
        grid=(m // tm, ff // tf),
        in_specs=in_specs,
        out_specs=pl.BlockSpec((tm, d), lambda i, j: (i, 0)),
        out_shape=jax.ShapeDtypeStruct((m, d), f32),
        scratch_shapes=[pltpu.VMEM((tm, d), bf16)],
        compiler_params=_params("parallel", "arbitrary"),
        name="mlp",
    )(*args)


A_BLK_GROUPS = 2
A_BLK = A_BLK_GROUPS * A_GROUP_W
A_NBLK = A_HALF // A_BLK


def _mixer_a_kernel(*refs, sample):
    if sample:
        ws_ref, bs_ref, x_ref, g_ref, win_ref, nv_ref, wout_ref, o_ref, vo_ref, xn_ref, v_ref, ssq_ref, us_ref = refs
    else:
        x_ref, g_ref, win_ref, nv_ref, ws_ref, bs_ref, wout_ref, o_ref, xn_ref, v_ref, ssq_ref, us_ref = refs
    j = pl.program_id(1)
    tm = x_ref.shape[0]

    @pl.when(j == 0)
    def _():
        xn_ref[...] = _rms(x_ref[...], g_ref[...]).astype(bf16)
        ssq_ref[...] = jnp.zeros(ssq_ref.shape, f32)

    for k in range(A_NBLK):
        @pl.when(j == k)
        def _(k=k):
            v = _gelu(_dot(xn_ref[...], win_ref[0].astype(bf16)))
            v_ref[:, k * A_BLK:(k + 1) * A_BLK] = v
            ssq_ref[...] += jnp.sum(v * v, axis=-1, keepdims=True)

    if not sample:
        row = lax.broadcasted_iota(jnp.int32, (A_CHUNK, A_CHUNK), 0)
        col = lax.broadcasted_iota(jnp.int32, (A_CHUNK, A_CHUNK), 1)
        causal = row >= col

    for k in range(A_NBLK):
        @pl.when(j == A_NBLK + k)
        def _(k=k):
            u = _gelu(_dot(xn_ref[...], win_ref[0].astype(bf16)))
            rinv = lax.rsqrt(ssq_ref[...] * (1.0 / A_HALF) + EPS)
            for gg in range(A_BLK_GROUPS):
                g = k * A_BLK_GROUPS + gg
                cols = slice(g * A_GROUP_W, (g + 1) * A_GROUP_W)
                ucols = slice(gg * A_GROUP_W, (gg + 1) * A_GROUP_W)
                vn = v_ref[:, cols] * rinv * nv_ref[:, cols]
                if sample:
                    vo_ref[:, cols] = vn
                    vt = [vn[t * DEC_BATCH:(t + 1) * DEC_BATCH] for t in range(DEC_SEQ)]
                    s_rows = []
                    for t in range(DEC_SEQ):
                        s = ws_ref[(g * DEC_SEQ + t) * DEC_SEQ] * vt[0]
                        for t2 in range(1, t + 1):
                            s = s + ws_ref[(g * DEC_SEQ + t) * DEC_SEQ + t2] * vt[t2]
                        s_rows.append(s + bs_ref[g * DEC_SEQ + t])
                    s = jnp.concatenate(s_rows, axis=0)
                else:
                    w = jnp.where(causal, ws_ref[g], 0.0).astype(bf16)
                    bias = bs_ref[:, g:g + 1]
                    vb = vn.astype(bf16)
                    s = jnp.concatenate(
                        [_dot(w, vb[c * A_CHUNK:(c + 1) * A_CHUNK]) + bias for c in range(tm // A_CHUNK)], axis=0)
                us_ref[:, ucols] = (u[:, ucols] * s).astype(bf16)
            y = _dot(us_ref[...], wout_ref[0].astype(bf16))
            if k == 0:
                o_ref[...] = x_ref[...] + y
            else:
                o_ref[...] += y


def mixer_a(x, g, w_in, norm_v, w_sp, b_sp, w_out, layer, *, tm, sample):
    m, d = x.shape
    nj = 2 * A_NBLK
    row = lambda w: pl.BlockSpec((tm, w), lambda i, j: (i, 0))
    full = lambda *shape: pl.BlockSpec(shape, lambda i, j: (0,) * len(shape))
    smem = pl.BlockSpec(memory_space=pltpu.SMEM)
    win_spec = pl.BlockSpec((1, d, A_BLK), lambda i, j: (layer, 0, (j + A_NBLK) % nj))
    wout_spec = pl.BlockSpec((1, A_BLK, d), lambda i, j: (layer, jnp.maximum(j - A_NBLK, 0), 0))
    if sample:
        in_specs = [smem, smem, row(d), full(1, d), win_spec, full(1, A_HALF), wout_spec]
        args = (w_sp, b_sp, x, g, w_in, norm_v, w_out)
        out_specs = [row(d), row(A_HALF)]
        out_shape = [jax.ShapeDtypeStruct((m, d), f32), jax.ShapeDtypeStruct((m, A_HALF), f32)]
    else:
        in_specs = [row(d), full(1, d), win_spec, full(1, A_HALF), full(A_GROUPS, A_CHUNK, A_CHUNK),
                    full(A_CHUNK, A_GROUPS), wout_spec]
        args = (x, g, w_in, norm_v, w_sp, b_sp, w_out)
        out_specs = row(d)
        out_shape = jax.ShapeDtypeStruct((m, d), f32)
    return pl.pallas_call(
        functools.partial(_mixer_a_kernel, sample=sample),
        grid=(m // tm, nj),
        in_specs=in_specs,
        out_specs=out_specs,
        out_shape=out_shape,
        scratch_shapes=[pltpu.VMEM((tm, d), bf16), pltpu.VMEM((tm, A_HALF), f32), pltpu.VMEM((tm, 1), f32),
                        pltpu.VMEM((tm, A_BLK), bf16)],
        compiler_params=_params("parallel", "arbitrary"),
        name="mixer_a_sample" if sample else "mixer_a_prompt",
    )(*args)


def _bucket_table():
    i = np.arange(B_BLOCK)[:, None]
    j = np.arange(2 * B_BLOCK)[None, :]
    n = np.maximum(B_BLOCK + i - j, 0)
    max_exact = N_BUCKETS // 2
    nf = np.maximum(n, 1).astype(np.float64)
    val = np.log(nf / max_exact) / math.log(MAX_DISTANCE / max_exact) * (N_BUCKETS - max_exact)
    in_window = (n >= max_exact) & (n < B_WINDOW)
    assert np.all(np.abs(val - np.round(val))[in_window & (n != max_exact)] > 1e-3)
    large = np.minimum(max_exact + np.floor(val + 1e-9).astype(np.int64), N_BUCKETS - 1)
    return np.where(n < max_exact, n, large).astype(np.int32)


def _bias_table_kernel(rb_ref, bk_ref, o_ref):
    bk = bk_ref[...]
    for h in range(B_HEADS):
        acc = jnp.zeros(bk.shape, f32)
        for b in range(N_BUCKETS):
            acc = jnp.where(bk == b, rb_ref[b * B_HEADS + h], acc)
        o_ref[h] = acc


def bias_table(rel_bias):
    return pl.pallas_call(
        _bias_table_kernel,
        in_specs=[pl.BlockSpec(memory_space=pltpu.SMEM), pl.BlockSpec(memory_space=pltpu.VMEM)],
        out_specs=pl.BlockSpec(memory_space=pltpu.VMEM),
        out_shape=jax.ShapeDtypeStruct((B_HEADS, B_BLOCK, 2 * B_BLOCK), f32),
        name="bias_table",
    )(rel_bias.reshape(-1), jnp.asarray(_bucket_table()))


def _softmax_with_sink(logits, sink):
    m = jnp.maximum(jnp.max(logits, axis=-1, keepdims=True), sink)
    p = jnp.exp(logits - m)
    return p, jnp.sum(p, axis=-1, keepdims=True) + jnp.exp(sink - m)


def _rms_head_pairs(x, g2, lo):
    sq = x * x
    s_lo = jnp.sum(jnp.where(lo, sq, 0.0), axis=-1, keepdims=True)
    s_hi = jnp.sum(jnp.where(lo, 0.0, sq), axis=-1, keepdims=True)
    r = lax.rsqrt(jnp.where(lo, s_lo, s_hi) * (1.0 / B_HEAD_DIM) + EPS)
    return x * r * g2


def _attn_prompt_kernel(sink_ref, qkv_ref, tab_ref, qn_ref, kn_ref, o_ref, ko_ref, vo_ref,
                        kband_ref, vband_ref, q_ref, p_ref, rhs_ref):
    n = pl.program_id(1)
    T = B_BLOCK

    @pl.when(n == 0)
    def _():
        kband_ref[0:T, :] = jnp.zeros((T, B_KV_DIM), f32)
        vband_ref[0:T, :] = jnp.zeros((T, B_KV_DIM), f32)
        rhs_ref[...] = jnp.ones(rhs_ref.shape, bf16)

    lo = lax.broadcasted_iota(jnp.int32, (1, LANES), 1) < B_HEAD_DIM
    qn2 = qn_ref[...]
    kn2 = kn_ref[...]
    for t in range(B_KV_DIM // LANES):
        lanes = slice(t * LANES, (t + 1) * LANES)
        k2 = _rms_head_pairs(qkv_ref[:, B_Q_DIM + t * LANES:B_Q_DIM + (t + 1) * LANES].astype(f32), kn2, lo)
        ko_ref[0, :, lanes] = k2
        kband_ref[T:, lanes] = k2
    v = qkv_ref[:, B_Q_DIM + B_KV_DIM:].astype(f32)
    vo_ref[0] = v
    vband_ref[T:, :] = v
    for t in range(B_Q_DIM // LANES):
        q2 = qkv_ref[:, t * LANES:(t + 1) * LANES].astype(f32)
        q_ref[t * T:(t + 1) * T, :] = _rms_head_pairs(q2, qn2, lo).astype(bf16)

    key_ops = {}
    band_row = lax.broadcasted_iota(jnp.int32, (2 * T, LANES), 0)
    for t in range(B_KV_DIM // LANES):
        lanes = slice(t * LANES, (t + 1) * LANES)
        kt = kband_ref[:, lanes]
        kr = pltpu.roll(kt, B_HEAD_DIM, axis=1)
        vt = jnp.where(band_row == 0, 0.0, vband_ref[:, lanes])
        vr = pltpu.roll(vt, B_HEAD_DIM, axis=1)
        hi = jnp.logical_not(lo)
        for half, (ksrc, vsrc) in enumerate(((kt, vt), (kr, vr))):
            g_lo, g_hi = (2 * t, 2 * t + 1) if half == 0 else (2 * t + 1, 2 * t)
            key_ops[(g_lo, 0)] = jnp.where(lo, ksrc, 0.0).astype(bf16)
            key_ops[(g_hi, 1)] = jnp.where(hi, ksrc, 0.0).astype(bf16)
            rhs_ref[g_lo * 2 + 0, :, 0:LANES] = jnp.where(lo, vsrc, 1.0).astype(bf16)
            rhs_ref[g_hi * 2 + 1, :, 0:LANES] = jnp.where(hi, vsrc, 1.0).astype(bf16)

    i = lax.broadcasted_iota(jnp.int32, (T, 2 * T), 0)
    j = lax.broadcasted_iota(jnp.int32, (T, 2 * T), 1)
    first_key = jnp.where(n == 0, T, 0)
    valid = (j > i) & (j <= i + B_WINDOW) & (j >= first_key)
    sink_col = j == 0
    lo_t = lax.broadcasted_iota(jnp.int32, (T, LANES), 1) < B_HEAD_DIM
    for g in range(B_KV_HEADS):
        q2 = q_ref[2 * g * T:(2 * g + 2) * T, :]
        res = []
        for half in range(2):
            logits = _dot_nt(q2, key_ops[(g, half)]) * (B_HEAD_DIM ** -0.5)
            for pair in range(2):
                h = g * B_REP + 2 * pair + half
                l = jnp.where(valid, logits[pair * T:(pair + 1) * T] + tab_ref[h], NEG_INF)
                l = jnp.where(sink_col, sink_ref[h], l)
                p = jnp.exp(l - jnp.max(l, axis=-1, keepdims=True))
                p_ref[half, pair * T:(pair + 1) * T, :] = p.astype(bf16)
            res.append(_dot(p_ref[half], rhs_ref[g * 2 + half]))
        for pair in range(2):
            rows = slice(pair * T, (pair + 1) * T)
            even = res[0][rows, 0:LANES] / res[0][rows, LANES:]
            odd = res[1][rows, 0:LANES] / res[1][rows, LANES:]
            t = 2 * g + pair
            o_ref[:, t * LANES:(t + 1) * LANES] = jnp.where(lo_t, even, odd).astype(bf16)
    kband_ref[0:T, :] = kband_ref[T:, :]
    vband_ref[0:T, :] = vband_ref[T:, :]


def attn_prompt(qkv, table, q_norm, k_norm, sinks):
    nb = SEQ // B_BLOCK
    return pl.pallas_call(
        _attn_prompt_kernel,
        grid=(BATCH, nb),
        in_specs=[
            pl.BlockSpec(memory_space=pltpu.SMEM),
            pl.BlockSpec((B_BLOCK, B_Q_DIM + 2 * B_KV_DIM), lambda b, n: (b * nb + n, 0)),
            pl.BlockSpec((B_HEADS, B_BLOCK, 2 * B_BLOCK), lambda b, n: (0, 0, 0)),
            pl.BlockSpec((1, LANES), lambda b, n: (0, 0)),
            pl.BlockSpec((1, LANES), lambda b, n: (0, 0)),
        ],
        out_specs=[
            pl.BlockSpec((B_BLOCK, B_Q_DIM), lambda b, n: (b * nb + n, 0)),
            pl.BlockSpec((1, B_BLOCK, B_KV_DIM), lambda b, n: (b, 0, 0)),
            pl.BlockSpec((1, B_BLOCK, B_KV_DIM), lambda b, n: (b, 0, 0)),
        ],
        out_shape=[
            jax.ShapeDtypeStruct((PROMPT_ROWS, B_Q_DIM), bf16),
            jax.ShapeDtypeStruct((BATCH, B_BLOCK, B_KV_DIM), f32),
            jax.ShapeDtypeStruct((BATCH, B_BLOCK, B_KV_DIM), f32),
        ],
        scratch_shapes=[
            pltpu.VMEM((2 * B_BLOCK, B_KV_DIM), f32),
            pltpu.VMEM((2 * B_BLOCK, B_KV_DIM), f32),
            pltpu.VMEM((B_Q_DIM // LANES * B_BLOCK, LANES), bf16),
            pltpu.VMEM((2, 2 * B_BLOCK, 2 * B_BLOCK), bf16),
            pltpu.VMEM((2 * B_KV_HEADS, 2 * B_BLOCK, 2 * LANES), bf16),
        ],
        compiler_params=_params("parallel", "arbitrary"),
        name="attn_prompt",
    )(sinks, qkv, table, jnp.tile(q_norm, (1, LANES // B_HEAD_DIM)), jnp.tile(k_norm, (1, LANES // B_HEAD_DIM)))


ATT_BB = 8
ATT_QR = B_REP * DEC_SEQ
ATT_KEYS = B_WINDOW + 2 * DEC_SEQ


def _attn_sample_kernel(sink_ref, q_ref, kn_ref, vn_ref, kc_ref, vc_ref, tab_ref, qn_ref, knm_ref,
                        o_ref, ko_ref, vo_ref, kall_ref, vall_ref):
    qn2 = qn_ref[...]
    kn2 = knm_ref[...]
    n_rows = B_KV_HEADS * ATT_QR
    lo = lax.broadcasted_iota(jnp.int32, (1, LANES), 1) < B_HEAD_DIM
    row = lax.broadcasted_iota(jnp.int32, (n_rows, ATT_KEYS), 0)
    j = lax.broadcasted_iota(jnp.int32, (n_rows, ATT_KEYS), 1)
    t = row % DEC_SEQ
    valid = (j > t) & (j <= t + B_WINDOW)
    q_row_group = lax.broadcasted_iota(jnp.int32, (n_rows, B_KV_DIM), 0) // ATT_QR
    q_lane_group = lax.broadcasted_iota(jnp.int32, (n_rows, B_KV_DIM), 1) // B_HEAD_DIM
    own_group = q_row_group == q_lane_group
    o_lane_group = lax.broadcasted_iota(jnp.int32, (ATT_QR, B_KV_DIM), 1) // B_HEAD_DIM
    pad = jnp.zeros((ATT_KEYS - B_WINDOW - DEC_SEQ, B_KV_DIM), f32)
    bias = tab_ref[...]
    sink = sink_ref[...]
    for s in range(ATT_BB):
        kc = kc_ref[s]
        vc = vc_ref[s]
        k_new = jnp.concatenate(
            [_rms_head_pairs(kn_ref[s, :, tt * LANES:(tt + 1) * LANES], kn2, lo) for tt in range(B_KV_DIM // LANES)],
            axis=1)
        v_new = vn_ref[s]
        ko_ref[s, 0:B_WINDOW - DEC_SEQ, :] = kc[DEC_SEQ:, :]
        ko_ref[s, B_WINDOW - DEC_SEQ:, :] = k_new
        vo_ref[s, 0:B_WINDOW - DEC_SEQ, :] = vc[DEC_SEQ:, :]
        vo_ref[s, B_WINDOW - DEC_SEQ:, :] = v_new
        kall_ref[s, 0:B_WINDOW, :] = kc
        kall_ref[s, B_WINDOW:B_WINDOW + DEC_SEQ, :] = k_new
        kall_ref[s, B_WINDOW + DEC_SEQ:, :] = pad
        vall_ref[s, 0:B_WINDOW, :] = vc
        vall_ref[s, B_WINDOW:B_WINDOW + DEC_SEQ, :] = v_new
        vall_ref[s, B_WINDOW + DEC_SEQ:, :] = pad
        q = q_ref[s]
        qn = jnp.concatenate(
            [_rms_head_pairs(q[:, tt * LANES:(tt + 1) * LANES], qn2, lo) for tt in range(B_KV_DIM // LANES)], axis=1)
        q_all = jnp.where(own_group, jnp.concatenate([qn] * B_KV_HEADS, axis=0), 0.0).astype(bf16)
        logits = _dot_nt(q_all, kall_ref[s].astype(bf16)) * (B_HEAD_DIM ** -0.5)
        logits = jnp.where(valid, logits + bias, NEG_INF)
        p, denom = _softmax_with_sink(logits, sink)
        res = _dot(p.astype(bf16), vall_ref[s].astype(bf16)) / denom
        out = jnp.zeros((ATT_QR, B_KV_DIM), f32)
        for g in range(B_KV_HEADS):
            out = jnp.where(o_lane_group == g, res[g * ATT_QR:(g + 1) * ATT_QR, :], out)
        o_ref[s] = out


def attn_sample(q_s, k_new, v_new, k_cache, v_cache, table_s, q_norm, k_norm, sinks):
    blk = lambda *shape: pl.BlockSpec((ATT_BB,) + shape, lambda i: (i,) + (0,) * len(shape))
    full = lambda *shape: pl.BlockSpec(shape, lambda i: (0,) * len(shape))
    return pl.pallas_call(
        _attn_sample_kernel,
        grid=(DEC_BATCH // ATT_BB,),
        in_specs=[
            full(B_KV_HEADS * ATT_QR, 1),
            blk(ATT_QR, B_KV_DIM), blk(DEC_SEQ, B_KV_DIM), blk(DEC_SEQ, B_KV_DIM),
            blk(B_WINDOW, B_KV_DIM), blk(B_WINDOW, B_KV_DIM),
            full(B_KV_HEADS * ATT_QR, ATT_KEYS), full(1, LANES), full(1, LANES),
        ],
        out_specs=[blk(ATT_QR, B_KV_DIM), blk(B_WINDOW, B_KV_DIM), blk(B_WINDOW, B_KV_DIM)],
        out_shape=[
            jax.ShapeDtypeStruct((DEC_BATCH, ATT_QR, B_KV_DIM), f32),
            jax.ShapeDtypeStruct((DEC_BATCH, B_WINDOW, B_KV_DIM), f32),
            jax.ShapeDtypeStruct((DEC_BATCH, B_WINDOW, B_KV_DIM), f32),
        ],
        scratch_shapes=[pltpu.VMEM((ATT_BB, ATT_KEYS, B_KV_DIM), f32), pltpu.VMEM((ATT_BB, ATT_KEYS, B_KV_DIM), f32)],
        compiler_params=_params("parallel"),
        name="attn_sample",
    )(jnp.repeat(sinks, DEC_SEQ).reshape(B_KV_HEADS * ATT_QR, 1), q_s, k_new, v_new, k_cache, v_cache,
      table_s.reshape(B_KV_HEADS * ATT_QR, ATT_KEYS),
      jnp.tile(q_norm, (1, LANES // B_HEAD_DIM)), jnp.tile(k_norm, (1, LANES // B_HEAD_DIM)))


CONV_PAD = 2 * SUBLANES

def _gated_group_norm(y, z, norm_w):
    gt = y * _silu(z)
    parts = []
    for g in range(C_GROUPS):
        gg = gt[:, g * C_GROUP_W:(g + 1) * C_GROUP_W]
        parts.append(gg * lax.rsqrt(jnp.mean(gg * gg, axis=-1, keepdims=True) + EPS))
    return jnp.concatenate(parts, axis=1) * norm_w


LOG2E = math.log2(math.e)


def _expand_heads(v, sel3):
    lane = lax.broadcasted_iota(jnp.int32, (1, LANES), 1)
    v = jnp.where(lane < C_HEADS, v, 0.0)
    hi = v.astype(bf16).astype(f32)
    r1 = v - hi
    mid = r1.astype(bf16).astype(f32)
    lo = r1 - mid
    packed = hi + pltpu.roll(mid, C_HEADS, axis=1) + pltpu.roll(lo, 2 * C_HEADS, axis=1)
    return _dot(packed.astype(bf16), sel3)


def _ssd_prompt_kernel(zx_ref, dtr_ref, cw_ref, cb_ref, dtb_ref, alog_ref, dsk_ref, nw_ref, sel_ref,
                       yn_ref, hfin_ref, cout_ref, xpad_ref, ht_ref, y_ref):
    c = pl.program_id(1)
    T = C_CHUNK

    @pl.when(c == 0)
    def _():
        xpad_ref[0:CONV_PAD, :] = jnp.zeros((CONV_PAD, C_CONV_DIM), bf16)
        ht_ref[...] = jnp.zeros(ht_ref.shape, f32)

    xpad_ref[CONV_PAD:, :] = zx_ref[:, C_D_INNER:]
    xp = xpad_ref[...]
    xbc = zx_ref[:, C_D_INNER:].astype(f32)
    cw = cw_ref[...]
    acc = cb_ref[...]
    srow = lax.broadcasted_iota(jnp.int32, (T, CONV_PAD + T), 0)
    scol = lax.broadcasted_iota(jnp.int32, (T, CONV_PAD + T), 1)
    for tap in range(C_D_CONV - 1):
        shift = (scol == srow + (CONV_PAD - (C_D_CONV - 1 - tap))).astype(bf16)
        acc = acc + _dot(shift, xp) * cw[tap:tap + 1, :]
    acc = acc + xbc * cw[C_D_CONV - 1:C_D_CONV, :]
    xpad_ref[0:CONV_PAD, :] = xpad_ref[T:, :]
    cout_ref[0] = xbc[T - (C_D_CONV - 1):, :]
    act = _silu(acc)
    xs = act[:, :C_D_INNER]
    bm = act[:, C_D_INNER:C_D_INNER + C_BC_DIM]
    cm = act[:, C_D_INNER + C_BC_DIM:]
    xb = xs.astype(bf16)

    dt = _softplus(dtr_ref[...] + dtb_ref[...])
    a_neg = -jnp.exp(alog_ref[...])
    row = lax.broadcasted_iota(jnp.int32, (T, T), 0)
    col = lax.broadcasted_iota(jnp.int32, (T, T), 1)
    causal = row >= col
    acs = _dot_exact_lhs01(causal.astype(f32), dt * a_neg)
    a2 = acs * LOG2E
    sel3 = sel_ref[...]
    e_exp = jnp.exp2(_expand_heads(a2, sel3))
    w_exp = _expand_heads(jnp.exp(acs[T - 1:T, :] - acs) * dt, sel3)
    cdec = e_exp[T - 1:T, :]
    b2_t = a2.T - jnp.log2(dt.T)
    xw = (xs * w_exp).astype(bf16)
    hb = ht_ref[...].astype(bf16)
    dsk = dsk_ref[...]
    lo_t = lax.broadcasted_iota(jnp.int32, (T, LANES), 1) < C_HEAD_DIM

    for g in range(C_GROUPS):
        ns = slice(g * C_D_STATE, (g + 1) * C_D_STATE)
        gs = slice(g * C_GROUP_W, (g + 1) * C_GROUP_W)
        b_g = bm[:, ns]
        c_g = cm[:, ns].astype(bf16)
        cb = _dot_nt(c_g, b_g.astype(bf16))
        yi = _dot(c_g, hb[:, gs])
        ht_ref[:, gs] = ht_ref[:, gs] * cdec[:, gs] + _dot(b_g.T.astype(bf16), xw[:, gs])
        for tt in range(C_GROUP_W // LANES):
            t = g * (C_GROUP_W // LANES) + tt
            lanes = slice(t * LANES, (t + 1) * LANES)
            xt = xb[:, lanes]
            res = []
            for half in range(2):
                h = 2 * t + half
                a_col = jnp.broadcast_to(a2[:, h:h + 1], (T, T))
                b_row = jnp.broadcast_to(b2_t[h:h + 1, :], (T, T))
                w = jnp.where(causal, cb * jnp.exp2(a_col - b_row), 0.0)
                res.append(_dot(w.astype(bf16), xt))
            y_intra = jnp.where(lo_t, res[0], res[1])
            y_ref[:, lanes] = y_intra + e_exp[:, lanes] * yi[:, tt * LANES:(tt + 1) * LANES] + dsk[:, lanes] * xs[:, lanes]

    yn_ref[...] = _gated_group_norm(y_ref[...], zx_ref[:, :C_D_INNER].astype(f32), nw_ref[...]).astype(bf16)

    @pl.when(c == pl.num_programs(1) - 1)
    def _():
        for t in range(C_D_INNER // LANES):
            hfin_ref[0, t * LANES:(t + 1) * LANES, :] = ht_ref[:, t * LANES:(t + 1) * LANES].T


def _head_select3():
    k = np.arange(LANES)[:, None]
    ch = np.arange(C_D_INNER)[None, :] // C_HEAD_DIM
    return jnp.asarray((k % C_HEADS == ch) & (k < 3 * C_HEADS), dtype=bf16)


def ssd_prompt(zx, dtr, conv_w, conv_b, dt_bias, a_log, d_skip, norm_w):
    nc = SEQ // C_CHUNK
    full = lambda *shape: pl.BlockSpec(shape, lambda b, c: (0,) * len(shape))
    return pl.pallas_call(
        _ssd_prompt_kernel,
        grid=(BATCH, nc),
        in_specs=[
            pl.BlockSpec((C_CHUNK, C_D_INNER + C_CONV_DIM), lambda b, c: (b * nc + c, 0)),
            pl.BlockSpec((C_CHUNK, LANES), lambda b, c: (b * nc + c, 0)),
            full(C_D_CONV, C_CONV_DIM), full(1, C_CONV_DIM), full(1, LANES), full(1, LANES),
            full(1, C_D_INNER), full(1, C_D_INNER), full(LANES, C_D_INNER),
        ],
        out_specs=[
            pl.BlockSpec((C_CHUNK, C_D_INNER), lambda b, c: (b * nc + c, 0)),
            pl.BlockSpec((1, C_D_INNER, C_D_STATE), lambda b, c: (b, 0, 0)),
            pl.BlockSpec((1, C_D_CONV - 1, C_CONV_DIM), lambda b, c: (b, 0, 0)),
        ],
        out_shape=[
            jax.ShapeDtypeStruct((PROMPT_ROWS, C_D_INNER), bf16),
            jax.ShapeDtypeStruct((BATCH, C_D_INNER, C_D_STATE), f32),
            jax.ShapeDtypeStruct((BATCH, C_D_CONV - 1, C_CONV_DIM), f32),
        ],
        scratch_shapes=[
            pltpu.VMEM((CONV_PAD + C_CHUNK, C_CONV_DIM), bf16),
            pltpu.VMEM((C_D_STATE, C_D_INNER), f32),
            pltpu.VMEM((C_CHUNK, C_D_INNER), f32),
        ],
        compiler_params=_params("parallel", "arbitrary"),
        name="ssd_prompt",
    )(zx, dtr, conv_w, conv_b, dt_bias, a_log, d_skip, norm_w, _head_select3())


SSD_BB = 8
SSD_TP = SUBLANES
_N_PAIRS = DEC_SEQ * (DEC_SEQ + 1) // 2
_N_COEF = _N_PAIRS + 2 * DEC_SEQ


def _ssd_sample_kernel(zx_ref, dtr_ref, cs_ref, h0_ref, cw_ref, cb_ref, dtb_ref, alog_ref, dsk_ref, nw_ref,
                       sel_ref, yn_ref, hn_ref, cout_ref, c_scr, b_scr, xw_scr, yi_scr, cd_scr):
    L = DEC_SEQ
    cw = cw_ref[...]
    xp = [cs_ref[k] for k in range(C_D_CONV - 1)] + [zx_ref[t, :, C_D_INNER:] for t in range(L)]
    for k in range(C_D_CONV - 1):
        cout_ref[k] = xp[L + k]
    act = []
    for t in range(L):
        acc = cb_ref[...]
        for tap in range(C_D_CONV):
            acc = acc + xp[t + tap] * cw[tap:tap + 1, :]
        act.append(_silu(acc))
    xs = [a[:, :C_D_INNER] for a in act]
    bm = [a[:, C_D_INNER:C_D_INNER + C_BC_DIM] for a in act]
    cm = [a[:, C_D_INNER + C_BC_DIM:] for a in act]

    a_neg = -jnp.exp(alog_ref[...])
    dt = [_softplus(dtr_ref[t] + dtb_ref[...]) for t in range(L)]
    acs = []
    for t in range(L):
        acs.append(dt[t] * a_neg if t == 0 else acs[t - 1] + dt[t] * a_neg)

    lane_group = lax.broadcasted_iota(jnp.int32, (SSD_BB, LANES), 1) // C_REP
    coefs = []
    for t in range(L):
        for t2 in range(t + 1):
            cbh = jnp.zeros((SSD_BB, LANES), f32)
            for g in range(C_GROUPS):
                ns = slice(g * C_D_STATE, (g + 1) * C_D_STATE)
                cbg = jnp.sum(cm[t][:, ns] * bm[t2][:, ns], axis=-1, keepdims=True)
                cbh = jnp.where(lane_group == g, cbg, cbh)
            coefs.append(cbh * jnp.exp(acs[t] - acs[t2]) * dt[t2])
    for t in range(L):
        coefs.append(jnp.exp(acs[t]))
    for t in range(L):
        coefs.append(jnp.exp(acs[L - 1] - acs[t]) * dt[t])
    coef = jnp.concatenate(coefs, axis=0)
    cexp = _dot_exact_rhs01(coef, sel_ref[...])
    cexp = [cexp[k * SSD_BB:(k + 1) * SSD_BB, :] for k in range(_N_COEF)]
    w_intra = cexp[:_N_PAIRS]
    w_inter = cexp[_N_PAIRS:_N_PAIRS + L]
    w_state = cexp[_N_PAIRS + L:]

    cd = jnp.concatenate([jnp.exp(acs[L - 1]), jnp.zeros((LANES - SSD_BB, LANES), f32)], axis=0)
    cd_t = cd.T
    for s in range(SSD_BB):
        cd_scr[s] = jnp.broadcast_to(cd_t[0:C_HEADS, s:s + 1], (C_HEADS, C_D_STATE))

    zeros_tail = jnp.zeros((SSD_BB, SSD_TP - L, C_D_INNER), f32)
    c_scr[:, L:, :] = zeros_tail[:, :, :C_BC_DIM]
    b_scr[:, L:, :] = zeros_tail[:, :, :C_BC_DIM]
    xw_scr[:, L:, :] = zeros_tail
    for t in range(L):
        xw_t = xs[t] * w_state[t]
        for s in range(SSD_BB):
            c_scr[s, t:t + 1, :] = cm[t][s:s + 1, :]
            b_scr[s, t:t + 1, :] = bm[t][s:s + 1, :]
            xw_scr[s, t:t + 1, :] = xw_t[s:s + 1, :]

    for s in range(SSD_BB):
        for g in range(C_GROUPS):
            ns = slice(g * C_D_STATE, (g + 1) * C_D_STATE)
            gs = slice(g * C_GROUP_W, (g + 1) * C_GROUP_W)
            h0 = h0_ref[s, gs, :]
            yi = _dot_nt(c_scr[s, :, ns].astype(bf16), h0.astype(bf16))
            for t in range(L):
                yi_scr[t, s:s + 1, gs] = yi[t:t + 1, :]
            st = _dot_tn(xw_scr[s, :, gs].astype(bf16), b_scr[s, :, ns].astype(bf16))
            for r in range(C_REP):
                h = g * C_REP + r
                rs = slice(r * C_HEAD_DIM, (r + 1) * C_HEAD_DIM)
                scale = cd_scr[s, h:h + 1, :]
                hn_ref[s, h * C_HEAD_DIM:(h + 1) * C_HEAD_DIM, :] = h0[rs, :] * scale + st[rs, :]

    dsk = dsk_ref[...]
    nw = nw_ref[...]
    pair = 0
    for t in range(L):
        y = w_inter[t] * yi_scr[t] + dsk * xs[t]
        for t2 in range(t + 1):
            y = y + w_intra[pair] * xs[t2]
            pair += 1
        yn_ref[t] = _gated_group_norm(y, zx_ref[t, :, :C_D_INNER], nw).astype(bf16)


def ssd_sample(zx_t, dtr_t, conv_state_t, h0, conv_w, conv_b, dt_bias, a_log, d_skip, norm_w, sel):
    tmaj = lambda n, w: pl.BlockSpec((n, SSD_BB, w), lambda i: (0, i, 0))
    full = lambda *shape: pl.BlockSpec(shape, lambda i: (0,) * len(shape))
    return pl.pallas_call(
        _ssd_sample_kernel,
        grid=(DEC_BATCH // SSD_BB,),
        in_specs=[
            tmaj(DEC_SEQ, C_D_INNER + C_CONV_DIM), tmaj(DEC_SEQ, LANES), tmaj(C_D_CONV - 1, C_CONV_DIM),
            pl.BlockSpec((SSD_BB, C_D_INNER, C_D_STATE), lambda i: (i, 0, 0)),
            full(C_D_CONV, C_CONV_DIM), full(1, C_CONV_DIM), full(1, LANES), full(1, LANES),
            full(1, C_D_INNER), full(1, C_D_INNER), full(LANES, C_D_INNER),
        ],
        out_specs=[
            tmaj(DEC_SEQ, C_D_INNER),
            pl.BlockSpec((SSD_BB, C_D_INNER, C_D_STATE), lambda i: (i, 0, 0)),
            tmaj(C_D_CONV - 1, C_CONV_DIM),
        ],
        out_shape=[
            jax.ShapeDtypeStruct((DEC_SEQ, DEC_BATCH, C_D_INNER), bf16),
            jax.ShapeDtypeStruct((DEC_BATCH, C_D_INNER, C_D_STATE), f32),
            jax.ShapeDtypeStruct((C_D_CONV - 1, DEC_BATCH, C_CONV_DIM), f32),
        ],
        scratch_shapes=[
            pltpu.VMEM((SSD_BB, SSD_TP, C_BC_DIM), f32),
            pltpu.VMEM((SSD_BB, SSD_TP, C_BC_DIM), f32),
            pltpu.VMEM((SSD_BB, SSD_TP, C_D_INNER), f32),
            pltpu.VMEM((DEC_SEQ, SSD_BB, C_D_INNER), f32),
            pltpu.VMEM((SSD_BB, C_HEADS, C_D_STATE), f32),
        ],
        compiler_params=_params("parallel"),
        name="ssd_sample",
    )(zx_t, dtr_t, conv_state_t, h0, conv_w, conv_b, dt_bias, a_log, d_skip, norm_w, sel)


def _pad_lanes(v):
    return jnp.pad(v.astype(f32), (0, LANES - v.shape[0])).reshape(1, LANES)


def _mixer_a(xp, xs, g, j, w_in, norm_v, w_sp, b_sp, w_out):
    nv = norm_v.reshape(1, A_HALF)
    xp = mixer_a(xp, g, w_in, nv, w_sp, b_sp.T, w_out, j, tm=TM_PROMPT, sample=False)
    xs, v_s = mixer_a(xs, g, w_in, nv, w_sp[:, :DEC_SEQ, :DEC_SEQ].reshape(-1), b_sp[:, :DEC_SEQ].reshape(-1),
                      w_out, j, tm=TM_SAMPLE, sample=True)
    return xp, xs, jnp.swapaxes(v_s.reshape(DEC_SEQ, DEC_BATCH, A_HALF), 0, 1)


def _mixer_b(xp, xs, g, j, k_cache, v_cache, w_qkv, q_norm, k_norm, sinks, rel_bias):
    qn = q_norm.reshape(1, B_HEAD_DIM)
    kn = k_norm.reshape(1, B_HEAD_DIM)
    table = bias_table(rel_bias)
    n_qkv = B_Q_DIM + 2 * B_KV_DIM

    qkv_p = norm_matmul(xp, g, w_qkv, j, n_qkv, tm=2 * TM_PROMPT, tn=n_qkv, out_dtype=bf16)
    o_p, k_p, v_p = attn_prompt(qkv_p, table, qn, kn, sinks)

    qkv_s = norm_matmul(xs, g, w_qkv, j, n_qkv, tm=TM_SAMPLE, tn=512).reshape(DEC_SEQ, DEC_BATCH, -1)
    q_s = qkv_s[:, :, :B_Q_DIM].reshape(DEC_SEQ, DEC_BATCH, B_KV_HEADS, B_REP, B_HEAD_DIM)
    q_s = q_s.transpose(1, 3, 0, 2, 4).reshape(DEC_BATCH, ATT_QR, B_KV_DIM)
    k_new = jnp.swapaxes(qkv_s[:, :, B_Q_DIM:B_Q_DIM + B_KV_DIM], 0, 1)
    v_new = jnp.swapaxes(qkv_s[:, :, B_Q_DIM + B_KV_DIM:], 0, 1)
    table_s = table[:, :DEC_SEQ, :ATT_KEYS].reshape(B_KV_HEADS, ATT_QR, ATT_KEYS)
    o_s, k_s, v_s = attn_sample(q_s, k_new, v_new,
                                k_cache.reshape(DEC_BATCH, B_WINDOW, B_KV_DIM),
                                v_cache.reshape(DEC_BATCH, B_WINDOW, B_KV_DIM),
                                table_s, qn, kn, sinks)
    o_s = o_s.reshape(DEC_BATCH, B_REP, DEC_SEQ, B_KV_HEADS, B_HEAD_DIM).transpose(2, 0, 3, 1, 4)
    o_s = o_s.reshape(SAMPLE_ROWS, B_Q_DIM).astype(bf16)
    kv_shape_p = (BATCH, B_WINDOW, B_KV_HEADS, B_HEAD_DIM)
    kv_shape_s = (DEC_BATCH, B_WINDOW, B_KV_HEADS, B_HEAD_DIM)
    return o_p, o_s, k_p.reshape(kv_shape_p), v_p.reshape(kv_shape_p), k_s.reshape(kv_shape_s), v_s.reshape(kv_shape_s)


def _mixer_c(xp, xs, g, j, h0, conv_state, w_in, conv_w, conv_b, dt_bias, a_log, d_skip, norm_w):
    n_zx = C_D_INNER + C_CONV_DIM
    w_dt = jnp.pad(w_in[j, :, n_zx:], ((0, 0), (0, LANES - C_HEADS)))
    cb = conv_b.reshape(1, C_CONV_DIM)
    dtb = _pad_lanes(dt_bias)
    alog = _pad_lanes(a_log)
    dsk = jnp.repeat(d_skip.astype(f32), C_HEAD_DIM).reshape(1, C_D_INNER)
    nw = norm_w.reshape(1, C_D_INNER)

    zx_p, dtr_p = norm_matmul(xp, g, w_in, j, n_zx, tm=2 * TM_PROMPT, tn=1024, w_tail=w_dt, out_dtype=bf16)
    yn_p, h_p, conv_p = ssd_prompt(zx_p, dtr_p, conv_w, cb, dtb, alog, dsk, nw)

    zx_s, dtr_s = norm_matmul(xs, g, w_in, j, n_zx, tm=TM_SAMPLE, tn=1024, w_tail=w_dt)
    sel = (jnp.arange(LANES)[:, None] == jnp.arange(C_D_INNER)[None, :] // C_HEAD_DIM).astype(f32)
    yn_s, h_s, conv_s = ssd_sample(
        zx_s.reshape(DEC_SEQ, DEC_BATCH, -1), dtr_s.reshape(DEC_SEQ, DEC_BATCH, LANES),
        jnp.swapaxes(conv_state, 0, 1), h0.reshape(DEC_BATCH, C_D_INNER, C_D_STATE),
        conv_w, cb, dtb, alog, dsk, nw, sel)
    st_shape = (C_HEADS, C_HEAD_DIM, C_D_STATE)
    return (yn_p, yn_s.reshape(SAMPLE_ROWS, C_D_INNER), h_p.reshape((BATCH,) + st_shape), conv_p,
            h_s.reshape((DEC_BATCH,) + st_shape), jnp.swapaxes(conv_s, 0, 1))


def kernel(x_prompt, x_sample, cache_swa_k, cache_swa_v, state_ssm, state_conv, norm_mixer, norm_mlp, mlp_w_up, mlp_w_down, a_w_in, a_norm_v, a_w_spatial, a_b_spatial, a_w_out, b_w_qkv, b_q_norm, b_k_norm, b_sinks, rel_bias, b_w_out, c_w_in, c_conv_w, c_conv_b, c_dt_bias, c_a_log, c_d, c_norm, c_w_out):
    xp = x_prompt.reshape(PROMPT_ROWS, D_MODEL)
    xs = jnp.swapaxes(x_sample, 0, 1).reshape(SAMPLE_ROWS, D_MODEL)
    chunk_v_s = []
    swa_kp, swa_vp, swa_ks, swa_vs = [], [], [], []
    ssm_p, conv_p, ssm_s, conv_s = [], [], [], []
    a_w_in, a_w_out, c_w_in = a_w_in.astype(bf16), a_w_out.astype(bf16), c_w_in.astype(bf16)
    b_w_out, c_w_out = b_w_out.astype(bf16), c_w_out.astype(bf16)
    for i in range(DEPTH):
        kind = i % N_MIXERS
        j = i // N_MIXERS
        g = norm_mixer[i].reshape(1, D_MODEL)
        proj_p = proj_s = None
        if kind == 0:
            xp, xs, v_new = _mixer_a(xp, xs, g, j, a_w_in, a_norm_v[j], a_w_spatial[j], a_b_spatial[j], a_w_out)
            chunk_v_s.append(v_new)
        elif kind == 1:
            o_p, o_s, kp, vp, ks_, vs_ = _mixer_b(xp, xs, g, j, cache_swa_k[j], cache_swa_v[j], b_w_qkv, b_q_norm[j],
                                                  b_k_norm[j], b_sinks[j], rel_bias)
            proj_p, proj_s = (o_p, b_w_out, j), (o_s, b_w_out, j)
            swa_kp.append(kp); swa_vp.append(vp); swa_ks.append(ks_); swa_vs.append(vs_)
        else:
            y_p, y_s, hp, bp, hs, bs = _mixer_c(xp, xs, g, j, state_ssm[j], state_conv[j], c_w_in, c_conv_w[j],
                                                c_conv_b[j], c_dt_bias[j], c_a_log[j], c_d[j], c_norm[j])
            proj_p, proj_s = (y_p, c_w_out, j), (y_s, c_w_out, j)
            ssm_p.append(hp); conv_p.append(bp); ssm_s.append(hs); conv_s.append(bs)
        gm = norm_mlp[i].reshape(1, D_MODEL)
        xp = mlp(xp, gm, mlp_w_up, mlp_w_down, i, tm=TM_PROMPT, tf=MLP_TF, proj=proj_p)
        xs = mlp(xs, gm, mlp_w_up, mlp_w_down, i, tm=TM_SAMPLE, tf=MLP_TF, proj=proj_s)
    y_prompt = xp.reshape(BATCH, SEQ, D_MODEL)
    y_sample = jnp.swapaxes(xs.reshape(DEC_SEQ, DEC_BATCH, D_MODEL), 0, 1)
    return (y_prompt, y_sample, jnp.stack(chunk_v_s),
            jnp.stack(swa_kp), jnp.stack(swa_vp), jnp.stack(swa_ks), jnp.stack(swa_vs),
            jnp.stack(ssm_p), jnp.stack(conv_p), jnp.stack(ssm_s), jnp.stack(conv_s))
```

```python
import functools
import math

import jax
import jax.numpy as jnp
import numpy as np
from jax import lax
from jax.experimental import pallas as pl
from jax.experimental.pallas import tpu as pltpu

f32 = jnp.float32
bf16 = jnp.bfloat16

D_MODEL = 1024
BATCH = 4
SEQ = 4096
DEPTH = 4
DEC_BATCH = 128
DEC_SEQ = 4
PAST_LEN = 8192
N_MIXERS = 3
D_FF = 4 * D_MODEL
EPS = 1e-6
NEG_INF = -1e30

A_CHUNK = 128
A_D_FFN = 6 * D_MODEL
A_HALF = A_D_FFN // 2
A_GROUPS = 8
A_GROUP_W = A_HALF // A_GROUPS

B_HEADS = 16
B_KV_HEADS = 4
B_HEAD_DIM = 64
B_REP = B_HEADS // B_KV_HEADS
B_WINDOW = 128
B_BLOCK = 128
B_Q_DIM = B_HEADS * B_HEAD_DIM
B_KV_DIM = B_KV_HEADS * B_HEAD_DIM
N_BUCKETS = 32
MAX_DISTANCE = 128

C_D_INNER = 2 * D_MODEL
C_HEAD_DIM = 64
C_HEADS = C_D_INNER // C_HEAD_DIM
C_GROUPS = 4
C_REP = C_HEADS // C_GROUPS
C_D_STATE = 128
C_D_CONV = 4
C_BC_DIM = C_GROUPS * C_D_STATE
C_CONV_DIM = C_D_INNER + 2 * C_BC_DIM
C_GROUP_W = C_D_INNER // C_GROUPS
C_CHUNK = 128

LANES = 128
SUBLANES = 8
VMEM_LIMIT_BYTES = 56 * 1024 * 1024

PROMPT_ROWS = BATCH * SEQ
SAMPLE_ROWS = DEC_BATCH * DEC_SEQ
TM_PROMPT = 1024
TM_SAMPLE = SAMPLE_ROWS
MLP_TF = 1024


def _params(*sem):
    return pltpu.CompilerParams(dimension_semantics=sem, vmem_limit_bytes=VMEM_LIMIT_BYTES)


def _rms(x, g):
    ms = jnp.mean(x * x, axis=-1, keepdims=True)
    return x * lax.rsqrt(ms + EPS) * g


def _gelu(x):
    return 0.5 * x * (1.0 + lax.erf(x * math.sqrt(0.5)))


def _silu(x):
    return x * jax.nn.sigmoid(x)


def _softplus(x):
    return jnp.maximum(x, 0.0) + jnp.log1p(jnp.exp(-jnp.abs(x)))


def _dot(a, b):
    return jnp.dot(a, b, preferred_element_type=f32)


def _dot_nt(a, b):
    return lax.dot_general(a, b, (((1,), (1,)), ((), ())), preferred_element_type=f32)


def _dot_tn(a, b):
    return lax.dot_general(a, b, (((0,), (0,)), ((), ())), preferred_element_type=f32)


def _dot_exact_lhs01(a01, x):
    a = a01.astype(bf16)
    hi = x.astype(bf16)
    r1 = x - hi.astype(f32)
    mid = r1.astype(bf16)
    lo = (r1 - mid.astype(f32)).astype(bf16)
    return _dot(a, hi) + _dot(a, mid) + _dot(a, lo)


def _dot_exact_rhs01(x, b01):
    b = b01.astype(bf16)
    hi = x.astype(bf16)
    r1 = x - hi.astype(f32)
    mid = r1.astype(bf16)
    lo = (r1 - mid.astype(f32)).astype(bf16)
    return _dot(hi, b) + _dot(mid, b) + _dot(lo, b)


def _norm_matmul_kernel(*refs, nj, tail):
    if tail:
        x_ref, g_ref, w_ref, wt_ref, o_ref, ot_ref, xn_ref = refs
    else:
        x_ref, g_ref, w_ref, o_ref, xn_ref = refs
    j = pl.program_id(1)

    @pl.when(j == 0)
    def _():
        xn_ref[...] = _rms(x_ref[...], g_ref[...]).astype(bf16)

    @pl.when(j < nj)
    def _():
        o_ref[...] = _dot(xn_ref[...], w_ref[0].astype(bf16)).astype(o_ref.dtype)

    if tail:
        @pl.when(j == nj)
        def _():
            ot_ref[...] = _dot(xn_ref[...], wt_ref[...].astype(bf16))


def norm_matmul(x, g, w, layer, n, *, tm, tn, w_tail=None, out_dtype=f32):
    m, k = x.shape
    nj = n // tn
    tail = w_tail is not None
    last = nj - 1
    in_specs = [
        pl.BlockSpec((tm, k), lambda i, j: (i, 0)),
        pl.BlockSpec((1, k), lambda i, j: (0, 0)),
        pl.BlockSpec((1, k, tn), lambda i, j: (layer, 0, jnp.minimum(j, last))),
    ]
    out_specs = [pl.BlockSpec((tm, tn), lambda i, j: (i, jnp.minimum(j, last)))]
    out_shape = [jax.ShapeDtypeStruct((m, n), out_dtype)]
    args = [x, g, w]
    if tail:
        in_specs.append(pl.BlockSpec((k, LANES), lambda i, j: (0, 0)))
        out_specs.append(pl.BlockSpec((tm, LANES), lambda i, j: (i, 0)))
        out_shape.append(jax.ShapeDtypeStruct((m, LANES), f32))
        args.append(w_tail)
    out = pl.pallas_call(
        functools.partial(_norm_matmul_kernel, nj=nj, tail=tail),
        grid=(m // tm, nj + (1 if tail else 0)),
        in_specs=in_specs,
        out_specs=out_specs,
        out_shape=out_shape,
        scratch_shapes=[pltpu.VMEM((tm, k), bf16)],
        compiler_params=_params("parallel", "arbitrary"),
        name="norm_matmul",
    )(*args)
    return out if tail else out[0]


def _mlp_kernel(*refs, proj):
    if proj:
        a_ref, wo_ref, x_ref, g_ref, wu_ref, wd_ref, o_ref, xn_ref = refs
    else:
        x_ref, g_ref, wu_ref, wd_ref, o_ref, xn_ref = refs

    @pl.when(pl.program_id(1) == 0)
    def _():
        x = x_ref[...]
        if proj:
            x = x + _dot(a_ref[...], wo_ref[0])
        xn_ref[...] = _rms(x, g_ref[...]).astype(bf16)
        o_ref[...] = x

    h = jnp.maximum(_dot(xn_ref[...], wu_ref[0].astype(bf16)), 0.0)
    o_ref[...] += _dot((h * h).astype(bf16), wd_ref[0].astype(bf16))


def mlp(x, g, w_up, w_down, layer, *, tm, tf, proj=None):
    m, d = x.shape
    ff = w_up.shape[2]
    in_specs = [
        pl.BlockSpec((tm, d), lambda i, j: (i, 0)),
        pl.BlockSpec((1, d), lambda i, j: (0, 0)),
        pl.BlockSpec((1, d, tf), lambda i, j: (layer, 0, j)),
        pl.BlockSpec((1, tf, d), lambda i, j: (layer, j, 0)),
    ]
    args = [x, g, w_up, w_down]
    if proj is not None:
        a, w_o, lo = proj
        k = a.shape[1]
        in_specs = [pl.BlockSpec((tm, k), lambda i, j: (i, 0)),
                    pl.BlockSpec((1, k, d), lambda i, j: (lo, 0, 0), pipeline_mode=pl.Buffered(1))] + in_specs
        args = [a, w_o] + args
    return pl.pallas_call(
        functools.partial(_mlp_kernel, proj=proj is not None),
        grid=(m // tm, ff // tf),
        in_specs=in_specs,
        out_specs=pl.BlockSpec((tm, d), lambda i, j: (i, 0)),
        out_shape=jax.ShapeDtypeStruct((m, d), f32),
        scratch_shapes=[pltpu.VMEM((tm, d), bf16)],
        compiler_params=_params("parallel", "arbitrary"),
        name="mlp",
    )(*args)


A_BLK_GROUPS = 2
A_BLK = A_BLK_GROUPS * A_GROUP_W
A_NBLK = A_HALF // A_BLK


def _mixer_a_kernel(*refs, sample):
    if sample:
        ws_ref, bs_ref, x_ref, g_ref, win_ref, nv_ref, wout_ref, o_ref, vo_ref, xn_ref, v_ref, ssq_ref, us_ref = refs
    else:
        x_ref, g_ref, win_ref, nv_ref, ws_ref, bs_ref, wout_ref, o_ref, xn_ref, v_ref, ssq_ref, us_ref = refs
    j = pl.program_id(1)
    tm = x_ref.shape[0]

    @pl.when(j == 0)
    def _():
        xn_ref[...] = _rms(x_ref[...], g_ref[...]).astype(bf16)
        ssq_ref[...] = jnp.zeros(ssq_ref.shape, f32)

    for k in range(A_NBLK):
        @pl.when(j == k)
        def _(k=k):
            v = _gelu(_dot(xn_ref[...], win_ref[0].astype(bf16)))
            v_ref[:, k * A_BLK:(k + 1) * A_BLK] = v
            ssq_ref[...] += jnp.sum(v * v, axis=-1, keepdims=True)

    if not sample:
        row = lax.broadcasted_iota(jnp.int32, (A_CHUNK, A_CHUNK), 0)
        col = lax.broadcasted_iota(jnp.int32, (A_CHUNK, A_CHUNK), 1)
        causal = row >= col

    for k in range(A_NBLK):
        @pl.when(j == A_NBLK + k)
        def _(k=k):
            u = _gelu(_dot(xn_ref[...], win_ref[0].astype(bf16)))
            rinv = lax.rsqrt(ssq_ref[...] * (1.0 / A_HALF) + EPS)
            for gg in range(A_BLK_GROUPS):
                g = k * A_BLK_GROUPS + gg
                cols = slice(g * A_GROUP_W, (g + 1) * A_GROUP_W)
                ucols = slice(gg * A_GROUP_W, (gg + 1) * A_GROUP_W)
                vn = v_ref[:, cols] * rinv * nv_ref[:, cols]
                if sample:
                    vo_ref[:, cols] = vn
                    vt = [vn[t * DEC_BATCH:(t + 1) * DEC_BATCH] for t in range(DEC_SEQ)]
                    s_rows = []
                    for t in range(DEC_SEQ):
                        s = ws_ref[(g * DEC_SEQ + t) * DEC_SEQ] * vt[0]
                        for t2 in range(1, t + 1):
                            s = s + ws_ref[(g * DEC_SEQ + t) * DEC_SEQ + t2] * vt[t2]
                        s_rows.append(s + bs_ref[g * DEC_SEQ + t])
                    s = jnp.concatenate(s_rows, axis=0)
                else:
                    w = jnp.where(causal, ws_ref[g], 0.0).astype(bf16)
                    bias = bs_ref[:, g:g + 1]
                    vb = vn.astype(bf16)
                    s = jnp.concatenate(
                        [_dot(w, vb[c * A_CHUNK:(c + 1) * A_CHUNK]) + bias for c in range(tm // A_CHUNK)], axis=0)
                us_ref[:, ucols] = (u[:, ucols] * s).astype(bf16)
            y = _dot(us_ref[...], wout_ref[0].astype(bf16))
            if k == 0:
                o_ref[...] = x_ref[...] + y
            else:
                o_ref[...] += y


def mixer_a(x, g, w_in, norm_v, w_sp, b_sp, w_out, layer, *, tm, sample):
    m, d = x.shape
    nj = 2 * A_NBLK
    row = lambda w: pl.BlockSpec((tm, w), lambda i, j: (i, 0))
    full = lambda *shape: pl.BlockSpec(shape, lambda i, j: (0,) * len(shape))
    smem = pl.BlockSpec(memory_space=pltpu.SMEM)
    win_spec = pl.BlockSpec((1, d, A_BLK), lambda i, j: (layer, 0, (j + A_NBLK) % nj))
    wout_spec = pl.BlockSpec((1, A_BLK, d), lambda i, j: (layer, jnp.maximum(j - A_NBLK, 0), 0))
    if sample:
        in_specs = [smem, smem, row(d), full(1, d), win_spec, full(1, A_HALF), wout_spec]
        args = (w_sp, b_sp, x, g, w_in, norm_v, w_out)
        out_specs = [row(d), row(A_HALF)]
        out_shape = [jax.ShapeDtypeStruct((m, d), f32), jax.ShapeDtypeStruct((m, A_HALF), f32)]
    else:
        in_specs = [row(d), full(1, d), win_spec, full(1, A_HALF), full(A_GROUPS, A_CHUNK, A_CHUNK),
                    full(A_CHUNK, A_GROUPS), wout_spec]
        args = (x, g, w_in, norm_v, w_sp, b_sp, w_out)
        out_specs = row(d)
        out_shape = jax.ShapeDtypeStruct((m, d), f32)
    return pl.pallas_call(
        functools.partial(_mixer_a_kernel, sample=sample),
        grid=(m // tm, nj),
        in_specs=in_specs,
        out_specs=out_specs,
        out_shape=out_shape,
        scratch_shapes=[pltpu.VMEM((tm, d), bf16), pltpu.VMEM((tm, A_HALF), f32), pltpu.VMEM((tm, 1), f32),
                        pltpu.VMEM((tm, A_BLK), bf16)],
        compiler_params=_params("parallel", "arbitrary"),
        name="mixer_a_sample" if sample else "mixer_a_prompt",
    )(*args)


def _bucket_table():
    i = np.arange(B_BLOCK)[:, None]
    j = np.arange(2 * B_BLOCK)[None, :]
    n = np.maximum(B_BLOCK + i - j, 0)
    max_exact = N_BUCKETS // 2
    nf = np.maximum(n, 1).astype(np.float64)
    val = np.log(nf / max_exact) / math.log(MAX_DISTANCE / max_exact) * (N_BUCKETS - max_exact)
    in_window = (n >= max_exact) & (n < B_WINDOW)
    assert np.all(np.abs(val - np.round(val))[in_window & (n != max_exact)] > 1e-3)
    large = np.minimum(max_exact + np.floor(val + 1e-9).astype(np.int64), N_BUCKETS - 1)
    return np.where(n < max_exact, n, large).astype(np.int32)


def _bias_table_kernel(rb_ref, bk_ref, o_ref):
    bk = bk_ref[...]
    for h in range(B_HEADS):
        acc = jnp.zeros(bk.shape, f32)
        for b in range(N_BUCKETS):
            acc = jnp.where(bk == b, rb_ref[b * B_HEADS + h], acc)
        o_ref[h] = acc


def bias_table(rel_bias):
    return pl.pallas_call(
        _bias_table_kernel,
        in_specs=[pl.BlockSpec(memory_space=pltpu.SMEM), pl.BlockSpec(memory_space=pltpu.VMEM)],
        out_specs=pl.BlockSpec(memory_space=pltpu.VMEM),
        out_shape=jax.ShapeDtypeStruct((B_HEADS, B_BLOCK, 2 * B_BLOCK), f32),
        name="bias_table",
    )(rel_bias.reshape(-1), jnp.asarray(_bucket_table()))


def _softmax_with_sink(logits, sink):
    m = jnp.maximum(jnp.max(logits, axis=-1, keepdims=True), sink)
    p = jnp.exp(logits - m)
    return p, jnp.sum(p, axis=-1, keepdims=True) + jnp.exp(sink - m)


def _rms_head_pairs(x, g2, lo):
    sq = x * x
    s_lo = jnp.sum(jnp.where(lo, sq, 0.0), axis=-1, keepdims=True)
    s_hi = jnp.sum(jnp.where(lo, 0.0, sq), axis=-1, keepdims=True)
    r = lax.rsqrt(jnp.where(lo, s_lo, s_hi) * (1.0 / B_HEAD_DIM) + EPS)
    return x * r * g2


ATT_SUB = 4


def _attn_prompt_kernel(sink_ref, qkv_ref, tab_ref, qn_ref, kn_ref, o_ref, ko_ref, vo_ref,
                        kband_ref, vband_ref, q_ref, p_ref, rhs_ref):
    n = pl.program_id(1)
    T = B_BLOCK

    @pl.when(n == 0)
    def _():
        kband_ref[0:T, :] = jnp.zeros((T, B_KV_DIM), f32)
        vband_ref[0:T, :] = jnp.zeros((T, B_KV_DIM), f32)
        rhs_ref[...] = jnp.ones(rhs_ref.shape, bf16)

    for sub in range(ATT_SUB):
        rows = pl.ds(sub * T, T)
        first_key = jnp.where(n == 0, T, 0) if sub == 0 else 0
        _attn_block(first_key, sink_ref, qkv_ref.at[rows], tab_ref, qn_ref, kn_ref, o_ref.at[rows], ko_ref, vo_ref,
                    kband_ref, vband_ref, q_ref.at[sub], p_ref.at[sub], rhs_ref.at[sub])


def _attn_block(first_key, sink_ref, qkv_ref, tab_ref, qn_ref, kn_ref, o_ref, ko_ref, vo_ref,
                kband_ref, vband_ref, q_ref, p_ref, rhs_ref):
    T = B_BLOCK
    lo = lax.broadcasted_iota(jnp.int32, (1, LANES), 1) < B_HEAD_DIM
    qn2 = qn_ref[...]
    kn2 = kn_ref[...]
    for t in range(B_KV_DIM // LANES):
        lanes = slice(t * LANES, (t + 1) * LANES)
        k2 = _rms_head_pairs(qkv_ref[:, B_Q_DIM + t * LANES:B_Q_DIM + (t + 1) * LANES].astype(f32), kn2, lo)
        ko_ref[0, :, lanes] = k2
        kband_ref[T:, lanes] = k2
    v = qkv_ref[:, B_Q_DIM + B_KV_DIM:].astype(f32)
    vo_ref[0] = v
    vband_ref[T:, :] = v
    for t in range(B_Q_DIM // LANES):
        q2 = qkv_ref[:, t * LANES:(t + 1) * LANES].astype(f32)
        q_ref[t * T:(t + 1) * T, :] = _rms_head_pairs(q2, qn2, lo).astype(bf16)

    key_ops = {}
    band_row = lax.broadcasted_iota(jnp.int32, (2 * T, LANES), 0)
    for t in range(B_KV_DIM // LANES):
        lanes = slice(t * LANES, (t + 1) * LANES)
        kt = kband_ref[:, lanes]
        kr = pltpu.roll(kt, B_HEAD_DIM, axis=1)
        vt = jnp.where(band_row == 0, 0.0, vband_ref[:, lanes])
        vr = pltpu.roll(vt, B_HEAD_DIM, axis=1)
        hi = jnp.logical_not(lo)
        for half, (ksrc, vsrc) in enumerate(((kt, vt), (kr, vr))):
            g_lo, g_hi = (2 * t, 2 * t + 1) if half == 0 else (2 * t + 1, 2 * t)
            key_ops[(g_lo, 0)] = jnp.where(lo, ksrc, 0.0).astype(bf16)
            key_ops[(g_hi, 1)] = jnp.where(hi, ksrc, 0.0).astype(bf16)
            rhs_ref[g_lo * 2 + 0, :, 0:LANES] = jnp.where(lo, vsrc, 1.0).astype(bf16)
            rhs_ref[g_hi * 2 + 1, :, 0:LANES] = jnp.where(hi, vsrc, 1.0).astype(bf16)

    i = lax.broadcasted_iota(jnp.int32, (T, 2 * T), 0)
    j = lax.broadcasted_iota(jnp.int32, (T, 2 * T), 1)
    valid = (j > i) & (j <= i + B_WINDOW) & (j >= first_key)
    sink_col = j == 0
    lo_t = lax.broadcasted_iota(jnp.int32, (T, LANES), 1) < B_HEAD_DIM
    for g in range(B_KV_HEADS):
        q2 = q_ref[2 * g * T:(2 * g + 2) * T, :]
        res = []
        for half in range(2):
            logits = _dot_nt(q2, key_ops[(g, half)]) * (B_HEAD_DIM ** -0.5)
            for pair in range(2):
                h = g * B_REP + 2 * pair + half
                l = jnp.where(valid, logits[pair * T:(pair + 1) * T] + tab_ref[h], NEG_INF)
                l = jnp.where(sink_col, sink_ref[h], l)
                p = jnp.exp(l - jnp.max(l, axis=-1, keepdims=True))
                p_ref[half, pair * T:(pair + 1) * T, :] = p.astype(bf16)
            res.append(_dot(p_ref[half], rhs_ref[g * 2 + half]))
        for pair in range(2):
            rows = slice(pair * T, (pair + 1) * T)
            even = res[0][rows, 0:LANES] / res[0][rows, LANES:]
            odd = res[1][rows, 0:LANES] / res[1][rows, LANES:]
            t = 2 * g + pair
            o_ref[:, t * LANES:(t + 1) * LANES] = jnp.where(lo_t, even, odd).astype(bf16)
    kband_ref[0:T, :] = kband_ref[T:, :]
    vband_ref[0:T, :] = vband_ref[T:, :]


def attn_prompt(qkv, table, q_norm, k_norm, sinks):
    step = ATT_SUB * B_BLOCK
    nb = SEQ // step
    return pl.pallas_call(
        _attn_prompt_kernel,
        grid=(BATCH, nb),
        in_specs=[
            pl.BlockSpec(memory_space=pltpu.SMEM),
            pl.BlockSpec((step, B_Q_DIM + 2 * B_KV_DIM), lambda b, n: (b * nb + n, 0)),
            pl.BlockSpec((B_HEADS, B_BLOCK, 2 * B_BLOCK), lambda b, n: (0, 0, 0)),
            pl.BlockSpec((1, LANES), lambda b, n: (0, 0)),
            pl.BlockSpec((1, LANES), lambda b, n: (0, 0)),
        ],
        out_specs=[
            pl.BlockSpec((step, B_Q_DIM), lambda b, n: (b * nb + n, 0)),
            pl.BlockSpec((1, B_BLOCK, B_KV_DIM), lambda b, n: (b, 0, 0)),
            pl.BlockSpec((1, B_BLOCK, B_KV_DIM), lambda b, n: (b, 0, 0)),
        ],
        out_shape=[
            jax.ShapeDtypeStruct((PROMPT_ROWS, B_Q_DIM), bf16),
            jax.ShapeDtypeStruct((BATCH, B_BLOCK, B_KV_DIM), f32),
            jax.ShapeDtypeStruct((BATCH, B_BLOCK, B_KV_DIM), f32),
        ],
        scratch_shapes=[
            pltpu.VMEM((2 * B_BLOCK, B_KV_DIM), f32),
            pltpu.VMEM((2 * B_BLOCK, B_KV_DIM), f32),
            pltpu.VMEM((ATT_SUB, B_Q_DIM // LANES * B_BLOCK, LANES), bf16),
            pltpu.VMEM((ATT_SUB, 2, 2 * B_BLOCK, 2 * B_BLOCK), bf16),
            pltpu.VMEM((ATT_SUB, 2 * B_KV_HEADS, 2 * B_BLOCK, 2 * LANES), bf16),
        ],
        compiler_params=_params("parallel", "arbitrary"),
        name="attn_prompt",
    )(sinks, qkv, table, jnp.tile(q_norm, (1, LANES // B_HEAD_DIM)), jnp.tile(k_norm, (1, LANES // B_HEAD_DIM)))


ATT_BB = 8
ATT_QR = B_REP * DEC_SEQ
ATT_KEYS = B_WINDOW + 2 * DEC_SEQ


def _attn_sample_kernel(sink_ref, q_ref, kn_ref, vn_ref, kc_ref, vc_ref, tab_ref, qn_ref, knm_ref,
                        o_ref, ko_ref, vo_ref, kall_ref, vall_ref):
    qn2 = qn_ref[...]
    kn2 = knm_ref[...]
    n_rows = B_KV_HEADS * ATT_QR
    lo = lax.broadcasted_iota(jnp.int32, (1, LANES), 1) < B_HEAD_DIM
    row = lax.broadcasted_iota(jnp.int32, (n_rows, ATT_KEYS), 0)
    j = lax.broadcasted_iota(jnp.int32, (n_rows, ATT_KEYS), 1)
    t = row % DEC_SEQ
    valid = (j > t) & (j <= t + B_WINDOW)
    q_row_group = lax.broadcasted_iota(jnp.int32, (n_rows, B_KV_DIM), 0) // ATT_QR
    q_lane_group = lax.broadcasted_iota(jnp.int32, (n_rows, B_KV_DIM), 1) // B_HEAD_DIM
    own_group = q_row_group == q_lane_group
    o_lane_group = lax.broadcasted_iota(jnp.int32, (ATT_QR, B_KV_DIM), 1) // B_HEAD_DIM
    pad = jnp.zeros((ATT_KEYS - B_WINDOW - DEC_SEQ, B_KV_DIM), f32)
    bias = tab_ref[...]
    sink = sink_ref[...]
    for s in range(ATT_BB):
        kc = kc_ref[s]
        vc = vc_ref[s]
        k_new = jnp.concatenate(
            [_rms_head_pairs(kn_ref[s, :, tt * LANES:(tt + 1) * LANES], kn2, lo) for tt in range(B_KV_DIM // LANES)],
            axis=1)
        v_new = vn_ref[s]
        ko_ref[s, 0:B_WINDOW - DEC_SEQ, :] = kc[DEC_SEQ:, :]
        ko_ref[s, B_WINDOW - DEC_SEQ:, :] = k_new
        vo_ref[s, 0:B_WINDOW - DEC_SEQ, :] = vc[DEC_SEQ:, :]
        vo_ref[s, B_WINDOW - DEC_SEQ:, :] = v_new
        kall_ref[s, 0:B_WINDOW, :] = kc
        kall_ref[s, B_WINDOW:B_WINDOW + DEC_SEQ, :] = k_new
        kall_ref[s, B_WINDOW + DEC_SEQ:, :] = pad
        vall_ref[s, 0:B_WINDOW, :] = vc
        vall_ref[s, B_WINDOW:B_WINDOW + DEC_SEQ, :] = v_new
        vall_ref[s, B_WINDOW + DEC_SEQ:, :] = pad
        q = q_ref[s]
        qn = jnp.concatenate(
            [_rms_head_pairs(q[:, tt * LANES:(tt + 1) * LANES], qn2, lo) for tt in range(B_KV_DIM // LANES)], axis=1)
        q_all = jnp.where(own_group, jnp.concatenate([qn] * B_KV_HEADS, axis=0), 0.0).astype(bf16)
        logits = _dot_nt(q_all, kall_ref[s].astype(bf16)) * (B_HEAD_DIM ** -0.5)
        logits = jnp.where(valid, logits + bias, NEG_INF)
        p, denom = _softmax_with_sink(logits, sink)
        res = _dot(p.astype(bf16), vall_ref[s].astype(bf16)) / denom
        out = jnp.zeros((ATT_QR, B_KV_DIM), f32)
        for g in range(B_KV_HEADS):
            out = jnp.where(o_lane_group == g, res[g * ATT_QR:(g + 1) * ATT_QR, :], out)
        o_ref[s] = out


def attn_sample(q_s, k_new, v_new, k_cache, v_cache, table_s, q_norm, k_norm, sinks):
    blk = lambda *shape: pl.BlockSpec((ATT_BB,) + shape, lambda i: (i,) + (0,) * len(shape))
    full = lambda *shape: pl.BlockSpec(shape, lambda i: (0,) * len(shape))
    return pl.pallas_call(
        _attn_sample_kernel,
        grid=(DEC_BATCH // ATT_BB,),
        in_specs=[
            full(B_KV_HEADS * ATT_QR, 1),
            blk(ATT_QR, B_KV_DIM), blk(DEC_SEQ, B_KV_DIM), blk(DEC_SEQ, B_KV_DIM),
            blk(B_WINDOW, B_KV_DIM), blk(B_WINDOW, B_KV_DIM),
            full(B_KV_HEADS * ATT_QR, ATT_KEYS), full(1, LANES), full(1, LANES),
        ],
        out_specs=[blk(ATT_QR, B_KV_DIM), blk(B_WINDOW, B_KV_DIM), blk(B_WINDOW, B_KV_DIM)],
        out_shape=[
            jax.ShapeDtypeStruct((DEC_BATCH, ATT_QR, B_KV_DIM), f32),
            jax.ShapeDtypeStruct((DEC_BATCH, B_WINDOW, B_KV_DIM), f32),
            jax.ShapeDtypeStruct((DEC_BATCH, B_WINDOW, B_KV_DIM), f32),
        ],
        scratch_shapes=[pltpu.VMEM((ATT_BB, ATT_KEYS, B_KV_DIM), f32), pltpu.VMEM((ATT_BB, ATT_KEYS, B_KV_DIM), f32)],
        compiler_params=_params("parallel"),
        name="attn_sample",
    )(jnp.repeat(sinks, DEC_SEQ).reshape(B_KV_HEADS * ATT_QR, 1), q_s, k_new, v_new, k_cache, v_cache,
      table_s.reshape(B_KV_HEADS * ATT_QR, ATT_KEYS),
      jnp.tile(q_norm, (1, LANES // B_HEAD_DIM)), jnp.tile(k_norm, (1, LANES // B_HEAD_DIM)))


CONV_PAD = SUBLANES

def _gated_group_norm(y, z, norm_w):
    gt = y * _silu(z)
    parts = []
    for g in range(C_GROUPS):
        gg = gt[:, g * C_GROUP_W:(g + 1) * C_GROUP_W]
        parts.append(gg * lax.rsqrt(jnp.mean(gg * gg, axis=-1, keepdims=True) + EPS))
    return jnp.concatenate(parts, axis=1) * norm_w


LOG2E = math.log2(math.e)


def _expand_heads(v, sel3):
    lane = lax.broadcasted_iota(jnp.int32, (1, LANES), 1)
    v = jnp.where(lane < C_HEADS, v, 0.0)
    hi = v.astype(bf16).astype(f32)
    r1 = v - hi
    mid = r1.astype(bf16).astype(f32)
    lo = r1 - mid
    packed = hi + pltpu.roll(mid, C_HEADS, axis=1) + pltpu.roll(lo, 2 * C_HEADS, axis=1)
    return _dot(packed.astype(bf16), sel3)


SSD_SUB = 4


def _ssd_prompt_kernel(zx_ref, dtr_ref, cw_ref, cb_ref, dtb_ref, alog_ref, dsk_ref, nw_ref, sel_ref,
                       yn_ref, hfin_ref, cout_ref, xpad_ref, ht_ref, y_ref):
    c = pl.program_id(1)

    @pl.when(c == 0)
    def _():
        xpad_ref[0, 0:CONV_PAD, :] = jnp.zeros((CONV_PAD, C_CONV_DIM), f32)
        ht_ref[...] = jnp.zeros(ht_ref.shape, f32)

    for sub in range(SSD_SUB):
        rows = pl.ds(sub * C_CHUNK, C_CHUNK)
        _ssd_chunk(zx_ref.at[rows], dtr_ref.at[rows], cw_ref, cb_ref, dtb_ref, alog_ref, dsk_ref, nw_ref, sel_ref,
                   yn_ref.at[rows], cout_ref, xpad_ref.at[sub], xpad_ref.at[(sub + 1) % SSD_SUB], ht_ref,
                   y_ref.at[sub])

    @pl.when(c == pl.num_programs(1) - 1)
    def _():
        for t in range(C_D_INNER // LANES):
            hfin_ref[0, t * LANES:(t + 1) * LANES, :] = ht_ref[:, t * LANES:(t + 1) * LANES].T


def _ssd_chunk(zx_ref, dtr_ref, cw_ref, cb_ref, dtb_ref, alog_ref, dsk_ref, nw_ref, sel_ref,
               yn_ref, cout_ref, xpad_ref, xpad_next_ref, ht_ref, y_ref):
    T = C_CHUNK
    xbc = zx_ref[:, C_D_INNER:].astype(f32)
    xpad_ref[CONV_PAD:, :] = xbc
    xp = xpad_ref[...]
    cw = cw_ref[...]
    acc = cb_ref[...]
    for tap in range(C_D_CONV - 1):
        shifted = pltpu.roll(xp, C_D_CONV - 1 - tap, axis=0)[CONV_PAD:, :]
        acc = acc + shifted * cw[tap:tap + 1, :]
    acc = acc + xbc * cw[C_D_CONV - 1:C_D_CONV, :]
    xpad_next_ref[0:CONV_PAD, :] = xbc[T - CONV_PAD:, :]
    cout_ref[0] = xbc[T - (C_D_CONV - 1):, :]
    act = _silu(acc)
    xs = act[:, :C_D_INNER]
    bm = act[:, C_D_INNER:C_D_INNER + C_BC_DIM]
    cm = act[:, C_D_INNER + C_BC_DIM:]
    xb = xs.astype(bf16)

    dt = _softplus(dtr_ref[...] + dtb_ref[...])
    a_neg = -jnp.exp(alog_ref[...])
    row = lax.broadcasted_iota(jnp.int32, (T, T), 0)
    col = lax.broadcasted_iota(jnp.int32, (T, T), 1)
    causal = row >= col
    acs = _dot_exact_lhs01(causal.astype(f32), dt * a_neg)
    a2 = acs * LOG2E
    sel3 = sel_ref[...]
    e_exp = jnp.exp2(_expand_heads(a2, sel3))
    w_exp = _expand_heads(jnp.exp(acs[T - 1:T, :] - acs) * dt, sel3)
    cdec = e_exp[T - 1:T, :]
    b2_t = a2.T - jnp.log2(dt.T)
    xw = (xs * w_exp).astype(bf16)
    hb = ht_ref[...].astype(bf16)
    dsk = dsk_ref[...]
    lo_t = lax.broadcasted_iota(jnp.int32, (T, LANES), 1) < C_HEAD_DIM

    for g in range(C_GROUPS):
        ns = slice(g * C_D_STATE, (g + 1) * C_D_STATE)
        gs = slice(g * C_GROUP_W, (g + 1) * C_GROUP_W)
        b_g = bm[:, ns]
        c_g = cm[:, ns].astype(bf16)
        cb = _dot_nt(c_g, b_g.astype(bf16))
        yi = _dot(c_g, hb[:, gs])
        ht_ref[:, gs] = ht_ref[:, gs] * cdec[:, gs] + _dot(b_g.T.astype(bf16), xw[:, gs])
        for tt in range(C_GROUP_W // LANES):
            t = g * (C_GROUP_W // LANES) + tt
            lanes = slice(t * LANES, (t + 1) * LANES)
            xt = xb[:, lanes]
            res = []
            for half in range(2):
                h = 2 * t + half
                a_col = jnp.broadcast_to(a2[:, h:h + 1], (T, T))
                b_row = jnp.broadcast_to(b2_t[h:h + 1, :], (T, T))
                w = jnp.where(causal, cb * jnp.exp2(a_col - b_row), 0.0)
                res.append(_dot(w.astype(bf16), xt))
            y_intra = jnp.where(lo_t, res[0], res[1])
            y_ref[:, lanes] = y_intra + e_exp[:, lanes] * yi[:, tt * LANES:(tt + 1) * LANES] + dsk[:, lanes] * xs[:, lanes]

    yn_ref[...] = _gated_group_norm(y_ref[...], zx_ref[:, :C_D_INNER].astype(f32), nw_ref[...]).astype(bf16)


def _head_select3():
    k = np.arange(LANES)[:, None]
    ch = np.arange(C_D_INNER)[None, :] // C_HEAD_DIM
    return jnp.asarray((k % C_HEADS == ch) & (k < 3 * C_HEADS), dtype=bf16)


def ssd_prompt(zx, dtr, conv_w, conv_b, dt_bias, a_log, d_skip, norm_w):
    step = SSD_SUB * C_CHUNK
    nc = SEQ // step
    full = lambda *shape: pl.BlockSpec(shape, lambda b, c: (0,) * len(shape))
    return pl.pallas_call(
        _ssd_prompt_kernel,
        grid=(BATCH, nc),
        in_specs=[
            pl.BlockSpec((step, C_D_INNER + C_CONV_DIM), lambda b, c: (b * nc + c, 0)),
            pl.BlockSpec((step, LANES), lambda b, c: (b * nc + c, 0)),
            full(C_D_CONV, C_CONV_DIM), full(1, C_CONV_DIM), full(1, LANES), full(1, LANES),
            full(1, C_D_INNER), full(1, C_D_INNER), full(LANES, C_D_INNER),
        ],
        out_specs=[
            pl.BlockSpec((step, C_D_INNER), lambda b, c: (b * nc + c, 0)),
            pl.BlockSpec((1, C_D_INNER, C_D_STATE), lambda b, c: (b, 0, 0)),
            pl.BlockSpec((1, C_D_CONV - 1, C_CONV_DIM), lambda b, c: (b, 0, 0)),
        ],
        out_shape=[
            jax.ShapeDtypeStruct((PROMPT_ROWS, C_D_INNER), bf16),
            jax.ShapeDtypeStruct((BATCH, C_D_INNER, C_D_STATE), f32),
            jax.ShapeDtypeStruct((BATCH, C_D_CONV - 1, C_CONV_DIM), f32),
        ],
        scratch_shapes=[
            pltpu.VMEM((SSD_SUB, CONV_PAD + C_CHUNK, C_CONV_DIM), f32),
            pltpu.VMEM((C_D_STATE, C_D_INNER), f32),
            pltpu.VMEM((SSD_SUB, C_CHUNK, C_D_INNER), f32),
        ],
        compiler_params=_params("parallel", "arbitrary"),
        name="ssd_prompt",
    )(zx, dtr, conv_w, conv_b, dt_bias, a_log, d_skip, norm_w, _head_select3())


SSD_BB = 8
SSD_TP = SUBLANES
_N_PAIRS = DEC_SEQ * (DEC_SEQ + 1) // 2
_N_COEF = _N_PAIRS + 2 * DEC_SEQ


def _ssd_sample_kernel(zx_ref, dtr_ref, cs_ref, h0_ref, cw_ref, cb_ref, dtb_ref, alog_ref, dsk_ref, nw_ref,
                       sel_ref, yn_ref, hn_ref, cout_ref, c_scr, b_scr, xw_scr, yi_scr, cd_scr):
    L = DEC_SEQ
    cw = cw_ref[...]
    xp = [cs_ref[k] for k in range(C_D_CONV - 1)] + [zx_ref[t, :, C_D_INNER:] for t in range(L)]
    for k in range(C_D_CONV - 1):
        cout_ref[k] = xp[L + k]
    act = []
    for t in range(L):
        acc = cb_ref[...]
        for tap in range(C_D_CONV):
            acc = acc + xp[t + tap] * cw[tap:tap + 1, :]
        act.append(_silu(acc))
    xs = [a[:, :C_D_INNER] for a in act]
    bm = [a[:, C_D_INNER:C_D_INNER + C_BC_DIM] for a in act]
    cm = [a[:, C_D_INNER + C_BC_DIM:] for a in act]

    a_neg = -jnp.exp(alog_ref[...])
    dt = [_softplus(dtr_ref[t] + dtb_ref[...]) for t in range(L)]
    acs = []
    for t in range(L):
        acs.append(dt[t] * a_neg if t == 0 else acs[t - 1] + dt[t] * a_neg)

    lane_group = lax.broadcasted_iota(jnp.int32, (SSD_BB, LANES), 1) // C_REP
    coefs = []
    for t in range(L):
        for t2 in range(t + 1):
            cbh = jnp.zeros((SSD_BB, LANES), f32)
            for g in range(C_GROUPS):
                ns = slice(g * C_D_STATE, (g + 1) * C_D_STATE)
                cbg = jnp.sum(cm[t][:, ns] * bm[t2][:, ns], axis=-1, keepdims=True)
                cbh = jnp.where(lane_group == g, cbg, cbh)
            coefs.append(cbh * jnp.exp(acs[t] - acs[t2]) * dt[t2])
    for t in range(L):
        coefs.append(jnp.exp(acs[t]))
    for t in range(L):
        coefs.append(jnp.exp(acs[L - 1] - acs[t]) * dt[t])
    coef = jnp.concatenate(coefs, axis=0)
    cexp = _dot_exact_rhs01(coef, sel_ref[...])
    cexp = [cexp[k * SSD_BB:(k + 1) * SSD_BB, :] for k in range(_N_COEF)]
    w_intra = cexp[:_N_PAIRS]
    w_inter = cexp[_N_PAIRS:_N_PAIRS + L]
    w_state = cexp[_N_PAIRS + L:]

    cd = jnp.concatenate([jnp.exp(acs[L - 1]), jnp.zeros((LANES - SSD_BB, LANES), f32)], axis=0)
    cd_t = cd.T
    for s in range(SSD_BB):
        cd_scr[s] = jnp.broadcast_to(cd_t[0:C_HEADS, s:s + 1], (C_HEADS, C_D_STATE))

    zeros_tail = jnp.zeros((SSD_BB, SSD_TP - L, C_D_INNER), f32)
    c_scr[:, L:, :] = zeros_tail[:, :, :C_BC_DIM]
    b_scr[:, L:, :] = zeros_tail[:, :, :C_BC_DIM]
    xw_scr[:, L:, :] = zeros_tail
    for t in range(L):
        xw_t = xs[t] * w_state[t]
        for s in range(SSD_BB):
            c_scr[s, t:t + 1, :] = cm[t][s:s + 1, :]
            b_scr[s, t:t + 1, :] = bm[t][s:s + 1, :]
            xw_scr[s, t:t + 1, :] = xw_t[s:s + 1, :]

    for s in range(SSD_BB):
        for g in range(C_GROUPS):
            ns = slice(g * C_D_STATE, (g + 1) * C_D_STATE)
            gs = slice(g * C_GROUP_W, (g + 1) * C_GROUP_W)
            h0 = h0_ref[s, gs, :]
            yi = _dot_nt(c_scr[s, :, ns].astype(bf16), h0.astype(bf16))
            for t in range(L):
                yi_scr[t, s:s + 1, gs] = yi[t:t + 1, :]
            st = _dot_tn(xw_scr[s, :, gs].astype(bf16), b_scr[s, :, ns].astype(bf16))
            for r in range(C_REP):
                h = g * C_REP + r
                rs = slice(r * C_HEAD_DIM, (r + 1) * C_HEAD_DIM)
                scale = cd_scr[s, h:h + 1, :]
                hn_ref[s, h * C_HEAD_DIM:(h + 1) * C_HEAD_DIM, :] = h0[rs, :] * scale + st[rs, :]

    dsk = dsk_ref[...]
    nw = nw_ref[...]
    pair = 0
    for t in range(L):
        y = w_inter[t] * yi_scr[t] + dsk * xs[t]
        for t2 in range(t + 1):
            y = y + w_intra[pair] * xs[t2]
            pair += 1
        yn_ref[t] = _gated_group_norm(y, zx_ref[t, :, :C_D_INNER], nw).astype(bf16)


def ssd_sample(zx_t, dtr_t, conv_state_t, h0, conv_w, conv_b, dt_bias, a_log, d_skip, norm_w, sel):
    tmaj = lambda n, w: pl.BlockSpec((n, SSD_BB, w), lambda i: (0, i, 0))
    full = lambda *shape: pl.BlockSpec(shape, lambda i: (0,) * len(shape))
    return pl.pallas_call(
        _ssd_sample_kernel,
        grid=(DEC_BATCH // SSD_BB,),
        in_specs=[
            tmaj(DEC_SEQ, C_D_INNER + C_CONV_DIM), tmaj(DEC_SEQ, LANES), tmaj(C_D_CONV - 1, C_CONV_DIM),
            pl.BlockSpec((SSD_BB, C_D_INNER, C_D_STATE), lambda i: (i, 0, 0)),
            full(C_D_CONV, C_CONV_DIM), full(1, C_CONV_DIM), full(1, LANES), full(1, LANES),
            full(1, C_D_INNER), full(1, C_D_INNER), full(LANES, C_D_INNER),
        ],
        out_specs=[
            tmaj(DEC_SEQ, C_D_INNER),
            pl.BlockSpec((SSD_BB, C_D_INNER, C_D_STATE), lambda i: (i, 0, 0)),
            tmaj(C_D_CONV - 1, C_CONV_DIM),
        ],
        out_shape=[
            jax.ShapeDtypeStruct((DEC_SEQ, DEC_BATCH, C_D_INNER), bf16),
            jax.ShapeDtypeStruct((DEC_BATCH, C_D_INNER, C_D_STATE), f32),
            jax.ShapeDtypeStruct((C_D_CONV - 1, DEC_BATCH, C_CONV_DIM), f32),
        ],
        scratch_shapes=[
            pltpu.VMEM((SSD_BB, SSD_TP, C_BC_DIM), f32),
            pltpu.VMEM((SSD_BB, SSD_TP, C_BC_DIM), f32),
            pltpu.VMEM((SSD_BB, SSD_TP, C_D_INNER), f32),
            pltpu.VMEM((DEC_SEQ, SSD_BB, C_D_INNER), f32),
            pltpu.VMEM((SSD_BB, C_HEADS, C_D_STATE), f32),
        ],
        compiler_params=_params("parallel"),
        name="ssd_sample",
    )(zx_t, dtr_t, conv_state_t, h0, conv_w, conv_b, dt_bias, a_log, d_skip, norm_w, sel)


def _pad_lanes(v):
    return jnp.pad(v.astype(f32), (0, LANES - v.shape[0])).reshape(1, LANES)


def _mixer_a(xp, xs, g, j, w_in, norm_v, w_sp, b_sp, w_out):
    nv = norm_v.reshape(1, A_HALF)
    xp = mixer_a(xp, g, w_in, nv, w_sp, b_sp.T, w_out, j, tm=TM_PROMPT, sample=False)
    xs, v_s = mixer_a(xs, g, w_in, nv, w_sp[:, :DEC_SEQ, :DEC_SEQ].reshape(-1), b_sp[:, :DEC_SEQ].reshape(-1),
                      w_out, j, tm=TM_SAMPLE, sample=True)
    return xp, xs, jnp.swapaxes(v_s.reshape(DEC_SEQ, DEC_BATCH, A_HALF), 0, 1)


def _mixer_b(xp, xs, g, j, k_cache, v_cache, w_qkv, q_norm, k_norm, sinks, rel_bias):
    qn = q_norm.reshape(1, B_HEAD_DIM)
    kn = k_norm.reshape(1, B_HEAD_DIM)
    table = bias_table(rel_bias)
    n_qkv = B_Q_DIM + 2 * B_KV_DIM

    qkv_p = norm_matmul(xp, g, w_qkv, j, n_qkv, tm=2 * TM_PROMPT, tn=n_qkv, out_dtype=bf16)
    o_p, k_p, v_p = attn_prompt(qkv_p, table, qn, kn, sinks)

    qkv_s = norm_matmul(xs, g, w_qkv, j, n_qkv, tm=TM_SAMPLE, tn=512).reshape(DEC_SEQ, DEC_BATCH, -1)
    q_s = qkv_s[:, :, :B_Q_DIM].reshape(DEC_SEQ, DEC_BATCH, B_KV_HEADS, B_REP, B_HEAD_DIM)
    q_s = q_s.transpose(1, 3, 0, 2, 4).reshape(DEC_BATCH, ATT_QR, B_KV_DIM)
    k_new = jnp.swapaxes(qkv_s[:, :, B_Q_DIM:B_Q_DIM + B_KV_DIM], 0, 1)
    v_new = jnp.swapaxes(qkv_s[:, :, B_Q_DIM + B_KV_DIM:], 0, 1)
    table_s = table[:, :DEC_SEQ, :ATT_KEYS].reshape(B_KV_HEADS, ATT_QR, ATT_KEYS)
    o_s, k_s, v_s = attn_sample(q_s, k_new, v_new,
                                k_cache.reshape(DEC_BATCH, B_WINDOW, B_KV_DIM),
                                v_cache.reshape(DEC_BATCH, B_WINDOW, B_KV_DIM),
                                table_s, qn, kn, sinks)
    o_s = o_s.reshape(DEC_BATCH, B_REP, DEC_SEQ, B_KV_HEADS, B_HEAD_DIM).transpose(2, 0, 3, 1, 4)
    o_s = o_s.reshape(SAMPLE_ROWS, B_Q_DIM).astype(bf16)
    kv_shape_p = (BATCH, B_WINDOW, B_KV_HEADS, B_HEAD_DIM)
    kv_shape_s = (DEC_BATCH, B_WINDOW, B_KV_HEADS, B_HEAD_DIM)
    return o_p, o_s, k_p.reshape(kv_shape_p), v_p.reshape(kv_shape_p), k_s.reshape(kv_shape_s), v_s.reshape(kv_shape_s)


def _mixer_c(xp, xs, g, j, h0, conv_state, w_in, conv_w, conv_b, dt_bias, a_log, d_skip, norm_w):
    n_zx = C_D_INNER + C_CONV_DIM
    w_dt = jnp.pad(w_in[j, :, n_zx:], ((0, 0), (0, LANES - C_HEADS)))
    cb = conv_b.reshape(1, C_CONV_DIM)
    dtb = _pad_lanes(dt_bias)
    alog = _pad_lanes(a_log)
    dsk = jnp.repeat(d_skip.astype(f32), C_HEAD_DIM).reshape(1, C_D_INNER)
    nw = norm_w.reshape(1, C_D_INNER)

    zx_p, dtr_p = norm_matmul(xp, g, w_in, j, n_zx, tm=2 * TM_PROMPT, tn=1024, w_tail=w_dt, out_dtype=bf16)
    yn_p, h_p, conv_p = ssd_prompt(zx_p, dtr_p, conv_w, cb, dtb, alog, dsk, nw)

    zx_s, dtr_s = norm_matmul(xs, g, w_in, j, n_zx, tm=TM_SAMPLE, tn=1024, w_tail=w_dt)
    sel = (jnp.arange(LANES)[:, None] == jnp.arange(C_D_INNER)[None, :] // C_HEAD_DIM).astype(f32)
    yn_s, h_s, conv_s = ssd_sample(
        zx_s.reshape(DEC_SEQ, DEC_BATCH, -1), dtr_s.reshape(DEC_SEQ, DEC_BATCH, LANES),
        jnp.swapaxes(conv_state, 0, 1), h0.reshape(DEC_BATCH, C_D_INNER, C_D_STATE),
        conv_w, cb, dtb, alog, dsk, nw, sel)
    st_shape = (C_HEADS, C_HEAD_DIM, C_D_STATE)
    return (yn_p, yn_s.reshape(SAMPLE_ROWS, C_D_INNER), h_p.reshape((BATCH,) + st_shape), conv_p,
            h_s.reshape((DEC_BATCH,) + st_shape), jnp.swapaxes(conv_s, 0, 1))


def kernel(x_prompt, x_sample, cache_swa_k, cache_swa_v, state_ssm, state_conv, norm_mixer, norm_mlp, mlp_w_up, mlp_w_down, a_w_in, a_norm_v, a_w_spatial, a_b_spatial, a_w_out, b_w_qkv, b_q_norm, b_k_norm, b_sinks, rel_bias, b_w_out, c_w_in, c_conv_w, c_conv_b, c_dt_bias, c_a_log, c_d, c_norm, c_w_out):
    xp = x_prompt.reshape(PROMPT_ROWS, D_MODEL)
    xs = jnp.swapaxes(x_sample, 0, 1).reshape(SAMPLE_ROWS, D_MODEL)
    chunk_v_s = []
    swa_kp, swa_vp, swa_ks, swa_vs = [], [], [], []
    ssm_p, conv_p, ssm_s, conv_s = [], [], [], []
    a_w_in, a_w_out, c_w_in = a_w_in.astype(bf16), a_w_out.astype(bf16), c_w_in.astype(bf16)
    b_w_out, c_w_out = b_w_out.astype(bf16), c_w_out.astype(bf16)
    for i in range(DEPTH):
        kind = i % N_MIXERS
        j = i // N_MIXERS
        g = norm_mixer[i].reshape(1, D_MODEL)
        proj_p = proj_s = None
        if kind == 0:
            xp, xs, v_new = _mixer_a(xp, xs, g, j, a_w_in, a_norm_v[j], a_w_spatial[j], a_b_spatial[j], a_w_out)
            chunk_v_s.append(v_new)
        elif kind == 1:
            o_p, o_s, kp, vp, ks_, vs_ = _mixer_b(xp, xs, g, j, cache_swa_k[j], cache_swa_v[j], b_w_qkv, b_q_norm[j],
                                                  b_k_norm[j], b_sinks[j], rel_bias)
            proj_p, proj_s = (o_p, b_w_out, j), (o_s, b_w_out, j)
            swa_kp.append(kp); swa_vp.append(vp); swa_ks.append(ks_); swa_vs.append(vs_)
        else:
            y_p, y_s, hp, bp, hs, bs = _mixer_c(xp, xs, g, j, state_ssm[j], state_conv[j], c_w_in, c_conv_w[j],
                                                c_conv_b[j], c_dt_bias[j], c_a_log[j], c_d[j], c_norm[j])
            proj_p, proj_s = (y_p, c_w_out, j), (y_s, c_w_out, j)
            ssm_p.append(hp); conv_p.append(bp); ssm_s.append(hs); conv_s.append(bs)
        gm = norm_mlp[i].reshape(1, D_MODEL)
        xp = mlp(xp, gm, mlp_w_up, mlp_w_down, i, tm=TM_PROMPT, tf=MLP_TF, proj=proj_p)
        xs = mlp(xs, gm, mlp_w_up, mlp_w_down, i, tm=TM_SAMPLE, tf=MLP_TF, proj=proj_s)
    y_prompt = xp.reshape(BATCH, SEQ, D_MODEL)
    y_sample = jnp.swapaxes(xs.reshape(DEC_SEQ, DEC_BATCH, D_MODEL), 0, 1)
    return (y_prompt, y_sample, jnp.stack(chunk_v_s),
            jnp.stack(swa_kp), jnp.stack(swa_vp), jnp.stack(swa_ks), jnp.stack(swa_vs),
            jnp.stack(ssm_p), jnp.stack(conv_p), jnp.stack(ssm_s), jnp.stack(conv_s))
```

```python
import functools
import math

import jax
import jax.numpy as jnp
import numpy as np
from jax import lax
from jax.experimental import pallas as pl
from jax.experimental.pallas import tpu as pltpu

f32 = jnp.float32
bf16 = jnp.bfloat16

D_MODEL = 1024
BATCH = 4
SEQ = 4096
DEPTH = 4
DEC_BATCH = 128
DEC_SEQ = 4
PAST_LEN = 8192
N_MIXERS = 3
D_FF = 4 * D_MODEL
EPS = 1e-6
NEG_INF = -1e30

A_CHUNK = 128
A_D_FFN = 6 * D_MODEL
A_HALF = A_D_FFN // 2
A_GROUPS = 8
A_GROUP_W = A_HALF // A_GROUPS

B_HEADS = 16
B_KV_HEADS = 4
B_HEAD_DIM = 64
B_REP = B_HEADS // B_KV_HEADS
B_WINDOW = 128
B_BLOCK = 128
B_Q_DIM = B_HEADS * B_HEAD_DIM
B_KV_DIM = B_KV_HEADS * B_HEAD_DIM
N_BUCKETS = 32
MAX_DISTANCE = 128

C_D_INNER = 2 * D_MODEL
C_HEAD_DIM = 64
C_HEADS = C_D_INNER // C_HEAD_DIM
C_GROUPS = 4
C_REP = C_HEADS // C_GROUPS
C_D_STATE = 128
C_D_CONV = 4
C_BC_DIM = C_GROUPS * C_D_STATE
C_CONV_DIM = C_D_INNER + 2 * C_BC_DIM
C_GROUP_W = C_D_INNER // C_GROUPS
C_CHUNK = 128

LANES = 128
SUBLANES = 8
VMEM_LIMIT_BYTES = 56 * 1024 * 1024

PROMPT_ROWS = BATCH * SEQ
SAMPLE_ROWS = DEC_BATCH * DEC_SEQ
TM_PROMPT = 1024
TM_SAMPLE = SAMPLE_ROWS
MLP_TF = 1024


def _params(*sem):
    return pltpu.CompilerParams(dimension_semantics=sem, vmem_limit_bytes=VMEM_LIMIT_BYTES)


def _rms(x, g):
    ms = jnp.mean(x * x, axis=-1, keepdims=True)
    return x * lax.rsqrt(ms + EPS) * g


def _gelu(x):
    return 0.5 * x * (1.0 + lax.erf(x * math.sqrt(0.5)))


def _silu(x):
    return x * jax.nn.sigmoid(x)


def _softplus(x):
    return jnp.maximum(x, 0.0) + jnp.log1p(jnp.exp(-jnp.abs(x)))


def _dot(a, b):
    return jnp.dot(a, b, preferred_element_type=f32)


def _dot_nt(a, b):
    return lax.dot_general(a, b, (((1,), (1,)), ((), ())), preferred_element_type=f32)


def _dot_tn(a, b):
    return lax.dot_general(a, b, (((0,), (0,)), ((), ())), preferred_element_type=f32)


def _dot_exact_lhs01(a01, x):
    a = a01.astype(bf16)
    hi = x.astype(bf16)
    r1 = x - hi.astype(f32)
    mid = r1.astype(bf16)
    lo = (r1 - mid.astype(f32)).astype(bf16)
    return _dot(a, hi) + _dot(a, mid) + _dot(a, lo)


def _dot_exact_rhs01(x, b01):
    b = b01.astype(bf16)
    hi = x.astype(bf16)
    r1 = x - hi.astype(f32)
    mid = r1.astype(bf16)
    lo = (r1 - mid.astype(f32)).astype(bf16)
    return _dot(hi, b) + _dot(mid, b) + _dot(lo, b)


def _norm_matmul_kernel(*refs, nj, tail):
    if tail:
        x_ref, g_ref, w_ref, wt_ref, o_ref, ot_ref, xn_ref = refs
    else:
        x_ref, g_ref, w_ref, o_ref, xn_ref = refs
    j = pl.program_id(1)

    @pl.when(j == 0)
    def _():
        xn_ref[...] = _rms(x_ref[...], g_ref[...]).astype(bf16)

    @pl.when(j < nj)
    def _():
        o_ref[...] = _dot(xn_ref[...], w_ref[0].astype(bf16)).astype(o_ref.dtype)

    if tail:
        @pl.when(j == nj)
        def _():
            ot_ref[...] = _dot(xn_ref[...], wt_ref[...].astype(bf16))


def norm_matmul(x, g, w, layer, n, *, tm, tn, w_tail=None, out_dtype=f32):
    m, k = x.shape
    nj = n // tn
    tail = w_tail is not None
    last = nj - 1
    in_specs = [
        pl.BlockSpec((tm, k), lambda i, j: (i, 0)),
        pl.BlockSpec((1, k), lambda i, j: (0, 0)),
        pl.BlockSpec((1, k, tn), lambda i, j: (layer, 0, jnp.minimum(j, last))),
    ]
    out_specs = [pl.BlockSpec((tm, tn), lambda i, j: (i, jnp.minimum(j, last)))]
    out_shape = [jax.ShapeDtypeStruct((m, n), out_dtype)]
    args = [x, g, w]
    if tail:
        in_specs.append(pl.BlockSpec((k, LANES), lambda i, j: (0, 0)))
        out_specs.append(pl.BlockSpec((tm, LANES), lambda i, j: (i, 0)))
        out_shape.append(jax.ShapeDtypeStruct((m, LANES), f32))
        args.append(w_tail)
    out = pl.pallas_call(
        functools.partial(_norm_matmul_kernel, nj=nj, tail=tail),
        grid=(m // tm, nj + (1 if tail else 0)),
        in_specs=in_specs,
        out_specs=out_specs,
        out_shape=out_shape,
        scratch_shapes=[pltpu.VMEM((tm, k), bf16)],
        compiler_params=_params("parallel", "arbitrary"),
        name="norm_matmul",
    )(*args)
    return out if tail else out[0]


def _mlp_kernel(*refs, proj, emit):
    refs = list(refs)
    a_ref, wo_ref = (refs.pop(0), refs.pop(0)) if proj else (None, None)
    x_ref, g_ref, wu_ref, wd_ref, o_ref = refs[:5]
    wub_ref, wdb_ref = (refs[5], refs[6]) if emit else (None, None)
    xn_ref = refs[-1]

    @pl.when(pl.program_id(1) == 0)
    def _():
        x = x_ref[...]
        if proj:
            x = x + _dot(a_ref[...], wo_ref[0])
        xn_ref[...] = _rms(x, g_ref[...]).astype(bf16)
        o_ref[...] = x

    wu = wu_ref[0].astype(bf16)
    wd = wd_ref[0].astype(bf16)
    if emit:
        wub_ref[0] = wu
        wdb_ref[0] = wd
    h = jnp.maximum(_dot(xn_ref[...], wu), 0.0)
    o_ref[...] += _dot((h * h).astype(bf16), wd)


def mlp(x, g, w_up, w_down, layer, *, tm, tf, proj=None, emit=False):
    m, d = x.shape
    ff = w_up.shape[2]
    assert not emit or m == tm
    in_specs = [
        pl.BlockSpec((tm, d), lambda i, j: (i, 0)),
        pl.BlockSpec((1, d), lambda i, j: (0, 0)),
        pl.BlockSpec((1, d, tf), lambda i, j: (layer, 0, j)),
        pl.BlockSpec((1, tf, d), lambda i, j: (layer, j, 0)),
    ]
    args = [x, g, w_up, w_down]
    if proj is not None:
        a, w_o, lo = proj
        k = a.shape[1]
        in_specs = [pl.BlockSpec((tm, k), lambda i, j: (i, 0)),
                    pl.BlockSpec((1, k, d), lambda i, j: (lo, 0, 0), pipeline_mode=pl.Buffered(1))] + in_specs
        args = [a, w_o] + args
    out_specs = [pl.BlockSpec((tm, d), lambda i, j: (i, 0))]
    out_shape = [jax.ShapeDtypeStruct((m, d), f32)]
    if emit:
        out_specs += [pl.BlockSpec((1, d, tf), lambda i, j: (0, 0, j)), pl.BlockSpec((1, tf, d), lambda i, j: (0, j, 0))]
        out_shape += [jax.ShapeDtypeStruct((1, d, ff), bf16), jax.ShapeDtypeStruct((1, ff, d), bf16)]
    out = pl.pallas_call(
        functools.partial(_mlp_kernel, proj=proj is not None, emit=emit),
        grid=(m // tm, ff // tf),
        in_specs=in_specs,
        out_specs=out_specs,
        out_shape=out_shape,
        scratch_shapes=[pltpu.VMEM((tm, d), bf16)],
        compiler_params=_params("parallel", "arbitrary"),
        name="mlp",
    )(*args)
    return out if emit else out[0]


A_BLK_GROUPS = 2
A_BLK = A_BLK_GROUPS * A_GROUP_W
A_NBLK = A_HALF // A_BLK


def _mixer_a_kernel(*refs, sample):
    if sample:
        (ws_ref, bs_ref, x_ref, g_ref, win_ref, nv_ref, wout_ref, o_ref, vo_ref, winb_ref, woutb_ref,
         xn_ref, v_ref, ssq_ref, us_ref) = refs
    else:
        x_ref, g_ref, win_ref, nv_ref, ws_ref, bs_ref, wout_ref, o_ref, xn_ref, v_ref, ssq_ref, us_ref = refs
        winb_ref = woutb_ref = None
    j = pl.program_id(1)
    tm = x_ref.shape[0]

    def w_in_block():
        w = win_ref[0].astype(bf16)
        if winb_ref is not None:
            winb_ref[0] = w
        return w

    def w_out_block():
        w = wout_ref[0].astype(bf16)
        if woutb_ref is not None:
            woutb_ref[0] = w
        return w

    @pl.when(j == 0)
    def _():
        xn_ref[...] = _rms(x_ref[...], g_ref[...]).astype(bf16)
        ssq_ref[...] = jnp.zeros(ssq_ref.shape, f32)

    for k in range(A_NBLK):
        @pl.when(j == k)
        def _(k=k):
            v = _gelu(_dot(xn_ref[...], w_in_block()))
            v_ref[:, k * A_BLK:(k + 1) * A_BLK] = v
            ssq_ref[...] += jnp.sum(v * v, axis=-1, keepdims=True)

    if not sample:
        row = lax.broadcasted_iota(jnp.int32, (A_CHUNK, A_CHUNK), 0)
        col = lax.broadcasted_iota(jnp.int32, (A_CHUNK, A_CHUNK), 1)
        causal = row >= col

    for k in range(A_NBLK):
        @pl.when(j == A_NBLK + k)
        def _(k=k):
            u = _gelu(_dot(xn_ref[...], w_in_block()))
            rinv = lax.rsqrt(ssq_ref[...] * (1.0 / A_HALF) + EPS)
            for gg in range(A_BLK_GROUPS):
                g = k * A_BLK_GROUPS + gg
                cols = slice(g * A_GROUP_W, (g + 1) * A_GROUP_W)
                ucols = slice(gg * A_GROUP_W, (gg + 1) * A_GROUP_W)
                vn = v_ref[:, cols] * rinv * nv_ref[:, cols]
                if sample:
                    vo_ref[:, cols] = vn
                    vt = [vn[t * DEC_BATCH:(t + 1) * DEC_BATCH] for t in range(DEC_SEQ)]
                    s_rows = []
                    for t in range(DEC_SEQ):
                        s = ws_ref[(g * DEC_SEQ + t) * DEC_SEQ] * vt[0]
                        for t2 in range(1, t + 1):
                            s = s + ws_ref[(g * DEC_SEQ + t) * DEC_SEQ + t2] * vt[t2]
                        s_rows.append(s + bs_ref[g * DEC_SEQ + t])
                    s = jnp.concatenate(s_rows, axis=0)
                else:
                    w = jnp.where(causal, ws_ref[g], 0.0).astype(bf16)
                    bias = bs_ref[:, g:g + 1]
                    vb = vn.astype(bf16)
                    s = jnp.concatenate(
                        [_dot(w, vb[c * A_CHUNK:(c + 1) * A_CHUNK]) + bias for c in range(tm // A_CHUNK)], axis=0)
                us_ref[:, ucols] = (u[:, ucols] * s).astype(bf16)
            y = _dot(us_ref[...], w_out_block())
            if k == 0:
                o_ref[...] = x_ref[...] + y
            else:
                o_ref[...] += y


def mixer_a(x, g, w_in, norm_v, w_sp, b_sp, w_out, layer, *, tm, sample):
    m, d = x.shape
    nj = 2 * A_NBLK
    row = lambda w: pl.BlockSpec((tm, w), lambda i, j: (i, 0))
    full = lambda *shape: pl.BlockSpec(shape, lambda i, j: (0,) * len(shape))
    smem = pl.BlockSpec(memory_space=pltpu.SMEM)
    win_spec = pl.BlockSpec((1, d, A_BLK), lambda i, j: (layer, 0, (j + A_NBLK) % nj))
    wout_spec = pl.BlockSpec((1, A_BLK, d), lambda i, j: (layer, jnp.maximum(j - A_NBLK, 0), 0))
    if sample:
        in_specs = [smem, smem, row(d), full(1, d), win_spec, full(1, A_HALF), wout_spec]
        args = (w_sp, b_sp, x, g, w_in, norm_v, w_out)
        out_specs = [row(d), row(A_HALF),
                     pl.BlockSpec((1, d, A_BLK), lambda i, j: (0, 0, (j + A_NBLK) % nj)),
                     pl.BlockSpec((1, A_BLK, d), lambda i, j: (0, jnp.maximum(j - A_NBLK, 0), 0))]
        out_shape = [jax.ShapeDtypeStruct((m, d), f32), jax.ShapeDtypeStruct((m, A_HALF), f32),
                     jax.ShapeDtypeStruct((1, d, 2 * A_HALF), bf16), jax.ShapeDtypeStruct((1, A_HALF, d), bf16)]
    else:
        in_specs = [row(d), full(1, d), win_spec, full(1, A_HALF), full(A_GROUPS, A_CHUNK, A_CHUNK),
                    full(A_CHUNK, A_GROUPS), wout_spec]
        args = (x, g, w_in, norm_v, w_sp, b_sp, w_out)
        out_specs = row(d)
        out_shape = jax.ShapeDtypeStruct((m, d), f32)
    return pl.pallas_call(
        functools.partial(_mixer_a_kernel, sample=sample),
        grid=(m // tm, nj),
        in_specs=in_specs,
        out_specs=out_specs,
        out_shape=out_shape,
        scratch_shapes=[pltpu.VMEM((tm, d), bf16), pltpu.VMEM((tm, A_HALF), f32), pltpu.VMEM((tm, 1), f32),
                        pltpu.VMEM((tm, A_BLK), bf16)],
        compiler_params=_params("parallel", "arbitrary"),
        name="mixer_a_sample" if sample else "mixer_a_prompt",
    )(*args)


def _bucket_table():
    i = np.arange(B_BLOCK)[:, None]
    j = np.arange(2 * B_BLOCK)[None, :]
    n = np.maximum(B_BLOCK + i - j, 0)
    max_exact = N_BUCKETS // 2
    nf = np.maximum(n, 1).astype(np.float64)
    val = np.log(nf / max_exact) / math.log(MAX_DISTANCE / max_exact) * (N_BUCKETS - max_exact)
    in_window = (n >= max_exact) & (n < B_WINDOW)
    assert np.all(np.abs(val - np.round(val))[in_window & (n != max_exact)] > 1e-3)
    large = np.minimum(max_exact + np.floor(val + 1e-9).astype(np.int64), N_BUCKETS - 1)
    return np.where(n < max_exact, n, large).astype(np.int32)


def _bias_table_kernel(rb_ref, bk_ref, o_ref):
    bk = bk_ref[...]
    for h in range(B_HEADS):
        acc = jnp.zeros(bk.shape, f32)
        for b in range(N_BUCKETS):
            acc = jnp.where(bk == b, rb_ref[b * B_HEADS + h], acc)
        o_ref[h] = acc


def bias_table(rel_bias):
    return pl.pallas_call(
        _bias_table_kernel,
        in_specs=[pl.BlockSpec(memory_space=pltpu.SMEM), pl.BlockSpec(memory_space=pltpu.VMEM)],
        out_specs=pl.BlockSpec(memory_space=pltpu.VMEM),
        out_shape=jax.ShapeDtypeStruct((B_HEADS, B_BLOCK, 2 * B_BLOCK), f32),
        name="bias_table",
    )(rel_bias.reshape(-1), jnp.asarray(_bucket_table()))


def _softmax_with_sink(logits, sink):
    m = jnp.maximum(jnp.max(logits, axis=-1, keepdims=True), sink)
    p = jnp.exp(logits - m)
    return p, jnp.sum(p, axis=-1, keepdims=True) + jnp.exp(sink - m)


def _rms_head_pairs(x, g2, lo):
    sq = x * x
    s_lo = jnp.sum(jnp.where(lo, sq, 0.0), axis=-1, keepdims=True)
    s_hi = jnp.sum(jnp.where(lo, 0.0, sq), axis=-1, keepdims=True)
    r = lax.rsqrt(jnp.where(lo, s_lo, s_hi) * (1.0 / B_HEAD_DIM) + EPS)
    return x * r * g2


ATT_SUB = 8


def _attn_prompt_kernel(sink_ref, qkv_ref, tab_ref, qn_ref, kn_ref, o_ref, ko_ref, vo_ref,
                        kband_ref, vband_ref, q_ref, p_ref, rhs_ref):
    n = pl.program_id(1)
    T = B_BLOCK

    @pl.when(n == 0)
    def _():
        kband_ref[0:T, :] = jnp.zeros((T, B_KV_DIM), f32)
        vband_ref[0:T, :] = jnp.zeros((T, B_KV_DIM), f32)
        rhs_ref[...] = jnp.ones(rhs_ref.shape, bf16)

    for sub in range(ATT_SUB):
        rows = pl.ds(sub * T, T)
        first_key = jnp.where(n == 0, T, 0) if sub == 0 else 0
        _attn_block(first_key, sink_ref, qkv_ref.at[rows], tab_ref, qn_ref, kn_ref, o_ref.at[rows], ko_ref, vo_ref,
                    kband_ref, vband_ref, q_ref.at[sub], p_ref.at[sub], rhs_ref.at[sub])


def _attn_block(first_key, sink_ref, qkv_ref, tab_ref, qn_ref, kn_ref, o_ref, ko_ref, vo_ref,
                kband_ref, vband_ref, q_ref, p_ref, rhs_ref):
    T = B_BLOCK
    lo = lax.broadcasted_iota(jnp.int32, (1, LANES), 1) < B_HEAD_DIM
    qn2 = qn_ref[...]
    kn2 = kn_ref[...]
    for t in range(B_KV_DIM // LANES):
        lanes = slice(t * LANES, (t + 1) * LANES)
        k2 = _rms_head_pairs(qkv_ref[:, B_Q_DIM + t * LANES:B_Q_DIM + (t + 1) * LANES].astype(f32), kn2, lo)
        ko_ref[0, :, lanes] = k2
        kband_ref[T:, lanes] = k2
    v = qkv_ref[:, B_Q_DIM + B_KV_DIM:].astype(f32)
    vo_ref[0] = v
    vband_ref[T:, :] = v
    for t in range(B_Q_DIM // LANES):
        q2 = qkv_ref[:, t * LANES:(t + 1) * LANES].astype(f32)
        q_ref[t * T:(t + 1) * T, :] = _rms_head_pairs(q2, qn2, lo).astype(bf16)

    key_ops = {}
    band_row = lax.broadcasted_iota(jnp.int32, (2 * T, LANES), 0)
    for t in range(B_KV_DIM // LANES):
        lanes = slice(t * LANES, (t + 1) * LANES)
        kt = kband_ref[:, lanes]
        kr = pltpu.roll(kt, B_HEAD_DIM, axis=1)
        vt = jnp.where(band_row == 0, 0.0, vband_ref[:, lanes])
        vr = pltpu.roll(vt, B_HEAD_DIM, axis=1)
        hi = jnp.logical_not(lo)
        for half, (ksrc, vsrc) in enumerate(((kt, vt), (kr, vr))):
            g_lo, g_hi = (2 * t, 2 * t + 1) if half == 0 else (2 * t + 1, 2 * t)
            key_ops[(g_lo, 0)] = jnp.where(lo, ksrc, 0.0).astype(bf16)
            key_ops[(g_hi, 1)] = jnp.where(hi, ksrc, 0.0).astype(bf16)
            rhs_ref[g_lo * 2 + 0, :, 0:LANES] = jnp.where(lo, vsrc, 1.0).astype(bf16)
            rhs_ref[g_hi * 2 + 1, :, 0:LANES] = jnp.where(hi, vsrc, 1.0).astype(bf16)

    i = lax.broadcasted_iota(jnp.int32, (T, 2 * T), 0)
    j = lax.broadcasted_iota(jnp.int32, (T, 2 * T), 1)
    valid = (j > i) & (j <= i + B_WINDOW) & (j >= first_key)
    sink_col = j == 0
    lo_t = lax.broadcasted_iota(jnp.int32, (T, LANES), 1) < B_HEAD_DIM
    for g in range(B_KV_HEADS):
        q2 = q_ref[2 * g * T:(2 * g + 2) * T, :]
        res = []
        for half in range(2):
            logits = _dot_nt(q2, key_ops[(g, half)]) * (B_HEAD_DIM ** -0.5)
            for pair in range(2):
                h = g * B_REP + 2 * pair + half
                l = jnp.where(valid, logits[pair * T:(pair + 1) * T] + tab_ref[h], NEG_INF)
                l = jnp.where(sink_col, sink_ref[h], l)
                p = jnp.exp(l - jnp.max(l, axis=-1, keepdims=True))
                p_ref[g * 2 + half, pair * T:(pair + 1) * T, :] = p.astype(bf16)
            res.append(_dot(p_ref[g * 2 + half], rhs_ref[g * 2 + half]))
        for pair in range(2):
            rows = slice(pair * T, (pair + 1) * T)
            even = res[0][rows, 0:LANES] / res[0][rows, LANES:]
            odd = res[1][rows, 0:LANES] / res[1][rows, LANES:]
            t = 2 * g + pair
            o_ref[:, t * LANES:(t + 1) * LANES] = jnp.where(lo_t, even, odd).astype(bf16)
    kband_ref[0:T, :] = kband_ref[T:, :]
    vband_ref[0:T, :] = vband_ref[T:, :]


def attn_prompt(qkv, table, q_norm, k_norm, sinks):
    step = ATT_SUB * B_BLOCK
    nb = SEQ // step
    return pl.pallas_call(
        _attn_prompt_kernel,
        grid=(BATCH, nb),
        in_specs=[
            pl.BlockSpec(memory_space=pltpu.SMEM),
            pl.BlockSpec((step, B_Q_DIM + 2 * B_KV_DIM), lambda b, n: (b * nb + n, 0)),
            pl.BlockSpec((B_HEADS, B_BLOCK, 2 * B_BLOCK), lambda b, n: (0, 0, 0)),
            pl.BlockSpec((1, LANES), lambda b, n: (0, 0)),
            pl.BlockSpec((1, LANES), lambda b, n: (0, 0)),
        ],
        out_specs=[
            pl.BlockSpec((step, B_Q_DIM), lambda b, n: (b * nb + n, 0)),
            pl.BlockSpec((1, B_BLOCK, B_KV_DIM), lambda b, n: (b, 0, 0)),
            pl.BlockSpec((1, B_BLOCK, B_KV_DIM), lambda b, n: (b, 0, 0)),
        ],
        out_shape=[
            jax.ShapeDtypeStruct((PROMPT_ROWS, B_Q_DIM), bf16),
            jax.ShapeDtypeStruct((BATCH, B_BLOCK, B_KV_DIM), f32),
            jax.ShapeDtypeStruct((BATCH, B_BLOCK, B_KV_DIM), f32),
        ],
        scratch_shapes=[
            pltpu.VMEM((2 * B_BLOCK, B_KV_DIM), f32),
            pltpu.VMEM((2 * B_BLOCK, B_KV_DIM), f32),
            pltpu.VMEM((ATT_SUB, B_Q_DIM // LANES * B_BLOCK, LANES), bf16),
            pltpu.VMEM((ATT_SUB, 2 * B_KV_HEADS, 2 * B_BLOCK, 2 * B_BLOCK), bf16),
            pltpu.VMEM((ATT_SUB, 2 * B_KV_HEADS, 2 * B_BLOCK, 2 * LANES), bf16),
        ],
        compiler_params=_params("parallel", "arbitrary"),
        name="attn_prompt",
    )(sinks, qkv, table, jnp.tile(q_norm, (1, LANES // B_HEAD_DIM)), jnp.tile(k_norm, (1, LANES // B_HEAD_DIM)))


ATT_BB = 8
ATT_QR = B_REP * DEC_SEQ
ATT_KEYS = B_WINDOW + 2 * DEC_SEQ


def _attn_sample_kernel(sink_ref, q_ref, kn_ref, vn_ref, kc_ref, vc_ref, tab_ref, qn_ref, knm_ref,
                        o_ref, ko_ref, vo_ref, kall_ref, vall_ref):
    qn2 = qn_ref[...]
    kn2 = knm_ref[...]
    n_rows = B_KV_HEADS * ATT_QR
    lo = lax.broadcasted_iota(jnp.int32, (1, LANES), 1) < B_HEAD_DIM
    row = lax.broadcasted_iota(jnp.int32, (n_rows, ATT_KEYS), 0)
    j = lax.broadcasted_iota(jnp.int32, (n_rows, ATT_KEYS), 1)
    t = row % DEC_SEQ
    valid = (j > t) & (j <= t + B_WINDOW)
    q_row_group = lax.broadcasted_iota(jnp.int32, (n_rows, B_KV_DIM), 0) // ATT_QR
    q_lane_group = lax.broadcasted_iota(jnp.int32, (n_rows, B_KV_DIM), 1) // B_HEAD_DIM
    own_group = q_row_group == q_lane_group
    o_lane_group = lax.broadcasted_iota(jnp.int32, (ATT_QR, B_KV_DIM), 1) // B_HEAD_DIM
    pad = jnp.zeros((ATT_KEYS - B_WINDOW - DEC_SEQ, B_KV_DIM), f32)
    bias = tab_ref[...]
    sink = sink_ref[...]
    for s in range(ATT_BB):
        kc = kc_ref[s]
        vc = vc_ref[s]
        k_new = jnp.concatenate(
            [_rms_head_pairs(kn_ref[s, :, tt * LANES:(tt + 1) * LANES], kn2, lo) for tt in range(B_KV_DIM // LANES)],
            axis=1)
        v_new = vn_ref[s]
        ko_ref[s, 0:B_WINDOW - DEC_SEQ, :] = kc[DEC_SEQ:, :]
        ko_ref[s, B_WINDOW - DEC_SEQ:, :] = k_new
        vo_ref[s, 0:B_WINDOW - DEC_SEQ, :] = vc[DEC_SEQ:, :]
        vo_ref[s, B_WINDOW - DEC_SEQ:, :] = v_new
        kall_ref[s, 0:B_WINDOW, :] = kc
        kall_ref[s, B_WINDOW:B_WINDOW + DEC_SEQ, :] = k_new
        kall_ref[s, B_WINDOW + DEC_SEQ:, :] = pad
        vall_ref[s, 0:B_WINDOW, :] = vc
        vall_ref[s, B_WINDOW:B_WINDOW + DEC_SEQ, :] = v_new
        vall_ref[s, B_WINDOW + DEC_SEQ:, :] = pad
        q = q_ref[s]
        qn = jnp.concatenate(
            [_rms_head_pairs(q[:, tt * LANES:(tt + 1) * LANES], qn2, lo) for tt in range(B_KV_DIM // LANES)], axis=1)
        q_all = jnp.where(own_group, jnp.concatenate([qn] * B_KV_HEADS, axis=0), 0.0).astype(bf16)
        logits = _dot_nt(q_all, kall_ref[s].astype(bf16)) * (B_HEAD_DIM ** -0.5)
        logits = jnp.where(valid, logits + bias, NEG_INF)
        p, denom = _softmax_with_sink(logits, sink)
        res = _dot(p.astype(bf16), vall_ref[s].astype(bf16)) / denom
        out = jnp.zeros((ATT_QR, B_KV_DIM), f32)
        for g in range(B_KV_HEADS):
            out = jnp.where(o_lane_group == g, res[g * ATT_QR:(g + 1) * ATT_QR, :], out)
        o_ref[s] = out


def attn_sample(q_s, k_new, v_new, k_cache, v_cache, table_s, q_norm, k_norm, sinks):
    blk = lambda *shape: pl.BlockSpec((ATT_BB,) + shape, lambda i: (i,) + (0,) * len(shape))
    full = lambda *shape: pl.BlockSpec(shape, lambda i: (0,) * len(shape))
    return pl.pallas_call(
        _attn_sample_kernel,
        grid=(DEC_BATCH // ATT_BB,),
        in_specs=[
            full(B_KV_HEADS * ATT_QR, 1),
            blk(ATT_QR, B_KV_DIM), blk(DEC_SEQ, B_KV_DIM), blk(DEC_SEQ, B_KV_DIM),
            blk(B_WINDOW, B_KV_DIM), blk(B_WINDOW, B_KV_DIM),
            full(B_KV_HEADS * ATT_QR, ATT_KEYS), full(1, LANES), full(1, LANES),
        ],
        out_specs=[blk(ATT_QR, B_KV_DIM), blk(B_WINDOW, B_KV_DIM), blk(B_WINDOW, B_KV_DIM)],
        out_shape=[
            jax.ShapeDtypeStruct((DEC_BATCH, ATT_QR, B_KV_DIM), f32),
            jax.ShapeDtypeStruct((DEC_BATCH, B_WINDOW, B_KV_DIM), f32),
            jax.ShapeDtypeStruct((DEC_BATCH, B_WINDOW, B_KV_DIM), f32),
        ],
        scratch_shapes=[pltpu.VMEM((ATT_BB, ATT_KEYS, B_KV_DIM), f32), pltpu.VMEM((ATT_BB, ATT_KEYS, B_KV_DIM), f32)],
        compiler_params=_params("parallel"),
        name="attn_sample",
    )(jnp.repeat(sinks, DEC_SEQ).reshape(B_KV_HEADS * ATT_QR, 1), q_s, k_new, v_new, k_cache, v_cache,
      table_s.reshape(B_KV_HEADS * ATT_QR, ATT_KEYS),
      jnp.tile(q_norm, (1, LANES // B_HEAD_DIM)), jnp.tile(k_norm, (1, LANES // B_HEAD_DIM)))


CONV_PAD = SUBLANES

def _gated_group_norm(y, z, norm_w):
    gt = y * _silu(z)
    parts = []
    for g in range(C_GROUPS):
        gg = gt[:, g * C_GROUP_W:(g + 1) * C_GROUP_W]
        parts.append(gg * lax.rsqrt(jnp.mean(gg * gg, axis=-1, keepdims=True) + EPS))
    return jnp.concatenate(parts, axis=1) * norm_w


LOG2E = math.log2(math.e)


def _expand_heads(v, sel3):
    lane = lax.broadcasted_iota(jnp.int32, (1, LANES), 1)
    v = jnp.where(lane < C_HEADS, v, 0.0)
    hi = v.astype(bf16).astype(f32)
    r1 = v - hi
    mid = r1.astype(bf16).astype(f32)
    lo = r1 - mid
    packed = hi + pltpu.roll(mid, C_HEADS, axis=1) + pltpu.roll(lo, 2 * C_HEADS, axis=1)
    return _dot(packed.astype(bf16), sel3)


SSD_SUB = 4


def _ssd_prompt_kernel(zx_ref, dtr_ref, cw_ref, cb_ref, dtb_ref, alog_ref, dsk_ref, nw_ref, sel_ref,
                       yn_ref, hfin_ref, cout_ref, xpad_ref, ht_ref, y_ref):
    c = pl.program_id(1)

    @pl.when(c == 0)
    def _():
        xpad_ref[0, 0:CONV_PAD, :] = jnp.zeros((CONV_PAD, C_CONV_DIM), f32)
        ht_ref[...] = jnp.zeros(ht_ref.shape, f32)

    for sub in range(SSD_SUB):
        rows = pl.ds(sub * C_CHUNK, C_CHUNK)
        _ssd_chunk(zx_ref.at[rows], dtr_ref.at[rows], cw_ref, cb_ref, dtb_ref, alog_ref, dsk_ref, nw_ref, sel_ref,
                   yn_ref.at[rows], cout_ref, xpad_ref.at[sub], xpad_ref.at[(sub + 1) % SSD_SUB], ht_ref,
                   y_ref.at[sub])

    @pl.when(c == pl.num_programs(1) - 1)
    def _():
        for t in range(C_D_INNER // LANES):
            hfin_ref[0, t * LANES:(t + 1) * LANES, :] = ht_ref[:, t * LANES:(t + 1) * LANES].T


def _ssd_chunk(zx_ref, dtr_ref, cw_ref, cb_ref, dtb_ref, alog_ref, dsk_ref, nw_ref, sel_ref,
               yn_ref, cout_ref, xpad_ref, xpad_next_ref, ht_ref, y_ref):
    T = C_CHUNK
    xbc = zx_ref[:, C_D_INNER:].astype(f32)
    xpad_ref[CONV_PAD:, :] = xbc
    xp = xpad_ref[...]
    cw = cw_ref[...]
    acc = cb_ref[...]
    for tap in range(C_D_CONV - 1):
        shifted = pltpu.roll(xp, C_D_CONV - 1 - tap, axis=0)[CONV_PAD:, :]
        acc = acc + shifted * cw[tap:tap + 1, :]
    acc = acc + xbc * cw[C_D_CONV - 1:C_D_CONV, :]
    xpad_next_ref[0:CONV_PAD, :] = xbc[T - CONV_PAD:, :]
    cout_ref[0] = xbc[T - (C_D_CONV - 1):, :]
    act = _silu(acc)
    xs = act[:, :C_D_INNER]
    bm = act[:, C_D_INNER:C_D_INNER + C_BC_DIM]
    cm = act[:, C_D_INNER + C_BC_DIM:]
    xb = xs.astype(bf16)

    dt = _softplus(dtr_ref[...] + dtb_ref[...])
    a_neg = -jnp.exp(alog_ref[...])
    row = lax.broadcasted_iota(jnp.int32, (T, T), 0)
    col = lax.broadcasted_iota(jnp.int32, (T, T), 1)
    causal = row >= col
    acs = _dot_exact_lhs01(causal.astype(f32), dt * a_neg)
    a2 = acs * LOG2E
    sel3 = sel_ref[...]
    e_exp = jnp.exp2(_expand_heads(a2, sel3))
    w_exp = _expand_heads(jnp.exp(acs[T - 1:T, :] - acs) * dt, sel3)
    cdec = e_exp[T - 1:T, :]
    b2_t = a2.T - jnp.log2(dt.T)
    xw = (xs * w_exp).astype(bf16)
    hb = ht_ref[...].astype(bf16)
    dsk = dsk_ref[...]
    lo_t = lax.broadcasted_iota(jnp.int32, (T, LANES), 1) < C_HEAD_DIM

    for g in range(C_GROUPS):
        ns = slice(g * C_D_STATE, (g + 1) * C_D_STATE)
        gs = slice(g * C_GROUP_W, (g + 1) * C_GROUP_W)
        b_g = bm[:, ns]
        c_g = cm[:, ns].astype(bf16)
        cb = _dot_nt(c_g, b_g.astype(bf16))
        yi = _dot(c_g, hb[:, gs])
        ht_ref[:, gs] = ht_ref[:, gs] * cdec[:, gs] + _dot(b_g.T.astype(bf16), xw[:, gs])
        for tt in range(C_GROUP_W // LANES):
            t = g * (C_GROUP_W // LANES) + tt
            lanes = slice(t * LANES, (t + 1) * LANES)
            xt = xb[:, lanes]
            res = []
            for half in range(2):
                h = 2 * t + half
                a_col = jnp.broadcast_to(a2[:, h:h + 1], (T, T))
                b_row = jnp.broadcast_to(b2_t[h:h + 1, :], (T, T))
                w = jnp.where(causal, cb * jnp.exp2(a_col - b_row), 0.0)
                res.append(_dot(w.astype(bf16), xt))
            y_intra = jnp.where(lo_t, res[0], res[1])
            y_ref[:, lanes] = y_intra + e_exp[:, lanes] * yi[:, tt * LANES:(tt + 1) * LANES] + dsk[:, lanes] * xs[:, lanes]

    yn_ref[...] = _gated_group_norm(y_ref[...], zx_ref[:, :C_D_INNER].astype(f32), nw_ref[...]).astype(bf16)


def _head_select3():
    k = np.arange(LANES)[:, None]
    ch = np.arange(C_D_INNER)[None, :] // C_HEAD_DIM
    return jnp.asarray((k % C_HEADS == ch) & (k < 3 * C_HEADS), dtype=bf16)


def ssd_prompt(zx, dtr, conv_w, conv_b, dt_bias, a_log, d_skip, norm_w):
    step = SSD_SUB * C_CHUNK
    nc = SEQ // step
    full = lambda *shape: pl.BlockSpec(shape, lambda b, c: (0,) * len(shape))
    return pl.pallas_call(
        _ssd_prompt_kernel,
        grid=(BATCH, nc),
        in_specs=[
            pl.BlockSpec((step, C_D_INNER + C_CONV_DIM), lambda b, c: (b * nc + c, 0)),
            pl.BlockSpec((step, LANES), lambda b, c: (b * nc + c, 0)),
            full(C_D_CONV, C_CONV_DIM), full(1, C_CONV_DIM), full(1, LANES), full(1, LANES),
            full(1, C_D_INNER), full(1, C_D_INNER), full(LANES, C_D_INNER),
        ],
        out_specs=[
            pl.BlockSpec((step, C_D_INNER), lambda b, c: (b * nc + c, 0)),
            pl.BlockSpec((1, C_D_INNER, C_D_STATE), lambda b, c: (b, 0, 0)),
            pl.BlockSpec((1, C_D_CONV - 1, C_CONV_DIM), lambda b, c: (b, 0, 0)),
        ],
        out_shape=[
            jax.ShapeDtypeStruct((PROMPT_ROWS, C_D_INNER), bf16),
            jax.ShapeDtypeStruct((BATCH, C_D_INNER, C_D_STATE), f32),
            jax.ShapeDtypeStruct((BATCH, C_D_CONV - 1, C_CONV_DIM), f32),
        ],
        scratch_shapes=[
            pltpu.VMEM((SSD_SUB, CONV_PAD + C_CHUNK, C_CONV_DIM), f32),
            pltpu.VMEM((C_D_STATE, C_D_INNER), f32),
            pltpu.VMEM((SSD_SUB, C_CHUNK, C_D_INNER), f32),
        ],
        compiler_params=_params("parallel", "arbitrary"),
        name="ssd_prompt",
    )(zx, dtr, conv_w, conv_b, dt_bias, a_log, d_skip, norm_w, _head_select3())


SSD_BB = 8
SSD_TP = SUBLANES
_N_PAIRS = DEC_SEQ * (DEC_SEQ + 1) // 2
_N_COEF = _N_PAIRS + 2 * DEC_SEQ


def _ssd_sample_kernel(zx_ref, dtr_ref, cs_ref, h0_ref, cw_ref, cb_ref, dtb_ref, alog_ref, dsk_ref, nw_ref,
                       sel_ref, yn_ref, hn_ref, cout_ref, c_scr, b_scr, xw_scr, yi_scr, cd_scr):
    L = DEC_SEQ
    cw = cw_ref[...]
    xp = [cs_ref[k] for k in range(C_D_CONV - 1)] + [zx_ref[t, :, C_D_INNER:] for t in range(L)]
    for k in range(C_D_CONV - 1):
        cout_ref[k] = xp[L + k]
    act = []
    for t in range(L):
        acc = cb_ref[...]
        for tap in range(C_D_CONV):
            acc = acc + xp[t + tap] * cw[tap:tap + 1, :]
        act.append(_silu(acc))
    xs = [a[:, :C_D_INNER] for a in act]
    bm = [a[:, C_D_INNER:C_D_INNER + C_BC_DIM] for a in act]
    cm = [a[:, C_D_INNER + C_BC_DIM:] for a in act]

    a_neg = -jnp.exp(alog_ref[...])
    dt = [_softplus(dtr_ref[t] + dtb_ref[...]) for t in range(L)]
    acs = []
    for t in range(L):
        acs.append(dt[t] * a_neg if t == 0 else acs[t - 1] + dt[t] * a_neg)

    lane_group = lax.broadcasted_iota(jnp.int32, (SSD_BB, LANES), 1) // C_REP
    coefs = []
    for t in range(L):
        for t2 in range(t + 1):
            cbh = jnp.zeros((SSD_BB, LANES), f32)
            for g in range(C_GROUPS):
                ns = slice(g * C_D_STATE, (g + 1) * C_D_STATE)
                cbg = jnp.sum(cm[t][:, ns] * bm[t2][:, ns], axis=-1, keepdims=True)
                cbh = jnp.where(lane_group == g, cbg, cbh)
            coefs.append(cbh * jnp.exp(acs[t] - acs[t2]) * dt[t2])
    for t in range(L):
        coefs.append(jnp.exp(acs[t]))
    for t in range(L):
        coefs.append(jnp.exp(acs[L - 1] - acs[t]) * dt[t])
    coef = jnp.concatenate(coefs, axis=0)
    cexp = _dot_exact_rhs01(coef, sel_ref[...])
    cexp = [cexp[k * SSD_BB:(k + 1) * SSD_BB, :] for k in range(_N_COEF)]
    w_intra = cexp[:_N_PAIRS]
    w_inter = cexp[_N_PAIRS:_N_PAIRS + L]
    w_state = cexp[_N_PAIRS + L:]

    cd = jnp.concatenate([jnp.exp(acs[L - 1]), jnp.zeros((LANES - SSD_BB, LANES), f32)], axis=0)
    cd_t = cd.T
    for s in range(SSD_BB):
        cd_scr[s] = jnp.broadcast_to(cd_t[0:C_HEADS, s:s + 1], (C_HEADS, C_D_STATE))

    zeros_tail = jnp.zeros((SSD_BB, SSD_TP - L, C_D_INNER), f32)
    c_scr[:, L:, :] = zeros_tail[:, :, :C_BC_DIM]
    b_scr[:, L:, :] = zeros_tail[:, :, :C_BC_DIM]
    xw_scr[:, L:, :] = zeros_tail
    for t in range(L):
        xw_t = xs[t] * w_state[t]
        for s in range(SSD_BB):
            c_scr[s, t:t + 1, :] = cm[t][s:s + 1, :]
            b_scr[s, t:t + 1, :] = bm[t][s:s + 1, :]
            xw_scr[s, t:t + 1, :] = xw_t[s:s + 1, :]

    for s in range(SSD_BB):
        for g in range(C_GROUPS):
            ns = slice(g * C_D_STATE, (g + 1) * C_D_STATE)
            gs = slice(g * C_GROUP_W, (g + 1) * C_GROUP_W)
            h0 = h0_ref[s, gs, :]
            yi = _dot_nt(c_scr[s, :, ns].astype(bf16), h0.astype(bf16))
            for t in range(L):
                yi_scr[t, s:s + 1, gs] = yi[t:t + 1, :]
            st = _dot_tn(xw_scr[s, :, gs].astype(bf16), b_scr[s, :, ns].astype(bf16))
            for r in range(C_REP):
                h = g * C_REP + r
                rs = slice(r * C_HEAD_DIM, (r + 1) * C_HEAD_DIM)
                scale = cd_scr[s, h:h + 1, :]
                hn_ref[s, h * C_HEAD_DIM:(h + 1) * C_HEAD_DIM, :] = h0[rs, :] * scale + st[rs, :]

    dsk = dsk_ref[...]
    nw = nw_ref[...]
    pair = 0
    for t in range(L):
        y = w_inter[t] * yi_scr[t] + dsk * xs[t]
        for t2 in range(t + 1):
            y = y + w_intra[pair] * xs[t2]
            pair += 1
        yn_ref[t] = _gated_group_norm(y, zx_ref[t, :, :C_D_INNER], nw).astype(bf16)


def ssd_sample(zx_t, dtr_t, conv_state_t, h0, conv_w, conv_b, dt_bias, a_log, d_skip, norm_w, sel):
    tmaj = lambda n, w: pl.BlockSpec((n, SSD_BB, w), lambda i: (0, i, 0))
    full = lambda *shape: pl.BlockSpec(shape, lambda i: (0,) * len(shape))
    return pl.pallas_call(
        _ssd_sample_kernel,
        grid=(DEC_BATCH // SSD_BB,),
        in_specs=[
            tmaj(DEC_SEQ, C_D_INNER + C_CONV_DIM), tmaj(DEC_SEQ, LANES), tmaj(C_D_CONV - 1, C_CONV_DIM),
            pl.BlockSpec((SSD_BB, C_D_INNER, C_D_STATE), lambda i: (i, 0, 0)),
            full(C_D_CONV, C_CONV_DIM), full(1, C_CONV_DIM), full(1, LANES), full(1, LANES),
            full(1, C_D_INNER), full(1, C_D_INNER), full(LANES, C_D_INNER),
        ],
        out_specs=[
            tmaj(DEC_SEQ, C_D_INNER),
            pl.BlockSpec((SSD_BB, C_D_INNER, C_D_STATE), lambda i: (i, 0, 0)),
            tmaj(C_D_CONV - 1, C_CONV_DIM),
        ],
        out_shape=[
            jax.ShapeDtypeStruct((DEC_SEQ, DEC_BATCH, C_D_INNER), bf16),
            jax.ShapeDtypeStruct((DEC_BATCH, C_D_INNER, C_D_STATE), f32),
            jax.ShapeDtypeStruct((C_D_CONV - 1, DEC_BATCH, C_CONV_DIM), f32),
        ],
        scratch_shapes=[
            pltpu.VMEM((SSD_BB, SSD_TP, C_BC_DIM), f32),
            pltpu.VMEM((SSD_BB, SSD_TP, C_BC_DIM), f32),
            pltpu.VMEM((SSD_BB, SSD_TP, C_D_INNER), f32),
            pltpu.VMEM((DEC_SEQ, SSD_BB, C_D_INNER), f32),
            pltpu.VMEM((SSD_BB, C_HEADS, C_D_STATE), f32),
        ],
        compiler_params=_params("parallel"),
        name="ssd_sample",
    )(zx_t, dtr_t, conv_state_t, h0, conv_w, conv_b, dt_bias, a_log, d_skip, norm_w, sel)


def _pad_lanes(v):
    return jnp.pad(v.astype(f32), (0, LANES - v.shape[0])).reshape(1, LANES)


def _mixer_a(xp, xs, g, j, w_in, norm_v, w_sp, b_sp, w_out):
    nv = norm_v.reshape(1, A_HALF)
    xs, v_s, w_in_b, w_out_b = mixer_a(xs, g, w_in, nv, w_sp[:, :DEC_SEQ, :DEC_SEQ].reshape(-1),
                                       b_sp[:, :DEC_SEQ].reshape(-1), w_out, j, tm=TM_SAMPLE, sample=True)
    xp = mixer_a(xp, g, w_in_b, nv, w_sp, b_sp.T, w_out_b, 0, tm=TM_PROMPT, sample=False)
    return xp, xs, jnp.swapaxes(v_s.reshape(DEC_SEQ, DEC_BATCH, A_HALF), 0, 1)


def _mixer_b(xp, xs, g, j, k_cache, v_cache, w_qkv, q_norm, k_norm, sinks, rel_bias):
    qn = q_norm.reshape(1, B_HEAD_DIM)
    kn = k_norm.reshape(1, B_HEAD_DIM)
    table = bias_table(rel_bias)
    n_qkv = B_Q_DIM + 2 * B_KV_DIM

    qkv_p = norm_matmul(xp, g, w_qkv, j, n_qkv, tm=2 * TM_PROMPT, tn=n_qkv, out_dtype=bf16)
    o_p, k_p, v_p = attn_prompt(qkv_p, table, qn, kn, sinks)

    qkv_s = norm_matmul(xs, g, w_qkv, j, n_qkv, tm=TM_SAMPLE, tn=512).reshape(DEC_SEQ, DEC_BATCH, -1)
    q_s = qkv_s[:, :, :B_Q_DIM].reshape(DEC_SEQ, DEC_BATCH, B_KV_HEADS, B_REP, B_HEAD_DIM)
    q_s = q_s.transpose(1, 3, 0, 2, 4).reshape(DEC_BATCH, ATT_QR, B_KV_DIM)
    k_new = jnp.swapaxes(qkv_s[:, :, B_Q_DIM:B_Q_DIM + B_KV_DIM], 0, 1)
    v_new = jnp.swapaxes(qkv_s[:, :, B_Q_DIM + B_KV_DIM:], 0, 1)
    table_s = table[:, :DEC_SEQ, :ATT_KEYS].reshape(B_KV_HEADS, ATT_QR, ATT_KEYS)
    o_s, k_s, v_s = attn_sample(q_s, k_new, v_new,
                                k_cache.reshape(DEC_BATCH, B_WINDOW, B_KV_DIM),
                                v_cache.reshape(DEC_BATCH, B_WINDOW, B_KV_DIM),
                                table_s, qn, kn, sinks)
    o_s = o_s.reshape(DEC_BATCH, B_REP, DEC_SEQ, B_KV_HEADS, B_HEAD_DIM).transpose(2, 0, 3, 1, 4)
    o_s = o_s.reshape(SAMPLE_ROWS, B_Q_DIM).astype(bf16)
    kv_shape_p = (BATCH, B_WINDOW, B_KV_HEADS, B_HEAD_DIM)
    kv_shape_s = (DEC_BATCH, B_WINDOW, B_KV_HEADS, B_HEAD_DIM)
    return o_p, o_s, k_p.reshape(kv_shape_p), v_p.reshape(kv_shape_p), k_s.reshape(kv_shape_s), v_s.reshape(kv_shape_s)


def _mixer_c(xp, xs, g, j, h0, conv_state, w_in, conv_w, conv_b, dt_bias, a_log, d_skip, norm_w):
    n_zx = C_D_INNER + C_CONV_DIM
    w_dt = jnp.pad(w_in[j, :, n_zx:], ((0, 0), (0, LANES - C_HEADS)))
    cb = conv_b.reshape(1, C_CONV_DIM)
    dtb = _pad_lanes(dt_bias)
    alog = _pad_lanes(a_log)
    dsk = jnp.repeat(d_skip.astype(f32), C_HEAD_DIM).reshape(1, C_D_INNER)
    nw = norm_w.reshape(1, C_D_INNER)

    zx_p, dtr_p = norm_matmul(xp, g, w_in, j, n_zx, tm=2 * TM_PROMPT, tn=1024, w_tail=w_dt, out_dtype=bf16)
    yn_p, h_p, conv_p = ssd_prompt(zx_p, dtr_p, conv_w, cb, dtb, alog, dsk, nw)

    zx_s, dtr_s = norm_matmul(xs, g, w_in, j, n_zx, tm=TM_SAMPLE, tn=1024, w_tail=w_dt)
    sel = (jnp.arange(LANES)[:, None] == jnp.arange(C_D_INNER)[None, :] // C_HEAD_DIM).astype(f32)
    yn_s, h_s, conv_s = ssd_sample(
        zx_s.reshape(DEC_SEQ, DEC_BATCH, -1), dtr_s.reshape(DEC_SEQ, DEC_BATCH, LANES),
        jnp.swapaxes(conv_state, 0, 1), h0.reshape(DEC_BATCH, C_D_INNER, C_D_STATE),
        conv_w, cb, dtb, alog, dsk, nw, sel)
    st_shape = (C_HEADS, C_HEAD_DIM, C_D_STATE)
    return (yn_p, yn_s.reshape(SAMPLE_ROWS, C_D_INNER), h_p.reshape((BATCH,) + st_shape), conv_p,
            h_s.reshape((DEC_BATCH,) + st_shape), jnp.swapaxes(conv_s, 0, 1))


def kernel(x_prompt, x_sample, cache_swa_k, cache_swa_v, state_ssm, state_conv, norm_mixer, norm_mlp, mlp_w_up, mlp_w_down, a_w_in, a_norm_v, a_w_spatial, a_b_spatial, a_w_out, b_w_qkv, b_q_norm, b_k_norm, b_sinks, rel_bias, b_w_out, c_w_in, c_conv_w, c_conv_b, c_dt_bias, c_a_log, c_d, c_norm, c_w_out):
    xp = x_prompt.reshape(PROMPT_ROWS, D_MODEL)
    xs = jnp.swapaxes(x_sample, 0, 1).reshape(SAMPLE_ROWS, D_MODEL)
    chunk_v_s = []
    swa_kp, swa_vp, swa_ks, swa_vs = [], [], [], []
    ssm_p, conv_p, ssm_s, conv_s = [], [], [], []
    c_w_in, b_w_out, c_w_out = c_w_in.astype(bf16), b_w_out.astype(bf16), c_w_out.astype(bf16)
    for i in range(DEPTH):
        kind = i % N_MIXERS
        j = i // N_MIXERS
        g = norm_mixer[i].reshape(1, D_MODEL)
        proj_p = proj_s = None
        if kind == 0:
            xp, xs, v_new = _mixer_a(xp, xs, g, j, a_w_in, a_norm_v[j], a_w_spatial[j], a_b_spatial[j], a_w_out)
            chunk_v_s.append(v_new)
        elif kind == 1:
            o_p, o_s, kp, vp, ks_, vs_ = _mixer_b(xp, xs, g, j, cache_swa_k[j], cache_swa_v[j], b_w_qkv, b_q_norm[j],
                                                  b_k_norm[j], b_sinks[j], rel_bias)
            proj_p, proj_s = (o_p, b_w_out, j), (o_s, b_w_out, j)
            swa_kp.append(kp); swa_vp.append(vp); swa_ks.append(ks_); swa_vs.append(vs_)
        else:
            y_p, y_s, hp, bp, hs, bs = _mixer_c(xp, xs, g, j, state_ssm[j], state_conv[j], c_w_in, c_conv_w[j],
                                                c_conv_b[j], c_dt_bias[j], c_a_log[j], c_d[j], c_norm[j])
            proj_p, proj_s = (y_p, c_w_out, j), (y_s, c_w_out, j)
            ssm_p.append(hp); conv_p.append(bp); ssm_s.append(hs); conv_s.append(bs)
        gm = norm_mlp[i].reshape(1, D_MODEL)
        xs, w_up_b, w_down_b = mlp(xs, gm, mlp_w_up, mlp_w_down, i, tm=TM_SAMPLE, tf=MLP_TF, proj=proj_s, emit=True)
        xp = mlp(xp, gm, w_up_b, w_down_b, 0, tm=TM_PROMPT, tf=MLP_TF, proj=proj_p)
    y_prompt = xp.reshape(BATCH, SEQ, D_MODEL)
    y_sample = jnp.swapaxes(xs.reshape(DEC_SEQ, DEC_BATCH, D_MODEL), 0, 1)
    return (y_prompt, y_sample, jnp.stack(chunk_v_s),
            jnp.stack(swa_kp), jnp.stack(swa_vp), jnp.stack(swa_ks), jnp.stack(swa_vs),
            jnp.stack(ssm_p), jnp.stack(conv_p), jnp.stack(ssm_s), jnp.stack(conv_s))
```

```python
import functools
import math

import jax
import jax.numpy as jnp
import numpy as np
from jax import lax
from jax.experimental import pallas as pl
from jax.experimental.pallas import tpu as pltpu

f32 = jnp.float32
bf16 = jnp.bfloat16

D_MODEL = 1024
BATCH = 4
SEQ = 4096
DEPTH = 4
DEC_BATCH = 128
DEC_SEQ = 4
PAST_LEN = 8192
N_MIXERS = 3
D_FF = 4 * D_MODEL
EPS = 1e-6
NEG_INF = -1e30

A_CHUNK = 128
A_D_FFN = 6 * D_MODEL
A_HALF = A_D_FFN // 2
A_GROUPS = 8
A_GROUP_W = A_HALF // A_GROUPS

B_HEADS = 16
B_KV_HEADS = 4
B_HEAD_DIM = 64
B_REP = B_HEADS // B_KV_HEADS
B_WINDOW = 128
B_BLOCK = 128
B_Q_DIM = B_HEADS * B_HEAD_DIM
B_KV_DIM = B_KV_HEADS * B_HEAD_DIM
N_BUCKETS = 32
MAX_DISTANCE = 128

C_D_INNER = 2 * D_MODEL
C_HEAD_DIM = 64
C_HEADS = C_D_INNER // C_HEAD_DIM
C_GROUPS = 4
C_REP = C_HEADS // C_GROUPS
C_D_STATE = 128
C_D_CONV = 4
C_BC_DIM = C_GROUPS * C_D_STATE
C_CONV_DIM = C_D_INNER + 2 * C_BC_DIM
C_GROUP_W = C_D_INNER // C_GROUPS
C_CHUNK = 128

LANES = 128
SUBLANES = 8
VMEM_LIMIT_BYTES = 56 * 1024 * 1024

PROMPT_ROWS = BATCH * SEQ
SAMPLE_ROWS = DEC_BATCH * DEC_SEQ
TM_PROMPT = 1024
TM_SAMPLE = SAMPLE_ROWS
MLP_TF = 1024


def _params(*sem):
    return pltpu.CompilerParams(dimension_semantics=sem, vmem_limit_bytes=VMEM_LIMIT_BYTES)


def _rms(x, g):
    ms = jnp.mean(x * x, axis=-1, keepdims=True)
    return x * lax.rsqrt(ms + EPS) * g


def _gelu(x):
    return 0.5 * x * (1.0 + lax.erf(x * math.sqrt(0.5)))


def _silu(x):
    return x * jax.nn.sigmoid(x)


def _softplus(x):
    return jnp.maximum(x, 0.0) + jnp.log1p(jnp.exp(-jnp.abs(x)))


def _dot(a, b):
    return jnp.dot(a, b, preferred_element_type=f32)


def _dot_nt(a, b):
    return lax.dot_general(a, b, (((1,), (1,)), ((), ())), preferred_element_type=f32)


def _dot_tn(a, b):
    return lax.dot_general(a, b, (((0,), (0,)), ((), ())), preferred_element_type=f32)


def _dot_exact_lhs01(a01, x):
    a = a01.astype(bf16)
    hi = x.astype(bf16)
    r1 = x - hi.astype(f32)
    mid = r1.astype(bf16)
    lo = (r1 - mid.astype(f32)).astype(bf16)
    return _dot(a, hi) + _dot(a, mid) + _dot(a, lo)


def _dot_exact_rhs01(x, b01):
    b = b01.astype(bf16)
    hi = x.astype(bf16)
    r1 = x - hi.astype(f32)
    mid = r1.astype(bf16)
    lo = (r1 - mid.astype(f32)).astype(bf16)
    return _dot(hi, b) + _dot(mid, b) + _dot(lo, b)


def _norm_matmul_kernel(*refs, nj, tail):
    if tail:
        x_ref, g_ref, w_ref, wt_ref, o_ref, ot_ref, xn_ref = refs
    else:
        x_ref, g_ref, w_ref, o_ref, xn_ref = refs
    j = pl.program_id(1)

    @pl.when(j == 0)
    def _():
        xn_ref[...] = _rms(x_ref[...], g_ref[...]).astype(bf16)

    @pl.when(j < nj)
    def _():
        o_ref[...] = _dot(xn_ref[...], w_ref[0].astype(bf16)).astype(o_ref.dtype)

    if tail:
        @pl.when(j == nj)
        def _():
            ot_ref[...] = _dot(xn_ref[...], wt_ref[...].astype(bf16))


def norm_matmul(x, g, w, layer, n, *, tm, tn, w_tail=None, out_dtype=f32):
    m, k = x.shape
    nj = n // tn
    tail = w_tail is not None
    last = nj - 1
    in_specs = [
        pl.BlockSpec((tm, k), lambda i, j: (i, 0)),
        pl.BlockSpec((1, k), lambda i, j: (0, 0)),
        pl.BlockSpec((1, k, tn), lambda i, j: (layer, 0, jnp.minimum(j, last))),
    ]
    out_specs = [pl.BlockSpec((tm, tn), lambda i, j: (i, jnp.minimum(j, last)))]
    out_shape = [jax.ShapeDtypeStruct((m, n), out_dtype)]
    args = [x, g, w]
    if tail:
        in_specs.append(pl.BlockSpec((k, LANES), lambda i, j: (0, 0)))
        out_specs.append(pl.BlockSpec((tm, LANES), lambda i, j: (i, 0)))
        out_shape.append(jax.ShapeDtypeStruct((m, LANES), f32))
        args.append(w_tail)
    out = pl.pallas_call(
        functools.partial(_norm_matmul_kernel, nj=nj, tail=tail),
        grid=(m // tm, nj + (1 if tail else 0)),
        in_specs=in_specs,
        out_specs=out_specs,
        out_shape=out_shape,
        scratch_shapes=[pltpu.VMEM((tm, k), bf16)],
        compiler_params=_params("parallel", "arbitrary"),
        name="norm_matmul",
    )(*args)
    return out if tail else out[0]


def _mlp_kernel(*refs, proj, emit):
    refs = list(refs)
    a_ref, wo_ref = (refs.pop(0), refs.pop(0)) if proj else (None, None)
    x_ref, g_ref, wu_ref, wd_ref, o_ref = refs[:5]
    wub_ref, wdb_ref = (refs[5], refs[6]) if emit else (None, None)
    xn_ref = refs[-1]

    @pl.when(pl.program_id(1) == 0)
    def _():
        x = x_ref[...]
        if proj:
            x = x + _dot(a_ref[...], wo_ref[0])
        xn_ref[...] = _rms(x, g_ref[...]).astype(bf16)
        o_ref[...] = x

    wu = wu_ref[0].astype(bf16)
    wd = wd_ref[0].astype(bf16)
    if emit:
        wub_ref[0] = wu
        wdb_ref[0] = wd
    h = jnp.maximum(_dot(xn_ref[...], wu), 0.0)
    o_ref[...] += _dot((h * h).astype(bf16), wd)


def mlp(x, g, w_up, w_down, layer, *, tm, tf, proj=None, emit=False):
    m, d = x.shape
    ff = w_up.shape[2]
    assert not emit or m == tm
    in_specs = [
        pl.BlockSpec((tm, d), lambda i, j: (i, 0)),
        pl.BlockSpec((1, d), lambda i, j: (0, 0)),
        pl.BlockSpec((1, d, tf), lambda i, j: (layer, 0, j)),
        pl.BlockSpec((1, tf, d), lambda i, j: (layer, j, 0)),
    ]
    args = [x, g, w_up, w_down]
    if proj is not None:
        a, w_o, lo = proj
        k = a.shape[1]
        in_specs = [pl.BlockSpec((tm, k), lambda i, j: (i, 0)),
                    pl.BlockSpec((1, k, d), lambda i, j: (lo, 0, 0), pipeline_mode=pl.Buffered(1))] + in_specs
        args = [a, w_o] + args
    out_specs = [pl.BlockSpec((tm, d), lambda i, j: (i, 0))]
    out_shape = [jax.ShapeDtypeStruct((m, d), f32)]
    if emit:
        out_specs += [pl.BlockSpec((1, d, tf), lambda i, j: (0, 0, j)), pl.BlockSpec((1, tf, d), lambda i, j: (0, j, 0))]
        out_shape += [jax.ShapeDtypeStruct((1, d, ff), bf16), jax.ShapeDtypeStruct((1, ff, d), bf16)]
    out = pl.pallas_call(
        functools.partial(_mlp_kernel, proj=proj is not None, emit=emit),
        grid=(m // tm, ff // tf),
        in_specs=in_specs,
        out_specs=out_specs,
        out_shape=out_shape,
        scratch_shapes=[pltpu.VMEM((tm, d), bf16)],
        compiler_params=_params("parallel", "arbitrary"),
        name="mlp",
    )(*args)
    return out if emit else out[0]


A_BLK_GROUPS = 2
A_BLK = A_BLK_GROUPS * A_GROUP_W
A_NBLK = A_HALF // A_BLK


def _mixer_a_kernel(*refs, sample):
    if sample:
        (ws_ref, bs_ref, x_ref, g_ref, win_ref, nv_ref, wout_ref, o_ref, vo_ref, winb_ref, woutb_ref,
         xn_ref, v_ref, ssq_ref, us_ref) = refs
    else:
        x_ref, g_ref, win_ref, nv_ref, ws_ref, bs_ref, wout_ref, o_ref, xn_ref, v_ref, ssq_ref, us_ref = refs
        winb_ref = woutb_ref = None
    j = pl.program_id(1)
    tm = x_ref.shape[0]

    def w_in_block():
        w = win_ref[0].astype(bf16)
        if winb_ref is not None:
            winb_ref[0] = w
        return w

    def w_out_block():
        w = wout_ref[0].astype(bf16)
        if woutb_ref is not None:
            woutb_ref[0] = w
        return w

    @pl.when(j == 0)
    def _():
        xn_ref[...] = _rms(x_ref[...], g_ref[...]).astype(bf16)
        ssq_ref[...] = jnp.zeros(ssq_ref.shape, f32)

    for k in range(A_NBLK):
        @pl.when(j == k)
        def _(k=k):
            v = _gelu(_dot(xn_ref[...], w_in_block()))
            v_ref[:, k * A_BLK:(k + 1) * A_BLK] = v
            ssq_ref[...] += jnp.sum(v * v, axis=-1, keepdims=True)

    if not sample:
        row = lax.broadcasted_iota(jnp.int32, (A_CHUNK, A_CHUNK), 0)
        col = lax.broadcasted_iota(jnp.int32, (A_CHUNK, A_CHUNK), 1)
        causal = row >= col

    for k in range(A_NBLK):
        @pl.when(j == A_NBLK + k)
        def _(k=k):
            u = _gelu(_dot(xn_ref[...], w_in_block()))
            rinv = lax.rsqrt(ssq_ref[...] * (1.0 / A_HALF) + EPS)
            for gg in range(A_BLK_GROUPS):
                g = k * A_BLK_GROUPS + gg
                cols = slice(g * A_GROUP_W, (g + 1) * A_GROUP_W)
                ucols = slice(gg * A_GROUP_W, (gg + 1) * A_GROUP_W)
                vn = v_ref[:, cols] * rinv * nv_ref[:, cols]
                if sample:
                    vo_ref[:, cols] = vn
                    vt = [vn[t * DEC_BATCH:(t + 1) * DEC_BATCH] for t in range(DEC_SEQ)]
                    s_rows = []
                    for t in range(DEC_SEQ):
                        s = ws_ref[(g * DEC_SEQ + t) * DEC_SEQ] * vt[0]
                        for t2 in range(1, t + 1):
                            s = s + ws_ref[(g * DEC_SEQ + t) * DEC_SEQ + t2] * vt[t2]
                        s_rows.append(s + bs_ref[g * DEC_SEQ + t])
                    s = jnp.concatenate(s_rows, axis=0)
                else:
                    w = jnp.where(causal, ws_ref[g], 0.0).astype(bf16)
                    bias = bs_ref[:, g:g + 1]
                    vb = vn.astype(bf16)
                    s = jnp.concatenate(
                        [_dot(w, vb[c * A_CHUNK:(c + 1) * A_CHUNK]) + bias for c in range(tm // A_CHUNK)], axis=0)
                us_ref[:, ucols] = (u[:, ucols] * s).astype(bf16)
            y = _dot(us_ref[...], w_out_block())
            if k == 0:
                o_ref[...] = x_ref[...] + y
            else:
                o_ref[...] += y


def mixer_a(x, g, w_in, norm_v, w_sp, b_sp, w_out, layer, *, tm, sample):
    m, d = x.shape
    nj = 2 * A_NBLK
    row = lambda w: pl.BlockSpec((tm, w), lambda i, j: (i, 0))
    full = lambda *shape: pl.BlockSpec(shape, lambda i, j: (0,) * len(shape))
    smem = pl.BlockSpec(memory_space=pltpu.SMEM)
    win_spec = pl.BlockSpec((1, d, A_BLK), lambda i, j: (layer, 0, (j + A_NBLK) % nj))
    wout_spec = pl.BlockSpec((1, A_BLK, d), lambda i, j: (layer, jnp.maximum(j - A_NBLK, 0), 0))
    if sample:
        in_specs = [smem, smem, row(d), full(1, d), win_spec, full(1, A_HALF), wout_spec]
        args = (w_sp, b_sp, x, g, w_in, norm_v, w_out)
        out_specs = [row(d), row(A_HALF),
                     pl.BlockSpec((1, d, A_BLK), lambda i, j: (0, 0, (j + A_NBLK) % nj)),
                     pl.BlockSpec((1, A_BLK, d), lambda i, j: (0, jnp.maximum(j - A_NBLK, 0), 0))]
        out_shape = [jax.ShapeDtypeStruct((m, d), f32), jax.ShapeDtypeStruct((m, A_HALF), f32),
                     jax.ShapeDtypeStruct((1, d, 2 * A_HALF), bf16), jax.ShapeDtypeStruct((1, A_HALF, d), bf16)]
    else:
        in_specs = [row(d), full(1, d), win_spec, full(1, A_HALF), full(A_GROUPS, A_CHUNK, A_CHUNK),
                    full(A_CHUNK, A_GROUPS), wout_spec]
        args = (x, g, w_in, norm_v, w_sp, b_sp, w_out)
        out_specs = row(d)
        out_shape = jax.ShapeDtypeStruct((m, d), f32)
    return pl.pallas_call(
        functools.partial(_mixer_a_kernel, sample=sample),
        grid=(m // tm, nj),
        in_specs=in_specs,
        out_specs=out_specs,
        out_shape=out_shape,
        scratch_shapes=[pltpu.VMEM((tm, d), bf16), pltpu.VMEM((tm, A_HALF), f32), pltpu.VMEM((tm, 1), f32),
                        pltpu.VMEM((tm, A_BLK), bf16)],
        compiler_params=_params("parallel", "arbitrary"),
        name="mixer_a_sample" if sample else "mixer_a_prompt",
    )(*args)


def _bucket_table():
    i = np.arange(B_BLOCK)[:, None]
    j = np.arange(2 * B_BLOCK)[None, :]
    n = np.maximum(B_BLOCK + i - j, 0)
    max_exact = N_BUCKETS // 2
    nf = np.maximum(n, 1).astype(np.float64)
    val = np.log(nf / max_exact) / math.log(MAX_DISTANCE / max_exact) * (N_BUCKETS - max_exact)
    in_window = (n >= max_exact) & (n < B_WINDOW)
    assert np.all(np.abs(val - np.round(val))[in_window & (n != max_exact)] > 1e-3)
    large = np.minimum(max_exact + np.floor(val + 1e-9).astype(np.int64), N_BUCKETS - 1)
    return np.where(n < max_exact, n, large).astype(np.int32)


def _bias_table_kernel(rb_ref, bk_ref, o_ref):
    bk = bk_ref[...]
    for h in range(B_HEADS):
        acc = jnp.zeros(bk.shape, f32)
        for b in range(N_BUCKETS):
            acc = jnp.where(bk == b, rb_ref[b * B_HEADS + h], acc)
        o_ref[h] = acc


def bias_table(rel_bias):
    return pl.pallas_call(
        _bias_table_kernel,
        in_specs=[pl.BlockSpec(memory_space=pltpu.SMEM), pl.BlockSpec(memory_space=pltpu.VMEM)],
        out_specs=pl.BlockSpec(memory_space=pltpu.VMEM),
        out_shape=jax.ShapeDtypeStruct((B_HEADS, B_BLOCK, 2 * B_BLOCK), f32),
        name="bias_table",
    )(rel_bias.reshape(-1), jnp.asarray(_bucket_table()))


def _softmax_with_sink(logits, sink):
    m = jnp.maximum(jnp.max(logits, axis=-1, keepdims=True), sink)
    p = jnp.exp(logits - m)
    return p, jnp.sum(p, axis=-1, keepdims=True) + jnp.exp(sink - m)


def _rms_head_pairs(x, g2, lo):
    sq = x * x
    s_lo = jnp.sum(jnp.where(lo, sq, 0.0), axis=-1, keepdims=True)
    s_hi = jnp.sum(jnp.where(lo, 0.0, sq), axis=-1, keepdims=True)
    r = lax.rsqrt(jnp.where(lo, s_lo, s_hi) * (1.0 / B_HEAD_DIM) + EPS)
    return x * r * g2


ATT_SUB = 8


def _attn_prompt_kernel(sink_ref, qkv_ref, tab_ref, qn_ref, kn_ref, o_ref, ko_ref, vo_ref,
                        kband_ref, vband_ref, q_ref, p_ref, rhs_ref):
    n = pl.program_id(1)
    T = B_BLOCK

    @pl.when(n == 0)
    def _():
        kband_ref[0:T, :] = jnp.zeros((T, B_KV_DIM), f32)
        vband_ref[0:T, :] = jnp.zeros((T, B_KV_DIM), f32)
        rhs_ref[...] = jnp.ones(rhs_ref.shape, bf16)

    for sub in range(ATT_SUB):
        rows = pl.ds(sub * T, T)
        first_key = jnp.where(n == 0, T, 0) if sub == 0 else 0
        _attn_block(first_key, sink_ref, qkv_ref.at[rows], tab_ref, qn_ref, kn_ref, o_ref.at[rows], ko_ref, vo_ref,
                    kband_ref, vband_ref, q_ref.at[sub], p_ref.at[sub], rhs_ref.at[sub])


def _attn_block(first_key, sink_ref, qkv_ref, tab_ref, qn_ref, kn_ref, o_ref, ko_ref, vo_ref,
                kband_ref, vband_ref, q_ref, p_ref, rhs_ref):
    T = B_BLOCK
    lo = lax.broadcasted_iota(jnp.int32, (1, LANES), 1) < B_HEAD_DIM
    qn2 = qn_ref[...]
    kn2 = kn_ref[...]
    for t in range(B_KV_DIM // LANES):
        lanes = slice(t * LANES, (t + 1) * LANES)
        k2 = _rms_head_pairs(qkv_ref[:, B_Q_DIM + t * LANES:B_Q_DIM + (t + 1) * LANES].astype(f32), kn2, lo)
        ko_ref[0, :, lanes] = k2
        kband_ref[T:, lanes] = k2
    v = qkv_ref[:, B_Q_DIM + B_KV_DIM:].astype(f32)
    vo_ref[0] = v
    vband_ref[T:, :] = v
    for t in range(B_Q_DIM // LANES):
        q2 = qkv_ref[:, t * LANES:(t + 1) * LANES].astype(f32)
        q_ref[t * T:(t + 1) * T, :] = _rms_head_pairs(q2, qn2, lo).astype(bf16)

    key_ops = {}
    band_row = lax.broadcasted_iota(jnp.int32, (2 * T, LANES), 0)
    for t in range(B_KV_DIM // LANES):
        lanes = slice(t * LANES, (t + 1) * LANES)
        kt = kband_ref[:, lanes]
        kr = pltpu.roll(kt, B_HEAD_DIM, axis=1)
        vt = jnp.where(band_row == 0, 0.0, vband_ref[:, lanes])
        vr = pltpu.roll(vt, B_HEAD_DIM, axis=1)
        hi = jnp.logical_not(lo)
        for half, (ksrc, vsrc) in enumerate(((kt, vt), (kr, vr))):
            g_lo, g_hi = (2 * t, 2 * t + 1) if half == 0 else (2 * t + 1, 2 * t)
            key_ops[(g_lo, 0)] = jnp.where(lo, ksrc, 0.0).astype(bf16)
            key_ops[(g_hi, 1)] = jnp.where(hi, ksrc, 0.0).astype(bf16)
            rhs_ref[g_lo * 2 + 0, :, 0:LANES] = jnp.where(lo, vsrc, 1.0).astype(bf16)
            rhs_ref[g_hi * 2 + 1, :, 0:LANES] = jnp.where(hi, vsrc, 1.0).astype(bf16)

    i = lax.broadcasted_iota(jnp.int32, (T, 2 * T), 0)
    j = lax.broadcasted_iota(jnp.int32, (T, 2 * T), 1)
    valid = (j > i) & (j <= i + B_WINDOW) & (j >= first_key)
    sink_col = j == 0
    lo_t = lax.broadcasted_iota(jnp.int32, (T, LANES), 1) < B_HEAD_DIM
    for g in range(B_KV_HEADS):
        q2 = q_ref[2 * g * T:(2 * g + 2) * T, :]
        res = []
        for half in range(2):
            logits = _dot_nt(q2, key_ops[(g, half)]) * (B_HEAD_DIM ** -0.5)
            for pair in range(2):
                h = g * B_REP + 2 * pair + half
                l = jnp.where(valid, logits[pair * T:(pair + 1) * T] + tab_ref[h], NEG_INF)
                l = jnp.where(sink_col, sink_ref[h], l)
                p = jnp.exp(l - jnp.max(l, axis=-1, keepdims=True))
                p_ref[g * 2 + half, pair * T:(pair + 1) * T, :] = p.astype(bf16)
            res.append(_dot(p_ref[g * 2 + half], rhs_ref[g * 2 + half]))
        for pair in range(2):
            rows = slice(pair * T, (pair + 1) * T)
            even = res[0][rows, 0:LANES] / res[0][rows, LANES:]
            odd = res[1][rows, 0:LANES] / res[1][rows, LANES:]
            t = 2 * g + pair
            o_ref[:, t * LANES:(t + 1) * LANES] = jnp.where(lo_t, even, odd).astype(bf16)
    kband_ref[0:T, :] = kband_ref[T:, :]
    vband_ref[0:T, :] = vband_ref[T:, :]


def attn_prompt(qkv, table, q_norm, k_norm, sinks):
    step = ATT_SUB * B_BLOCK
    nb = SEQ // step
    return pl.pallas_call(
        _attn_prompt_kernel,
        grid=(BATCH, nb),
        in_specs=[
            pl.BlockSpec(memory_space=pltpu.SMEM),
            pl.BlockSpec((step, B_Q_DIM + 2 * B_KV_DIM), lambda b, n: (b * nb + n, 0)),
            pl.BlockSpec((B_HEADS, B_BLOCK, 2 * B_BLOCK), lambda b, n: (0, 0, 0)),
            pl.BlockSpec((1, LANES), lambda b, n: (0, 0)),
            pl.BlockSpec((1, LANES), lambda b, n: (0, 0)),
        ],
        out_specs=[
            pl.BlockSpec((step, B_Q_DIM), lambda b, n: (b * nb + n, 0)),
            pl.BlockSpec((1, B_BLOCK, B_KV_DIM), lambda b, n: (b, 0, 0)),
            pl.BlockSpec((1, B_BLOCK, B_KV_DIM), lambda b, n: (b, 0, 0)),
        ],
        out_shape=[
            jax.ShapeDtypeStruct((PROMPT_ROWS, B_Q_DIM), bf16),
            jax.ShapeDtypeStruct((BATCH, B_BLOCK, B_KV_DIM), f32),
            jax.ShapeDtypeStruct((BATCH, B_BLOCK, B_KV_DIM), f32),
        ],
        scratch_shapes=[
            pltpu.VMEM((2 * B_BLOCK, B_KV_DIM), f32),
            pltpu.VMEM((2 * B_BLOCK, B_KV_DIM), f32),
            pltpu.VMEM((ATT_SUB, B_Q_DIM // LANES * B_BLOCK, LANES), bf16),
            pltpu.VMEM((ATT_SUB, 2 * B_KV_HEADS, 2 * B_BLOCK, 2 * B_BLOCK), bf16),
            pltpu.VMEM((ATT_SUB, 2 * B_KV_HEADS, 2 * B_BLOCK, 2 * LANES), bf16),
        ],
        compiler_params=_params("parallel", "arbitrary"),
        name="attn_prompt",
    )(sinks, qkv, table, jnp.tile(q_norm, (1, LANES // B_HEAD_DIM)), jnp.tile(k_norm, (1, LANES // B_HEAD_DIM)))


ATT_BB = 8
ATT_QR = B_REP * DEC_SEQ
ATT_KEYS = B_WINDOW + 2 * DEC_SEQ


def _attn_sample_kernel(sink_ref, q_ref, kn_ref, vn_ref, kc_ref, vc_ref, tab_ref, qn_ref, knm_ref,
                        o_ref, ko_ref, vo_ref, kall_ref, vall_ref):
    qn2 = qn_ref[...]
    kn2 = knm_ref[...]
    n_rows = B_KV_HEADS * ATT_QR
    lo = lax.broadcasted_iota(jnp.int32, (1, LANES), 1) < B_HEAD_DIM
    row = lax.broadcasted_iota(jnp.int32, (n_rows, ATT_KEYS), 0)
    j = lax.broadcasted_iota(jnp.int32, (n_rows, ATT_KEYS), 1)
    t = row % DEC_SEQ
    valid = (j > t) & (j <= t + B_WINDOW)
    q_row_group = lax.broadcasted_iota(jnp.int32, (n_rows, B_KV_DIM), 0) // ATT_QR
    q_lane_group = lax.broadcasted_iota(jnp.int32, (n_rows, B_KV_DIM), 1) // B_HEAD_DIM
    own_group = q_row_group == q_lane_group
    o_lane_group = lax.broadcasted_iota(jnp.int32, (ATT_QR, B_KV_DIM), 1) // B_HEAD_DIM
    pad = jnp.zeros((ATT_KEYS - B_WINDOW - DEC_SEQ, B_KV_DIM), f32)
    bias = tab_ref[...]
    sink = sink_ref[...]
    for s in range(ATT_BB):
        kc = kc_ref[s]
        vc = vc_ref[s]
        k_new = jnp.concatenate(
            [_rms_head_pairs(kn_ref[s, :, tt * LANES:(tt + 1) * LANES], kn2, lo) for tt in range(B_KV_DIM // LANES)],
            axis=1)
        v_new = vn_ref[s]
        ko_ref[s, 0:B_WINDOW - DEC_SEQ, :] = kc[DEC_SEQ:, :]
        ko_ref[s, B_WINDOW - DEC_SEQ:, :] = k_new
        vo_ref[s, 0:B_WINDOW - DEC_SEQ, :] = vc[DEC_SEQ:, :]
        vo_ref[s, B_WINDOW - DEC_SEQ:, :] = v_new
        kall_ref[s, 0:B_WINDOW, :] = kc
        kall_ref[s, B_WINDOW:B_WINDOW + DEC_SEQ, :] = k_new
        kall_ref[s, B_WINDOW + DEC_SEQ:, :] = pad
        vall_ref[s, 0:B_WINDOW, :] = vc
        vall_ref[s, B_WINDOW:B_WINDOW + DEC_SEQ, :] = v_new
        vall_ref[s, B_WINDOW + DEC_SEQ:, :] = pad
        q = q_ref[s]
        qn = jnp.concatenate(
            [_rms_head_pairs(q[:, tt * LANES:(tt + 1) * LANES], qn2, lo) for tt in range(B_KV_DIM // LANES)], axis=1)
        q_all = jnp.where(own_group, jnp.concatenate([qn] * B_KV_HEADS, axis=0), 0.0).astype(bf16)
        logits = _dot_nt(q_all, kall_ref[s].astype(bf16)) * (B_HEAD_DIM ** -0.5)
        logits = jnp.where(valid, logits + bias, NEG_INF)
        p, denom = _softmax_with_sink(logits, sink)
        res = _dot(p.astype(bf16), vall_ref[s].astype(bf16)) / denom
        out = jnp.zeros((ATT_QR, B_KV_DIM), f32)
        for g in range(B_KV_HEADS):
            out = jnp.where(o_lane_group == g, res[g * ATT_QR:(g + 1) * ATT_QR, :], out)
        o_ref[s] = out


def attn_sample(q_s, k_new, v_new, k_cache, v_cache, table_s, q_norm, k_norm, sinks):
    blk = lambda *shape: pl.BlockSpec((ATT_BB,) + shape, lambda i: (i,) + (0,) * len(shape))
    full = lambda *shape: pl.BlockSpec(shape, lambda i: (0,) * len(shape))
    return pl.pallas_call(
        _attn_sample_kernel,
        grid=(DEC_BATCH // ATT_BB,),
        in_specs=[
            full(B_KV_HEADS * ATT_QR, 1),
            blk(ATT_QR, B_KV_DIM), blk(DEC_SEQ, B_KV_DIM), blk(DEC_SEQ, B_KV_DIM),
            blk(B_WINDOW, B_KV_DIM), blk(B_WINDOW, B_KV_DIM),
            full(B_KV_HEADS * ATT_QR, ATT_KEYS), full(1, LANES), full(1, LANES),
        ],
        out_specs=[blk(ATT_QR, B_KV_DIM), blk(B_WINDOW, B_KV_DIM), blk(B_WINDOW, B_KV_DIM)],
        out_shape=[
            jax.ShapeDtypeStruct((DEC_BATCH, ATT_QR, B_KV_DIM), f32),
            jax.ShapeDtypeStruct((DEC_BATCH, B_WINDOW, B_KV_DIM), f32),
            jax.ShapeDtypeStruct((DEC_BATCH, B_WINDOW, B_KV_DIM), f32),
        ],
        scratch_shapes=[pltpu.VMEM((ATT_BB, ATT_KEYS, B_KV_DIM), f32), pltpu.VMEM((ATT_BB, ATT_KEYS, B_KV_DIM), f32)],
        compiler_params=_params("parallel"),
        name="attn_sample",
    )(jnp.repeat(sinks, DEC_SEQ).reshape(B_KV_HEADS * ATT_QR, 1), q_s, k_new, v_new, k_cache, v_cache,
      table_s.reshape(B_KV_HEADS * ATT_QR, ATT_KEYS),
      jnp.tile(q_norm, (1, LANES // B_HEAD_DIM)), jnp.tile(k_norm, (1, LANES // B_HEAD_DIM)))


CONV_PAD = SUBLANES

def _gated_group_norm(y, z, norm_w):
    gt = y * _silu(z)
    parts = []
    for g in range(C_GROUPS):
        gg = gt[:, g * C_GROUP_W:(g + 1) * C_GROUP_W]
        parts.append(gg * lax.rsqrt(jnp.mean(gg * gg, axis=-1, keepdims=True) + EPS))
    return jnp.concatenate(parts, axis=1) * norm_w


LOG2E = math.log2(math.e)


def _expand_heads(v, sel3):
    lane = lax.broadcasted_iota(jnp.int32, (1, LANES), 1)
    v = jnp.where(lane < C_HEADS, v, 0.0)
    hi = v.astype(bf16).astype(f32)
    r1 = v - hi
    mid = r1.astype(bf16).astype(f32)
    lo = r1 - mid
    packed = hi + pltpu.roll(mid, C_HEADS, axis=1) + pltpu.roll(lo, 2 * C_HEADS, axis=1)
    return _dot(packed.astype(bf16), sel3)


SSD_SUB = 4


def _ssd_prompt_kernel(x_ref, g_ref, w_ref, wdt_ref, cw_ref, cb_ref, dtb_ref, alog_ref, dsk_ref, nw_ref, sel_ref,
                       yn_ref, hfin_ref, cout_ref, xn_ref, xpad_ref, ht_ref, y_ref):
    c = pl.program_id(1)
    n_zx = C_D_INNER + C_CONV_DIM

    @pl.when(c == 0)
    def _():
        xpad_ref[0, 0:CONV_PAD, :] = jnp.zeros((CONV_PAD, C_CONV_DIM), f32)
        ht_ref[...] = jnp.zeros(ht_ref.shape, f32)

    xn_ref[...] = _rms(x_ref[...], g_ref[...]).astype(bf16)
    for sub in range(SSD_SUB):
        rows = pl.ds(sub * C_CHUNK, C_CHUNK)
        xn = xn_ref[rows, :]
        zx = _dot(xn, w_ref[0, :, 0:n_zx])
        dtr = _dot(xn, wdt_ref[...])
        _ssd_chunk(zx[:, :C_D_INNER], zx[:, C_D_INNER:], dtr, cw_ref, cb_ref, dtb_ref, alog_ref, dsk_ref, nw_ref,
                   sel_ref, yn_ref.at[rows], cout_ref, xpad_ref.at[sub], xpad_ref.at[(sub + 1) % SSD_SUB], ht_ref,
                   y_ref.at[sub])

    @pl.when(c == pl.num_programs(1) - 1)
    def _():
        for t in range(C_D_INNER // LANES):
            hfin_ref[0, t * LANES:(t + 1) * LANES, :] = ht_ref[:, t * LANES:(t + 1) * LANES].T


def _ssd_chunk(z, xbc, dtr, cw_ref, cb_ref, dtb_ref, alog_ref, dsk_ref, nw_ref, sel_ref,
               yn_ref, cout_ref, xpad_ref, xpad_next_ref, ht_ref, y_ref):
    T = C_CHUNK
    xpad_ref[CONV_PAD:, :] = xbc
    xp = xpad_ref[...]
    cw = cw_ref[...]
    acc = cb_ref[...]
    for tap in range(C_D_CONV - 1):
        shifted = pltpu.roll(xp, C_D_CONV - 1 - tap, axis=0)[CONV_PAD:, :]
        acc = acc + shifted * cw[tap:tap + 1, :]
    acc = acc + xbc * cw[C_D_CONV - 1:C_D_CONV, :]
    xpad_next_ref[0:CONV_PAD, :] = xbc[T - CONV_PAD:, :]
    cout_ref[0] = xbc[T - (C_D_CONV - 1):, :]
    act = _silu(acc)
    xs = act[:, :C_D_INNER]
    bm = act[:, C_D_INNER:C_D_INNER + C_BC_DIM]
    cm = act[:, C_D_INNER + C_BC_DIM:]
    xb = xs.astype(bf16)

    dt = _softplus(dtr + dtb_ref[...])
    a_neg = -jnp.exp(alog_ref[...])
    row = lax.broadcasted_iota(jnp.int32, (T, T), 0)
    col = lax.broadcasted_iota(jnp.int32, (T, T), 1)
    causal = row >= col
    acs = _dot_exact_lhs01(causal.astype(f32), dt * a_neg)
    a2 = acs * LOG2E
    sel3 = sel_ref[...]
    e_exp = jnp.exp2(_expand_heads(a2, sel3))
    w_exp = _expand_heads(jnp.exp(acs[T - 1:T, :] - acs) * dt, sel3)
    cdec = e_exp[T - 1:T, :]
    b2_t = a2.T - jnp.log2(dt.T)
    xw = (xs * w_exp).astype(bf16)
    hb = ht_ref[...].astype(bf16)
    dsk = dsk_ref[...]
    lo_t = lax.broadcasted_iota(jnp.int32, (T, LANES), 1) < C_HEAD_DIM

    for g in range(C_GROUPS):
        ns = slice(g * C_D_STATE, (g + 1) * C_D_STATE)
        gs = slice(g * C_GROUP_W, (g + 1) * C_GROUP_W)
        b_g = bm[:, ns]
        c_g = cm[:, ns].astype(bf16)
        cb = _dot_nt(c_g, b_g.astype(bf16))
        yi = _dot(c_g, hb[:, gs])
        ht_ref[:, gs] = ht_ref[:, gs] * cdec[:, gs] + _dot(b_g.T.astype(bf16), xw[:, gs])
        for tt in range(C_GROUP_W // LANES):
            t = g * (C_GROUP_W // LANES) + tt
            lanes = slice(t * LANES, (t + 1) * LANES)
            xt = xb[:, lanes]
            res = []
            for half in range(2):
                h = 2 * t + half
                a_col = jnp.broadcast_to(a2[:, h:h + 1], (T, T))
                b_row = jnp.broadcast_to(b2_t[h:h + 1, :], (T, T))
                w = jnp.where(causal, cb * jnp.exp2(a_col - b_row), 0.0)
                res.append(_dot(w.astype(bf16), xt))
            y_intra = jnp.where(lo_t, res[0], res[1])
            y_ref[:, lanes] = y_intra + e_exp[:, lanes] * yi[:, tt * LANES:(tt + 1) * LANES] + dsk[:, lanes] * xs[:, lanes]

    yn_ref[...] = _gated_group_norm(y_ref[...], z, nw_ref[...]).astype(bf16)


def _head_select3():
    k = np.arange(LANES)[:, None]
    ch = np.arange(C_D_INNER)[None, :] // C_HEAD_DIM
    return jnp.asarray((k % C_HEADS == ch) & (k < 3 * C_HEADS), dtype=bf16)


def ssd_prompt(x, g, w_in, layer, w_dt, conv_w, conv_b, dt_bias, a_log, d_skip, norm_w):
    step = SSD_SUB * C_CHUNK
    nc = SEQ // step
    full = lambda *shape: pl.BlockSpec(shape, lambda b, c: (0,) * len(shape))
    once = lambda *shape: pl.BlockSpec(shape, lambda b, c: (0,) * len(shape), pipeline_mode=pl.Buffered(1))
    return pl.pallas_call(
        _ssd_prompt_kernel,
        grid=(BATCH, nc),
        in_specs=[
            pl.BlockSpec((step, D_MODEL), lambda b, c: (b * nc + c, 0)),
            full(1, D_MODEL),
            pl.BlockSpec((1,) + w_in.shape[1:], lambda b, c: (layer, 0, 0), pipeline_mode=pl.Buffered(1)),
            once(D_MODEL, LANES),
            full(C_D_CONV, C_CONV_DIM), full(1, C_CONV_DIM), full(1, LANES), full(1, LANES),
            full(1, C_D_INNER), full(1, C_D_INNER), full(LANES, C_D_INNER),
        ],
        out_specs=[
            pl.BlockSpec((step, C_D_INNER), lambda b, c: (b * nc + c, 0)),
            pl.BlockSpec((1, C_D_INNER, C_D_STATE), lambda b, c: (b, 0, 0)),
            pl.BlockSpec((1, C_D_CONV - 1, C_CONV_DIM), lambda b, c: (b, 0, 0)),
        ],
        out_shape=[
            jax.ShapeDtypeStruct((PROMPT_ROWS, C_D_INNER), bf16),
            jax.ShapeDtypeStruct((BATCH, C_D_INNER, C_D_STATE), f32),
            jax.ShapeDtypeStruct((BATCH, C_D_CONV - 1, C_CONV_DIM), f32),
        ],
        scratch_shapes=[
            pltpu.VMEM((step, D_MODEL), bf16),
            pltpu.VMEM((SSD_SUB, CONV_PAD + C_CHUNK, C_CONV_DIM), f32),
            pltpu.VMEM((C_D_STATE, C_D_INNER), f32),
            pltpu.VMEM((SSD_SUB, C_CHUNK, C_D_INNER), f32),
        ],
        compiler_params=_params("parallel", "arbitrary"),
        name="ssd_prompt",
    )(x, g, w_in, w_dt, conv_w, conv_b, dt_bias, a_log, d_skip, norm_w, _head_select3())


SSD_BB = 8
SSD_TP = SUBLANES
_N_PAIRS = DEC_SEQ * (DEC_SEQ + 1) // 2
_N_COEF = _N_PAIRS + 2 * DEC_SEQ


def _ssd_sample_kernel(zx_ref, dtr_ref, cs_ref, h0_ref, cw_ref, cb_ref, dtb_ref, alog_ref, dsk_ref, nw_ref,
                       sel_ref, yn_ref, hn_ref, cout_ref, c_scr, b_scr, xw_scr, yi_scr, cd_scr):
    L = DEC_SEQ
    cw = cw_ref[...]
    xp = [cs_ref[k] for k in range(C_D_CONV - 1)] + [zx_ref[t, :, C_D_INNER:] for t in range(L)]
    for k in range(C_D_CONV - 1):
        cout_ref[k] = xp[L + k]
    act = []
    for t in range(L):
        acc = cb_ref[...]
        for tap in range(C_D_CONV):
            acc = acc + xp[t + tap] * cw[tap:tap + 1, :]
        act.append(_silu(acc))
    xs = [a[:, :C_D_INNER] for a in act]
    bm = [a[:, C_D_INNER:C_D_INNER + C_BC_DIM] for a in act]
    cm = [a[:, C_D_INNER + C_BC_DIM:] for a in act]

    a_neg = -jnp.exp(alog_ref[...])
    dt = [_softplus(dtr_ref[t] + dtb_ref[...]) for t in range(L)]
    acs = []
    for t in range(L):
        acs.append(dt[t] * a_neg if t == 0 else acs[t - 1] + dt[t] * a_neg)

    lane_group = lax.broadcasted_iota(jnp.int32, (SSD_BB, LANES), 1) // C_REP
    coefs = []
    for t in range(L):
        for t2 in range(t + 1):
            cbh = jnp.zeros((SSD_BB, LANES), f32)
            for g in range(C_GROUPS):
                ns = slice(g * C_D_STATE, (g + 1) * C_D_STATE)
                cbg = jnp.sum(cm[t][:, ns] * bm[t2][:, ns], axis=-1, keepdims=True)
                cbh = jnp.where(lane_group == g, cbg, cbh)
            coefs.append(cbh * jnp.exp(acs[t] - acs[t2]) * dt[t2])
    for t in range(L):
        coefs.append(jnp.exp(acs[t]))
    for t in range(L):
        coefs.append(jnp.exp(acs[L - 1] - acs[t]) * dt[t])
    coef = jnp.concatenate(coefs, axis=0)
    cexp = _dot_exact_rhs01(coef, sel_ref[...])
    cexp = [cexp[k * SSD_BB:(k + 1) * SSD_BB, :] for k in range(_N_COEF)]
    w_intra = cexp[:_N_PAIRS]
    w_inter = cexp[_N_PAIRS:_N_PAIRS + L]
    w_state = cexp[_N_PAIRS + L:]

    cd = jnp.concatenate([jnp.exp(acs[L - 1]), jnp.zeros((LANES - SSD_BB, LANES), f32)], axis=0)
    cd_t = cd.T
    for s in range(SSD_BB):
        cd_scr[s] = jnp.broadcast_to(cd_t[0:C_HEADS, s:s + 1], (C_HEADS, C_D_STATE))

    zeros_tail = jnp.zeros((SSD_BB, SSD_TP - L, C_D_INNER), f32)
    c_scr[:, L:, :] = zeros_tail[:, :, :C_BC_DIM]
    b_scr[:, L:, :] = zeros_tail[:, :, :C_BC_DIM]
    xw_scr[:, L:, :] = zeros_tail
    for t in range(L):
        xw_t = xs[t] * w_state[t]
        for s in range(SSD_BB):
            c_scr[s, t:t + 1, :] = cm[t][s:s + 1, :]
            b_scr[s, t:t + 1, :] = bm[t][s:s + 1, :]
            xw_scr[s, t:t + 1, :] = xw_t[s:s + 1, :]

    for s in range(SSD_BB):
        for g in range(C_GROUPS):
            ns = slice(g * C_D_STATE, (g + 1) * C_D_STATE)
            gs = slice(g * C_GROUP_W, (g + 1) * C_GROUP_W)
            h0 = h0_ref[s, gs, :]
            yi = _dot_nt(c_scr[s, :, ns].astype(bf16), h0.astype(bf16))
            for t in range(L):
                yi_scr[t, s:s + 1, gs] = yi[t:t + 1, :]
            st = _dot_tn(xw_scr[s, :, gs].astype(bf16), b_scr[s, :, ns].astype(bf16))
            for r in range(C_REP):
                h = g * C_REP + r
                rs = slice(r * C_HEAD_DIM, (r + 1) * C_HEAD_DIM)
                scale = cd_scr[s, h:h + 1, :]
                hn_ref[s, h * C_HEAD_DIM:(h + 1) * C_HEAD_DIM, :] = h0[rs, :] * scale + st[rs, :]

    dsk = dsk_ref[...]
    nw = nw_ref[...]
    pair = 0
    for t in range(L):
        y = w_inter[t] * yi_scr[t] + dsk * xs[t]
        for t2 in range(t + 1):
            y = y + w_intra[pair] * xs[t2]
            pair += 1
        yn_ref[t] = _gated_group_norm(y, zx_ref[t, :, :C_D_INNER], nw).astype(bf16)


def ssd_sample(zx_t, dtr_t, conv_state_t, h0, conv_w, conv_b, dt_bias, a_log, d_skip, norm_w, sel):
    tmaj = lambda n, w: pl.BlockSpec((n, SSD_BB, w), lambda i: (0, i, 0))
    full = lambda *shape: pl.BlockSpec(shape, lambda i: (0,) * len(shape))
    return pl.pallas_call(
        _ssd_sample_kernel,
        grid=(DEC_BATCH // SSD_BB,),
        in_specs=[
            tmaj(DEC_SEQ, C_D_INNER + C_CONV_DIM), tmaj(DEC_SEQ, LANES), tmaj(C_D_CONV - 1, C_CONV_DIM),
            pl.BlockSpec((SSD_BB, C_D_INNER, C_D_STATE), lambda i: (i, 0, 0)),
            full(C_D_CONV, C_CONV_DIM), full(1, C_CONV_DIM), full(1, LANES), full(1, LANES),
            full(1, C_D_INNER), full(1, C_D_INNER), full(LANES, C_D_INNER),
        ],
        out_specs=[
            tmaj(DEC_SEQ, C_D_INNER),
            pl.BlockSpec((SSD_BB, C_D_INNER, C_D_STATE), lambda i: (i, 0, 0)),
            tmaj(C_D_CONV - 1, C_CONV_DIM),
        ],
        out_shape=[
            jax.ShapeDtypeStruct((DEC_SEQ, DEC_BATCH, C_D_INNER), bf16),
            jax.ShapeDtypeStruct((DEC_BATCH, C_D_INNER, C_D_STATE), f32),
            jax.ShapeDtypeStruct((C_D_CONV - 1, DEC_BATCH, C_CONV_DIM), f32),
        ],
        scratch_shapes=[
            pltpu.VMEM((SSD_BB, SSD_TP, C_BC_DIM), f32),
            pltpu.VMEM((SSD_BB, SSD_TP, C_BC_DIM), f32),
            pltpu.VMEM((SSD_BB, SSD_TP, C_D_INNER), f32),
            pltpu.VMEM((DEC_SEQ, SSD_BB, C_D_INNER), f32),
            pltpu.VMEM((SSD_BB, C_HEADS, C_D_STATE), f32),
        ],
        compiler_params=_params("parallel"),
        name="ssd_sample",
    )(zx_t, dtr_t, conv_state_t, h0, conv_w, conv_b, dt_bias, a_log, d_skip, norm_w, sel)


def _pad_lanes(v):
    return jnp.pad(v.astype(f32), (0, LANES - v.shape[0])).reshape(1, LANES)


def _mixer_a(xp, xs, g, j, w_in, norm_v, w_sp, b_sp, w_out):
    nv = norm_v.reshape(1, A_HALF)
    xs, v_s, w_in_b, w_out_b = mixer_a(xs, g, w_in, nv, w_sp[:, :DEC_SEQ, :DEC_SEQ].reshape(-1),
                                       b_sp[:, :DEC_SEQ].reshape(-1), w_out, j, tm=TM_SAMPLE, sample=True)
    xp = mixer_a(xp, g, w_in_b, nv, w_sp, b_sp.T, w_out_b, 0, tm=TM_PROMPT, sample=False)
    return xp, xs, jnp.swapaxes(v_s.reshape(DEC_SEQ, DEC_BATCH, A_HALF), 0, 1)


def _mixer_b(xp, xs, g, j, k_cache, v_cache, w_qkv, q_norm, k_norm, sinks, rel_bias):
    qn = q_norm.reshape(1, B_HEAD_DIM)
    kn = k_norm.reshape(1, B_HEAD_DIM)
    table = bias_table(rel_bias)
    n_qkv = B_Q_DIM + 2 * B_KV_DIM

    qkv_p = norm_matmul(xp, g, w_qkv, j, n_qkv, tm=2 * TM_PROMPT, tn=n_qkv, out_dtype=bf16)
    o_p, k_p, v_p = attn_prompt(qkv_p, table, qn, kn, sinks)

    qkv_s = norm_matmul(xs, g, w_qkv, j, n_qkv, tm=TM_SAMPLE, tn=512).reshape(DEC_SEQ, DEC_BATCH, -1)
    q_s = qkv_s[:, :, :B_Q_DIM].reshape(DEC_SEQ, DEC_BATCH, B_KV_HEADS, B_REP, B_HEAD_DIM)
    q_s = q_s.transpose(1, 3, 0, 2, 4).reshape(DEC_BATCH, ATT_QR, B_KV_DIM)
    k_new = jnp.swapaxes(qkv_s[:, :, B_Q_DIM:B_Q_DIM + B_KV_DIM], 0, 1)
    v_new = jnp.swapaxes(qkv_s[:, :, B_Q_DIM + B_KV_DIM:], 0, 1)
    table_s = table[:, :DEC_SEQ, :ATT_KEYS].reshape(B_KV_HEADS, ATT_QR, ATT_KEYS)
    o_s, k_s, v_s = attn_sample(q_s, k_new, v_new,
                                k_cache.reshape(DEC_BATCH, B_WINDOW, B_KV_DIM),
                                v_cache.reshape(DEC_BATCH, B_WINDOW, B_KV_DIM),
                                table_s, qn, kn, sinks)
    o_s = o_s.reshape(DEC_BATCH, B_REP, DEC_SEQ, B_KV_HEADS, B_HEAD_DIM).transpose(2, 0, 3, 1, 4)
    o_s = o_s.reshape(SAMPLE_ROWS, B_Q_DIM).astype(bf16)
    kv_shape_p = (BATCH, B_WINDOW, B_KV_HEADS, B_HEAD_DIM)
    kv_shape_s = (DEC_BATCH, B_WINDOW, B_KV_HEADS, B_HEAD_DIM)
    return o_p, o_s, k_p.reshape(kv_shape_p), v_p.reshape(kv_shape_p), k_s.reshape(kv_shape_s), v_s.reshape(kv_shape_s)


def _mixer_c(xp, xs, g, j, h0, conv_state, w_in, conv_w, conv_b, dt_bias, a_log, d_skip, norm_w):
    n_zx = C_D_INNER + C_CONV_DIM
    w_dt = jnp.pad(w_in[j, :, n_zx:], ((0, 0), (0, LANES - C_HEADS)))
    cb = conv_b.reshape(1, C_CONV_DIM)
    dtb = _pad_lanes(dt_bias)
    alog = _pad_lanes(a_log)
    dsk = jnp.repeat(d_skip.astype(f32), C_HEAD_DIM).reshape(1, C_D_INNER)
    nw = norm_w.reshape(1, C_D_INNER)

    yn_p, h_p, conv_p = ssd_prompt(xp, g, w_in, j, w_dt, conv_w, cb, dtb, alog, dsk, nw)

    zx_s, dtr_s = norm_matmul(xs, g, w_in, j, n_zx, tm=TM_SAMPLE, tn=1024, w_tail=w_dt)
    sel = (jnp.arange(LANES)[:, None] == jnp.arange(C_D_INNER)[None, :] // C_HEAD_DIM).astype(f32)
    yn_s, h_s, conv_s = ssd_sample(
        zx_s.reshape(DEC_SEQ, DEC_BATCH, -1), dtr_s.reshape(DEC_SEQ, DEC_BATCH, LANES),
        jnp.swapaxes(conv_state, 0, 1), h0.reshape(DEC_BATCH, C_D_INNER, C_D_STATE),
        conv_w, cb, dtb, alog, dsk, nw, sel)
    st_shape = (C_HEADS, C_HEAD_DIM, C_D_STATE)
    return (yn_p, yn_s.reshape(SAMPLE_ROWS, C_D_INNER), h_p.reshape((BATCH,) + st_shape), conv_p,
            h_s.reshape((DEC_BATCH,) + st_shape), jnp.swapaxes(conv_s, 0, 1))


def kernel(x_prompt, x_sample, cache_swa_k, cache_swa_v, state_ssm, state_conv, norm_mixer, norm_mlp, mlp_w_up, mlp_w_down, a_w_in, a_norm_v, a_w_spatial, a_b_spatial, a_w_out, b_w_qkv, b_q_norm, b_k_norm, b_sinks, rel_bias, b_w_out, c_w_in, c_conv_w, c_conv_b, c_dt_bias, c_a_log, c_d, c_norm, c_w_out):
    xp = x_prompt.reshape(PROMPT_ROWS, D_MODEL)
    xs = jnp.swapaxes(x_sample, 0, 1).reshape(SAMPLE_ROWS, D_MODEL)
    chunk_v_s = []
    swa_kp, swa_vp, swa_ks, swa_vs = [], [], [], []
    ssm_p, conv_p, ssm_s, conv_s = [], [], [], []
    c_w_in, b_w_out, c_w_out = c_w_in.astype(bf16), b_w_out.astype(bf16), c_w_out.astype(bf16)
    for i in range(DEPTH):
        kind = i % N_MIXERS
        j = i // N_MIXERS
        g = norm_mixer[i].reshape(1, D_MODEL)
        proj_p = proj_s = None
        if kind == 0:
            xp, xs, v_new = _mixer_a(xp, xs, g, j, a_w_in, a_norm_v[j], a_w_spatial[j], a_b_spatial[j], a_w_out)
            chunk_v_s.append(v_new)
        elif kind == 1:
            o_p, o_s, kp, vp, ks_, vs_ = _mixer_b(xp, xs, g, j, cache_swa_k[j], cache_swa_v[j], b_w_qkv, b_q_norm[j],
                                                  b_k_norm[j], b_sinks[j], rel_bias)
            proj_p, proj_s = (o_p, b_w_out, j), (o_s, b_w_out, j)
            swa_kp.append(kp); swa_vp.append(vp); swa_ks.append(ks_); swa_vs.append(vs_)
        else:
            y_p, y_s, hp, bp, hs, bs = _mixer_c(xp, xs, g, j, state_ssm[j], state_conv[j], c_w_in, c_conv_w[j],
                                                c_conv_b[j], c_dt_bias[j], c_a_log[j], c_d[j], c_norm[j])
            proj_p, proj_s = (y_p, c_w_out, j), (y_s, c_w_out, j)
            ssm_p.append(hp); conv_p.append(bp); ssm_s.append(hs); conv_s.append(bs)
        gm = norm_mlp[i].reshape(1, D_MODEL)
        xs, w_up_b, w_down_b = mlp(xs, gm, mlp_w_up, mlp_w_down, i, tm=TM_SAMPLE, tf=MLP_TF, proj=proj_s, emit=True)
        xp = mlp(xp, gm, w_up_b, w_down_b, 0, tm=TM_PROMPT, tf=MLP_TF, proj=proj_p)
    y_prompt = xp.reshape(BATCH, SEQ, D_MODEL)
    y_sample = jnp.swapaxes(xs.reshape(DEC_SEQ, DEC_BATCH, D_MODEL), 0, 1)
    return (y_prompt, y_sample, jnp.stack(chunk_v_s),
            jnp.stack(swa_kp), jnp.stack(swa_vp), jnp.stack(swa_ks), jnp.stack(swa_vs),
            jnp.stack(ssm_p), jnp.stack(conv_p), jnp.stack(ssm_s), jnp.stack(conv_s))
```

```python
import functools
import math

import jax
import jax.numpy as jnp
import numpy as np
from jax import lax
from jax.experimental import pallas as pl
from jax.experimental.pallas import tpu as pltpu

f32 = jnp.float32
bf16 = jnp.bfloat16

D_MODEL = 1024
BATCH = 4
SEQ = 4096
DEPTH = 4
DEC_BATCH = 128
DEC_SEQ = 4
PAST_LEN = 8192
N_MIXERS = 3
D_FF = 4 * D_MODEL
EPS = 1e-6
NEG_INF = -1e30

A_CHUNK = 128
A_D_FFN = 6 * D_MODEL
A_HALF = A_D_FFN // 2
A_GROUPS = 8
A_GROUP_W = A_HALF // A_GROUPS

B_HEADS = 16
B_KV_HEADS = 4
B_HEAD_DIM = 64
B_REP = B_HEADS // B_KV_HEADS
B_WINDOW = 128
B_BLOCK = 128
B_Q_DIM = B_HEADS * B_HEAD_DIM
B_KV_DIM = B_KV_HEADS * B_HEAD_DIM
N_BUCKETS = 32
MAX_DISTANCE = 128

C_D_INNER = 2 * D_MODEL
C_HEAD_DIM = 64
C_HEADS = C_D_INNER // C_HEAD_DIM
C_GROUPS = 4
C_REP = C_HEADS // C_GROUPS
C_D_STATE = 128
C_D_CONV = 4
C_BC_DIM = C_GROUPS * C_D_STATE
C_CONV_DIM = C_D_INNER + 2 * C_BC_DIM
C_GROUP_W = C_D_INNER // C_GROUPS
C_CHUNK = 128

LANES = 128
SUBLANES = 8
VMEM_LIMIT_BYTES = 56 * 1024 * 1024

PROMPT_ROWS = BATCH * SEQ
SAMPLE_ROWS = DEC_BATCH * DEC_SEQ
TM_PROMPT = 1024
TM_SAMPLE = SAMPLE_ROWS
MLP_TF = 1024


def _params(*sem):
    return pltpu.CompilerParams(dimension_semantics=sem, vmem_limit_bytes=VMEM_LIMIT_BYTES)


def _rms(x, g):
    ms = jnp.mean(x * x, axis=-1, keepdims=True)
    return x * lax.rsqrt(ms + EPS) * g


def _gelu(x):
    return 0.5 * x * (1.0 + lax.erf(x * math.sqrt(0.5)))


def _silu(x):
    return x * jax.nn.sigmoid(x)


def _softplus(x):
    return jnp.maximum(x, 0.0) + jnp.log1p(jnp.exp(-jnp.abs(x)))


def _dot(a, b):
    return jnp.dot(a, b, preferred_element_type=f32)


def _dot_nt(a, b):
    return lax.dot_general(a, b, (((1,), (1,)), ((), ())), preferred_element_type=f32)


def _dot_tn(a, b):
    return lax.dot_general(a, b, (((0,), (0,)), ((), ())), preferred_element_type=f32)


def _dot_exact_lhs01(a01, x):
    a = a01.astype(bf16)
    hi = x.astype(bf16)
    r1 = x - hi.astype(f32)
    mid = r1.astype(bf16)
    lo = (r1 - mid.astype(f32)).astype(bf16)
    return _dot(a, hi) + _dot(a, mid) + _dot(a, lo)


def _dot_exact_rhs01(x, b01):
    b = b01.astype(bf16)
    hi = x.astype(bf16)
    r1 = x - hi.astype(f32)
    mid = r1.astype(bf16)
    lo = (r1 - mid.astype(f32)).astype(bf16)
    return _dot(hi, b) + _dot(mid, b) + _dot(lo, b)


def _norm_matmul_kernel(*refs, nj, tail):
    if tail:
        x_ref, g_ref, w_ref, wt_ref, o_ref, ot_ref, xn_ref = refs
    else:
        x_ref, g_ref, w_ref, o_ref, xn_ref = refs
    j = pl.program_id(1)

    @pl.when(j == 0)
    def _():
        xn_ref[...] = _rms(x_ref[...], g_ref[...]).astype(bf16)

    @pl.when(j < nj)
    def _():
        o_ref[...] = _dot(xn_ref[...], w_ref[0].astype(bf16)).astype(o_ref.dtype)

    if tail:
        @pl.when(j == nj)
        def _():
            ot_ref[...] = _dot(xn_ref[...], wt_ref[...].astype(bf16))


def norm_matmul(x, g, w, layer, n, *, tm, tn, w_tail=None, out_dtype=f32):
    m, k = x.shape
    nj = n // tn
    tail = w_tail is not None
    last = nj - 1
    in_specs = [
        pl.BlockSpec((tm, k), lambda i, j: (i, 0)),
        pl.BlockSpec((1, k), lambda i, j: (0, 0)),
        pl.BlockSpec((1, k, tn), lambda i, j: (layer, 0, jnp.minimum(j, last))),
    ]
    out_specs = [pl.BlockSpec((tm, tn), lambda i, j: (i, jnp.minimum(j, last)))]
    out_shape = [jax.ShapeDtypeStruct((m, n), out_dtype)]
    args = [x, g, w]
    if tail:
        in_specs.append(pl.BlockSpec((k, LANES), lambda i, j: (0, 0)))
        out_specs.append(pl.BlockSpec((tm, LANES), lambda i, j: (i, 0)))
        out_shape.append(jax.ShapeDtypeStruct((m, LANES), f32))
        args.append(w_tail)
    out = pl.pallas_call(
        functools.partial(_norm_matmul_kernel, nj=nj, tail=tail),
        grid=(m // tm, nj + (1 if tail else 0)),
        in_specs=in_specs,
        out_specs=out_specs,
        out_shape=out_shape,
        scratch_shapes=[pltpu.VMEM((tm, k), bf16)],
        compiler_params=_params("parallel", "arbitrary"),
        name="norm_matmul",
    )(*args)
    return out if tail else out[0]


def _mlp_kernel(*refs, proj, emit):
    refs = list(refs)
    a_ref, wo_ref = (refs.pop(0), refs.pop(0)) if proj else (None, None)
    x_ref, g_ref, wu_ref, wd_ref, o_ref = refs[:5]
    wub_ref, wdb_ref = (refs[5], refs[6]) if emit else (None, None)
    xn_ref = refs[-1]

    @pl.when(pl.program_id(1) == 0)
    def _():
        x = x_ref[...]
        if proj:
            x = x + _dot(a_ref[...], wo_ref[0])
        xn_ref[...] = _rms(x, g_ref[...]).astype(bf16)
        o_ref[...] = x

    wu = wu_ref[0].astype(bf16)
    wd = wd_ref[0].astype(bf16)
    if emit:
        wub_ref[0] = wu
        wdb_ref[0] = wd
    h = jnp.maximum(_dot(xn_ref[...], wu), 0.0)
    o_ref[...] += _dot((h * h).astype(bf16), wd)


def mlp(x, g, w_up, w_down, layer, *, tm, tf, proj=None, emit=False):
    m, d = x.shape
    ff = w_up.shape[2]
    assert not emit or m == tm
    in_specs = [
        pl.BlockSpec((tm, d), lambda i, j: (i, 0)),
        pl.BlockSpec((1, d), lambda i, j: (0, 0)),
        pl.BlockSpec((1, d, tf), lambda i, j: (layer, 0, j)),
        pl.BlockSpec((1, tf, d), lambda i, j: (layer, j, 0)),
    ]
    args = [x, g, w_up, w_down]
    if proj is not None:
        a, w_o, lo = proj
        k = a.shape[1]
        in_specs = [pl.BlockSpec((tm, k), lambda i, j: (i, 0)),
                    pl.BlockSpec((1, k, d), lambda i, j: (lo, 0, 0), pipeline_mode=pl.Buffered(1))] + in_specs
        args = [a, w_o] + args
    out_specs = [pl.BlockSpec((tm, d), lambda i, j: (i, 0))]
    out_shape = [jax.ShapeDtypeStruct((m, d), f32)]
    if emit:
        out_specs += [pl.BlockSpec((1, d, tf), lambda i, j: (0, 0, j)), pl.BlockSpec((1, tf, d), lambda i, j: (0, j, 0))]
        out_shape += [jax.ShapeDtypeStruct((1, d, ff), bf16), jax.ShapeDtypeStruct((1, ff, d), bf16)]
    out = pl.pallas_call(
        functools.partial(_mlp_kernel, proj=proj is not None, emit=emit),
        grid=(m // tm, ff // tf),
        in_specs=in_specs,
        out_specs=out_specs,
        out_shape=out_shape,
        scratch_shapes=[pltpu.VMEM((tm, d), bf16)],
        compiler_params=_params("parallel", "arbitrary"),
        name="mlp",
    )(*args)
    return out if emit else out[0]


A_BLK_GROUPS = 2
A_BLK = A_BLK_GROUPS * A_GROUP_W
A_NBLK = A_HALF // A_BLK


def _mixer_a_kernel(*refs, sample):
    if sample:
        (ws_ref, bs_ref, x_ref, g_ref, win_ref, nv_ref, wout_ref, o_ref, vo_ref, winb_ref, woutb_ref,
         xn_ref, v_ref, ssq_ref, us_ref) = refs
    else:
        x_ref, g_ref, win_ref, nv_ref, ws_ref, bs_ref, wout_ref, o_ref, xn_ref, v_ref, ssq_ref, us_ref = refs
        winb_ref = woutb_ref = None
    j = pl.program_id(1)
    tm = x_ref.shape[0]

    def w_in_block():
        w = win_ref[0].astype(bf16)
        if winb_ref is not None:
            winb_ref[0] = w
        return w

    def w_out_block():
        w = wout_ref[0].astype(bf16)
        if woutb_ref is not None:
            woutb_ref[0] = w
        return w

    @pl.when(j == 0)
    def _():
        xn_ref[...] = _rms(x_ref[...], g_ref[...]).astype(bf16)
        ssq_ref[...] = jnp.zeros(ssq_ref.shape, f32)

    for k in range(A_NBLK):
        @pl.when(j == k)
        def _(k=k):
            v = _gelu(_dot(xn_ref[...], w_in_block()))
            v_ref[:, k * A_BLK:(k + 1) * A_BLK] = v
            ssq_ref[...] += jnp.sum(v * v, axis=-1, keepdims=True)

    if not sample:
        row = lax.broadcasted_iota(jnp.int32, (A_CHUNK, A_CHUNK), 0)
        col = lax.broadcasted_iota(jnp.int32, (A_CHUNK, A_CHUNK), 1)
        causal = row >= col

    for k in range(A_NBLK):
        @pl.when(j == A_NBLK + k)
        def _(k=k):
            u = _gelu(_dot(xn_ref[...], w_in_block()))
            rinv = lax.rsqrt(ssq_ref[...] * (1.0 / A_HALF) + EPS)
            for gg in range(A_BLK_GROUPS):
                g = k * A_BLK_GROUPS + gg
                cols = slice(g * A_GROUP_W, (g + 1) * A_GROUP_W)
                ucols = slice(gg * A_GROUP_W, (gg + 1) * A_GROUP_W)
                vn = v_ref[:, cols] * rinv * nv_ref[:, cols]
                if sample:
                    vo_ref[:, cols] = vn
                    vt = [vn[t * DEC_BATCH:(t + 1) * DEC_BATCH] for t in range(DEC_SEQ)]
                    s_rows = []
                    for t in range(DEC_SEQ):
                        s = ws_ref[(g * DEC_SEQ + t) * DEC_SEQ] * vt[0]
                        for t2 in range(1, t + 1):
                            s = s + ws_ref[(g * DEC_SEQ + t) * DEC_SEQ + t2] * vt[t2]
                        s_rows.append(s + bs_ref[g * DEC_SEQ + t])
                    s = jnp.concatenate(s_rows, axis=0)
                else:
                    w = jnp.where(causal, ws_ref[g], 0.0).astype(bf16)
                    bias = bs_ref[:, g:g + 1]
                    vb = vn.astype(bf16)
                    s = jnp.concatenate(
                        [_dot(w, vb[c * A_CHUNK:(c + 1) * A_CHUNK]) + bias for c in range(tm // A_CHUNK)], axis=0)
                us_ref[:, ucols] = (u[:, ucols] * s).astype(bf16)
            y = _dot(us_ref[...], w_out_block())
            if k == 0:
                o_ref[...] = x_ref[...] + y
            else:
                o_ref[...] += y


def mixer_a(x, g, w_in, norm_v, w_sp, b_sp, w_out, layer, *, tm, sample):
    m, d = x.shape
    nj = 2 * A_NBLK
    row = lambda w: pl.BlockSpec((tm, w), lambda i, j: (i, 0))
    full = lambda *shape: pl.BlockSpec(shape, lambda i, j: (0,) * len(shape))
    smem = pl.BlockSpec(memory_space=pltpu.SMEM)
    win_spec = pl.BlockSpec((1, d, A_BLK), lambda i, j: (layer, 0, (j + A_NBLK) % nj))
    wout_spec = pl.BlockSpec((1, A_BLK, d), lambda i, j: (layer, jnp.maximum(j - A_NBLK, 0), 0))
    if sample:
        in_specs = [smem, smem, row(d), full(1, d), win_spec, full(1, A_HALF), wout_spec]
        args = (w_sp, b_sp, x, g, w_in, norm_v, w_out)
        out_specs = [row(d), row(A_HALF),
                     pl.BlockSpec((1, d, A_BLK), lambda i, j: (0, 0, (j + A_NBLK) % nj)),
                     pl.BlockSpec((1, A_BLK, d), lambda i, j: (0, jnp.maximum(j - A_NBLK, 0), 0))]
        out_shape = [jax.ShapeDtypeStruct((m, d), f32), jax.ShapeDtypeStruct((m, A_HALF), f32),
                     jax.ShapeDtypeStruct((1, d, 2 * A_HALF), bf16), jax.ShapeDtypeStruct((1, A_HALF, d), bf16)]
    else:
        in_specs = [row(d), full(1, d), win_spec, full(1, A_HALF), full(A_GROUPS, A_CHUNK, A_CHUNK),
                    full(A_CHUNK, A_GROUPS), wout_spec]
        args = (x, g, w_in, norm_v, w_sp, b_sp, w_out)
        out_specs = row(d)
        out_shape = jax.ShapeDtypeStruct((m, d), f32)
    return pl.pallas_call(
        functools.partial(_mixer_a_kernel, sample=sample),
        grid=(m // tm, nj),
        in_specs=in_specs,
        out_specs=out_specs,
        out_shape=out_shape,
        scratch_shapes=[pltpu.VMEM((tm, d), bf16), pltpu.VMEM((tm, A_HALF), f32), pltpu.VMEM((tm, 1), f32),
                        pltpu.VMEM((tm, A_BLK), bf16)],
        compiler_params=_params("parallel", "arbitrary"),
        name="mixer_a_sample" if sample else "mixer_a_prompt",
    )(*args)


def _bucket_table():
    i = np.arange(B_BLOCK)[:, None]
    j = np.arange(2 * B_BLOCK)[None, :]
    n = np.maximum(B_BLOCK + i - j, 0)
    max_exact = N_BUCKETS // 2
    nf = np.maximum(n, 1).astype(np.float64)
    val = np.log(nf / max_exact) / math.log(MAX_DISTANCE / max_exact) * (N_BUCKETS - max_exact)
    in_window = (n >= max_exact) & (n < B_WINDOW)
    assert np.all(np.abs(val - np.round(val))[in_window & (n != max_exact)] > 1e-3)
    large = np.minimum(max_exact + np.floor(val + 1e-9).astype(np.int64), N_BUCKETS - 1)
    return np.where(n < max_exact, n, large).astype(np.int32)


def _bias_table_kernel(rb_ref, bk_ref, o_ref):
    bk = bk_ref[...]
    for h in range(B_HEADS):
        acc = jnp.zeros(bk.shape, f32)
        for b in range(N_BUCKETS):
            acc = jnp.where(bk == b, rb_ref[b * B_HEADS + h], acc)
        o_ref[h] = acc


def bias_table(rel_bias):
    return pl.pallas_call(
        _bias_table_kernel,
        in_specs=[pl.BlockSpec(memory_space=pltpu.SMEM), pl.BlockSpec(memory_space=pltpu.VMEM)],
        out_specs=pl.BlockSpec(memory_space=pltpu.VMEM),
        out_shape=jax.ShapeDtypeStruct((B_HEADS, B_BLOCK, 2 * B_BLOCK), f32),
        name="bias_table",
    )(rel_bias.reshape(-1), jnp.asarray(_bucket_table()))


def _softmax_with_sink(logits, sink):
    m = jnp.maximum(jnp.max(logits, axis=-1, keepdims=True), sink)
    p = jnp.exp(logits - m)
    return p, jnp.sum(p, axis=-1, keepdims=True) + jnp.exp(sink - m)


def _rms_head_pairs(x, g2, lo):
    sq = x * x
    s_lo = jnp.sum(jnp.where(lo, sq, 0.0), axis=-1, keepdims=True)
    s_hi = jnp.sum(jnp.where(lo, 0.0, sq), axis=-1, keepdims=True)
    r = lax.rsqrt(jnp.where(lo, s_lo, s_hi) * (1.0 / B_HEAD_DIM) + EPS)
    return x * r * g2


ATT_SUB = 8


def _attn_prompt_kernel(sink_ref, x_ref, g_ref, w_ref, tab_ref, qn_ref, kn_ref, o_ref, ko_ref, vo_ref,
                        xn_ref, kband_ref, vband_ref, q_ref, p_ref, rhs_ref):
    n = pl.program_id(1)
    T = B_BLOCK

    @pl.when(n == 0)
    def _():
        kband_ref[0:T, :] = jnp.zeros((T, B_KV_DIM), f32)
        vband_ref[0:T, :] = jnp.zeros((T, B_KV_DIM), f32)
        rhs_ref[...] = jnp.ones(rhs_ref.shape, bf16)

    xn_ref[...] = _rms(x_ref[...], g_ref[...]).astype(bf16)
    for sub in range(ATT_SUB):
        rows = pl.ds(sub * T, T)
        qkv = _dot(xn_ref[rows, :], w_ref[0])
        first_key = jnp.where(n == 0, T, 0) if sub == 0 else 0
        _attn_block(first_key, sink_ref, qkv, tab_ref, qn_ref, kn_ref, o_ref.at[rows], ko_ref, vo_ref,
                    kband_ref, vband_ref, q_ref.at[sub], p_ref.at[sub], rhs_ref.at[sub])


def _attn_block(first_key, sink_ref, qkv_ref, tab_ref, qn_ref, kn_ref, o_ref, ko_ref, vo_ref,
                kband_ref, vband_ref, q_ref, p_ref, rhs_ref):
    T = B_BLOCK
    lo = lax.broadcasted_iota(jnp.int32, (1, LANES), 1) < B_HEAD_DIM
    qn2 = qn_ref[...]
    kn2 = kn_ref[...]
    for t in range(B_KV_DIM // LANES):
        lanes = slice(t * LANES, (t + 1) * LANES)
        k2 = _rms_head_pairs(qkv_ref[:, B_Q_DIM + t * LANES:B_Q_DIM + (t + 1) * LANES].astype(f32), kn2, lo)
        ko_ref[0, :, lanes] = k2
        kband_ref[T:, lanes] = k2
    v = qkv_ref[:, B_Q_DIM + B_KV_DIM:].astype(f32)
    vo_ref[0] = v
    vband_ref[T:, :] = v
    for t in range(B_Q_DIM // LANES):
        q2 = qkv_ref[:, t * LANES:(t + 1) * LANES].astype(f32)
        q_ref[t * T:(t + 1) * T, :] = _rms_head_pairs(q2, qn2, lo).astype(bf16)

    key_ops = {}
    band_row = lax.broadcasted_iota(jnp.int32, (2 * T, LANES), 0)
    for t in range(B_KV_DIM // LANES):
        lanes = slice(t * LANES, (t + 1) * LANES)
        kt = kband_ref[:, lanes]
        kr = pltpu.roll(kt, B_HEAD_DIM, axis=1)
        vt = jnp.where(band_row == 0, 0.0, vband_ref[:, lanes])
        vr = pltpu.roll(vt, B_HEAD_DIM, axis=1)
        hi = jnp.logical_not(lo)
        for half, (ksrc, vsrc) in enumerate(((kt, vt), (kr, vr))):
            g_lo, g_hi = (2 * t, 2 * t + 1) if half == 0 else (2 * t + 1, 2 * t)
            key_ops[(g_lo, 0)] = jnp.where(lo, ksrc, 0.0).astype(bf16)
            key_ops[(g_hi, 1)] = jnp.where(hi, ksrc, 0.0).astype(bf16)
            rhs_ref[g_lo * 2 + 0, :, 0:LANES] = jnp.where(lo, vsrc, 1.0).astype(bf16)
            rhs_ref[g_hi * 2 + 1, :, 0:LANES] = jnp.where(hi, vsrc, 1.0).astype(bf16)

    i = lax.broadcasted_iota(jnp.int32, (T, 2 * T), 0)
    j = lax.broadcasted_iota(jnp.int32, (T, 2 * T), 1)
    valid = (j > i) & (j <= i + B_WINDOW) & (j >= first_key)
    sink_col = j == 0
    lo_t = lax.broadcasted_iota(jnp.int32, (T, LANES), 1) < B_HEAD_DIM
    for g in range(B_KV_HEADS):
        q2 = q_ref[2 * g * T:(2 * g + 2) * T, :]
        res = []
        for half in range(2):
            logits = _dot_nt(q2, key_ops[(g, half)]) * (B_HEAD_DIM ** -0.5)
            for pair in range(2):
                h = g * B_REP + 2 * pair + half
                l = jnp.where(valid, logits[pair * T:(pair + 1) * T] + tab_ref[h], NEG_INF)
                l = jnp.where(sink_col, sink_ref[h], l)
                p = jnp.exp(l - jnp.max(l, axis=-1, keepdims=True))
                p_ref[g * 2 + half, pair * T:(pair + 1) * T, :] = p.astype(bf16)
            res.append(_dot(p_ref[g * 2 + half], rhs_ref[g * 2 + half]))
        for pair in range(2):
            rows = slice(pair * T, (pair + 1) * T)
            even = res[0][rows, 0:LANES] / res[0][rows, LANES:]
            odd = res[1][rows, 0:LANES] / res[1][rows, LANES:]
            t = 2 * g + pair
            o_ref[:, t * LANES:(t + 1) * LANES] = jnp.where(lo_t, even, odd).astype(bf16)
    kband_ref[0:T, :] = kband_ref[T:, :]
    vband_ref[0:T, :] = vband_ref[T:, :]


def attn_prompt(x, g, w_qkv, layer, table, q_norm, k_norm, sinks):
    step = ATT_SUB * B_BLOCK
    nb = SEQ // step
    return pl.pallas_call(
        _attn_prompt_kernel,
        grid=(BATCH, nb),
        in_specs=[
            pl.BlockSpec(memory_space=pltpu.SMEM),
            pl.BlockSpec((step, D_MODEL), lambda b, n: (b * nb + n, 0)),
            pl.BlockSpec((1, D_MODEL), lambda b, n: (0, 0)),
            pl.BlockSpec((1,) + w_qkv.shape[1:], lambda b, n: (layer, 0, 0), pipeline_mode=pl.Buffered(1)),
            pl.BlockSpec((B_HEADS, B_BLOCK, 2 * B_BLOCK), lambda b, n: (0, 0, 0)),
            pl.BlockSpec((1, LANES), lambda b, n: (0, 0)),
            pl.BlockSpec((1, LANES), lambda b, n: (0, 0)),
        ],
        out_specs=[
            pl.BlockSpec((step, B_Q_DIM), lambda b, n: (b * nb + n, 0)),
            pl.BlockSpec((1, B_BLOCK, B_KV_DIM), lambda b, n: (b, 0, 0)),
            pl.BlockSpec((1, B_BLOCK, B_KV_DIM), lambda b, n: (b, 0, 0)),
        ],
        out_shape=[
            jax.ShapeDtypeStruct((PROMPT_ROWS, B_Q_DIM), bf16),
            jax.ShapeDtypeStruct((BATCH, B_BLOCK, B_KV_DIM), f32),
            jax.ShapeDtypeStruct((BATCH, B_BLOCK, B_KV_DIM), f32),
        ],
        scratch_shapes=[
            pltpu.VMEM((step, D_MODEL), bf16),
            pltpu.VMEM((2 * B_BLOCK, B_KV_DIM), f32),
            pltpu.VMEM((2 * B_BLOCK, B_KV_DIM), f32),
            pltpu.VMEM((ATT_SUB, B_Q_DIM // LANES * B_BLOCK, LANES), bf16),
            pltpu.VMEM((ATT_SUB, 2 * B_KV_HEADS, 2 * B_BLOCK, 2 * B_BLOCK), bf16),
            pltpu.VMEM((ATT_SUB, 2 * B_KV_HEADS, 2 * B_BLOCK, 2 * LANES), bf16),
        ],
        compiler_params=_params("parallel", "arbitrary"),
        name="attn_prompt",
    )(sinks, x, g, w_qkv, table, jnp.tile(q_norm, (1, LANES // B_HEAD_DIM)),
      jnp.tile(k_norm, (1, LANES // B_HEAD_DIM)))


ATT_BB = 8
ATT_QR = B_REP * DEC_SEQ
ATT_KEYS = B_WINDOW + 2 * DEC_SEQ


def _attn_sample_kernel(sink_ref, q_ref, kn_ref, vn_ref, kc_ref, vc_ref, tab_ref, qn_ref, knm_ref,
                        o_ref, ko_ref, vo_ref, kall_ref, vall_ref):
    qn2 = qn_ref[...]
    kn2 = knm_ref[...]
    n_rows = B_KV_HEADS * ATT_QR
    lo = lax.broadcasted_iota(jnp.int32, (1, LANES), 1) < B_HEAD_DIM
    row = lax.broadcasted_iota(jnp.int32, (n_rows, ATT_KEYS), 0)
    j = lax.broadcasted_iota(jnp.int32, (n_rows, ATT_KEYS), 1)
    t = row % DEC_SEQ
    valid = (j > t) & (j <= t + B_WINDOW)
    q_row_group = lax.broadcasted_iota(jnp.int32, (n_rows, B_KV_DIM), 0) // ATT_QR
    q_lane_group = lax.broadcasted_iota(jnp.int32, (n_rows, B_KV_DIM), 1) // B_HEAD_DIM
    own_group = q_row_group == q_lane_group
    o_lane_group = lax.broadcasted_iota(jnp.int32, (ATT_QR, B_KV_DIM), 1) // B_HEAD_DIM
    pad = jnp.zeros((ATT_KEYS - B_WINDOW - DEC_SEQ, B_KV_DIM), f32)
    bias = tab_ref[...]
    sink = sink_ref[...]
    for s in range(ATT_BB):
        kc = kc_ref[s]
        vc = vc_ref[s]
        k_new = jnp.concatenate(
            [_rms_head_pairs(kn_ref[s, :, tt * LANES:(tt + 1) * LANES], kn2, lo) for tt in range(B_KV_DIM // LANES)],
            axis=1)
        v_new = vn_ref[s]
        ko_ref[s, 0:B_WINDOW - DEC_SEQ, :] = kc[DEC_SEQ:, :]
        ko_ref[s, B_WINDOW - DEC_SEQ:, :] = k_new
        vo_ref[s, 0:B_WINDOW - DEC_SEQ, :] = vc[DEC_SEQ:, :]
        vo_ref[s, B_WINDOW - DEC_SEQ:, :] = v_new
        kall_ref[s, 0:B_WINDOW, :] = kc
        kall_ref[s, B_WINDOW:B_WINDOW + DEC_SEQ, :] = k_new
        kall_ref[s, B_WINDOW + DEC_SEQ:, :] = pad
        vall_ref[s, 0:B_WINDOW, :] = vc
        vall_ref[s, B_WINDOW:B_WINDOW + DEC_SEQ, :] = v_new
        vall_ref[s, B_WINDOW + DEC_SEQ:, :] = pad
        q = q_ref[s]
        qn = jnp.concatenate(
            [_rms_head_pairs(q[:, tt * LANES:(tt + 1) * LANES], qn2, lo) for tt in range(B_KV_DIM // LANES)], axis=1)
        q_all = jnp.where(own_group, jnp.concatenate([qn] * B_KV_HEADS, axis=0), 0.0).astype(bf16)
        logits = _dot_nt(q_all, kall_ref[s].astype(bf16)) * (B_HEAD_DIM ** -0.5)
        logits = jnp.where(valid, logits + bias, NEG_INF)
        p, denom = _softmax_with_sink(logits, sink)
        res = _dot(p.astype(bf16), vall_ref[s].astype(bf16)) / denom
        out = jnp.zeros((ATT_QR, B_KV_DIM), f32)
        for g in range(B_KV_HEADS):
            out = jnp.where(o_lane_group == g, res[g * ATT_QR:(g + 1) * ATT_QR, :], out)
        o_ref[s] = out


def attn_sample(q_s, k_new, v_new, k_cache, v_cache, table_s, q_norm, k_norm, sinks):
    blk = lambda *shape: pl.BlockSpec((ATT_BB,) + shape, lambda i: (i,) + (0,) * len(shape))
    full = lambda *shape: pl.BlockSpec(shape, lambda i: (0,) * len(shape))
    return pl.pallas_call(
        _attn_sample_kernel,
        grid=(DEC_BATCH // ATT_BB,),
        in_specs=[
            full(B_KV_HEADS * ATT_QR, 1),
            blk(ATT_QR, B_KV_DIM), blk(DEC_SEQ, B_KV_DIM), blk(DEC_SEQ, B_KV_DIM),
            blk(B_WINDOW, B_KV_DIM), blk(B_WINDOW, B_KV_DIM),
            full(B_KV_HEADS * ATT_QR, ATT_KEYS), full(1, LANES), full(1, LANES),
        ],
        out_specs=[blk(ATT_QR, B_KV_DIM), blk(B_WINDOW, B_KV_DIM), blk(B_WINDOW, B_KV_DIM)],
        out_shape=[
            jax.ShapeDtypeStruct((DEC_BATCH, ATT_QR, B_KV_DIM), f32),
            jax.ShapeDtypeStruct((DEC_BATCH, B_WINDOW, B_KV_DIM), f32),
            jax.ShapeDtypeStruct((DEC_BATCH, B_WINDOW, B_KV_DIM), f32),
        ],
        scratch_shapes=[pltpu.VMEM((ATT_BB, ATT_KEYS, B_KV_DIM), f32), pltpu.VMEM((ATT_BB, ATT_KEYS, B_KV_DIM), f32)],
        compiler_params=_params("parallel"),
        name="attn_sample",
    )(jnp.repeat(sinks, DEC_SEQ).reshape(B_KV_HEADS * ATT_QR, 1), q_s, k_new, v_new, k_cache, v_cache,
      table_s.reshape(B_KV_HEADS * ATT_QR, ATT_KEYS),
      jnp.tile(q_norm, (1, LANES // B_HEAD_DIM)), jnp.tile(k_norm, (1, LANES // B_HEAD_DIM)))


CONV_PAD = SUBLANES

def _gated_group_norm(y, z, norm_w):
    gt = y * _silu(z)
    parts = []
    for g in range(C_GROUPS):
        gg = gt[:, g * C_GROUP_W:(g + 1) * C_GROUP_W]
        parts.append(gg * lax.rsqrt(jnp.mean(gg * gg, axis=-1, keepdims=True) + EPS))
    return jnp.concatenate(parts, axis=1) * norm_w


LOG2E = math.log2(math.e)


def _expand_heads(v, sel3):
    lane = lax.broadcasted_iota(jnp.int32, (1, LANES), 1)
    v = jnp.where(lane < C_HEADS, v, 0.0)
    hi = v.astype(bf16).astype(f32)
    r1 = v - hi
    mid = r1.astype(bf16).astype(f32)
    lo = r1 - mid
    packed = hi + pltpu.roll(mid, C_HEADS, axis=1) + pltpu.roll(lo, 2 * C_HEADS, axis=1)
    return _dot(packed.astype(bf16), sel3)


SSD_SUB = 4


def _ssd_prompt_kernel(x_ref, g_ref, w_ref, wdt_ref, cw_ref, cb_ref, dtb_ref, alog_ref, dsk_ref, nw_ref, sel_ref,
                       yn_ref, hfin_ref, cout_ref, xn_ref, xpad_ref, ht_ref, y_ref):
    c = pl.program_id(1)
    n_zx = C_D_INNER + C_CONV_DIM

    @pl.when(c == 0)
    def _():
        xpad_ref[0, 0:CONV_PAD, :] = jnp.zeros((CONV_PAD, C_CONV_DIM), f32)
        ht_ref[...] = jnp.zeros(ht_ref.shape, f32)

    xn_ref[...] = _rms(x_ref[...], g_ref[...]).astype(bf16)
    for sub in range(SSD_SUB):
        rows = pl.ds(sub * C_CHUNK, C_CHUNK)
        xn = xn_ref[rows, :]
        zx = _dot(xn, w_ref[0, :, 0:n_zx])
        dtr = _dot(xn, wdt_ref[...])
        _ssd_chunk(zx[:, :C_D_INNER], zx[:, C_D_INNER:], dtr, cw_ref, cb_ref, dtb_ref, alog_ref, dsk_ref, nw_ref,
                   sel_ref, yn_ref.at[rows], cout_ref, xpad_ref.at[sub], xpad_ref.at[(sub + 1) % SSD_SUB], ht_ref,
                   y_ref.at[sub])

    @pl.when(c == pl.num_programs(1) - 1)
    def _():
        for t in range(C_D_INNER // LANES):
            hfin_ref[0, t * LANES:(t + 1) * LANES, :] = ht_ref[:, t * LANES:(t + 1) * LANES].T


def _ssd_chunk(z, xbc, dtr, cw_ref, cb_ref, dtb_ref, alog_ref, dsk_ref, nw_ref, sel_ref,
               yn_ref, cout_ref, xpad_ref, xpad_next_ref, ht_ref, y_ref):
    T = C_CHUNK
    xpad_ref[CONV_PAD:, :] = xbc
    xp = xpad_ref[...]
    cw = cw_ref[...]
    acc = cb_ref[...]
    for tap in range(C_D_CONV - 1):
        shifted = pltpu.roll(xp, C_D_CONV - 1 - tap, axis=0)[CONV_PAD:, :]
        acc = acc + shifted * cw[tap:tap + 1, :]
    acc = acc + xbc * cw[C_D_CONV - 1:C_D_CONV, :]
    xpad_next_ref[0:CONV_PAD, :] = xbc[T - CONV_PAD:, :]
    cout_ref[0] = xbc[T - (C_D_CONV - 1):, :]
    act = _silu(acc)
    xs = act[:, :C_D_INNER]
    bm = act[:, C_D_INNER:C_D_INNER + C_BC_DIM]
    cm = act[:, C_D_INNER + C_BC_DIM:]
    xb = xs.astype(bf16)

    dt = _softplus(dtr + dtb_ref[...])
    a_neg = -jnp.exp(alog_ref[...])
    row = lax.broadcasted_iota(jnp.int32, (T, T), 0)
    col = lax.broadcasted_iota(jnp.int32, (T, T), 1)
    causal = row >= col
    acs = _dot_exact_lhs01(causal.astype(f32), dt * a_neg)
    a2 = acs * LOG2E
    sel3 = sel_ref[...]
    e_exp = jnp.exp2(_expand_heads(a2, sel3))
    w_exp = _expand_heads(jnp.exp(acs[T - 1:T, :] - acs) * dt, sel3)
    cdec = e_exp[T - 1:T, :]
    b2_t = a2.T - jnp.log2(dt.T)
    xw = (xs * w_exp).astype(bf16)
    hb = ht_ref[...].astype(bf16)
    dsk = dsk_ref[...]
    lo_t = lax.broadcasted_iota(jnp.int32, (T, LANES), 1) < C_HEAD_DIM

    for g in range(C_GROUPS):
        ns = slice(g * C_D_STATE, (g + 1) * C_D_STATE)
        gs = slice(g * C_GROUP_W, (g + 1) * C_GROUP_W)
        b_g = bm[:, ns]
        c_g = cm[:, ns].astype(bf16)
        cb = _dot_nt(c_g, b_g.astype(bf16))
        yi = _dot(c_g, hb[:, gs])
        ht_ref[:, gs] = ht_ref[:, gs] * cdec[:, gs] + _dot(b_g.T.astype(bf16), xw[:, gs])
        for tt in range(C_GROUP_W // LANES):
            t = g * (C_GROUP_W // LANES) + tt
            lanes = slice(t * LANES, (t + 1) * LANES)
            xt = xb[:, lanes]
            res = []
            for half in range(2):
                h = 2 * t + half
                a_col = jnp.broadcast_to(a2[:, h:h + 1], (T, T))
                b_row = jnp.broadcast_to(b2_t[h:h + 1, :], (T, T))
                w = jnp.where(causal, cb * jnp.exp2(a_col - b_row), 0.0)
                res.append(_dot(w.astype(bf16), xt))
            y_intra = jnp.where(lo_t, res[0], res[1])
            y_ref[:, lanes] = y_intra + e_exp[:, lanes] * yi[:, tt * LANES:(tt + 1) * LANES] + dsk[:, lanes] * xs[:, lanes]

    yn_ref[...] = _gated_group_norm(y_ref[...], z, nw_ref[...]).astype(bf16)


def _head_select3():
    k = np.arange(LANES)[:, None]
    ch = np.arange(C_D_INNER)[None, :] // C_HEAD_DIM
    return jnp.asarray((k % C_HEADS == ch) & (k < 3 * C_HEADS), dtype=bf16)


def ssd_prompt(x, g, w_in, layer, w_dt, conv_w, conv_b, dt_bias, a_log, d_skip, norm_w):
    step = SSD_SUB * C_CHUNK
    nc = SEQ // step
    full = lambda *shape: pl.BlockSpec(shape, lambda b, c: (0,) * len(shape))
    once = lambda *shape: pl.BlockSpec(shape, lambda b, c: (0,) * len(shape), pipeline_mode=pl.Buffered(1))
    return pl.pallas_call(
        _ssd_prompt_kernel,
        grid=(BATCH, nc),
        in_specs=[
            pl.BlockSpec((step, D_MODEL), lambda b, c: (b * nc + c, 0)),
            full(1, D_MODEL),
            pl.BlockSpec((1,) + w_in.shape[1:], lambda b, c: (layer, 0, 0), pipeline_mode=pl.Buffered(1)),
            once(D_MODEL, LANES),
            full(C_D_CONV, C_CONV_DIM), full(1, C_CONV_DIM), full(1, LANES), full(1, LANES),
            full(1, C_D_INNER), full(1, C_D_INNER), full(LANES, C_D_INNER),
        ],
        out_specs=[
            pl.BlockSpec((step, C_D_INNER), lambda b, c: (b * nc + c, 0)),
            pl.BlockSpec((1, C_D_INNER, C_D_STATE), lambda b, c: (b, 0, 0)),
            pl.BlockSpec((1, C_D_CONV - 1, C_CONV_DIM), lambda b, c: (b, 0, 0)),
        ],
        out_shape=[
            jax.ShapeDtypeStruct((PROMPT_ROWS, C_D_INNER), bf16),
            jax.ShapeDtypeStruct((BATCH, C_D_INNER, C_D_STATE), f32),
            jax.ShapeDtypeStruct((BATCH, C_D_CONV - 1, C_CONV_DIM), f32),
        ],
        scratch_shapes=[
            pltpu.VMEM((step, D_MODEL), bf16),
            pltpu.VMEM((SSD_SUB, CONV_PAD + C_CHUNK, C_CONV_DIM), f32),
            pltpu.VMEM((C_D_STATE, C_D_INNER), f32),
            pltpu.VMEM((SSD_SUB, C_CHUNK, C_D_INNER), f32),
        ],
        compiler_params=_params("parallel", "arbitrary"),
        name="ssd_prompt",
    )(x, g, w_in, w_dt, conv_w, conv_b, dt_bias, a_log, d_skip, norm_w, _head_select3())


SSD_BB = 8
SSD_TP = SUBLANES
_N_PAIRS = DEC_SEQ * (DEC_SEQ + 1) // 2
_N_COEF = _N_PAIRS + 2 * DEC_SEQ


def _ssd_sample_kernel(zx_ref, dtr_ref, cs_ref, h0_ref, cw_ref, cb_ref, dtb_ref, alog_ref, dsk_ref, nw_ref,
                       sel_ref, yn_ref, hn_ref, cout_ref, c_scr, b_scr, xw_scr, yi_scr, cd_scr):
    L = DEC_SEQ
    cw = cw_ref[...]
    xp = [cs_ref[k] for k in range(C_D_CONV - 1)] + [zx_ref[t, :, C_D_INNER:] for t in range(L)]
    for k in range(C_D_CONV - 1):
        cout_ref[k] = xp[L + k]
    act = []
    for t in range(L):
        acc = cb_ref[...]
        for tap in range(C_D_CONV):
            acc = acc + xp[t + tap] * cw[tap:tap + 1, :]
        act.append(_silu(acc))
    xs = [a[:, :C_D_INNER] for a in act]
    bm = [a[:, C_D_INNER:C_D_INNER + C_BC_DIM] for a in act]
    cm = [a[:, C_D_INNER + C_BC_DIM:] for a in act]

    a_neg = -jnp.exp(alog_ref[...])
    dt = [_softplus(dtr_ref[t] + dtb_ref[...]) for t in range(L)]
    acs = []
    for t in range(L):
        acs.append(dt[t] * a_neg if t == 0 else acs[t - 1] + dt[t] * a_neg)

    lane_group = lax.broadcasted_iota(jnp.int32, (SSD_BB, LANES), 1) // C_REP
    coefs = []
    for t in range(L):
        for t2 in range(t + 1):
            cbh = jnp.zeros((SSD_BB, LANES), f32)
            for g in range(C_GROUPS):
                ns = slice(g * C_D_STATE, (g + 1) * C_D_STATE)
                cbg = jnp.sum(cm[t][:, ns] * bm[t2][:, ns], axis=-1, keepdims=True)
                cbh = jnp.where(lane_group == g, cbg, cbh)
            coefs.append(cbh * jnp.exp(acs[t] - acs[t2]) * dt[t2])
    for t in range(L):
        coefs.append(jnp.exp(acs[t]))
    for t in range(L):
        coefs.append(jnp.exp(acs[L - 1] - acs[t]) * dt[t])
    coef = jnp.concatenate(coefs, axis=0)
    cexp = _dot_exact_rhs01(coef, sel_ref[...])
    cexp = [cexp[k * SSD_BB:(k + 1) * SSD_BB, :] for k in range(_N_COEF)]
    w_intra = cexp[:_N_PAIRS]
    w_inter = cexp[_N_PAIRS:_N_PAIRS + L]
    w_state = cexp[_N_PAIRS + L:]

    cd = jnp.concatenate([jnp.exp(acs[L - 1]), jnp.zeros((LANES - SSD_BB, LANES), f32)], axis=0)
    cd_t = cd.T
    for s in range(SSD_BB):
        cd_scr[s] = jnp.broadcast_to(cd_t[0:C_HEADS, s:s + 1], (C_HEADS, C_D_STATE))

    zeros_tail = jnp.zeros((SSD_BB, SSD_TP - L, C_D_INNER), f32)
    c_scr[:, L:, :] = zeros_tail[:, :, :C_BC_DIM]
    b_scr[:, L:, :] = zeros_tail[:, :, :C_BC_DIM]
    xw_scr[:, L:, :] = zeros_tail
    for t in range(L):
        xw_t = xs[t] * w_state[t]
        for s in range(SSD_BB):
            c_scr[s, t:t + 1, :] = cm[t][s:s + 1, :]
            b_scr[s, t:t + 1, :] = bm[t][s:s + 1, :]
            xw_scr[s, t:t + 1, :] = xw_t[s:s + 1, :]

    for s in range(SSD_BB):
        for g in range(C_GROUPS):
            ns = slice(g * C_D_STATE, (g + 1) * C_D_STATE)
            gs = slice(g * C_GROUP_W, (g + 1) * C_GROUP_W)
            h0 = h0_ref[s, gs, :]
            yi = _dot_nt(c_scr[s, :, ns].astype(bf16), h0.astype(bf16))
            for t in range(L):
                yi_scr[t, s:s + 1, gs] = yi[t:t + 1, :]
            st = _dot_tn(xw_scr[s, :, gs].astype(bf16), b_scr[s, :, ns].astype(bf16))
            for r in range(C_REP):
                h = g * C_REP + r
                rs = slice(r * C_HEAD_DIM, (r + 1) * C_HEAD_DIM)
                scale = cd_scr[s, h:h + 1, :]
                hn_ref[s, h * C_HEAD_DIM:(h + 1) * C_HEAD_DIM, :] = h0[rs, :] * scale + st[rs, :]

    dsk = dsk_ref[...]
    nw = nw_ref[...]
    pair = 0
    for t in range(L):
        y = w_inter[t] * yi_scr[t] + dsk * xs[t]
        for t2 in range(t + 1):
            y = y + w_intra[pair] * xs[t2]
            pair += 1
        yn_ref[t] = _gated_group_norm(y, zx_ref[t, :, :C_D_INNER], nw).astype(bf16)


def ssd_sample(zx_t, dtr_t, conv_state_t, h0, conv_w, conv_b, dt_bias, a_log, d_skip, norm_w, sel):
    tmaj = lambda n, w: pl.BlockSpec((n, SSD_BB, w), lambda i: (0, i, 0))
    full = lambda *shape: pl.BlockSpec(shape, lambda i: (0,) * len(shape))
    return pl.pallas_call(
        _ssd_sample_kernel,
        grid=(DEC_BATCH // SSD_BB,),
        in_specs=[
            tmaj(DEC_SEQ, C_D_INNER + C_CONV_DIM), tmaj(DEC_SEQ, LANES), tmaj(C_D_CONV - 1, C_CONV_DIM),
            pl.BlockSpec((SSD_BB, C_D_INNER, C_D_STATE), lambda i: (i, 0, 0)),
            full(C_D_CONV, C_CONV_DIM), full(1, C_CONV_DIM), full(1, LANES), full(1, LANES),
            full(1, C_D_INNER), full(1, C_D_INNER), full(LANES, C_D_INNER),
        ],
        out_specs=[
            tmaj(DEC_SEQ, C_D_INNER),
            pl.BlockSpec((SSD_BB, C_D_INNER, C_D_STATE), lambda i: (i, 0, 0)),
            tmaj(C_D_CONV - 1, C_CONV_DIM),
        ],
        out_shape=[
            jax.ShapeDtypeStruct((DEC_SEQ, DEC_BATCH, C_D_INNER), bf16),
            jax.ShapeDtypeStruct((DEC_BATCH, C_D_INNER, C_D_STATE), f32),
            jax.ShapeDtypeStruct((C_D_CONV - 1, DEC_BATCH, C_CONV_DIM), f32),
        ],
        scratch_shapes=[
            pltpu.VMEM((SSD_BB, SSD_TP, C_BC_DIM), f32),
            pltpu.VMEM((SSD_BB, SSD_TP, C_BC_DIM), f32),
            pltpu.VMEM((SSD_BB, SSD_TP, C_D_INNER), f32),
            pltpu.VMEM((DEC_SEQ, SSD_BB, C_D_INNER), f32),
            pltpu.VMEM((SSD_BB, C_HEADS, C_D_STATE), f32),
        ],
        compiler_params=_params("parallel"),
        name="ssd_sample",
    )(zx_t, dtr_t, conv_state_t, h0, conv_w, conv_b, dt_bias, a_log, d_skip, norm_w, sel)


def _pad_lanes(v):
    return jnp.pad(v.astype(f32), (0, LANES - v.shape[0])).reshape(1, LANES)


def _mixer_a(xp, xs, g, j, w_in, norm_v, w_sp, b_sp, w_out):
    nv = norm_v.reshape(1, A_HALF)
    xs, v_s, w_in_b, w_out_b = mixer_a(xs, g, w_in, nv, w_sp[:, :DEC_SEQ, :DEC_SEQ].reshape(-1),
                                       b_sp[:, :DEC_SEQ].reshape(-1), w_out, j, tm=TM_SAMPLE, sample=True)
    xp = mixer_a(xp, g, w_in_b, nv, w_sp, b_sp.T, w_out_b, 0, tm=TM_PROMPT, sample=False)
    return xp, xs, jnp.swapaxes(v_s.reshape(DEC_SEQ, DEC_BATCH, A_HALF), 0, 1)


def _mixer_b(xp, xs, g, j, k_cache, v_cache, w_qkv, q_norm, k_norm, sinks, rel_bias):
    qn = q_norm.reshape(1, B_HEAD_DIM)
    kn = k_norm.reshape(1, B_HEAD_DIM)
    table = bias_table(rel_bias)
    n_qkv = B_Q_DIM + 2 * B_KV_DIM

    o_p, k_p, v_p = attn_prompt(xp, g, w_qkv, j, table, qn, kn, sinks)

    qkv_s = norm_matmul(xs, g, w_qkv, j, n_qkv, tm=TM_SAMPLE, tn=512).reshape(DEC_SEQ, DEC_BATCH, -1)
    q_s = qkv_s[:, :, :B_Q_DIM].reshape(DEC_SEQ, DEC_BATCH, B_KV_HEADS, B_REP, B_HEAD_DIM)
    q_s = q_s.transpose(1, 3, 0, 2, 4).reshape(DEC_BATCH, ATT_QR, B_KV_DIM)
    k_new = jnp.swapaxes(qkv_s[:, :, B_Q_DIM:B_Q_DIM + B_KV_DIM], 0, 1)
    v_new = jnp.swapaxes(qkv_s[:, :, B_Q_DIM + B_KV_DIM:], 0, 1)
    table_s = table[:, :DEC_SEQ, :ATT_KEYS].reshape(B_KV_HEADS, ATT_QR, ATT_KEYS)
    o_s, k_s, v_s = attn_sample(q_s, k_new, v_new,
                                k_cache.reshape(DEC_BATCH, B_WINDOW, B_KV_DIM),
                                v_cache.reshape(DEC_BATCH, B_WINDOW, B_KV_DIM),
                                table_s, qn, kn, sinks)
    o_s = o_s.reshape(DEC_BATCH, B_REP, DEC_SEQ, B_KV_HEADS, B_HEAD_DIM).transpose(2, 0, 3, 1, 4)
    o_s = o_s.reshape(SAMPLE_ROWS, B_Q_DIM).astype(bf16)
    kv_shape_p = (BATCH, B_WINDOW, B_KV_HEADS, B_HEAD_DIM)
    kv_shape_s = (DEC_BATCH, B_WINDOW, B_KV_HEADS, B_HEAD_DIM)
    return o_p, o_s, k_p.reshape(kv_shape_p), v_p.reshape(kv_shape_p), k_s.reshape(kv_shape_s), v_s.reshape(kv_shape_s)


def _mixer_c(xp, xs, g, j, h0, conv_state, w_in, conv_w, conv_b, dt_bias, a_log, d_skip, norm_w):
    n_zx = C_D_INNER + C_CONV_DIM
    w_dt = jnp.pad(w_in[j, :, n_zx:], ((0, 0), (0, LANES - C_HEADS)))
    cb = conv_b.reshape(1, C_CONV_DIM)
    dtb = _pad_lanes(dt_bias)
    alog = _pad_lanes(a_log)
    dsk = jnp.repeat(d_skip.astype(f32), C_HEAD_DIM).reshape(1, C_D_INNER)
    nw = norm_w.reshape(1, C_D_INNER)

    yn_p, h_p, conv_p = ssd_prompt(xp, g, w_in, j, w_dt, conv_w, cb, dtb, alog, dsk, nw)

    zx_s, dtr_s = norm_matmul(xs, g, w_in, j, n_zx, tm=TM_SAMPLE, tn=1024, w_tail=w_dt)
    sel = (jnp.arange(LANES)[:, None] == jnp.arange(C_D_INNER)[None, :] // C_HEAD_DIM).astype(f32)
    yn_s, h_s, conv_s = ssd_sample(
        zx_s.reshape(DEC_SEQ, DEC_BATCH, -1), dtr_s.reshape(DEC_SEQ, DEC_BATCH, LANES),
        jnp.swapaxes(conv_state, 0, 1), h0.reshape(DEC_BATCH, C_D_INNER, C_D_STATE),
        conv_w, cb, dtb, alog, dsk, nw, sel)
    st_shape = (C_HEADS, C_HEAD_DIM, C_D_STATE)
    return (yn_p, yn_s.reshape(SAMPLE_ROWS, C_D_INNER), h_p.reshape((BATCH,) + st_shape), conv_p,
            h_s.reshape((DEC_BATCH,) + st_shape), jnp.swapaxes(conv_s, 0, 1))


def kernel(x_prompt, x_sample, cache_swa_k, cache_swa_v, state_ssm, state_conv, norm_mixer, norm_mlp, mlp_w_up, mlp_w_down, a_w_in, a_norm_v, a_w_spatial, a_b_spatial, a_w_out, b_w_qkv, b_q_norm, b_k_norm, b_sinks, rel_bias, b_w_out, c_w_in, c_conv_w, c_conv_b, c_dt_bias, c_a_log, c_d, c_norm, c_w_out):
    xp = x_prompt.reshape(PROMPT_ROWS, D_MODEL)
    xs = jnp.swapaxes(x_sample, 0, 1).reshape(SAMPLE_ROWS, D_MODEL)
    chunk_v_s = []
    swa_kp, swa_vp, swa_ks, swa_vs = [], [], [], []
    ssm_p, conv_p, ssm_s, conv_s = [], [], [], []
    c_w_in, b_w_out, c_w_out = c_w_in.astype(bf16), b_w_out.astype(bf16), c_w_out.astype(bf16)
    b_w_qkv = b_w_qkv.astype(bf16)
    for i in range(DEPTH):
        kind = i % N_MIXERS
        j = i // N_MIXERS
        g = norm_mixer[i].reshape(1, D_MODEL)
        proj_p = proj_s = None
        if kind == 0:
            xp, xs, v_new = _mixer_a(xp, xs, g, j, a_w_in, a_norm_v[j], a_w_spatial[j], a_b_spatial[j], a_w_out)
            chunk_v_s.append(v_new)
        elif kind == 1:
            o_p, o_s, kp, vp, ks_, vs_ = _mixer_b(xp, xs, g, j, cache_swa_k[j], cache_swa_v[j], b_w_qkv, b_q_norm[j],
                                                  b_k_norm[j], b_sinks[j], rel_bias)
            proj_p, proj_s = (o_p, b_w_out, j), (o_s, b_w_out, j)
            swa_kp.append(kp); swa_vp.append(vp); swa_ks.append(ks_); swa_vs.append(vs_)
        else:
            y_p, y_s, hp, bp, hs, bs = _mixer_c(xp, xs, g, j, state_ssm[j], state_conv[j], c_w_in, c_conv_w[j],
                                                c_conv_b[j], c_dt_bias[j], c_a_log[j], c_d[j], c_norm[j])
            proj_p, proj_s = (y_p, c_w_out, j), (y_s, c_w_out, j)
            ssm_p.append(hp); conv_p.append(bp); ssm_s.append(hs); conv_s.append(bs)
        gm = norm_mlp[i].reshape(1, D_MODEL)
        xs, w_up_b, w_down_b = mlp(xs, gm, mlp_w_up, mlp_w_down, i, tm=TM_SAMPLE, tf=MLP_TF, proj=proj_s, emit=True)
        xp = mlp(xp, gm, w_up_b, w_down_b, 0, tm=TM_PROMPT, tf=MLP_TF, proj=proj_p)
    y_prompt = xp.reshape(BATCH, SEQ, D_MODEL)
    y_sample = jnp.swapaxes(xs.reshape(DEC_SEQ, DEC_BATCH, D_MODEL), 0, 1)
    return (y_prompt, y_sample, jnp.stack(chunk_v_s),
            jnp.stack(swa_kp), jnp.stack(swa_vp), jnp.stack(swa_ks), jnp.stack(swa_vs),
            jnp.stack(ssm_p), jnp.stack(conv_p), jnp.stack(ssm_s), jnp.stack(conv_s))
```

```python
import functools
import math

import jax
import jax.numpy as jnp
import numpy as np
from jax import lax
from jax.experimental import pallas as pl
from jax.experimental.pallas import tpu as pltpu

f32 = jnp.float32
bf16 = jnp.bfloat16

D_MODEL = 1024
BATCH = 4
SEQ = 4096
DEPTH = 4
DEC_BATCH = 128
DEC_SEQ = 4
PAST_LEN = 8192
N_MIXERS = 3
D_FF = 4 * D_MODEL
EPS = 1e-6
NEG_INF = -1e30

A_CHUNK = 128
A_D_FFN = 6 * D_MODEL
A_HALF = A_D_FFN // 2
A_GROUPS = 8
A_GROUP_W = A_HALF // A_GROUPS

B_HEADS = 16
B_KV_HEADS = 4
B_HEAD_DIM = 64
B_REP = B_HEADS // B_KV_HEADS
B_WINDOW = 128
B_BLOCK = 128
B_Q_DIM = B_HEADS * B_HEAD_DIM
B_KV_DIM = B_KV_HEADS * B_HEAD_DIM
N_BUCKETS = 32
MAX_DISTANCE = 128

C_D_INNER = 2 * D_MODEL
C_HEAD_DIM = 64
C_HEADS = C_D_INNER // C_HEAD_DIM
C_GROUPS = 4
C_REP = C_HEADS // C_GROUPS
C_D_STATE = 128
C_D_CONV = 4
C_BC_DIM = C_GROUPS * C_D_STATE
C_CONV_DIM = C_D_INNER + 2 * C_BC_DIM
C_GROUP_W = C_D_INNER // C_GROUPS
C_CHUNK = 128

LANES = 128
SUBLANES = 8
VMEM_LIMIT_BYTES = 56 * 1024 * 1024

PROMPT_ROWS = BATCH * SEQ
SAMPLE_ROWS = DEC_BATCH * DEC_SEQ
TM_PROMPT = 1024
TM_SAMPLE = SAMPLE_ROWS
MLP_TF = 1024


def _params(*sem):
    return pltpu.CompilerParams(dimension_semantics=sem, vmem_limit_bytes=VMEM_LIMIT_BYTES)


def _rms(x, g):
    ms = jnp.mean(x * x, axis=-1, keepdims=True)
    return x * lax.rsqrt(ms + EPS) * g


def _gelu(x):
    return 0.5 * x * (1.0 + lax.erf(x * math.sqrt(0.5)))


def _silu(x):
    return x * jax.nn.sigmoid(x)


def _softplus(x):
    return jnp.maximum(x, 0.0) + jnp.log1p(jnp.exp(-jnp.abs(x)))


def _dot(a, b):
    return jnp.dot(a, b, preferred_element_type=f32)


def _dot_nt(a, b):
    return lax.dot_general(a, b, (((1,), (1,)), ((), ())), preferred_element_type=f32)


def _dot_tn(a, b):
    return lax.dot_general(a, b, (((0,), (0,)), ((), ())), preferred_element_type=f32)


def _dot_exact_lhs01(a01, x):
    a = a01.astype(bf16)
    hi = x.astype(bf16)
    r1 = x - hi.astype(f32)
    mid = r1.astype(bf16)
    lo = (r1 - mid.astype(f32)).astype(bf16)
    return _dot(a, hi) + _dot(a, mid) + _dot(a, lo)


def _dot_exact_rhs01(x, b01):
    b = b01.astype(bf16)
    hi = x.astype(bf16)
    r1 = x - hi.astype(f32)
    mid = r1.astype(bf16)
    lo = (r1 - mid.astype(f32)).astype(bf16)
    return _dot(hi, b) + _dot(mid, b) + _dot(lo, b)


def _norm_matmul_kernel(*refs, nj, tail, emit):
    refs = list(refs)
    x_ref, g_ref, w_ref = refs[:3]
    wt_ref = refs[3] if tail else None
    outs = refs[3 + tail:-1]
    o_ref = outs[0]
    ot_ref = outs[1] if tail else None
    wb_ref = outs[-1] if emit else None
    xn_ref = refs[-1]
    j = pl.program_id(1)

    @pl.when(j == 0)
    def _():
        xn_ref[...] = _rms(x_ref[...], g_ref[...]).astype(bf16)

    @pl.when(j < nj)
    def _():
        w = w_ref[0].astype(bf16)
        if emit:
            wb_ref[0] = w
        o_ref[...] = _dot(xn_ref[...], w).astype(o_ref.dtype)

    if tail:
        @pl.when(j == nj)
        def _():
            ot_ref[...] = _dot(xn_ref[...], wt_ref[...].astype(bf16))


def norm_matmul(x, g, w, layer, n, *, tm, tn, w_tail=None, out_dtype=f32, emit=False):
    m, k = x.shape
    nj = n // tn
    tail = w_tail is not None
    last = nj - 1
    assert not emit or m == tm
    in_specs = [
        pl.BlockSpec((tm, k), lambda i, j: (i, 0)),
        pl.BlockSpec((1, k), lambda i, j: (0, 0)),
        pl.BlockSpec((1, k, tn), lambda i, j: (layer, 0, jnp.minimum(j, last))),
    ]
    out_specs = [pl.BlockSpec((tm, tn), lambda i, j: (i, jnp.minimum(j, last)))]
    out_shape = [jax.ShapeDtypeStruct((m, n), out_dtype)]
    args = [x, g, w]
    if tail:
        in_specs.append(pl.BlockSpec((k, LANES), lambda i, j: (0, 0)))
        out_specs.append(pl.BlockSpec((tm, LANES), lambda i, j: (i, 0)))
        out_shape.append(jax.ShapeDtypeStruct((m, LANES), f32))
        args.append(w_tail)
    if emit:
        out_specs.append(pl.BlockSpec((1, k, tn), lambda i, j: (0, 0, jnp.minimum(j, last))))
        out_shape.append(jax.ShapeDtypeStruct((1, k, n), bf16))
    out = pl.pallas_call(
        functools.partial(_norm_matmul_kernel, nj=nj, tail=tail, emit=emit),
        grid=(m // tm, nj + (1 if tail else 0)),
        in_specs=in_specs,
        out_specs=out_specs,
        out_shape=out_shape,
        scratch_shapes=[pltpu.VMEM((tm, k), bf16)],
        compiler_params=_params("parallel", "arbitrary"),
        name="norm_matmul",
    )(*args)
    return out if (tail or emit) else out[0]


def _mlp_kernel(*refs, proj, emit):
    refs = list(refs)
    a_ref, wo_ref = (refs.pop(0), refs.pop(0)) if proj else (None, None)
    x_ref, g_ref, wu_ref, wd_ref, o_ref = refs[:5]
    wub_ref, wdb_ref = (refs[5], refs[6]) if emit else (None, None)
    xn_ref = refs[-1]

    @pl.when(pl.program_id(1) == 0)
    def _():
        x = x_ref[...]
        if proj:
            x = x + _dot(a_ref[...], wo_ref[0])
        xn_ref[...] = _rms(x, g_ref[...]).astype(bf16)
        o_ref[...] = x

    wu = wu_ref[0].astype(bf16)
    wd = wd_ref[0].astype(bf16)
    if emit:
        wub_ref[0] = wu
        wdb_ref[0] = wd
    h = jnp.maximum(_dot(xn_ref[...], wu), 0.0)
    o_ref[...] += _dot((h * h).astype(bf16), wd)


def mlp(x, g, w_up, w_down, layer, *, tm, tf, proj=None, emit=False):
    m, d = x.shape
    ff = w_up.shape[2]
    assert not emit or m == tm
    in_specs = [
        pl.BlockSpec((tm, d), lambda i, j: (i, 0)),
        pl.BlockSpec((1, d), lambda i, j: (0, 0)),
        pl.BlockSpec((1, d, tf), lambda i, j: (layer, 0, j)),
        pl.BlockSpec((1, tf, d), lambda i, j: (layer, j, 0)),
    ]
    args = [x, g, w_up, w_down]
    if proj is not None:
        a, w_o, lo = proj
        k = a.shape[1]
        in_specs = [pl.BlockSpec((tm, k), lambda i, j: (i, 0)),
                    pl.BlockSpec((1, k, d), lambda i, j: (lo, 0, 0), pipeline_mode=pl.Buffered(1))] + in_specs
        args = [a, w_o] + args
    out_specs = [pl.BlockSpec((tm, d), lambda i, j: (i, 0))]
    out_shape = [jax.ShapeDtypeStruct((m, d), f32)]
    if emit:
        out_specs += [pl.BlockSpec((1, d, tf), lambda i, j: (0, 0, j)), pl.BlockSpec((1, tf, d), lambda i, j: (0, j, 0))]
        out_shape += [jax.ShapeDtypeStruct((1, d, ff), bf16), jax.ShapeDtypeStruct((1, ff, d), bf16)]
    out = pl.pallas_call(
        functools.partial(_mlp_kernel, proj=proj is not None, emit=emit),
        grid=(m // tm, ff // tf),
        in_specs=in_specs,
        out_specs=out_specs,
        out_shape=out_shape,
        scratch_shapes=[pltpu.VMEM((tm, d), bf16)],
        compiler_params=_params("parallel", "arbitrary"),
        name="mlp",
    )(*args)
    return out if emit else out[0]


A_BLK_GROUPS = 2
A_BLK = A_BLK_GROUPS * A_GROUP_W
A_NBLK = A_HALF // A_BLK


def _mixer_a_kernel(*refs, sample):
    if sample:
        (ws_ref, bs_ref, x_ref, g_ref, win_ref, nv_ref, wout_ref, o_ref, vo_ref, winb_ref, woutb_ref,
         xn_ref, v_ref, ssq_ref, us_ref) = refs
    else:
        x_ref, g_ref, win_ref, nv_ref, ws_ref, bs_ref, wout_ref, o_ref, xn_ref, v_ref, ssq_ref, us_ref = refs
        winb_ref = woutb_ref = None
    j = pl.program_id(1)
    tm = x_ref.shape[0]

    def w_in_block():
        w = win_ref[0].astype(bf16)
        if winb_ref is not None:
            winb_ref[0] = w
        return w

    def w_out_block():
        w = wout_ref[0].astype(bf16)
        if woutb_ref is not None:
            woutb_ref[0] = w
        return w

    @pl.when(j == 0)
    def _():
        xn_ref[...] = _rms(x_ref[...], g_ref[...]).astype(bf16)
        ssq_ref[...] = jnp.zeros(ssq_ref.shape, f32)

    for k in range(A_NBLK):
        @pl.when(j == k)
        def _(k=k):
            v = _gelu(_dot(xn_ref[...], w_in_block()))
            v_ref[:, k * A_BLK:(k + 1) * A_BLK] = v
            ssq_ref[...] += jnp.sum(v * v, axis=-1, keepdims=True)

    if not sample:
        row = lax.broadcasted_iota(jnp.int32, (A_CHUNK, A_CHUNK), 0)
        col = lax.broadcasted_iota(jnp.int32, (A_CHUNK, A_CHUNK), 1)
        causal = row >= col

    for k in range(A_NBLK):
        @pl.when(j == A_NBLK + k)
        def _(k=k):
            u = _gelu(_dot(xn_ref[...], w_in_block()))
            rinv = lax.rsqrt(ssq_ref[...] * (1.0 / A_HALF) + EPS)
            for gg in range(A_BLK_GROUPS):
                g = k * A_BLK_GROUPS + gg
                cols = slice(g * A_GROUP_W, (g + 1) * A_GROUP_W)
                ucols = slice(gg * A_GROUP_W, (gg + 1) * A_GROUP_W)
                vn = v_ref[:, cols] * rinv * nv_ref[:, cols]
                if sample:
                    vo_ref[:, cols] = vn
                    vt = [vn[t * DEC_BATCH:(t + 1) * DEC_BATCH] for t in range(DEC_SEQ)]
                    s_rows = []
                    for t in range(DEC_SEQ):
                        s = ws_ref[(g * DEC_SEQ + t) * DEC_SEQ] * vt[0]
                        for t2 in range(1, t + 1):
                            s = s + ws_ref[(g * DEC_SEQ + t) * DEC_SEQ + t2] * vt[t2]
                        s_rows.append(s + bs_ref[g * DEC_SEQ + t])
                    s = jnp.concatenate(s_rows, axis=0)
                else:
                    w = jnp.where(causal, ws_ref[g], 0.0).astype(bf16)
                    bias = bs_ref[:, g:g + 1]
                    vb = vn.astype(bf16)
                    s = jnp.concatenate(
                        [_dot(w, vb[c * A_CHUNK:(c + 1) * A_CHUNK]) + bias for c in range(tm // A_CHUNK)], axis=0)
                us_ref[:, ucols] = (u[:, ucols] * s).astype(bf16)
            y = _dot(us_ref[...], w_out_block())
            if k == 0:
                o_ref[...] = x_ref[...] + y
            else:
                o_ref[...] += y


def mixer_a(x, g, w_in, norm_v, w_sp, b_sp, w_out, layer, *, tm, sample):
    m, d = x.shape
    nj = 2 * A_NBLK
    row = lambda w: pl.BlockSpec((tm, w), lambda i, j: (i, 0))
    full = lambda *shape: pl.BlockSpec(shape, lambda i, j: (0,) * len(shape))
    smem = pl.BlockSpec(memory_space=pltpu.SMEM)
    win_spec = pl.BlockSpec((1, d, A_BLK), lambda i, j: (layer, 0, (j + A_NBLK) % nj))
    wout_spec = pl.BlockSpec((1, A_BLK, d), lambda i, j: (layer, jnp.maximum(j - A_NBLK, 0), 0))
    if sample:
        in_specs = [smem, smem, row(d), full(1, d), win_spec, full(1, A_HALF), wout_spec]
        args = (w_sp, b_sp, x, g, w_in, norm_v, w_out)
        out_specs = [row(d), row(A_HALF),
                     pl.BlockSpec((1, d, A_BLK), lambda i, j: (0, 0, (j + A_NBLK) % nj)),
                     pl.BlockSpec((1, A_BLK, d), lambda i, j: (0, jnp.maximum(j - A_NBLK, 0), 0))]
        out_shape = [jax.ShapeDtypeStruct((m, d), f32), jax.ShapeDtypeStruct((m, A_HALF), f32),
                     jax.ShapeDtypeStruct((1, d, 2 * A_HALF), bf16), jax.ShapeDtypeStruct((1, A_HALF, d), bf16)]
    else:
        in_specs = [row(d), full(1, d), win_spec, full(1, A_HALF), full(A_GROUPS, A_CHUNK, A_CHUNK),
                    full(A_CHUNK, A_GROUPS), wout_spec]
        args = (x, g, w_in, norm_v, w_sp, b_sp, w_out)
        out_specs = row(d)
        out_shape = jax.ShapeDtypeStruct((m, d), f32)
    return pl.pallas_call(
        functools.partial(_mixer_a_kernel, sample=sample),
        grid=(m // tm, nj),
        in_specs=in_specs,
        out_specs=out_specs,
        out_shape=out_shape,
        scratch_shapes=[pltpu.VMEM((tm, d), bf16), pltpu.VMEM((tm, A_HALF), f32), pltpu.VMEM((tm, 1), f32),
                        pltpu.VMEM((tm, A_BLK), bf16)],
        compiler_params=_params("parallel", "arbitrary"),
        name="mixer_a_sample" if sample else "mixer_a_prompt",
    )(*args)


def _bucket_table():
    i = np.arange(B_BLOCK)[:, None]
    j = np.arange(2 * B_BLOCK)[None, :]
    n = np.maximum(B_BLOCK + i - j, 0)
    max_exact = N_BUCKETS // 2
    nf = np.maximum(n, 1).astype(np.float64)
    val = np.log(nf / max_exact) / math.log(MAX_DISTANCE / max_exact) * (N_BUCKETS - max_exact)
    in_window = (n >= max_exact) & (n < B_WINDOW)
    assert np.all(np.abs(val - np.round(val))[in_window & (n != max_exact)] > 1e-3)
    large = np.minimum(max_exact + np.floor(val + 1e-9).astype(np.int64), N_BUCKETS - 1)
    return np.where(n < max_exact, n, large).astype(np.int32)


def _bias_table_kernel(rb_ref, bk_ref, o_ref):
    bk = bk_ref[...]
    for h in range(B_HEADS):
        acc = jnp.zeros(bk.shape, f32)
        for b in range(N_BUCKETS):
            acc = jnp.where(bk == b, rb_ref[b * B_HEADS + h], acc)
        o_ref[h] = acc


def bias_table(rel_bias):
    return pl.pallas_call(
        _bias_table_kernel,
        in_specs=[pl.BlockSpec(memory_space=pltpu.SMEM), pl.BlockSpec(memory_space=pltpu.VMEM)],
        out_specs=pl.BlockSpec(memory_space=pltpu.VMEM),
        out_shape=jax.ShapeDtypeStruct((B_HEADS, B_BLOCK, 2 * B_BLOCK), f32),
        name="bias_table",
    )(rel_bias.reshape(-1), jnp.asarray(_bucket_table()))


def _softmax_with_sink(logits, sink):
    m = jnp.maximum(jnp.max(logits, axis=-1, keepdims=True), sink)
    p = jnp.exp(logits - m)
    return p, jnp.sum(p, axis=-1, keepdims=True) + jnp.exp(sink - m)


def _rms_head_pairs(x, g2, lo):
    sq = x * x
    s_lo = jnp.sum(jnp.where(lo, sq, 0.0), axis=-1, keepdims=True)
    s_hi = jnp.sum(jnp.where(lo, 0.0, sq), axis=-1, keepdims=True)
    r = lax.rsqrt(jnp.where(lo, s_lo, s_hi) * (1.0 / B_HEAD_DIM) + EPS)
    return x * r * g2


ATT_SUB = 8


def _attn_prompt_kernel(sink_ref, x_ref, g_ref, w_ref, tab_ref, qn_ref, kn_ref, o_ref, ko_ref, vo_ref,
                        xn_ref, kband_ref, vband_ref, q_ref, p_ref, rhs_ref):
    n = pl.program_id(1)
    T = B_BLOCK

    @pl.when(n == 0)
    def _():
        kband_ref[0:T, :] = jnp.zeros((T, B_KV_DIM), f32)
        vband_ref[0:T, :] = jnp.zeros((T, B_KV_DIM), f32)
        rhs_ref[...] = jnp.ones(rhs_ref.shape, bf16)

    xn_ref[...] = _rms(x_ref[...], g_ref[...]).astype(bf16)
    for sub in range(ATT_SUB):
        rows = pl.ds(sub * T, T)
        qkv = _dot(xn_ref[rows, :], w_ref[0])
        first_key = jnp.where(n == 0, T, 0) if sub == 0 else 0
        _attn_block(first_key, sink_ref, qkv, tab_ref, qn_ref, kn_ref, o_ref.at[rows], ko_ref, vo_ref,
                    kband_ref, vband_ref, q_ref.at[sub], p_ref.at[sub], rhs_ref.at[sub])


def _attn_block(first_key, sink_ref, qkv_ref, tab_ref, qn_ref, kn_ref, o_ref, ko_ref, vo_ref,
                kband_ref, vband_ref, q_ref, p_ref, rhs_ref):
    T = B_BLOCK
    lo = lax.broadcasted_iota(jnp.int32, (1, LANES), 1) < B_HEAD_DIM
    qn2 = qn_ref[...]
    kn2 = kn_ref[...]
    for t in range(B_KV_DIM // LANES):
        lanes = slice(t * LANES, (t + 1) * LANES)
        k2 = _rms_head_pairs(qkv_ref[:, B_Q_DIM + t * LANES:B_Q_DIM + (t + 1) * LANES].astype(f32), kn2, lo)
        ko_ref[0, :, lanes] = k2
        kband_ref[T:, lanes] = k2
    v = qkv_ref[:, B_Q_DIM + B_KV_DIM:].astype(f32)
    vo_ref[0] = v
    vband_ref[T:, :] = v
    for t in range(B_Q_DIM // LANES):
        q2 = qkv_ref[:, t * LANES:(t + 1) * LANES].astype(f32)
        q_ref[t * T:(t + 1) * T, :] = _rms_head_pairs(q2, qn2, lo).astype(bf16)

    key_ops = {}
    band_row = lax.broadcasted_iota(jnp.int32, (2 * T, LANES), 0)
    for t in range(B_KV_DIM // LANES):
        lanes = slice(t * LANES, (t + 1) * LANES)
        kt = kband_ref[:, lanes]
        kr = pltpu.roll(kt, B_HEAD_DIM, axis=1)
        vt = jnp.where(band_row == 0, 0.0, vband_ref[:, lanes])
        vr = pltpu.roll(vt, B_HEAD_DIM, axis=1)
        hi = jnp.logical_not(lo)
        for half, (ksrc, vsrc) in enumerate(((kt, vt), (kr, vr))):
            g_lo, g_hi = (2 * t, 2 * t + 1) if half == 0 else (2 * t + 1, 2 * t)
            key_ops[(g_lo, 0)] = jnp.where(lo, ksrc, 0.0).astype(bf16)
            key_ops[(g_hi, 1)] = jnp.where(hi, ksrc, 0.0).astype(bf16)
            rhs_ref[g_lo * 2 + 0, :, 0:LANES] = jnp.where(lo, vsrc, 1.0).astype(bf16)
            rhs_ref[g_hi * 2 + 1, :, 0:LANES] = jnp.where(hi, vsrc, 1.0).astype(bf16)

    i = lax.broadcasted_iota(jnp.int32, (T, 2 * T), 0)
    j = lax.broadcasted_iota(jnp.int32, (T, 2 * T), 1)
    valid = (j > i) & (j <= i + B_WINDOW) & (j >= first_key)
    sink_col = j == 0
    lo_t = lax.broadcasted_iota(jnp.int32, (T, LANES), 1) < B_HEAD_DIM
    for g in range(B_KV_HEADS):
        q2 = q_ref[2 * g * T:(2 * g + 2) * T, :]
        res = []
        for half in range(2):
            logits = _dot_nt(q2, key_ops[(g, half)]) * (B_HEAD_DIM ** -0.5)
            for pair in range(2):
                h = g * B_REP + 2 * pair + half
                l = jnp.where(valid, logits[pair * T:(pair + 1) * T] + tab_ref[h], NEG_INF)
                l = jnp.where(sink_col, sink_ref[h], l)
                p = jnp.exp(l - jnp.max(l, axis=-1, keepdims=True))
                p_ref[g * 2 + half, pair * T:(pair + 1) * T, :] = p.astype(bf16)
            res.append(_dot(p_ref[g * 2 + half], rhs_ref[g * 2 + half]))
        for pair in range(2):
            rows = slice(pair * T, (pair + 1) * T)
            even = res[0][rows, 0:LANES] / res[0][rows, LANES:]
            odd = res[1][rows, 0:LANES] / res[1][rows, LANES:]
            t = 2 * g + pair
            o_ref[:, t * LANES:(t + 1) * LANES] = jnp.where(lo_t, even, odd).astype(bf16)
    kband_ref[0:T, :] = kband_ref[T:, :]
    vband_ref[0:T, :] = vband_ref[T:, :]


def attn_prompt(x, g, w_qkv, layer, table, q_norm, k_norm, sinks):
    step = ATT_SUB * B_BLOCK
    nb = SEQ // step
    return pl.pallas_call(
        _attn_prompt_kernel,
        grid=(BATCH, nb),
        in_specs=[
            pl.BlockSpec(memory_space=pltpu.SMEM),
            pl.BlockSpec((step, D_MODEL), lambda b, n: (b * nb + n, 0)),
            pl.BlockSpec((1, D_MODEL), lambda b, n: (0, 0)),
            pl.BlockSpec((1,) + w_qkv.shape[1:], lambda b, n: (layer, 0, 0), pipeline_mode=pl.Buffered(1)),
            pl.BlockSpec((B_HEADS, B_BLOCK, 2 * B_BLOCK), lambda b, n: (0, 0, 0)),
            pl.BlockSpec((1, LANES), lambda b, n: (0, 0)),
            pl.BlockSpec((1, LANES), lambda b, n: (0, 0)),
        ],
        out_specs=[
            pl.BlockSpec((step, B_Q_DIM), lambda b, n: (b * nb + n, 0)),
            pl.BlockSpec((1, B_BLOCK, B_KV_DIM), lambda b, n: (b, 0, 0)),
            pl.BlockSpec((1, B_BLOCK, B_KV_DIM), lambda b, n: (b, 0, 0)),
        ],
        out_shape=[
            jax.ShapeDtypeStruct((PROMPT_ROWS, B_Q_DIM), bf16),
            jax.ShapeDtypeStruct((BATCH, B_BLOCK, B_KV_DIM), f32),
            jax.ShapeDtypeStruct((BATCH, B_BLOCK, B_KV_DIM), f32),
        ],
        scratch_shapes=[
            pltpu.VMEM((step, D_MODEL), bf16),
            pltpu.VMEM((2 * B_BLOCK, B_KV_DIM), f32),
            pltpu.VMEM((2 * B_BLOCK, B_KV_DIM), f32),
            pltpu.VMEM((ATT_SUB, B_Q_DIM // LANES * B_BLOCK, LANES), bf16),
            pltpu.VMEM((ATT_SUB, 2 * B_KV_HEADS, 2 * B_BLOCK, 2 * B_BLOCK), bf16),
            pltpu.VMEM((ATT_SUB, 2 * B_KV_HEADS, 2 * B_BLOCK, 2 * LANES), bf16),
        ],
        compiler_params=_params("parallel", "arbitrary"),
        name="attn_prompt",
    )(sinks, x, g, w_qkv, table, jnp.tile(q_norm, (1, LANES // B_HEAD_DIM)),
      jnp.tile(k_norm, (1, LANES // B_HEAD_DIM)))


ATT_BB = 8
ATT_QR = B_REP * DEC_SEQ
ATT_KEYS = B_WINDOW + 2 * DEC_SEQ


def _attn_sample_kernel(sink_ref, q_ref, kn_ref, vn_ref, kc_ref, vc_ref, tab_ref, qn_ref, knm_ref,
                        o_ref, ko_ref, vo_ref, kall_ref, vall_ref):
    qn2 = qn_ref[...]
    kn2 = knm_ref[...]
    n_rows = B_KV_HEADS * ATT_QR
    lo = lax.broadcasted_iota(jnp.int32, (1, LANES), 1) < B_HEAD_DIM
    row = lax.broadcasted_iota(jnp.int32, (n_rows, ATT_KEYS), 0)
    j = lax.broadcasted_iota(jnp.int32, (n_rows, ATT_KEYS), 1)
    t = row % DEC_SEQ
    valid = (j > t) & (j <= t + B_WINDOW)
    q_row_group = lax.broadcasted_iota(jnp.int32, (n_rows, B_KV_DIM), 0) // ATT_QR
    q_lane_group = lax.broadcasted_iota(jnp.int32, (n_rows, B_KV_DIM), 1) // B_HEAD_DIM
    own_group = q_row_group == q_lane_group
    o_lane_group = lax.broadcasted_iota(jnp.int32, (ATT_QR, B_KV_DIM), 1) // B_HEAD_DIM
    pad = jnp.zeros((ATT_KEYS - B_WINDOW - DEC_SEQ, B_KV_DIM), f32)
    bias = tab_ref[...]
    sink = sink_ref[...]
    for s in range(ATT_BB):
        kc = kc_ref[s]
        vc = vc_ref[s]
        k_new = jnp.concatenate(
            [_rms_head_pairs(kn_ref[s, :, tt * LANES:(tt + 1) * LANES], kn2, lo) for tt in range(B_KV_DIM // LANES)],
            axis=1)
        v_new = vn_ref[s]
        ko_ref[s, 0:B_WINDOW - DEC_SEQ, :] = kc[DEC_SEQ:, :]
        ko_ref[s, B_WINDOW - DEC_SEQ:, :] = k_new
        vo_ref[s, 0:B_WINDOW - DEC_SEQ, :] = vc[DEC_SEQ:, :]
        vo_ref[s, B_WINDOW - DEC_SEQ:, :] = v_new
        kall_ref[s, 0:B_WINDOW, :] = kc
        kall_ref[s, B_WINDOW:B_WINDOW + DEC_SEQ, :] = k_new
        kall_ref[s, B_WINDOW + DEC_SEQ:, :] = pad
        vall_ref[s, 0:B_WINDOW, :] = vc
        vall_ref[s, B_WINDOW:B_WINDOW + DEC_SEQ, :] = v_new
        vall_ref[s, B_WINDOW + DEC_SEQ:, :] = pad
        q = q_ref[s]
        qn = jnp.concatenate(
            [_rms_head_pairs(q[:, tt * LANES:(tt + 1) * LANES], qn2, lo) for tt in range(B_KV_DIM // LANES)], axis=1)
        q_all = jnp.where(own_group, jnp.concatenate([qn] * B_KV_HEADS, axis=0), 0.0).astype(bf16)
        logits = _dot_nt(q_all, kall_ref[s].astype(bf16)) * (B_HEAD_DIM ** -0.5)
        logits = jnp.where(valid, logits + bias, NEG_INF)
        p, denom = _softmax_with_sink(logits, sink)
        res = _dot(p.astype(bf16), vall_ref[s].astype(bf16)) / denom
        out = jnp.zeros((ATT_QR, B_KV_DIM), f32)
        for g in range(B_KV_HEADS):
            out = jnp.where(o_lane_group == g, res[g * ATT_QR:(g + 1) * ATT_QR, :], out)
        o_ref[s] = out


def attn_sample(q_s, k_new, v_new, k_cache, v_cache, table_s, q_norm, k_norm, sinks):
    blk = lambda *shape: pl.BlockSpec((ATT_BB,) + shape, lambda i: (i,) + (0,) * len(shape))
    full = lambda *shape: pl.BlockSpec(shape, lambda i: (0,) * len(shape))
    return pl.pallas_call(
        _attn_sample_kernel,
        grid=(DEC_BATCH // ATT_BB,),
        in_specs=[
            full(B_KV_HEADS * ATT_QR, 1),
            blk(ATT_QR, B_KV_DIM), blk(DEC_SEQ, B_KV_DIM), blk(DEC_SEQ, B_KV_DIM),
            blk(B_WINDOW, B_KV_DIM), blk(B_WINDOW, B_KV_DIM),
            full(B_KV_HEADS * ATT_QR, ATT_KEYS), full(1, LANES), full(1, LANES),
        ],
        out_specs=[blk(ATT_QR, B_KV_DIM), blk(B_WINDOW, B_KV_DIM), blk(B_WINDOW, B_KV_DIM)],
        out_shape=[
            jax.ShapeDtypeStruct((DEC_BATCH, ATT_QR, B_KV_DIM), f32),
            jax.ShapeDtypeStruct((DEC_BATCH, B_WINDOW, B_KV_DIM), f32),
            jax.ShapeDtypeStruct((DEC_BATCH, B_WINDOW, B_KV_DIM), f32),
        ],
        scratch_shapes=[pltpu.VMEM((ATT_BB, ATT_KEYS, B_KV_DIM), f32), pltpu.VMEM((ATT_BB, ATT_KEYS, B_KV_DIM), f32)],
        compiler_params=_params("parallel"),
        name="attn_sample",
    )(jnp.repeat(sinks, DEC_SEQ).reshape(B_KV_HEADS * ATT_QR, 1), q_s, k_new, v_new, k_cache, v_cache,
      table_s.reshape(B_KV_HEADS * ATT_QR, ATT_KEYS),
      jnp.tile(q_norm, (1, LANES // B_HEAD_DIM)), jnp.tile(k_norm, (1, LANES // B_HEAD_DIM)))


CONV_PAD = SUBLANES

def _gated_group_norm(y, z, norm_w):
    gt = y * _silu(z)
    parts = []
    for g in range(C_GROUPS):
        gg = gt[:, g * C_GROUP_W:(g + 1) * C_GROUP_W]
        parts.append(gg * lax.rsqrt(jnp.mean(gg * gg, axis=-1, keepdims=True) + EPS))
    return jnp.concatenate(parts, axis=1) * norm_w


LOG2E = math.log2(math.e)


def _expand_heads(v, sel3):
    lane = lax.broadcasted_iota(jnp.int32, (1, LANES), 1)
    v = jnp.where(lane < C_HEADS, v, 0.0)
    hi = v.astype(bf16).astype(f32)
    r1 = v - hi
    mid = r1.astype(bf16).astype(f32)
    lo = r1 - mid
    packed = hi + pltpu.roll(mid, C_HEADS, axis=1) + pltpu.roll(lo, 2 * C_HEADS, axis=1)
    return _dot(packed.astype(bf16), sel3)


SSD_SUB = 4


def _ssd_prompt_kernel(x_ref, g_ref, w_ref, wdt_ref, cw_ref, cb_ref, dtb_ref, alog_ref, dsk_ref, nw_ref, sel_ref,
                       yn_ref, hfin_ref, cout_ref, xn_ref, xpad_ref, ht_ref, y_ref):
    c = pl.program_id(1)
    n_zx = C_D_INNER + C_CONV_DIM

    @pl.when(c == 0)
    def _():
        xpad_ref[0, 0:CONV_PAD, :] = jnp.zeros((CONV_PAD, C_CONV_DIM), f32)
        ht_ref[...] = jnp.zeros(ht_ref.shape, f32)

    xn_ref[...] = _rms(x_ref[...], g_ref[...]).astype(bf16)
    for sub in range(SSD_SUB):
        rows = pl.ds(sub * C_CHUNK, C_CHUNK)
        xn = xn_ref[rows, :]
        zx = _dot(xn, w_ref[0, :, 0:n_zx])
        dtr = _dot(xn, wdt_ref[...])
        _ssd_chunk(zx[:, :C_D_INNER], zx[:, C_D_INNER:], dtr, cw_ref, cb_ref, dtb_ref, alog_ref, dsk_ref, nw_ref,
                   sel_ref, yn_ref.at[rows], cout_ref, xpad_ref.at[sub], xpad_ref.at[(sub + 1) % SSD_SUB], ht_ref,
                   y_ref.at[sub])

    @pl.when(c == pl.num_programs(1) - 1)
    def _():
        for t in range(C_D_INNER // LANES):
            hfin_ref[0, t * LANES:(t + 1) * LANES, :] = ht_ref[:, t * LANES:(t + 1) * LANES].T


def _ssd_chunk(z, xbc, dtr, cw_ref, cb_ref, dtb_ref, alog_ref, dsk_ref, nw_ref, sel_ref,
               yn_ref, cout_ref, xpad_ref, xpad_next_ref, ht_ref, y_ref):
    T = C_CHUNK
    xpad_ref[CONV_PAD:, :] = xbc
    xp = xpad_ref[...]
    cw = cw_ref[...]
    acc = cb_ref[...]
    for tap in range(C_D_CONV - 1):
        shifted = pltpu.roll(xp, C_D_CONV - 1 - tap, axis=0)[CONV_PAD:, :]
        acc = acc + shifted * cw[tap:tap + 1, :]
    acc = acc + xbc * cw[C_D_CONV - 1:C_D_CONV, :]
    xpad_next_ref[0:CONV_PAD, :] = xbc[T - CONV_PAD:, :]
    cout_ref[0] = xbc[T - (C_D_CONV - 1):, :]
    act = _silu(acc)
    xs = act[:, :C_D_INNER]
    bm = act[:, C_D_INNER:C_D_INNER + C_BC_DIM]
    cm = act[:, C_D_INNER + C_BC_DIM:]
    xb = xs.astype(bf16)

    dt = _softplus(dtr + dtb_ref[...])
    a_neg = -jnp.exp(alog_ref[...])
    row = lax.broadcasted_iota(jnp.int32, (T, T), 0)
    col = lax.broadcasted_iota(jnp.int32, (T, T), 1)
    causal = row >= col
    acs = _dot_exact_lhs01(causal.astype(f32), dt * a_neg)
    a2 = acs * LOG2E
    sel3 = sel_ref[...]
    e_exp = jnp.exp2(_expand_heads(a2, sel3))
    w_exp = _expand_heads(jnp.exp(acs[T - 1:T, :] - acs) * dt, sel3)
    cdec = e_exp[T - 1:T, :]
    b2_t = a2.T - jnp.log2(dt.T)
    xw = (xs * w_exp).astype(bf16)
    hb = ht_ref[...].astype(bf16)
    dsk = dsk_ref[...]
    lo_t = lax.broadcasted_iota(jnp.int32, (T, LANES), 1) < C_HEAD_DIM

    for g in range(C_GROUPS):
        ns = slice(g * C_D_STATE, (g + 1) * C_D_STATE)
        gs = slice(g * C_GROUP_W, (g + 1) * C_GROUP_W)
        b_g = bm[:, ns]
        c_g = cm[:, ns].astype(bf16)
        cb = _dot_nt(c_g, b_g.astype(bf16))
        yi = _dot(c_g, hb[:, gs])
        ht_ref[:, gs] = ht_ref[:, gs] * cdec[:, gs] + _dot(b_g.T.astype(bf16), xw[:, gs])
        for tt in range(C_GROUP_W // LANES):
            t = g * (C_GROUP_W // LANES) + tt
            lanes = slice(t * LANES, (t + 1) * LANES)
            xt = xb[:, lanes]
            res = []
            for half in range(2):
                h = 2 * t + half
                a_col = jnp.broadcast_to(a2[:, h:h + 1], (T, T))
                b_row = jnp.broadcast_to(b2_t[h:h + 1, :], (T, T))
                w = jnp.where(causal, cb * jnp.exp2(a_col - b_row), 0.0)
                res.append(_dot(w.astype(bf16), xt))
            y_intra = jnp.where(lo_t, res[0], res[1])
            y_ref[:, lanes] = y_intra + e_exp[:, lanes] * yi[:, tt * LANES:(tt + 1) * LANES] + dsk[:, lanes] * xs[:, lanes]

    yn_ref[...] = _gated_group_norm(y_ref[...], z, nw_ref[...]).astype(bf16)


def _head_select3():
    k = np.arange(LANES)[:, None]
    ch = np.arange(C_D_INNER)[None, :] // C_HEAD_DIM
    return jnp.asarray((k % C_HEADS == ch) & (k < 3 * C_HEADS), dtype=bf16)


def ssd_prompt(x, g, w_in, layer, w_dt, conv_w, conv_b, dt_bias, a_log, d_skip, norm_w):
    step = SSD_SUB * C_CHUNK
    nc = SEQ // step
    full = lambda *shape: pl.BlockSpec(shape, lambda b, c: (0,) * len(shape))
    once = lambda *shape: pl.BlockSpec(shape, lambda b, c: (0,) * len(shape), pipeline_mode=pl.Buffered(1))
    return pl.pallas_call(
        _ssd_prompt_kernel,
        grid=(BATCH, nc),
        in_specs=[
            pl.BlockSpec((step, D_MODEL), lambda b, c: (b * nc + c, 0)),
            full(1, D_MODEL),
            pl.BlockSpec((1,) + w_in.shape[1:], lambda b, c: (layer, 0, 0), pipeline_mode=pl.Buffered(1)),
            once(D_MODEL, LANES),
            full(C_D_CONV, C_CONV_DIM), full(1, C_CONV_DIM), full(1, LANES), full(1, LANES),
            full(1, C_D_INNER), full(1, C_D_INNER), full(LANES, C_D_INNER),
        ],
        out_specs=[
            pl.BlockSpec((step, C_D_INNER), lambda b, c: (b * nc + c, 0)),
            pl.BlockSpec((1, C_D_INNER, C_D_STATE), lambda b, c: (b, 0, 0)),
            pl.BlockSpec((1, C_D_CONV - 1, C_CONV_DIM), lambda b, c: (b, 0, 0)),
        ],
        out_shape=[
            jax.ShapeDtypeStruct((PROMPT_ROWS, C_D_INNER), bf16),
            jax.ShapeDtypeStruct((BATCH, C_D_INNER, C_D_STATE), f32),
            jax.ShapeDtypeStruct((BATCH, C_D_CONV - 1, C_CONV_DIM), f32),
        ],
        scratch_shapes=[
            pltpu.VMEM((step, D_MODEL), bf16),
            pltpu.VMEM((SSD_SUB, CONV_PAD + C_CHUNK, C_CONV_DIM), f32),
            pltpu.VMEM((C_D_STATE, C_D_INNER), f32),
            pltpu.VMEM((SSD_SUB, C_CHUNK, C_D_INNER), f32),
        ],
        compiler_params=_params("parallel", "arbitrary"),
        name="ssd_prompt",
    )(x, g, w_in, w_dt, conv_w, conv_b, dt_bias, a_log, d_skip, norm_w, _head_select3())


SSD_BB = 8
SSD_TP = SUBLANES
_N_PAIRS = DEC_SEQ * (DEC_SEQ + 1) // 2
_N_COEF = _N_PAIRS + 2 * DEC_SEQ


def _ssd_sample_kernel(zx_ref, dtr_ref, cs_ref, h0_ref, cw_ref, cb_ref, dtb_ref, alog_ref, dsk_ref, nw_ref,
                       sel_ref, yn_ref, hn_ref, cout_ref, c_scr, b_scr, xw_scr, yi_scr, cd_scr):
    L = DEC_SEQ
    cw = cw_ref[...]
    xp = [cs_ref[k] for k in range(C_D_CONV - 1)] + [zx_ref[t, :, C_D_INNER:] for t in range(L)]
    for k in range(C_D_CONV - 1):
        cout_ref[k] = xp[L + k]
    act = []
    for t in range(L):
        acc = cb_ref[...]
        for tap in range(C_D_CONV):
            acc = acc + xp[t + tap] * cw[tap:tap + 1, :]
        act.append(_silu(acc))
    xs = [a[:, :C_D_INNER] for a in act]
    bm = [a[:, C_D_INNER:C_D_INNER + C_BC_DIM] for a in act]
    cm = [a[:, C_D_INNER + C_BC_DIM:] for a in act]

    a_neg = -jnp.exp(alog_ref[...])
    dt = [_softplus(dtr_ref[t] + dtb_ref[...]) for t in range(L)]
    acs = []
    for t in range(L):
        acs.append(dt[t] * a_neg if t == 0 else acs[t - 1] + dt[t] * a_neg)

    lane_group = lax.broadcasted_iota(jnp.int32, (SSD_BB, LANES), 1) // C_REP
    coefs = []
    for t in range(L):
        for t2 in range(t + 1):
            cbh = jnp.zeros((SSD_BB, LANES), f32)
            for g in range(C_GROUPS):
                ns = slice(g * C_D_STATE, (g + 1) * C_D_STATE)
                cbg = jnp.sum(cm[t][:, ns] * bm[t2][:, ns], axis=-1, keepdims=True)
                cbh = jnp.where(lane_group == g, cbg, cbh)
            coefs.append(cbh * jnp.exp(acs[t] - acs[t2]) * dt[t2])
    for t in range(L):
        coefs.append(jnp.exp(acs[t]))
    for t in range(L):
        coefs.append(jnp.exp(acs[L - 1] - acs[t]) * dt[t])
    coef = jnp.concatenate(coefs, axis=0)
    cexp = _dot_exact_rhs01(coef, sel_ref[...])
    cexp = [cexp[k * SSD_BB:(k + 1) * SSD_BB, :] for k in range(_N_COEF)]
    w_intra = cexp[:_N_PAIRS]
    w_inter = cexp[_N_PAIRS:_N_PAIRS + L]
    w_state = cexp[_N_PAIRS + L:]

    cd = jnp.concatenate([jnp.exp(acs[L - 1]), jnp.zeros((LANES - SSD_BB, LANES), f32)], axis=0)
    cd_t = cd.T
    for s in range(SSD_BB):
        cd_scr[s] = jnp.broadcast_to(cd_t[0:C_HEADS, s:s + 1], (C_HEADS, C_D_STATE))

    zeros_tail = jnp.zeros((SSD_BB, SSD_TP - L, C_D_INNER), f32)
    c_scr[:, L:, :] = zeros_tail[:, :, :C_BC_DIM]
    b_scr[:, L:, :] = zeros_tail[:, :, :C_BC_DIM]
    xw_scr[:, L:, :] = zeros_tail
    for t in range(L):
        xw_t = xs[t] * w_state[t]
        for s in range(SSD_BB):
            c_scr[s, t:t + 1, :] = cm[t][s:s + 1, :]
            b_scr[s, t:t + 1, :] = bm[t][s:s + 1, :]
            xw_scr[s, t:t + 1, :] = xw_t[s:s + 1, :]

    for s in range(SSD_BB):
        for g in range(C_GROUPS):
            ns = slice(g * C_D_STATE, (g + 1) * C_D_STATE)
            gs = slice(g * C_GROUP_W, (g + 1) * C_GROUP_W)
            h0 = h0_ref[s, gs, :]
            yi = _dot_nt(c_scr[s, :, ns].astype(bf16), h0.astype(bf16))
            for t in range(L):
                yi_scr[t, s:s + 1, gs] = yi[t:t + 1, :]
            st = _dot_tn(xw_scr[s, :, gs].astype(bf16), b_scr[s, :, ns].astype(bf16))
            for r in range(C_REP):
                h = g * C_REP + r
                rs = slice(r * C_HEAD_DIM, (r + 1) * C_HEAD_DIM)
                scale = cd_scr[s, h:h + 1, :]
                hn_ref[s, h * C_HEAD_DIM:(h + 1) * C_HEAD_DIM, :] = h0[rs, :] * scale + st[rs, :]

    dsk = dsk_ref[...]
    nw = nw_ref[...]
    pair = 0
    for t in range(L):
        y = w_inter[t] * yi_scr[t] + dsk * xs[t]
        for t2 in range(t + 1):
            y = y + w_intra[pair] * xs[t2]
            pair += 1
        yn_ref[t] = _gated_group_norm(y, zx_ref[t, :, :C_D_INNER], nw).astype(bf16)


def ssd_sample(zx_t, dtr_t, conv_state_t, h0, conv_w, conv_b, dt_bias, a_log, d_skip, norm_w, sel):
    tmaj = lambda n, w: pl.BlockSpec((n, SSD_BB, w), lambda i: (0, i, 0))
    full = lambda *shape: pl.BlockSpec(shape, lambda i: (0,) * len(shape))
    return pl.pallas_call(
        _ssd_sample_kernel,
        grid=(DEC_BATCH // SSD_BB,),
        in_specs=[
            tmaj(DEC_SEQ, C_D_INNER + C_CONV_DIM), tmaj(DEC_SEQ, LANES), tmaj(C_D_CONV - 1, C_CONV_DIM),
            pl.BlockSpec((SSD_BB, C_D_INNER, C_D_STATE), lambda i: (i, 0, 0)),
            full(C_D_CONV, C_CONV_DIM), full(1, C_CONV_DIM), full(1, LANES), full(1, LANES),
            full(1, C_D_INNER), full(1, C_D_INNER), full(LANES, C_D_INNER),
        ],
        out_specs=[
            tmaj(DEC_SEQ, C_D_INNER),
            pl.BlockSpec((SSD_BB, C_D_INNER, C_D_STATE), lambda i: (i, 0, 0)),
            tmaj(C_D_CONV - 1, C_CONV_DIM),
        ],
        out_shape=[
            jax.ShapeDtypeStruct((DEC_SEQ, DEC_BATCH, C_D_INNER), bf16),
            jax.ShapeDtypeStruct((DEC_BATCH, C_D_INNER, C_D_STATE), f32),
            jax.ShapeDtypeStruct((C_D_CONV - 1, DEC_BATCH, C_CONV_DIM), f32),
        ],
        scratch_shapes=[
            pltpu.VMEM((SSD_BB, SSD_TP, C_BC_DIM), f32),
            pltpu.VMEM((SSD_BB, SSD_TP, C_BC_DIM), f32),
            pltpu.VMEM((SSD_BB, SSD_TP, C_D_INNER), f32),
            pltpu.VMEM((DEC_SEQ, SSD_BB, C_D_INNER), f32),
            pltpu.VMEM((SSD_BB, C_HEADS, C_D_STATE), f32),
        ],
        compiler_params=_params("parallel"),
        name="ssd_sample",
    )(zx_t, dtr_t, conv_state_t, h0, conv_w, conv_b, dt_bias, a_log, d_skip, norm_w, sel)


def _pad_lanes(v):
    return jnp.pad(v.astype(f32), (0, LANES - v.shape[0])).reshape(1, LANES)


def _mixer_a(xp, xs, g, j, w_in, norm_v, w_sp, b_sp, w_out):
    nv = norm_v.reshape(1, A_HALF)
    xs, v_s, w_in_b, w_out_b = mixer_a(xs, g, w_in, nv, w_sp[:, :DEC_SEQ, :DEC_SEQ].reshape(-1),
                                       b_sp[:, :DEC_SEQ].reshape(-1), w_out, j, tm=TM_SAMPLE, sample=True)
    xp = mixer_a(xp, g, w_in_b, nv, w_sp, b_sp.T, w_out_b, 0, tm=TM_PROMPT, sample=False)
    return xp, xs, jnp.swapaxes(v_s.reshape(DEC_SEQ, DEC_BATCH, A_HALF), 0, 1)


def _mixer_b(xp, xs, g, j, k_cache, v_cache, w_qkv, q_norm, k_norm, sinks, rel_bias):
    qn = q_norm.reshape(1, B_HEAD_DIM)
    kn = k_norm.reshape(1, B_HEAD_DIM)
    table = bias_table(rel_bias)
    n_qkv = B_Q_DIM + 2 * B_KV_DIM

    o_p, k_p, v_p = attn_prompt(xp, g, w_qkv, j, table, qn, kn, sinks)

    qkv_s = norm_matmul(xs, g, w_qkv, j, n_qkv, tm=TM_SAMPLE, tn=512).reshape(DEC_SEQ, DEC_BATCH, -1)
    q_s = qkv_s[:, :, :B_Q_DIM].reshape(DEC_SEQ, DEC_BATCH, B_KV_HEADS, B_REP, B_HEAD_DIM)
    q_s = q_s.transpose(1, 3, 0, 2, 4).reshape(DEC_BATCH, ATT_QR, B_KV_DIM)
    k_new = jnp.swapaxes(qkv_s[:, :, B_Q_DIM:B_Q_DIM + B_KV_DIM], 0, 1)
    v_new = jnp.swapaxes(qkv_s[:, :, B_Q_DIM + B_KV_DIM:], 0, 1)
    table_s = table[:, :DEC_SEQ, :ATT_KEYS].reshape(B_KV_HEADS, ATT_QR, ATT_KEYS)
    o_s, k_s, v_s = attn_sample(q_s, k_new, v_new,
                                k_cache.reshape(DEC_BATCH, B_WINDOW, B_KV_DIM),
                                v_cache.reshape(DEC_BATCH, B_WINDOW, B_KV_DIM),
                                table_s, qn, kn, sinks)
    o_s = o_s.reshape(DEC_BATCH, B_REP, DEC_SEQ, B_KV_HEADS, B_HEAD_DIM).transpose(2, 0, 3, 1, 4)
    o_s = o_s.reshape(SAMPLE_ROWS, B_Q_DIM).astype(bf16)
    kv_shape_p = (BATCH, B_WINDOW, B_KV_HEADS, B_HEAD_DIM)
    kv_shape_s = (DEC_BATCH, B_WINDOW, B_KV_HEADS, B_HEAD_DIM)
    return o_p, o_s, k_p.reshape(kv_shape_p), v_p.reshape(kv_shape_p), k_s.reshape(kv_shape_s), v_s.reshape(kv_shape_s)


def _mixer_c(xp, xs, g, j, h0, conv_state, w_in, conv_w, conv_b, dt_bias, a_log, d_skip, norm_w):
    n_zx = C_D_INNER + C_CONV_DIM
    w_dt = jnp.pad(w_in[j, :, n_zx:], ((0, 0), (0, LANES - C_HEADS))).astype(bf16)
    cb = conv_b.reshape(1, C_CONV_DIM)
    dtb = _pad_lanes(dt_bias)
    alog = _pad_lanes(a_log)
    dsk = jnp.repeat(d_skip.astype(f32), C_HEAD_DIM).reshape(1, C_D_INNER)
    nw = norm_w.reshape(1, C_D_INNER)

    zx_s, dtr_s, w_zx_b = norm_matmul(xs, g, w_in, j, n_zx, tm=TM_SAMPLE, tn=1024, w_tail=w_dt, emit=True)
    yn_p, h_p, conv_p = ssd_prompt(xp, g, w_zx_b, 0, w_dt, conv_w, cb, dtb, alog, dsk, nw)

    sel = (jnp.arange(LANES)[:, None] == jnp.arange(C_D_INNER)[None, :] // C_HEAD_DIM).astype(f32)
    yn_s, h_s, conv_s = ssd_sample(
        zx_s.reshape(DEC_SEQ, DEC_BATCH, -1), dtr_s.reshape(DEC_SEQ, DEC_BATCH, LANES),
        jnp.swapaxes(conv_state, 0, 1), h0.reshape(DEC_BATCH, C_D_INNER, C_D_STATE),
        conv_w, cb, dtb, alog, dsk, nw, sel)
    st_shape = (C_HEADS, C_HEAD_DIM, C_D_STATE)
    return (yn_p, yn_s.reshape(SAMPLE_ROWS, C_D_INNER), h_p.reshape((BATCH,) + st_shape), conv_p,
            h_s.reshape((DEC_BATCH,) + st_shape), jnp.swapaxes(conv_s, 0, 1))


def kernel(x_prompt, x_sample, cache_swa_k, cache_swa_v, state_ssm, state_conv, norm_mixer, norm_mlp, mlp_w_up, mlp_w_down, a_w_in, a_norm_v, a_w_spatial, a_b_spatial, a_w_out, b_w_qkv, b_q_norm, b_k_norm, b_sinks, rel_bias, b_w_out, c_w_in, c_conv_w, c_conv_b, c_dt_bias, c_a_log, c_d, c_norm, c_w_out):
    xp = x_prompt.reshape(PROMPT_ROWS, D_MODEL)
    xs = jnp.swapaxes(x_sample, 0, 1).reshape(SAMPLE_ROWS, D_MODEL)
    chunk_v_s = []
    swa_kp, swa_vp, swa_ks, swa_vs = [], [], [], []
    ssm_p, conv_p, ssm_s, conv_s = [], [], [], []
    b_w_qkv, b_w_out, c_w_out = b_w_qkv.astype(bf16), b_w_out.astype(bf16), c_w_out.astype(bf16)
    for i in range(DEPTH):
        kind = i % N_MIXERS
        j = i // N_MIXERS
        g = norm_mixer[i].reshape(1, D_MODEL)
        proj_p = proj_s = None
        if kind == 0:
            xp, xs, v_new = _mixer_a(xp, xs, g, j, a_w_in, a_norm_v[j], a_w_spatial[j], a_b_spatial[j], a_w_out)
            chunk_v_s.append(v_new)
        elif kind == 1:
            o_p, o_s, kp, vp, ks_, vs_ = _mixer_b(xp, xs, g, j, cache_swa_k[j], cache_swa_v[j], b_w_qkv, b_q_norm[j],
                                                  b_k_norm[j], b_sinks[j], rel_bias)
            proj_p, proj_s = (o_p, b_w_out, j), (o_s, b_w_out, j)
            swa_kp.append(kp); swa_vp.append(vp); swa_ks.append(ks_); swa_vs.append(vs_)
        else:
            y_p, y_s, hp, bp, hs, bs = _mixer_c(xp, xs, g, j, state_ssm[j], state_conv[j], c_w_in, c_conv_w[j],
                                                c_conv_b[j], c_dt_bias[j], c_a_log[j], c_d[j], c_norm[j])
            proj_p, proj_s = (y_p, c_w_out, j), (y_s, c_w_out, j)
            ssm_p.append(hp); conv_p.append(bp); ssm_s.append(hs); conv_s.append(bs)
        gm = norm_mlp[i].reshape(1, D_MODEL)
        xs, w_up_b, w_down_b = mlp(xs, gm, mlp_w_up, mlp_w_down, i, tm=TM_SAMPLE, tf=MLP_TF, proj=proj_s, emit=True)
        xp = mlp(xp, gm, w_up_b, w_down_b, 0, tm=TM_PROMPT, tf=MLP_TF if proj_p else 2 * MLP_TF, proj=proj_p)
    y_prompt = xp.reshape(BATCH, SEQ, D_MODEL)
    y_sample = jnp.swapaxes(xs.reshape(DEC_SEQ, DEC_BATCH, D_MODEL), 0, 1)
    return (y_prompt, y_sample, jnp.stack(chunk_v_s),
            jnp.stack(swa_kp), jnp.stack(swa_vp), jnp.stack(swa_ks), jnp.stack(swa_vs),
            jnp.stack(ssm_p), jnp.stack(conv_p), jnp.stack(ssm_s), jnp.stack(conv_s))
```

```python
import functools
import math

import jax
import jax.numpy as jnp
import numpy as np
from jax import lax
from jax.experimental import pallas as pl
from jax.experimental.pallas import tpu as pltpu

f32 = jnp.float32
bf16 = jnp.bfloat16

D_MODEL = 1024
BATCH = 4
SEQ = 4096
DEPTH = 4
DEC_BATCH = 128
DEC_SEQ = 4
PAST_LEN = 8192
N_MIXERS = 3
D_FF = 4 * D_MODEL
EPS = 1e-6
NEG_INF = -1e30

A_CHUNK = 128
A_D_FFN = 6 * D_MODEL
A_HALF = A_D_FFN // 2
A_GROUPS = 8
A_GROUP_W = A_HALF // A_GROUPS

B_HEADS = 16
B_KV_HEADS = 4
B_HEAD_DIM = 64
B_REP = B_HEADS // B_KV_HEADS
B_WINDOW = 128
B_BLOCK = 128
B_Q_DIM = B_HEADS * B_HEAD_DIM
B_KV_DIM = B_KV_HEADS * B_HEAD_DIM
N_BUCKETS = 32
MAX_DISTANCE = 128

C_D_INNER = 2 * D_MODEL
C_HEAD_DIM = 64
C_HEADS = C_D_INNER // C_HEAD_DIM
C_GROUPS = 4
C_REP = C_HEADS // C_GROUPS
C_D_STATE = 128
C_D_CONV = 4
C_BC_DIM = C_GROUPS * C_D_STATE
C_CONV_DIM = C_D_INNER + 2 * C_BC_DIM
C_GROUP_W = C_D_INNER // C_GROUPS
C_CHUNK = 128

LANES = 128
SUBLANES = 8
VMEM_LIMIT_BYTES = 56 * 1024 * 1024

PROMPT_ROWS = BATCH * SEQ
SAMPLE_ROWS = DEC_BATCH * DEC_SEQ
TM_PROMPT = 1024
TM_SAMPLE = SAMPLE_ROWS
MLP_TF = 1024


def _params(*sem):
    return pltpu.CompilerParams(dimension_semantics=sem, vmem_limit_bytes=VMEM_LIMIT_BYTES)


def _rms(x, g):
    ms = jnp.mean(x * x, axis=-1, keepdims=True)
    return x * lax.rsqrt(ms + EPS) * g


def _gelu(x):
    return 0.5 * x * (1.0 + lax.erf(x * math.sqrt(0.5)))


def _silu(x):
    return x * jax.nn.sigmoid(x)


def _softplus(x):
    return jnp.maximum(x, 0.0) + jnp.log1p(jnp.exp(-jnp.abs(x)))


def _dot(a, b):
    return jnp.dot(a, b, preferred_element_type=f32)


def _dot_nt(a, b):
    return lax.dot_general(a, b, (((1,), (1,)), ((), ())), preferred_element_type=f32)


def _dot_tn(a, b):
    return lax.dot_general(a, b, (((0,), (0,)), ((), ())), preferred_element_type=f32)


def _dot_exact_lhs01(a01, x):
    a = a01.astype(bf16)
    hi = x.astype(bf16)
    r1 = x - hi.astype(f32)
    mid = r1.astype(bf16)
    lo = (r1 - mid.astype(f32)).astype(bf16)
    return _dot(a, hi) + _dot(a, mid) + _dot(a, lo)


def _dot_exact_rhs01(x, b01):
    b = b01.astype(bf16)
    hi = x.astype(bf16)
    r1 = x - hi.astype(f32)
    mid = r1.astype(bf16)
    lo = (r1 - mid.astype(f32)).astype(bf16)
    return _dot(hi, b) + _dot(mid, b) + _dot(lo, b)


def _norm_matmul_kernel(*refs, nj, tail):
    if tail:
        x_ref, g_ref, w_ref, wt_ref, o_ref, ot_ref, xn_ref = refs
    else:
        x_ref, g_ref, w_ref, o_ref, xn_ref = refs
    j = pl.program_id(1)

    @pl.when(j == 0)
    def _():
        xn_ref[...] = _rms(x_ref[...], g_ref[...]).astype(bf16)

    @pl.when(j < nj)
    def _():
        o_ref[...] = _dot(xn_ref[...], w_ref[0].astype(bf16))

    if tail:
        @pl.when(j == nj)
        def _():
            ot_ref[...] = _dot(xn_ref[...], wt_ref[...].astype(bf16))


def norm_matmul(x, g, w, layer, n, *, tm, tn, w_tail=None):
    m, k = x.shape
    nj = n // tn
    tail = w_tail is not None
    last = nj - 1
    in_specs = [
        pl.BlockSpec((tm, k), lambda i, j: (i, 0)),
        pl.BlockSpec((1, k), lambda i, j: (0, 0)),
        pl.BlockSpec((1, k, tn), lambda i, j: (layer, 0, jnp.minimum(j, last))),
    ]
    out_specs = [pl.BlockSpec((tm, tn), lambda i, j: (i, jnp.minimum(j, last)))]
    out_shape = [jax.ShapeDtypeStruct((m, n), f32)]
    args = [x, g, w]
    if tail:
        in_specs.append(pl.BlockSpec((k, LANES), lambda i, j: (0, 0)))
        out_specs.append(pl.BlockSpec((tm, LANES), lambda i, j: (i, 0)))
        out_shape.append(jax.ShapeDtypeStruct((m, LANES), f32))
        args.append(w_tail)
    out = pl.pallas_call(
        functools.partial(_norm_matmul_kernel, nj=nj, tail=tail),
        grid=(m // tm, nj + (1 if tail else 0)),
        in_specs=in_specs,
        out_specs=out_specs,
        out_shape=out_shape,
        scratch_shapes=[pltpu.VMEM((tm, k), bf16)],
        compiler_params=_params("parallel", "arbitrary"),
        name="norm_matmul",
    )(*args)
    return out if tail else out[0]


def _mlp_kernel(*refs, proj, emit):
    refs = list(refs)
    a_ref, wo_ref = (refs.pop(0), refs.pop(0)) if proj else (None, None)
    x_ref, g_ref, wu_ref, wd_ref, o_ref = refs[:5]
    wub_ref, wdb_ref = (refs[5], refs[6]) if emit else (None, None)
    xn_ref = refs[-1]

    @pl.when(pl.program_id(1) == 0)
    def _():
        x = x_ref[...]
        if proj:
            x = x + _dot(a_ref[...], wo_ref[0])
        xn_ref[...] = _rms(x, g_ref[...]).astype(bf16)
        o_ref[...] = x

    wu = wu_ref[0].astype(bf16)
    wd = wd_ref[0].astype(bf16)
    if emit:
        wub_ref[0] = wu
        wdb_ref[0] = wd
    h = jnp.maximum(_dot(xn_ref[...], wu), 0.0)
    o_ref[...] += _dot((h * h).astype(bf16), wd)


def mlp(x, g, w_up, w_down, layer, *, tm, tf, proj=None, emit=False):
    m, d = x.shape
    ff = w_up.shape[2]
    assert not emit or m == tm
    in_specs = [
        pl.BlockSpec((tm, d), lambda i, j: (i, 0)),
        pl.BlockSpec((1, d), lambda i, j: (0, 0)),
        pl.BlockSpec((1, d, tf), lambda i, j: (layer, 0, j)),
        pl.BlockSpec((1, tf, d), lambda i, j: (layer, j, 0)),
    ]
    args = [x, g, w_up, w_down]
    if proj is not None:
        a, w_o, lo = proj
        k = a.shape[1]
        in_specs = [pl.BlockSpec((tm, k), lambda i, j: (i, 0)),
                    pl.BlockSpec((1, k, d), lambda i, j: (lo, 0, 0), pipeline_mode=pl.Buffered(1))] + in_specs
        args = [a, w_o] + args
    out_specs = [pl.BlockSpec((tm, d), lambda i, j: (i, 0))]
    out_shape = [jax.ShapeDtypeStruct((m, d), f32)]
    if emit:
        out_specs += [pl.BlockSpec((1, d, tf), lambda i, j: (0, 0, j)), pl.BlockSpec((1, tf, d), lambda i, j: (0, j, 0))]
        out_shape += [jax.ShapeDtypeStruct((1, d, ff), bf16), jax.ShapeDtypeStruct((1, ff, d), bf16)]
    out = pl.pallas_call(
        functools.partial(_mlp_kernel, proj=proj is not None, emit=emit),
        grid=(m // tm, ff // tf),
        in_specs=in_specs,
        out_specs=out_specs,
        out_shape=out_shape,
        scratch_shapes=[pltpu.VMEM((tm, d), bf16)],
        compiler_params=_params("parallel", "arbitrary"),
        name="mlp",
    )(*args)
    return out if emit else out[0]


A_BLK_GROUPS = 2
A_BLK = A_BLK_GROUPS * A_GROUP_W
A_NBLK = A_HALF // A_BLK


def _mixer_a_kernel(*refs, sample):
    if sample:
        (ws_ref, bs_ref, x_ref, g_ref, win_ref, nv_ref, wout_ref, o_ref, vo_ref, winb_ref, woutb_ref,
         xn_ref, v_ref, ssq_ref, us_ref) = refs
    else:
        x_ref, g_ref, win_ref, nv_ref, ws_ref, bs_ref, wout_ref, o_ref, xn_ref, v_ref, ssq_ref, us_ref = refs
        winb_ref = woutb_ref = None
    j = pl.program_id(1)
    tm = x_ref.shape[0]

    def w_in_block():
        w = win_ref[0].astype(bf16)
        if winb_ref is not None:
            winb_ref[0] = w
        return w

    def w_out_block():
        w = wout_ref[0].astype(bf16)
        if woutb_ref is not None:
            woutb_ref[0] = w
        return w

    @pl.when(j == 0)
    def _():
        xn_ref[...] = _rms(x_ref[...], g_ref[...]).astype(bf16)
        ssq_ref[...] = jnp.zeros(ssq_ref.shape, f32)

    for k in range(A_NBLK):
        @pl.when(j == k)
        def _(k=k):
            v = _gelu(_dot(xn_ref[...], w_in_block()))
            v_ref[:, k * A_BLK:(k + 1) * A_BLK] = v
            ssq_ref[...] += jnp.sum(v * v, axis=-1, keepdims=True)

    if not sample:
        row = lax.broadcasted_iota(jnp.int32, (A_CHUNK, A_CHUNK), 0)
        col = lax.broadcasted_iota(jnp.int32, (A_CHUNK, A_CHUNK), 1)
        causal = row >= col

    for k in range(A_NBLK):
        @pl.when(j == A_NBLK + k)
        def _(k=k):
            u = _gelu(_dot(xn_ref[...], w_in_block()))
            rinv = lax.rsqrt(ssq_ref[...] * (1.0 / A_HALF) + EPS)
            for gg in range(A_BLK_GROUPS):
                g = k * A_BLK_GROUPS + gg
                cols = slice(g * A_GROUP_W, (g + 1) * A_GROUP_W)
                ucols = slice(gg * A_GROUP_W, (gg + 1) * A_GROUP_W)
                vn = v_ref[:, cols] * rinv * nv_ref[:, cols]
                if sample:
                    vo_ref[:, cols] = vn
                    vt = [vn[t * DEC_BATCH:(t + 1) * DEC_BATCH] for t in range(DEC_SEQ)]
                    s_rows = []
                    for t in range(DEC_SEQ):
                        s = ws_ref[(g * DEC_SEQ + t) * DEC_SEQ] * vt[0]
                        for t2 in range(1, t + 1):
                            s = s + ws_ref[(g * DEC_SEQ + t) * DEC_SEQ + t2] * vt[t2]
                        s_rows.append(s + bs_ref[g * DEC_SEQ + t])
                    s = jnp.concatenate(s_rows, axis=0)
                else:
                    w = jnp.where(causal, ws_ref[g], 0.0).astype(bf16)
                    bias = bs_ref[:, g:g + 1]
                    vb = vn.astype(bf16)
                    s = jnp.concatenate(
                        [_dot(w, vb[c * A_CHUNK:(c + 1) * A_CHUNK]) + bias for c in range(tm // A_CHUNK)], axis=0)
                us_ref[:, ucols] = (u[:, ucols] * s).astype(bf16)
            y = _dot(us_ref[...], w_out_block())
            if k == 0:
                o_ref[...] = x_ref[...] + y
            else:
                o_ref[...] += y


def mixer_a(x, g, w_in, norm_v, w_sp, b_sp, w_out, layer, *, tm, sample):
    m, d = x.shape
    nj = 2 * A_NBLK
    row = lambda w: pl.BlockSpec((tm, w), lambda i, j: (i, 0))
    full = lambda *shape: pl.BlockSpec(shape, lambda i, j: (0,) * len(shape))
    smem = pl.BlockSpec(memory_space=pltpu.SMEM)
    win_spec = pl.BlockSpec((1, d, A_BLK), lambda i, j: (layer, 0, (j + A_NBLK) % nj))
    wout_spec = pl.BlockSpec((1, A_BLK, d), lambda i, j: (layer, jnp.maximum(j - A_NBLK, 0), 0))
    if sample:
        in_specs = [smem, smem, row(d), full(1, d), win_spec, full(1, A_HALF), wout_spec]
        args = (w_sp, b_sp, x, g, w_in, norm_v, w_out)
        out_specs = [row(d), row(A_HALF),
                     pl.BlockSpec((1, d, A_BLK), lambda i, j: (0, 0, (j + A_NBLK) % nj)),
                     pl.BlockSpec((1, A_BLK, d), lambda i, j: (0, jnp.maximum(j - A_NBLK, 0), 0))]
        out_shape = [jax.ShapeDtypeStruct((m, d), f32), jax.ShapeDtypeStruct((m, A_HALF), f32),
                     jax.ShapeDtypeStruct((1, d, 2 * A_HALF), bf16), jax.ShapeDtypeStruct((1, A_HALF, d), bf16)]
    else:
        in_specs = [row(d), full(1, d), win_spec, full(1, A_HALF), full(A_GROUPS, A_CHUNK, A_CHUNK),
                    full(A_CHUNK, A_GROUPS), wout_spec]
        args = (x, g, w_in, norm_v, w_sp, b_sp, w_out)
        out_specs = row(d)
        out_shape = jax.ShapeDtypeStruct((m, d), f32)
    return pl.pallas_call(
        functools.partial(_mixer_a_kernel, sample=sample),
        grid=(m // tm, nj),
        in_specs=in_specs,
        out_specs=out_specs,
        out_shape=out_shape,
        scratch_shapes=[pltpu.VMEM((tm, d), bf16), pltpu.VMEM((tm, A_HALF), f32), pltpu.VMEM((tm, 1), f32),
                        pltpu.VMEM((tm, A_BLK), bf16)],
        compiler_params=_params("parallel", "arbitrary"),
        name="mixer_a_sample" if sample else "mixer_a_prompt",
    )(*args)


def _bucket_table():
    i = np.arange(B_BLOCK)[:, None]
    j = np.arange(2 * B_BLOCK)[None, :]
    n = np.maximum(B_BLOCK + i - j, 0)
    max_exact = N_BUCKETS // 2
    nf = np.maximum(n, 1).astype(np.float64)
    val = np.log(nf / max_exact) / math.log(MAX_DISTANCE / max_exact) * (N_BUCKETS - max_exact)
    in_window = (n >= max_exact) & (n < B_WINDOW)
    assert np.all(np.abs(val - np.round(val))[in_window & (n != max_exact)] > 1e-3)
    large = np.minimum(max_exact + np.floor(val + 1e-9).astype(np.int64), N_BUCKETS - 1)
    return np.where(n < max_exact, n, large).astype(np.int32)


def _bias_table_kernel(rb_ref, bk_ref, o_ref):
    bk = bk_ref[...]
    for h in range(B_HEADS):
        acc = jnp.zeros(bk.shape, f32)
        for b in range(N_BUCKETS):
            acc = jnp.where(bk == b, rb_ref[b * B_HEADS + h], acc)
        o_ref[h] = acc


def bias_table(rel_bias):
    return pl.pallas_call(
        _bias_table_kernel,
        in_specs=[pl.BlockSpec(memory_space=pltpu.SMEM), pl.BlockSpec(memory_space=pltpu.VMEM)],
        out_specs=pl.BlockSpec(memory_space=pltpu.VMEM),
        out_shape=jax.ShapeDtypeStruct((B_HEADS, B_BLOCK, 2 * B_BLOCK), f32),
        name="bias_table",
    )(rel_bias.reshape(-1), jnp.asarray(_bucket_table()))


def _softmax_with_sink(logits, sink):
    m = jnp.maximum(jnp.max(logits, axis=-1, keepdims=True), sink)
    p = jnp.exp(logits - m)
    return p, jnp.sum(p, axis=-1, keepdims=True) + jnp.exp(sink - m)


def _rms_head_pairs(x, g2, lo):
    sq = x * x
    s_lo = jnp.sum(jnp.where(lo, sq, 0.0), axis=-1, keepdims=True)
    s_hi = jnp.sum(jnp.where(lo, 0.0, sq), axis=-1, keepdims=True)
    r = lax.rsqrt(jnp.where(lo, s_lo, s_hi) * (1.0 / B_HEAD_DIM) + EPS)
    return x * r * g2


ATT_SUB = 8


def _attn_prompt_kernel(sink_ref, x_ref, g_ref, w_ref, tab_ref, qn_ref, kn_ref, o_ref, ko_ref, vo_ref,
                        xn_ref, kband_ref, vband_ref, q_ref, p_ref, rhs_ref):
    n = pl.program_id(1)
    T = B_BLOCK

    @pl.when(n == 0)
    def _():
        kband_ref[0:T, :] = jnp.zeros((T, B_KV_DIM), f32)
        vband_ref[0:T, :] = jnp.zeros((T, B_KV_DIM), f32)
        rhs_ref[...] = jnp.ones(rhs_ref.shape, bf16)

    xn_ref[...] = _rms(x_ref[...], g_ref[...]).astype(bf16)
    for sub in range(ATT_SUB):
        rows = pl.ds(sub * T, T)
        qkv = _dot(xn_ref[rows, :], w_ref[0])
        first_key = jnp.where(n == 0, T, 0) if sub == 0 else 0
        _attn_block(first_key, sink_ref, qkv, tab_ref, qn_ref, kn_ref, o_ref.at[rows], ko_ref, vo_ref,
                    kband_ref, vband_ref, q_ref.at[sub], p_ref.at[sub], rhs_ref.at[sub])


def _attn_block(first_key, sink_ref, qkv_ref, tab_ref, qn_ref, kn_ref, o_ref, ko_ref, vo_ref,
                kband_ref, vband_ref, q_ref, p_ref, rhs_ref):
    T = B_BLOCK
    lo = lax.broadcasted_iota(jnp.int32, (1, LANES), 1) < B_HEAD_DIM
    qn2 = qn_ref[...]
    kn2 = kn_ref[...]
    for t in range(B_KV_DIM // LANES):
        lanes = slice(t * LANES, (t + 1) * LANES)
        k2 = _rms_head_pairs(qkv_ref[:, B_Q_DIM + t * LANES:B_Q_DIM + (t + 1) * LANES].astype(f32), kn2, lo)
        ko_ref[0, :, lanes] = k2
        kband_ref[T:, lanes] = k2
    v = qkv_ref[:, B_Q_DIM + B_KV_DIM:].astype(f32)
    vo_ref[0] = v
    vband_ref[T:, :] = v
    for t in range(B_Q_DIM // LANES):
        q2 = qkv_ref[:, t * LANES:(t + 1) * LANES].astype(f32)
        q_ref[t * T:(t + 1) * T, :] = _rms_head_pairs(q2, qn2, lo).astype(bf16)

    key_ops = {}
    band_row = lax.broadcasted_iota(jnp.int32, (2 * T, LANES), 0)
    for t in range(B_KV_DIM // LANES):
        lanes = slice(t * LANES, (t + 1) * LANES)
        kt = kband_ref[:, lanes]
        kr = pltpu.roll(kt, B_HEAD_DIM, axis=1)
        vt = jnp.where(band_row == 0, 0.0, vband_ref[:, lanes])
        vr = pltpu.roll(vt, B_HEAD_DIM, axis=1)
        hi = jnp.logical_not(lo)
        for half, (ksrc, vsrc) in enumerate(((kt, vt), (kr, vr))):
            g_lo, g_hi = (2 * t, 2 * t + 1) if half == 0 else (2 * t + 1, 2 * t)
            key_ops[(g_lo, 0)] = jnp.where(lo, ksrc, 0.0).astype(bf16)
            key_ops[(g_hi, 1)] = jnp.where(hi, ksrc, 0.0).astype(bf16)
            rhs_ref[g_lo * 2 + 0, :, 0:LANES] = jnp.where(lo, vsrc, 1.0).astype(bf16)
            rhs_ref[g_hi * 2 + 1, :, 0:LANES] = jnp.where(hi, vsrc, 1.0).astype(bf16)

    i = lax.broadcasted_iota(jnp.int32, (T, 2 * T), 0)
    j = lax.broadcasted_iota(jnp.int32, (T, 2 * T), 1)
    valid = (j > i) & (j <= i + B_WINDOW) & (j >= first_key)
    sink_col = j == 0
    lo_t = lax.broadcasted_iota(jnp.int32, (T, LANES), 1) < B_HEAD_DIM
    for g in range(B_KV_HEADS):
        q2 = q_ref[2 * g * T:(2 * g + 2) * T, :]
        res = []
        for half in range(2):
            logits = _dot_nt(q2, key_ops[(g, half)]) * (B_HEAD_DIM ** -0.5)
            for pair in range(2):
                h = g * B_REP + 2 * pair + half
                l = jnp.where(valid, logits[pair * T:(pair + 1) * T] + tab_ref[h], NEG_INF)
                l = jnp.where(sink_col, sink_ref[h], l)
                p = jnp.exp(l - jnp.max(l, axis=-1, keepdims=True))
                p_ref[g * 2 + half, pair * T:(pair + 1) * T, :] = p.astype(bf16)
            res.append(_dot(p_ref[g * 2 + half], rhs_ref[g * 2 + half]))
        for pair in range(2):
            rows = slice(pair * T, (pair + 1) * T)
            even = res[0][rows, 0:LANES] / res[0][rows, LANES:]
            odd = res[1][rows, 0:LANES] / res[1][rows, LANES:]
            t = 2 * g + pair
            o_ref[:, t * LANES:(t + 1) * LANES] = jnp.where(lo_t, even, odd).astype(bf16)
    kband_ref[0:T, :] = kband_ref[T:, :]
    vband_ref[0:T, :] = vband_ref[T:, :]


def attn_prompt(x, g, w_qkv, layer, table, q_norm, k_norm, sinks):
    step = ATT_SUB * B_BLOCK
    nb = SEQ // step
    return pl.pallas_call(
        _attn_prompt_kernel,
        grid=(BATCH, nb),
        in_specs=[
            pl.BlockSpec(memory_space=pltpu.SMEM),
            pl.BlockSpec((step, D_MODEL), lambda b, n: (b * nb + n, 0)),
            pl.BlockSpec((1, D_MODEL), lambda b, n: (0, 0)),
            pl.BlockSpec((1,) + w_qkv.shape[1:], lambda b, n: (layer, 0, 0), pipeline_mode=pl.Buffered(1)),
            pl.BlockSpec((B_HEADS, B_BLOCK, 2 * B_BLOCK), lambda b, n: (0, 0, 0)),
            pl.BlockSpec((1, LANES), lambda b, n: (0, 0)),
            pl.BlockSpec((1, LANES), lambda b, n: (0, 0)),
        ],
        out_specs=[
            pl.BlockSpec((step, B_Q_DIM), lambda b, n: (b * nb + n, 0)),
            pl.BlockSpec((1, B_BLOCK, B_KV_DIM), lambda b, n: (b, 0, 0)),
            pl.BlockSpec((1, B_BLOCK, B_KV_DIM), lambda b, n: (b, 0, 0)),
        ],
        out_shape=[
            jax.ShapeDtypeStruct((PROMPT_ROWS, B_Q_DIM), bf16),
            jax.ShapeDtypeStruct((BATCH, B_BLOCK, B_KV_DIM), f32),
            jax.ShapeDtypeStruct((BATCH, B_BLOCK, B_KV_DIM), f32),
        ],
        scratch_shapes=[
            pltpu.VMEM((step, D_MODEL), bf16),
            pltpu.VMEM((2 * B_BLOCK, B_KV_DIM), f32),
            pltpu.VMEM((2 * B_BLOCK, B_KV_DIM), f32),
            pltpu.VMEM((ATT_SUB, B_Q_DIM // LANES * B_BLOCK, LANES), bf16),
            pltpu.VMEM((ATT_SUB, 2 * B_KV_HEADS, 2 * B_BLOCK, 2 * B_BLOCK), bf16),
            pltpu.VMEM((ATT_SUB, 2 * B_KV_HEADS, 2 * B_BLOCK, 2 * LANES), bf16),
        ],
        compiler_params=_params("parallel", "arbitrary"),
        name="attn_prompt",
    )(sinks, x, g, w_qkv, table, jnp.tile(q_norm, (1, LANES // B_HEAD_DIM)),
      jnp.tile(k_norm, (1, LANES // B_HEAD_DIM)))


ATT_BB = 8
ATT_QR = B_REP * DEC_SEQ
ATT_KEYS = B_WINDOW + 2 * DEC_SEQ


def _attn_sample_kernel(sink_ref, q_ref, kn_ref, vn_ref, kc_ref, vc_ref, tab_ref, qn_ref, knm_ref,
                        o_ref, ko_ref, vo_ref, kall_ref, vall_ref):
    qn2 = qn_ref[...]
    kn2 = knm_ref[...]
    n_rows = B_KV_HEADS * ATT_QR
    lo = lax.broadcasted_iota(jnp.int32, (1, LANES), 1) < B_HEAD_DIM
    row = lax.broadcasted_iota(jnp.int32, (n_rows, ATT_KEYS), 0)
    j = lax.broadcasted_iota(jnp.int32, (n_rows, ATT_KEYS), 1)
    t = row % DEC_SEQ
    valid = (j > t) & (j <= t + B_WINDOW)
    q_row_group = lax.broadcasted_iota(jnp.int32, (n_rows, B_KV_DIM), 0) // ATT_QR
    q_lane_group = lax.broadcasted_iota(jnp.int32, (n_rows, B_KV_DIM), 1) // B_HEAD_DIM
    own_group = q_row_group == q_lane_group
    o_lane_group = lax.broadcasted_iota(jnp.int32, (ATT_QR, B_KV_DIM), 1) // B_HEAD_DIM
    pad = jnp.zeros((ATT_KEYS - B_WINDOW - DEC_SEQ, B_KV_DIM), f32)
    bias = tab_ref[...]
    sink = sink_ref[...]
    for s in range(ATT_BB):
        kc = kc_ref[s]
        vc = vc_ref[s]
        k_new = jnp.concatenate(
            [_rms_head_pairs(kn_ref[s, :, tt * LANES:(tt + 1) * LANES], kn2, lo) for tt in range(B_KV_DIM // LANES)],
            axis=1)
        v_new = vn_ref[s]
        ko_ref[s, 0:B_WINDOW - DEC_SEQ, :] = kc[DEC_SEQ:, :]
        ko_ref[s, B_WINDOW - DEC_SEQ:, :] = k_new
        vo_ref[s, 0:B_WINDOW - DEC_SEQ, :] = vc[DEC_SEQ:, :]
        vo_ref[s, B_WINDOW - DEC_SEQ:, :] = v_new
        kall_ref[s, 0:B_WINDOW, :] = kc
        kall_ref[s, B_WINDOW:B_WINDOW + DEC_SEQ, :] = k_new
        kall_ref[s, B_WINDOW + DEC_SEQ:, :] = pad
        vall_ref[s, 0:B_WINDOW, :] = vc
        vall_ref[s, B_WINDOW:B_WINDOW + DEC_SEQ, :] = v_new
        vall_ref[s, B_WINDOW + DEC_SEQ:, :] = pad
        q = q_ref[s]
        qn = jnp.concatenate(
            [_rms_head_pairs(q[:, tt * LANES:(tt + 1) * LANES], qn2, lo) for tt in range(B_KV_DIM // LANES)], axis=1)
        q_all = jnp.where(own_group, jnp.concatenate([qn] * B_KV_HEADS, axis=0), 0.0).astype(bf16)
        logits = _dot_nt(q_all, kall_ref[s].astype(bf16)) * (B_HEAD_DIM ** -0.5)
        logits = jnp.where(valid, logits + bias, NEG_INF)
        p, denom = _softmax_with_sink(logits, sink)
        res = _dot(p.astype(bf16), vall_ref[s].astype(bf16)) / denom
        out = jnp.zeros((ATT_QR, B_KV_DIM), f32)
        for g in range(B_KV_HEADS):
            out = jnp.where(o_lane_group == g, res[g * ATT_QR:(g + 1) * ATT_QR, :], out)
        o_ref[s] = out


def attn_sample(q_s, k_new, v_new, k_cache, v_cache, table_s, q_norm, k_norm, sinks):
    blk = lambda *shape: pl.BlockSpec((ATT_BB,) + shape, lambda i: (i,) + (0,) * len(shape))
    full = lambda *shape: pl.BlockSpec(shape, lambda i: (0,) * len(shape))
    return pl.pallas_call(
        _attn_sample_kernel,
        grid=(DEC_BATCH // ATT_BB,),
        in_specs=[
            full(B_KV_HEADS * ATT_QR, 1),
            blk(ATT_QR, B_KV_DIM), blk(DEC_SEQ, B_KV_DIM), blk(DEC_SEQ, B_KV_DIM),
            blk(B_WINDOW, B_KV_DIM), blk(B_WINDOW, B_KV_DIM),
            full(B_KV_HEADS * ATT_QR, ATT_KEYS), full(1, LANES), full(1, LANES),
        ],
        out_specs=[blk(ATT_QR, B_KV_DIM), blk(B_WINDOW, B_KV_DIM), blk(B_WINDOW, B_KV_DIM)],
        out_shape=[
            jax.ShapeDtypeStruct((DEC_BATCH, ATT_QR, B_KV_DIM), f32),
            jax.ShapeDtypeStruct((DEC_BATCH, B_WINDOW, B_KV_DIM), f32),
            jax.ShapeDtypeStruct((DEC_BATCH, B_WINDOW, B_KV_DIM), f32),
        ],
        scratch_shapes=[pltpu.VMEM((ATT_BB, ATT_KEYS, B_KV_DIM), f32), pltpu.VMEM((ATT_BB, ATT_KEYS, B_KV_DIM), f32)],
        compiler_params=_params("parallel"),
        name="attn_sample",
    )(jnp.repeat(sinks, DEC_SEQ).reshape(B_KV_HEADS * ATT_QR, 1), q_s, k_new, v_new, k_cache, v_cache,
      table_s.reshape(B_KV_HEADS * ATT_QR, ATT_KEYS),
      jnp.tile(q_norm, (1, LANES // B_HEAD_DIM)), jnp.tile(k_norm, (1, LANES // B_HEAD_DIM)))


CONV_PAD = SUBLANES

def _gated_group_norm(y, z, norm_w):
    gt = y * _silu(z)
    parts = []
    for g in range(C_GROUPS):
        gg = gt[:, g * C_GROUP_W:(g + 1) * C_GROUP_W]
        parts.append(gg * lax.rsqrt(jnp.mean(gg * gg, axis=-1, keepdims=True) + EPS))
    return jnp.concatenate(parts, axis=1) * norm_w


LOG2E = math.log2(math.e)


def _expand_heads(v, sel3):
    lane = lax.broadcasted_iota(jnp.int32, (1, LANES), 1)
    v = jnp.where(lane < C_HEADS, v, 0.0)
    hi = v.astype(bf16).astype(f32)
    r1 = v - hi
    mid = r1.astype(bf16).astype(f32)
    lo = r1 - mid
    packed = hi + pltpu.roll(mid, C_HEADS, axis=1) + pltpu.roll(lo, 2 * C_HEADS, axis=1)
    return _dot(packed.astype(bf16), sel3)


SSD_SUB = 4


def _ssd_prompt_kernel(x_ref, g_ref, w_ref, wdt_ref, cw_ref, cb_ref, dtb_ref, alog_ref, dsk_ref, nw_ref, sel_ref,
                       yn_ref, hfin_ref, cout_ref, xn_ref, xpad_ref, ht_ref, y_ref):
    c = pl.program_id(1)
    n_zx = C_D_INNER + C_CONV_DIM

    @pl.when(c == 0)
    def _():
        xpad_ref[0, 0:CONV_PAD, :] = jnp.zeros((CONV_PAD, C_CONV_DIM), f32)
        ht_ref[...] = jnp.zeros(ht_ref.shape, f32)

    xn_ref[...] = _rms(x_ref[...], g_ref[...]).astype(bf16)
    for sub in range(SSD_SUB):
        rows = pl.ds(sub * C_CHUNK, C_CHUNK)
        xn = xn_ref[rows, :]
        zx = _dot(xn, w_ref[0, :, 0:n_zx])
        dtr = _dot(xn, wdt_ref[...])
        _ssd_chunk(zx[:, :C_D_INNER], zx[:, C_D_INNER:], dtr, cw_ref, cb_ref, dtb_ref, alog_ref, dsk_ref, nw_ref,
                   sel_ref, yn_ref.at[rows], cout_ref, xpad_ref.at[sub], xpad_ref.at[(sub + 1) % SSD_SUB], ht_ref,
                   y_ref.at[sub])

    @pl.when(c == pl.num_programs(1) - 1)
    def _():
        for t in range(C_D_INNER // LANES):
            hfin_ref[0, t * LANES:(t + 1) * LANES, :] = ht_ref[:, t * LANES:(t + 1) * LANES].T


def _ssd_chunk(z, xbc, dtr, cw_ref, cb_ref, dtb_ref, alog_ref, dsk_ref, nw_ref, sel_ref,
               yn_ref, cout_ref, xpad_ref, xpad_next_ref, ht_ref, y_ref):
    T = C_CHUNK
    xpad_ref[CONV_PAD:, :] = xbc
    xp = xpad_ref[...]
    cw = cw_ref[...]
    acc = cb_ref[...]
    for tap in range(C_D_CONV - 1):
        shifted = pltpu.roll(xp, C_D_CONV - 1 - tap, axis=0)[CONV_PAD:, :]
        acc = acc + shifted * cw[tap:tap + 1, :]
    acc = acc + xbc * cw[C_D_CONV - 1:C_D_CONV, :]
    xpad_next_ref[0:CONV_PAD, :] = xbc[T - CONV_PAD:, :]
    cout_ref[0] = xbc[T - (C_D_CONV - 1):, :]
    act = _silu(acc)
    xs = act[:, :C_D_INNER]
    bm = act[:, C_D_INNER:C_D_INNER + C_BC_DIM]
    cm = act[:, C_D_INNER + C_BC_DIM:]
    xb = xs.astype(bf16)

    dt = _softplus(dtr + dtb_ref[...])
    a_neg = -jnp.exp(alog_ref[...])
    row = lax.broadcasted_iota(jnp.int32, (T, T), 0)
    col = lax.broadcasted_iota(jnp.int32, (T, T), 1)
    causal = row >= col
    acs = _dot_exact_lhs01(causal.astype(f32), dt * a_neg)
    a2 = acs * LOG2E
    sel3 = sel_ref[...]
    e_exp = jnp.exp2(_expand_heads(a2, sel3))
    w_exp = _expand_heads(jnp.exp(acs[T - 1:T, :] - acs) * dt, sel3)
    cdec = e_exp[T - 1:T, :]
    b2_t = a2.T - jnp.log2(dt.T)
    xw = (xs * w_exp).astype(bf16)
    hb = ht_ref[...].astype(bf16)
    dsk = dsk_ref[...]
    lo_t = lax.broadcasted_iota(jnp.int32, (T, LANES), 1) < C_HEAD_DIM

    for g in range(C_GROUPS):
        ns = slice(g * C_D_STATE, (g + 1) * C_D_STATE)
        gs = slice(g * C_GROUP_W, (g + 1) * C_GROUP_W)
        b_g = bm[:, ns]
        c_g = cm[:, ns].astype(bf16)
        cb = _dot_nt(c_g, b_g.astype(bf16))
        yi = _dot(c_g, hb[:, gs])
        ht_ref[:, gs] = ht_ref[:, gs] * cdec[:, gs] + _dot(b_g.T.astype(bf16), xw[:, gs])
        for tt in range(C_GROUP_W // LANES):
            t = g * (C_GROUP_W // LANES) + tt
            lanes = slice(t * LANES, (t + 1) * LANES)
            xt = xb[:, lanes]
            res = []
            for half in range(2):
                h = 2 * t + half
                a_col = jnp.broadcast_to(a2[:, h:h + 1], (T, T))
                b_row = jnp.broadcast_to(b2_t[h:h + 1, :], (T, T))
                w = jnp.where(causal, cb * jnp.exp2(a_col - b_row), 0.0)
                res.append(_dot(w.astype(bf16), xt))
            y_intra = jnp.where(lo_t, res[0], res[1])
            y_ref[:, lanes] = y_intra + e_exp[:, lanes] * yi[:, tt * LANES:(tt + 1) * LANES] + dsk[:, lanes] * xs[:, lanes]

    yn_ref[...] = _gated_group_norm(y_ref[...], z, nw_ref[...]).astype(bf16)


def _head_select3():
    k = np.arange(LANES)[:, None]
    ch = np.arange(C_D_INNER)[None, :] // C_HEAD_DIM
    return jnp.asarray((k % C_HEADS == ch) & (k < 3 * C_HEADS), dtype=bf16)


def ssd_prompt(x, g, w_in, layer, w_dt, conv_w, conv_b, dt_bias, a_log, d_skip, norm_w):
    step = SSD_SUB * C_CHUNK
    nc = SEQ // step
    full = lambda *shape: pl.BlockSpec(shape, lambda b, c: (0,) * len(shape))
    once = lambda *shape: pl.BlockSpec(shape, lambda b, c: (0,) * len(shape), pipeline_mode=pl.Buffered(1))
    return pl.pallas_call(
        _ssd_prompt_kernel,
        grid=(BATCH, nc),
        in_specs=[
            pl.BlockSpec((step, D_MODEL), lambda b, c: (b * nc + c, 0)),
            full(1, D_MODEL),
            pl.BlockSpec((1,) + w_in.shape[1:], lambda b, c: (layer, 0, 0), pipeline_mode=pl.Buffered(1)),
            once(D_MODEL, LANES),
            full(C_D_CONV, C_CONV_DIM), full(1, C_CONV_DIM), full(1, LANES), full(1, LANES),
            full(1, C_D_INNER), full(1, C_D_INNER), full(LANES, C_D_INNER),
        ],
        out_specs=[
            pl.BlockSpec((step, C_D_INNER), lambda b, c: (b * nc + c, 0)),
            pl.BlockSpec((1, C_D_INNER, C_D_STATE), lambda b, c: (b, 0, 0)),
            pl.BlockSpec((1, C_D_CONV - 1, C_CONV_DIM), lambda b, c: (b, 0, 0)),
        ],
        out_shape=[
            jax.ShapeDtypeStruct((PROMPT_ROWS, C_D_INNER), bf16),
            jax.ShapeDtypeStruct((BATCH, C_D_INNER, C_D_STATE), f32),
            jax.ShapeDtypeStruct((BATCH, C_D_CONV - 1, C_CONV_DIM), f32),
        ],
        scratch_shapes=[
            pltpu.VMEM((step, D_MODEL), bf16),
            pltpu.VMEM((SSD_SUB, CONV_PAD + C_CHUNK, C_CONV_DIM), f32),
            pltpu.VMEM((C_D_STATE, C_D_INNER), f32),
            pltpu.VMEM((SSD_SUB, C_CHUNK, C_D_INNER), f32),
        ],
        compiler_params=_params("parallel", "arbitrary"),
        name="ssd_prompt",
    )(x, g, w_in, w_dt, conv_w, conv_b, dt_bias, a_log, d_skip, norm_w, _head_select3())


SSD_BB = 8
SSD_TP = SUBLANES
_N_PAIRS = DEC_SEQ * (DEC_SEQ + 1) // 2
_N_COEF = _N_PAIRS + 2 * DEC_SEQ


def _ssd_sample_kernel(zx_ref, dtr_ref, cs_ref, h0_ref, cw_ref, cb_ref, dtb_ref, alog_ref, dsk_ref, nw_ref,
                       sel_ref, yn_ref, hn_ref, cout_ref, c_scr, b_scr, xw_scr, yi_scr, cd_scr):
    L = DEC_SEQ
    cw = cw_ref[...]
    xp = [cs_ref[k] for k in range(C_D_CONV - 1)] + [zx_ref[t, :, C_D_INNER:] for t in range(L)]
    for k in range(C_D_CONV - 1):
        cout_ref[k] = xp[L + k]
    act = []
    for t in range(L):
        acc = cb_ref[...]
        for tap in range(C_D_CONV):
            acc = acc + xp[t + tap] * cw[tap:tap + 1, :]
        act.append(_silu(acc))
    xs = [a[:, :C_D_INNER] for a in act]
    bm = [a[:, C_D_INNER:C_D_INNER + C_BC_DIM] for a in act]
    cm = [a[:, C_D_INNER + C_BC_DIM:] for a in act]

    a_neg = -jnp.exp(alog_ref[...])
    dt = [_softplus(dtr_ref[t] + dtb_ref[...]) for t in range(L)]
    acs = []
    for t in range(L):
        acs.append(dt[t] * a_neg if t == 0 else acs[t - 1] + dt[t] * a_neg)

    lane_group = lax.broadcasted_iota(jnp.int32, (SSD_BB, LANES), 1) // C_REP
    coefs = []
    for t in range(L):
        for t2 in range(t + 1):
            cbh = jnp.zeros((SSD_BB, LANES), f32)
            for g in range(C_GROUPS):
                ns = slice(g * C_D_STATE, (g + 1) * C_D_STATE)
                cbg = jnp.sum(cm[t][:, ns] * bm[t2][:, ns], axis=-1, keepdims=True)
                cbh = jnp.where(lane_group == g, cbg, cbh)
            coefs.append(cbh * jnp.exp(acs[t] - acs[t2]) * dt[t2])
    for t in range(L):
        coefs.append(jnp.exp(acs[t]))
    for t in range(L):
        coefs.append(jnp.exp(acs[L - 1] - acs[t]) * dt[t])
    coef = jnp.concatenate(coefs, axis=0)
    cexp = _dot_exact_rhs01(coef, sel_ref[...])
    cexp = [cexp[k * SSD_BB:(k + 1) * SSD_BB, :] for k in range(_N_COEF)]
    w_intra = cexp[:_N_PAIRS]
    w_inter = cexp[_N_PAIRS:_N_PAIRS + L]
    w_state = cexp[_N_PAIRS + L:]

    cd = jnp.concatenate([jnp.exp(acs[L - 1]), jnp.zeros((LANES - SSD_BB, LANES), f32)], axis=0)
    cd_t = cd.T
    for s in range(SSD_BB):
        cd_scr[s] = jnp.broadcast_to(cd_t[0:C_HEADS, s:s + 1], (C_HEADS, C_D_STATE))

    zeros_tail = jnp.zeros((SSD_BB, SSD_TP - L, C_D_INNER), f32)
    c_scr[:, L:, :] = zeros_tail[:, :, :C_BC_DIM]
    b_scr[:, L:, :] = zeros_tail[:, :, :C_BC_DIM]
    xw_scr[:, L:, :] = zeros_tail
    for t in range(L):
        xw_t = xs[t] * w_state[t]
        for s in range(SSD_BB):
            c_scr[s, t:t + 1, :] = cm[t][s:s + 1, :]
            b_scr[s, t:t + 1, :] = bm[t][s:s + 1, :]
            xw_scr[s, t:t + 1, :] = xw_t[s:s + 1, :]

    for s in range(SSD_BB):
        for g in range(C_GROUPS):
            ns = slice(g * C_D_STATE, (g + 1) * C_D_STATE)
            gs = slice(g * C_GROUP_W, (g + 1) * C_GROUP_W)
            h0 = h0_ref[s, gs, :]
            yi = _dot_nt(c_scr[s, :, ns].astype(bf16), h0.astype(bf16))
            for t in range(L):
                yi_scr[t, s:s + 1, gs] = yi[t:t + 1, :]
            st = _dot_tn(xw_scr[s, :, gs].astype(bf16), b_scr[s, :, ns].astype(bf16))
            for r in range(C_REP):
                h = g * C_REP + r
                rs = slice(r * C_HEAD_DIM, (r + 1) * C_HEAD_DIM)
                scale = cd_scr[s, h:h + 1, :]
                hn_ref[s, h * C_HEAD_DIM:(h + 1) * C_HEAD_DIM, :] = h0[rs, :] * scale + st[rs, :]

    dsk = dsk_ref[...]
    nw = nw_ref[...]
    pair = 0
    for t in range(L):
        y = w_inter[t] * yi_scr[t] + dsk * xs[t]
        for t2 in range(t + 1):
            y = y + w_intra[pair] * xs[t2]
            pair += 1
        yn_ref[t] = _gated_group_norm(y, zx_ref[t, :, :C_D_INNER], nw).astype(bf16)


def ssd_sample(zx_t, dtr_t, conv_state_t, h0, conv_w, conv_b, dt_bias, a_log, d_skip, norm_w, sel):
    tmaj = lambda n, w: pl.BlockSpec((n, SSD_BB, w), lambda i: (0, i, 0))
    full = lambda *shape: pl.BlockSpec(shape, lambda i: (0,) * len(shape))
    return pl.pallas_call(
        _ssd_sample_kernel,
        grid=(DEC_BATCH // SSD_BB,),
        in_specs=[
            tmaj(DEC_SEQ, C_D_INNER + C_CONV_DIM), tmaj(DEC_SEQ, LANES), tmaj(C_D_CONV - 1, C_CONV_DIM),
            pl.BlockSpec((SSD_BB, C_D_INNER, C_D_STATE), lambda i: (i, 0, 0)),
            full(C_D_CONV, C_CONV_DIM), full(1, C_CONV_DIM), full(1, LANES), full(1, LANES),
            full(1, C_D_INNER), full(1, C_D_INNER), full(LANES, C_D_INNER),
        ],
        out_specs=[
            tmaj(DEC_SEQ, C_D_INNER),
            pl.BlockSpec((SSD_BB, C_D_INNER, C_D_STATE), lambda i: (i, 0, 0)),
            tmaj(C_D_CONV - 1, C_CONV_DIM),
        ],
        out_shape=[
            jax.ShapeDtypeStruct((DEC_SEQ, DEC_BATCH, C_D_INNER), bf16),
            jax.ShapeDtypeStruct((DEC_BATCH, C_D_INNER, C_D_STATE), f32),
            jax.ShapeDtypeStruct((C_D_CONV - 1, DEC_BATCH, C_CONV_DIM), f32),
        ],
        scratch_shapes=[
            pltpu.VMEM((SSD_BB, SSD_TP, C_BC_DIM), f32),
            pltpu.VMEM((SSD_BB, SSD_TP, C_BC_DIM), f32),
            pltpu.VMEM((SSD_BB, SSD_TP, C_D_INNER), f32),
            pltpu.VMEM((DEC_SEQ, SSD_BB, C_D_INNER), f32),
            pltpu.VMEM((SSD_BB, C_HEADS, C_D_STATE), f32),
        ],
        compiler_params=_params("parallel"),
        name="ssd_sample",
    )(zx_t, dtr_t, conv_state_t, h0, conv_w, conv_b, dt_bias, a_log, d_skip, norm_w, sel)


def _pad_lanes(v):
    return jnp.pad(v.astype(f32), (0, LANES - v.shape[0])).reshape(1, LANES)


def _mixer_a(xp, xs, g, j, w_in, norm_v, w_sp, b_sp, w_out):
    nv = norm_v.reshape(1, A_HALF)
    xs, v_s, w_in_b, w_out_b = mixer_a(xs, g, w_in, nv, w_sp[:, :DEC_SEQ, :DEC_SEQ].reshape(-1),
                                       b_sp[:, :DEC_SEQ].reshape(-1), w_out, j, tm=TM_SAMPLE, sample=True)
    xp = mixer_a(xp, g, w_in_b, nv, w_sp, b_sp.T, w_out_b, 0, tm=TM_PROMPT, sample=False)
    return xp, xs, jnp.swapaxes(v_s.reshape(DEC_SEQ, DEC_BATCH, A_HALF), 0, 1)


def _mixer_b(xp, xs, g, j, k_cache, v_cache, w_qkv, q_norm, k_norm, sinks, rel_bias):
    qn = q_norm.reshape(1, B_HEAD_DIM)
    kn = k_norm.reshape(1, B_HEAD_DIM)
    table = bias_table(rel_bias)
    n_qkv = B_Q_DIM + 2 * B_KV_DIM

    o_p, k_p, v_p = attn_prompt(xp, g, w_qkv, j, table, qn, kn, sinks)

    qkv_s = norm_matmul(xs, g, w_qkv, j, n_qkv, tm=TM_SAMPLE, tn=512).reshape(DEC_SEQ, DEC_BATCH, -1)
    q_s = qkv_s[:, :, :B_Q_DIM].reshape(DEC_SEQ, DEC_BATCH, B_KV_HEADS, B_REP, B_HEAD_DIM)
    q_s = q_s.transpose(1, 3, 0, 2, 4).reshape(DEC_BATCH, ATT_QR, B_KV_DIM)
    k_new = jnp.swapaxes(qkv_s[:, :, B_Q_DIM:B_Q_DIM + B_KV_DIM], 0, 1)
    v_new = jnp.swapaxes(qkv_s[:, :, B_Q_DIM + B_KV_DIM:], 0, 1)
    table_s = table[:, :DEC_SEQ, :ATT_KEYS].reshape(B_KV_HEADS, ATT_QR, ATT_KEYS)
    o_s, k_s, v_s = attn_sample(q_s, k_new, v_new,
                                k_cache.reshape(DEC_BATCH, B_WINDOW, B_KV_DIM),
                                v_cache.reshape(DEC_BATCH, B_WINDOW, B_KV_DIM),
                                table_s, qn, kn, sinks)
    o_s = o_s.reshape(DEC_BATCH, B_REP, DEC_SEQ, B_KV_HEADS, B_HEAD_DIM).transpose(2, 0, 3, 1, 4)
    o_s = o_s.reshape(SAMPLE_ROWS, B_Q_DIM).astype(bf16)
    kv_shape_p = (BATCH, B_WINDOW, B_KV_HEADS, B_HEAD_DIM)
    kv_shape_s = (DEC_BATCH, B_WINDOW, B_KV_HEADS, B_HEAD_DIM)
    return o_p, o_s, k_p.reshape(kv_shape_p), v_p.reshape(kv_shape_p), k_s.reshape(kv_shape_s), v_s.reshape(kv_shape_s)


def _mixer_c(xp, xs, g, j, h0, conv_state, w_in, conv_w, conv_b, dt_bias, a_log, d_skip, norm_w):
    n_zx = C_D_INNER + C_CONV_DIM
    w_dt = jnp.pad(w_in[j, :, n_zx:], ((0, 0), (0, LANES - C_HEADS)))
    cb = conv_b.reshape(1, C_CONV_DIM)
    dtb = _pad_lanes(dt_bias)
    alog = _pad_lanes(a_log)
    dsk = jnp.repeat(d_skip.astype(f32), C_HEAD_DIM).reshape(1, C_D_INNER)
    nw = norm_w.reshape(1, C_D_INNER)

    yn_p, h_p, conv_p = ssd_prompt(xp, g, w_in, j, w_dt, conv_w, cb, dtb, alog, dsk, nw)
    zx_s, dtr_s = norm_matmul(xs, g, w_in, j, n_zx, tm=TM_SAMPLE, tn=1024, w_tail=w_dt)
    sel = (jnp.arange(LANES)[:, None] == jnp.arange(C_D_INNER)[None, :] // C_HEAD_DIM).astype(f32)
    yn_s, h_s, conv_s = ssd_sample(
        zx_s.reshape(DEC_SEQ, DEC_BATCH, -1), dtr_s.reshape(DEC_SEQ, DEC_BATCH, LANES),
        jnp.swapaxes(conv_state, 0, 1), h0.reshape(DEC_BATCH, C_D_INNER, C_D_STATE),
        conv_w, cb, dtb, alog, dsk, nw, sel)
    st_shape = (C_HEADS, C_HEAD_DIM, C_D_STATE)
    return (yn_p, yn_s.reshape(SAMPLE_ROWS, C_D_INNER), h_p.reshape((BATCH,) + st_shape), conv_p,
            h_s.reshape((DEC_BATCH,) + st_shape), jnp.swapaxes(conv_s, 0, 1))


def kernel(x_prompt, x_sample, cache_swa_k, cache_swa_v, state_ssm, state_conv, norm_mixer, norm_mlp, mlp_w_up, mlp_w_down, a_w_in, a_norm_v, a_w_spatial, a_b_spatial, a_w_out, b_w_qkv, b_q_norm, b_k_norm, b_sinks, rel_bias, b_w_out, c_w_in, c_conv_w, c_conv_b, c_dt_bias, c_a_log, c_d, c_norm, c_w_out):
    xp = x_prompt.reshape(PROMPT_ROWS, D_MODEL)
    xs = jnp.swapaxes(x_sample, 0, 1).reshape(SAMPLE_ROWS, D_MODEL)
    chunk_v_s = []
    swa_kp, swa_vp, swa_ks, swa_vs = [], [], [], []
    ssm_p, conv_p, ssm_s, conv_s = [], [], [], []
    b_w_qkv, b_w_out = b_w_qkv.astype(bf16), b_w_out.astype(bf16)
    c_w_in, c_w_out = c_w_in.astype(bf16), c_w_out.astype(bf16)
    for i in range(DEPTH):
        kind = i % N_MIXERS
        j = i // N_MIXERS
        g = norm_mixer[i].reshape(1, D_MODEL)
        proj_p = proj_s = None
        if kind == 0:
            xp, xs, v_new = _mixer_a(xp, xs, g, j, a_w_in, a_norm_v[j], a_w_spatial[j], a_b_spatial[j], a_w_out)
            chunk_v_s.append(v_new)
        elif kind == 1:
            o_p, o_s, kp, vp, ks_, vs_ = _mixer_b(xp, xs, g, j, cache_swa_k[j], cache_swa_v[j], b_w_qkv, b_q_norm[j],
                                                  b_k_norm[j], b_sinks[j], rel_bias)
            proj_p, proj_s = (o_p, b_w_out, j), (o_s, b_w_out, j)
            swa_kp.append(kp); swa_vp.append(vp); swa_ks.append(ks_); swa_vs.append(vs_)
        else:
            y_p, y_s, hp, bp, hs, bs = _mixer_c(xp, xs, g, j, state_ssm[j], state_conv[j], c_w_in, c_conv_w[j],
                                                c_conv_b[j], c_dt_bias[j], c_a_log[j], c_d[j], c_norm[j])
            proj_p, proj_s = (y_p, c_w_out, j), (y_s, c_w_out, j)
            ssm_p.append(hp); conv_p.append(bp); ssm_s.append(hs); conv_s.append(bs)
        gm = norm_mlp[i].reshape(1, D_MODEL)
        xs, w_up_b, w_down_b = mlp(xs, gm, mlp_w_up, mlp_w_down, i, tm=TM_SAMPLE, tf=MLP_TF, proj=proj_s, emit=True)
        xp = mlp(xp, gm, w_up_b, w_down_b, 0, tm=TM_PROMPT, tf=MLP_TF if proj_p else 2 * MLP_TF, proj=proj_p)
    y_prompt = xp.reshape(BATCH, SEQ, D_MODEL)
    y_sample = jnp.swapaxes(xs.reshape(DEC_SEQ, DEC_BATCH, D_MODEL), 0, 1)
    return (y_prompt, y_sample, jnp.stack(chunk_v_s),
            jnp.stack(swa_kp), jnp.stack(swa_vp), jnp.stack(swa_ks), jnp.stack(swa_vs),
            jnp.stack(ssm_p), jnp.stack(conv_p), jnp.stack(ssm_s), jnp.stack(conv_s))
```

```python
import functools
import math

import jax
import jax.numpy as jnp
import numpy as np
from jax import lax
from jax.experimental import pallas as pl
from jax.experimental.pallas import tpu as pltpu

f32 = jnp.float32
bf16 = jnp.bfloat16

D_MODEL = 1024
BATCH = 4
SEQ = 4096
DEPTH = 4
DEC_BATCH = 128
DEC_SEQ = 4
PAST_LEN = 8192
N_MIXERS = 3
D_FF = 4 * D_MODEL
EPS = 1e-6
NEG_INF = -1e30

A_CHUNK = 128
A_D_FFN = 6 * D_MODEL
A_HALF = A_D_FFN // 2
A_GROUPS = 8
A_GROUP_W = A_HALF // A_GROUPS

B_HEADS = 16
B_KV_HEADS = 4
B_HEAD_DIM = 64
B_REP = B_HEADS // B_KV_HEADS
B_WINDOW = 128
B_BLOCK = 128
B_Q_DIM = B_HEADS * B_HEAD_DIM
B_KV_DIM = B_KV_HEADS * B_HEAD_DIM
N_BUCKETS = 32
MAX_DISTANCE = 128

C_D_INNER = 2 * D_MODEL
C_HEAD_DIM = 64
C_HEADS = C_D_INNER // C_HEAD_DIM
C_GROUPS = 4
C_REP = C_HEADS // C_GROUPS
C_D_STATE = 128
C_D_CONV = 4
C_BC_DIM = C_GROUPS * C_D_STATE
C_CONV_DIM = C_D_INNER + 2 * C_BC_DIM
C_GROUP_W = C_D_INNER // C_GROUPS
C_CHUNK = 128

LANES = 128
SUBLANES = 8
VMEM_LIMIT_BYTES = 56 * 1024 * 1024

PROMPT_ROWS = BATCH * SEQ
SAMPLE_ROWS = DEC_BATCH * DEC_SEQ
TM_PROMPT = 1024
TM_SAMPLE = SAMPLE_ROWS
MLP_TF = 1024


def _params(*sem):
    return pltpu.CompilerParams(dimension_semantics=sem, vmem_limit_bytes=VMEM_LIMIT_BYTES)


def _rms(x, g):
    ms = jnp.mean(x * x, axis=-1, keepdims=True)
    return x * lax.rsqrt(ms + EPS) * g


def _gelu(x):
    return 0.5 * x * (1.0 + lax.erf(x * math.sqrt(0.5)))


def _silu(x):
    return x * jax.nn.sigmoid(x)


def _softplus(x):
    return jnp.maximum(x, 0.0) + jnp.log1p(jnp.exp(-jnp.abs(x)))


def _dot(a, b):
    return jnp.dot(a, b, preferred_element_type=f32)


def _dot_nt(a, b):
    return lax.dot_general(a, b, (((1,), (1,)), ((), ())), preferred_element_type=f32)


def _dot_tn(a, b):
    return lax.dot_general(a, b, (((0,), (0,)), ((), ())), preferred_element_type=f32)


def _dot_exact_lhs01(a01, x):
    a = a01.astype(bf16)
    hi = x.astype(bf16)
    r1 = x - hi.astype(f32)
    mid = r1.astype(bf16)
    lo = (r1 - mid.astype(f32)).astype(bf16)
    return _dot(a, hi) + _dot(a, mid) + _dot(a, lo)


def _dot_exact_rhs01(x, b01):
    b = b01.astype(bf16)
    hi = x.astype(bf16)
    r1 = x - hi.astype(f32)
    mid = r1.astype(bf16)
    lo = (r1 - mid.astype(f32)).astype(bf16)
    return _dot(hi, b) + _dot(mid, b) + _dot(lo, b)


def _norm_matmul_kernel(*refs, nj, tail):
    if tail:
        x_ref, g_ref, w_ref, wt_ref, o_ref, ot_ref, xn_ref = refs
    else:
        x_ref, g_ref, w_ref, o_ref, xn_ref = refs
    j = pl.program_id(1)

    @pl.when(j == 0)
    def _():
        xn_ref[...] = _rms(x_ref[...], g_ref[...]).astype(bf16)

    @pl.when(j < nj)
    def _():
        o_ref[...] = _dot(xn_ref[...], w_ref[0].astype(bf16))

    if tail:
        @pl.when(j == nj)
        def _():
            ot_ref[...] = _dot(xn_ref[...], wt_ref[...].astype(bf16))


def norm_matmul(x, g, w, layer, n, *, tm, tn, w_tail=None):
    m, k = x.shape
    nj = n // tn
    tail = w_tail is not None
    last = nj - 1
    in_specs = [
        pl.BlockSpec((tm, k), lambda i, j: (i, 0)),
        pl.BlockSpec((1, k), lambda i, j: (0, 0)),
        pl.BlockSpec((1, k, tn), lambda i, j: (layer, 0, jnp.minimum(j, last))),
    ]
    out_specs = [pl.BlockSpec((tm, tn), lambda i, j: (i, jnp.minimum(j, last)))]
    out_shape = [jax.ShapeDtypeStruct((m, n), f32)]
    args = [x, g, w]
    if tail:
        in_specs.append(pl.BlockSpec((k, LANES), lambda i, j: (0, 0)))
        out_specs.append(pl.BlockSpec((tm, LANES), lambda i, j: (i, 0)))
        out_shape.append(jax.ShapeDtypeStruct((m, LANES), f32))
        args.append(w_tail)
    out = pl.pallas_call(
        functools.partial(_norm_matmul_kernel, nj=nj, tail=tail),
        grid=(m // tm, nj + (1 if tail else 0)),
        in_specs=in_specs,
        out_specs=out_specs,
        out_shape=out_shape,
        scratch_shapes=[pltpu.VMEM((tm, k), bf16)],
        compiler_params=_params("parallel", "arbitrary"),
        name="norm_matmul",
    )(*args)
    return out if tail else out[0]


def _mlp_kernel(*refs, proj, emit):
    refs = list(refs)
    a_ref, wo_ref = (refs.pop(0), refs.pop(0)) if proj else (None, None)
    x_ref, g_ref, wu_ref, wd_ref, o_ref = refs[:5]
    wub_ref, wdb_ref = (refs[5], refs[6]) if emit else (None, None)
    xn_ref = refs[-1]

    @pl.when(pl.program_id(1) == 0)
    def _():
        x = x_ref[...]
        if proj:
            x = x + _dot(a_ref[...], wo_ref[0])
        xn_ref[...] = _rms(x, g_ref[...]).astype(bf16)
        o_ref[...] = x

    wu = wu_ref[0].astype(bf16)
    wd = wd_ref[0].astype(bf16)
    if emit:
        wub_ref[0] = wu
        wdb_ref[0] = wd
    h = jnp.maximum(_dot(xn_ref[...], wu), 0.0)
    o_ref[...] += _dot((h * h).astype(bf16), wd)


def mlp(x, g, w_up, w_down, layer, *, tm, tf, proj=None, emit=False):
    m, d = x.shape
    ff = w_up.shape[2]
    assert not emit or m == tm
    in_specs = [
        pl.BlockSpec((tm, d), lambda i, j: (i, 0)),
        pl.BlockSpec((1, d), lambda i, j: (0, 0)),
        pl.BlockSpec((1, d, tf), lambda i, j: (layer, 0, j)),
        pl.BlockSpec((1, tf, d), lambda i, j: (layer, j, 0)),
    ]
    args = [x, g, w_up, w_down]
    if proj is not None:
        a, w_o, lo = proj
        k = a.shape[1]
        in_specs = [pl.BlockSpec((tm, k), lambda i, j: (i, 0)),
                    pl.BlockSpec((1, k, d), lambda i, j: (lo, 0, 0), pipeline_mode=pl.Buffered(1))] + in_specs
        args = [a, w_o] + args
    out_specs = [pl.BlockSpec((tm, d), lambda i, j: (i, 0))]
    out_shape = [jax.ShapeDtypeStruct((m, d), f32)]
    if emit:
        out_specs += [pl.BlockSpec((1, d, tf), lambda i, j: (0, 0, j)), pl.BlockSpec((1, tf, d), lambda i, j: (0, j, 0))]
        out_shape += [jax.ShapeDtypeStruct((1, d, ff), bf16), jax.ShapeDtypeStruct((1, ff, d), bf16)]
    out = pl.pallas_call(
        functools.partial(_mlp_kernel, proj=proj is not None, emit=emit),
        grid=(m // tm, ff // tf),
        in_specs=in_specs,
        out_specs=out_specs,
        out_shape=out_shape,
        scratch_shapes=[pltpu.VMEM((tm, d), bf16)],
        compiler_params=_params("parallel", "arbitrary"),
        name="mlp",
    )(*args)
    return out if emit else out[0]


A_BLK_GROUPS = 2
A_BLK = A_BLK_GROUPS * A_GROUP_W
A_NBLK = A_HALF // A_BLK


def _mixer_a_kernel(*refs, sample, v_steps):
    if sample:
        (ws_ref, bs_ref, x_ref, g_ref, win_ref, nv_ref, wout_ref, o_ref, vo_ref, winb_ref, woutb_ref,
         xn_ref, v_ref, ssq_ref, us_ref) = refs
        winv_ref = win_ref
    else:
        (x_ref, g_ref, winv_ref, win_ref, nv_ref, ws_ref, bs_ref, wout_ref, o_ref,
         xn_ref, v_ref, ssq_ref, us_ref) = refs
        winb_ref = woutb_ref = None
    j = pl.program_id(1)
    tm = x_ref.shape[0]
    v_blk = A_HALF // v_steps

    def w_in_block(ref=None):
        w = (win_ref if ref is None else ref)[0].astype(bf16)
        if winb_ref is not None:
            winb_ref[0] = w
        return w

    def w_out_block():
        w = wout_ref[0].astype(bf16)
        if woutb_ref is not None:
            woutb_ref[0] = w
        return w

    @pl.when(j == 0)
    def _():
        xn_ref[...] = _rms(x_ref[...], g_ref[...]).astype(bf16)
        ssq_ref[...] = jnp.zeros(ssq_ref.shape, f32)

    for k in range(v_steps):
        @pl.when(j == k)
        def _(k=k):
            v = _gelu(_dot(xn_ref[...], w_in_block(winv_ref)))
            v_ref[:, k * v_blk:(k + 1) * v_blk] = v
            ssq_ref[...] += jnp.sum(v * v, axis=-1, keepdims=True)

    if not sample:
        row = lax.broadcasted_iota(jnp.int32, (A_CHUNK, A_CHUNK), 0)
        col = lax.broadcasted_iota(jnp.int32, (A_CHUNK, A_CHUNK), 1)
        causal = row >= col

    for k in range(A_NBLK):
        @pl.when(j == v_steps + k)
        def _(k=k):
            u = _gelu(_dot(xn_ref[...], w_in_block()))
            rinv = lax.rsqrt(ssq_ref[...] * (1.0 / A_HALF) + EPS)
            for gg in range(A_BLK_GROUPS):
                g = k * A_BLK_GROUPS + gg
                cols = slice(g * A_GROUP_W, (g + 1) * A_GROUP_W)
                ucols = slice(gg * A_GROUP_W, (gg + 1) * A_GROUP_W)
                vn = v_ref[:, cols] * rinv * nv_ref[:, cols]
                if sample:
                    vo_ref[:, cols] = vn
                    vt = [vn[t * DEC_BATCH:(t + 1) * DEC_BATCH] for t in range(DEC_SEQ)]
                    s_rows = []
                    for t in range(DEC_SEQ):
                        s = ws_ref[(g * DEC_SEQ + t) * DEC_SEQ] * vt[0]
                        for t2 in range(1, t + 1):
                            s = s + ws_ref[(g * DEC_SEQ + t) * DEC_SEQ + t2] * vt[t2]
                        s_rows.append(s + bs_ref[g * DEC_SEQ + t])
                    s = jnp.concatenate(s_rows, axis=0)
                else:
                    w = jnp.where(causal, ws_ref[g], 0.0).astype(bf16)
                    bias = bs_ref[:, g:g + 1]
                    vb = vn.astype(bf16)
                    s = jnp.concatenate(
                        [_dot(w, vb[c * A_CHUNK:(c + 1) * A_CHUNK]) + bias for c in range(tm // A_CHUNK)], axis=0)
                us_ref[:, ucols] = (u[:, ucols] * s).astype(bf16)
            y = _dot(us_ref[...], w_out_block())
            if k == 0:
                o_ref[...] = x_ref[...] + y
            else:
                o_ref[...] += y


def mixer_a(x, g, w_in, norm_v, w_sp, b_sp, w_out, layer, *, tm, sample):
    m, d = x.shape
    v_steps = A_NBLK if sample else A_NBLK // 2
    v_blk = A_HALF // v_steps
    nj = v_steps + A_NBLK
    row = lambda w: pl.BlockSpec((tm, w), lambda i, j: (i, 0))
    full = lambda *shape: pl.BlockSpec(shape, lambda i, j: (0,) * len(shape))
    smem = pl.BlockSpec(memory_space=pltpu.SMEM)
    win_spec = pl.BlockSpec((1, d, A_BLK), lambda i, j: (layer, 0, (j + A_NBLK) % nj))
    winv_spec = pl.BlockSpec((1, d, v_blk), lambda i, j: (layer, 0, v_steps + jnp.minimum(j, v_steps - 1)))
    winu_spec = pl.BlockSpec((1, d, A_BLK), lambda i, j: (layer, 0, jnp.maximum(j - v_steps, 0)))
    wout_spec = pl.BlockSpec((1, A_BLK, d), lambda i, j: (layer, jnp.maximum(j - v_steps, 0), 0))
    if sample:
        in_specs = [smem, smem, row(d), full(1, d), win_spec, full(1, A_HALF), wout_spec]
        args = (w_sp, b_sp, x, g, w_in, norm_v, w_out)
        out_specs = [row(d), row(A_HALF),
                     pl.BlockSpec((1, d, A_BLK), lambda i, j: (0, 0, (j + A_NBLK) % nj)),
                     pl.BlockSpec((1, A_BLK, d), lambda i, j: (0, jnp.maximum(j - A_NBLK, 0), 0))]
        out_shape = [jax.ShapeDtypeStruct((m, d), f32), jax.ShapeDtypeStruct((m, A_HALF), f32),
                     jax.ShapeDtypeStruct((1, d, 2 * A_HALF), bf16), jax.ShapeDtypeStruct((1, A_HALF, d), bf16)]
    else:
        in_specs = [row(d), full(1, d), winv_spec, winu_spec, full(1, A_HALF), full(A_GROUPS, A_CHUNK, A_CHUNK),
                    full(A_CHUNK, A_GROUPS), wout_spec]
        args = (x, g, w_in, w_in, norm_v, w_sp, b_sp, w_out)
        out_specs = row(d)
        out_shape = jax.ShapeDtypeStruct((m, d), f32)
    return pl.pallas_call(
        functools.partial(_mixer_a_kernel, sample=sample, v_steps=v_steps),
        grid=(m // tm, nj),
        in_specs=in_specs,
        out_specs=out_specs,
        out_shape=out_shape,
        scratch_shapes=[pltpu.VMEM((tm, d), bf16), pltpu.VMEM((tm, A_HALF), f32), pltpu.VMEM((tm, 1), f32),
                        pltpu.VMEM((tm, A_BLK), bf16)],
        compiler_params=_params("parallel", "arbitrary"),
        name="mixer_a_sample" if sample else "mixer_a_prompt",
    )(*args)


def _bucket_table():
    i = np.arange(B_BLOCK)[:, None]
    j = np.arange(2 * B_BLOCK)[None, :]
    n = np.maximum(B_BLOCK + i - j, 0)
    max_exact = N_BUCKETS // 2
    nf = np.maximum(n, 1).astype(np.float64)
    val = np.log(nf / max_exact) / math.log(MAX_DISTANCE / max_exact) * (N_BUCKETS - max_exact)
    in_window = (n >= max_exact) & (n < B_WINDOW)
    assert np.all(np.abs(val - np.round(val))[in_window & (n != max_exact)] > 1e-3)
    large = np.minimum(max_exact + np.floor(val + 1e-9).astype(np.int64), N_BUCKETS - 1)
    return np.where(n < max_exact, n, large).astype(np.int32)


def _bias_table_kernel(rb_ref, bk_ref, o_ref):
    bk = bk_ref[...]
    for h in range(B_HEADS):
        acc = jnp.zeros(bk.shape, f32)
        for b in range(N_BUCKETS):
            acc = jnp.where(bk == b, rb_ref[b * B_HEADS + h], acc)
        o_ref[h] = acc


def bias_table(rel_bias):
    return pl.pallas_call(
        _bias_table_kernel,
        in_specs=[pl.BlockSpec(memory_space=pltpu.SMEM), pl.BlockSpec(memory_space=pltpu.VMEM)],
        out_specs=pl.BlockSpec(memory_space=pltpu.VMEM),
        out_shape=jax.ShapeDtypeStruct((B_HEADS, B_BLOCK, 2 * B_BLOCK), f32),
        name="bias_table",
    )(rel_bias.reshape(-1), jnp.asarray(_bucket_table()))


def _softmax_with_sink(logits, sink):
    m = jnp.maximum(jnp.max(logits, axis=-1, keepdims=True), sink)
    p = jnp.exp(logits - m)
    return p, jnp.sum(p, axis=-1, keepdims=True) + jnp.exp(sink - m)


def _rms_head_pairs(x, g2, lo):
    sq = x * x
    s_lo = jnp.sum(jnp.where(lo, sq, 0.0), axis=-1, keepdims=True)
    s_hi = jnp.sum(jnp.where(lo, 0.0, sq), axis=-1, keepdims=True)
    r = lax.rsqrt(jnp.where(lo, s_lo, s_hi) * (1.0 / B_HEAD_DIM) + EPS)
    return x * r * g2


ATT_SUB = 8


def _attn_prompt_kernel(sink_ref, x_ref, g_ref, w_ref, tab_ref, qn_ref, kn_ref, o_ref, ko_ref, vo_ref,
                        xn_ref, kband_ref, vband_ref, q_ref, p_ref, rhs_ref):
    n = pl.program_id(1)
    T = B_BLOCK

    @pl.when(n == 0)
    def _():
        kband_ref[0:T, :] = jnp.zeros((T, B_KV_DIM), f32)
        vband_ref[0:T, :] = jnp.zeros((T, B_KV_DIM), f32)
        rhs_ref[...] = jnp.ones(rhs_ref.shape, bf16)

    xn_ref[...] = _rms(x_ref[...], g_ref[...]).astype(bf16)
    for sub in range(ATT_SUB):
        rows = pl.ds(sub * T, T)
        qkv = _dot(xn_ref[rows, :], w_ref[0])
        first_key = jnp.where(n == 0, T, 0) if sub == 0 else 0
        _attn_block(first_key, sink_ref, qkv, tab_ref, qn_ref, kn_ref, o_ref.at[rows], ko_ref, vo_ref,
                    kband_ref, vband_ref, q_ref.at[sub], p_ref.at[sub], rhs_ref.at[sub])


def _attn_block(first_key, sink_ref, qkv_ref, tab_ref, qn_ref, kn_ref, o_ref, ko_ref, vo_ref,
                kband_ref, vband_ref, q_ref, p_ref, rhs_ref):
    T = B_BLOCK
    lo = lax.broadcasted_iota(jnp.int32, (1, LANES), 1) < B_HEAD_DIM
    qn2 = qn_ref[...]
    kn2 = kn_ref[...]
    for t in range(B_KV_DIM // LANES):
        lanes = slice(t * LANES, (t + 1) * LANES)
        k2 = _rms_head_pairs(qkv_ref[:, B_Q_DIM + t * LANES:B_Q_DIM + (t + 1) * LANES].astype(f32), kn2, lo)
        ko_ref[0, :, lanes] = k2
        kband_ref[T:, lanes] = k2
    v = qkv_ref[:, B_Q_DIM + B_KV_DIM:].astype(f32)
    vo_ref[0] = v
    vband_ref[T:, :] = v
    for t in range(B_Q_DIM // LANES):
        q2 = qkv_ref[:, t * LANES:(t + 1) * LANES].astype(f32)
        q_ref[t * T:(t + 1) * T, :] = _rms_head_pairs(q2, qn2, lo).astype(bf16)

    key_ops = {}
    band_row = lax.broadcasted_iota(jnp.int32, (2 * T, LANES), 0)
    for t in range(B_KV_DIM // LANES):
        lanes = slice(t * LANES, (t + 1) * LANES)
        kt = kband_ref[:, lanes]
        kr = pltpu.roll(kt, B_HEAD_DIM, axis=1)
        vt = jnp.where(band_row == 0, 0.0, vband_ref[:, lanes])
        vr = pltpu.roll(vt, B_HEAD_DIM, axis=1)
        hi = jnp.logical_not(lo)
        for half, (ksrc, vsrc) in enumerate(((kt, vt), (kr, vr))):
            g_lo, g_hi = (2 * t, 2 * t + 1) if half == 0 else (2 * t + 1, 2 * t)
            key_ops[(g_lo, 0)] = jnp.where(lo, ksrc, 0.0).astype(bf16)
            key_ops[(g_hi, 1)] = jnp.where(hi, ksrc, 0.0).astype(bf16)
            rhs_ref[g_lo * 2 + 0, :, 0:LANES] = jnp.where(lo, vsrc, 1.0).astype(bf16)
            rhs_ref[g_hi * 2 + 1, :, 0:LANES] = jnp.where(hi, vsrc, 1.0).astype(bf16)

    i = lax.broadcasted_iota(jnp.int32, (T, 2 * T), 0)
    j = lax.broadcasted_iota(jnp.int32, (T, 2 * T), 1)
    valid = (j > i) & (j <= i + B_WINDOW) & (j >= first_key)
    sink_col = j == 0
    lo_t = lax.broadcasted_iota(jnp.int32, (T, LANES), 1) < B_HEAD_DIM
    for g in range(B_KV_HEADS):
        q2 = q_ref[2 * g * T:(2 * g + 2) * T, :]
        res = []
        for half in range(2):
            logits = _dot_nt(q2, key_ops[(g, half)]) * (B_HEAD_DIM ** -0.5)
            for pair in range(2):
                h = g * B_REP + 2 * pair + half
                l = jnp.where(valid, logits[pair * T:(pair + 1) * T] + tab_ref[h], NEG_INF)
                l = jnp.where(sink_col, sink_ref[h], l)
                p = jnp.exp(l - jnp.max(l, axis=-1, keepdims=True))
                p_ref[g * 2 + half, pair * T:(pair + 1) * T, :] = p.astype(bf16)
            res.append(_dot(p_ref[g * 2 + half], rhs_ref[g * 2 + half]))
        for pair in range(2):
            rows = slice(pair * T, (pair + 1) * T)
            even = res[0][rows, 0:LANES] / res[0][rows, LANES:]
            odd = res[1][rows, 0:LANES] / res[1][rows, LANES:]
            t = 2 * g + pair
            o_ref[:, t * LANES:(t + 1) * LANES] = jnp.where(lo_t, even, odd).astype(bf16)
    kband_ref[0:T, :] = kband_ref[T:, :]
    vband_ref[0:T, :] = vband_ref[T:, :]


def attn_prompt(x, g, w_qkv, layer, table, q_norm, k_norm, sinks):
    step = ATT_SUB * B_BLOCK
    nb = SEQ // step
    return pl.pallas_call(
        _attn_prompt_kernel,
        grid=(BATCH, nb),
        in_specs=[
            pl.BlockSpec(memory_space=pltpu.SMEM),
            pl.BlockSpec((step, D_MODEL), lambda b, n: (b * nb + n, 0)),
            pl.BlockSpec((1, D_MODEL), lambda b, n: (0, 0)),
            pl.BlockSpec((1,) + w_qkv.shape[1:], lambda b, n: (layer, 0, 0), pipeline_mode=pl.Buffered(1)),
            pl.BlockSpec((B_HEADS, B_BLOCK, 2 * B_BLOCK), lambda b, n: (0, 0, 0)),
            pl.BlockSpec((1, LANES), lambda b, n: (0, 0)),
            pl.BlockSpec((1, LANES), lambda b, n: (0, 0)),
        ],
        out_specs=[
            pl.BlockSpec((step, B_Q_DIM), lambda b, n: (b * nb + n, 0)),
            pl.BlockSpec((1, B_BLOCK, B_KV_DIM), lambda b, n: (b, 0, 0)),
            pl.BlockSpec((1, B_BLOCK, B_KV_DIM), lambda b, n: (b, 0, 0)),
        ],
        out_shape=[
            jax.ShapeDtypeStruct((PROMPT_ROWS, B_Q_DIM), bf16),
            jax.ShapeDtypeStruct((BATCH, B_BLOCK, B_KV_DIM), f32),
            jax.ShapeDtypeStruct((BATCH, B_BLOCK, B_KV_DIM), f32),
        ],
        scratch_shapes=[
            pltpu.VMEM((step, D_MODEL), bf16),
            pltpu.VMEM((2 * B_BLOCK, B_KV_DIM), f32),
            pltpu.VMEM((2 * B_BLOCK, B_KV_DIM), f32),
            pltpu.VMEM((ATT_SUB, B_Q_DIM // LANES * B_BLOCK, LANES), bf16),
            pltpu.VMEM((ATT_SUB, 2 * B_KV_HEADS, 2 * B_BLOCK, 2 * B_BLOCK), bf16),
            pltpu.VMEM((ATT_SUB, 2 * B_KV_HEADS, 2 * B_BLOCK, 2 * LANES), bf16),
        ],
        compiler_params=_params("parallel", "arbitrary"),
        name="attn_prompt",
    )(sinks, x, g, w_qkv, table, jnp.tile(q_norm, (1, LANES // B_HEAD_DIM)),
      jnp.tile(k_norm, (1, LANES // B_HEAD_DIM)))


ATT_BB = 8
ATT_QR = B_REP * DEC_SEQ
ATT_KEYS = B_WINDOW + 2 * DEC_SEQ


def _attn_sample_kernel(sink_ref, q_ref, kn_ref, vn_ref, kc_ref, vc_ref, tab_ref, qn_ref, knm_ref,
                        o_ref, ko_ref, vo_ref, kall_ref, vall_ref):
    qn2 = qn_ref[...]
    kn2 = knm_ref[...]
    n_rows = B_KV_HEADS * ATT_QR
    lo = lax.broadcasted_iota(jnp.int32, (1, LANES), 1) < B_HEAD_DIM
    row = lax.broadcasted_iota(jnp.int32, (n_rows, ATT_KEYS), 0)
    j = lax.broadcasted_iota(jnp.int32, (n_rows, ATT_KEYS), 1)
    t = row % DEC_SEQ
    valid = (j > t) & (j <= t + B_WINDOW)
    q_row_group = lax.broadcasted_iota(jnp.int32, (n_rows, B_KV_DIM), 0) // ATT_QR
    q_lane_group = lax.broadcasted_iota(jnp.int32, (n_rows, B_KV_DIM), 1) // B_HEAD_DIM
    own_group = q_row_group == q_lane_group
    o_lane_group = lax.broadcasted_iota(jnp.int32, (ATT_QR, B_KV_DIM), 1) // B_HEAD_DIM
    pad = jnp.zeros((ATT_KEYS - B_WINDOW - DEC_SEQ, B_KV_DIM), f32)
    bias = tab_ref[...]
    sink = sink_ref[...]
    for s in range(ATT_BB):
        kc = kc_ref[s]
        vc = vc_ref[s]
        k_new = jnp.concatenate(
            [_rms_head_pairs(kn_ref[s, :, tt * LANES:(tt + 1) * LANES], kn2, lo) for tt in range(B_KV_DIM // LANES)],
            axis=1)
        v_new = vn_ref[s]
        ko_ref[s, 0:B_WINDOW - DEC_SEQ, :] = kc[DEC_SEQ:, :]
        ko_ref[s, B_WINDOW - DEC_SEQ:, :] = k_new
        vo_ref[s, 0:B_WINDOW - DEC_SEQ, :] = vc[DEC_SEQ:, :]
        vo_ref[s, B_WINDOW - DEC_SEQ:, :] = v_new
        kall_ref[s, 0:B_WINDOW, :] = kc
        kall_ref[s, B_WINDOW:B_WINDOW + DEC_SEQ, :] = k_new
        kall_ref[s, B_WINDOW + DEC_SEQ:, :] = pad
        vall_ref[s, 0:B_WINDOW, :] = vc
        vall_ref[s, B_WINDOW:B_WINDOW + DEC_SEQ, :] = v_new
        vall_ref[s, B_WINDOW + DEC_SEQ:, :] = pad
        q = q_ref[s]
        qn = jnp.concatenate(
            [_rms_head_pairs(q[:, tt * LANES:(tt + 1) * LANES], qn2, lo) for tt in range(B_KV_DIM // LANES)], axis=1)
        q_all = jnp.where(own_group, jnp.concatenate([qn] * B_KV_HEADS, axis=0), 0.0).astype(bf16)
        logits = _dot_nt(q_all, kall_ref[s].astype(bf16)) * (B_HEAD_DIM ** -0.5)
        logits = jnp.where(valid, logits + bias, NEG_INF)
        p, denom = _softmax_with_sink(logits, sink)
        res = _dot(p.astype(bf16), vall_ref[s].astype(bf16)) / denom
        out = jnp.zeros((ATT_QR, B_KV_DIM), f32)
        for g in range(B_KV_HEADS):
            out = jnp.where(o_lane_group == g, res[g * ATT_QR:(g + 1) * ATT_QR, :], out)
        o_ref[s] = out


def attn_sample(q_s, k_new, v_new, k_cache, v_cache, table_s, q_norm, k_norm, sinks):
    blk = lambda *shape: pl.BlockSpec((ATT_BB,) + shape, lambda i: (i,) + (0,) * len(shape))
    full = lambda *shape: pl.BlockSpec(shape, lambda i: (0,) * len(shape))
    return pl.pallas_call(
        _attn_sample_kernel,
        grid=(DEC_BATCH // ATT_BB,),
        in_specs=[
            full(B_KV_HEADS * ATT_QR, 1),
            blk(ATT_QR, B_KV_DIM), blk(DEC_SEQ, B_KV_DIM), blk(DEC_SEQ, B_KV_DIM),
            blk(B_WINDOW, B_KV_DIM), blk(B_WINDOW, B_KV_DIM),
            full(B_KV_HEADS * ATT_QR, ATT_KEYS), full(1, LANES), full(1, LANES),
        ],
        out_specs=[blk(ATT_QR, B_KV_DIM), blk(B_WINDOW, B_KV_DIM), blk(B_WINDOW, B_KV_DIM)],
        out_shape=[
            jax.ShapeDtypeStruct((DEC_BATCH, ATT_QR, B_KV_DIM), f32),
            jax.ShapeDtypeStruct((DEC_BATCH, B_WINDOW, B_KV_DIM), f32),
            jax.ShapeDtypeStruct((DEC_BATCH, B_WINDOW, B_KV_DIM), f32),
        ],
        scratch_shapes=[pltpu.VMEM((ATT_BB, ATT_KEYS, B_KV_DIM), f32), pltpu.VMEM((ATT_BB, ATT_KEYS, B_KV_DIM), f32)],
        compiler_params=_params("parallel"),
        name="attn_sample",
    )(jnp.repeat(sinks, DEC_SEQ).reshape(B_KV_HEADS * ATT_QR, 1), q_s, k_new, v_new, k_cache, v_cache,
      table_s.reshape(B_KV_HEADS * ATT_QR, ATT_KEYS),
      jnp.tile(q_norm, (1, LANES // B_HEAD_DIM)), jnp.tile(k_norm, (1, LANES // B_HEAD_DIM)))


CONV_PAD = SUBLANES

def _gated_group_norm(y, z, norm_w):
    gt = y * _silu(z)
    parts = []
    for g in range(C_GROUPS):
        gg = gt[:, g * C_GROUP_W:(g + 1) * C_GROUP_W]
        parts.append(gg * lax.rsqrt(jnp.mean(gg * gg, axis=-1, keepdims=True) + EPS))
    return jnp.concatenate(parts, axis=1) * norm_w


LOG2E = math.log2(math.e)


def _expand_heads(v, sel3):
    lane = lax.broadcasted_iota(jnp.int32, (1, LANES), 1)
    v = jnp.where(lane < C_HEADS, v, 0.0)
    hi = v.astype(bf16).astype(f32)
    r1 = v - hi
    mid = r1.astype(bf16).astype(f32)
    lo = r1 - mid
    packed = hi + pltpu.roll(mid, C_HEADS, axis=1) + pltpu.roll(lo, 2 * C_HEADS, axis=1)
    return _dot(packed.astype(bf16), sel3)


SSD_SUB = 4


def _ssd_prompt_kernel(x_ref, g_ref, w_ref, wdt_ref, cw_ref, cb_ref, dtb_ref, alog_ref, dsk_ref, nw_ref, sel_ref,
                       yn_ref, hfin_ref, cout_ref, xn_ref, xpad_ref, ht_ref, y_ref):
    c = pl.program_id(1)
    n_zx = C_D_INNER + C_CONV_DIM

    @pl.when(c == 0)
    def _():
        xpad_ref[0, 0:CONV_PAD, :] = jnp.zeros((CONV_PAD, C_CONV_DIM), f32)
        ht_ref[...] = jnp.zeros(ht_ref.shape, f32)

    xn_ref[...] = _rms(x_ref[...], g_ref[...]).astype(bf16)
    for sub in range(SSD_SUB):
        rows = pl.ds(sub * C_CHUNK, C_CHUNK)
        xn = xn_ref[rows, :]
        zx = _dot(xn, w_ref[0, :, 0:n_zx])
        dtr = _dot(xn, wdt_ref[...])
        _ssd_chunk(zx[:, :C_D_INNER], zx[:, C_D_INNER:], dtr, cw_ref, cb_ref, dtb_ref, alog_ref, dsk_ref, nw_ref,
                   sel_ref, yn_ref.at[rows], cout_ref, xpad_ref.at[sub], xpad_ref.at[(sub + 1) % SSD_SUB], ht_ref,
                   y_ref.at[sub])

    @pl.when(c == pl.num_programs(1) - 1)
    def _():
        for t in range(C_D_INNER // LANES):
            hfin_ref[0, t * LANES:(t + 1) * LANES, :] = ht_ref[:, t * LANES:(t + 1) * LANES].T


def _ssd_chunk(z, xbc, dtr, cw_ref, cb_ref, dtb_ref, alog_ref, dsk_ref, nw_ref, sel_ref,
               yn_ref, cout_ref, xpad_ref, xpad_next_ref, ht_ref, y_ref):
    T = C_CHUNK
    xpad_ref[CONV_PAD:, :] = xbc
    xp = xpad_ref[...]
    cw = cw_ref[...]
    acc = cb_ref[...]
    for tap in range(C_D_CONV - 1):
        shifted = pltpu.roll(xp, C_D_CONV - 1 - tap, axis=0)[CONV_PAD:, :]
        acc = acc + shifted * cw[tap:tap + 1, :]
    acc = acc + xbc * cw[C_D_CONV - 1:C_D_CONV, :]
    xpad_next_ref[0:CONV_PAD, :] = xbc[T - CONV_PAD:, :]
    cout_ref[0] = xbc[T - (C_D_CONV - 1):, :]
    act = _silu(acc)
    xs = act[:, :C_D_INNER]
    bm = act[:, C_D_INNER:C_D_INNER + C_BC_DIM]
    cm = act[:, C_D_INNER + C_BC_DIM:]
    xb = xs.astype(bf16)

    dt = _softplus(dtr + dtb_ref[...])
    a_neg = -jnp.exp(alog_ref[...])
    row = lax.broadcasted_iota(jnp.int32, (T, T), 0)
    col = lax.broadcasted_iota(jnp.int32, (T, T), 1)
    causal = row >= col
    acs = _dot_exact_lhs01(causal.astype(f32), dt * a_neg)
    a2 = acs * LOG2E
    sel3 = sel_ref[...]
    e_exp = jnp.exp2(_expand_heads(a2, sel3))
    w_exp = _expand_heads(jnp.exp(acs[T - 1:T, :] - acs) * dt, sel3)
    cdec = e_exp[T - 1:T, :]
    b2_t = a2.T - jnp.log2(dt.T)
    xw = (xs * w_exp).astype(bf16)
    hb = ht_ref[...].astype(bf16)
    dsk = dsk_ref[...]
    lo_t = lax.broadcasted_iota(jnp.int32, (T, LANES), 1) < C_HEAD_DIM

    for g in range(C_GROUPS):
        ns = slice(g * C_D_STATE, (g + 1) * C_D_STATE)
        gs = slice(g * C_GROUP_W, (g + 1) * C_GROUP_W)
        b_g = bm[:, ns]
        c_g = cm[:, ns].astype(bf16)
        cb = _dot_nt(c_g, b_g.astype(bf16))
        yi = _dot(c_g, hb[:, gs])
        ht_ref[:, gs] = ht_ref[:, gs] * cdec[:, gs] + _dot(b_g.T.astype(bf16), xw[:, gs])
        for tt in range(C_GROUP_W // LANES):
            t = g * (C_GROUP_W // LANES) + tt
            lanes = slice(t * LANES, (t + 1) * LANES)
            xt = xb[:, lanes]
            res = []
            for half in range(2):
                h = 2 * t + half
                a_col = jnp.broadcast_to(a2[:, h:h + 1], (T, T))
                b_row = jnp.broadcast_to(b2_t[h:h + 1, :], (T, T))
                w = jnp.where(causal, cb * jnp.exp2(a_col - b_row), 0.0)
                res.append(_dot(w.astype(bf16), xt))
            y_intra = jnp.where(lo_t, res[0], res[1])
            y_ref[:, lanes] = y_intra + e_exp[:, lanes] * yi[:, tt * LANES:(tt + 1) * LANES] + dsk[:, lanes] * xs[:, lanes]

    yn_ref[...] = _gated_group_norm(y_ref[...], z, nw_ref[...]).astype(bf16)


def _head_select3():
    k = np.arange(LANES)[:, None]
    ch = np.arange(C_D_INNER)[None, :] // C_HEAD_DIM
    return jnp.asarray((k % C_HEADS == ch) & (k < 3 * C_HEADS), dtype=bf16)


def ssd_prompt(x, g, w_in, layer, w_dt, conv_w, conv_b, dt_bias, a_log, d_skip, norm_w):
    step = SSD_SUB * C_CHUNK
    nc = SEQ // step
    full = lambda *shape: pl.BlockSpec(shape, lambda b, c: (0,) * len(shape))
    once = lambda *shape: pl.BlockSpec(shape, lambda b, c: (0,) * len(shape), pipeline_mode=pl.Buffered(1))
    return pl.pallas_call(
        _ssd_prompt_kernel,
        grid=(BATCH, nc),
        in_specs=[
            pl.BlockSpec((step, D_MODEL), lambda b, c: (b * nc + c, 0)),
            full(1, D_MODEL),
            pl.BlockSpec((1,) + w_in.shape[1:], lambda b, c: (layer, 0, 0), pipeline_mode=pl.Buffered(1)),
            once(D_MODEL, LANES),
            full(C_D_CONV, C_CONV_DIM), full(1, C_CONV_DIM), full(1, LANES), full(1, LANES),
            full(1, C_D_INNER), full(1, C_D_INNER), full(LANES, C_D_INNER),
        ],
        out_specs=[
            pl.BlockSpec((step, C_D_INNER), lambda b, c: (b * nc + c, 0)),
            pl.BlockSpec((1, C_D_INNER, C_D_STATE), lambda b, c: (b, 0, 0)),
            pl.BlockSpec((1, C_D_CONV - 1, C_CONV_DIM), lambda b, c: (b, 0, 0)),
        ],
        out_shape=[
            jax.ShapeDtypeStruct((PROMPT_ROWS, C_D_INNER), bf16),
            jax.ShapeDtypeStruct((BATCH, C_D_INNER, C_D_STATE), f32),
            jax.ShapeDtypeStruct((BATCH, C_D_CONV - 1, C_CONV_DIM), f32),
        ],
        scratch_shapes=[
            pltpu.VMEM((step, D_MODEL), bf16),
            pltpu.VMEM((SSD_SUB, CONV_PAD + C_CHUNK, C_CONV_DIM), f32),
            pltpu.VMEM((C_D_STATE, C_D_INNER), f32),
            pltpu.VMEM((SSD_SUB, C_CHUNK, C_D_INNER), f32),
        ],
        compiler_params=_params("parallel", "arbitrary"),
        name="ssd_prompt",
    )(x, g, w_in, w_dt, conv_w, conv_b, dt_bias, a_log, d_skip, norm_w, _head_select3())


SSD_BB = 8
SSD_TP = SUBLANES
_N_PAIRS = DEC_SEQ * (DEC_SEQ + 1) // 2
_N_COEF = _N_PAIRS + 2 * DEC_SEQ


def _ssd_sample_kernel(zx_ref, dtr_ref, cs_ref, h0_ref, cw_ref, cb_ref, dtb_ref, alog_ref, dsk_ref, nw_ref,
                       sel_ref, yn_ref, hn_ref, cout_ref, c_scr, b_scr, xw_scr, yi_scr, cd_scr):
    L = DEC_SEQ
    cw = cw_ref[...]
    xp = [cs_ref[k] for k in range(C_D_CONV - 1)] + [zx_ref[t, :, C_D_INNER:] for t in range(L)]
    for k in range(C_D_CONV - 1):
        cout_ref[k] = xp[L + k]
    act = []
    for t in range(L):
        acc = cb_ref[...]
        for tap in range(C_D_CONV):
            acc = acc + xp[t + tap] * cw[tap:tap + 1, :]
        act.append(_silu(acc))
    xs = [a[:, :C_D_INNER] for a in act]
    bm = [a[:, C_D_INNER:C_D_INNER + C_BC_DIM] for a in act]
    cm = [a[:, C_D_INNER + C_BC_DIM:] for a in act]

    a_neg = -jnp.exp(alog_ref[...])
    dt = [_softplus(dtr_ref[t] + dtb_ref[...]) for t in range(L)]
    acs = []
    for t in range(L):
        acs.append(dt[t] * a_neg if t == 0 else acs[t - 1] + dt[t] * a_neg)

    lane_group = lax.broadcasted_iota(jnp.int32, (SSD_BB, LANES), 1) // C_REP
    coefs = []
    for t in range(L):
        for t2 in range(t + 1):
            cbh = jnp.zeros((SSD_BB, LANES), f32)
            for g in range(C_GROUPS):
                ns = slice(g * C_D_STATE, (g + 1) * C_D_STATE)
                cbg = jnp.sum(cm[t][:, ns] * bm[t2][:, ns], axis=-1, keepdims=True)
                cbh = jnp.where(lane_group == g, cbg, cbh)
            coefs.append(cbh * jnp.exp(acs[t] - acs[t2]) * dt[t2])
    for t in range(L):
        coefs.append(jnp.exp(acs[t]))
    for t in range(L):
        coefs.append(jnp.exp(acs[L - 1] - acs[t]) * dt[t])
    coef = jnp.concatenate(coefs, axis=0)
    cexp = _dot_exact_rhs01(coef, sel_ref[...])
    cexp = [cexp[k * SSD_BB:(k + 1) * SSD_BB, :] for k in range(_N_COEF)]
    w_intra = cexp[:_N_PAIRS]
    w_inter = cexp[_N_PAIRS:_N_PAIRS + L]
    w_state = cexp[_N_PAIRS + L:]

    cd = jnp.concatenate([jnp.exp(acs[L - 1]), jnp.zeros((LANES - SSD_BB, LANES), f32)], axis=0)
    cd_t = cd.T
    for s in range(SSD_BB):
        cd_scr[s] = jnp.broadcast_to(cd_t[0:C_HEADS, s:s + 1], (C_HEADS, C_D_STATE))

    zeros_tail = jnp.zeros((SSD_BB, SSD_TP - L, C_D_INNER), f32)
    c_scr[:, L:, :] = zeros_tail[:, :, :C_BC_DIM]
    b_scr[:, L:, :] = zeros_tail[:, :, :C_BC_DIM]
    xw_scr[:, L:, :] = zeros_tail
    for t in range(L):
        xw_t = xs[t] * w_state[t]
        for s in range(SSD_BB):
            c_scr[s, t:t + 1, :] = cm[t][s:s + 1, :]
            b_scr[s, t:t + 1, :] = bm[t][s:s + 1, :]
            xw_scr[s, t:t + 1, :] = xw_t[s:s + 1, :]

    for s in range(SSD_BB):
        for g in range(C_GROUPS):
            ns = slice(g * C_D_STATE, (g + 1) * C_D_STATE)
            gs = slice(g * C_GROUP_W, (g + 1) * C_GROUP_W)
            h0 = h0_ref[s, gs, :]
            yi = _dot_nt(c_scr[s, :, ns].astype(bf16), h0.astype(bf16))
            for t in range(L):
                yi_scr[t, s:s + 1, gs] = yi[t:t + 1, :]
            st = _dot_tn(xw_scr[s, :, gs].astype(bf16), b_scr[s, :, ns].astype(bf16))
            for r in range(C_REP):
                h = g * C_REP + r
                rs = slice(r * C_HEAD_DIM, (r + 1) * C_HEAD_DIM)
                scale = cd_scr[s, h:h + 1, :]
                hn_ref[s, h * C_HEAD_DIM:(h + 1) * C_HEAD_DIM, :] = h0[rs, :] * scale + st[rs, :]

    dsk = dsk_ref[...]
    nw = nw_ref[...]
    pair = 0
    for t in range(L):
        y = w_inter[t] * yi_scr[t] + dsk * xs[t]
        for t2 in range(t + 1):
            y = y + w_intra[pair] * xs[t2]
            pair += 1
        yn_ref[t] = _gated_group_norm(y, zx_ref[t, :, :C_D_INNER], nw).astype(bf16)


def ssd_sample(zx_t, dtr_t, conv_state_t, h0, conv_w, conv_b, dt_bias, a_log, d_skip, norm_w, sel):
    tmaj = lambda n, w: pl.BlockSpec((n, SSD_BB, w), lambda i: (0, i, 0))
    full = lambda *shape: pl.BlockSpec(shape, lambda i: (0,) * len(shape))
    return pl.pallas_call(
        _ssd_sample_kernel,
        grid=(DEC_BATCH // SSD_BB,),
        in_specs=[
            tmaj(DEC_SEQ, C_D_INNER + C_CONV_DIM), tmaj(DEC_SEQ, LANES), tmaj(C_D_CONV - 1, C_CONV_DIM),
            pl.BlockSpec((SSD_BB, C_D_INNER, C_D_STATE), lambda i: (i, 0, 0)),
            full(C_D_CONV, C_CONV_DIM), full(1, C_CONV_DIM), full(1, LANES), full(1, LANES),
            full(1, C_D_INNER), full(1, C_D_INNER), full(LANES, C_D_INNER),
        ],
        out_specs=[
            tmaj(DEC_SEQ, C_D_INNER),
            pl.BlockSpec((SSD_BB, C_D_INNER, C_D_STATE), lambda i: (i, 0, 0)),
            tmaj(C_D_CONV - 1, C_CONV_DIM),
        ],
        out_shape=[
            jax.ShapeDtypeStruct((DEC_SEQ, DEC_BATCH, C_D_INNER), bf16),
            jax.ShapeDtypeStruct((DEC_BATCH, C_D_INNER, C_D_STATE), f32),
            jax.ShapeDtypeStruct((C_D_CONV - 1, DEC_BATCH, C_CONV_DIM), f32),
        ],
        scratch_shapes=[
            pltpu.VMEM((SSD_BB, SSD_TP, C_BC_DIM), f32),
            pltpu.VMEM((SSD_BB, SSD_TP, C_BC_DIM), f32),
            pltpu.VMEM((SSD_BB, SSD_TP, C_D_INNER), f32),
            pltpu.VMEM((DEC_SEQ, SSD_BB, C_D_INNER), f32),
            pltpu.VMEM((SSD_BB, C_HEADS, C_D_STATE), f32),
        ],
        compiler_params=_params("parallel"),
        name="ssd_sample",
    )(zx_t, dtr_t, conv_state_t, h0, conv_w, conv_b, dt_bias, a_log, d_skip, norm_w, sel)


def _pad_lanes(v):
    return jnp.pad(v.astype(f32), (0, LANES - v.shape[0])).reshape(1, LANES)


def _mixer_a(xp, xs, g, j, w_in, norm_v, w_sp, b_sp, w_out):
    nv = norm_v.reshape(1, A_HALF)
    xs, v_s, w_in_b, w_out_b = mixer_a(xs, g, w_in, nv, w_sp[:, :DEC_SEQ, :DEC_SEQ].reshape(-1),
                                       b_sp[:, :DEC_SEQ].reshape(-1), w_out, j, tm=TM_SAMPLE, sample=True)
    xp = mixer_a(xp, g, w_in_b, nv, w_sp, b_sp.T, w_out_b, 0, tm=TM_PROMPT, sample=False)
    return xp, xs, jnp.swapaxes(v_s.reshape(DEC_SEQ, DEC_BATCH, A_HALF), 0, 1)


def _mixer_b(xp, xs, g, j, k_cache, v_cache, w_qkv, q_norm, k_norm, sinks, rel_bias):
    qn = q_norm.reshape(1, B_HEAD_DIM)
    kn = k_norm.reshape(1, B_HEAD_DIM)
    table = bias_table(rel_bias)
    n_qkv = B_Q_DIM + 2 * B_KV_DIM

    o_p, k_p, v_p = attn_prompt(xp, g, w_qkv, j, table, qn, kn, sinks)

    qkv_s = norm_matmul(xs, g, w_qkv, j, n_qkv, tm=TM_SAMPLE, tn=512).reshape(DEC_SEQ, DEC_BATCH, -1)
    q_s = qkv_s[:, :, :B_Q_DIM].reshape(DEC_SEQ, DEC_BATCH, B_KV_HEADS, B_REP, B_HEAD_DIM)
    q_s = q_s.transpose(1, 3, 0, 2, 4).reshape(DEC_BATCH, ATT_QR, B_KV_DIM)
    k_new = jnp.swapaxes(qkv_s[:, :, B_Q_DIM:B_Q_DIM + B_KV_DIM], 0, 1)
    v_new = jnp.swapaxes(qkv_s[:, :, B_Q_DIM + B_KV_DIM:], 0, 1)
    table_s = table[:, :DEC_SEQ, :ATT_KEYS].reshape(B_KV_HEADS, ATT_QR, ATT_KEYS)
    o_s, k_s, v_s = attn_sample(q_s, k_new, v_new,
                                k_cache.reshape(DEC_BATCH, B_WINDOW, B_KV_DIM),
                                v_cache.reshape(DEC_BATCH, B_WINDOW, B_KV_DIM),
                                table_s, qn, kn, sinks)
    o_s = o_s.reshape(DEC_BATCH, B_REP, DEC_SEQ, B_KV_HEADS, B_HEAD_DIM).transpose(2, 0, 3, 1, 4)
    o_s = o_s.reshape(SAMPLE_ROWS, B_Q_DIM).astype(bf16)
    kv_shape_p = (BATCH, B_WINDOW, B_KV_HEADS, B_HEAD_DIM)
    kv_shape_s = (DEC_BATCH, B_WINDOW, B_KV_HEADS, B_HEAD_DIM)
    return o_p, o_s, k_p.reshape(kv_shape_p), v_p.reshape(kv_shape_p), k_s.reshape(kv_shape_s), v_s.reshape(kv_shape_s)


def _mixer_c(xp, xs, g, j, h0, conv_state, w_in, conv_w, conv_b, dt_bias, a_log, d_skip, norm_w):
    n_zx = C_D_INNER + C_CONV_DIM
    w_dt = jnp.pad(w_in[j, :, n_zx:], ((0, 0), (0, LANES - C_HEADS)))
    cb = conv_b.reshape(1, C_CONV_DIM)
    dtb = _pad_lanes(dt_bias)
    alog = _pad_lanes(a_log)
    dsk = jnp.repeat(d_skip.astype(f32), C_HEAD_DIM).reshape(1, C_D_INNER)
    nw = norm_w.reshape(1, C_D_INNER)

    yn_p, h_p, conv_p = ssd_prompt(xp, g, w_in, j, w_dt, conv_w, cb, dtb, alog, dsk, nw)
    zx_s, dtr_s = norm_matmul(xs, g, w_in, j, n_zx, tm=TM_SAMPLE, tn=1024, w_tail=w_dt)
    sel = (jnp.arange(LANES)[:, None] == jnp.arange(C_D_INNER)[None, :] // C_HEAD_DIM).astype(f32)
    yn_s, h_s, conv_s = ssd_sample(
        zx_s.reshape(DEC_SEQ, DEC_BATCH, -1), dtr_s.reshape(DEC_SEQ, DEC_BATCH, LANES),
        jnp.swapaxes(conv_state, 0, 1), h0.reshape(DEC_BATCH, C_D_INNER, C_D_STATE),
        conv_w, cb, dtb, alog, dsk, nw, sel)
    st_shape = (C_HEADS, C_HEAD_DIM, C_D_STATE)
    return (yn_p, yn_s.reshape(SAMPLE_ROWS, C_D_INNER), h_p.reshape((BATCH,) + st_shape), conv_p,
            h_s.reshape((DEC_BATCH,) + st_shape), jnp.swapaxes(conv_s, 0, 1))


def kernel(x_prompt, x_sample, cache_swa_k, cache_swa_v, state_ssm, state_conv, norm_mixer, norm_mlp, mlp_w_up, mlp_w_down, a_w_in, a_norm_v, a_w_spatial, a_b_spatial, a_w_out, b_w_qkv, b_q_norm, b_k_norm, b_sinks, rel_bias, b_w_out, c_w_in, c_conv_w, c_conv_b, c_dt_bias, c_a_log, c_d, c_norm, c_w_out):
    xp = x_prompt.reshape(PROMPT_ROWS, D_MODEL)
    xs = jnp.swapaxes(x_sample, 0, 1).reshape(SAMPLE_ROWS, D_MODEL)
    chunk_v_s = []
    swa_kp, swa_vp, swa_ks, swa_vs = [], [], [], []
    ssm_p, conv_p, ssm_s, conv_s = [], [], [], []
    b_w_qkv, b_w_out = b_w_qkv.astype(bf16), b_w_out.astype(bf16)
    c_w_in, c_w_out = c_w_in.astype(bf16), c_w_out.astype(bf16)
    for i in range(DEPTH):
        kind = i % N_MIXERS
        j = i // N_MIXERS
        g = norm_mixer[i].reshape(1, D_MODEL)
        proj_p = proj_s = None
        if kind == 0:
            xp, xs, v_new = _mixer_a(xp, xs, g, j, a_w_in, a_norm_v[j], a_w_spatial[j], a_b_spatial[j], a_w_out)
            chunk_v_s.append(v_new)
        elif kind == 1:
            o_p, o_s, kp, vp, ks_, vs_ = _mixer_b(xp, xs, g, j, cache_swa_k[j], cache_swa_v[j], b_w_qkv, b_q_norm[j],
                                                  b_k_norm[j], b_sinks[j], rel_bias)
            proj_p, proj_s = (o_p, b_w_out, j), (o_s, b_w_out, j)
            swa_kp.append(kp); swa_vp.append(vp); swa_ks.append(ks_); swa_vs.append(vs_)
        else:
            y_p, y_s, hp, bp, hs, bs = _mixer_c(xp, xs, g, j, state_ssm[j], state_conv[j], c_w_in, c_conv_w[j],
                                                c_conv_b[j], c_dt_bias[j], c_a_log[j], c_d[j], c_norm[j])
            proj_p, proj_s = (y_p, c_w_out, j), (y_s, c_w_out, j)
            ssm_p.append(hp); conv_p.append(bp); ssm_s.append(hs); conv_s.append(bs)
        gm = norm_mlp[i].reshape(1, D_MODEL)
        xs, w_up_b, w_down_b = mlp(xs, gm, mlp_w_up, mlp_w_down, i, tm=TM_SAMPLE, tf=MLP_TF, proj=proj_s, emit=True)
        wide = proj_p is None or proj_p[0].shape[1] <= D_MODEL
        xp = mlp(xp, gm, w_up_b, w_down_b, 0, tm=TM_PROMPT, tf=2 * MLP_TF if wide else MLP_TF, proj=proj_p)
    y_prompt = xp.reshape(BATCH, SEQ, D_MODEL)
    y_sample = jnp.swapaxes(xs.reshape(DEC_SEQ, DEC_BATCH, D_MODEL), 0, 1)
    return (y_prompt, y_sample, jnp.stack(chunk_v_s),
            jnp.stack(swa_kp), jnp.stack(swa_vp), jnp.stack(swa_ks), jnp.stack(swa_vs),
            jnp.stack(ssm_p), jnp.stack(conv_p), jnp.stack(ssm_s), jnp.stack(conv_s))
```

```python
import functools
import math

import jax
import jax.numpy as jnp
import numpy as np
from jax import lax
from jax.experimental import pallas as pl
from jax.experimental.pallas import tpu as pltpu

f32 = jnp.float32
bf16 = jnp.bfloat16

D_MODEL = 1024
BATCH = 4
SEQ = 4096
DEPTH = 4
DEC_BATCH = 128
DEC_SEQ = 4
PAST_LEN = 8192
N_MIXERS = 3
D_FF = 4 * D_MODEL
EPS = 1e-6
NEG_INF = -1e30

A_CHUNK = 128
A_D_FFN = 6 * D_MODEL
A_HALF = A_D_FFN // 2
A_GROUPS = 8
A_GROUP_W = A_HALF // A_GROUPS

B_HEADS = 16
B_KV_HEADS = 4
B_HEAD_DIM = 64
B_REP = B_HEADS // B_KV_HEADS
B_WINDOW = 128
B_BLOCK = 128
B_Q_DIM = B_HEADS * B_HEAD_DIM
B_KV_DIM = B_KV_HEADS * B_HEAD_DIM
N_BUCKETS = 32
MAX_DISTANCE = 128

C_D_INNER = 2 * D_MODEL
C_HEAD_DIM = 64
C_HEADS = C_D_INNER // C_HEAD_DIM
C_GROUPS = 4
C_REP = C_HEADS // C_GROUPS
C_D_STATE = 128
C_D_CONV = 4
C_BC_DIM = C_GROUPS * C_D_STATE
C_CONV_DIM = C_D_INNER + 2 * C_BC_DIM
C_GROUP_W = C_D_INNER // C_GROUPS
C_CHUNK = 128

LANES = 128
SUBLANES = 8
VMEM_LIMIT_BYTES = 56 * 1024 * 1024

PROMPT_ROWS = BATCH * SEQ
SAMPLE_ROWS = DEC_BATCH * DEC_SEQ
TM_PROMPT = 1024
TM_SAMPLE = SAMPLE_ROWS
MLP_TF = 1024
QKV_TN_SAMPLE = 512
ZX_TN_SAMPLE = 1024


def _params(*sem):
    return pltpu.CompilerParams(dimension_semantics=sem, vmem_limit_bytes=VMEM_LIMIT_BYTES)


def _rms(x, g):
    ms = jnp.mean(x * x, axis=-1, keepdims=True)
    return x * lax.rsqrt(ms + EPS) * g


def _gelu(x):
    return 0.5 * x * (1.0 + lax.erf(x * math.sqrt(0.5)))


LOG2E = math.log2(math.e)


def _silu(x):
    return x * jax.nn.sigmoid(x)


def _softplus(x):
    return jnp.maximum(x, 0.0) + jnp.log1p(jnp.exp(-jnp.abs(x)))


def _dot(a, b):
    return jnp.dot(a, b, preferred_element_type=f32)


def _dot_nt(a, b):
    return lax.dot_general(a, b, (((1,), (1,)), ((), ())), preferred_element_type=f32)


def _dot_tn(a, b):
    return lax.dot_general(a, b, (((0,), (0,)), ((), ())), preferred_element_type=f32)


def _dot_exact_lhs01(a01, x):
    a = a01.astype(bf16)
    hi = x.astype(bf16)
    r1 = x - hi.astype(f32)
    mid = r1.astype(bf16)
    lo = (r1 - mid.astype(f32)).astype(bf16)
    return _dot(a, hi) + _dot(a, mid) + _dot(a, lo)


def _dot_exact_rhs01(x, b01):
    b = b01.astype(bf16)
    hi = x.astype(bf16)
    r1 = x - hi.astype(f32)
    mid = r1.astype(bf16)
    lo = (r1 - mid.astype(f32)).astype(bf16)
    return _dot(hi, b) + _dot(mid, b) + _dot(lo, b)


def _norm_matmul_kernel(*refs, nj, tail):
    if tail:
        x_ref, g_ref, w_ref, wt_ref, o_ref, ot_ref, xn_ref = refs
    else:
        x_ref, g_ref, w_ref, o_ref, xn_ref = refs
    j = pl.program_id(1)

    @pl.when(j == 0)
    def _():
        xn_ref[...] = _rms(x_ref[...], g_ref[...]).astype(bf16)

    @pl.when(j < nj)
    def _():
        o_ref[...] = _dot(xn_ref[...], w_ref[0].astype(bf16))

    if tail:
        @pl.when(j == nj)
        def _():
            ot_ref[...] = _dot(xn_ref[...], wt_ref[...].astype(bf16))


def norm_matmul(x, g, w, layer, n, *, tm, tn, w_tail=None):
    m, k = x.shape
    nj = n // tn
    tail = w_tail is not None
    last = nj - 1
    in_specs = [
        pl.BlockSpec((tm, k), lambda i, j: (i, 0)),
        pl.BlockSpec((1, k), lambda i, j: (0, 0)),
        pl.BlockSpec((1, k, tn), lambda i, j: (layer, 0, jnp.minimum(j, last))),
    ]
    out_specs = [pl.BlockSpec((tm, tn), lambda i, j: (i, jnp.minimum(j, last)))]
    out_shape = [jax.ShapeDtypeStruct((m, n), f32)]
    args = [x, g, w]
    if tail:
        in_specs.append(pl.BlockSpec((k, LANES), lambda i, j: (0, 0)))
        out_specs.append(pl.BlockSpec((tm, LANES), lambda i, j: (i, 0)))
        out_shape.append(jax.ShapeDtypeStruct((m, LANES), f32))
        args.append(w_tail)
    out = pl.pallas_call(
        functools.partial(_norm_matmul_kernel, nj=nj, tail=tail),
        grid=(m // tm, nj + (1 if tail else 0)),
        in_specs=in_specs,
        out_specs=out_specs,
        out_shape=out_shape,
        scratch_shapes=[pltpu.VMEM((tm, k), bf16)],
        compiler_params=_params("parallel", "arbitrary"),
        name="norm_matmul",
    )(*args)
    return out if tail else out[0]


def _mlp_kernel(*refs, proj, emit):
    refs = list(refs)
    a_ref, wo_ref = (refs.pop(0), refs.pop(0)) if proj else (None, None)
    x_ref, g_ref, wu_ref, wd_ref, o_ref = refs[:5]
    wub_ref, wdb_ref = (refs[5], refs[6]) if emit else (None, None)
    xn_ref = refs[-1]

    @pl.when(pl.program_id(1) == 0)
    def _():
        x = x_ref[...]
        if proj:
            x = x + _dot(a_ref[...], wo_ref[0])
        xn_ref[...] = _rms(x, g_ref[...]).astype(bf16)
        o_ref[...] = x

    wu = wu_ref[0].astype(bf16)
    wd = wd_ref[0].astype(bf16)
    if emit:
        wub_ref[0] = wu
        wdb_ref[0] = wd
    h = jnp.maximum(_dot(xn_ref[...], wu), 0.0)
    o_ref[...] += _dot((h * h).astype(bf16), wd)


def mlp(x, g, w_up, w_down, layer, *, tm, tf, proj=None, emit=False):
    m, d = x.shape
    ff = w_up.shape[2]
    assert not emit or m == tm
    in_specs = [
        pl.BlockSpec((tm, d), lambda i, j: (i, 0)),
        pl.BlockSpec((1, d), lambda i, j: (0, 0)),
        pl.BlockSpec((1, d, tf), lambda i, j: (layer, 0, j)),
        pl.BlockSpec((1, tf, d), lambda i, j: (layer, j, 0)),
    ]
    args = [x, g, w_up, w_down]
    if proj is not None:
        a, w_o, lo = proj
        k = a.shape[1]
        in_specs = [pl.BlockSpec((tm, k), lambda i, j: (i, 0)),
                    pl.BlockSpec((1, k, d), lambda i, j: (lo, 0, 0), pipeline_mode=pl.Buffered(1))] + in_specs
        args = [a, w_o] + args
    out_specs = [pl.BlockSpec((tm, d), lambda i, j: (i, 0))]
    out_shape = [jax.ShapeDtypeStruct((m, d), f32)]
    if emit:
        out_specs += [pl.BlockSpec((1, d, tf), lambda i, j: (0, 0, j)), pl.BlockSpec((1, tf, d), lambda i, j: (0, j, 0))]
        out_shape += [jax.ShapeDtypeStruct((1, d, ff), bf16), jax.ShapeDtypeStruct((1, ff, d), bf16)]
    out = pl.pallas_call(
        functools.partial(_mlp_kernel, proj=proj is not None, emit=emit),
        grid=(m // tm, ff // tf),
        in_specs=in_specs,
        out_specs=out_specs,
        out_shape=out_shape,
        scratch_shapes=[pltpu.VMEM((tm, d), bf16)],
        compiler_params=_params("parallel", "arbitrary"),
        name="mlp",
    )(*args)
    return out if emit else out[0]


A_BLK_GROUPS = 2
A_BLK = A_BLK_GROUPS * A_GROUP_W
A_NBLK = A_HALF // A_BLK


def _mixer_a_kernel(*refs, sample, v_steps):
    if sample:
        (ws_ref, bs_ref, x_ref, g_ref, win_ref, nv_ref, wout_ref, o_ref, vo_ref, winb_ref, woutb_ref,
         xn_ref, v_ref, ssq_ref, us_ref) = refs
        winv_ref = win_ref
    else:
        (x_ref, g_ref, winv_ref, win_ref, nv_ref, ws_ref, bs_ref, wout_ref, o_ref,
         xn_ref, v_ref, ssq_ref, us_ref) = refs
        winb_ref = woutb_ref = None
    j = pl.program_id(1)
    tm = x_ref.shape[0]
    v_blk = A_HALF // v_steps

    def w_in_block(ref=None):
        w = (win_ref if ref is None else ref)[0].astype(bf16)
        if winb_ref is not None:
            winb_ref[0] = w
        return w

    def w_out_block():
        w = wout_ref[0].astype(bf16)
        if woutb_ref is not None:
            woutb_ref[0] = w
        return w

    @pl.when(j == 0)
    def _():
        xn_ref[...] = _rms(x_ref[...], g_ref[...]).astype(bf16)
        ssq_ref[...] = jnp.zeros(ssq_ref.shape, f32)

    for k in range(v_steps):
        @pl.when(j == k)
        def _(k=k):
            v = _gelu(_dot(xn_ref[...], w_in_block(winv_ref)))
            v_ref[:, k * v_blk:(k + 1) * v_blk] = v
            ssq_ref[...] += jnp.sum(v * v, axis=-1, keepdims=True)

    if not sample:
        row = lax.broadcasted_iota(jnp.int32, (A_CHUNK, A_CHUNK), 0)
        col = lax.broadcasted_iota(jnp.int32, (A_CHUNK, A_CHUNK), 1)
        causal = row >= col

    for k in range(A_NBLK):
        @pl.when(j == v_steps + k)
        def _(k=k):
            u = _gelu(_dot(xn_ref[...], w_in_block()))
            rinv = lax.rsqrt(ssq_ref[...] * (1.0 / A_HALF) + EPS)
            for gg in range(A_BLK_GROUPS):
                g = k * A_BLK_GROUPS + gg
                cols = slice(g * A_GROUP_W, (g + 1) * A_GROUP_W)
                ucols = slice(gg * A_GROUP_W, (gg + 1) * A_GROUP_W)
                vn = v_ref[:, cols] * rinv * nv_ref[:, cols]
                if sample:
                    vo_ref[:, cols] = vn
                    vt = [vn[t * DEC_BATCH:(t + 1) * DEC_BATCH] for t in range(DEC_SEQ)]
                    s_rows = []
                    for t in range(DEC_SEQ):
                        s = ws_ref[(g * DEC_SEQ + t) * DEC_SEQ] * vt[0]
                        for t2 in range(1, t + 1):
                            s = s + ws_ref[(g * DEC_SEQ + t) * DEC_SEQ + t2] * vt[t2]
                        s_rows.append(s + bs_ref[g * DEC_SEQ + t])
                    s = jnp.concatenate(s_rows, axis=0)
                else:
                    w = jnp.where(causal, ws_ref[g], 0.0).astype(bf16)
                    bias = bs_ref[:, g:g + 1]
                    vb = vn.astype(bf16)
                    s = jnp.concatenate(
                        [_dot(w, vb[c * A_CHUNK:(c + 1) * A_CHUNK]) + bias for c in range(tm // A_CHUNK)], axis=0)
                us_ref[:, ucols] = (u[:, ucols] * s).astype(bf16)
            y = _dot(us_ref[...], w_out_block())
            if k == 0:
                o_ref[...] = x_ref[...] + y
            else:
                o_ref[...] += y


def mixer_a(x, g, w_in, norm_v, w_sp, b_sp, w_out, layer, *, tm, sample):
    m, d = x.shape
    v_steps = A_NBLK if sample else A_NBLK // 2
    v_blk = A_HALF // v_steps
    nj = v_steps + A_NBLK
    row = lambda w: pl.BlockSpec((tm, w), lambda i, j: (i, 0))
    full = lambda *shape: pl.BlockSpec(shape, lambda i, j: (0,) * len(shape))
    smem = pl.BlockSpec(memory_space=pltpu.SMEM)
    win_spec = pl.BlockSpec((1, d, A_BLK), lambda i, j: (layer, 0, (j + A_NBLK) % nj))
    winv_spec = pl.BlockSpec((1, d, v_blk), lambda i, j: (layer, 0, v_steps + jnp.minimum(j, v_steps - 1)))
    winu_spec = pl.BlockSpec((1, d, A_BLK), lambda i, j: (layer, 0, jnp.maximum(j - v_steps, 0)))
    wout_spec = pl.BlockSpec((1, A_BLK, d), lambda i, j: (layer, jnp.maximum(j - v_steps, 0), 0))
    if sample:
        in_specs = [smem, smem, row(d), full(1, d), win_spec, full(1, A_HALF), wout_spec]
        args = (w_sp, b_sp, x, g, w_in, norm_v, w_out)
        out_specs = [row(d), row(A_HALF),
                     pl.BlockSpec((1, d, A_BLK), lambda i, j: (0, 0, (j + A_NBLK) % nj)),
                     pl.BlockSpec((1, A_BLK, d), lambda i, j: (0, jnp.maximum(j - A_NBLK, 0), 0))]
        out_shape = [jax.ShapeDtypeStruct((m, d), f32), jax.ShapeDtypeStruct((m, A_HALF), f32),
                     jax.ShapeDtypeStruct((1, d, 2 * A_HALF), bf16), jax.ShapeDtypeStruct((1, A_HALF, d), bf16)]
    else:
        in_specs = [row(d), full(1, d), winv_spec, winu_spec, full(1, A_HALF), full(A_GROUPS, A_CHUNK, A_CHUNK),
                    full(A_CHUNK, A_GROUPS), wout_spec]
        args = (x, g, w_in, w_in, norm_v, w_sp, b_sp, w_out)
        out_specs = row(d)
        out_shape = jax.ShapeDtypeStruct((m, d), f32)
    return pl.pallas_call(
        functools.partial(_mixer_a_kernel, sample=sample, v_steps=v_steps),
        grid=(m // tm, nj),
        in_specs=in_specs,
        out_specs=out_specs,
        out_shape=out_shape,
        scratch_shapes=[pltpu.VMEM((tm, d), bf16), pltpu.VMEM((tm, A_HALF), f32), pltpu.VMEM((tm, 1), f32),
                        pltpu.VMEM((tm, A_BLK), bf16)],
        compiler_params=_params("parallel", "arbitrary"),
        name="mixer_a_sample" if sample else "mixer_a_prompt",
    )(*args)


def _bucket_table():
    i = np.arange(B_BLOCK)[:, None]
    j = np.arange(2 * B_BLOCK)[None, :]
    n = np.maximum(B_BLOCK + i - j, 0)
    max_exact = N_BUCKETS // 2
    nf = np.maximum(n, 1).astype(np.float64)
    val = np.log(nf / max_exact) / math.log(MAX_DISTANCE / max_exact) * (N_BUCKETS - max_exact)
    in_window = (n >= max_exact) & (n < B_WINDOW)
    assert np.all(np.abs(val - np.round(val))[in_window & (n != max_exact)] > 1e-3)
    large = np.minimum(max_exact + np.floor(val + 1e-9).astype(np.int64), N_BUCKETS - 1)
    return np.where(n < max_exact, n, large).astype(np.int32)


def _bias_table_kernel(rb_ref, sink_ref, bk_ref, o_ref):
    bk = bk_ref[...]
    i = lax.broadcasted_iota(jnp.int32, bk.shape, 0)
    j = lax.broadcasted_iota(jnp.int32, bk.shape, 1)
    in_window = (j > i) & (j <= i + B_WINDOW)
    for h in range(B_HEADS):
        acc = jnp.zeros(bk.shape, f32)
        for b in range(N_BUCKETS):
            acc = jnp.where(bk == b, rb_ref[b * B_HEADS + h], acc)
        acc = jnp.where(in_window, acc, NEG_INF)
        o_ref[h] = jnp.where(j == 0, sink_ref[h], acc)


def bias_table(rel_bias, sinks):
    smem = pl.BlockSpec(memory_space=pltpu.SMEM)
    return pl.pallas_call(
        _bias_table_kernel,
        in_specs=[smem, smem, pl.BlockSpec(memory_space=pltpu.VMEM)],
        out_specs=pl.BlockSpec(memory_space=pltpu.VMEM),
        out_shape=jax.ShapeDtypeStruct((B_HEADS, B_BLOCK, 2 * B_BLOCK), f32),
        name="bias_table",
    )(rel_bias.reshape(-1), sinks, jnp.asarray(_bucket_table()))


def _softmax_with_sink(logits, sink):
    m = jnp.maximum(jnp.max(logits, axis=-1, keepdims=True), sink)
    p = jnp.exp(logits - m)
    return p, jnp.sum(p, axis=-1, keepdims=True) + jnp.exp(sink - m)


def _rms_head_pairs(x, g2, lo):
    sq = x * x
    s_lo = jnp.sum(jnp.where(lo, sq, 0.0), axis=-1, keepdims=True)
    s_hi = jnp.sum(jnp.where(lo, 0.0, sq), axis=-1, keepdims=True)
    r = lax.rsqrt(jnp.where(lo, s_lo, s_hi) * (1.0 / B_HEAD_DIM) + EPS)
    return x * r * g2


ATT_SUB = 8


def _attn_prompt_kernel(x_ref, g_ref, w_ref, tab_ref, qn_ref, kn_ref, o_ref, ko_ref, vo_ref,
                        xn_ref, kband_ref, vband_ref, q_ref, p_ref, rhs_ref):
    n = pl.program_id(1)
    T = B_BLOCK

    @pl.when(n == 0)
    def _():
        kband_ref[0:T, :] = jnp.zeros((T, B_KV_DIM), f32)
        vband_ref[0:T, :] = jnp.zeros((T, B_KV_DIM), f32)
        rhs_ref[...] = jnp.ones(rhs_ref.shape, bf16)

    xn_ref[...] = _rms(x_ref[...], g_ref[...]).astype(bf16)
    for sub in range(ATT_SUB):
        rows = pl.ds(sub * T, T)
        qkv = _dot(xn_ref[rows, :], w_ref[0])
        first_key = jnp.where(n == 0, T, 0) if sub == 0 else 0
        _attn_block(first_key, qkv, tab_ref, qn_ref, kn_ref, o_ref.at[rows], ko_ref, vo_ref,
                    kband_ref, vband_ref, q_ref.at[sub], p_ref.at[sub], rhs_ref.at[sub])


def _attn_block(first_key, qkv_ref, tab_ref, qn_ref, kn_ref, o_ref, ko_ref, vo_ref,
                kband_ref, vband_ref, q_ref, p_ref, rhs_ref):
    T = B_BLOCK
    lo = lax.broadcasted_iota(jnp.int32, (1, LANES), 1) < B_HEAD_DIM
    qn2 = qn_ref[...]
    kn2 = kn_ref[...]
    for t in range(B_KV_DIM // LANES):
        lanes = slice(t * LANES, (t + 1) * LANES)
        k2 = _rms_head_pairs(qkv_ref[:, B_Q_DIM + t * LANES:B_Q_DIM + (t + 1) * LANES].astype(f32), kn2, lo)
        ko_ref[0, :, lanes] = k2
        kband_ref[T:, lanes] = k2
    v = qkv_ref[:, B_Q_DIM + B_KV_DIM:].astype(f32)
    vo_ref[0] = v
    vband_ref[T:, :] = v
    for t in range(B_Q_DIM // LANES):
        q2 = qkv_ref[:, t * LANES:(t + 1) * LANES].astype(f32)
        q_ref[t * T:(t + 1) * T, :] = _rms_head_pairs(q2, qn2, lo).astype(bf16)

    key_ops = {}
    band_row = lax.broadcasted_iota(jnp.int32, (2 * T, LANES), 0)
    for t in range(B_KV_DIM // LANES):
        lanes = slice(t * LANES, (t + 1) * LANES)
        kt = jnp.where(band_row == 0, 0.0, kband_ref[:, lanes])
        kr = pltpu.roll(kt, B_HEAD_DIM, axis=1)
        vt = jnp.where(band_row == 0, 0.0, vband_ref[:, lanes])
        vr = pltpu.roll(vt, B_HEAD_DIM, axis=1)
        hi = jnp.logical_not(lo)
        for half, (ksrc, vsrc) in enumerate(((kt, vt), (kr, vr))):
            g_lo, g_hi = (2 * t, 2 * t + 1) if half == 0 else (2 * t + 1, 2 * t)
            key_ops[(g_lo, 0)] = jnp.where(lo, ksrc, 0.0).astype(bf16)
            key_ops[(g_hi, 1)] = jnp.where(hi, ksrc, 0.0).astype(bf16)
            rhs_ref[g_lo * 2 + 0, :, 0:LANES] = jnp.where(lo, vsrc, 1.0).astype(bf16)
            rhs_ref[g_hi * 2 + 1, :, 0:LANES] = jnp.where(hi, vsrc, 1.0).astype(bf16)

    masked_first = not (isinstance(first_key, int) and first_key == 0)
    if masked_first:
        j = lax.broadcasted_iota(jnp.int32, (T, 2 * T), 1)
        keep = (j >= first_key) | (j == 0)
    lo_t = lax.broadcasted_iota(jnp.int32, (T, LANES), 1) < B_HEAD_DIM
    for g in range(B_KV_HEADS):
        q2 = q_ref[2 * g * T:(2 * g + 2) * T, :]
        res = []
        for half in range(2):
            logits = _dot_nt(q2, key_ops[(g, half)])
            for pair in range(2):
                h = g * B_REP + 2 * pair + half
                l = logits[pair * T:(pair + 1) * T] + tab_ref[h]
                if masked_first:
                    l = jnp.where(keep, l, NEG_INF)
                p = jnp.exp(l - jnp.max(l, axis=-1, keepdims=True))
                p_ref[g * 2 + half, pair * T:(pair + 1) * T, :] = p.astype(bf16)
            res.append(_dot(p_ref[g * 2 + half], rhs_ref[g * 2 + half]))
        for pair in range(2):
            rows = slice(pair * T, (pair + 1) * T)
            even = res[0][rows, 0:LANES] / res[0][rows, LANES:]
            odd = res[1][rows, 0:LANES] / res[1][rows, LANES:]
            t = 2 * g + pair
            o_ref[:, t * LANES:(t + 1) * LANES] = jnp.where(lo_t, even, odd).astype(bf16)
    kband_ref[0:T, :] = kband_ref[T:, :]
    vband_ref[0:T, :] = vband_ref[T:, :]


def attn_prompt(x, g, w_qkv, layer, table, q_norm, k_norm):
    step = ATT_SUB * B_BLOCK
    nb = SEQ // step
    return pl.pallas_call(
        _attn_prompt_kernel,
        grid=(BATCH, nb),
        in_specs=[
            pl.BlockSpec((step, D_MODEL), lambda b, n: (b * nb + n, 0)),
            pl.BlockSpec((1, D_MODEL), lambda b, n: (0, 0)),
            pl.BlockSpec((1,) + w_qkv.shape[1:], lambda b, n: (layer, 0, 0), pipeline_mode=pl.Buffered(1)),
            pl.BlockSpec((B_HEADS, B_BLOCK, 2 * B_BLOCK), lambda b, n: (0, 0, 0)),
            pl.BlockSpec((1, LANES), lambda b, n: (0, 0)),
            pl.BlockSpec((1, LANES), lambda b, n: (0, 0)),
        ],
        out_specs=[
            pl.BlockSpec((step, B_Q_DIM), lambda b, n: (b * nb + n, 0)),
            pl.BlockSpec((1, B_BLOCK, B_KV_DIM), lambda b, n: (b, 0, 0)),
            pl.BlockSpec((1, B_BLOCK, B_KV_DIM), lambda b, n: (b, 0, 0)),
        ],
        out_shape=[
            jax.ShapeDtypeStruct((PROMPT_ROWS, B_Q_DIM), bf16),
            jax.ShapeDtypeStruct((BATCH, B_BLOCK, B_KV_DIM), f32),
            jax.ShapeDtypeStruct((BATCH, B_BLOCK, B_KV_DIM), f32),
        ],
        scratch_shapes=[
            pltpu.VMEM((step, D_MODEL), bf16),
            pltpu.VMEM((2 * B_BLOCK, B_KV_DIM), f32),
            pltpu.VMEM((2 * B_BLOCK, B_KV_DIM), f32),
            pltpu.VMEM((ATT_SUB, B_Q_DIM // LANES * B_BLOCK, LANES), bf16),
            pltpu.VMEM((ATT_SUB, 2 * B_KV_HEADS, 2 * B_BLOCK, 2 * B_BLOCK), bf16),
            pltpu.VMEM((ATT_SUB, 2 * B_KV_HEADS, 2 * B_BLOCK, 2 * LANES), bf16),
        ],
        compiler_params=_params("parallel", "arbitrary"),
        name="attn_prompt",
    )(x, g, w_qkv, table, jnp.tile(q_norm * (B_HEAD_DIM ** -0.5), (1, LANES // B_HEAD_DIM)),
      jnp.tile(k_norm, (1, LANES // B_HEAD_DIM)))


ATT_BB = 8
ATT_QR = B_REP * DEC_SEQ
ATT_KEYS = B_WINDOW + 2 * DEC_SEQ


def _attn_sample_kernel(sink_ref, q_ref, kn_ref, vn_ref, kc_ref, vc_ref, tab_ref, qn_ref, knm_ref,
                        o_ref, ko_ref, vo_ref, kall_ref, vall_ref):
    qn2 = qn_ref[...]
    kn2 = knm_ref[...]
    n_rows = B_KV_HEADS * ATT_QR
    lo = lax.broadcasted_iota(jnp.int32, (1, LANES), 1) < B_HEAD_DIM
    row = lax.broadcasted_iota(jnp.int32, (n_rows, ATT_KEYS), 0)
    j = lax.broadcasted_iota(jnp.int32, (n_rows, ATT_KEYS), 1)
    t = row % DEC_SEQ
    valid = (j > t) & (j <= t + B_WINDOW)
    q_row_group = lax.broadcasted_iota(jnp.int32, (n_rows, B_KV_DIM), 0) // ATT_QR
    q_lane_group = lax.broadcasted_iota(jnp.int32, (n_rows, B_KV_DIM), 1) // B_HEAD_DIM
    own_group = q_row_group == q_lane_group
    o_lane_group = lax.broadcasted_iota(jnp.int32, (ATT_QR, B_KV_DIM), 1) // B_HEAD_DIM
    pad = jnp.zeros((ATT_KEYS - B_WINDOW - DEC_SEQ, B_KV_DIM), f32)
    bias = tab_ref[...]
    sink = sink_ref[...]
    for s in range(ATT_BB):
        kc = kc_ref[s]
        vc = vc_ref[s]
        k_new = jnp.concatenate(
            [_rms_head_pairs(kn_ref[s, :, tt * LANES:(tt + 1) * LANES], kn2, lo) for tt in range(B_KV_DIM // LANES)],
            axis=1)
        v_new = vn_ref[s]
        ko_ref[s, 0:B_WINDOW - DEC_SEQ, :] = kc[DEC_SEQ:, :]
        ko_ref[s, B_WINDOW - DEC_SEQ:, :] = k_new
        vo_ref[s, 0:B_WINDOW - DEC_SEQ, :] = vc[DEC_SEQ:, :]
        vo_ref[s, B_WINDOW - DEC_SEQ:, :] = v_new
        kall_ref[s, 0:B_WINDOW, :] = kc
        kall_ref[s, B_WINDOW:B_WINDOW + DEC_SEQ, :] = k_new
        kall_ref[s, B_WINDOW + DEC_SEQ:, :] = pad
        vall_ref[s, 0:B_WINDOW, :] = vc
        vall_ref[s, B_WINDOW:B_WINDOW + DEC_SEQ, :] = v_new
        vall_ref[s, B_WINDOW + DEC_SEQ:, :] = pad
        q = q_ref[s]
        qn = jnp.concatenate(
            [_rms_head_pairs(q[:, tt * LANES:(tt + 1) * LANES], qn2, lo) for tt in range(B_KV_DIM // LANES)], axis=1)
        q_all = jnp.where(own_group, jnp.concatenate([qn] * B_KV_HEADS, axis=0), 0.0).astype(bf16)
        logits = _dot_nt(q_all, kall_ref[s].astype(bf16)) * (B_HEAD_DIM ** -0.5)
        logits = jnp.where(valid, logits + bias, NEG_INF)
        p, denom = _softmax_with_sink(logits, sink)
        res = _dot(p.astype(bf16), vall_ref[s].astype(bf16)) / denom
        out = jnp.zeros((ATT_QR, B_KV_DIM), f32)
        for g in range(B_KV_HEADS):
            out = jnp.where(o_lane_group == g, res[g * ATT_QR:(g + 1) * ATT_QR, :], out)
        o_ref[s] = out


def attn_sample(q_s, k_new, v_new, k_cache, v_cache, table_s, q_norm, k_norm, sinks):
    blk = lambda *shape: pl.BlockSpec((ATT_BB,) + shape, lambda i: (i,) + (0,) * len(shape))
    full = lambda *shape: pl.BlockSpec(shape, lambda i: (0,) * len(shape))
    return pl.pallas_call(
        _attn_sample_kernel,
        grid=(DEC_BATCH // ATT_BB,),
        in_specs=[
            full(B_KV_HEADS * ATT_QR, 1),
            blk(ATT_QR, B_KV_DIM), blk(DEC_SEQ, B_KV_DIM), blk(DEC_SEQ, B_KV_DIM),
            blk(B_WINDOW, B_KV_DIM), blk(B_WINDOW, B_KV_DIM),
            full(B_KV_HEADS * ATT_QR, ATT_KEYS), full(1, LANES), full(1, LANES),
        ],
        out_specs=[blk(ATT_QR, B_KV_DIM), blk(B_WINDOW, B_KV_DIM), blk(B_WINDOW, B_KV_DIM)],
        out_shape=[
            jax.ShapeDtypeStruct((DEC_BATCH, ATT_QR, B_KV_DIM), f32),
            jax.ShapeDtypeStruct((DEC_BATCH, B_WINDOW, B_KV_DIM), f32),
            jax.ShapeDtypeStruct((DEC_BATCH, B_WINDOW, B_KV_DIM), f32),
        ],
        scratch_shapes=[pltpu.VMEM((ATT_BB, ATT_KEYS, B_KV_DIM), f32), pltpu.VMEM((ATT_BB, ATT_KEYS, B_KV_DIM), f32)],
        compiler_params=_params("parallel"),
        name="attn_sample",
    )(jnp.repeat(sinks, DEC_SEQ).reshape(B_KV_HEADS * ATT_QR, 1), q_s, k_new, v_new, k_cache, v_cache,
      table_s.reshape(B_KV_HEADS * ATT_QR, ATT_KEYS),
      jnp.tile(q_norm, (1, LANES // B_HEAD_DIM)), jnp.tile(k_norm, (1, LANES // B_HEAD_DIM)))


CONV_PAD = SUBLANES

def _gated_group_norm(y, z, norm_w):
    gt = y * _silu(z)
    parts = []
    for g in range(C_GROUPS):
        gg = gt[:, g * C_GROUP_W:(g + 1) * C_GROUP_W]
        parts.append(gg * lax.rsqrt(jnp.mean(gg * gg, axis=-1, keepdims=True) + EPS))
    return jnp.concatenate(parts, axis=1) * norm_w


def _expand_heads(v, sel3):
    lane = lax.broadcasted_iota(jnp.int32, (1, LANES), 1)
    v = jnp.where(lane < C_HEADS, v, 0.0)
    hi = v.astype(bf16).astype(f32)
    r1 = v - hi
    mid = r1.astype(bf16).astype(f32)
    lo = r1 - mid
    packed = hi + pltpu.roll(mid, C_HEADS, axis=1) + pltpu.roll(lo, 2 * C_HEADS, axis=1)
    return _dot(packed.astype(bf16), sel3)


SSD_SUB = 4


def _ssd_prompt_kernel(x_ref, g_ref, w_ref, wdt_ref, cw_ref, cb_ref, dtb_ref, alog_ref, dsk_ref, nw_ref, sel_ref,
                       yn_ref, hfin_ref, cout_ref, xn_ref, xpad_ref, ht_ref, y_ref):
    c = pl.program_id(1)
    n_zx = C_D_INNER + C_CONV_DIM

    @pl.when(c == 0)
    def _():
        xpad_ref[0, 0:CONV_PAD, :] = jnp.zeros((CONV_PAD, C_CONV_DIM), f32)
        ht_ref[...] = jnp.zeros(ht_ref.shape, f32)

    xn_ref[...] = _rms(x_ref[...], g_ref[...]).astype(bf16)
    for sub in range(SSD_SUB):
        rows = pl.ds(sub * C_CHUNK, C_CHUNK)
        xn = xn_ref[rows, :]
        zx = _dot(xn, w_ref[0, :, 0:n_zx])
        dtr = _dot(xn, wdt_ref[...])
        _ssd_chunk(zx[:, :C_D_INNER], zx[:, C_D_INNER:], dtr, cw_ref, cb_ref, dtb_ref, alog_ref, dsk_ref, nw_ref,
                   sel_ref, yn_ref.at[rows], cout_ref, xpad_ref.at[sub], xpad_ref.at[(sub + 1) % SSD_SUB], ht_ref,
                   y_ref.at[sub])

    @pl.when(c == pl.num_programs(1) - 1)
    def _():
        for t in range(C_D_INNER // LANES):
            hfin_ref[0, t * LANES:(t + 1) * LANES, :] = ht_ref[:, t * LANES:(t + 1) * LANES].T


def _ssd_chunk(z, xbc, dtr, cw_ref, cb_ref, dtb_ref, alog_ref, dsk_ref, nw_ref, sel_ref,
               yn_ref, cout_ref, xpad_ref, xpad_next_ref, ht_ref, y_ref):
    T = C_CHUNK
    xpad_ref[CONV_PAD:, :] = xbc
    xp = xpad_ref[...]
    cw = cw_ref[...]
    acc = cb_ref[...]
    for tap in range(C_D_CONV - 1):
        shifted = pltpu.roll(xp, C_D_CONV - 1 - tap, axis=0)[CONV_PAD:, :]
        acc = acc + shifted * cw[tap:tap + 1, :]
    acc = acc + xbc * cw[C_D_CONV - 1:C_D_CONV, :]
    xpad_next_ref[0:CONV_PAD, :] = xbc[T - CONV_PAD:, :]
    cout_ref[0] = xbc[T - (C_D_CONV - 1):, :]
    act = _silu(acc)
    xs = act[:, :C_D_INNER]
    bm = act[:, C_D_INNER:C_D_INNER + C_BC_DIM]
    cm = act[:, C_D_INNER + C_BC_DIM:]
    xb = xs.astype(bf16)

    dt = _softplus(dtr + dtb_ref[...])
    a_neg = -jnp.exp(alog_ref[...])
    row = lax.broadcasted_iota(jnp.int32, (T, T), 0)
    col = lax.broadcasted_iota(jnp.int32, (T, T), 1)
    causal = row >= col
    acs = _dot_exact_lhs01(causal.astype(f32), dt * a_neg)
    a2 = acs * LOG2E
    sel3 = sel_ref[...]
    e_exp = jnp.exp2(_expand_heads(a2, sel3))
    w_exp = _expand_heads(jnp.exp(acs[T - 1:T, :] - acs) * dt, sel3)
    cdec = e_exp[T - 1:T, :]
    b2_t = a2.T - jnp.log2(dt.T)
    xw = (xs * w_exp).astype(bf16)
    hb = ht_ref[...].astype(bf16)
    dsk = dsk_ref[...]
    lo_t = lax.broadcasted_iota(jnp.int32, (T, LANES), 1) < C_HEAD_DIM

    for g in range(C_GROUPS):
        ns = slice(g * C_D_STATE, (g + 1) * C_D_STATE)
        gs = slice(g * C_GROUP_W, (g + 1) * C_GROUP_W)
        b_g = bm[:, ns]
        c_g = cm[:, ns].astype(bf16)
        cb = _dot_nt(c_g, b_g.astype(bf16))
        yi = _dot(c_g, hb[:, gs])
        ht_ref[:, gs] = ht_ref[:, gs] * cdec[:, gs] + _dot(b_g.T.astype(bf16), xw[:, gs])
        for tt in range(C_GROUP_W // LANES):
            t = g * (C_GROUP_W // LANES) + tt
            lanes = slice(t * LANES, (t + 1) * LANES)
            xt = xb[:, lanes]
            res = []
            for half in range(2):
                h = 2 * t + half
                a_col = jnp.broadcast_to(a2[:, h:h + 1], (T, T))
                b_row = jnp.broadcast_to(b2_t[h:h + 1, :], (T, T))
                w = jnp.where(causal, cb * jnp.exp2(a_col - b_row), 0.0)
                res.append(_dot(w.astype(bf16), xt))
            y_intra = jnp.where(lo_t, res[0], res[1])
            y_ref[:, lanes] = y_intra + e_exp[:, lanes] * yi[:, tt * LANES:(tt + 1) * LANES] + dsk[:, lanes] * xs[:, lanes]

    yn_ref[...] = _gated_group_norm(y_ref[...], z, nw_ref[...]).astype(bf16)


def _head_select3():
    k = np.arange(LANES)[:, None]
    ch = np.arange(C_D_INNER)[None, :] // C_HEAD_DIM
    return jnp.asarray((k % C_HEADS == ch) & (k < 3 * C_HEADS), dtype=bf16)


def ssd_prompt(x, g, w_in, layer, w_dt, conv_w, conv_b, dt_bias, a_log, d_skip, norm_w):
    step = SSD_SUB * C_CHUNK
    nc = SEQ // step
    full = lambda *shape: pl.BlockSpec(shape, lambda b, c: (0,) * len(shape))
    once = lambda *shape: pl.BlockSpec(shape, lambda b, c: (0,) * len(shape), pipeline_mode=pl.Buffered(1))
    return pl.pallas_call(
        _ssd_prompt_kernel,
        grid=(BATCH, nc),
        in_specs=[
            pl.BlockSpec((step, D_MODEL), lambda b, c: (b * nc + c, 0)),
            full(1, D_MODEL),
            pl.BlockSpec((1,) + w_in.shape[1:], lambda b, c: (layer, 0, 0), pipeline_mode=pl.Buffered(1)),
            once(D_MODEL, LANES),
            full(C_D_CONV, C_CONV_DIM), full(1, C_CONV_DIM), full(1, LANES), full(1, LANES),
            full(1, C_D_INNER), full(1, C_D_INNER), full(LANES, C_D_INNER),
        ],
        out_specs=[
            pl.BlockSpec((step, C_D_INNER), lambda b, c: (b * nc + c, 0)),
            pl.BlockSpec((1, C_D_INNER, C_D_STATE), lambda b, c: (b, 0, 0)),
            pl.BlockSpec((1, C_D_CONV - 1, C_CONV_DIM), lambda b, c: (b, 0, 0)),
        ],
        out_shape=[
            jax.ShapeDtypeStruct((PROMPT_ROWS, C_D_INNER), bf16),
            jax.ShapeDtypeStruct((BATCH, C_D_INNER, C_D_STATE), f32),
            jax.ShapeDtypeStruct((BATCH, C_D_CONV - 1, C_CONV_DIM), f32),
        ],
        scratch_shapes=[
            pltpu.VMEM((step, D_MODEL), bf16),
            pltpu.VMEM((SSD_SUB, CONV_PAD + C_CHUNK, C_CONV_DIM), f32),
            pltpu.VMEM((C_D_STATE, C_D_INNER), f32),
            pltpu.VMEM((SSD_SUB, C_CHUNK, C_D_INNER), f32),
        ],
        compiler_params=_params("parallel", "arbitrary"),
        name="ssd_prompt",
    )(x, g, w_in, w_dt, conv_w, conv_b, dt_bias, a_log, d_skip, norm_w, _head_select3())


SSD_BB = 8
SSD_TP = SUBLANES
_N_PAIRS = DEC_SEQ * (DEC_SEQ + 1) // 2
_N_COEF = _N_PAIRS + 2 * DEC_SEQ


def _ssd_sample_kernel(zx_ref, dtr_ref, cs_ref, h0_ref, cw_ref, cb_ref, dtb_ref, alog_ref, dsk_ref, nw_ref,
                       sel_ref, yn_ref, hn_ref, cout_ref, c_scr, b_scr, xw_scr, yi_scr, cd_scr):
    L = DEC_SEQ
    cw = cw_ref[...]
    xp = [cs_ref[k] for k in range(C_D_CONV - 1)] + [zx_ref[t, :, C_D_INNER:] for t in range(L)]
    for k in range(C_D_CONV - 1):
        cout_ref[k] = xp[L + k]
    act = []
    for t in range(L):
        acc = cb_ref[...]
        for tap in range(C_D_CONV):
            acc = acc + xp[t + tap] * cw[tap:tap + 1, :]
        act.append(_silu(acc))
    xs = [a[:, :C_D_INNER] for a in act]
    bm = [a[:, C_D_INNER:C_D_INNER + C_BC_DIM] for a in act]
    cm = [a[:, C_D_INNER + C_BC_DIM:] for a in act]

    a_neg = -jnp.exp(alog_ref[...])
    dt = [_softplus(dtr_ref[t] + dtb_ref[...]) for t in range(L)]
    acs = []
    for t in range(L):
        acs.append(dt[t] * a_neg if t == 0 else acs[t - 1] + dt[t] * a_neg)

    lane_group = lax.broadcasted_iota(jnp.int32, (SSD_BB, LANES), 1) // C_REP
    coefs = []
    for t in range(L):
        for t2 in range(t + 1):
            cbh = jnp.zeros((SSD_BB, LANES), f32)
            for g in range(C_GROUPS):
                ns = slice(g * C_D_STATE, (g + 1) * C_D_STATE)
                cbg = jnp.sum(cm[t][:, ns] * bm[t2][:, ns], axis=-1, keepdims=True)
                cbh = jnp.where(lane_group == g, cbg, cbh)
            coefs.append(cbh * jnp.exp(acs[t] - acs[t2]) * dt[t2])
    for t in range(L):
        coefs.append(jnp.exp(acs[t]))
    for t in range(L):
        coefs.append(jnp.exp(acs[L - 1] - acs[t]) * dt[t])
    coef = jnp.concatenate(coefs, axis=0)
    cexp = _dot_exact_rhs01(coef, sel_ref[...])
    cexp = [cexp[k * SSD_BB:(k + 1) * SSD_BB, :] for k in range(_N_COEF)]
    w_intra = cexp[:_N_PAIRS]
    w_inter = cexp[_N_PAIRS:_N_PAIRS + L]
    w_state = cexp[_N_PAIRS + L:]

    cd = jnp.concatenate([jnp.exp(acs[L - 1]), jnp.zeros((LANES - SSD_BB, LANES), f32)], axis=0)
    cd_t = cd.T
    for s in range(SSD_BB):
        cd_scr[s] = jnp.broadcast_to(cd_t[0:C_HEADS, s:s + 1], (C_HEADS, C_D_STATE))

    zeros_tail = jnp.zeros((SSD_BB, SSD_TP - L, C_D_INNER), f32)
    c_scr[:, L:, :] = zeros_tail[:, :, :C_BC_DIM]
    b_scr[:, L:, :] = zeros_tail[:, :, :C_BC_DIM]
    xw_scr[:, L:, :] = zeros_tail
    for t in range(L):
        xw_t = xs[t] * w_state[t]
        for s in range(SSD_BB):
            c_scr[s, t:t + 1, :] = cm[t][s:s + 1, :]
            b_scr[s, t:t + 1, :] = bm[t][s:s + 1, :]
            xw_scr[s, t:t + 1, :] = xw_t[s:s + 1, :]

    for s in range(SSD_BB):
        for g in range(C_GROUPS):
            ns = slice(g * C_D_STATE, (g + 1) * C_D_STATE)
            gs = slice(g * C_GROUP_W, (g + 1) * C_GROUP_W)
            h0 = h0_ref[s, gs, :]
            yi = _dot_nt(c_scr[s, :, ns].astype(bf16), h0.astype(bf16))
            for t in range(L):
                yi_scr[t, s:s + 1, gs] = yi[t:t + 1, :]
            st = _dot_tn(xw_scr[s, :, gs].astype(bf16), b_scr[s, :, ns].astype(bf16))
            for r in range(C_REP):
                h = g * C_REP + r
                rs = slice(r * C_HEAD_DIM, (r + 1) * C_HEAD_DIM)
                scale = cd_scr[s, h:h + 1, :]
                hn_ref[s, h * C_HEAD_DIM:(h + 1) * C_HEAD_DIM, :] = h0[rs, :] * scale + st[rs, :]

    dsk = dsk_ref[...]
    nw = nw_ref[...]
    pair = 0
    for t in range(L):
        y = w_inter[t] * yi_scr[t] + dsk * xs[t]
        for t2 in range(t + 1):
            y = y + w_intra[pair] * xs[t2]
            pair += 1
        yn_ref[t] = _gated_group_norm(y, zx_ref[t, :, :C_D_INNER], nw).astype(bf16)


def ssd_sample(zx_t, dtr_t, conv_state_t, h0, conv_w, conv_b, dt_bias, a_log, d_skip, norm_w, sel):
    tmaj = lambda n, w: pl.BlockSpec((n, SSD_BB, w), lambda i: (0, i, 0))
    full = lambda *shape: pl.BlockSpec(shape, lambda i: (0,) * len(shape))
    return pl.pallas_call(
        _ssd_sample_kernel,
        grid=(DEC_BATCH // SSD_BB,),
        in_specs=[
            tmaj(DEC_SEQ, C_D_INNER + C_CONV_DIM), tmaj(DEC_SEQ, LANES), tmaj(C_D_CONV - 1, C_CONV_DIM),
            pl.BlockSpec((SSD_BB, C_D_INNER, C_D_STATE), lambda i: (i, 0, 0)),
            full(C_D_CONV, C_CONV_DIM), full(1, C_CONV_DIM), full(1, LANES), full(1, LANES),
            full(1, C_D_INNER), full(1, C_D_INNER), full(LANES, C_D_INNER),
        ],
        out_specs=[
            tmaj(DEC_SEQ, C_D_INNER),
            pl.BlockSpec((SSD_BB, C_D_INNER, C_D_STATE), lambda i: (i, 0, 0)),
            tmaj(C_D_CONV - 1, C_CONV_DIM),
        ],
        out_shape=[
            jax.ShapeDtypeStruct((DEC_SEQ, DEC_BATCH, C_D_INNER), bf16),
            jax.ShapeDtypeStruct((DEC_BATCH, C_D_INNER, C_D_STATE), f32),
            jax.ShapeDtypeStruct((C_D_CONV - 1, DEC_BATCH, C_CONV_DIM), f32),
        ],
        scratch_shapes=[
            pltpu.VMEM((SSD_BB, SSD_TP, C_BC_DIM), f32),
            pltpu.VMEM((SSD_BB, SSD_TP, C_BC_DIM), f32),
            pltpu.VMEM((SSD_BB, SSD_TP, C_D_INNER), f32),
            pltpu.VMEM((DEC_SEQ, SSD_BB, C_D_INNER), f32),
            pltpu.VMEM((SSD_BB, C_HEADS, C_D_STATE), f32),
        ],
        compiler_params=_params("parallel"),
        name="ssd_sample",
    )(zx_t, dtr_t, conv_state_t, h0, conv_w, conv_b, dt_bias, a_log, d_skip, norm_w, sel)


def _pad_lanes(v):
    return jnp.pad(v.astype(f32), (0, LANES - v.shape[0])).reshape(1, LANES)


def _mixer_a(xp, xs, g, j, w_in, norm_v, w_sp, b_sp, w_out):
    nv = norm_v.reshape(1, A_HALF)
    xs, v_s, w_in_b, w_out_b = mixer_a(xs, g, w_in, nv, w_sp[:, :DEC_SEQ, :DEC_SEQ].reshape(-1),
                                       b_sp[:, :DEC_SEQ].reshape(-1), w_out, j, tm=TM_SAMPLE, sample=True)
    xp = mixer_a(xp, g, w_in_b, nv, w_sp, b_sp.T, w_out_b, 0, tm=TM_PROMPT, sample=False)
    return xp, xs, jnp.swapaxes(v_s.reshape(DEC_SEQ, DEC_BATCH, A_HALF), 0, 1)


def _mixer_b(xp, xs, g, j, k_cache, v_cache, w_qkv, q_norm, k_norm, sinks, rel_bias):
    qn = q_norm.reshape(1, B_HEAD_DIM)
    kn = k_norm.reshape(1, B_HEAD_DIM)
    table = bias_table(rel_bias, sinks)
    n_qkv = B_Q_DIM + 2 * B_KV_DIM

    o_p, k_p, v_p = attn_prompt(xp, g, w_qkv, j, table, qn, kn)

    qkv_s = norm_matmul(xs, g, w_qkv, j, n_qkv, tm=TM_SAMPLE, tn=QKV_TN_SAMPLE).reshape(DEC_SEQ, DEC_BATCH, -1)
    q_s = qkv_s[:, :, :B_Q_DIM].reshape(DEC_SEQ, DEC_BATCH, B_KV_HEADS, B_REP, B_HEAD_DIM)
    q_s = q_s.transpose(1, 3, 0, 2, 4).reshape(DEC_BATCH, ATT_QR, B_KV_DIM)
    k_new = jnp.swapaxes(qkv_s[:, :, B_Q_DIM:B_Q_DIM + B_KV_DIM], 0, 1)
    v_new = jnp.swapaxes(qkv_s[:, :, B_Q_DIM + B_KV_DIM:], 0, 1)
    table_s = table[:, :DEC_SEQ, :ATT_KEYS].reshape(B_KV_HEADS, ATT_QR, ATT_KEYS)
    o_s, k_s, v_s = attn_sample(q_s, k_new, v_new,
                                k_cache.reshape(DEC_BATCH, B_WINDOW, B_KV_DIM),
                                v_cache.reshape(DEC_BATCH, B_WINDOW, B_KV_DIM),
                                table_s, qn, kn, sinks)
    o_s = o_s.reshape(DEC_BATCH, B_REP, DEC_SEQ, B_KV_HEADS, B_HEAD_DIM).transpose(2, 0, 3, 1, 4)
    o_s = o_s.reshape(SAMPLE_ROWS, B_Q_DIM).astype(bf16)
    kv_shape_p = (BATCH, B_WINDOW, B_KV_HEADS, B_HEAD_DIM)
    kv_shape_s = (DEC_BATCH, B_WINDOW, B_KV_HEADS, B_HEAD_DIM)
    return o_p, o_s, k_p.reshape(kv_shape_p), v_p.reshape(kv_shape_p), k_s.reshape(kv_shape_s), v_s.reshape(kv_shape_s)


def _mixer_c(xp, xs, g, j, h0, conv_state, w_in, conv_w, conv_b, dt_bias, a_log, d_skip, norm_w):
    n_zx = C_D_INNER + C_CONV_DIM
    w_dt = jnp.pad(w_in[j, :, n_zx:], ((0, 0), (0, LANES - C_HEADS)))
    cb = conv_b.reshape(1, C_CONV_DIM)
    dtb = _pad_lanes(dt_bias)
    alog = _pad_lanes(a_log)
    dsk = jnp.repeat(d_skip.astype(f32), C_HEAD_DIM).reshape(1, C_D_INNER)
    nw = norm_w.reshape(1, C_D_INNER)

    yn_p, h_p, conv_p = ssd_prompt(xp, g, w_in, j, w_dt, conv_w, cb, dtb, alog, dsk, nw)
    zx_s, dtr_s = norm_matmul(xs, g, w_in, j, n_zx, tm=TM_SAMPLE, tn=ZX_TN_SAMPLE, w_tail=w_dt)
    sel = (jnp.arange(LANES)[:, None] == jnp.arange(C_D_INNER)[None, :] // C_HEAD_DIM).astype(f32)
    yn_s, h_s, conv_s = ssd_sample(
        zx_s.reshape(DEC_SEQ, DEC_BATCH, -1), dtr_s.reshape(DEC_SEQ, DEC_BATCH, LANES),
        jnp.swapaxes(conv_state, 0, 1), h0.reshape(DEC_BATCH, C_D_INNER, C_D_STATE),
        conv_w, cb, dtb, alog, dsk, nw, sel)
    st_shape = (C_HEADS, C_HEAD_DIM, C_D_STATE)
    return (yn_p, yn_s.reshape(SAMPLE_ROWS, C_D_INNER), h_p.reshape((BATCH,) + st_shape), conv_p,
            h_s.reshape((DEC_BATCH,) + st_shape), jnp.swapaxes(conv_s, 0, 1))


def kernel(x_prompt, x_sample, cache_swa_k, cache_swa_v, state_ssm, state_conv, norm_mixer, norm_mlp, mlp_w_up, mlp_w_down, a_w_in, a_norm_v, a_w_spatial, a_b_spatial, a_w_out, b_w_qkv, b_q_norm, b_k_norm, b_sinks, rel_bias, b_w_out, c_w_in, c_conv_w, c_conv_b, c_dt_bias, c_a_log, c_d, c_norm, c_w_out):
    xp = x_prompt.reshape(PROMPT_ROWS, D_MODEL)
    xs = jnp.swapaxes(x_sample, 0, 1).reshape(SAMPLE_ROWS, D_MODEL)
    chunk_v_s = []
    swa_kp, swa_vp, swa_ks, swa_vs = [], [], [], []
    ssm_p, conv_p, ssm_s, conv_s = [], [], [], []
    b_w_qkv, b_w_out = b_w_qkv.astype(bf16), b_w_out.astype(bf16)
    c_w_in, c_w_out = c_w_in.astype(bf16), c_w_out.astype(bf16)
    for i in range(DEPTH):
        kind = i % N_MIXERS
        j = i // N_MIXERS
        g = norm_mixer[i].reshape(1, D_MODEL)
        proj_p = proj_s = None
        if kind == 0:
            xp, xs, v_new = _mixer_a(xp, xs, g, j, a_w_in, a_norm_v[j], a_w_spatial[j], a_b_spatial[j], a_w_out)
            chunk_v_s.append(v_new)
        elif kind == 1:
            o_p, o_s, kp, vp, ks_, vs_ = _mixer_b(xp, xs, g, j, cache_swa_k[j], cache_swa_v[j], b_w_qkv, b_q_norm[j],
                                                  b_k_norm[j], b_sinks[j], rel_bias)
            proj_p, proj_s = (o_p, b_w_out, j), (o_s, b_w_out, j)
            swa_kp.append(kp); swa_vp.append(vp); swa_ks.append(ks_); swa_vs.append(vs_)
        else:
            y_p, y_s, hp, bp, hs, bs = _mixer_c(xp, xs, g, j, state_ssm[j], state_conv[j], c_w_in, c_conv_w[j],
                                                c_conv_b[j], c_dt_bias[j], c_a_log[j], c_d[j], c_norm[j])
            proj_p, proj_s = (y_p, c_w_out, j), (y_s, c_w_out, j)
            ssm_p.append(hp); conv_p.append(bp); ssm_s.append(hs); conv_s.append(bs)
        gm = norm_mlp[i].reshape(1, D_MODEL)
        xs, w_up_b, w_down_b = mlp(xs, gm, mlp_w_up, mlp_w_down, i, tm=TM_SAMPLE, tf=MLP_TF, proj=proj_s, emit=True)
        wide = proj_p is None or proj_p[0].shape[1] <= D_MODEL
        xp = mlp(xp, gm, w_up_b, w_down_b, 0, tm=TM_PROMPT, tf=2 * MLP_TF if wide else MLP_TF, proj=proj_p)
    y_prompt = xp.reshape(BATCH, SEQ, D_MODEL)
    y_sample = jnp.swapaxes(xs.reshape(DEC_SEQ, DEC_BATCH, D_MODEL), 0, 1)
    return (y_prompt, y_sample, jnp.stack(chunk_v_s),
            jnp.stack(swa_kp), jnp.stack(swa_vp), jnp.stack(swa_ks), jnp.stack(swa_vs),
            jnp.stack(ssm_p), jnp.stack(conv_p), jnp.stack(ssm_s), jnp.stack(conv_s))
```

```python
import functools
import math

import jax
import jax.numpy as jnp
import numpy as np
from jax import lax
from jax.experimental import pallas as pl
from jax.experimental.pallas import tpu as pltpu

f32 = jnp.float32
bf16 = jnp.bfloat16

D_MODEL = 1024
BATCH = 4
SEQ = 4096
DEPTH = 4
DEC_BATCH = 128
DEC_SEQ = 4
PAST_LEN = 8192
N_MIXERS = 3
D_FF = 4 * D_MODEL
EPS = 1e-6
NEG_INF = -1e30

A_CHUNK = 128
A_D_FFN = 6 * D_MODEL
A_HALF = A_D_FFN // 2
A_GROUPS = 8
A_GROUP_W = A_HALF // A_GROUPS

B_HEADS = 16
B_KV_HEADS = 4
B_HEAD_DIM = 64
B_REP = B_HEADS // B_KV_HEADS
B_WINDOW = 128
B_BLOCK = 128
B_Q_DIM = B_HEADS * B_HEAD_DIM
B_KV_DIM = B_KV_HEADS * B_HEAD_DIM
N_BUCKETS = 32
MAX_DISTANCE = 128

C_D_INNER = 2 * D_MODEL
C_HEAD_DIM = 64
C_HEADS = C_D_INNER // C_HEAD_DIM
C_GROUPS = 4
C_REP = C_HEADS // C_GROUPS
C_D_STATE = 128
C_D_CONV = 4
C_BC_DIM = C_GROUPS * C_D_STATE
C_CONV_DIM = C_D_INNER + 2 * C_BC_DIM
C_GROUP_W = C_D_INNER // C_GROUPS
C_CHUNK = 128

LANES = 128
SUBLANES = 8
VMEM_LIMIT_BYTES = 56 * 1024 * 1024

PROMPT_ROWS = BATCH * SEQ
SAMPLE_ROWS = DEC_BATCH * DEC_SEQ
TM_PROMPT = 1024
TM_SAMPLE = SAMPLE_ROWS
MLP_TF = 1024
QKV_TN_SAMPLE = 512
ZX_TN_SAMPLE = 1024


def _params(*sem):
    return pltpu.CompilerParams(dimension_semantics=sem, vmem_limit_bytes=VMEM_LIMIT_BYTES)


def _rms(x, g):
    ms = jnp.mean(x * x, axis=-1, keepdims=True)
    return x * lax.rsqrt(ms + EPS) * g


def _gelu(x):
    return 0.5 * x * (1.0 + lax.erf(x * math.sqrt(0.5)))


LOG2E = math.log2(math.e)


def _silu(x):
    return x * jax.nn.sigmoid(x)


def _softplus(x):
    return jnp.maximum(x, 0.0) + jnp.log1p(jnp.exp(-jnp.abs(x)))


def _dot(a, b):
    return jnp.dot(a, b, preferred_element_type=f32)


def _dot_nt(a, b):
    return lax.dot_general(a, b, (((1,), (1,)), ((), ())), preferred_element_type=f32)


def _dot_tn(a, b):
    return lax.dot_general(a, b, (((0,), (0,)), ((), ())), preferred_element_type=f32)


def _dot_exact_lhs01(a01, x):
    a = a01.astype(bf16)
    hi = x.astype(bf16)
    r1 = x - hi.astype(f32)
    mid = r1.astype(bf16)
    lo = (r1 - mid.astype(f32)).astype(bf16)
    return _dot(a, hi) + _dot(a, mid) + _dot(a, lo)


def _dot_exact_rhs01(x, b01):
    b = b01.astype(bf16)
    hi = x.astype(bf16)
    r1 = x - hi.astype(f32)
    mid = r1.astype(bf16)
    lo = (r1 - mid.astype(f32)).astype(bf16)
    return _dot(hi, b) + _dot(mid, b) + _dot(lo, b)


def _norm_matmul_kernel(*refs, nj, tail):
    if tail:
        x_ref, g_ref, w_ref, wt_ref, o_ref, ot_ref, xn_ref = refs
    else:
        x_ref, g_ref, w_ref, o_ref, xn_ref = refs
    j = pl.program_id(1)

    @pl.when(j == 0)
    def _():
        xn_ref[...] = _rms(x_ref[...], g_ref[...]).astype(bf16)

    @pl.when(j < nj)
    def _():
        o_ref[...] = _dot(xn_ref[...], w_ref[0].astype(bf16))

    if tail:
        @pl.when(j == nj)
        def _():
            ot_ref[...] = _dot(xn_ref[...], wt_ref[...].astype(bf16))


def norm_matmul(x, g, w, layer, n, *, tm, tn, w_tail=None):
    m, k = x.shape
    nj = n // tn
    tail = w_tail is not None
    last = nj - 1
    in_specs = [
        pl.BlockSpec((tm, k), lambda i, j: (i, 0)),
        pl.BlockSpec((1, k), lambda i, j: (0, 0)),
        pl.BlockSpec((1, k, tn), lambda i, j: (layer, 0, jnp.minimum(j, last))),
    ]
    out_specs = [pl.BlockSpec((tm, tn), lambda i, j: (i, jnp.minimum(j, last)))]
    out_shape = [jax.ShapeDtypeStruct((m, n), f32)]
    args = [x, g, w]
    if tail:
        in_specs.append(pl.BlockSpec((k, LANES), lambda i, j: (0, 0)))
        out_specs.append(pl.BlockSpec((tm, LANES), lambda i, j: (i, 0)))
        out_shape.append(jax.ShapeDtypeStruct((m, LANES), f32))
        args.append(w_tail)
    out = pl.pallas_call(
        functools.partial(_norm_matmul_kernel, nj=nj, tail=tail),
        grid=(m // tm, nj + (1 if tail else 0)),
        in_specs=in_specs,
        out_specs=out_specs,
        out_shape=out_shape,
        scratch_shapes=[pltpu.VMEM((tm, k), bf16)],
        compiler_params=_params("parallel", "arbitrary"),
        name="norm_matmul",
    )(*args)
    return out if tail else out[0]


def _mlp_kernel(*refs, proj, emit):
    refs = list(refs)
    a_ref, wo_ref = (refs.pop(0), refs.pop(0)) if proj else (None, None)
    x_ref, g_ref, wu_ref, wd_ref, o_ref = refs[:5]
    wub_ref, wdb_ref = (refs[5], refs[6]) if emit else (None, None)
    xn_ref = refs[-1]

    @pl.when(pl.program_id(1) == 0)
    def _():
        x = x_ref[...]
        if proj:
            x = x + _dot(a_ref[...], wo_ref[0])
        xn_ref[...] = _rms(x, g_ref[...]).astype(bf16)
        o_ref[...] = x

    wu = wu_ref[0].astype(bf16)
    wd = wd_ref[0].astype(bf16)
    if emit:
        wub_ref[0] = wu
        wdb_ref[0] = wd
    h = jnp.maximum(_dot(xn_ref[...], wu), 0.0)
    o_ref[...] += _dot((h * h).astype(bf16), wd)


def mlp(x, g, w_up, w_down, layer, *, tm, tf, proj=None, emit=False):
    m, d = x.shape
    ff = w_up.shape[2]
    assert not emit or m == tm
    in_specs = [
        pl.BlockSpec((tm, d), lambda i, j: (i, 0)),
        pl.BlockSpec((1, d), lambda i, j: (0, 0)),
        pl.BlockSpec((1, d, tf), lambda i, j: (layer, 0, j)),
        pl.BlockSpec((1, tf, d), lambda i, j: (layer, j, 0)),
    ]
    args = [x, g, w_up, w_down]
    if proj is not None:
        a, w_o, lo = proj
        k = a.shape[1]
        in_specs = [pl.BlockSpec((tm, k), lambda i, j: (i, 0)),
                    pl.BlockSpec((1, k, d), lambda i, j: (lo, 0, 0), pipeline_mode=pl.Buffered(1))] + in_specs
        args = [a, w_o] + args
    out_specs = [pl.BlockSpec((tm, d), lambda i, j: (i, 0))]
    out_shape = [jax.ShapeDtypeStruct((m, d), f32)]
    if emit:
        out_specs += [pl.BlockSpec((1, d, tf), lambda i, j: (0, 0, j)), pl.BlockSpec((1, tf, d), lambda i, j: (0, j, 0))]
        out_shape += [jax.ShapeDtypeStruct((1, d, ff), bf16), jax.ShapeDtypeStruct((1, ff, d), bf16)]
    out = pl.pallas_call(
        functools.partial(_mlp_kernel, proj=proj is not None, emit=emit),
        grid=(m // tm, ff // tf),
        in_specs=in_specs,
        out_specs=out_specs,
        out_shape=out_shape,
        scratch_shapes=[pltpu.VMEM((tm, d), bf16)],
        compiler_params=_params("parallel", "arbitrary"),
        name="mlp",
    )(*args)
    return out if emit else out[0]


A_BLK_GROUPS = 2
A_BLK = A_BLK_GROUPS * A_GROUP_W
A_NBLK = A_HALF // A_BLK


def _mixer_a_kernel(*refs, sample, v_steps):
    if sample:
        (ws_ref, bs_ref, x_ref, g_ref, win_ref, nv_ref, wout_ref, o_ref, vo_ref, winb_ref, woutb_ref,
         xn_ref, v_ref, ssq_ref, us_ref) = refs
        winv_ref = win_ref
    else:
        (x_ref, g_ref, winv_ref, win_ref, nv_ref, ws_ref, bs_ref, wout_ref, o_ref,
         xn_ref, v_ref, ssq_ref, us_ref) = refs
        winb_ref = woutb_ref = None
    j = pl.program_id(1)
    tm = x_ref.shape[0]
    v_blk = A_HALF // v_steps

    def w_in_block(ref=None):
        w = (win_ref if ref is None else ref)[0].astype(bf16)
        if winb_ref is not None:
            winb_ref[0] = w
        return w

    def w_out_block():
        w = wout_ref[0].astype(bf16)
        if woutb_ref is not None:
            woutb_ref[0] = w
        return w

    @pl.when(j == 0)
    def _():
        xn_ref[...] = _rms(x_ref[...], g_ref[...]).astype(bf16)
        ssq_ref[...] = jnp.zeros(ssq_ref.shape, f32)

    for k in range(v_steps):
        @pl.when(j == k)
        def _(k=k):
            v = _gelu(_dot(xn_ref[...], w_in_block(winv_ref)))
            v_ref[:, k * v_blk:(k + 1) * v_blk] = v
            ssq_ref[...] += jnp.sum(v * v, axis=-1, keepdims=True)

    if not sample:
        row = lax.broadcasted_iota(jnp.int32, (A_CHUNK, A_CHUNK), 0)
        col = lax.broadcasted_iota(jnp.int32, (A_CHUNK, A_CHUNK), 1)
        causal = row >= col

    for k in range(A_NBLK):
        @pl.when(j == v_steps + k)
        def _(k=k):
            u = _gelu(_dot(xn_ref[...], w_in_block()))
            rinv = lax.rsqrt(ssq_ref[...] * (1.0 / A_HALF) + EPS)
            for gg in range(A_BLK_GROUPS):
                g = k * A_BLK_GROUPS + gg
                cols = slice(g * A_GROUP_W, (g + 1) * A_GROUP_W)
                ucols = slice(gg * A_GROUP_W, (gg + 1) * A_GROUP_W)
                vn = v_ref[:, cols] * rinv * nv_ref[:, cols]
                if sample:
                    vo_ref[:, cols] = vn
                    vt = [vn[t * DEC_BATCH:(t + 1) * DEC_BATCH] for t in range(DEC_SEQ)]
                    s_rows = []
                    for t in range(DEC_SEQ):
                        s = ws_ref[(g * DEC_SEQ + t) * DEC_SEQ] * vt[0]
                        for t2 in range(1, t + 1):
                            s = s + ws_ref[(g * DEC_SEQ + t) * DEC_SEQ + t2] * vt[t2]
                        s_rows.append(s + bs_ref[g * DEC_SEQ + t])
                    s = jnp.concatenate(s_rows, axis=0)
                else:
                    w = jnp.where(causal, ws_ref[g], 0.0).astype(bf16)
                    bias = bs_ref[:, g:g + 1]
                    vb = vn.astype(bf16)
                    s = jnp.concatenate(
                        [_dot(w, vb[c * A_CHUNK:(c + 1) * A_CHUNK]) + bias for c in range(tm // A_CHUNK)], axis=0)
                us_ref[:, ucols] = (u[:, ucols] * s).astype(bf16)
            y = _dot(us_ref[...], w_out_block())
            if k == 0:
                o_ref[...] = x_ref[...] + y
            else:
                o_ref[...] += y


def mixer_a(x, g, w_in, norm_v, w_sp, b_sp, w_out, layer, *, tm, sample):
    m, d = x.shape
    v_steps = A_NBLK if sample else A_NBLK // 2
    v_blk = A_HALF // v_steps
    nj = v_steps + A_NBLK
    row = lambda w: pl.BlockSpec((tm, w), lambda i, j: (i, 0))
    full = lambda *shape: pl.BlockSpec(shape, lambda i, j: (0,) * len(shape))
    smem = pl.BlockSpec(memory_space=pltpu.SMEM)
    win_spec = pl.BlockSpec((1, d, A_BLK), lambda i, j: (layer, 0, (j + A_NBLK) % nj))
    winv_spec = pl.BlockSpec((1, d, v_blk), lambda i, j: (layer, 0, v_steps + jnp.minimum(j, v_steps - 1)))
    winu_spec = pl.BlockSpec((1, d, A_BLK), lambda i, j: (layer, 0, jnp.maximum(j - v_steps, 0)))
    wout_spec = pl.BlockSpec((1, A_BLK, d), lambda i, j: (layer, jnp.maximum(j - v_steps, 0), 0))
    if sample:
        in_specs = [smem, smem, row(d), full(1, d), win_spec, full(1, A_HALF), wout_spec]
        args = (w_sp, b_sp, x, g, w_in, norm_v, w_out)
        out_specs = [row(d), row(A_HALF),
                     pl.BlockSpec((1, d, A_BLK), lambda i, j: (0, 0, (j + A_NBLK) % nj)),
                     pl.BlockSpec((1, A_BLK, d), lambda i, j: (0, jnp.maximum(j - A_NBLK, 0), 0))]
        out_shape = [jax.ShapeDtypeStruct((m, d), f32), jax.ShapeDtypeStruct((m, A_HALF), f32),
                     jax.ShapeDtypeStruct((1, d, 2 * A_HALF), bf16), jax.ShapeDtypeStruct((1, A_HALF, d), bf16)]
    else:
        in_specs = [row(d), full(1, d), winv_spec, winu_spec, full(1, A_HALF), full(A_GROUPS, A_CHUNK, A_CHUNK),
                    full(A_CHUNK, A_GROUPS), wout_spec]
        args = (x, g, w_in, w_in, norm_v, w_sp, b_sp, w_out)
        out_specs = row(d)
        out_shape = jax.ShapeDtypeStruct((m, d), f32)
    return pl.pallas_call(
        functools.partial(_mixer_a_kernel, sample=sample, v_steps=v_steps),
        grid=(m // tm, nj),
        in_specs=in_specs,
        out_specs=out_specs,
        out_shape=out_shape,
        scratch_shapes=[pltpu.VMEM((tm, d), bf16), pltpu.VMEM((tm, A_HALF), f32), pltpu.VMEM((tm, 1), f32),
                        pltpu.VMEM((tm, A_BLK), bf16)],
        compiler_params=_params("parallel", "arbitrary"),
        name="mixer_a_sample" if sample else "mixer_a_prompt",
    )(*args)


def _bucket_table():
    i = np.arange(B_BLOCK)[:, None]
    j = np.arange(2 * B_BLOCK)[None, :]
    n = np.maximum(B_BLOCK + i - j, 0)
    max_exact = N_BUCKETS // 2
    nf = np.maximum(n, 1).astype(np.float64)
    val = np.log(nf / max_exact) / math.log(MAX_DISTANCE / max_exact) * (N_BUCKETS - max_exact)
    in_window = (n >= max_exact) & (n < B_WINDOW)
    assert np.all(np.abs(val - np.round(val))[in_window & (n != max_exact)] > 1e-3)
    large = np.minimum(max_exact + np.floor(val + 1e-9).astype(np.int64), N_BUCKETS - 1)
    return np.where(n < max_exact, n, large).astype(np.int32)


def _bias_table_kernel(rb_ref, sink_ref, bk_ref, o_ref):
    bk = bk_ref[...]
    i = lax.broadcasted_iota(jnp.int32, bk.shape, 0)
    j = lax.broadcasted_iota(jnp.int32, bk.shape, 1)
    in_window = (j > i) & (j <= i + B_WINDOW)
    for h in range(B_HEADS):
        acc = jnp.zeros(bk.shape, f32)
        for b in range(N_BUCKETS):
            acc = jnp.where(bk == b, rb_ref[b * B_HEADS + h], acc)
        acc = jnp.where(in_window, acc, NEG_INF)
        o_ref[h] = jnp.where(j == 0, sink_ref[h], acc)


def bias_table(rel_bias, sinks):
    smem = pl.BlockSpec(memory_space=pltpu.SMEM)
    return pl.pallas_call(
        _bias_table_kernel,
        in_specs=[smem, smem, pl.BlockSpec(memory_space=pltpu.VMEM)],
        out_specs=pl.BlockSpec(memory_space=pltpu.VMEM),
        out_shape=jax.ShapeDtypeStruct((B_HEADS, B_BLOCK, 2 * B_BLOCK), f32),
        name="bias_table",
    )(rel_bias.reshape(-1), sinks, jnp.asarray(_bucket_table()))


def _softmax_with_sink(logits, sink):
    m = jnp.maximum(jnp.max(logits, axis=-1, keepdims=True), sink)
    p = jnp.exp(logits - m)
    return p, jnp.sum(p, axis=-1, keepdims=True) + jnp.exp(sink - m)


def _rms_head_pairs(x, g2, lo):
    sq = x * x
    s_lo = jnp.sum(jnp.where(lo, sq, 0.0), axis=-1, keepdims=True)
    s_hi = jnp.sum(jnp.where(lo, 0.0, sq), axis=-1, keepdims=True)
    r = lax.rsqrt(jnp.where(lo, s_lo, s_hi) * (1.0 / B_HEAD_DIM) + EPS)
    return x * r * g2


ATT_SUB = 8


def _attn_prompt_kernel(x_ref, g_ref, w_ref, wo_ref, tab_ref, qn_ref, kn_ref, out_ref, ko_ref, vo_ref,
                        xn_ref, kband_ref, vband_ref, q_ref, p_ref, rhs_ref, o_ref):
    n = pl.program_id(1)
    T = B_BLOCK

    @pl.when(n == 0)
    def _():
        kband_ref[0:T, :] = jnp.zeros((T, B_KV_DIM), f32)
        vband_ref[0:T, :] = jnp.zeros((T, B_KV_DIM), f32)
        rhs_ref[...] = jnp.ones(rhs_ref.shape, bf16)

    xn_ref[...] = _rms(x_ref[...], g_ref[...]).astype(bf16)
    for sub in range(ATT_SUB):
        rows = pl.ds(sub * T, T)
        qkv = _dot(xn_ref[rows, :], w_ref[0])
        first_key = jnp.where(n == 0, T, 0) if sub == 0 else 0
        _attn_block(first_key, qkv, tab_ref, qn_ref, kn_ref, o_ref.at[sub], ko_ref, vo_ref,
                    kband_ref, vband_ref, q_ref.at[sub], p_ref.at[sub], rhs_ref.at[sub])
        out_ref[rows, :] = x_ref[rows, :] + _dot(o_ref[sub], wo_ref[0])


def _attn_block(first_key, qkv_ref, tab_ref, qn_ref, kn_ref, o_ref, ko_ref, vo_ref,
                kband_ref, vband_ref, q_ref, p_ref, rhs_ref):
    T = B_BLOCK
    lo = lax.broadcasted_iota(jnp.int32, (1, LANES), 1) < B_HEAD_DIM
    qn2 = qn_ref[...]
    kn2 = kn_ref[...]
    for t in range(B_KV_DIM // LANES):
        lanes = slice(t * LANES, (t + 1) * LANES)
        k2 = _rms_head_pairs(qkv_ref[:, B_Q_DIM + t * LANES:B_Q_DIM + (t + 1) * LANES].astype(f32), kn2, lo)
        ko_ref[0, :, lanes] = k2
        kband_ref[T:, lanes] = k2
    v = qkv_ref[:, B_Q_DIM + B_KV_DIM:].astype(f32)
    vo_ref[0] = v
    vband_ref[T:, :] = v
    for t in range(B_Q_DIM // LANES):
        q2 = qkv_ref[:, t * LANES:(t + 1) * LANES].astype(f32)
        q_ref[t * T:(t + 1) * T, :] = _rms_head_pairs(q2, qn2, lo).astype(bf16)

    key_ops = {}
    band_row = lax.broadcasted_iota(jnp.int32, (2 * T, LANES), 0)
    for t in range(B_KV_DIM // LANES):
        lanes = slice(t * LANES, (t + 1) * LANES)
        kt = jnp.where(band_row == 0, 0.0, kband_ref[:, lanes])
        kr = pltpu.roll(kt, B_HEAD_DIM, axis=1)
        vt = jnp.where(band_row == 0, 0.0, vband_ref[:, lanes])
        vr = pltpu.roll(vt, B_HEAD_DIM, axis=1)
        hi = jnp.logical_not(lo)
        for half, (ksrc, vsrc) in enumerate(((kt, vt), (kr, vr))):
            g_lo, g_hi = (2 * t, 2 * t + 1) if half == 0 else (2 * t + 1, 2 * t)
            key_ops[(g_lo, 0)] = jnp.where(lo, ksrc, 0.0).astype(bf16)
            key_ops[(g_hi, 1)] = jnp.where(hi, ksrc, 0.0).astype(bf16)
            rhs_ref[g_lo * 2 + 0, :, 0:LANES] = jnp.where(lo, vsrc, 1.0).astype(bf16)
            rhs_ref[g_hi * 2 + 1, :, 0:LANES] = jnp.where(hi, vsrc, 1.0).astype(bf16)

    masked_first = not (isinstance(first_key, int) and first_key == 0)
    if masked_first:
        j = lax.broadcasted_iota(jnp.int32, (T, 2 * T), 1)
        keep = (j >= first_key) | (j == 0)
    lo_t = lax.broadcasted_iota(jnp.int32, (T, LANES), 1) < B_HEAD_DIM
    for g in range(B_KV_HEADS):
        q2 = q_ref[2 * g * T:(2 * g + 2) * T, :]
        res = []
        for half in range(2):
            logits = _dot_nt(q2, key_ops[(g, half)])
            for pair in range(2):
                h = g * B_REP + 2 * pair + half
                l = logits[pair * T:(pair + 1) * T] + tab_ref[h]
                if masked_first:
                    l = jnp.where(keep, l, NEG_INF)
                p = jnp.exp(l - jnp.max(l, axis=-1, keepdims=True))
                p_ref[g * 2 + half, pair * T:(pair + 1) * T, :] = p.astype(bf16)
            res.append(_dot(p_ref[g * 2 + half], rhs_ref[g * 2 + half]))
        for pair in range(2):
            rows = slice(pair * T, (pair + 1) * T)
            even = res[0][rows, 0:LANES] / res[0][rows, LANES:]
            odd = res[1][rows, 0:LANES] / res[1][rows, LANES:]
            t = 2 * g + pair
            o_ref[:, t * LANES:(t + 1) * LANES] = jnp.where(lo_t, even, odd).astype(bf16)
    kband_ref[0:T, :] = kband_ref[T:, :]
    vband_ref[0:T, :] = vband_ref[T:, :]


def attn_prompt(x, g, w_qkv, w_out, layer, table, q_norm, k_norm):
    step = ATT_SUB * B_BLOCK
    nb = SEQ // step
    return pl.pallas_call(
        _attn_prompt_kernel,
        grid=(BATCH, nb),
        in_specs=[
            pl.BlockSpec((step, D_MODEL), lambda b, n: (b * nb + n, 0)),
            pl.BlockSpec((1, D_MODEL), lambda b, n: (0, 0)),
            pl.BlockSpec((1,) + w_qkv.shape[1:], lambda b, n: (layer, 0, 0), pipeline_mode=pl.Buffered(1)),
            pl.BlockSpec((1,) + w_out.shape[1:], lambda b, n: (layer, 0, 0), pipeline_mode=pl.Buffered(1)),
            pl.BlockSpec((B_HEADS, B_BLOCK, 2 * B_BLOCK), lambda b, n: (0, 0, 0)),
            pl.BlockSpec((1, LANES), lambda b, n: (0, 0)),
            pl.BlockSpec((1, LANES), lambda b, n: (0, 0)),
        ],
        out_specs=[
            pl.BlockSpec((step, D_MODEL), lambda b, n: (b * nb + n, 0)),
            pl.BlockSpec((1, B_BLOCK, B_KV_DIM), lambda b, n: (b, 0, 0)),
            pl.BlockSpec((1, B_BLOCK, B_KV_DIM), lambda b, n: (b, 0, 0)),
        ],
        out_shape=[
            jax.ShapeDtypeStruct((PROMPT_ROWS, D_MODEL), f32),
            jax.ShapeDtypeStruct((BATCH, B_BLOCK, B_KV_DIM), f32),
            jax.ShapeDtypeStruct((BATCH, B_BLOCK, B_KV_DIM), f32),
        ],
        scratch_shapes=[
            pltpu.VMEM((step, D_MODEL), bf16),
            pltpu.VMEM((2 * B_BLOCK, B_KV_DIM), f32),
            pltpu.VMEM((2 * B_BLOCK, B_KV_DIM), f32),
            pltpu.VMEM((ATT_SUB, B_Q_DIM // LANES * B_BLOCK, LANES), bf16),
            pltpu.VMEM((ATT_SUB, 2 * B_KV_HEADS, 2 * B_BLOCK, 2 * B_BLOCK), bf16),
            pltpu.VMEM((ATT_SUB, 2 * B_KV_HEADS, 2 * B_BLOCK, 2 * LANES), bf16),
            pltpu.VMEM((ATT_SUB, B_BLOCK, B_Q_DIM), bf16),
        ],
        compiler_params=_params("parallel", "arbitrary"),
        name="attn_prompt",
    )(x, g, w_qkv, w_out, table, jnp.tile(q_norm * (B_HEAD_DIM ** -0.5), (1, LANES // B_HEAD_DIM)),
      jnp.tile(k_norm, (1, LANES // B_HEAD_DIM)))


ATT_BB = 8
ATT_QR = B_REP * DEC_SEQ
ATT_KEYS = B_WINDOW + 2 * DEC_SEQ


def _attn_sample_kernel(sink_ref, q_ref, kn_ref, vn_ref, kc_ref, vc_ref, tab_ref, qn_ref, knm_ref,
                        o_ref, ko_ref, vo_ref, kall_ref, vall_ref):
    qn2 = qn_ref[...]
    kn2 = knm_ref[...]
    n_rows = B_KV_HEADS * ATT_QR
    lo = lax.broadcasted_iota(jnp.int32, (1, LANES), 1) < B_HEAD_DIM
    row = lax.broadcasted_iota(jnp.int32, (n_rows, ATT_KEYS), 0)
    j = lax.broadcasted_iota(jnp.int32, (n_rows, ATT_KEYS), 1)
    t = row % DEC_SEQ
    valid = (j > t) & (j <= t + B_WINDOW)
    q_row_group = lax.broadcasted_iota(jnp.int32, (n_rows, B_KV_DIM), 0) // ATT_QR
    q_lane_group = lax.broadcasted_iota(jnp.int32, (n_rows, B_KV_DIM), 1) // B_HEAD_DIM
    own_group = q_row_group == q_lane_group
    o_lane_group = lax.broadcasted_iota(jnp.int32, (ATT_QR, B_KV_DIM), 1) // B_HEAD_DIM
    pad = jnp.zeros((ATT_KEYS - B_WINDOW - DEC_SEQ, B_KV_DIM), f32)
    bias = tab_ref[...]
    sink = sink_ref[...]
    for s in range(ATT_BB):
        kc = kc_ref[s]
        vc = vc_ref[s]
        k_new = jnp.concatenate(
            [_rms_head_pairs(kn_ref[s, :, tt * LANES:(tt + 1) * LANES], kn2, lo) for tt in range(B_KV_DIM // LANES)],
            axis=1)
        v_new = vn_ref[s]
        ko_ref[s, 0:B_WINDOW - DEC_SEQ, :] = kc[DEC_SEQ:, :]
        ko_ref[s, B_WINDOW - DEC_SEQ:, :] = k_new
        vo_ref[s, 0:B_WINDOW - DEC_SEQ, :] = vc[DEC_SEQ:, :]
        vo_ref[s, B_WINDOW - DEC_SEQ:, :] = v_new
        kall_ref[s, 0:B_WINDOW, :] = kc
        kall_ref[s, B_WINDOW:B_WINDOW + DEC_SEQ, :] = k_new
        kall_ref[s, B_WINDOW + DEC_SEQ:, :] = pad
        vall_ref[s, 0:B_WINDOW, :] = vc
        vall_ref[s, B_WINDOW:B_WINDOW + DEC_SEQ, :] = v_new
        vall_ref[s, B_WINDOW + DEC_SEQ:, :] = pad
        q = q_ref[s]
        qn = jnp.concatenate(
            [_rms_head_pairs(q[:, tt * LANES:(tt + 1) * LANES], qn2, lo) for tt in range(B_KV_DIM // LANES)], axis=1)
        q_all = jnp.where(own_group, jnp.concatenate([qn] * B_KV_HEADS, axis=0), 0.0).astype(bf16)
        logits = _dot_nt(q_all, kall_ref[s].astype(bf16)) * (B_HEAD_DIM ** -0.5)
        logits = jnp.where(valid, logits + bias, NEG_INF)
        p, denom = _softmax_with_sink(logits, sink)
        res = _dot(p.astype(bf16), vall_ref[s].astype(bf16)) / denom
        out = jnp.zeros((ATT_QR, B_KV_DIM), f32)
        for g in range(B_KV_HEADS):
            out = jnp.where(o_lane_group == g, res[g * ATT_QR:(g + 1) * ATT_QR, :], out)
        o_ref[s] = out


def attn_sample(q_s, k_new, v_new, k_cache, v_cache, table_s, q_norm, k_norm, sinks):
    blk = lambda *shape: pl.BlockSpec((ATT_BB,) + shape, lambda i: (i,) + (0,) * len(shape))
    full = lambda *shape: pl.BlockSpec(shape, lambda i: (0,) * len(shape))
    return pl.pallas_call(
        _attn_sample_kernel,
        grid=(DEC_BATCH // ATT_BB,),
        in_specs=[
            full(B_KV_HEADS * ATT_QR, 1),
            blk(ATT_QR, B_KV_DIM), blk(DEC_SEQ, B_KV_DIM), blk(DEC_SEQ, B_KV_DIM),
            blk(B_WINDOW, B_KV_DIM), blk(B_WINDOW, B_KV_DIM),
            full(B_KV_HEADS * ATT_QR, ATT_KEYS), full(1, LANES), full(1, LANES),
        ],
        out_specs=[blk(ATT_QR, B_KV_DIM), blk(B_WINDOW, B_KV_DIM), blk(B_WINDOW, B_KV_DIM)],
        out_shape=[
            jax.ShapeDtypeStruct((DEC_BATCH, ATT_QR, B_KV_DIM), f32),
            jax.ShapeDtypeStruct((DEC_BATCH, B_WINDOW, B_KV_DIM), f32),
            jax.ShapeDtypeStruct((DEC_BATCH, B_WINDOW, B_KV_DIM), f32),
        ],
        scratch_shapes=[pltpu.VMEM((ATT_BB, ATT_KEYS, B_KV_DIM), f32), pltpu.VMEM((ATT_BB, ATT_KEYS, B_KV_DIM), f32)],
        compiler_params=_params("parallel"),
        name="attn_sample",
    )(jnp.repeat(sinks, DEC_SEQ).reshape(B_KV_HEADS * ATT_QR, 1), q_s, k_new, v_new, k_cache, v_cache,
      table_s.reshape(B_KV_HEADS * ATT_QR, ATT_KEYS),
      jnp.tile(q_norm, (1, LANES // B_HEAD_DIM)), jnp.tile(k_norm, (1, LANES // B_HEAD_DIM)))


CONV_PAD = SUBLANES

def _gated_group_norm(y, z, norm_w):
    gt = y * _silu(z)
    parts = []
    for g in range(C_GROUPS):
        gg = gt[:, g * C_GROUP_W:(g + 1) * C_GROUP_W]
        parts.append(gg * lax.rsqrt(jnp.mean(gg * gg, axis=-1, keepdims=True) + EPS))
    return jnp.concatenate(parts, axis=1) * norm_w


def _expand_heads(v, sel3):
    lane = lax.broadcasted_iota(jnp.int32, (1, LANES), 1)
    v = jnp.where(lane < C_HEADS, v, 0.0)
    hi = v.astype(bf16).astype(f32)
    r1 = v - hi
    mid = r1.astype(bf16).astype(f32)
    lo = r1 - mid
    packed = hi + pltpu.roll(mid, C_HEADS, axis=1) + pltpu.roll(lo, 2 * C_HEADS, axis=1)
    return _dot(packed.astype(bf16), sel3)


SSD_SUB = 4


def _ssd_prompt_kernel(x_ref, g_ref, w_ref, wdt_ref, cw_ref, cb_ref, dtb_ref, alog_ref, dsk_ref, nw_ref, sel_ref,
                       yn_ref, hfin_ref, cout_ref, xn_ref, xpad_ref, ht_ref, y_ref):
    c = pl.program_id(1)
    n_zx = C_D_INNER + C_CONV_DIM

    @pl.when(c == 0)
    def _():
        xpad_ref[0, 0:CONV_PAD, :] = jnp.zeros((CONV_PAD, C_CONV_DIM), f32)
        ht_ref[...] = jnp.zeros(ht_ref.shape, f32)

    xn_ref[...] = _rms(x_ref[...], g_ref[...]).astype(bf16)
    for sub in range(SSD_SUB):
        rows = pl.ds(sub * C_CHUNK, C_CHUNK)
        xn = xn_ref[rows, :]
        zx = _dot(xn, w_ref[0, :, 0:n_zx])
        dtr = _dot(xn, wdt_ref[...])
        _ssd_chunk(zx[:, :C_D_INNER], zx[:, C_D_INNER:], dtr, cw_ref, cb_ref, dtb_ref, alog_ref, dsk_ref, nw_ref,
                   sel_ref, yn_ref.at[rows], cout_ref, xpad_ref.at[sub], xpad_ref.at[(sub + 1) % SSD_SUB], ht_ref,
                   y_ref.at[sub])

    @pl.when(c == pl.num_programs(1) - 1)
    def _():
        for t in range(C_D_INNER // LANES):
            hfin_ref[0, t * LANES:(t + 1) * LANES, :] = ht_ref[:, t * LANES:(t + 1) * LANES].T


def _ssd_chunk(z, xbc, dtr, cw_ref, cb_ref, dtb_ref, alog_ref, dsk_ref, nw_ref, sel_ref,
               yn_ref, cout_ref, xpad_ref, xpad_next_ref, ht_ref, y_ref):
    T = C_CHUNK
    xpad_ref[CONV_PAD:, :] = xbc
    xp = xpad_ref[...]
    cw = cw_ref[...]
    acc = cb_ref[...]
    for tap in range(C_D_CONV - 1):
        shifted = pltpu.roll(xp, C_D_CONV - 1 - tap, axis=0)[CONV_PAD:, :]
        acc = acc + shifted * cw[tap:tap + 1, :]
    acc = acc + xbc * cw[C_D_CONV - 1:C_D_CONV, :]
    xpad_next_ref[0:CONV_PAD, :] = xbc[T - CONV_PAD:, :]
    cout_ref[0] = xbc[T - (C_D_CONV - 1):, :]
    act = _silu(acc)
    xs = act[:, :C_D_INNER]
    bm = act[:, C_D_INNER:C_D_INNER + C_BC_DIM]
    cm = act[:, C_D_INNER + C_BC_DIM:]
    xb = xs.astype(bf16)

    dt = _softplus(dtr + dtb_ref[...])
    a_neg = -jnp.exp(alog_ref[...])
    row = lax.broadcasted_iota(jnp.int32, (T, T), 0)
    col = lax.broadcasted_iota(jnp.int32, (T, T), 1)
    causal = row >= col
    acs = _dot_exact_lhs01(causal.astype(f32), dt * a_neg)
    a2 = acs * LOG2E
    sel3 = sel_ref[...]
    e_exp = jnp.exp2(_expand_heads(a2, sel3))
    w_exp = _expand_heads(jnp.exp(acs[T - 1:T, :] - acs) * dt, sel3)
    cdec = e_exp[T - 1:T, :]
    b2_t = a2.T - jnp.log2(dt.T)
    xw = (xs * w_exp).astype(bf16)
    hb = ht_ref[...].astype(bf16)
    dsk = dsk_ref[...]
    lo_t = lax.broadcasted_iota(jnp.int32, (T, LANES), 1) < C_HEAD_DIM

    for g in range(C_GROUPS):
        ns = slice(g * C_D_STATE, (g + 1) * C_D_STATE)
        gs = slice(g * C_GROUP_W, (g + 1) * C_GROUP_W)
        b_g = bm[:, ns]
        c_g = cm[:, ns].astype(bf16)
        cb = _dot_nt(c_g, b_g.astype(bf16))
        yi = _dot(c_g, hb[:, gs])
        ht_ref[:, gs] = ht_ref[:, gs] * cdec[:, gs] + _dot(b_g.T.astype(bf16), xw[:, gs])
        for tt in range(C_GROUP_W // LANES):
            t = g * (C_GROUP_W // LANES) + tt
            lanes = slice(t * LANES, (t + 1) * LANES)
            xt = xb[:, lanes]
            res = []
            for half in range(2):
                h = 2 * t + half
                a_col = jnp.broadcast_to(a2[:, h:h + 1], (T, T))
                b_row = jnp.broadcast_to(b2_t[h:h + 1, :], (T, T))
                w = jnp.where(causal, cb * jnp.exp2(a_col - b_row), 0.0)
                res.append(_dot(w.astype(bf16), xt))
            y_intra = jnp.where(lo_t, res[0], res[1])
            y_ref[:, lanes] = y_intra + e_exp[:, lanes] * yi[:, tt * LANES:(tt + 1) * LANES] + dsk[:, lanes] * xs[:, lanes]

    yn_ref[...] = _gated_group_norm(y_ref[...], z, nw_ref[...]).astype(bf16)


def _head_select3():
    k = np.arange(LANES)[:, None]
    ch = np.arange(C_D_INNER)[None, :] // C_HEAD_DIM
    return jnp.asarray((k % C_HEADS == ch) & (k < 3 * C_HEADS), dtype=bf16)


def ssd_prompt(x, g, w_in, layer, w_dt, conv_w, conv_b, dt_bias, a_log, d_skip, norm_w):
    step = SSD_SUB * C_CHUNK
    nc = SEQ // step
    full = lambda *shape: pl.BlockSpec(shape, lambda b, c: (0,) * len(shape))
    once = lambda *shape: pl.BlockSpec(shape, lambda b, c: (0,) * len(shape), pipeline_mode=pl.Buffered(1))
    return pl.pallas_call(
        _ssd_prompt_kernel,
        grid=(BATCH, nc),
        in_specs=[
            pl.BlockSpec((step, D_MODEL), lambda b, c: (b * nc + c, 0)),
            full(1, D_MODEL),
            pl.BlockSpec((1,) + w_in.shape[1:], lambda b, c: (layer, 0, 0), pipeline_mode=pl.Buffered(1)),
            once(D_MODEL, LANES),
            full(C_D_CONV, C_CONV_DIM), full(1, C_CONV_DIM), full(1, LANES), full(1, LANES),
            full(1, C_D_INNER), full(1, C_D_INNER), full(LANES, C_D_INNER),
        ],
        out_specs=[
            pl.BlockSpec((step, C_D_INNER), lambda b, c: (b * nc + c, 0)),
            pl.BlockSpec((1, C_D_INNER, C_D_STATE), lambda b, c: (b, 0, 0)),
            pl.BlockSpec((1, C_D_CONV - 1, C_CONV_DIM), lambda b, c: (b, 0, 0)),
        ],
        out_shape=[
            jax.ShapeDtypeStruct((PROMPT_ROWS, C_D_INNER), bf16),
            jax.ShapeDtypeStruct((BATCH, C_D_INNER, C_D_STATE), f32),
            jax.ShapeDtypeStruct((BATCH, C_D_CONV - 1, C_CONV_DIM), f32),
        ],
        scratch_shapes=[
            pltpu.VMEM((step, D_MODEL), bf16),
            pltpu.VMEM((SSD_SUB, CONV_PAD + C_CHUNK, C_CONV_DIM), f32),
            pltpu.VMEM((C_D_STATE, C_D_INNER), f32),
            pltpu.VMEM((SSD_SUB, C_CHUNK, C_D_INNER), f32),
        ],
        compiler_params=_params("parallel", "arbitrary"),
        name="ssd_prompt",
    )(x, g, w_in, w_dt, conv_w, conv_b, dt_bias, a_log, d_skip, norm_w, _head_select3())


SSD_BB = 8
SSD_TP = SUBLANES
_N_PAIRS = DEC_SEQ * (DEC_SEQ + 1) // 2
_N_COEF = _N_PAIRS + 2 * DEC_SEQ


def _ssd_sample_kernel(zx_ref, dtr_ref, cs_ref, h0_ref, cw_ref, cb_ref, dtb_ref, alog_ref, dsk_ref, nw_ref,
                       sel_ref, yn_ref, hn_ref, cout_ref, c_scr, b_scr, xw_scr, yi_scr, cd_scr):
    L = DEC_SEQ
    cw = cw_ref[...]
    xp = [cs_ref[k] for k in range(C_D_CONV - 1)] + [zx_ref[t, :, C_D_INNER:] for t in range(L)]
    for k in range(C_D_CONV - 1):
        cout_ref[k] = xp[L + k]
    act = []
    for t in range(L):
        acc = cb_ref[...]
        for tap in range(C_D_CONV):
            acc = acc + xp[t + tap] * cw[tap:tap + 1, :]
        act.append(_silu(acc))
    xs = [a[:, :C_D_INNER] for a in act]
    bm = [a[:, C_D_INNER:C_D_INNER + C_BC_DIM] for a in act]
    cm = [a[:, C_D_INNER + C_BC_DIM:] for a in act]

    a_neg = -jnp.exp(alog_ref[...])
    dt = [_softplus(dtr_ref[t] + dtb_ref[...]) for t in range(L)]
    acs = []
    for t in range(L):
        acs.append(dt[t] * a_neg if t == 0 else acs[t - 1] + dt[t] * a_neg)

    lane_group = lax.broadcasted_iota(jnp.int32, (SSD_BB, LANES), 1) // C_REP
    coefs = []
    for t in range(L):
        for t2 in range(t + 1):
            cbh = jnp.zeros((SSD_BB, LANES), f32)
            for g in range(C_GROUPS):
                ns = slice(g * C_D_STATE, (g + 1) * C_D_STATE)
                cbg = jnp.sum(cm[t][:, ns] * bm[t2][:, ns], axis=-1, keepdims=True)
                cbh = jnp.where(lane_group == g, cbg, cbh)
            coefs.append(cbh * jnp.exp(acs[t] - acs[t2]) * dt[t2])
    for t in range(L):
        coefs.append(jnp.exp(acs[t]))
    for t in range(L):
        coefs.append(jnp.exp(acs[L - 1] - acs[t]) * dt[t])
    coef = jnp.concatenate(coefs, axis=0)
    cexp = _dot_exact_rhs01(coef, sel_ref[...])
    cexp = [cexp[k * SSD_BB:(k + 1) * SSD_BB, :] for k in range(_N_COEF)]
    w_intra = cexp[:_N_PAIRS]
    w_inter = cexp[_N_PAIRS:_N_PAIRS + L]
    w_state = cexp[_N_PAIRS + L:]

    cd = jnp.concatenate([jnp.exp(acs[L - 1]), jnp.zeros((LANES - SSD_BB, LANES), f32)], axis=0)
    cd_t = cd.T
    for s in range(SSD_BB):
        cd_scr[s] = jnp.broadcast_to(cd_t[0:C_HEADS, s:s + 1], (C_HEADS, C_D_STATE))

    zeros_tail = jnp.zeros((SSD_BB, SSD_TP - L, C_D_INNER), f32)
    c_scr[:, L:, :] = zeros_tail[:, :, :C_BC_DIM]
    b_scr[:, L:, :] = zeros_tail[:, :, :C_BC_DIM]
    xw_scr[:, L:, :] = zeros_tail
    for t in range(L):
        xw_t = xs[t] * w_state[t]
        for s in range(SSD_BB):
            c_scr[s, t:t + 1, :] = cm[t][s:s + 1, :]
            b_scr[s, t:t + 1, :] = bm[t][s:s + 1, :]
            xw_scr[s, t:t + 1, :] = xw_t[s:s + 1, :]

    for s in range(SSD_BB):
        for g in range(C_GROUPS):
            ns = slice(g * C_D_STATE, (g + 1) * C_D_STATE)
            gs = slice(g * C_GROUP_W, (g + 1) * C_GROUP_W)
            h0 = h0_ref[s, gs, :]
            yi = _dot_nt(c_scr[s, :, ns].astype(bf16), h0.astype(bf16))
            for t in range(L):
                yi_scr[t, s:s + 1, gs] = yi[t:t + 1, :]
            st = _dot_tn(xw_scr[s, :, gs].astype(bf16), b_scr[s, :, ns].astype(bf16))
            for r in range(C_REP):
                h = g * C_REP + r
                rs = slice(r * C_HEAD_DIM, (r + 1) * C_HEAD_DIM)
                scale = cd_scr[s, h:h + 1, :]
                hn_ref[s, h * C_HEAD_DIM:(h + 1) * C_HEAD_DIM, :] = h0[rs, :] * scale + st[rs, :]

    dsk = dsk_ref[...]
    nw = nw_ref[...]
    pair = 0
    for t in range(L):
        y = w_inter[t] * yi_scr[t] + dsk * xs[t]
        for t2 in range(t + 1):
            y = y + w_intra[pair] * xs[t2]
            pair += 1
        yn_ref[t] = _gated_group_norm(y, zx_ref[t, :, :C_D_INNER], nw).astype(bf16)


def ssd_sample(zx_t, dtr_t, conv_state_t, h0, conv_w, conv_b, dt_bias, a_log, d_skip, norm_w, sel):
    tmaj = lambda n, w: pl.BlockSpec((n, SSD_BB, w), lambda i: (0, i, 0))
    full = lambda *shape: pl.BlockSpec(shape, lambda i: (0,) * len(shape))
    return pl.pallas_call(
        _ssd_sample_kernel,
        grid=(DEC_BATCH // SSD_BB,),
        in_specs=[
            tmaj(DEC_SEQ, C_D_INNER + C_CONV_DIM), tmaj(DEC_SEQ, LANES), tmaj(C_D_CONV - 1, C_CONV_DIM),
            pl.BlockSpec((SSD_BB, C_D_INNER, C_D_STATE), lambda i: (i, 0, 0)),
            full(C_D_CONV, C_CONV_DIM), full(1, C_CONV_DIM), full(1, LANES), full(1, LANES),
            full(1, C_D_INNER), full(1, C_D_INNER), full(LANES, C_D_INNER),
        ],
        out_specs=[
            tmaj(DEC_SEQ, C_D_INNER),
            pl.BlockSpec((SSD_BB, C_D_INNER, C_D_STATE), lambda i: (i, 0, 0)),
            tmaj(C_D_CONV - 1, C_CONV_DIM),
        ],
        out_shape=[
            jax.ShapeDtypeStruct((DEC_SEQ, DEC_BATCH, C_D_INNER), bf16),
            jax.ShapeDtypeStruct((DEC_BATCH, C_D_INNER, C_D_STATE), f32),
            jax.ShapeDtypeStruct((C_D_CONV - 1, DEC_BATCH, C_CONV_DIM), f32),
        ],
        scratch_shapes=[
            pltpu.VMEM((SSD_BB, SSD_TP, C_BC_DIM), f32),
            pltpu.VMEM((SSD_BB, SSD_TP, C_BC_DIM), f32),
            pltpu.VMEM((SSD_BB, SSD_TP, C_D_INNER), f32),
            pltpu.VMEM((DEC_SEQ, SSD_BB, C_D_INNER), f32),
            pltpu.VMEM((SSD_BB, C_HEADS, C_D_STATE), f32),
        ],
        compiler_params=_params("parallel"),
        name="ssd_sample",
    )(zx_t, dtr_t, conv_state_t, h0, conv_w, conv_b, dt_bias, a_log, d_skip, norm_w, sel)


def _pad_lanes(v):
    return jnp.pad(v.astype(f32), (0, LANES - v.shape[0])).reshape(1, LANES)


def _mixer_a(xp, xs, g, j, w_in, norm_v, w_sp, b_sp, w_out):
    nv = norm_v.reshape(1, A_HALF)
    xs, v_s, w_in_b, w_out_b = mixer_a(xs, g, w_in, nv, w_sp[:, :DEC_SEQ, :DEC_SEQ].reshape(-1),
                                       b_sp[:, :DEC_SEQ].reshape(-1), w_out, j, tm=TM_SAMPLE, sample=True)
    xp = mixer_a(xp, g, w_in_b, nv, w_sp, b_sp.T, w_out_b, 0, tm=TM_PROMPT, sample=False)
    return xp, xs, jnp.swapaxes(v_s.reshape(DEC_SEQ, DEC_BATCH, A_HALF), 0, 1)


def _mixer_b(xp, xs, g, j, k_cache, v_cache, w_qkv, w_out, q_norm, k_norm, sinks, rel_bias):
    qn = q_norm.reshape(1, B_HEAD_DIM)
    kn = k_norm.reshape(1, B_HEAD_DIM)
    table = bias_table(rel_bias, sinks)
    n_qkv = B_Q_DIM + 2 * B_KV_DIM

    xp, k_p, v_p = attn_prompt(xp, g, w_qkv, w_out, j, table, qn, kn)

    qkv_s = norm_matmul(xs, g, w_qkv, j, n_qkv, tm=TM_SAMPLE, tn=QKV_TN_SAMPLE).reshape(DEC_SEQ, DEC_BATCH, -1)
    q_s = qkv_s[:, :, :B_Q_DIM].reshape(DEC_SEQ, DEC_BATCH, B_KV_HEADS, B_REP, B_HEAD_DIM)
    q_s = q_s.transpose(1, 3, 0, 2, 4).reshape(DEC_BATCH, ATT_QR, B_KV_DIM)
    k_new = jnp.swapaxes(qkv_s[:, :, B_Q_DIM:B_Q_DIM + B_KV_DIM], 0, 1)
    v_new = jnp.swapaxes(qkv_s[:, :, B_Q_DIM + B_KV_DIM:], 0, 1)
    table_s = table[:, :DEC_SEQ, :ATT_KEYS].reshape(B_KV_HEADS, ATT_QR, ATT_KEYS)
    o_s, k_s, v_s = attn_sample(q_s, k_new, v_new,
                                k_cache.reshape(DEC_BATCH, B_WINDOW, B_KV_DIM),
                                v_cache.reshape(DEC_BATCH, B_WINDOW, B_KV_DIM),
                                table_s, qn, kn, sinks)
    o_s = o_s.reshape(DEC_BATCH, B_REP, DEC_SEQ, B_KV_HEADS, B_HEAD_DIM).transpose(2, 0, 3, 1, 4)
    o_s = o_s.reshape(SAMPLE_ROWS, B_Q_DIM).astype(bf16)
    kv_shape_p = (BATCH, B_WINDOW, B_KV_HEADS, B_HEAD_DIM)
    kv_shape_s = (DEC_BATCH, B_WINDOW, B_KV_HEADS, B_HEAD_DIM)
    return xp, o_s, k_p.reshape(kv_shape_p), v_p.reshape(kv_shape_p), k_s.reshape(kv_shape_s), v_s.reshape(kv_shape_s)


def _mixer_c(xp, xs, g, j, h0, conv_state, w_in, conv_w, conv_b, dt_bias, a_log, d_skip, norm_w):
    n_zx = C_D_INNER + C_CONV_DIM
    w_dt = jnp.pad(w_in[j, :, n_zx:], ((0, 0), (0, LANES - C_HEADS)))
    cb = conv_b.reshape(1, C_CONV_DIM)
    dtb = _pad_lanes(dt_bias)
    alog = _pad_lanes(a_log)
    dsk = jnp.repeat(d_skip.astype(f32), C_HEAD_DIM).reshape(1, C_D_INNER)
    nw = norm_w.reshape(1, C_D_INNER)

    yn_p, h_p, conv_p = ssd_prompt(xp, g, w_in, j, w_dt, conv_w, cb, dtb, alog, dsk, nw)
    zx_s, dtr_s = norm_matmul(xs, g, w_in, j, n_zx, tm=TM_SAMPLE, tn=ZX_TN_SAMPLE, w_tail=w_dt)
    sel = (jnp.arange(LANES)[:, None] == jnp.arange(C_D_INNER)[None, :] // C_HEAD_DIM).astype(f32)
    yn_s, h_s, conv_s = ssd_sample(
        zx_s.reshape(DEC_SEQ, DEC_BATCH, -1), dtr_s.reshape(DEC_SEQ, DEC_BATCH, LANES),
        jnp.swapaxes(conv_state, 0, 1), h0.reshape(DEC_BATCH, C_D_INNER, C_D_STATE),
        conv_w, cb, dtb, alog, dsk, nw, sel)
    st_shape = (C_HEADS, C_HEAD_DIM, C_D_STATE)
    return (yn_p, yn_s.reshape(SAMPLE_ROWS, C_D_INNER), h_p.reshape((BATCH,) + st_shape), conv_p,
            h_s.reshape((DEC_BATCH,) + st_shape), jnp.swapaxes(conv_s, 0, 1))


def kernel(x_prompt, x_sample, cache_swa_k, cache_swa_v, state_ssm, state_conv, norm_mixer, norm_mlp, mlp_w_up, mlp_w_down, a_w_in, a_norm_v, a_w_spatial, a_b_spatial, a_w_out, b_w_qkv, b_q_norm, b_k_norm, b_sinks, rel_bias, b_w_out, c_w_in, c_conv_w, c_conv_b, c_dt_bias, c_a_log, c_d, c_norm, c_w_out):
    xp = x_prompt.reshape(PROMPT_ROWS, D_MODEL)
    xs = jnp.swapaxes(x_sample, 0, 1).reshape(SAMPLE_ROWS, D_MODEL)
    chunk_v_s = []
    swa_kp, swa_vp, swa_ks, swa_vs = [], [], [], []
    ssm_p, conv_p, ssm_s, conv_s = [], [], [], []
    b_w_qkv, b_w_out = b_w_qkv.astype(bf16), b_w_out.astype(bf16)
    c_w_in, c_w_out = c_w_in.astype(bf16), c_w_out.astype(bf16)
    for i in range(DEPTH):
        kind = i % N_MIXERS
        j = i // N_MIXERS
        g = norm_mixer[i].reshape(1, D_MODEL)
        proj_p = proj_s = None
        if kind == 0:
            xp, xs, v_new = _mixer_a(xp, xs, g, j, a_w_in, a_norm_v[j], a_w_spatial[j], a_b_spatial[j], a_w_out)
            chunk_v_s.append(v_new)
        elif kind == 1:
            xp, o_s, kp, vp, ks_, vs_ = _mixer_b(xp, xs, g, j, cache_swa_k[j], cache_swa_v[j], b_w_qkv, b_w_out,
                                                 b_q_norm[j], b_k_norm[j], b_sinks[j], rel_bias)
            proj_s = (o_s, b_w_out, j)
            swa_kp.append(kp); swa_vp.append(vp); swa_ks.append(ks_); swa_vs.append(vs_)
        else:
            y_p, y_s, hp, bp, hs, bs = _mixer_c(xp, xs, g, j, state_ssm[j], state_conv[j], c_w_in, c_conv_w[j],
                                                c_conv_b[j], c_dt_bias[j], c_a_log[j], c_d[j], c_norm[j])
            proj_p, proj_s = (y_p, c_w_out, j), (y_s, c_w_out, j)
            ssm_p.append(hp); conv_p.append(bp); ssm_s.append(hs); conv_s.append(bs)
        gm = norm_mlp[i].reshape(1, D_MODEL)
        xs, w_up_b, w_down_b = mlp(xs, gm, mlp_w_up, mlp_w_down, i, tm=TM_SAMPLE, tf=MLP_TF, proj=proj_s, emit=True)
        wide = proj_p is None or proj_p[0].shape[1] <= D_MODEL
        xp = mlp(xp, gm, w_up_b, w_down_b, 0, tm=TM_PROMPT, tf=2 * MLP_TF if wide else MLP_TF, proj=proj_p)
    y_prompt = xp.reshape(BATCH, SEQ, D_MODEL)
    y_sample = jnp.swapaxes(xs.reshape(DEC_SEQ, DEC_BATCH, D_MODEL), 0, 1)
    return (y_prompt, y_sample, jnp.stack(chunk_v_s),
            jnp.stack(swa_kp), jnp.stack(swa_vp), jnp.stack(swa_ks), jnp.stack(swa_vs),
            jnp.stack(ssm_p), jnp.stack(conv_p), jnp.stack(ssm_s), jnp.stack(conv_s))
```
